```python
import math
import jax, jax.numpy as jnp
from jax import lax
import numpy as np

D_MODEL = 1024
BATCH = 4
SEQ = 4096
DEPTH = 4

N_MIXERS = 2
N_HEADS = 16
HEAD_DIM = D_MODEL // N_HEADS
FOX_Q_BLOCK = 128
MOBA_BLOCK = 256
MOBA_TOP_K = 3
MOBA_Q_CHUNK = 32
REL_BUCKETS = 32
REL_MAX_DIST = 128
D_FF = ((8 * D_MODEL // 3 + 255) // 256) * 256
PLE_DIM = 256
RMS_EPS = 1e-6
N_FOX = (DEPTH + 1) // 2
N_MOBA = DEPTH // 2

kernel_name = "hybrid_fox_moba_t5bias_swiglu_ple"


def _rmsnorm(x, g):
    x32 = x.astype(jnp.float32)
    y = x32 * lax.rsqrt(jnp.mean(x32 * x32, axis=-1, keepdims=True) + RMS_EPS)
    return (y * g.astype(jnp.float32)).astype(x.dtype)


def _heads(t):
    b, s, _ = t.shape
    return t.reshape(b, s, N_HEADS, HEAD_DIM).transpose(0, 2, 1, 3)


def _merge_heads(o):
    b, h, s, d = o.shape
    return o.transpose(0, 2, 1, 3).reshape(b, s, h * d)


def _t5_bucket(rel):
    n = jnp.maximum(rel, 0)
    max_exact = REL_BUCKETS // 2
    nf = jnp.maximum(n, 1).astype(jnp.float32)
    large = max_exact + (jnp.log(nf / max_exact) / math.log(REL_MAX_DIST / max_exact)
                         * (REL_BUCKETS - max_exact)).astype(jnp.int32)
    large = jnp.minimum(large, REL_BUCKETS - 1)
    return jnp.where(n < max_exact, n, large)


def _fox_attention(q, k, v, log_f):
    b, h, s, d = q.shape
    nb = s // FOX_Q_BLOCK
    scale = HEAD_DIM ** -0.5
    c = jnp.cumsum(log_f, axis=-1)
    qb = q.reshape(b, h, nb, FOX_Q_BLOCK, d).transpose(2, 0, 1, 3, 4)
    cqb = c.reshape(b, h, nb, FOX_Q_BLOCK).transpose(2, 0, 1, 3)
    key_pos = jnp.arange(s)

    def block(args):
        qblk, cq, j = args
        t = j * FOX_Q_BLOCK + jnp.arange(FOX_Q_BLOCK)
        logits = jnp.einsum('bhqd,bhkd->bhqk', qblk, k).astype(jnp.float32) * scale
        logits = logits + cq[..., None] - c[:, :, None, :]
        logits = jnp.where(key_pos[None, :] <= t[:, None], logits, -jnp.inf)
        probs = jax.nn.softmax(logits, axis=-1).astype(v.dtype)
        return jnp.einsum('bhqk,bhkd->bhqd', probs, v)

    out = lax.map(block, (qb, cqb, jnp.arange(nb)))
    return out.transpose(1, 2, 0, 3, 4).reshape(b, h, s, d)


def _moba_attention(q, k, v, rel_table):
    b, h, s, d = q.shape
    scale = HEAD_DIM ** -0.5
    nblk = -(-s // MOBA_BLOCK)
    pad = nblk * MOBA_BLOCK - s
    kp = jnp.pad(k, ((0, 0), (0, 0), (0, pad), (0, 0)))
    vp = jnp.pad(v, ((0, 0), (0, 0), (0, pad), (0, 0)))
    kb = kp.reshape(b, h, nblk, MOBA_BLOCK, d)
    vb = vp.reshape(b, h, nblk, MOBA_BLOCK, d)
    kmean = jnp.mean(kb.astype(jnp.float32), axis=3).astype(q.dtype)
    n_sel = min(MOBA_TOP_K, nblk)
    n_chunks = s // MOBA_Q_CHUNK
    qc = q.reshape(b, h, n_chunks, MOBA_Q_CHUNK, d).transpose(2, 0, 1, 3, 4)
    table_ht = rel_table.T.astype(jnp.float32)
    bi = jnp.arange(b)[:, None, None, None]
    hi = jnp.arange(h)[None, :, None, None]
    hi5 = jnp.arange(h)[None, :, None, None, None]
    blk_ids = jnp.arange(nblk)
    in_blk = jnp.arange(MOBA_BLOCK)

    def chunk(args):
        qblk, ci = args
        t = ci * MOBA_Q_CHUNK + jnp.arange(MOBA_Q_CHUNK)
        own = (ci * MOBA_Q_CHUNK) // MOBA_BLOCK
        gate = jnp.einsum('bhqd,bhnd->bhqn', qblk, kmean).astype(jnp.float32)
        gate = jnp.where(blk_ids < own, gate, -jnp.inf)
        _, idx = lax.top_k(gate, n_sel)
        sel_valid = idx < own
        kg = kb[bi, hi, idx]
        vg = vb[bi, hi, idx]
        s_g = jnp.einsum('bhqd,bhqnkd->bhqnk', qblk, kg).astype(jnp.float32) * scale
        pos_g = idx[..., None] * MOBA_BLOCK + in_blk
        rel_g = t[None, None, :, None, None] - pos_g
        s_g = s_g + table_ht[hi5, _t5_bucket(rel_g)]
        s_g = jnp.where(sel_valid[..., None], s_g, -jnp.inf)
        ko = lax.dynamic_slice_in_dim(kp, own * MOBA_BLOCK, MOBA_BLOCK, axis=2)
        vo = lax.dynamic_slice_in_dim(vp, own * MOBA_BLOCK, MOBA_BLOCK, axis=2)
        s_o = jnp.einsum('bhqd,bhkd->bhqk', qblk, ko).astype(jnp.float32) * scale
        rel_o = t[:, None] - (own * MOBA_BLOCK + in_blk)[None, :]
        s_o = s_o + table_ht[:, _t5_bucket(rel_o)][None]
        s_o = jnp.where(rel_o >= 0, s_o, -jnp.inf)
        logits = jnp.concatenate(
            [s_g.reshape(b, h, MOBA_Q_CHUNK, n_sel * MOBA_BLOCK), s_o], axis=-1)
        probs = jax.nn.softmax(logits, axis=-1).astype(v.dtype)
        p_g = probs[..., :n_sel * MOBA_BLOCK].reshape(b, h, MOBA_Q_CHUNK, n_sel, MOBA_BLOCK)
        p_o = probs[..., n_sel * MOBA_BLOCK:]
        return (jnp.einsum('bhqnk,bhqnkd->bhqd', p_g, vg)
                + jnp.einsum('bhqk,bhkd->bhqd', p_o, vo))

    out = lax.map(chunk, (qc, jnp.arange(n_chunks)))
    return out.transpose(1, 2, 0, 3, 4).reshape(b, h, s, d)


def setup_inputs(seed: int = 0) -> dict:
    key = jax.random.key(seed)
    ks = jax.random.split(key, 20)
    f32 = jnp.float32
    D, H = D_MODEL, N_HEADS
    res_scale = (2.0 * DEPTH) ** -0.5

    def nrm(k, shape, scale):
        return jax.random.normal(k, shape, f32) * scale

    return {
        "x": nrm(ks[0], (BATCH, SEQ, D), 1.0),
        "p": nrm(ks[1], (DEPTH, BATCH, SEQ, PLE_DIM), 1.0),
        "attn_norm_g": 1.0 + nrm(ks[2], (DEPTH, D), 0.02),
        "fox_w_in": nrm(ks[3], (N_FOX, D, 3 * D + H), D ** -0.5),
        "fox_b_f": 2.0 + nrm(ks[4], (N_FOX, H), 0.5),
        "fox_w_o": nrm(ks[5], (N_FOX, D, D), D ** -0.5 * res_scale),
        "moba_w_in": nrm(ks[6], (N_MOBA, D, 3 * D), D ** -0.5),
        "moba_w_o": nrm(ks[7], (N_MOBA, D, D), D ** -0.5 * res_scale),
        "rel_bias_table": nrm(ks[8], (REL_BUCKETS, H), 0.5),
        "ffn_norm_g": 1.0 + nrm(ks[9], (DEPTH, D), 0.02),
        "ffn_w_in": nrm(ks[10], (DEPTH, D, 2 * D_FF), D ** -0.5),
        "ffn_w_out": nrm(ks[11], (DEPTH, D_FF, D), D_FF ** -0.5 * res_scale),
        "ple_norm_g": 1.0 + nrm(ks[12], (DEPTH, D), 0.02),
        "ple_w_gate": nrm(ks[13], (DEPTH, D, D), D ** -0.5),
        "ple_w_up": nrm(ks[14], (DEPTH, PLE_DIM, D), PLE_DIM ** -0.5 * res_scale),
        "final_norm_g": 1.0 + nrm(ks[15], (D,), 0.02),
    }


def reference(x, p, attn_norm_g, fox_w_in, fox_b_f, fox_w_o, moba_w_in, moba_w_o, rel_bias_table,
              ffn_norm_g, ffn_w_in, ffn_w_out, ple_norm_g, ple_w_gate, ple_w_up, final_norm_g):
    D = D_MODEL
    h = x
    for i in range(DEPTH):
        u = _rmsnorm(h, attn_norm_g[i])
        j = i // N_MIXERS
        if i % N_MIXERS == 0:
            proj = u @ fox_w_in[j]
            q, k, v = _heads(proj[..., :D]), _heads(proj[..., D:2 * D]), _heads(proj[..., 2 * D:3 * D])
            f_logit = proj[..., 3 * D:].astype(jnp.float32) + fox_b_f[j].astype(jnp.float32)
            log_f = jax.nn.log_sigmoid(f_logit).transpose(0, 2, 1)
            o = _fox_attention(q, k, v, log_f)
            h = h + _merge_heads(o) @ fox_w_o[j]
        else:
            proj = u @ moba_w_in[j]
            q, k, v = _heads(proj[..., :D]), _heads(proj[..., D:2 * D]), _heads(proj[..., 2 * D:])
            o = _moba_attention(q, k, v, rel_bias_table)
            h = h + _merge_heads(o) @ moba_w_o[j]
        u = _rmsnorm(h, ffn_norm_g[i])
        gu = u @ ffn_w_in[i]
        h = h + (jax.nn.silu(gu[..., :D_FF]) * gu[..., D_FF:]) @ ffn_w_out[i]
        gate = jax.nn.sigmoid(_rmsnorm(h, ple_norm_g[i]) @ ple_w_gate[i])
        h = h + gate * (p[i] @ ple_w_up[i])
    return _rmsnorm(h, final_norm_g)
```

```python
import functools
import math

import numpy as np
import jax
import jax.numpy as jnp
from jax import lax
from jax.experimental import pallas as pl
from jax.experimental.pallas import tpu as pltpu

F32 = jnp.float32
BF16 = jnp.bfloat16

RMS_EPS = 1e-6
HEAD_DIM = 64
MOBA_BLOCK = 256
MOBA_TOP_K = 3
REL_BUCKETS = 32
REL_MAX_DIST = 128

LANES = 128
ATTN_BLK = 256
CUMSUM_BLK = 256
VMEM_LIMIT_BYTES = 56 * 1024 * 1024
NEG_INF = float("-inf")

_NT = (((1,), (1,)), ((), ()))


def _params(n_axes):
    return pltpu.CompilerParams(
        dimension_semantics=("arbitrary",) * n_axes,
        vmem_limit_bytes=VMEM_LIMIT_BYTES)


def _split3(x):
    x1 = x.astype(BF16)
    r1 = x - x1.astype(F32)
    x2 = r1.astype(BF16)
    x3 = (r1 - x2.astype(F32)).astype(BF16)
    return x1, x2, x3


def _rmsnorm(x, g):
    ms = jnp.mean(x * x, axis=-1, keepdims=True)
    return x * lax.rsqrt(ms + RMS_EPS) * g


def _log_sigmoid(x):
    return jnp.minimum(x, 0.0) - jnp.log1p(jnp.exp(-jnp.abs(x)))


def _proj_kernel(x_ref, g_ref, w_ref, o_ref, u_ref):
    @pl.when(pl.program_id(1) == 0)
    def _():
        u_ref[...] = _rmsnorm(x_ref[...], g_ref[...]).astype(BF16)

    o_ref[...] = jnp.dot(u_ref[...], w_ref[...],
                         preferred_element_type=F32).astype(o_ref.dtype)


def _proj_gate_kernel(x_ref, g_ref, w_ref, wf_ref, bf_ref, o_ref, lf_ref, u_ref):
    @pl.when(pl.program_id(1) == 0)
    def _():
        u = _rmsnorm(x_ref[...], g_ref[...]).astype(BF16)
        u_ref[...] = u
        f_logit = jnp.dot(u, wf_ref[...], preferred_element_type=F32) + bf_ref[...]
        lf_ref[...] = _log_sigmoid(f_logit)

    o_ref[...] = jnp.dot(u_ref[...], w_ref[...],
                         preferred_element_type=F32).astype(o_ref.dtype)


def _project(h2, g, w, wf=None, bf=None, *, tm, tn):
    n, d = h2.shape
    nout = w.shape[1]
    grid = (n // tm, nout // tn)
    x_spec = pl.BlockSpec((tm, d), lambda i, j: (i, 0))
    g_spec = pl.BlockSpec((1, d), lambda i, j: (0, 0))
    w_spec = pl.BlockSpec((d, tn), lambda i, j: (0, j))
    o_spec = pl.BlockSpec((tm, tn), lambda i, j: (i, j))
    scratch = [pltpu.VMEM((tm, d), BF16)]
    if wf is None:
        return pl.pallas_call(
            _proj_kernel, grid=grid,
            in_specs=[x_spec, g_spec, w_spec], out_specs=o_spec,
            out_shape=jax.ShapeDtypeStruct((n, nout), BF16),
            scratch_shapes=scratch, compiler_params=_params(2),
            name="proj")(h2, g, w)
    return pl.pallas_call(
        _proj_gate_kernel, grid=grid,
        in_specs=[x_spec, g_spec, w_spec,
                  pl.BlockSpec((d, LANES), lambda i, j: (0, 0)),
                  pl.BlockSpec((1, LANES), lambda i, j: (0, 0))],
        out_specs=[o_spec, pl.BlockSpec((tm, LANES), lambda i, j: (i, 0))],
        out_shape=[jax.ShapeDtypeStruct((n, nout), BF16),
                   jax.ShapeDtypeStruct((n, LANES), F32)],
        scratch_shapes=scratch, compiler_params=_params(2),
        name="proj_gate")(h2, g, w, wf, bf)


def _cumsum_kernel(lf_ref, c_ref, carry_ref):
    @pl.when(pl.program_id(1) == 0)
    def _():
        carry_ref[...] = jnp.zeros_like(carry_ref)

    t = lf_ref.shape[1]
    row = lax.broadcasted_iota(jnp.int32, (t, t), 0)
    col = lax.broadcasted_iota(jnp.int32, (t, t), 1)
    tril = jnp.where(col <= row, 1.0, 0.0).astype(BF16)
    x1, x2, x3 = _split3(lf_ref[0])
    cs = (jnp.dot(tril, x1, preferred_element_type=F32)
          + jnp.dot(tril, x2, preferred_element_type=F32)
          + jnp.dot(tril, x3, preferred_element_type=F32))
    cs = cs + carry_ref[0:1, :]
    c_ref[0] = cs
    carry_ref[...] = jnp.broadcast_to(cs[t - 1:t, :], carry_ref.shape)


def _cumsum(lf3):
    b, s, _ = lf3.shape
    t = min(CUMSUM_BLK, s)
    spec = pl.BlockSpec((1, t, LANES), lambda i, j: (i, j, 0))
    return pl.pallas_call(
        _cumsum_kernel, grid=(b, s // t),
        in_specs=[spec], out_specs=spec,
        out_shape=jax.ShapeDtypeStruct(lf3.shape, F32),
        scratch_shapes=[pltpu.VMEM((8, LANES), F32)],
        compiler_params=_params(2), name="gate_cumsum")(lf3)


def _head_lane_mask(hh):
    lane = lax.broadcasted_iota(jnp.int32, (1, LANES), 1)
    return (lane < HEAD_DIM) if hh == 0 else (lane >= HEAD_DIM)


def _transpose_bf16(x):
    return x.astype(F32).T.astype(BF16)


def _flash_first(s_t, v_t, m_ref, l_ref, acc_ref, hh):
    m = jnp.max(s_t, axis=0, keepdims=True)
    p = jnp.exp(s_t - m)
    m_ref[hh] = m
    l_ref[hh] = jnp.sum(p, axis=0, keepdims=True)
    acc_ref[hh] = jnp.dot(v_t, p.astype(BF16), preferred_element_type=F32)


def _flash_update(s_t, v_t, m_ref, l_ref, acc_ref, hh, sel=None, const=0.0):
    m_old = m_ref[hh]
    mx = jnp.max(s_t, axis=0, keepdims=True) + const
    if sel is not None:
        mx = jnp.where(sel, mx, NEG_INF)
    m_new = jnp.maximum(m_old, mx)
    shift = m_new - const
    if sel is not None:
        shift = jnp.where(sel, shift, float("inf"))
    p = jnp.exp(s_t - shift)
    alpha = jnp.exp(m_old - m_new)
    m_ref[hh] = m_new
    l_ref[hh] = alpha * l_ref[hh] + jnp.sum(p, axis=0, keepdims=True)
    acc_ref[hh] = alpha * acc_ref[hh] + jnp.dot(
        v_t, p.astype(BF16), preferred_element_type=F32)


def _write_heads(o_ref, l_ref, acc_ref):
    o0 = acc_ref[0] / l_ref[0]
    o1 = acc_ref[1] / l_ref[1]
    row = lax.broadcasted_iota(jnp.int32, o0.shape, 0)
    o_t = jnp.where(row < HEAD_DIM, o0, o1)
    o_ref[0] = o_t.T.astype(o_ref.dtype)


def _fox_kernel(q_ref, k_ref, v_ref, cq_ref, ck_ref, o_ref,
                kaug_ref, vt_ref, m_ref, l_ref, acc_ref):
    hp = pl.program_id(1)
    qi = pl.program_id(2)
    blk = q_ref.shape[1]
    nblk = k_ref.shape[1] // blk
    scale = HEAD_DIM ** -0.5
    lane = lax.broadcasted_iota(jnp.int32, (1, LANES), 1)
    prow = lax.broadcasted_iota(jnp.int32, (LANES, LANES), 0)
    pcol = lax.broadcasted_iota(jnp.int32, (LANES, LANES), 1)

    def place(parts, src_lane, first_dst_lane):
        out = None
        for i, part in enumerate(parts):
            sel = jnp.where((prow == src_lane) & (pcol == first_dst_lane + i),
                            1.0, 0.0).astype(BF16)
            term = jnp.dot(part, sel, preferred_element_type=F32)
            out = term if out is None else out + term
        return out

    @pl.when(qi == 0)
    def _():
        def build(jb, carry):
            rows = pl.ds(pl.multiple_of(jb * blk, blk), blk)
            k2 = k_ref[0, rows, :]
            parts = _split3(ck_ref[0, rows, :])
            for hh in range(2):
                base = HEAD_DIM * (1 - hh)
                aug = -place(parts, 2 * hp + hh, base + 3)
                aug = aug + jnp.where((lane >= base) & (lane < base + 3), 1.0, 0.0)
                kaug_ref[hh, rows, :] = jnp.where(_head_lane_mask(hh), k2, aug.astype(BF16))
            vt_ref[jb] = _transpose_bf16(v_ref[0, rows, :])
            return carry
        lax.fori_loop(0, nblk, build, 0)

    q2 = (q_ref[0].astype(F32) * scale).astype(BF16)
    q_parts = _split3(cq_ref[0])
    krow = lax.broadcasted_iota(jnp.int32, (blk, blk), 0)
    qcol = lax.broadcasted_iota(jnp.int32, (blk, blk), 1)
    for hh in range(2):
        base = HEAD_DIM * (1 - hh)
        aug = place(q_parts, 2 * hp + hh, base)
        aug = aug + jnp.where((lane >= base + 3) & (lane < base + 6), 1.0, 0.0)
        q_aug = jnp.where(_head_lane_mask(hh), q2, aug.astype(BF16))

        def scores(j):
            rows = pl.ds(pl.multiple_of(j * blk, blk), blk)
            return lax.dot_general(kaug_ref[hh, rows, :], q_aug, _NT,
                                   preferred_element_type=F32)

        s_t = jnp.where(krow <= qcol, scores(qi), NEG_INF)
        _flash_first(s_t, vt_ref[qi], m_ref, l_ref, acc_ref, hh)

        def body(j, carry):
            _flash_update(scores(j), vt_ref[j], m_ref, l_ref, acc_ref, hh)
            return carry
        lax.fori_loop(0, qi, body, 0)

    _write_heads(o_ref, l_ref, acc_ref)


def _fox_attention(qkv3, c3, n_heads):
    b, s, d3 = qkv3.shape
    d = d3 // 3
    blk = min(ATTN_BLK, s)
    npair = n_heads // 2
    tile = pl.BlockSpec((1, blk, LANES), lambda bi, hp, qi: (bi, qi, hp))
    return pl.pallas_call(
        _fox_kernel, grid=(b, npair, s // blk),
        in_specs=[
            tile,
            pl.BlockSpec((1, s, LANES), lambda bi, hp, qi: (bi, 0, npair + hp)),
            pl.BlockSpec((1, s, LANES), lambda bi, hp, qi: (bi, 0, 2 * npair + hp)),
            pl.BlockSpec((1, blk, LANES), lambda bi, hp, qi: (bi, qi, 0)),
            pl.BlockSpec((1, s, LANES), lambda bi, hp, qi: (bi, 0, 0)),
        ],
        out_specs=tile,
        out_shape=jax.ShapeDtypeStruct((b, s, d), BF16),
        scratch_shapes=[
            pltpu.VMEM((2, s, LANES), BF16),
            pltpu.VMEM((s // blk, LANES, blk), BF16),
            pltpu.VMEM((2, 1, blk), F32),
            pltpu.VMEM((2, 1, blk), F32),
            pltpu.VMEM((2, LANES, blk), F32),
        ],
        compiler_params=_params(3), name="fox_attention")(qkv3, qkv3, qkv3, c3, c3)


def _t5_bucket_np(n):
    max_exact = REL_BUCKETS // 2
    nf = np.maximum(n, 1).astype(np.float64)
    large = max_exact + (np.log(nf / max_exact) / math.log(REL_MAX_DIST / max_exact)
                         * (REL_BUCKETS - max_exact)).astype(np.int32)
    return np.where(n < max_exact, n, np.minimum(large, REL_BUCKETS - 1)).astype(np.int32)


def _bucket_tiles(blk):
    key = np.arange(blk)[:, None]
    qry = np.arange(blk)[None, :]
    own = np.where(key <= qry, _t5_bucket_np(np.maximum(qry - key, 0)), -1)
    prev = _t5_bucket_np(blk + qry - key)
    return np.stack([own, prev]).astype(np.int32)


def _bias_kernel(tab_ref, bucket_ref, o_ref):
    h = pl.program_id(0)
    bucket = bucket_ref[...]
    acc = jnp.where(bucket < 0, NEG_INF, 0.0).astype(F32)
    for bkt in range(REL_BUCKETS):
        acc = jnp.where(bucket == bkt, tab_ref[bkt, h], acc)
    o_ref[0] = acc


def _bias_tiles(rel_table, blk):
    n_heads = rel_table.shape[1]
    buckets = jnp.asarray(_bucket_tiles(blk))
    return pl.pallas_call(
        _bias_kernel, grid=(n_heads,),
        in_specs=[pl.BlockSpec(memory_space=pltpu.SMEM),
                  pl.BlockSpec((2, blk, blk), lambda h: (0, 0, 0))],
        out_specs=pl.BlockSpec((1, 2, blk, blk), lambda h: (h, 0, 0, 0)),
        out_shape=jax.ShapeDtypeStruct((n_heads, 2, blk, blk), F32),
        compiler_params=_params(1), name="t5_bias_tiles")(rel_table, buckets)


def _moba_kernel(tab_ref, q_ref, k_ref, v_ref, bias_ref, o_ref,
                 vt_ref, km_ref, sel_ref, m_ref, l_ref, acc_ref):
    hp = pl.program_id(1)
    qi = pl.program_id(2)
    blk = q_ref.shape[1]
    nblk = k_ref.shape[1] // blk
    scale = HEAD_DIM ** -0.5
    assert blk >= REL_MAX_DIST

    @pl.when(qi == 0)
    def _():
        def build(jb, carry):
            rows = pl.ds(pl.multiple_of(jb * blk, blk), blk)
            km_ref[pl.ds(jb, 1), :] = jnp.mean(k_ref[0, rows, :].astype(F32),
                                                axis=0, keepdims=True)
            vt_ref[jb] = _transpose_bf16(v_ref[0, rows, :])
            return carry
        lax.fori_loop(0, nblk, build, 0)

    q2 = q_ref[0]
    km_parts = _split3(km_ref[...])
    blk_id = lax.broadcasted_iota(jnp.int32, (nblk, blk), 0)
    for hh in range(2):
        head = _head_lane_mask(hh)
        q_gate = jnp.where(head, q2, jnp.zeros_like(q2))
        q_m = (q_gate.astype(F32) * scale).astype(BF16)

        gate = None
        for part in km_parts:
            term = lax.dot_general(part, q_gate, _NT, preferred_element_type=F32)
            gate = term if gate is None else gate + term
        gate = jnp.where(blk_id < qi, gate, NEG_INF)
        rank = jnp.zeros(gate.shape, jnp.int32)
        for jp in range(nblk):
            other = gate[jp:jp + 1, :]
            beats = (other > gate) | ((other == gate) & (jp < blk_id))
            rank = rank + jnp.where(beats, 1, 0)
        sel_ref[hh] = jnp.where((rank < MOBA_TOP_K) & (blk_id < qi), 1.0, 0.0)

        def scores(j):
            rows = pl.ds(pl.multiple_of(j * blk, blk), blk)
            return lax.dot_general(k_ref[0, rows, :], q_m, _NT,
                                   preferred_element_type=F32)

        def keep(j):
            return sel_ref[hh, pl.ds(j, 1), :] > 0.5

        _flash_first(scores(qi) + bias_ref[hh, 0], vt_ref[qi], m_ref, l_ref, acc_ref, hh)

        @pl.when(qi >= 1)
        def _():
            j = qi - 1
            _flash_update(scores(j) + bias_ref[hh, 1], vt_ref[j],
                          m_ref, l_ref, acc_ref, hh, sel=keep(j))

        far_bias = tab_ref[REL_BUCKETS - 1, 2 * hp + hh]

        def body(j, carry):
            _flash_update(scores(j), vt_ref[j], m_ref, l_ref, acc_ref, hh,
                          sel=keep(j), const=far_bias)
            return carry
        lax.fori_loop(0, jnp.maximum(qi - 1, 0), body, 0)

    _write_heads(o_ref, l_ref, acc_ref)


def _moba_attention(qkv3, rel_table, bias_t, n_heads):
    b, s, d3 = qkv3.shape
    d = d3 // 3
    blk = MOBA_BLOCK
    assert s % blk == 0
    npair = n_heads // 2
    nblk = s // blk
    tile = pl.BlockSpec((1, blk, LANES), lambda bi, hp, qi: (bi, qi, hp))
    return pl.pallas_call(
        _moba_kernel, grid=(b, npair, nblk),
        in_specs=[
            pl.BlockSpec(memory_space=pltpu.SMEM),
            tile,
            pl.BlockSpec((1, s, LANES), lambda bi, hp, qi: (bi, 0, npair + hp)),
            pl.BlockSpec((1, s, LANES), lambda bi, hp, qi: (bi, 0, 2 * npair + hp)),
            pl.BlockSpec((2, 2, blk, blk), lambda bi, hp, qi: (hp, 0, 0, 0)),
        ],
        out_specs=tile,
        out_shape=jax.ShapeDtypeStruct((b, s, d), BF16),
        scratch_shapes=[
            pltpu.VMEM((nblk, LANES, blk), BF16),
            pltpu.VMEM((nblk, LANES), F32),
            pltpu.VMEM((2, nblk, blk), F32),
            pltpu.VMEM((2, 1, blk), F32),
            pltpu.VMEM((2, 1, blk), F32),
            pltpu.VMEM((2, LANES, blk), F32),
        ],
        compiler_params=_params(3), name="moba_attention")(
            rel_table, qkv3, qkv3, qkv3, bias_t)


def _oproj_kernel(o_ref, w_ref, h_ref, out_ref):
    out_ref[...] = h_ref[...] + jnp.dot(o_ref[...], w_ref[...],
                                        preferred_element_type=F32)


def _out_project(o2, w, h2, *, tm, tn):
    n, d = h2.shape
    return pl.pallas_call(
        _oproj_kernel, grid=(n // tm, d // tn),
        in_specs=[pl.BlockSpec((tm, o2.shape[1]), lambda i, j: (i, 0)),
                  pl.BlockSpec((o2.shape[1], tn), lambda i, j: (0, j)),
                  pl.BlockSpec((tm, tn), lambda i, j: (i, j))],
        out_specs=pl.BlockSpec((tm, tn), lambda i, j: (i, j)),
        out_shape=jax.ShapeDtypeStruct((n, d), F32),
        compiler_params=_params(2), name="out_proj")(o2, w, h2)


def _ffn_kernel(x_ref, g_ref, wg_ref, wu_ref, wo_ref, out_ref, u_ref, acc_ref):
    c = pl.program_id(1)

    @pl.when(c == 0)
    def _():
        u_ref[...] = _rmsnorm(x_ref[...], g_ref[...]).astype(BF16)
        acc_ref[...] = jnp.zeros_like(acc_ref)

    u = u_ref[...]
    gate = jnp.dot(u, wg_ref[...], preferred_element_type=F32)
    up = jnp.dot(u, wu_ref[...], preferred_element_type=F32)
    act = (gate * jax.nn.sigmoid(gate) * up).astype(BF16)
    acc_ref[...] += jnp.dot(act, wo_ref[...], preferred_element_type=F32)

    @pl.when(c == pl.num_programs(1) - 1)
    def _():
        out_ref[...] = x_ref[...] + acc_ref[...]


def _ffn(h2, g, w_in, w_out, *, tm, tf):
    n, d = h2.shape
    d_ff = w_out.shape[0]
    nchunk = d_ff // tf
    return pl.pallas_call(
        _ffn_kernel, grid=(n // tm, nchunk),
        in_specs=[pl.BlockSpec((tm, d), lambda i, c: (i, 0)),
                  pl.BlockSpec((1, d), lambda i, c: (0, 0)),
                  pl.BlockSpec((d, tf), lambda i, c: (0, c)),
                  pl.BlockSpec((d, tf), lambda i, c: (0, nchunk + c)),
                  pl.BlockSpec((tf, d), lambda i, c: (c, 0))],
        out_specs=pl.BlockSpec((tm, d), lambda i, c: (i, 0)),
        out_shape=jax.ShapeDtypeStruct((n, d), F32),
        scratch_shapes=[pltpu.VMEM((tm, d), BF16), pltpu.VMEM((tm, d), F32)],
        compiler_params=_params(2), name="swiglu_ffn")(h2, g, w_in, w_in, w_out)


def _ple_kernel(x_ref, g_ref, wg_ref, p_ref, wu_ref, fg_ref, out_ref, *, final_norm):
    x = x_ref[...]
    u = _rmsnorm(x, g_ref[...]).astype(BF16)
    gate = jax.nn.sigmoid(jnp.dot(u, wg_ref[...], preferred_element_type=F32))
    up = jnp.dot(p_ref[...].astype(BF16), wu_ref[...], preferred_element_type=F32)
    y = x + gate * up
    if final_norm:
        y = _rmsnorm(y, fg_ref[...])
    out_ref[...] = y


def _ple(h2, g, w_gate, p2, w_up, final_g, *, tm, final_norm):
    n, d = h2.shape
    pd = p2.shape[1]
    return pl.pallas_call(
        functools.partial(_ple_kernel, final_norm=final_norm), grid=(n // tm,),
        in_specs=[pl.BlockSpec((tm, d), lambda i: (i, 0)),
                  pl.BlockSpec((1, d), lambda i: (0, 0)),
                  pl.BlockSpec((d, d), lambda i: (0, 0)),
                  pl.BlockSpec((tm, pd), lambda i: (i, 0)),
                  pl.BlockSpec((pd, d), lambda i: (0, 0)),
                  pl.BlockSpec((1, d), lambda i: (0, 0))],
        out_specs=pl.BlockSpec((tm, d), lambda i: (i, 0)),
        out_shape=jax.ShapeDtypeStruct((n, d), F32),
        compiler_params=_params(1), name="ple")(h2, g, w_gate, p2, w_up, final_g)


def _row_tile(n, want):
    t = min(want, n)
    assert n % t == 0
    return t


def _col_tile(n, want):
    t = min(want, n)
    while n % t:
        t -= LANES
    return t


def kernel(x, p, attn_norm_g, fox_w_in, fox_b_f, fox_w_o, moba_w_in, moba_w_o, rel_bias_table,
           ffn_norm_g, ffn_w_in, ffn_w_out, ple_norm_g, ple_w_gate, ple_w_up, final_norm_g):
    b, s, d = x.shape
    depth = p.shape[0]
    n_heads = rel_bias_table.shape[1]
    assert d == n_heads * HEAD_DIM and n_heads % 2 == 0 and n_heads <= LANES
    n = b * s
    tm = _row_tile(n, 1024)
    tm_ple = _row_tile(n, 512)
    tn = _col_tile(d, 512)
    tf = _col_tile(ffn_w_out.shape[1], 256)

    bias_t = _bias_tiles(rel_bias_table.astype(F32), MOBA_BLOCK)
    final_g = final_norm_g.reshape(1, d).astype(F32)

    h = x.reshape(n, d).astype(F32)
    for i in range(depth):
        j = i // 2
        g_attn = attn_norm_g[i].reshape(1, d).astype(F32)
        if i % 2 == 0:
            w = fox_w_in[j]
            w_qkv = w[:, :3 * d].astype(BF16)
            w_f = jnp.pad(w[:, 3 * d:], ((0, 0), (0, LANES - n_heads))).astype(BF16)
            b_f = jnp.pad(fox_b_f[j].astype(F32), (0, LANES - n_heads)).reshape(1, LANES)
            qkv, log_f = _project(h, g_attn, w_qkv, w_f, b_f, tm=tm, tn=_col_tile(3 * d, 512))
            c = _cumsum(log_f.reshape(b, s, LANES))
            o = _fox_attention(qkv.reshape(b, s, 3 * d), c, n_heads)
            w_o = fox_w_o[j]
        else:
            qkv = _project(h, g_attn, moba_w_in[j].astype(BF16), tm=tm, tn=_col_tile(3 * d, 512))
            o = _moba_attention(qkv.reshape(b, s, 3 * d), rel_bias_table.astype(F32),
                                bias_t, n_heads)
            w_o = moba_w_o[j]
        h = _out_project(o.reshape(n, d), w_o.astype(BF16), h, tm=tm, tn=tn)
        h = _ffn(h, ffn_norm_g[i].reshape(1, d).astype(F32), ffn_w_in[i].astype(BF16),
                 ffn_w_out[i].astype(BF16), tm=tm, tf=tf)
        h = _ple(h, ple_norm_g[i].reshape(1, d).astype(F32), ple_w_gate[i].astype(BF16),
                 p[i].reshape(n, -1), ple_w_up[i].astype(BF16), final_g,
                 tm=tm_ple, final_norm=(i == depth - 1))
    return h.reshape(b, s, d).astype(x.dtype)
```

```python
import functools
import math

import numpy as np
import jax
import jax.numpy as jnp
from jax import lax
from jax.experimental import pallas as pl
from jax.experimental.pallas import tpu as pltpu

F32 = jnp.float32
BF16 = jnp.bfloat16

RMS_EPS = 1e-6
HEAD_DIM = 64
MOBA_BLOCK = 256
MOBA_TOP_K = 3
REL_BUCKETS = 32
REL_MAX_DIST = 128

LANES = 128
FOX_BLK = 512
MOBA_Q_TILE = 2 * MOBA_BLOCK
CUMSUM_BLK = 256
VMEM_LIMIT_BYTES = 56 * 1024 * 1024
NEG_INF = float("-inf")

_NT = (((1,), (1,)), ((), ()))


def _params(n_axes):
    return pltpu.CompilerParams(
        dimension_semantics=("arbitrary",) * n_axes,
        vmem_limit_bytes=VMEM_LIMIT_BYTES)


def _split3(x):
    x1 = x.astype(BF16)
    r1 = x - x1.astype(F32)
    x2 = r1.astype(BF16)
    x3 = (r1 - x2.astype(F32)).astype(BF16)
    return x1, x2, x3


def _rmsnorm(x, g):
    ms = jnp.mean(x * x, axis=-1, keepdims=True)
    return x * lax.rsqrt(ms + RMS_EPS) * g


def _log_sigmoid(x):
    return jnp.minimum(x, 0.0) - jnp.log1p(jnp.exp(-jnp.abs(x)))


def _proj_kernel(x_ref, g_ref, w_ref, o_ref, u_ref):
    @pl.when(pl.program_id(1) == 0)
    def _():
        u_ref[...] = _rmsnorm(x_ref[...], g_ref[...]).astype(BF16)

    o_ref[...] = jnp.dot(u_ref[...], w_ref[...],
                         preferred_element_type=F32).astype(o_ref.dtype)


def _proj_gate_kernel(x_ref, g_ref, w_ref, wf_ref, bf_ref, o_ref, lf_ref, u_ref):
    @pl.when(pl.program_id(1) == 0)
    def _():
        u = _rmsnorm(x_ref[...], g_ref[...]).astype(BF16)
        u_ref[...] = u
        f_logit = jnp.dot(u, wf_ref[...], preferred_element_type=F32) + bf_ref[...]
        lf_ref[...] = _log_sigmoid(f_logit)

    o_ref[...] = jnp.dot(u_ref[...], w_ref[...],
                         preferred_element_type=F32).astype(o_ref.dtype)


def _project(h2, g, w, wf=None, bf=None, *, tm, tn):
    n, d = h2.shape
    nout = w.shape[1]
    grid = (n // tm, nout // tn)
    x_spec = pl.BlockSpec((tm, d), lambda i, j: (i, 0))
    g_spec = pl.BlockSpec((1, d), lambda i, j: (0, 0))
    w_spec = pl.BlockSpec((d, tn), lambda i, j: (0, j))
    o_spec = pl.BlockSpec((tm, tn), lambda i, j: (i, j))
    scratch = [pltpu.VMEM((tm, d), BF16)]
    if wf is None:
        return pl.pallas_call(
            _proj_kernel, grid=grid,
            in_specs=[x_spec, g_spec, w_spec], out_specs=o_spec,
            out_shape=jax.ShapeDtypeStruct((n, nout), BF16),
            scratch_shapes=scratch, compiler_params=_params(2),
            name="proj")(h2, g, w)
    return pl.pallas_call(
        _proj_gate_kernel, grid=grid,
        in_specs=[x_spec, g_spec, w_spec,
                  pl.BlockSpec((d, LANES), lambda i, j: (0, 0)),
                  pl.BlockSpec((1, LANES), lambda i, j: (0, 0))],
        out_specs=[o_spec, pl.BlockSpec((tm, LANES), lambda i, j: (i, 0))],
        out_shape=[jax.ShapeDtypeStruct((n, nout), BF16),
                   jax.ShapeDtypeStruct((n, LANES), F32)],
        scratch_shapes=scratch, compiler_params=_params(2),
        name="proj_gate")(h2, g, w, wf, bf)


def _cumsum_kernel(lf_ref, c_ref, carry_ref):
    @pl.when(pl.program_id(1) == 0)
    def _():
        carry_ref[...] = jnp.zeros_like(carry_ref)

    t = lf_ref.shape[1]
    row = lax.broadcasted_iota(jnp.int32, (t, t), 0)
    col = lax.broadcasted_iota(jnp.int32, (t, t), 1)
    tril = jnp.where(col <= row, 1.0, 0.0).astype(BF16)
    x1, x2, x3 = _split3(lf_ref[0])
    cs = (jnp.dot(tril, x1, preferred_element_type=F32)
          + jnp.dot(tril, x2, preferred_element_type=F32)
          + jnp.dot(tril, x3, preferred_element_type=F32))
    cs = cs + carry_ref[0:1, :]
    c_ref[0] = cs
    carry_ref[...] = jnp.broadcast_to(cs[t - 1:t, :], carry_ref.shape)


def _cumsum(lf3):
    b, s, _ = lf3.shape
    t = min(CUMSUM_BLK, s)
    spec = pl.BlockSpec((1, t, LANES), lambda i, j: (i, j, 0))
    return pl.pallas_call(
        _cumsum_kernel, grid=(b, s // t),
        in_specs=[spec], out_specs=spec,
        out_shape=jax.ShapeDtypeStruct(lf3.shape, F32),
        scratch_shapes=[pltpu.VMEM((8, LANES), F32)],
        compiler_params=_params(2), name="gate_cumsum")(lf3)


def _head_lane_mask(hh):
    lane = lax.broadcasted_iota(jnp.int32, (1, LANES), 1)
    return (lane < HEAD_DIM) if hh == 0 else (lane >= HEAD_DIM)


def _transpose_bf16(x):
    return x.astype(F32).T.astype(BF16)


def _flash_init(m_ref, l_ref, acc_ref):
    m_ref[...] = jnp.full(m_ref.shape, NEG_INF, F32)
    l_ref[...] = jnp.zeros(l_ref.shape, F32)
    acc_ref[...] = jnp.zeros(acc_ref.shape, F32)


def _flash_update(s_t, v_t, m_ref, l_ref, acc_ref, hh, keep=None, const=None):
    m_old = m_ref[hh]
    mx = jnp.max(s_t, axis=0, keepdims=True)
    if const is not None:
        mx = mx + const
    if keep is not None:
        mx = jnp.where(keep, mx, NEG_INF)
    m_new = jnp.maximum(m_old, mx)
    m_safe = jnp.where(m_new == NEG_INF, 0.0, m_new)
    shift = m_safe if const is None else m_safe - const
    if keep is not None:
        shift = jnp.where(keep, shift, float("inf"))
    p = jnp.exp(s_t - shift)
    alpha = jnp.exp(m_old - m_safe)
    m_ref[hh] = m_new
    l_ref[hh] = alpha * l_ref[hh] + jnp.sum(p, axis=0, keepdims=True)
    acc_ref[hh] = alpha * acc_ref[hh] + jnp.dot(
        v_t, p.astype(BF16), preferred_element_type=F32)


def _write_heads(o_ref, l_ref, acc_ref):
    o0 = acc_ref[0] / l_ref[0]
    o1 = acc_ref[1] / l_ref[1]
    row = lax.broadcasted_iota(jnp.int32, o0.shape, 0)
    o_t = jnp.where(row < HEAD_DIM, o0, o1)
    o_ref[0] = o_t.T.astype(o_ref.dtype)


def _fox_kernel(q_ref, k_ref, v_ref, cq_ref, ck_ref, o_ref,
                kaug_ref, vt_ref, m_ref, l_ref, acc_ref):
    hp = pl.program_id(1)
    qi = pl.program_id(2)
    blk = q_ref.shape[1]
    nblk = k_ref.shape[1] // blk
    scale = HEAD_DIM ** -0.5
    lane = lax.broadcasted_iota(jnp.int32, (1, LANES), 1)
    prow = lax.broadcasted_iota(jnp.int32, (LANES, LANES), 0)
    pcol = lax.broadcasted_iota(jnp.int32, (LANES, LANES), 1)

    def place(parts, src_lane, first_dst_lane):
        out = None
        for i, part in enumerate(parts):
            sel = jnp.where((prow == src_lane) & (pcol == first_dst_lane + i),
                            1.0, 0.0).astype(BF16)
            term = jnp.dot(part, sel, preferred_element_type=F32)
            out = term if out is None else out + term
        return out

    @pl.when(qi == 0)
    def _():
        def build(jb, carry):
            rows = pl.ds(pl.multiple_of(jb * blk, blk), blk)
            k2 = k_ref[0, rows, :]
            parts = _split3(ck_ref[0, rows, :])
            for hh in range(2):
                base = HEAD_DIM * (1 - hh)
                aug = -place(parts, 2 * hp + hh, base + 3)
                aug = aug + jnp.where((lane >= base) & (lane < base + 3), 1.0, 0.0)
                kaug_ref[hh, rows, :] = jnp.where(_head_lane_mask(hh), k2, aug.astype(BF16))
            vt_ref[jb] = _transpose_bf16(v_ref[0, rows, :])
            return carry
        lax.fori_loop(0, nblk, build, 0)

    q2 = (q_ref[0].astype(F32) * scale).astype(BF16)
    q_parts = _split3(cq_ref[0])
    krow = lax.broadcasted_iota(jnp.int32, (blk, blk), 0)
    qcol = lax.broadcasted_iota(jnp.int32, (blk, blk), 1)
    q_aug = []
    for hh in range(2):
        base = HEAD_DIM * (1 - hh)
        aug = place(q_parts, 2 * hp + hh, base)
        aug = aug + jnp.where((lane >= base + 3) & (lane < base + 6), 1.0, 0.0)
        q_aug.append(jnp.where(_head_lane_mask(hh), q2, aug.astype(BF16)))

    def scores(hh, j):
        rows = pl.ds(pl.multiple_of(j * blk, blk), blk)
        return lax.dot_general(kaug_ref[hh, rows, :], q_aug[hh], _NT,
                               preferred_element_type=F32)

    _flash_init(m_ref, l_ref, acc_ref)
    for hh in range(2):
        s_t = jnp.where(krow <= qcol, scores(hh, qi), NEG_INF)
        _flash_update(s_t, vt_ref[qi], m_ref, l_ref, acc_ref, hh)

    def body(j, carry):
        for hh in range(2):
            _flash_update(scores(hh, j), vt_ref[j], m_ref, l_ref, acc_ref, hh)
        return carry
    lax.fori_loop(0, qi, body, 0)

    _write_heads(o_ref, l_ref, acc_ref)


def _fox_attention(qkv3, c3, n_heads):
    b, s, d3 = qkv3.shape
    d = d3 // 3
    blk = min(FOX_BLK, s)
    npair = n_heads // 2
    tile = pl.BlockSpec((1, blk, LANES), lambda bi, hp, qi: (bi, qi, hp))
    return pl.pallas_call(
        _fox_kernel, grid=(b, npair, s // blk),
        in_specs=[
            tile,
            pl.BlockSpec((1, s, LANES), lambda bi, hp, qi: (bi, 0, npair + hp)),
            pl.BlockSpec((1, s, LANES), lambda bi, hp, qi: (bi, 0, 2 * npair + hp)),
            pl.BlockSpec((1, blk, LANES), lambda bi, hp, qi: (bi, qi, 0)),
            pl.BlockSpec((1, s, LANES), lambda bi, hp, qi: (bi, 0, 0)),
        ],
        out_specs=tile,
        out_shape=jax.ShapeDtypeStruct((b, s, d), BF16),
        scratch_shapes=[
            pltpu.VMEM((2, s, LANES), BF16),
            pltpu.VMEM((s // blk, LANES, blk), BF16),
            pltpu.VMEM((2, 1, blk), F32),
            pltpu.VMEM((2, 1, blk), F32),
            pltpu.VMEM((2, LANES, blk), F32),
        ],
        compiler_params=_params(3), name="fox_attention")(qkv3, qkv3, qkv3, c3, c3)


def _t5_bucket_np(n):
    max_exact = REL_BUCKETS // 2
    nf = np.maximum(n, 1).astype(np.float64)
    large = max_exact + (np.log(nf / max_exact) / math.log(REL_MAX_DIST / max_exact)
                         * (REL_BUCKETS - max_exact)).astype(np.int32)
    return np.where(n < max_exact, n, np.minimum(large, REL_BUCKETS - 1)).astype(np.int32)


def _bucket_tiles(blk):
    key = np.arange(blk)[:, None]
    qry = np.arange(blk)[None, :]
    own = np.where(key <= qry, _t5_bucket_np(np.maximum(qry - key, 0)), -1)
    prev = _t5_bucket_np(blk + qry - key)
    return np.stack([own, prev]).astype(np.int32)


def _bias_kernel(tab_ref, bucket_ref, o_ref):
    h = pl.program_id(0)
    bucket = bucket_ref[...]
    acc = jnp.where(bucket < 0, NEG_INF, 0.0).astype(F32)
    for bkt in range(REL_BUCKETS):
        acc = jnp.where(bucket == bkt, tab_ref[bkt, h], acc)
    o_ref[0] = acc


def _bias_tiles(rel_table, blk):
    n_heads = rel_table.shape[1]
    buckets = jnp.asarray(_bucket_tiles(blk))
    return pl.pallas_call(
        _bias_kernel, grid=(n_heads,),
        in_specs=[pl.BlockSpec(memory_space=pltpu.SMEM),
                  pl.BlockSpec((2, blk, blk), lambda h: (0, 0, 0))],
        out_specs=pl.BlockSpec((1, 2, blk, blk), lambda h: (h, 0, 0, 0)),
        out_shape=jax.ShapeDtypeStruct((n_heads, 2, blk, blk), F32),
        compiler_params=_params(1), name="t5_bias_tiles")(rel_table, buckets)


def _moba_kernel(tab_ref, q_ref, k_ref, v_ref, bias_ref, o_ref,
                 vt_ref, km_ref, sel_ref, m_ref, l_ref, acc_ref):
    hp = pl.program_id(1)
    qi = pl.program_id(2)
    blk = MOBA_BLOCK
    tq = q_ref.shape[1]
    nblk = k_ref.shape[1] // blk
    scale = HEAD_DIM ** -0.5
    assert tq == 2 * blk and blk >= REL_MAX_DIST
    first_own = 2 * qi

    @pl.when(qi == 0)
    def _():
        def build(jb, carry):
            rows = pl.ds(pl.multiple_of(jb * blk, blk), blk)
            km_ref[pl.ds(jb, 1), :] = jnp.mean(k_ref[0, rows, :].astype(F32),
                                                axis=0, keepdims=True)
            vt_ref[jb] = _transpose_bf16(v_ref[0, rows, :])
            return carry
        lax.fori_loop(0, nblk, build, 0)

    q2 = q_ref[0]
    km_parts = _split3(km_ref[...])
    blk_id = lax.broadcasted_iota(jnp.int32, (nblk, tq), 0)
    qpos = lax.broadcasted_iota(jnp.int32, (nblk, tq), 1)
    own = first_own + jnp.where(qpos >= blk, 1, 0)
    past = blk_id < own
    q_m = []
    for hh in range(2):
        head = _head_lane_mask(hh)
        q_gate = jnp.where(head, q2, jnp.zeros_like(q2))
        q_m.append((q_gate.astype(F32) * scale).astype(BF16))

        gate = None
        for part in km_parts:
            term = lax.dot_general(part, q_gate, _NT, preferred_element_type=F32)
            gate = term if gate is None else gate + term
        gate = jnp.where(past, gate, NEG_INF)
        rank = jnp.zeros(gate.shape, jnp.int32)
        for jp in range(nblk):
            other = gate[jp:jp + 1, :]
            beats = (other > gate) | ((other == gate) & (jp < blk_id))
            rank = rank + jnp.where(beats, 1, 0)
        keep_all = ((rank < MOBA_TOP_K) & past) | (blk_id == own) | (blk_id == first_own + 1)
        sel_ref[hh] = jnp.where(keep_all, 1.0, 0.0)

    def scores(hh, j):
        rows = pl.ds(pl.multiple_of(j * blk, blk), blk)
        return lax.dot_general(k_ref[0, rows, :], q_m[hh], _NT,
                               preferred_element_type=F32)

    def keep(hh, j):
        return sel_ref[hh, pl.ds(j, 1), :] > 0.5

    _flash_init(m_ref, l_ref, acc_ref)
    far_bias = [tab_ref[REL_BUCKETS - 1, 2 * hp + hh] for hh in range(2)]

    for hh in range(2):
        own_t, prev_t = bias_ref[hh, 0], bias_ref[hh, 1]
        j = first_own + 1
        bias = jnp.concatenate([jnp.full((blk, blk), NEG_INF, F32), own_t], axis=1)
        _flash_update(scores(hh, j) + bias, vt_ref[j], m_ref, l_ref, acc_ref, hh)
        j = first_own
        bias = jnp.concatenate([own_t, prev_t], axis=1)
        _flash_update(scores(hh, j) + bias, vt_ref[j], m_ref, l_ref, acc_ref, hh,
                      keep=keep(hh, j))

    @pl.when(qi >= 1)
    def _():
        j = first_own - 1
        for hh in range(2):
            bias = jnp.concatenate(
                [bias_ref[hh, 1], jnp.full((blk, blk), far_bias[hh], F32)], axis=1)
            _flash_update(scores(hh, j) + bias, vt_ref[j], m_ref, l_ref, acc_ref, hh,
                          keep=keep(hh, j))

    def body(j, carry):
        for hh in range(2):
            _flash_update(scores(hh, j), vt_ref[j], m_ref, l_ref, acc_ref, hh,
                          keep=keep(hh, j), const=far_bias[hh])
        return carry
    lax.fori_loop(0, jnp.maximum(first_own - 1, 0), body, 0)

    _write_heads(o_ref, l_ref, acc_ref)


def _moba_attention(qkv3, rel_table, bias_t, n_heads):
    b, s, d3 = qkv3.shape
    d = d3 // 3
    blk = MOBA_BLOCK
    tq = MOBA_Q_TILE
    assert s % tq == 0
    npair = n_heads // 2
    nblk = s // blk
    tile = pl.BlockSpec((1, tq, LANES), lambda bi, hp, qi: (bi, qi, hp))
    return pl.pallas_call(
        _moba_kernel, grid=(b, npair, s // tq),
        in_specs=[
            pl.BlockSpec(memory_space=pltpu.SMEM),
            tile,
            pl.BlockSpec((1, s, LANES), lambda bi, hp, qi: (bi, 0, npair + hp)),
            pl.BlockSpec((1, s, LANES), lambda bi, hp, qi: (bi, 0, 2 * npair + hp)),
            pl.BlockSpec((2, 2, blk, blk), lambda bi, hp, qi: (hp, 0, 0, 0)),
        ],
        out_specs=tile,
        out_shape=jax.ShapeDtypeStruct((b, s, d), BF16),
        scratch_shapes=[
            pltpu.VMEM((nblk, LANES, blk), BF16),
            pltpu.VMEM((nblk, LANES), F32),
            pltpu.VMEM((2, nblk, tq), F32),
            pltpu.VMEM((2, 1, tq), F32),
            pltpu.VMEM((2, 1, tq), F32),
            pltpu.VMEM((2, LANES, tq), F32),
        ],
        compiler_params=_params(3), name="moba_attention")(
            rel_table, qkv3, qkv3, qkv3, bias_t)


def _oproj_kernel(o_ref, w_ref, h_ref, out_ref):
    out_ref[...] = h_ref[...] + jnp.dot(o_ref[...], w_ref[...],
                                        preferred_element_type=F32)


def _out_project(o2, w, h2, *, tm, tn):
    n, d = h2.shape
    return pl.pallas_call(
        _oproj_kernel, grid=(n // tm, d // tn),
        in_specs=[pl.BlockSpec((tm, o2.shape[1]), lambda i, j: (i, 0)),
                  pl.BlockSpec((o2.shape[1], tn), lambda i, j: (0, j)),
                  pl.BlockSpec((tm, tn), lambda i, j: (i, j))],
        out_specs=pl.BlockSpec((tm, tn), lambda i, j: (i, j)),
        out_shape=jax.ShapeDtypeStruct((n, d), F32),
        compiler_params=_params(2), name="out_proj")(o2, w, h2)


def _ffn_kernel(x_ref, g_ref, wg_ref, wu_ref, wo_ref, out_ref, u_ref, acc_ref):
    c = pl.program_id(1)

    @pl.when(c == 0)
    def _():
        u_ref[...] = _rmsnorm(x_ref[...], g_ref[...]).astype(BF16)
        acc_ref[...] = jnp.zeros_like(acc_ref)

    u = u_ref[...]
    gate = jnp.dot(u, wg_ref[...], preferred_element_type=F32)
    up = jnp.dot(u, wu_ref[...], preferred_element_type=F32)
    act = (gate * jax.nn.sigmoid(gate) * up).astype(BF16)
    acc_ref[...] += jnp.dot(act, wo_ref[...], preferred_element_type=F32)

    @pl.when(c == pl.num_programs(1) - 1)
    def _():
        out_ref[...] = x_ref[...] + acc_ref[...]


def _ffn(h2, g, w_in, w_out, *, tm, tf):
    n, d = h2.shape
    d_ff = w_out.shape[0]
    nchunk = d_ff // tf
    return pl.pallas_call(
        _ffn_kernel, grid=(n // tm, nchunk),
        in_specs=[pl.BlockSpec((tm, d), lambda i, c: (i, 0)),
                  pl.BlockSpec((1, d), lambda i, c: (0, 0)),
                  pl.BlockSpec((d, tf), lambda i, c: (0, c)),
                  pl.BlockSpec((d, tf), lambda i, c: (0, nchunk + c)),
                  pl.BlockSpec((tf, d), lambda i, c: (c, 0))],
        out_specs=pl.BlockSpec((tm, d), lambda i, c: (i, 0)),
        out_shape=jax.ShapeDtypeStruct((n, d), F32),
        scratch_shapes=[pltpu.VMEM((tm, d), BF16), pltpu.VMEM((tm, d), F32)],
        compiler_params=_params(2), name="swiglu_ffn")(h2, g, w_in, w_in, w_out)


def _ple_kernel(x_ref, g_ref, wg_ref, p_ref, wu_ref, fg_ref, out_ref, *, final_norm):
    x = x_ref[...]
    u = _rmsnorm(x, g_ref[...]).astype(BF16)
    gate = jax.nn.sigmoid(jnp.dot(u, wg_ref[...], preferred_element_type=F32))
    up = jnp.dot(p_ref[...].astype(BF16), wu_ref[...], preferred_element_type=F32)
    y = x + gate * up
    if final_norm:
        y = _rmsnorm(y, fg_ref[...])
    out_ref[...] = y


def _ple(h2, g, w_gate, p2, w_up, final_g, *, tm, final_norm):
    n, d = h2.shape
    pd = p2.shape[1]
    return pl.pallas_call(
        functools.partial(_ple_kernel, final_norm=final_norm), grid=(n // tm,),
        in_specs=[pl.BlockSpec((tm, d), lambda i: (i, 0)),
                  pl.BlockSpec((1, d), lambda i: (0, 0)),
                  pl.BlockSpec((d, d), lambda i: (0, 0)),
                  pl.BlockSpec((tm, pd), lambda i: (i, 0)),
                  pl.BlockSpec((pd, d), lambda i: (0, 0)),
                  pl.BlockSpec((1, d), lambda i: (0, 0))],
        out_specs=pl.BlockSpec((tm, d), lambda i: (i, 0)),
        out_shape=jax.ShapeDtypeStruct((n, d), F32),
        compiler_params=_params(1), name="ple")(h2, g, w_gate, p2, w_up, final_g)


def _row_tile(n, want):
    t = min(want, n)
    assert n % t == 0
    return t


def _col_tile(n, want):
    t = min(want, n)
    while n % t:
        t -= LANES
    return t


def kernel(x, p, attn_norm_g, fox_w_in, fox_b_f, fox_w_o, moba_w_in, moba_w_o, rel_bias_table,
           ffn_norm_g, ffn_w_in, ffn_w_out, ple_norm_g, ple_w_gate, ple_w_up, final_norm_g):
    b, s, d = x.shape
    depth = p.shape[0]
    n_heads = rel_bias_table.shape[1]
    assert d == n_heads * HEAD_DIM and n_heads % 2 == 0 and n_heads <= LANES
    n = b * s
    tm = _row_tile(n, 1024)
    tm_ple = _row_tile(n, 512)
    tn = _col_tile(d, 512)
    tf = _col_tile(ffn_w_out.shape[1], 256)

    bias_t = _bias_tiles(rel_bias_table.astype(F32), MOBA_BLOCK)
    final_g = final_norm_g.reshape(1, d).astype(F32)

    h = x.reshape(n, d).astype(F32)
    for i in range(depth):
        j = i // 2
        g_attn = attn_norm_g[i].reshape(1, d).astype(F32)
        if i % 2 == 0:
            w = fox_w_in[j]
            w_qkv = w[:, :3 * d].astype(BF16)
            w_f = jnp.pad(w[:, 3 * d:], ((0, 0), (0, LANES - n_heads))).astype(BF16)
            b_f = jnp.pad(fox_b_f[j].astype(F32), (0, LANES - n_heads)).reshape(1, LANES)
            qkv, log_f = _project(h, g_attn, w_qkv, w_f, b_f, tm=tm, tn=_col_tile(3 * d, 512))
            c = _cumsum(log_f.reshape(b, s, LANES))
            o = _fox_attention(qkv.reshape(b, s, 3 * d), c, n_heads)
            w_o = fox_w_o[j]
        else:
            qkv = _project(h, g_attn, moba_w_in[j].astype(BF16), tm=tm, tn=_col_tile(3 * d, 512))
            o = _moba_attention(qkv.reshape(b, s, 3 * d), rel_bias_table.astype(F32),
                                bias_t, n_heads)
            w_o = moba_w_o[j]
        h = _out_project(o.reshape(n, d), w_o.astype(BF16), h, tm=tm, tn=tn)
        h = _ffn(h, ffn_norm_g[i].reshape(1, d).astype(F32), ffn_w_in[i].astype(BF16),
                 ffn_w_out[i].astype(BF16), tm=tm, tf=tf)
        h = _ple(h, ple_norm_g[i].reshape(1, d).astype(F32), ple_w_gate[i].astype(BF16),
                 p[i].reshape(n, -1), ple_w_up[i].astype(BF16), final_g,
                 tm=tm_ple, final_norm=(i == depth - 1))
    return h.reshape(b, s, d).astype(x.dtype)
```

```python
import functools
import math

import numpy as np
import jax
import jax.numpy as jnp
from jax import lax
from jax.experimental import pallas as pl
from jax.experimental.pallas import tpu as pltpu

F32 = jnp.float32
BF16 = jnp.bfloat16

RMS_EPS = 1e-6
HEAD_DIM = 64
MOBA_BLOCK = 256
MOBA_TOP_K = 3
REL_BUCKETS = 32
REL_MAX_DIST = 128

LANES = 128
FOX_BLK = 512
MOBA_Q_TILE = 2 * MOBA_BLOCK
CUMSUM_BLK = 256
VMEM_LIMIT_BYTES = 56 * 1024 * 1024
NEG_INF = float("-inf")

_NT = (((1,), (1,)), ((), ()))


def _params(n_axes):
    return pltpu.CompilerParams(
        dimension_semantics=("arbitrary",) * n_axes,
        vmem_limit_bytes=VMEM_LIMIT_BYTES)


def _split3(x):
    x1 = x.astype(BF16)
    r1 = x - x1.astype(F32)
    x2 = r1.astype(BF16)
    x3 = (r1 - x2.astype(F32)).astype(BF16)
    return x1, x2, x3


def _rmsnorm(x, g):
    ms = jnp.mean(x * x, axis=-1, keepdims=True)
    return x * lax.rsqrt(ms + RMS_EPS) * g


def _log_sigmoid(x):
    return jnp.minimum(x, 0.0) - jnp.log1p(jnp.exp(-jnp.abs(x)))


def _proj_kernel(x_ref, g_ref, w_ref, o_ref, u_ref):
    @pl.when(pl.program_id(1) == 0)
    def _():
        u_ref[...] = _rmsnorm(x_ref[...], g_ref[...]).astype(BF16)

    o_ref[...] = jnp.dot(u_ref[...], w_ref[...],
                         preferred_element_type=F32).astype(o_ref.dtype)


def _proj_gate_kernel(x_ref, g_ref, w_ref, wf_ref, bf_ref, o_ref, lf_ref, u_ref):
    @pl.when(pl.program_id(1) == 0)
    def _():
        u = _rmsnorm(x_ref[...], g_ref[...]).astype(BF16)
        u_ref[...] = u
        f_logit = jnp.dot(u, wf_ref[...], preferred_element_type=F32) + bf_ref[...]
        lf_ref[...] = _log_sigmoid(f_logit)

    o_ref[...] = jnp.dot(u_ref[...], w_ref[...],
                         preferred_element_type=F32).astype(o_ref.dtype)


def _project(h2, g, w, wf=None, bf=None, *, tm, tn):
    n, d = h2.shape
    nout = w.shape[1]
    grid = (n // tm, nout // tn)
    x_spec = pl.BlockSpec((tm, d), lambda i, j: (i, 0))
    g_spec = pl.BlockSpec((1, d), lambda i, j: (0, 0))
    w_spec = pl.BlockSpec((d, tn), lambda i, j: (0, j))
    o_spec = pl.BlockSpec((tm, tn), lambda i, j: (i, j))
    scratch = [pltpu.VMEM((tm, d), BF16)]
    if wf is None:
        return pl.pallas_call(
            _proj_kernel, grid=grid,
            in_specs=[x_spec, g_spec, w_spec], out_specs=o_spec,
            out_shape=jax.ShapeDtypeStruct((n, nout), BF16),
            scratch_shapes=scratch, compiler_params=_params(2),
            name="proj")(h2, g, w)
    return pl.pallas_call(
        _proj_gate_kernel, grid=grid,
        in_specs=[x_spec, g_spec, w_spec,
                  pl.BlockSpec((d, LANES), lambda i, j: (0, 0)),
                  pl.BlockSpec((1, LANES), lambda i, j: (0, 0))],
        out_specs=[o_spec, pl.BlockSpec((tm, LANES), lambda i, j: (i, 0))],
        out_shape=[jax.ShapeDtypeStruct((n, nout), BF16),
                   jax.ShapeDtypeStruct((n, LANES), F32)],
        scratch_shapes=scratch, compiler_params=_params(2),
        name="proj_gate")(h2, g, w, wf, bf)


def _cumsum_kernel(lf_ref, c_ref, carry_ref):
    @pl.when(pl.program_id(1) == 0)
    def _():
        carry_ref[...] = jnp.zeros_like(carry_ref)

    t = lf_ref.shape[1]
    row = lax.broadcasted_iota(jnp.int32, (t, t), 0)
    col = lax.broadcasted_iota(jnp.int32, (t, t), 1)
    tril = jnp.where(col <= row, 1.0, 0.0).astype(BF16)
    x1, x2, x3 = _split3(lf_ref[0])
    cs = (jnp.dot(tril, x1, preferred_element_type=F32)
          + jnp.dot(tril, x2, preferred_element_type=F32)
          + jnp.dot(tril, x3, preferred_element_type=F32))
    cs = cs + carry_ref[0:1, :]
    c_ref[0] = cs
    carry_ref[...] = jnp.broadcast_to(cs[t - 1:t, :], carry_ref.shape)


def _cumsum(lf3):
    b, s, _ = lf3.shape
    t = min(CUMSUM_BLK, s)
    spec = pl.BlockSpec((1, t, LANES), lambda i, j: (i, j, 0))
    return pl.pallas_call(
        _cumsum_kernel, grid=(b, s // t),
        in_specs=[spec], out_specs=spec,
        out_shape=jax.ShapeDtypeStruct(lf3.shape, F32),
        scratch_shapes=[pltpu.VMEM((8, LANES), F32)],
        compiler_params=_params(2), name="gate_cumsum")(lf3)


def _head_lane_mask(hh):
    lane = lax.broadcasted_iota(jnp.int32, (1, LANES), 1)
    return (lane < HEAD_DIM) if hh == 0 else (lane >= HEAD_DIM)


def _transpose_bf16(x):
    return x.astype(F32).T.astype(BF16)


def _flash_init(m_ref, l_ref, acc_ref):
    m_ref[...] = jnp.full(m_ref.shape, NEG_INF, F32)
    l_ref[...] = jnp.zeros(l_ref.shape, F32)
    acc_ref[...] = jnp.zeros(acc_ref.shape, F32)


def _pipelined_blocks(n, last_block, scores_into, update):
    scores_into(0, 0)

    def pair(jj, carry):
        j = 2 * jj
        scores_into(jnp.minimum(j + 1, last_block), 1)
        update(j, 0)
        scores_into(jnp.minimum(j + 2, last_block), 0)
        update(j + 1, 1)
        return carry
    lax.fori_loop(0, n // 2, pair, 0)

    @pl.when(n % 2 == 1)
    def _():
        update(n - 1, 0)


def _flash_update(load_s, v_t, m_ref, l_ref, acc_ref, hh, keep=None, const=None):
    m_old = m_ref[hh]
    mx = jnp.max(load_s(), axis=0, keepdims=True)
    if const is not None:
        mx = mx + const
    if keep is not None:
        mx = jnp.where(keep, mx, NEG_INF)
    m_new = jnp.maximum(m_old, mx)
    m_safe = jnp.where(m_new == NEG_INF, 0.0, m_new)
    shift = m_safe if const is None else m_safe - const
    if keep is not None:
        shift = jnp.where(keep, shift, float("inf"))
    p = jnp.exp(load_s() - shift)
    alpha = jnp.exp(m_old - m_safe)
    m_ref[hh] = m_new
    l_ref[hh] = alpha * l_ref[hh] + jnp.sum(p, axis=0, keepdims=True)
    acc_ref[hh] = alpha * acc_ref[hh] + jnp.dot(
        v_t, p.astype(BF16), preferred_element_type=F32)


def _write_heads(o_ref, l_ref, acc_ref):
    o0 = acc_ref[0] / l_ref[0]
    o1 = acc_ref[1] / l_ref[1]
    row = lax.broadcasted_iota(jnp.int32, o0.shape, 0)
    o_t = jnp.where(row < HEAD_DIM, o0, o1)
    o_ref[0] = o_t.T.astype(o_ref.dtype)


def _fox_kernel(q_ref, k_ref, v_ref, cq_ref, ck_ref, o_ref,
                kaug_ref, vt_ref, s_ref, m_ref, l_ref, acc_ref):
    hp = pl.program_id(1)
    qi = pl.program_id(2)
    blk = q_ref.shape[1]
    nblk = k_ref.shape[1] // blk
    scale = HEAD_DIM ** -0.5
    lane = lax.broadcasted_iota(jnp.int32, (1, LANES), 1)
    prow = lax.broadcasted_iota(jnp.int32, (LANES, LANES), 0)
    pcol = lax.broadcasted_iota(jnp.int32, (LANES, LANES), 1)

    def place(parts, src_lane, first_dst_lane):
        out = None
        for i, part in enumerate(parts):
            sel = jnp.where((prow == src_lane) & (pcol == first_dst_lane + i),
                            1.0, 0.0).astype(BF16)
            term = jnp.dot(part, sel, preferred_element_type=F32)
            out = term if out is None else out + term
        return out

    @pl.when(qi == 0)
    def _():
        def build(jb, carry):
            rows = pl.ds(pl.multiple_of(jb * blk, blk), blk)
            k2 = k_ref[0, rows, :]
            parts = _split3(ck_ref[0, rows, :])
            for hh in range(2):
                base = HEAD_DIM * (1 - hh)
                aug = -place(parts, 2 * hp + hh, base + 3)
                aug = aug + jnp.where((lane >= base) & (lane < base + 3), 1.0, 0.0)
                kaug_ref[hh, rows, :] = jnp.where(_head_lane_mask(hh), k2, aug.astype(BF16))
            vt_ref[jb] = _transpose_bf16(v_ref[0, rows, :])
            return carry
        lax.fori_loop(0, nblk, build, 0)

    q2 = (q_ref[0].astype(F32) * scale).astype(BF16)
    q_parts = _split3(cq_ref[0])
    krow = lax.broadcasted_iota(jnp.int32, (blk, blk), 0)
    qcol = lax.broadcasted_iota(jnp.int32, (blk, blk), 1)
    q_aug = []
    for hh in range(2):
        base = HEAD_DIM * (1 - hh)
        aug = place(q_parts, 2 * hp + hh, base)
        aug = aug + jnp.where((lane >= base + 3) & (lane < base + 6), 1.0, 0.0)
        q_aug.append(jnp.where(_head_lane_mask(hh), q2, aug.astype(BF16)))

    def scores(hh, j):
        rows = pl.ds(pl.multiple_of(j * blk, blk), blk)
        return lax.dot_general(kaug_ref[hh, rows, :], q_aug[hh], _NT,
                               preferred_element_type=F32)

    _flash_init(m_ref, l_ref, acc_ref)
    for hh in range(2):
        s_ref[1, hh] = jnp.where(krow <= qcol, scores(hh, qi), NEG_INF)
        _flash_update(lambda hh=hh: s_ref[1, hh], vt_ref[qi], m_ref, l_ref, acc_ref, hh)

    def scores_into(j, slot):
        for hh in range(2):
            s_ref[slot, hh] = scores(hh, j)

    def update(j, slot):
        for hh in range(2):
            _flash_update(lambda hh=hh: s_ref[slot, hh], vt_ref[j],
                          m_ref, l_ref, acc_ref, hh)

    _pipelined_blocks(qi, nblk - 1, scores_into, update)
    _write_heads(o_ref, l_ref, acc_ref)


def _fox_attention(qkv3, c3, n_heads):
    b, s, d3 = qkv3.shape
    d = d3 // 3
    blk = min(FOX_BLK, s)
    npair = n_heads // 2
    tile = pl.BlockSpec((1, blk, LANES), lambda bi, hp, qi: (bi, qi, hp))
    return pl.pallas_call(
        _fox_kernel, grid=(b, npair, s // blk),
        in_specs=[
            tile,
            pl.BlockSpec((1, s, LANES), lambda bi, hp, qi: (bi, 0, npair + hp)),
            pl.BlockSpec((1, s, LANES), lambda bi, hp, qi: (bi, 0, 2 * npair + hp)),
            pl.BlockSpec((1, blk, LANES), lambda bi, hp, qi: (bi, qi, 0)),
            pl.BlockSpec((1, s, LANES), lambda bi, hp, qi: (bi, 0, 0)),
        ],
        out_specs=tile,
        out_shape=jax.ShapeDtypeStruct((b, s, d), BF16),
        scratch_shapes=[
            pltpu.VMEM((2, s, LANES), BF16),
            pltpu.VMEM((s // blk, LANES, blk), BF16),
            pltpu.VMEM((2, 2, blk, blk), F32),
            pltpu.VMEM((2, 1, blk), F32),
            pltpu.VMEM((2, 1, blk), F32),
            pltpu.VMEM((2, LANES, blk), F32),
        ],
        compiler_params=_params(3), name="fox_attention")(qkv3, qkv3, qkv3, c3, c3)


def _t5_bucket_np(n):
    max_exact = REL_BUCKETS // 2
    nf = np.maximum(n, 1).astype(np.float64)
    large = max_exact + (np.log(nf / max_exact) / math.log(REL_MAX_DIST / max_exact)
                         * (REL_BUCKETS - max_exact)).astype(np.int32)
    return np.where(n < max_exact, n, np.minimum(large, REL_BUCKETS - 1)).astype(np.int32)


def _bucket_tiles(blk):
    key = np.arange(blk)[:, None]
    qry = np.arange(blk)[None, :]
    own = np.where(key <= qry, _t5_bucket_np(np.maximum(qry - key, 0)), -1)
    prev = _t5_bucket_np(blk + qry - key)
    return np.stack([own, prev]).astype(np.int32)


def _bias_kernel(tab_ref, bucket_ref, o_ref):
    h = pl.program_id(0)
    bucket = bucket_ref[...]
    acc = jnp.where(bucket < 0, NEG_INF, 0.0).astype(F32)
    for bkt in range(REL_BUCKETS):
        acc = jnp.where(bucket == bkt, tab_ref[bkt, h], acc)
    o_ref[0] = acc


def _bias_tiles(rel_table, blk):
    n_heads = rel_table.shape[1]
    buckets = jnp.asarray(_bucket_tiles(blk))
    return pl.pallas_call(
        _bias_kernel, grid=(n_heads,),
        in_specs=[pl.BlockSpec(memory_space=pltpu.SMEM),
                  pl.BlockSpec((2, blk, blk), lambda h: (0, 0, 0))],
        out_specs=pl.BlockSpec((1, 2, blk, blk), lambda h: (h, 0, 0, 0)),
        out_shape=jax.ShapeDtypeStruct((n_heads, 2, blk, blk), F32),
        compiler_params=_params(1), name="t5_bias_tiles")(rel_table, buckets)


def _moba_kernel(tab_ref, q_ref, k_ref, v_ref, bias_ref, o_ref,
                 vt_ref, km_ref, sel_ref, s_ref, m_ref, l_ref, acc_ref):
    hp = pl.program_id(1)
    qi = pl.program_id(2)
    blk = MOBA_BLOCK
    tq = q_ref.shape[1]
    nblk = k_ref.shape[1] // blk
    scale = HEAD_DIM ** -0.5
    assert tq == 2 * blk and blk >= REL_MAX_DIST
    first_own = 2 * qi

    @pl.when(qi == 0)
    def _():
        def build(jb, carry):
            rows = pl.ds(pl.multiple_of(jb * blk, blk), blk)
            km_ref[pl.ds(jb, 1), :] = jnp.mean(k_ref[0, rows, :].astype(F32),
                                                axis=0, keepdims=True)
            vt_ref[jb] = _transpose_bf16(v_ref[0, rows, :])
            return carry
        lax.fori_loop(0, nblk, build, 0)

    q2 = q_ref[0]
    km_parts = _split3(km_ref[...])
    blk_id = lax.broadcasted_iota(jnp.int32, (nblk, tq), 0)
    qpos = lax.broadcasted_iota(jnp.int32, (nblk, tq), 1)
    own = first_own + jnp.where(qpos >= blk, 1, 0)
    past = blk_id < own
    q_m = []
    for hh in range(2):
        head = _head_lane_mask(hh)
        q_gate = jnp.where(head, q2, jnp.zeros_like(q2))
        q_m.append((q_gate.astype(F32) * scale).astype(BF16))

        gate = None
        for part in km_parts:
            term = lax.dot_general(part, q_gate, _NT, preferred_element_type=F32)
            gate = term if gate is None else gate + term
        gate = jnp.where(past, gate, NEG_INF)
        rank = jnp.zeros(gate.shape, jnp.int32)
        for jp in range(nblk):
            other = gate[jp:jp + 1, :]
            beats = (other > gate) | ((other == gate) & (jp < blk_id))
            rank = rank + jnp.where(beats, 1, 0)
        keep_all = ((rank < MOBA_TOP_K) & past) | (blk_id == own) | (blk_id == first_own + 1)
        sel_ref[hh] = jnp.where(keep_all, 1.0, 0.0)

    def scores(hh, j):
        rows = pl.ds(pl.multiple_of(j * blk, blk), blk)
        return lax.dot_general(k_ref[0, rows, :], q_m[hh], _NT,
                               preferred_element_type=F32)

    def keep(hh, j):
        return sel_ref[hh, pl.ds(j, 1), :] > 0.5

    _flash_init(m_ref, l_ref, acc_ref)
    far_bias = [tab_ref[REL_BUCKETS - 1, 2 * hp + hh] for hh in range(2)]

    for hh in range(2):
        own_t, prev_t = bias_ref[hh, 0], bias_ref[hh, 1]
        j = first_own + 1
        bias = jnp.concatenate([jnp.full((blk, blk), NEG_INF, F32), own_t], axis=1)
        s_ref[0, hh] = scores(hh, j) + bias
        _flash_update(lambda hh=hh: s_ref[0, hh], vt_ref[j], m_ref, l_ref, acc_ref, hh)
        j = first_own
        bias = jnp.concatenate([own_t, prev_t], axis=1)
        s_ref[1, hh] = scores(hh, j) + bias
        _flash_update(lambda hh=hh: s_ref[1, hh], vt_ref[j], m_ref, l_ref, acc_ref, hh,
                      keep=keep(hh, j))

    @pl.when(qi >= 1)
    def _():
        j = first_own - 1
        for hh in range(2):
            bias = jnp.concatenate(
                [bias_ref[hh, 1], jnp.full((blk, blk), far_bias[hh], F32)], axis=1)
            s_ref[0, hh] = scores(hh, j) + bias
            _flash_update(lambda hh=hh: s_ref[0, hh], vt_ref[j], m_ref, l_ref, acc_ref, hh,
                          keep=keep(hh, j))

    def scores_into(j, slot):
        for hh in range(2):
            s_ref[slot, hh] = scores(hh, j)

    def update(j, slot):
        for hh in range(2):
            _flash_update(lambda hh=hh: s_ref[slot, hh], vt_ref[j],
                          m_ref, l_ref, acc_ref, hh,
                          keep=keep(hh, j), const=far_bias[hh])

    _pipelined_blocks(jnp.maximum(first_own - 1, 0), nblk - 1, scores_into, update)
    _write_heads(o_ref, l_ref, acc_ref)


def _moba_attention(qkv3, rel_table, bias_t, n_heads):
    b, s, d3 = qkv3.shape
    d = d3 // 3
    blk = MOBA_BLOCK
    tq = MOBA_Q_TILE
    assert s % tq == 0
    npair = n_heads // 2
    nblk = s // blk
    tile = pl.BlockSpec((1, tq, LANES), lambda bi, hp, qi: (bi, qi, hp))
    return pl.pallas_call(
        _moba_kernel, grid=(b, npair, s // tq),
        in_specs=[
            pl.BlockSpec(memory_space=pltpu.SMEM),
            tile,
            pl.BlockSpec((1, s, LANES), lambda bi, hp, qi: (bi, 0, npair + hp)),
            pl.BlockSpec((1, s, LANES), lambda bi, hp, qi: (bi, 0, 2 * npair + hp)),
            pl.BlockSpec((2, 2, blk, blk), lambda bi, hp, qi: (hp, 0, 0, 0)),
        ],
        out_specs=tile,
        out_shape=jax.ShapeDtypeStruct((b, s, d), BF16),
        scratch_shapes=[
            pltpu.VMEM((nblk, LANES, blk), BF16),
            pltpu.VMEM((nblk, LANES), F32),
            pltpu.VMEM((2, nblk, tq), F32),
            pltpu.VMEM((2, 2, blk, tq), F32),
            pltpu.VMEM((2, 1, tq), F32),
            pltpu.VMEM((2, 1, tq), F32),
            pltpu.VMEM((2, LANES, tq), F32),
        ],
        compiler_params=_params(3), name="moba_attention")(
            rel_table, qkv3, qkv3, qkv3, bias_t)


def _oproj_kernel(o_ref, w_ref, h_ref, out_ref):
    out_ref[...] = h_ref[...] + jnp.dot(o_ref[...], w_ref[...],
                                        preferred_element_type=F32)


def _out_project(o2, w, h2, *, tm, tn):
    n, d = h2.shape
    return pl.pallas_call(
        _oproj_kernel, grid=(n // tm, d // tn),
        in_specs=[pl.BlockSpec((tm, o2.shape[1]), lambda i, j: (i, 0)),
                  pl.BlockSpec((o2.shape[1], tn), lambda i, j: (0, j)),
                  pl.BlockSpec((tm, tn), lambda i, j: (i, j))],
        out_specs=pl.BlockSpec((tm, tn), lambda i, j: (i, j)),
        out_shape=jax.ShapeDtypeStruct((n, d), F32),
        compiler_params=_params(2), name="out_proj")(o2, w, h2)


def _ffn_kernel(x_ref, g_ref, wg_ref, wu_ref, wo_ref, out_ref, u_ref, acc_ref):
    c = pl.program_id(1)

    @pl.when(c == 0)
    def _():
        u_ref[...] = _rmsnorm(x_ref[...], g_ref[...]).astype(BF16)
        acc_ref[...] = jnp.zeros_like(acc_ref)

    u = u_ref[...]
    gate = jnp.dot(u, wg_ref[...], preferred_element_type=F32)
    up = jnp.dot(u, wu_ref[...], preferred_element_type=F32)
    act = (gate * jax.nn.sigmoid(gate) * up).astype(BF16)
    acc_ref[...] += jnp.dot(act, wo_ref[...], preferred_element_type=F32)

    @pl.when(c == pl.num_programs(1) - 1)
    def _():
        out_ref[...] = x_ref[...] + acc_ref[...]


def _ffn(h2, g, w_in, w_out, *, tm, tf):
    n, d = h2.shape
    d_ff = w_out.shape[0]
    nchunk = d_ff // tf
    return pl.pallas_call(
        _ffn_kernel, grid=(n // tm, nchunk),
        in_specs=[pl.BlockSpec((tm, d), lambda i, c: (i, 0)),
                  pl.BlockSpec((1, d), lambda i, c: (0, 0)),
                  pl.BlockSpec((d, tf), lambda i, c: (0, c)),
                  pl.BlockSpec((d, tf), lambda i, c: (0, nchunk + c)),
                  pl.BlockSpec((tf, d), lambda i, c: (c, 0))],
        out_specs=pl.BlockSpec((tm, d), lambda i, c: (i, 0)),
        out_shape=jax.ShapeDtypeStruct((n, d), F32),
        scratch_shapes=[pltpu.VMEM((tm, d), BF16), pltpu.VMEM((tm, d), F32)],
        compiler_params=_params(2), name="swiglu_ffn")(h2, g, w_in, w_in, w_out)


def _ple_kernel(x_ref, g_ref, wg_ref, p_ref, wu_ref, fg_ref, out_ref, *, final_norm):
    x = x_ref[...]
    u = _rmsnorm(x, g_ref[...]).astype(BF16)
    gate = jax.nn.sigmoid(jnp.dot(u, wg_ref[...], preferred_element_type=F32))
    up = jnp.dot(p_ref[...].astype(BF16), wu_ref[...], preferred_element_type=F32)
    y = x + gate * up
    if final_norm:
        y = _rmsnorm(y, fg_ref[...])
    out_ref[...] = y


def _ple(h2, g, w_gate, p2, w_up, final_g, *, tm, final_norm):
    n, d = h2.shape
    pd = p2.shape[1]
    return pl.pallas_call(
        functools.partial(_ple_kernel, final_norm=final_norm), grid=(n // tm,),
        in_specs=[pl.BlockSpec((tm, d), lambda i: (i, 0)),
                  pl.BlockSpec((1, d), lambda i: (0, 0)),
                  pl.BlockSpec((d, d), lambda i: (0, 0)),
                  pl.BlockSpec((tm, pd), lambda i: (i, 0)),
                  pl.BlockSpec((pd, d), lambda i: (0, 0)),
                  pl.BlockSpec((1, d), lambda i: (0, 0))],
        out_specs=pl.BlockSpec((tm, d), lambda i: (i, 0)),
        out_shape=jax.ShapeDtypeStruct((n, d), F32),
        compiler_params=_params(1), name="ple")(h2, g, w_gate, p2, w_up, final_g)


def _row_tile(n, want):
    t = min(want, n)
    assert n % t == 0
    return t


def _col_tile(n, want):
    t = min(want, n)
    while n % t:
        t -= LANES
    return t


def kernel(x, p, attn_norm_g, fox_w_in, fox_b_f, fox_w_o, moba_w_in, moba_w_o, rel_bias_table,
           ffn_norm_g, ffn_w_in, ffn_w_out, ple_norm_g, ple_w_gate, ple_w_up, final_norm_g):
    b, s, d = x.shape
    depth = p.shape[0]
    n_heads = rel_bias_table.shape[1]
    assert d == n_heads * HEAD_DIM and n_heads % 2 == 0 and n_heads <= LANES
    n = b * s
    tm = _row_tile(n, 1024)
    tm_ple = _row_tile(n, 512)
    tn = _col_tile(d, 512)
    tf = _col_tile(ffn_w_out.shape[1], 256)

    bias_t = _bias_tiles(rel_bias_table.astype(F32), MOBA_BLOCK)
    final_g = final_norm_g.reshape(1, d).astype(F32)

    h = x.reshape(n, d).astype(F32)
    for i in range(depth):
        j = i // 2
        g_attn = attn_norm_g[i].reshape(1, d).astype(F32)
        if i % 2 == 0:
            w = fox_w_in[j]
            w_qkv = w[:, :3 * d].astype(BF16)
            w_f = jnp.pad(w[:, 3 * d:], ((0, 0), (0, LANES - n_heads))).astype(BF16)
            b_f = jnp.pad(fox_b_f[j].astype(F32), (0, LANES - n_heads)).reshape(1, LANES)
            qkv, log_f = _project(h, g_attn, w_qkv, w_f, b_f, tm=tm, tn=_col_tile(3 * d, 512))
            c = _cumsum(log_f.reshape(b, s, LANES))
            o = _fox_attention(qkv.reshape(b, s, 3 * d), c, n_heads)
            w_o = fox_w_o[j]
        else:
            qkv = _project(h, g_attn, moba_w_in[j].astype(BF16), tm=tm, tn=_col_tile(3 * d, 512))
            o = _moba_attention(qkv.reshape(b, s, 3 * d), rel_bias_table.astype(F32),
                                bias_t, n_heads)
            w_o = moba_w_o[j]
        h = _out_project(o.reshape(n, d), w_o.astype(BF16), h, tm=tm, tn=tn)
        h = _ffn(h, ffn_norm_g[i].reshape(1, d).astype(F32), ffn_w_in[i].astype(BF16),
                 ffn_w_out[i].astype(BF16), tm=tm, tf=tf)
        h = _ple(h, ple_norm_g[i].reshape(1, d).astype(F32), ple_w_gate[i].astype(BF16),
                 p[i].reshape(n, -1), ple_w_up[i].astype(BF16), final_g,
                 tm=tm_ple, final_norm=(i == depth - 1))
    return h.reshape(b, s, d).astype(x.dtype)
```

```python
import functools
import math

import numpy as np
import jax
import jax.numpy as jnp
from jax import lax
from jax.experimental import pallas as pl
from jax.experimental.pallas import tpu as pltpu

F32 = jnp.float32
BF16 = jnp.bfloat16

RMS_EPS = 1e-6
HEAD_DIM = 64
MOBA_BLOCK = 256
MOBA_TOP_K = 3
REL_BUCKETS = 32
REL_MAX_DIST = 128

LANES = 128
BF16_SUBLANES = 16
V_ROWS = HEAD_DIM + BF16_SUBLANES
LOG2E = math.log2(math.e)
Q_SCALE = HEAD_DIM ** -0.5 * LOG2E
FOX_BLK = 512
MOBA_Q_TILE = 2 * MOBA_BLOCK
CUMSUM_BLK = 256
VMEM_LIMIT_BYTES = 56 * 1024 * 1024
NEG_INF = float("-inf")

_NT = (((1,), (1,)), ((), ()))


def _params(n_axes):
    return pltpu.CompilerParams(
        dimension_semantics=("arbitrary",) * n_axes,
        vmem_limit_bytes=VMEM_LIMIT_BYTES)


def _split3(x):
    x1 = x.astype(BF16)
    r1 = x - x1.astype(F32)
    x2 = r1.astype(BF16)
    x3 = (r1 - x2.astype(F32)).astype(BF16)
    return x1, x2, x3


def _rmsnorm(x, g):
    ms = jnp.mean(x * x, axis=-1, keepdims=True)
    return x * lax.rsqrt(ms + RMS_EPS) * g


def _log_sigmoid(x):
    return jnp.minimum(x, 0.0) - jnp.log1p(jnp.exp(-jnp.abs(x)))


def _proj_kernel(x_ref, g_ref, w_ref, o_ref, u_ref):
    @pl.when(pl.program_id(1) == 0)
    def _():
        u_ref[...] = _rmsnorm(x_ref[...], g_ref[...]).astype(BF16)

    o_ref[...] = jnp.dot(u_ref[...], w_ref[...],
                         preferred_element_type=F32).astype(o_ref.dtype)


def _proj_gate_kernel(x_ref, g_ref, w_ref, wf_ref, bf_ref, o_ref, lf_ref, u_ref):
    @pl.when(pl.program_id(1) == 0)
    def _():
        u = _rmsnorm(x_ref[...], g_ref[...]).astype(BF16)
        u_ref[...] = u
        f_logit = jnp.dot(u, wf_ref[...], preferred_element_type=F32) + bf_ref[...]
        lf_ref[...] = _log_sigmoid(f_logit)

    o_ref[...] = jnp.dot(u_ref[...], w_ref[...],
                         preferred_element_type=F32).astype(o_ref.dtype)


def _project(h2, g, w, wf=None, bf=None, *, tm, tn):
    n, d = h2.shape
    nout = w.shape[1]
    grid = (n // tm, nout // tn)
    x_spec = pl.BlockSpec((tm, d), lambda i, j: (i, 0))
    g_spec = pl.BlockSpec((1, d), lambda i, j: (0, 0))
    w_spec = pl.BlockSpec((d, tn), lambda i, j: (0, j))
    o_spec = pl.BlockSpec((tm, tn), lambda i, j: (i, j))
    scratch = [pltpu.VMEM((tm, d), BF16)]
    if wf is None:
        return pl.pallas_call(
            _proj_kernel, grid=grid,
            in_specs=[x_spec, g_spec, w_spec], out_specs=o_spec,
            out_shape=jax.ShapeDtypeStruct((n, nout), BF16),
            scratch_shapes=scratch, compiler_params=_params(2),
            name="proj")(h2, g, w)
    return pl.pallas_call(
        _proj_gate_kernel, grid=grid,
        in_specs=[x_spec, g_spec, w_spec,
                  pl.BlockSpec((d, LANES), lambda i, j: (0, 0)),
                  pl.BlockSpec((1, LANES), lambda i, j: (0, 0))],
        out_specs=[o_spec, pl.BlockSpec((tm, LANES), lambda i, j: (i, 0))],
        out_shape=[jax.ShapeDtypeStruct((n, nout), BF16),
                   jax.ShapeDtypeStruct((n, LANES), F32)],
        scratch_shapes=scratch, compiler_params=_params(2),
        name="proj_gate")(h2, g, w, wf, bf)


def _cumsum_kernel(lf_ref, c_ref, carry_ref):
    @pl.when(pl.program_id(1) == 0)
    def _():
        carry_ref[...] = jnp.zeros_like(carry_ref)

    t = lf_ref.shape[1]
    row = lax.broadcasted_iota(jnp.int32, (t, t), 0)
    col = lax.broadcasted_iota(jnp.int32, (t, t), 1)
    tril = jnp.where(col <= row, 1.0, 0.0).astype(BF16)
    x1, x2, x3 = _split3(lf_ref[0])
    cs = (jnp.dot(tril, x1, preferred_element_type=F32)
          + jnp.dot(tril, x2, preferred_element_type=F32)
          + jnp.dot(tril, x3, preferred_element_type=F32))
    cs = cs + carry_ref[0:1, :]
    c_ref[0] = cs * LOG2E
    carry_ref[...] = jnp.broadcast_to(cs[t - 1:t, :], carry_ref.shape)


def _cumsum(lf3):
    b, s, _ = lf3.shape
    t = min(CUMSUM_BLK, s)
    spec = pl.BlockSpec((1, t, LANES), lambda i, j: (i, j, 0))
    return pl.pallas_call(
        _cumsum_kernel, grid=(b, s // t),
        in_specs=[spec], out_specs=spec,
        out_shape=jax.ShapeDtypeStruct(lf3.shape, F32),
        scratch_shapes=[pltpu.VMEM((8, LANES), F32)],
        compiler_params=_params(2), name="gate_cumsum")(lf3)


def _head_lane_mask(hh):
    lane = lax.broadcasted_iota(jnp.int32, (1, LANES), 1)
    return (lane < HEAD_DIM) if hh == 0 else (lane >= HEAD_DIM)


def _value_rows(v2):
    v_t = v2.astype(F32).T
    row = lax.broadcasted_iota(jnp.int32, (V_ROWS - HEAD_DIM, v_t.shape[1]), 0)
    tail = jnp.where(row == 0, 1.0, 0.0)
    return [jnp.concatenate([v_t[hh * HEAD_DIM:(hh + 1) * HEAD_DIM], tail],
                            axis=0).astype(BF16) for hh in range(2)]


def _flash_init(m_ref, acc_ref):
    m_ref[...] = jnp.full(m_ref.shape, NEG_INF, F32)
    acc_ref[...] = jnp.zeros(acc_ref.shape, F32)


def _pipelined_blocks(n, last_block, scores_into, update):
    scores_into(0, 0)

    def pair(jj, carry):
        j = 2 * jj
        scores_into(jnp.minimum(j + 1, last_block), 1)
        update(j, 0)
        scores_into(jnp.minimum(j + 2, last_block), 0)
        update(j + 1, 1)
        return carry
    lax.fori_loop(0, n // 2, pair, 0)

    @pl.when(n % 2 == 1)
    def _():
        update(n - 1, 0)


def _flash_update(load_s, v_rows, m_ref, acc_ref, hh, keep=None, const=None):
    m_old = m_ref[hh]
    mx = jnp.max(load_s(), axis=0, keepdims=True)
    if const is not None:
        mx = mx + const
    if keep is not None:
        mx = jnp.where(keep, mx, NEG_INF)
    m_new = jnp.maximum(m_old, mx)
    m_safe = jnp.where(m_new == NEG_INF, 0.0, m_new)
    shift = m_safe if const is None else m_safe - const
    if keep is not None:
        shift = jnp.where(keep, shift, float("inf"))
    p = jnp.exp2(load_s() - shift).astype(BF16)
    alpha = jnp.exp2(m_old - m_safe)
    m_ref[hh] = m_new
    acc_ref[hh] = alpha * acc_ref[hh] + jnp.dot(v_rows, p, preferred_element_type=F32)


def _write_heads(o_ref, acc_ref):
    outs = []
    for hh in range(2):
        acc = acc_ref[hh]
        outs.append(acc[:HEAD_DIM] / acc[HEAD_DIM:HEAD_DIM + 1])
    o_t = jnp.concatenate(outs, axis=0)
    o_ref[0] = o_t.T.astype(o_ref.dtype)


def _fox_kernel(q_ref, k_ref, v_ref, cq_ref, ck_ref, o_ref,
                kaug_ref, vt_ref, s_ref, m_ref, acc_ref):
    hp = pl.program_id(1)
    qi = pl.program_id(2)
    blk = q_ref.shape[1]
    nblk = k_ref.shape[1] // blk
    lane = lax.broadcasted_iota(jnp.int32, (1, LANES), 1)
    prow = lax.broadcasted_iota(jnp.int32, (LANES, LANES), 0)
    pcol = lax.broadcasted_iota(jnp.int32, (LANES, LANES), 1)

    def place(parts, src_lane, first_dst_lane):
        out = None
        for i, part in enumerate(parts):
            sel = jnp.where((prow == src_lane) & (pcol == first_dst_lane + i),
                            1.0, 0.0).astype(BF16)
            term = jnp.dot(part, sel, preferred_element_type=F32)
            out = term if out is None else out + term
        return out

    @pl.when(qi == 0)
    def _():
        def build(jb, carry):
            rows = pl.ds(pl.multiple_of(jb * blk, blk), blk)
            k2 = k_ref[0, rows, :]
            parts = _split3(ck_ref[0, rows, :])
            for hh in range(2):
                base = HEAD_DIM * (1 - hh)
                aug = -place(parts, 2 * hp + hh, base + 3)
                aug = aug + jnp.where((lane >= base) & (lane < base + 3), 1.0, 0.0)
                kaug_ref[hh, rows, :] = jnp.where(_head_lane_mask(hh), k2, aug.astype(BF16))
            for hh, v_rows in enumerate(_value_rows(v_ref[0, rows, :])):
                vt_ref[hh, jb] = v_rows
            return carry
        lax.fori_loop(0, nblk, build, 0)

    q2 = q_ref[0]
    q_parts = _split3(cq_ref[0])
    krow = lax.broadcasted_iota(jnp.int32, (blk, blk), 0)
    qcol = lax.broadcasted_iota(jnp.int32, (blk, blk), 1)
    q_aug = []
    for hh in range(2):
        base = HEAD_DIM * (1 - hh)
        aug = place(q_parts, 2 * hp + hh, base)
        aug = aug + jnp.where((lane >= base + 3) & (lane < base + 6), 1.0, 0.0)
        q_aug.append(jnp.where(_head_lane_mask(hh), q2, aug.astype(BF16)))

    def scores(hh, j):
        rows = pl.ds(pl.multiple_of(j * blk, blk), blk)
        return lax.dot_general(kaug_ref[hh, rows, :], q_aug[hh], _NT,
                               preferred_element_type=F32)

    _flash_init(m_ref, acc_ref)
    for hh in range(2):
        s_ref[1, hh] = jnp.where(krow <= qcol, scores(hh, qi), NEG_INF)
        _flash_update(lambda hh=hh: s_ref[1, hh], vt_ref[hh, qi], m_ref, acc_ref, hh)

    def scores_into(j, slot):
        for hh in range(2):
            s_ref[slot, hh] = scores(hh, j)

    def update(j, slot):
        for hh in range(2):
            _flash_update(lambda hh=hh: s_ref[slot, hh], vt_ref[hh, j], m_ref, acc_ref, hh)

    _pipelined_blocks(qi, nblk - 1, scores_into, update)
    _write_heads(o_ref, acc_ref)


def _fox_attention(qkv3, c3, n_heads):
    b, s, d3 = qkv3.shape
    d = d3 // 3
    blk = min(FOX_BLK, s)
    npair = n_heads // 2
    tile = pl.BlockSpec((1, blk, LANES), lambda bi, hp, qi: (bi, qi, hp))
    return pl.pallas_call(
        _fox_kernel, grid=(b, npair, s // blk),
        in_specs=[
            tile,
            pl.BlockSpec((1, s, LANES), lambda bi, hp, qi: (bi, 0, npair + hp)),
            pl.BlockSpec((1, s, LANES), lambda bi, hp, qi: (bi, 0, 2 * npair + hp)),
            pl.BlockSpec((1, blk, LANES), lambda bi, hp, qi: (bi, qi, 0)),
            pl.BlockSpec((1, s, LANES), lambda bi, hp, qi: (bi, 0, 0)),
        ],
        out_specs=tile,
        out_shape=jax.ShapeDtypeStruct((b, s, d), BF16),
        scratch_shapes=[
            pltpu.VMEM((2, s, LANES), BF16),
            pltpu.VMEM((2, s // blk, V_ROWS, blk), BF16),
            pltpu.VMEM((2, 2, blk, blk), F32),
            pltpu.VMEM((2, 1, blk), F32),
            pltpu.VMEM((2, V_ROWS, blk), F32),
        ],
        compiler_params=_params(3), name="fox_attention")(qkv3, qkv3, qkv3, c3, c3)


def _t5_bucket_np(n):
    max_exact = REL_BUCKETS // 2
    nf = np.maximum(n, 1).astype(np.float64)
    large = max_exact + (np.log(nf / max_exact) / math.log(REL_MAX_DIST / max_exact)
                         * (REL_BUCKETS - max_exact)).astype(np.int32)
    return np.where(n < max_exact, n, np.minimum(large, REL_BUCKETS - 1)).astype(np.int32)


def _bucket_tiles(blk):
    key = np.arange(blk)[:, None]
    qry = np.arange(blk)[None, :]
    own = np.where(key <= qry, _t5_bucket_np(np.maximum(qry - key, 0)), -1)
    prev = _t5_bucket_np(blk + qry - key)
    return np.stack([own, prev]).astype(np.int32)


def _bias_kernel(tab_ref, bucket_ref, o_ref):
    h = pl.program_id(0)
    bucket = bucket_ref[...]
    acc = jnp.where(bucket < 0, NEG_INF, 0.0).astype(F32)
    for bkt in range(REL_BUCKETS):
        acc = jnp.where(bucket == bkt, tab_ref[bkt, h] * LOG2E, acc)
    o_ref[0] = acc


def _bias_tiles(rel_table, blk):
    n_heads = rel_table.shape[1]
    buckets = jnp.asarray(_bucket_tiles(blk))
    return pl.pallas_call(
        _bias_kernel, grid=(n_heads,),
        in_specs=[pl.BlockSpec(memory_space=pltpu.SMEM),
                  pl.BlockSpec((2, blk, blk), lambda h: (0, 0, 0))],
        out_specs=pl.BlockSpec((1, 2, blk, blk), lambda h: (h, 0, 0, 0)),
        out_shape=jax.ShapeDtypeStruct((n_heads, 2, blk, blk), F32),
        compiler_params=_params(1), name="t5_bias_tiles")(rel_table, buckets)


def _moba_kernel(tab_ref, q_ref, k_ref, v_ref, bias_ref, o_ref,
                 vt_ref, km_ref, sel_ref, s_ref, m_ref, acc_ref):
    hp = pl.program_id(1)
    qi = pl.program_id(2)
    blk = MOBA_BLOCK
    tq = q_ref.shape[1]
    nblk = k_ref.shape[1] // blk
    assert tq == 2 * blk and blk >= REL_MAX_DIST
    first_own = 2 * qi

    @pl.when(qi == 0)
    def _():
        def build(jb, carry):
            rows = pl.ds(pl.multiple_of(jb * blk, blk), blk)
            km_ref[pl.ds(jb, 1), :] = jnp.mean(k_ref[0, rows, :].astype(F32),
                                                axis=0, keepdims=True)
            for hh, v_rows in enumerate(_value_rows(v_ref[0, rows, :])):
                vt_ref[hh, jb] = v_rows
            return carry
        lax.fori_loop(0, nblk, build, 0)

    q2 = q_ref[0]
    km_parts = _split3(km_ref[...])
    blk_id = lax.broadcasted_iota(jnp.int32, (nblk, tq), 0)
    qpos = lax.broadcasted_iota(jnp.int32, (nblk, tq), 1)
    own = first_own + jnp.where(qpos >= blk, 1, 0)
    past = blk_id < own
    q_m = []
    for hh in range(2):
        q_m.append(jnp.where(_head_lane_mask(hh), q2, jnp.zeros_like(q2)))

        gate = None
        for part in km_parts:
            term = lax.dot_general(part, q_m[hh], _NT, preferred_element_type=F32)
            gate = term if gate is None else gate + term
        gate = jnp.where(past, gate, NEG_INF)
        rank = jnp.zeros(gate.shape, jnp.int32)
        for jp in range(nblk):
            other = gate[jp:jp + 1, :]
            beats = (other > gate) | ((other == gate) & (jp < blk_id))
            rank = rank + jnp.where(beats, 1, 0)
        keep_all = ((rank < MOBA_TOP_K) & past) | (blk_id == own) | (blk_id == first_own + 1)
        sel_ref[hh] = jnp.where(keep_all, 1.0, 0.0)

    def scores(hh, j):
        rows = pl.ds(pl.multiple_of(j * blk, blk), blk)
        return lax.dot_general(k_ref[0, rows, :], q_m[hh], _NT,
                               preferred_element_type=F32)

    def keep(hh, j):
        return sel_ref[hh, pl.ds(j, 1), :] > 0.5

    _flash_init(m_ref, acc_ref)
    far_bias = [tab_ref[REL_BUCKETS - 1, 2 * hp + hh] * LOG2E for hh in range(2)]

    for hh in range(2):
        own_t, prev_t = bias_ref[hh, 0], bias_ref[hh, 1]
        j = first_own + 1
        bias = jnp.concatenate([jnp.full((blk, blk), NEG_INF, F32), own_t], axis=1)
        s_ref[0, hh] = scores(hh, j) + bias
        _flash_update(lambda hh=hh: s_ref[0, hh], vt_ref[hh, j], m_ref, acc_ref, hh)
        j = first_own
        bias = jnp.concatenate([own_t, prev_t], axis=1)
        s_ref[1, hh] = scores(hh, j) + bias
        _flash_update(lambda hh=hh: s_ref[1, hh], vt_ref[hh, j], m_ref, acc_ref, hh,
                      keep=keep(hh, j))

    @pl.when(qi >= 1)
    def _():
        j = first_own - 1
        for hh in range(2):
            bias = jnp.concatenate(
                [bias_ref[hh, 1], jnp.full((blk, blk), far_bias[hh], F32)], axis=1)
            s_ref[0, hh] = scores(hh, j) + bias
            _flash_update(lambda hh=hh: s_ref[0, hh], vt_ref[hh, j], m_ref, acc_ref, hh,
                          keep=keep(hh, j))

    def scores_into(j, slot):
        for hh in range(2):
            s_ref[slot, hh] = scores(hh, j)

    def update(j, slot):
        for hh in range(2):
            _flash_update(lambda hh=hh: s_ref[slot, hh], vt_ref[hh, j], m_ref, acc_ref, hh,
                          keep=keep(hh, j), const=far_bias[hh])

    _pipelined_blocks(jnp.maximum(first_own - 1, 0), nblk - 1, scores_into, update)
    _write_heads(o_ref, acc_ref)


def _moba_attention(qkv3, rel_table, bias_t, n_heads):
    b, s, d3 = qkv3.shape
    d = d3 // 3
    blk = MOBA_BLOCK
    tq = MOBA_Q_TILE
    assert s % tq == 0
    npair = n_heads // 2
    nblk = s // blk
    tile = pl.BlockSpec((1, tq, LANES), lambda bi, hp, qi: (bi, qi, hp))
    return pl.pallas_call(
        _moba_kernel, grid=(b, npair, s // tq),
        in_specs=[
            pl.BlockSpec(memory_space=pltpu.SMEM),
            tile,
            pl.BlockSpec((1, s, LANES), lambda bi, hp, qi: (bi, 0, npair + hp)),
            pl.BlockSpec((1, s, LANES), lambda bi, hp, qi: (bi, 0, 2 * npair + hp)),
            pl.BlockSpec((2, 2, blk, blk), lambda bi, hp, qi: (hp, 0, 0, 0)),
        ],
        out_specs=tile,
        out_shape=jax.ShapeDtypeStruct((b, s, d), BF16),
        scratch_shapes=[
            pltpu.VMEM((2, nblk, V_ROWS, blk), BF16),
            pltpu.VMEM((nblk, LANES), F32),
            pltpu.VMEM((2, nblk, tq), F32),
            pltpu.VMEM((2, 2, blk, tq), F32),
            pltpu.VMEM((2, 1, tq), F32),
            pltpu.VMEM((2, V_ROWS, tq), F32),
        ],
        compiler_params=_params(3), name="moba_attention")(
            rel_table, qkv3, qkv3, qkv3, bias_t)


def _oproj_kernel(o_ref, w_ref, h_ref, out_ref):
    out_ref[...] = h_ref[...] + jnp.dot(o_ref[...], w_ref[...],
                                        preferred_element_type=F32)


def _out_project(o2, w, h2, *, tm, tn):
    n, d = h2.shape
    return pl.pallas_call(
        _oproj_kernel, grid=(n // tm, d // tn),
        in_specs=[pl.BlockSpec((tm, o2.shape[1]), lambda i, j: (i, 0)),
                  pl.BlockSpec((o2.shape[1], tn), lambda i, j: (0, j)),
                  pl.BlockSpec((tm, tn), lambda i, j: (i, j))],
        out_specs=pl.BlockSpec((tm, tn), lambda i, j: (i, j)),
        out_shape=jax.ShapeDtypeStruct((n, d), F32),
        compiler_params=_params(2), name="out_proj")(o2, w, h2)


def _ffn_kernel(x_ref, g_ref, wg_ref, wu_ref, wo_ref, out_ref, u_ref, acc_ref):
    c = pl.program_id(1)

    @pl.when(c == 0)
    def _():
        u_ref[...] = _rmsnorm(x_ref[...], g_ref[...]).astype(BF16)
        acc_ref[...] = jnp.zeros_like(acc_ref)

    u = u_ref[...]
    gate = jnp.dot(u, wg_ref[...], preferred_element_type=F32)
    up = jnp.dot(u, wu_ref[...], preferred_element_type=F32)
    act = (gate * jax.nn.sigmoid(gate) * up).astype(BF16)
    acc_ref[...] += jnp.dot(act, wo_ref[...], preferred_element_type=F32)

    @pl.when(c == pl.num_programs(1) - 1)
    def _():
        out_ref[...] = x_ref[...] + acc_ref[...]


def _ffn(h2, g, w_in, w_out, *, tm, tf):
    n, d = h2.shape
    d_ff = w_out.shape[0]
    nchunk = d_ff // tf
    return pl.pallas_call(
        _ffn_kernel, grid=(n // tm, nchunk),
        in_specs=[pl.BlockSpec((tm, d), lambda i, c: (i, 0)),
                  pl.BlockSpec((1, d), lambda i, c: (0, 0)),
                  pl.BlockSpec((d, tf), lambda i, c: (0, c)),
                  pl.BlockSpec((d, tf), lambda i, c: (0, nchunk + c)),
                  pl.BlockSpec((tf, d), lambda i, c: (c, 0))],
        out_specs=pl.BlockSpec((tm, d), lambda i, c: (i, 0)),
        out_shape=jax.ShapeDtypeStruct((n, d), F32),
        scratch_shapes=[pltpu.VMEM((tm, d), BF16), pltpu.VMEM((tm, d), F32)],
        compiler_params=_params(2), name="swiglu_ffn")(h2, g, w_in, w_in, w_out)


def _ple_kernel(x_ref, g_ref, wg_ref, p_ref, wu_ref, fg_ref, out_ref, *, final_norm):
    x = x_ref[...]
    u = _rmsnorm(x, g_ref[...]).astype(BF16)
    gate = jax.nn.sigmoid(jnp.dot(u, wg_ref[...], preferred_element_type=F32))
    up = jnp.dot(p_ref[...].astype(BF16), wu_ref[...], preferred_element_type=F32)
    y = x + gate * up
    if final_norm:
        y = _rmsnorm(y, fg_ref[...])
    out_ref[...] = y


def _ple(h2, g, w_gate, p2, w_up, final_g, *, tm, final_norm):
    n, d = h2.shape
    pd = p2.shape[1]
    return pl.pallas_call(
        functools.partial(_ple_kernel, final_norm=final_norm), grid=(n // tm,),
        in_specs=[pl.BlockSpec((tm, d), lambda i: (i, 0)),
                  pl.BlockSpec((1, d), lambda i: (0, 0)),
                  pl.BlockSpec((d, d), lambda i: (0, 0)),
                  pl.BlockSpec((tm, pd), lambda i: (i, 0)),
                  pl.BlockSpec((pd, d), lambda i: (0, 0)),
                  pl.BlockSpec((1, d), lambda i: (0, 0))],
        out_specs=pl.BlockSpec((tm, d), lambda i: (i, 0)),
        out_shape=jax.ShapeDtypeStruct((n, d), F32),
        compiler_params=_params(1), name="ple")(h2, g, w_gate, p2, w_up, final_g)


def _row_tile(n, want):
    t = min(want, n)
    assert n % t == 0
    return t


def _col_tile(n, want):
    t = min(want, n)
    while n % t:
        t -= LANES
    return t


def kernel(x, p, attn_norm_g, fox_w_in, fox_b_f, fox_w_o, moba_w_in, moba_w_o, rel_bias_table,
           ffn_norm_g, ffn_w_in, ffn_w_out, ple_norm_g, ple_w_gate, ple_w_up, final_norm_g):
    b, s, d = x.shape
    depth = p.shape[0]
    n_heads = rel_bias_table.shape[1]
    assert d == n_heads * HEAD_DIM and n_heads % 2 == 0 and n_heads <= LANES
    n = b * s
    tm = _row_tile(n, 1024)
    tm_ple = _row_tile(n, 512)
    tn = _col_tile(d, 512)
    tf = _col_tile(ffn_w_out.shape[1], 256)

    bias_t = _bias_tiles(rel_bias_table.astype(F32), MOBA_BLOCK)
    final_g = final_norm_g.reshape(1, d).astype(F32)
    col_scale = jnp.concatenate([jnp.full((d,), Q_SCALE, F32), jnp.ones((2 * d,), F32)])

    h = x.reshape(n, d).astype(F32)
    for i in range(depth):
        j = i // 2
        g_attn = attn_norm_g[i].reshape(1, d).astype(F32)
        if i % 2 == 0:
            w = fox_w_in[j]
            w_qkv = (w[:, :3 * d] * col_scale).astype(BF16)
            w_f = jnp.pad(w[:, 3 * d:], ((0, 0), (0, LANES - n_heads))).astype(BF16)
            b_f = jnp.pad(fox_b_f[j].astype(F32), (0, LANES - n_heads)).reshape(1, LANES)
            qkv, log_f = _project(h, g_attn, w_qkv, w_f, b_f, tm=tm, tn=_col_tile(3 * d, 512))
            c = _cumsum(log_f.reshape(b, s, LANES))
            o = _fox_attention(qkv.reshape(b, s, 3 * d), c, n_heads)
            w_o = fox_w_o[j]
        else:
            qkv = _project(h, g_attn, (moba_w_in[j] * col_scale).astype(BF16),
                           tm=tm, tn=_col_tile(3 * d, 512))
            o = _moba_attention(qkv.reshape(b, s, 3 * d), rel_bias_table.astype(F32),
                                bias_t, n_heads)
            w_o = moba_w_o[j]
        h = _out_project(o.reshape(n, d), w_o.astype(BF16), h, tm=tm, tn=tn)
        h = _ffn(h, ffn_norm_g[i].reshape(1, d).astype(F32), ffn_w_in[i].astype(BF16),
                 ffn_w_out[i].astype(BF16), tm=tm, tf=tf)
        h = _ple(h, ple_norm_g[i].reshape(1, d).astype(F32), ple_w_gate[i].astype(BF16),
                 p[i].reshape(n, -1), ple_w_up[i].astype(BF16), final_g,
                 tm=tm_ple, final_norm=(i == depth - 1))
    return h.reshape(b, s, d).astype(x.dtype)
```

```python
import functools
import math

import numpy as np
import jax
import jax.numpy as jnp
from jax import lax
from jax.experimental import pallas as pl
from jax.experimental.pallas import tpu as pltpu

F32 = jnp.float32
BF16 = jnp.bfloat16

RMS_EPS = 1e-6
HEAD_DIM = 64
MOBA_BLOCK = 256
MOBA_TOP_K = 3
REL_BUCKETS = 32
REL_MAX_DIST = 128

LANES = 128
BF16_SUBLANES = 16
V_ROWS = HEAD_DIM + BF16_SUBLANES
GATE_LANES = 8
LOG2E = math.log2(math.e)
Q_SCALE = HEAD_DIM ** -0.5 * LOG2E
FOX_BLK = 512
MOBA_Q_TILE = 2 * MOBA_BLOCK
CUMSUM_BLK = 256
VMEM_LIMIT_BYTES = 56 * 1024 * 1024
NEG_INF = float("-inf")

_NT = (((1,), (1,)), ((), ()))


def _params(n_axes):
    return pltpu.CompilerParams(
        dimension_semantics=("arbitrary",) * n_axes,
        vmem_limit_bytes=VMEM_LIMIT_BYTES)


def _split3(x):
    x1 = x.astype(BF16)
    r1 = x - x1.astype(F32)
    x2 = r1.astype(BF16)
    x3 = (r1 - x2.astype(F32)).astype(BF16)
    return x1, x2, x3


def _rmsnorm(x, g):
    ms = jnp.mean(x * x, axis=-1, keepdims=True)
    return x * lax.rsqrt(ms + RMS_EPS) * g


def _log_sigmoid(x):
    return jnp.minimum(x, 0.0) - jnp.log1p(jnp.exp(-jnp.abs(x)))


def _proj_kernel(x_ref, g_ref, w_ref, o_ref, u_ref):
    @pl.when(pl.program_id(1) == 0)
    def _():
        u_ref[...] = _rmsnorm(x_ref[...], g_ref[...]).astype(BF16)

    o_ref[...] = jnp.dot(u_ref[...], w_ref[...],
                         preferred_element_type=F32).astype(o_ref.dtype)


def _proj_gate_kernel(x_ref, g_ref, w_ref, wf_ref, bf_ref, o_ref, lf_ref, u_ref):
    @pl.when(pl.program_id(1) == 0)
    def _():
        u = _rmsnorm(x_ref[...], g_ref[...]).astype(BF16)
        u_ref[...] = u
        f_logit = jnp.dot(u, wf_ref[...], preferred_element_type=F32) + bf_ref[...]
        lf_ref[...] = _log_sigmoid(f_logit)

    o_ref[...] = jnp.dot(u_ref[...], w_ref[...],
                         preferred_element_type=F32).astype(o_ref.dtype)


def _project(h2, g, w, wf=None, bf=None, *, tm, tn):
    n, d = h2.shape
    nout = w.shape[1]
    grid = (n // tm, nout // tn)
    x_spec = pl.BlockSpec((tm, d), lambda i, j: (i, 0))
    g_spec = pl.BlockSpec((1, d), lambda i, j: (0, 0))
    w_spec = pl.BlockSpec((d, tn), lambda i, j: (0, j))
    o_spec = pl.BlockSpec((tm, tn), lambda i, j: (i, j))
    scratch = [pltpu.VMEM((tm, d), BF16)]
    if wf is None:
        return pl.pallas_call(
            _proj_kernel, grid=grid,
            in_specs=[x_spec, g_spec, w_spec], out_specs=o_spec,
            out_shape=jax.ShapeDtypeStruct((n, nout), BF16),
            scratch_shapes=scratch, compiler_params=_params(2),
            name="proj")(h2, g, w)
    return pl.pallas_call(
        _proj_gate_kernel, grid=grid,
        in_specs=[x_spec, g_spec, w_spec,
                  pl.BlockSpec((d, LANES), lambda i, j: (0, 0)),
                  pl.BlockSpec((1, LANES), lambda i, j: (0, 0))],
        out_specs=[o_spec, pl.BlockSpec((tm, LANES), lambda i, j: (i, 0))],
        out_shape=[jax.ShapeDtypeStruct((n, nout), BF16),
                   jax.ShapeDtypeStruct((n, LANES), F32)],
        scratch_shapes=scratch, compiler_params=_params(2),
        name="proj_gate")(h2, g, w, wf, bf)


def _cumsum_kernel(lf_ref, aq_ref, ak_ref, carry_ref):
    @pl.when(pl.program_id(1) == 0)
    def _():
        carry_ref[...] = jnp.zeros_like(carry_ref)

    t = lf_ref.shape[1]
    row = lax.broadcasted_iota(jnp.int32, (t, t), 0)
    col = lax.broadcasted_iota(jnp.int32, (t, t), 1)
    tril = jnp.where(col <= row, 1.0, 0.0).astype(BF16)
    x1, x2, x3 = _split3(lf_ref[0])
    cs = (jnp.dot(tril, x1, preferred_element_type=F32)
          + jnp.dot(tril, x2, preferred_element_type=F32)
          + jnp.dot(tril, x3, preferred_element_type=F32))
    cs = cs + carry_ref[0:1, :]
    carry_ref[...] = jnp.broadcast_to(cs[t - 1:t, :], carry_ref.shape)

    src = lax.broadcasted_iota(jnp.int32, (LANES, LANES), 0)
    dst = lax.broadcasted_iota(jnp.int32, (LANES, LANES), 1)
    lane = lax.broadcasted_iota(jnp.int32, (1, LANES), 1) & (GATE_LANES - 1)
    aq = jnp.where((lane >= 3) & (lane < 6), 1.0, 0.0)
    ak = jnp.where(lane < 3, 1.0, 0.0)
    for i, part in enumerate(_split3(cs * LOG2E)):
        to_q = jnp.where(dst == GATE_LANES * src + i, 1.0, 0.0).astype(BF16)
        to_k = jnp.where(dst == GATE_LANES * src + 3 + i, 1.0, 0.0).astype(BF16)
        aq = aq + jnp.dot(part, to_q, preferred_element_type=F32)
        ak = ak - jnp.dot(part, to_k, preferred_element_type=F32)
    aq_ref[0] = aq.astype(BF16)
    ak_ref[0] = ak.astype(BF16)


def _cumsum(lf3):
    b, s, _ = lf3.shape
    t = min(CUMSUM_BLK, s)
    spec = pl.BlockSpec((1, t, LANES), lambda i, j: (i, j, 0))
    out = jax.ShapeDtypeStruct(lf3.shape, BF16)
    return pl.pallas_call(
        _cumsum_kernel, grid=(b, s // t),
        in_specs=[spec], out_specs=[spec, spec], out_shape=[out, out],
        scratch_shapes=[pltpu.VMEM((8, LANES), F32)],
        compiler_params=_params(2), name="gate_cumsum")(lf3)


def _head_lane_mask(hh):
    lane = lax.broadcasted_iota(jnp.int32, (1, LANES), 1)
    return (lane < HEAD_DIM) if hh == 0 else (lane >= HEAD_DIM)


def _value_rows(v2):
    v_t = v2.astype(F32).T
    row = lax.broadcasted_iota(jnp.int32, (V_ROWS - HEAD_DIM, v_t.shape[1]), 0)
    tail = jnp.where(row == 0, 1.0, 0.0)
    return [jnp.concatenate([v_t[hh * HEAD_DIM:(hh + 1) * HEAD_DIM], tail],
                            axis=0).astype(BF16) for hh in range(2)]


def _flash_init(m_ref, acc_ref):
    m_ref[...] = jnp.full(m_ref.shape, NEG_INF, F32)
    acc_ref[...] = jnp.zeros(acc_ref.shape, F32)


def _pipelined_blocks(n, last_block, scores_into, update, final_update=None):
    scores_into(0, 0)

    def pair(jj, carry):
        j = 2 * jj
        scores_into(jnp.minimum(j + 1, last_block), 1)
        update(j, 0)
        scores_into(jnp.minimum(j + 2, last_block), 0)
        update(j + 1, 1)
        return carry
    lax.fori_loop(0, n // 2, pair, 0)

    @pl.when(n % 2 == 1)
    def _():
        if final_update is not None:
            scores_into(n, 1)
        update(n - 1, 0)
        if final_update is not None:
            final_update(1)

    if final_update is not None:
        @pl.when(n % 2 == 0)
        def _():
            final_update(0)


def _flash_update(load_s, v_rows, m_ref, acc_ref, hh, keep=None, const=None):
    m_old = m_ref[hh]
    mx = jnp.max(load_s(), axis=0, keepdims=True)
    if const is not None:
        mx = mx + const
    if keep is not None:
        mx = jnp.where(keep, mx, NEG_INF)
    m_new = jnp.maximum(m_old, mx)
    m_safe = jnp.where(m_new == NEG_INF, 0.0, m_new)
    shift = m_safe if const is None else m_safe - const
    if keep is not None:
        shift = jnp.where(keep, shift, float("inf"))
    p = jnp.exp2(load_s() - shift).astype(BF16)
    alpha = jnp.exp2(m_old - m_safe)
    m_ref[hh] = m_new
    if not isinstance(v_rows, (list, tuple)):
        v_rows = [v_rows]
    keys = p.shape[0] // len(v_rows)
    acc = alpha * acc_ref[hh]
    for i, v_i in enumerate(v_rows):
        acc = acc + jnp.dot(v_i, p[i * keys:(i + 1) * keys], preferred_element_type=F32)
    acc_ref[hh] = acc


def _write_heads(o_ref, acc_ref):
    outs = []
    for hh in range(2):
        acc = acc_ref[hh]
        outs.append(acc[:HEAD_DIM] / acc[HEAD_DIM:HEAD_DIM + 1])
    o_t = jnp.concatenate(outs, axis=0)
    o_ref[0] = o_t.T.astype(o_ref.dtype)


def _fox_kernel(q_ref, k_ref, v_ref, aq_ref, ak_ref, o_ref,
                kaug_ref, vt_ref, s_ref, m_ref, acc_ref):
    hp = pl.program_id(1)
    qi = pl.program_id(2)
    blk = q_ref.shape[1]
    nblk = k_ref.shape[1] // blk
    src = lax.broadcasted_iota(jnp.int32, (LANES, LANES), 0)
    dst = lax.broadcasted_iota(jnp.int32, (LANES, LANES), 1)

    def with_gate_lanes(x2, gate_lanes, hh):
        base = HEAD_DIM * (1 - hh)
        first = GATE_LANES * (2 * hp + hh)
        move = jnp.where((src >= first) & (src < first + GATE_LANES)
                         & (dst - base == src - first), 1.0, 0.0).astype(BF16)
        moved = jnp.dot(gate_lanes, move, preferred_element_type=F32).astype(BF16)
        return jnp.where(_head_lane_mask(hh), x2, moved)

    @pl.when(qi == 0)
    def _():
        def build(jb, carry):
            rows = pl.ds(pl.multiple_of(jb * blk, blk), blk)
            k2 = k_ref[0, rows, :]
            ak = ak_ref[0, rows, :]
            for hh in range(2):
                kaug_ref[hh, rows, :] = with_gate_lanes(k2, ak, hh)
            for hh, v_rows in enumerate(_value_rows(v_ref[0, rows, :])):
                vt_ref[hh, jb] = v_rows
            return carry
        lax.fori_loop(0, nblk, build, 0)

    q_aug = [with_gate_lanes(q_ref[0], aq_ref[0], hh) for hh in range(2)]
    krow = lax.broadcasted_iota(jnp.int32, (blk, blk), 0)
    qcol = lax.broadcasted_iota(jnp.int32, (blk, blk), 1)

    def scores_into(j, slot):
        rows = pl.ds(pl.multiple_of(j * blk, blk), blk)
        for hh in range(2):
            s_ref[slot, hh] = lax.dot_general(kaug_ref[hh, rows, :], q_aug[hh], _NT,
                                              preferred_element_type=F32)

    def update(j, slot):
        for hh in range(2):
            _flash_update(lambda hh=hh: s_ref[slot, hh], vt_ref[hh, j], m_ref, acc_ref, hh)

    def diagonal_update(slot):
        for hh in range(2):
            s_ref[slot, hh] = jnp.where(krow <= qcol, s_ref[slot, hh], NEG_INF)
        update(qi, slot)

    _flash_init(m_ref, acc_ref)
    _pipelined_blocks(qi, nblk - 1, scores_into, update, diagonal_update)
    _write_heads(o_ref, acc_ref)


def _fox_attention(qkv3, aq3, ak3, n_heads):
    b, s, d3 = qkv3.shape
    d = d3 // 3
    assert n_heads * GATE_LANES <= LANES
    blk = min(FOX_BLK, s)
    npair = n_heads // 2
    tile = pl.BlockSpec((1, blk, LANES), lambda bi, hp, qi: (bi, qi, hp))
    return pl.pallas_call(
        _fox_kernel, grid=(b, npair, s // blk),
        in_specs=[
            tile,
            pl.BlockSpec((1, s, LANES), lambda bi, hp, qi: (bi, 0, npair + hp)),
            pl.BlockSpec((1, s, LANES), lambda bi, hp, qi: (bi, 0, 2 * npair + hp)),
            pl.BlockSpec((1, blk, LANES), lambda bi, hp, qi: (bi, qi, 0)),
            pl.BlockSpec((1, s, LANES), lambda bi, hp, qi: (bi, 0, 0)),
        ],
        out_specs=tile,
        out_shape=jax.ShapeDtypeStruct((b, s, d), BF16),
        scratch_shapes=[
            pltpu.VMEM((2, s, LANES), BF16),
            pltpu.VMEM((2, s // blk, V_ROWS, blk), BF16),
            pltpu.VMEM((2, 2, blk, blk), F32),
            pltpu.VMEM((2, 1, blk), F32),
            pltpu.VMEM((2, V_ROWS, blk), F32),
        ],
        compiler_params=_params(3), name="fox_attention")(qkv3, qkv3, qkv3, aq3, ak3)


def _t5_bucket_np(n):
    max_exact = REL_BUCKETS // 2
    nf = np.maximum(n, 1).astype(np.float64)
    large = max_exact + (np.log(nf / max_exact) / math.log(REL_MAX_DIST / max_exact)
                         * (REL_BUCKETS - max_exact)).astype(np.int32)
    return np.where(n < max_exact, n, np.minimum(large, REL_BUCKETS - 1)).astype(np.int32)


def _bucket_tiles(blk):
    key = np.arange(blk)[:, None]
    qry = np.arange(blk)[None, :]
    own = np.where(key <= qry, _t5_bucket_np(np.maximum(qry - key, 0)), -1)
    prev = _t5_bucket_np(blk + qry - key)
    return np.stack([own, prev]).astype(np.int32)


def _bias_kernel(tab_ref, bucket_ref, o_ref):
    h = pl.program_id(0)
    bucket = bucket_ref[...]
    acc = jnp.where(bucket < 0, NEG_INF, 0.0).astype(F32)
    for bkt in range(REL_BUCKETS):
        acc = jnp.where(bucket == bkt, tab_ref[bkt, h] * LOG2E, acc)
    o_ref[0] = acc


def _bias_tiles(rel_table, blk):
    n_heads = rel_table.shape[1]
    buckets = jnp.asarray(_bucket_tiles(blk))
    return pl.pallas_call(
        _bias_kernel, grid=(n_heads,),
        in_specs=[pl.BlockSpec(memory_space=pltpu.SMEM),
                  pl.BlockSpec((2, blk, blk), lambda h: (0, 0, 0))],
        out_specs=pl.BlockSpec((1, 2, blk, blk), lambda h: (h, 0, 0, 0)),
        out_shape=jax.ShapeDtypeStruct((n_heads, 2, blk, blk), F32),
        compiler_params=_params(1), name="t5_bias_tiles")(rel_table, buckets)


def _moba_kernel(tab_ref, q_ref, k_ref, v_ref, bias_ref, o_ref,
                 vt_ref, km_ref, sel_ref, s_ref, near_ref, m_ref, acc_ref):
    hp = pl.program_id(1)
    qi = pl.program_id(2)
    blk = MOBA_BLOCK
    tq = q_ref.shape[1]
    nblk = k_ref.shape[1] // blk
    assert tq == 2 * blk and blk >= REL_MAX_DIST
    first_own = 2 * qi

    @pl.when(qi == 0)
    def _():
        def build(jb, carry):
            rows = pl.ds(pl.multiple_of(jb * blk, blk), blk)
            km_ref[pl.ds(jb, 1), :] = jnp.mean(k_ref[0, rows, :].astype(F32),
                                                axis=0, keepdims=True)
            for hh, v_rows in enumerate(_value_rows(v_ref[0, rows, :])):
                vt_ref[hh, jb] = v_rows
            return carry
        lax.fori_loop(0, nblk, build, 0)

    q2 = q_ref[0]
    km_parts = _split3(km_ref[...])
    blk_id = lax.broadcasted_iota(jnp.int32, (nblk, tq), 0)
    qpos = lax.broadcasted_iota(jnp.int32, (nblk, tq), 1)
    own = first_own + jnp.where(qpos >= blk, 1, 0)
    past = blk_id < own
    q_m = []
    for hh in range(2):
        q_m.append(jnp.where(_head_lane_mask(hh), q2, jnp.zeros_like(q2)))

        gate = None
        for part in km_parts:
            term = lax.dot_general(part, q_m[hh], _NT, preferred_element_type=F32)
            gate = term if gate is None else gate + term
        gate = jnp.where(past, gate, NEG_INF)
        rank = jnp.zeros(gate.shape, jnp.int32)
        for jp in range(nblk):
            other = gate[jp:jp + 1, :]
            beats = (other > gate) | ((other == gate) & (jp < blk_id))
            rank = rank + jnp.where(beats, 1, 0)
        keep_all = ((rank < MOBA_TOP_K) & past) | (blk_id == own) | (blk_id == first_own + 1)
        sel_ref[hh] = jnp.where(keep_all, 1.0, 0.0)

    def scores(hh, j):
        rows = pl.ds(pl.multiple_of(j * blk, blk), blk)
        return lax.dot_general(k_ref[0, rows, :], q_m[hh], _NT,
                               preferred_element_type=F32)

    def keep(hh, j):
        return sel_ref[hh, pl.ds(j, 1), :] > 0.5

    _flash_init(m_ref, acc_ref)
    far_bias = [tab_ref[REL_BUCKETS - 1, 2 * hp + hh] * LOG2E for hh in range(2)]

    j_prev = jnp.maximum(first_own - 1, 0)
    neg_tile = jnp.full((blk, blk), NEG_INF, F32)
    for hh in range(2):
        own_t, prev_t = bias_ref[hh, 0], bias_ref[hh, 1]
        mask_prev = jnp.where(keep(hh, j_prev) & (qi >= 1), 0.0, NEG_INF)
        mask_own = jnp.where(keep(hh, first_own), 0.0, NEG_INF)
        far_t = jnp.full((blk, blk), far_bias[hh], F32)
        near_ref[hh, 0:blk] = (scores(hh, j_prev) + mask_prev
                               + jnp.concatenate([prev_t, far_t], axis=1))
        near_ref[hh, blk:2 * blk] = (scores(hh, first_own) + mask_own
                                     + jnp.concatenate([own_t, prev_t], axis=1))
        near_ref[hh, 2 * blk:3 * blk] = (scores(hh, first_own + 1)
                                         + jnp.concatenate([neg_tile, own_t], axis=1))
        _flash_update(lambda hh=hh: near_ref[hh],
                      [vt_ref[hh, j_prev], vt_ref[hh, first_own], vt_ref[hh, first_own + 1]],
                      m_ref, acc_ref, hh)

    def scores_into(j, slot):
        for hh in range(2):
            s_ref[slot, hh] = scores(hh, j)

    def update(j, slot):
        for hh in range(2):
            _flash_update(lambda hh=hh: s_ref[slot, hh], vt_ref[hh, j], m_ref, acc_ref, hh,
                          keep=keep(hh, j), const=far_bias[hh])

    _pipelined_blocks(jnp.maximum(first_own - 1, 0), nblk - 1, scores_into, update)
    _write_heads(o_ref, acc_ref)


def _moba_attention(qkv3, rel_table, bias_t, n_heads):
    b, s, d3 = qkv3.shape
    d = d3 // 3
    blk = MOBA_BLOCK
    tq = MOBA_Q_TILE
    assert s % tq == 0
    npair = n_heads // 2
    nblk = s // blk
    tile = pl.BlockSpec((1, tq, LANES), lambda bi, hp, qi: (bi, qi, hp))
    return pl.pallas_call(
        _moba_kernel, grid=(b, npair, s // tq),
        in_specs=[
            pl.BlockSpec(memory_space=pltpu.SMEM),
            tile,
            pl.BlockSpec((1, s, LANES), lambda bi, hp, qi: (bi, 0, npair + hp)),
            pl.BlockSpec((1, s, LANES), lambda bi, hp, qi: (bi, 0, 2 * npair + hp)),
            pl.BlockSpec((2, 2, blk, blk), lambda bi, hp, qi: (hp, 0, 0, 0)),
        ],
        out_specs=tile,
        out_shape=jax.ShapeDtypeStruct((b, s, d), BF16),
        scratch_shapes=[
            pltpu.VMEM((2, nblk, V_ROWS, blk), BF16),
            pltpu.VMEM((nblk, LANES), F32),
            pltpu.VMEM((2, nblk, tq), F32),
            pltpu.VMEM((2, 2, blk, tq), F32),
            pltpu.VMEM((2, 3 * blk, tq), F32),
            pltpu.VMEM((2, 1, tq), F32),
            pltpu.VMEM((2, V_ROWS, tq), F32),
        ],
        compiler_params=_params(3), name="moba_attention")(
            rel_table, qkv3, qkv3, qkv3, bias_t)


def _oproj_kernel(o_ref, w_ref, h_ref, out_ref):
    out_ref[...] = h_ref[...] + jnp.dot(o_ref[...], w_ref[...],
                                        preferred_element_type=F32)


def _out_project(o2, w, h2, *, tm, tn):
    n, d = h2.shape
    return pl.pallas_call(
        _oproj_kernel, grid=(n // tm, d // tn),
        in_specs=[pl.BlockSpec((tm, o2.shape[1]), lambda i, j: (i, 0)),
                  pl.BlockSpec((o2.shape[1], tn), lambda i, j: (0, j)),
                  pl.BlockSpec((tm, tn), lambda i, j: (i, j))],
        out_specs=pl.BlockSpec((tm, tn), lambda i, j: (i, j)),
        out_shape=jax.ShapeDtypeStruct((n, d), F32),
        compiler_params=_params(2), name="out_proj")(o2, w, h2)


def _ffn_kernel(x_ref, g_ref, wg_ref, wu_ref, wo_ref, out_ref, u_ref, acc_ref):
    c = pl.program_id(1)

    @pl.when(c == 0)
    def _():
        u_ref[...] = _rmsnorm(x_ref[...], g_ref[...]).astype(BF16)
        acc_ref[...] = jnp.zeros_like(acc_ref)

    u = u_ref[...]
    gate = jnp.dot(u, wg_ref[...], preferred_element_type=F32)
    up = jnp.dot(u, wu_ref[...], preferred_element_type=F32)
    act = (gate * jax.nn.sigmoid(gate) * up).astype(BF16)
    acc_ref[...] += jnp.dot(act, wo_ref[...], preferred_element_type=F32)

    @pl.when(c == pl.num_programs(1) - 1)
    def _():
        out_ref[...] = x_ref[...] + acc_ref[...]


def _ffn(h2, g, w_in, w_out, *, tm, tf):
    n, d = h2.shape
    d_ff = w_out.shape[0]
    nchunk = d_ff // tf
    return pl.pallas_call(
        _ffn_kernel, grid=(n // tm, nchunk),
        in_specs=[pl.BlockSpec((tm, d), lambda i, c: (i, 0)),
                  pl.BlockSpec((1, d), lambda i, c: (0, 0)),
                  pl.BlockSpec((d, tf), lambda i, c: (0, c)),
                  pl.BlockSpec((d, tf), lambda i, c: (0, nchunk + c)),
                  pl.BlockSpec((tf, d), lambda i, c: (c, 0))],
        out_specs=pl.BlockSpec((tm, d), lambda i, c: (i, 0)),
        out_shape=jax.ShapeDtypeStruct((n, d), F32),
        scratch_shapes=[pltpu.VMEM((tm, d), BF16), pltpu.VMEM((tm, d), F32)],
        compiler_params=_params(2), name="swiglu_ffn")(h2, g, w_in, w_in, w_out)


def _ple_kernel(x_ref, g_ref, wg_ref, p_ref, wu_ref, fg_ref, out_ref, *, final_norm):
    x = x_ref[...]
    u = _rmsnorm(x, g_ref[...]).astype(BF16)
    gate = jax.nn.sigmoid(jnp.dot(u, wg_ref[...], preferred_element_type=F32))
    up = jnp.dot(p_ref[...].astype(BF16), wu_ref[...], preferred_element_type=F32)
    y = x + gate * up
    if final_norm:
        y = _rmsnorm(y, fg_ref[...])
    out_ref[...] = y


def _ple(h2, g, w_gate, p2, w_up, final_g, *, tm, final_norm):
    n, d = h2.shape
    pd = p2.shape[1]
    return pl.pallas_call(
        functools.partial(_ple_kernel, final_norm=final_norm), grid=(n // tm,),
        in_specs=[pl.BlockSpec((tm, d), lambda i: (i, 0)),
                  pl.BlockSpec((1, d), lambda i: (0, 0)),
                  pl.BlockSpec((d, d), lambda i: (0, 0)),
                  pl.BlockSpec((tm, pd), lambda i: (i, 0)),
                  pl.BlockSpec((pd, d), lambda i: (0, 0)),
                  pl.BlockSpec((1, d), lambda i: (0, 0))],
        out_specs=pl.BlockSpec((tm, d), lambda i: (i, 0)),
        out_shape=jax.ShapeDtypeStruct((n, d), F32),
        compiler_params=_params(1), name="ple")(h2, g, w_gate, p2, w_up, final_g)


def _row_tile(n, want):
    t = min(want, n)
    assert n % t == 0
    return t


def _col_tile(n, want):
    t = min(want, n)
    while n % t:
        t -= LANES
    return t


def kernel(x, p, attn_norm_g, fox_w_in, fox_b_f, fox_w_o, moba_w_in, moba_w_o, rel_bias_table,
           ffn_norm_g, ffn_w_in, ffn_w_out, ple_norm_g, ple_w_gate, ple_w_up, final_norm_g):
    b, s, d = x.shape
    depth = p.shape[0]
    n_heads = rel_bias_table.shape[1]
    assert d == n_heads * HEAD_DIM and n_heads % 2 == 0 and n_heads <= LANES
    n = b * s
    tm = _row_tile(n, 1024)
    tm_ple = _row_tile(n, 512)
    tn = _col_tile(d, 512)
    tf = _col_tile(ffn_w_out.shape[1], 256)

    bias_t = _bias_tiles(rel_bias_table.astype(F32), MOBA_BLOCK)
    final_g = final_norm_g.reshape(1, d).astype(F32)
    col_scale = jnp.concatenate([jnp.full((d,), Q_SCALE, F32), jnp.ones((2 * d,), F32)])

    h = x.reshape(n, d).astype(F32)
    for i in range(depth):
        j = i // 2
        g_attn = attn_norm_g[i].reshape(1, d).astype(F32)
        if i % 2 == 0:
            w = fox_w_in[j]
            w_qkv = (w[:, :3 * d] * col_scale).astype(BF16)
            w_f = jnp.pad(w[:, 3 * d:], ((0, 0), (0, LANES - n_heads))).astype(BF16)
            b_f = jnp.pad(fox_b_f[j].astype(F32), (0, LANES - n_heads)).reshape(1, LANES)
            qkv, log_f = _project(h, g_attn, w_qkv, w_f, b_f, tm=tm, tn=_col_tile(3 * d, 512))
            aq, ak = _cumsum(log_f.reshape(b, s, LANES))
            o = _fox_attention(qkv.reshape(b, s, 3 * d), aq, ak, n_heads)
            w_o = fox_w_o[j]
        else:
            qkv = _project(h, g_attn, (moba_w_in[j] * col_scale).astype(BF16),
                           tm=tm, tn=_col_tile(3 * d, 512))
            o = _moba_attention(qkv.reshape(b, s, 3 * d), rel_bias_table.astype(F32),
                                bias_t, n_heads)
            w_o = moba_w_o[j]
        h = _out_project(o.reshape(n, d), w_o.astype(BF16), h, tm=tm, tn=tn)
        h = _ffn(h, ffn_norm_g[i].reshape(1, d).astype(F32), ffn_w_in[i].astype(BF16),
                 ffn_w_out[i].astype(BF16), tm=tm, tf=tf)
        h = _ple(h, ple_norm_g[i].reshape(1, d).astype(F32), ple_w_gate[i].astype(BF16),
                 p[i].reshape(n, -1), ple_w_up[i].astype(BF16), final_g,
                 tm=tm_ple, final_norm=(i == depth - 1))
    return h.reshape(b, s, d).astype(x.dtype)
```

```python
import functools
import math
from typing import NamedTuple, Optional

import numpy as np
import jax
import jax.numpy as jnp
from jax import lax
from jax.experimental import pallas as pl
from jax.experimental.pallas import tpu as pltpu

F32 = jnp.float32
BF16 = jnp.bfloat16

RMS_EPS = 1e-6
HEAD_DIM = 64
MOBA_BLOCK = 256
MOBA_TOP_K = 3
REL_BUCKETS = 32
REL_MAX_DIST = 128

LANES = 128
BF16_SUBLANES = 16
V_ROWS = HEAD_DIM + BF16_SUBLANES
GATE_LANES = 8
LOG2E = math.log2(math.e)
Q_SCALE = HEAD_DIM ** -0.5 * LOG2E
FOX_BLK = 512
MOBA_Q_TILE = 2 * MOBA_BLOCK
CUMSUM_BLK = 256
DENSE_ROW_TILE = 512
PROJ_COL_CHUNK = 512
FFN_COL_CHUNK = 256
VMEM_LIMIT_BYTES = 56 * 1024 * 1024
NEG_INF = float("-inf")

_NT = (((1,), (1,)), ((), ()))


def _params(n_axes):
    return pltpu.CompilerParams(
        dimension_semantics=("arbitrary",) * n_axes,
        vmem_limit_bytes=VMEM_LIMIT_BYTES)


def _split3(x):
    x1 = x.astype(BF16)
    r1 = x - x1.astype(F32)
    x2 = r1.astype(BF16)
    x3 = (r1 - x2.astype(F32)).astype(BF16)
    return x1, x2, x3


def _rmsnorm(x, g):
    ms = jnp.mean(x * x, axis=-1, keepdims=True)
    return x * lax.rsqrt(ms + RMS_EPS) * g


def _log_sigmoid(x):
    return jnp.minimum(x, 0.0) - jnp.log1p(jnp.exp(-jnp.abs(x)))


def _resident(shape):
    return pl.BlockSpec(shape, lambda i: (0,) * len(shape), pipeline_mode=pl.Buffered(1))


class _Mixer(NamedTuple):
    g: jax.Array
    w_qkv: jax.Array
    w_f: Optional[jax.Array] = None
    b_f: Optional[jax.Array] = None

    def operands(self):
        return tuple(a for a in self if a is not None)

    def in_specs(self):
        return [_resident(a.shape) for a in self.operands()]

    def out_specs(self, tm):
        specs = [pl.BlockSpec((tm, self.w_qkv.shape[1]), lambda i: (i, 0))]
        if self.w_f is not None:
            specs.append(pl.BlockSpec((tm, LANES), lambda i: (i, 0)))
        return specs

    def out_shapes(self, n):
        shapes = [jax.ShapeDtypeStruct((n, self.w_qkv.shape[1]), BF16)]
        if self.w_f is not None:
            shapes.append(jax.ShapeDtypeStruct((n, LANES), F32))
        return shapes


def _mixer_inputs(y, mixer_refs, out_refs):
    g_ref, w_ref = mixer_refs[:2]
    u = _rmsnorm(y, g_ref[...]).astype(BF16)
    nout = w_ref.shape[1]
    tn = PROJ_COL_CHUNK if nout % PROJ_COL_CHUNK == 0 else nout
    for c in range(nout // tn):
        cols = slice(c * tn, (c + 1) * tn)
        out_refs[0][:, cols] = jnp.dot(u, w_ref[:, cols],
                                       preferred_element_type=F32).astype(BF16)
    if len(mixer_refs) > 2:
        wf_ref, bf_ref = mixer_refs[2:]
        f_logit = jnp.dot(u, wf_ref[...], preferred_element_type=F32) + bf_ref[...]
        out_refs[1][...] = _log_sigmoid(f_logit)


def _proj_kernel(x_ref, *refs, n_mixer):
    _mixer_inputs(x_ref[...], refs[:n_mixer], refs[n_mixer:])


def _project(h2, mixer, *, tm):
    n, d = h2.shape
    n_mixer = len(mixer.operands())
    return pl.pallas_call(
        functools.partial(_proj_kernel, n_mixer=n_mixer), grid=(n // tm,),
        in_specs=[pl.BlockSpec((tm, d), lambda i: (i, 0))] + mixer.in_specs(),
        out_specs=mixer.out_specs(tm), out_shape=mixer.out_shapes(n),
        compiler_params=_params(1), name="proj")(h2, *mixer.operands())


def _cumsum_kernel(lf_ref, aq_ref, ak_ref, carry_ref):
    @pl.when(pl.program_id(1) == 0)
    def _():
        carry_ref[...] = jnp.zeros_like(carry_ref)

    t = lf_ref.shape[1]
    row = lax.broadcasted_iota(jnp.int32, (t, t), 0)
    col = lax.broadcasted_iota(jnp.int32, (t, t), 1)
    tril = jnp.where(col <= row, 1.0, 0.0).astype(BF16)
    x1, x2, x3 = _split3(lf_ref[0])
    cs = (jnp.dot(tril, x1, preferred_element_type=F32)
          + jnp.dot(tril, x2, preferred_element_type=F32)
          + jnp.dot(tril, x3, preferred_element_type=F32))
    cs = cs + carry_ref[0:1, :]
    carry_ref[...] = jnp.broadcast_to(cs[t - 1:t, :], carry_ref.shape)

    src = lax.broadcasted_iota(jnp.int32, (LANES, LANES), 0)
    dst = lax.broadcasted_iota(jnp.int32, (LANES, LANES), 1)
    lane = lax.broadcasted_iota(jnp.int32, (1, LANES), 1) & (GATE_LANES - 1)
    aq = jnp.where((lane >= 3) & (lane < 6), 1.0, 0.0)
    ak = jnp.where(lane < 3, 1.0, 0.0)
    for i, part in enumerate(_split3(cs * LOG2E)):
        to_q = jnp.where(dst == GATE_LANES * src + i, 1.0, 0.0).astype(BF16)
        to_k = jnp.where(dst == GATE_LANES * src + 3 + i, 1.0, 0.0).astype(BF16)
        aq = aq + jnp.dot(part, to_q, preferred_element_type=F32)
        ak = ak - jnp.dot(part, to_k, preferred_element_type=F32)
    aq_ref[0] = aq.astype(BF16)
    ak_ref[0] = ak.astype(BF16)


def _cumsum(lf3):
    b, s, _ = lf3.shape
    t = min(CUMSUM_BLK, s)
    spec = pl.BlockSpec((1, t, LANES), lambda i, j: (i, j, 0))
    out = jax.ShapeDtypeStruct(lf3.shape, BF16)
    return pl.pallas_call(
        _cumsum_kernel, grid=(b, s // t),
        in_specs=[spec], out_specs=[spec, spec], out_shape=[out, out],
        scratch_shapes=[pltpu.VMEM((8, LANES), F32)],
        compiler_params=_params(2), name="gate_cumsum")(lf3)


def _head_lane_mask(hh):
    lane = lax.broadcasted_iota(jnp.int32, (1, LANES), 1)
    return (lane < HEAD_DIM) if hh == 0 else (lane >= HEAD_DIM)


def _value_rows(v2):
    v_t = v2.astype(F32).T
    row = lax.broadcasted_iota(jnp.int32, (V_ROWS - HEAD_DIM, v_t.shape[1]), 0)
    tail = jnp.where(row == 0, 1.0, 0.0)
    return [jnp.concatenate([v_t[hh * HEAD_DIM:(hh + 1) * HEAD_DIM], tail],
                            axis=0).astype(BF16) for hh in range(2)]


def _flash_init(m_ref, acc_ref):
    m_ref[...] = jnp.full(m_ref.shape, NEG_INF, F32)
    acc_ref[...] = jnp.zeros(acc_ref.shape, F32)


def _pipelined_blocks(n, last_block, scores_into, update, final_update=None):
    scores_into(0, 0)

    def pair(jj, carry):
        j = 2 * jj
        scores_into(jnp.minimum(j + 1, last_block), 1)
        update(j, 0)
        scores_into(jnp.minimum(j + 2, last_block), 0)
        update(j + 1, 1)
        return carry
    lax.fori_loop(0, n // 2, pair, 0)

    @pl.when(n % 2 == 1)
    def _():
        if final_update is not None:
            scores_into(n, 1)
        update(n - 1, 0)
        if final_update is not None:
            final_update(1)

    if final_update is not None:
        @pl.when(n % 2 == 0)
        def _():
            final_update(0)


def _flash_update(load_s, v_rows, m_ref, acc_ref, hh, keep=None, const=None):
    m_old = m_ref[hh]
    mx = jnp.max(load_s(), axis=0, keepdims=True)
    if const is not None:
        mx = mx + const
    if keep is not None:
        mx = jnp.where(keep, mx, NEG_INF)
    m_new = jnp.maximum(m_old, mx)
    m_safe = jnp.where(m_new == NEG_INF, 0.0, m_new)
    shift = m_safe if const is None else m_safe - const
    if keep is not None:
        shift = jnp.where(keep, shift, float("inf"))
    p = jnp.exp2(load_s() - shift).astype(BF16)
    alpha = jnp.exp2(m_old - m_safe)
    m_ref[hh] = m_new
    if not isinstance(v_rows, (list, tuple)):
        v_rows = [v_rows]
    keys = p.shape[0] // len(v_rows)
    acc = alpha * acc_ref[hh]
    for i, v_i in enumerate(v_rows):
        acc = acc + jnp.dot(v_i, p[i * keys:(i + 1) * keys], preferred_element_type=F32)
    acc_ref[hh] = acc


def _write_heads(o_ref, acc_ref):
    outs = []
    for hh in range(2):
        acc = acc_ref[hh]
        outs.append(acc[:HEAD_DIM] / acc[HEAD_DIM:HEAD_DIM + 1])
    o_t = jnp.concatenate(outs, axis=0)
    o_ref[0] = o_t.T.astype(o_ref.dtype)


def _fox_kernel(q_ref, k_ref, v_ref, aq_ref, ak_ref, o_ref,
                kaug_ref, vt_ref, s_ref, m_ref, acc_ref):
    hp = pl.program_id(1)
    qi = pl.program_id(2)
    blk = q_ref.shape[1]
    nblk = k_ref.shape[1] // blk
    src = lax.broadcasted_iota(jnp.int32, (LANES, LANES), 0)
    dst = lax.broadcasted_iota(jnp.int32, (LANES, LANES), 1)

    def with_gate_lanes(x2, gate_lanes, hh):
        base = HEAD_DIM * (1 - hh)
        first = GATE_LANES * (2 * hp + hh)
        move = jnp.where((src >= first) & (src < first + GATE_LANES)
                         & (dst - base == src - first), 1.0, 0.0).astype(BF16)
        moved = jnp.dot(gate_lanes, move, preferred_element_type=F32).astype(BF16)
        return jnp.where(_head_lane_mask(hh), x2, moved)

    @pl.when(qi == 0)
    def _():
        def build(jb, carry):
            rows = pl.ds(pl.multiple_of(jb * blk, blk), blk)
            k2 = k_ref[0, rows, :]
            ak = ak_ref[0, rows, :]
            for hh in range(2):
                kaug_ref[hh, rows, :] = with_gate_lanes(k2, ak, hh)
            for hh, v_rows in enumerate(_value_rows(v_ref[0, rows, :])):
                vt_ref[hh, jb] = v_rows
            return carry
        lax.fori_loop(0, nblk, build, 0)

    q_aug = [with_gate_lanes(q_ref[0], aq_ref[0], hh) for hh in range(2)]
    krow = lax.broadcasted_iota(jnp.int32, (blk, blk), 0)
    qcol = lax.broadcasted_iota(jnp.int32, (blk, blk), 1)

    def scores_into(j, slot):
        rows = pl.ds(pl.multiple_of(j * blk, blk), blk)
        for hh in range(2):
            s_ref[slot, hh] = lax.dot_general(kaug_ref[hh, rows, :], q_aug[hh], _NT,
                                              preferred_element_type=F32)

    def update(j, slot):
        for hh in range(2):
            _flash_update(lambda hh=hh: s_ref[slot, hh], vt_ref[hh, j], m_ref, acc_ref, hh)

    def diagonal_update(slot):
        for hh in range(2):
            s_ref[slot, hh] = jnp.where(krow <= qcol, s_ref[slot, hh], NEG_INF)
        update(qi, slot)

    _flash_init(m_ref, acc_ref)
    _pipelined_blocks(qi, nblk - 1, scores_into, update, diagonal_update)
    _write_heads(o_ref, acc_ref)


def _fox_attention(qkv3, aq3, ak3, n_heads):
    b, s, d3 = qkv3.shape
    d = d3 // 3
    assert n_heads * GATE_LANES <= LANES
    blk = min(FOX_BLK, s)
    npair = n_heads // 2
    tile = pl.BlockSpec((1, blk, LANES), lambda bi, hp, qi: (bi, qi, hp))
    return pl.pallas_call(
        _fox_kernel, grid=(b, npair, s // blk),
        in_specs=[
            tile,
            pl.BlockSpec((1, s, LANES), lambda bi, hp, qi: (bi, 0, npair + hp)),
            pl.BlockSpec((1, s, LANES), lambda bi, hp, qi: (bi, 0, 2 * npair + hp)),
            pl.BlockSpec((1, blk, LANES), lambda bi, hp, qi: (bi, qi, 0)),
            pl.BlockSpec((1, s, LANES), lambda bi, hp, qi: (bi, 0, 0)),
        ],
        out_specs=tile,
        out_shape=jax.ShapeDtypeStruct((b, s, d), BF16),
        scratch_shapes=[
            pltpu.VMEM((2, s, LANES), BF16),
            pltpu.VMEM((2, s // blk, V_ROWS, blk), BF16),
            pltpu.VMEM((2, 2, blk, blk), F32),
            pltpu.VMEM((2, 1, blk), F32),
            pltpu.VMEM((2, V_ROWS, blk), F32),
        ],
        compiler_params=_params(3), name="fox_attention")(qkv3, qkv3, qkv3, aq3, ak3)


def _t5_bucket_np(n):
    max_exact = REL_BUCKETS // 2
    nf = np.maximum(n, 1).astype(np.float64)
    large = max_exact + (np.log(nf / max_exact) / math.log(REL_MAX_DIST / max_exact)
                         * (REL_BUCKETS - max_exact)).astype(np.int32)
    return np.where(n < max_exact, n, np.minimum(large, REL_BUCKETS - 1)).astype(np.int32)


def _bucket_tiles(blk):
    key = np.arange(blk)[:, None]
    qry = np.arange(blk)[None, :]
    own = np.where(key <= qry, _t5_bucket_np(np.maximum(qry - key, 0)), -1)
    prev = _t5_bucket_np(blk + qry - key)
    return np.stack([own, prev]).astype(np.int32)


def _bias_kernel(tab_ref, bucket_ref, o_ref):
    h = pl.program_id(0)
    bucket = bucket_ref[...]
    acc = jnp.where(bucket < 0, NEG_INF, 0.0).astype(F32)
    for bkt in range(REL_BUCKETS):
        acc = jnp.where(bucket == bkt, tab_ref[bkt, h] * LOG2E, acc)
    o_ref[0] = acc


def _bias_tiles(rel_table, blk):
    n_heads = rel_table.shape[1]
    buckets = jnp.asarray(_bucket_tiles(blk))
    return pl.pallas_call(
        _bias_kernel, grid=(n_heads,),
        in_specs=[pl.BlockSpec(memory_space=pltpu.SMEM),
                  pl.BlockSpec((2, blk, blk), lambda h: (0, 0, 0))],
        out_specs=pl.BlockSpec((1, 2, blk, blk), lambda h: (h, 0, 0, 0)),
        out_shape=jax.ShapeDtypeStruct((n_heads, 2, blk, blk), F32),
        compiler_params=_params(1), name="t5_bias_tiles")(rel_table, buckets)


def _moba_kernel(tab_ref, q_ref, k_ref, v_ref, bias_ref, o_ref,
                 vt_ref, km_ref, sel_ref, s_ref, near_ref, m_ref, acc_ref):
    hp = pl.program_id(1)
    qi = pl.program_id(2)
    blk = MOBA_BLOCK
    tq = q_ref.shape[1]
    nblk = k_ref.shape[1] // blk
    assert tq == 2 * blk and blk >= REL_MAX_DIST
    first_own = 2 * qi

    @pl.when(qi == 0)
    def _():
        def build(jb, carry):
            rows = pl.ds(pl.multiple_of(jb * blk, blk), blk)
            km_ref[pl.ds(jb, 1), :] = jnp.mean(k_ref[0, rows, :].astype(F32),
                                                axis=0, keepdims=True)
            for hh, v_rows in enumerate(_value_rows(v_ref[0, rows, :])):
                vt_ref[hh, jb] = v_rows
            return carry
        lax.fori_loop(0, nblk, build, 0)

    q2 = q_ref[0]
    km_parts = _split3(km_ref[...])
    blk_id = lax.broadcasted_iota(jnp.int32, (nblk, tq), 0)
    qpos = lax.broadcasted_iota(jnp.int32, (nblk, tq), 1)
    own = first_own + jnp.where(qpos >= blk, 1, 0)
    past = blk_id < own
    q_m = []
    for hh in range(2):
        q_m.append(jnp.where(_head_lane_mask(hh), q2, jnp.zeros_like(q2)))

        gate = None
        for part in km_parts:
            term = lax.dot_general(part, q_m[hh], _NT, preferred_element_type=F32)
            gate = term if gate is None else gate + term
        gate = jnp.where(past, gate, NEG_INF)
        rank = jnp.zeros(gate.shape, jnp.int32)
        for jp in range(nblk):
            other = gate[jp:jp + 1, :]
            beats = (other > gate) | ((other == gate) & (jp < blk_id))
            rank = rank + jnp.where(beats, 1, 0)
        keep_all = ((rank < MOBA_TOP_K) & past) | (blk_id == own) | (blk_id == first_own + 1)
        sel_ref[hh] = jnp.where(keep_all, 1.0, 0.0)

    def scores(hh, j):
        rows = pl.ds(pl.multiple_of(j * blk, blk), blk)
        return lax.dot_general(k_ref[0, rows, :], q_m[hh], _NT,
                               preferred_element_type=F32)

    def keep(hh, j):
        return sel_ref[hh, pl.ds(j, 1), :] > 0.5

    _flash_init(m_ref, acc_ref)
    far_bias = [tab_ref[REL_BUCKETS - 1, 2 * hp + hh] * LOG2E for hh in range(2)]

    j_prev = jnp.maximum(first_own - 1, 0)
    neg_tile = jnp.full((blk, blk), NEG_INF, F32)
    for hh in range(2):
        own_t, prev_t = bias_ref[hh, 0], bias_ref[hh, 1]
        mask_prev = jnp.where(keep(hh, j_prev) & (qi >= 1), 0.0, NEG_INF)
        mask_own = jnp.where(keep(hh, first_own), 0.0, NEG_INF)
        far_t = jnp.full((blk, blk), far_bias[hh], F32)
        near_ref[hh, 0:blk] = (scores(hh, j_prev) + mask_prev
                               + jnp.concatenate([prev_t, far_t], axis=1))
        near_ref[hh, blk:2 * blk] = (scores(hh, first_own) + mask_own
                                     + jnp.concatenate([own_t, prev_t], axis=1))
        near_ref[hh, 2 * blk:3 * blk] = (scores(hh, first_own + 1)
                                         + jnp.concatenate([neg_tile, own_t], axis=1))
    for hh in range(2):
        _flash_update(lambda hh=hh: near_ref[hh],
                      [vt_ref[hh, j_prev], vt_ref[hh, first_own], vt_ref[hh, first_own + 1]],
                      m_ref, acc_ref, hh)

    def scores_into(j, slot):
        for hh in range(2):
            s_ref[slot, hh] = scores(hh, j)

    def update(j, slot):
        for hh in range(2):
            _flash_update(lambda hh=hh: s_ref[slot, hh], vt_ref[hh, j], m_ref, acc_ref, hh,
                          keep=keep(hh, j), const=far_bias[hh])

    _pipelined_blocks(jnp.maximum(first_own - 1, 0), nblk - 1, scores_into, update)
    _write_heads(o_ref, acc_ref)


def _moba_attention(qkv3, rel_table, bias_t, n_heads):
    b, s, d3 = qkv3.shape
    d = d3 // 3
    blk = MOBA_BLOCK
    tq = MOBA_Q_TILE
    assert s % tq == 0
    npair = n_heads // 2
    nblk = s // blk
    tile = pl.BlockSpec((1, tq, LANES), lambda bi, hp, qi: (bi, qi, hp))
    return pl.pallas_call(
        _moba_kernel, grid=(b, npair, s // tq),
        in_specs=[
            pl.BlockSpec(memory_space=pltpu.SMEM),
            tile,
            pl.BlockSpec((1, s, LANES), lambda bi, hp, qi: (bi, 0, npair + hp)),
            pl.BlockSpec((1, s, LANES), lambda bi, hp, qi: (bi, 0, 2 * npair + hp)),
            pl.BlockSpec((2, 2, blk, blk), lambda bi, hp, qi: (hp, 0, 0, 0)),
        ],
        out_specs=tile,
        out_shape=jax.ShapeDtypeStruct((b, s, d), BF16),
        scratch_shapes=[
            pltpu.VMEM((2, nblk, V_ROWS, blk), BF16),
            pltpu.VMEM((nblk, LANES), F32),
            pltpu.VMEM((2, nblk, tq), F32),
            pltpu.VMEM((2, 2, blk, tq), F32),
            pltpu.VMEM((2, 3 * blk, tq), F32),
            pltpu.VMEM((2, 1, tq), F32),
            pltpu.VMEM((2, V_ROWS, tq), F32),
        ],
        compiler_params=_params(3), name="moba_attention")(
            rel_table, qkv3, qkv3, qkv3, bias_t)


def _oproj_ffn_kernel(h_ref, o_ref, wo_ref, g_ref, win_ref, wout_ref, out_ref, *, tf):
    d_ff = wout_ref.shape[0]
    h1 = h_ref[...] + jnp.dot(o_ref[...], wo_ref[...], preferred_element_type=F32)
    u = _rmsnorm(h1, g_ref[...]).astype(BF16)
    acc = h1
    for c in range(d_ff // tf):
        gate = jnp.dot(u, win_ref[:, c * tf:(c + 1) * tf], preferred_element_type=F32)
        up = jnp.dot(u, win_ref[:, d_ff + c * tf:d_ff + (c + 1) * tf],
                     preferred_element_type=F32)
        act = (gate * jax.nn.sigmoid(gate) * up).astype(BF16)
        acc = acc + jnp.dot(act, wout_ref[c * tf:(c + 1) * tf, :],
                            preferred_element_type=F32)
    out_ref[...] = acc


def _oproj_ffn(h2, o2, w_o, g, w_in, w_out, *, tm, tf):
    n, d = h2.shape
    d_ff = w_out.shape[0]
    assert d_ff % tf == 0
    row = pl.BlockSpec((tm, d), lambda i: (i, 0))
    return pl.pallas_call(
        functools.partial(_oproj_ffn_kernel, tf=tf), grid=(n // tm,),
        in_specs=[row, row, _resident((d, d)), _resident((1, d)),
                  _resident((d, 2 * d_ff)), _resident((d_ff, d))],
        out_specs=row,
        out_shape=jax.ShapeDtypeStruct((n, d), F32),
        compiler_params=_params(1), name="oproj_ffn")(h2, o2, w_o, g, w_in, w_out)


def _ple_update(x_ref, g_ref, wg_ref, p_ref, wu_ref):
    x = x_ref[...]
    u = _rmsnorm(x, g_ref[...]).astype(BF16)
    gate = jax.nn.sigmoid(jnp.dot(u, wg_ref[...], preferred_element_type=F32))
    up = jnp.dot(p_ref[...].astype(BF16), wu_ref[...], preferred_element_type=F32)
    return x + gate * up


def _ple_next_kernel(x_ref, g_ref, wg_ref, p_ref, wu_ref, *refs, n_mixer):
    y = _ple_update(x_ref, g_ref, wg_ref, p_ref, wu_ref)
    refs[n_mixer][...] = y
    _mixer_inputs(y, refs[:n_mixer], refs[n_mixer + 1:])


def _ple_final_kernel(x_ref, g_ref, wg_ref, p_ref, wu_ref, fg_ref, out_ref):
    y = _ple_update(x_ref, g_ref, wg_ref, p_ref, wu_ref)
    out_ref[...] = _rmsnorm(y, fg_ref[...])


def _ple(h2, g, w_gate, p2, w_up, *, tm, mixer=None, final_g=None):
    n, d = h2.shape
    pd = p2.shape[1]
    row = pl.BlockSpec((tm, d), lambda i: (i, 0))
    in_specs = [row, _resident((1, d)), _resident((d, d)),
                pl.BlockSpec((tm, pd), lambda i: (i, 0)), _resident((pd, d))]
    h_shape = jax.ShapeDtypeStruct((n, d), F32)
    if mixer is None:
        return pl.pallas_call(
            _ple_final_kernel, grid=(n // tm,),
            in_specs=in_specs + [_resident((1, d))], out_specs=row, out_shape=h_shape,
            compiler_params=_params(1), name="ple_final")(h2, g, w_gate, p2, w_up, final_g)
    n_mixer = len(mixer.operands())
    return pl.pallas_call(
        functools.partial(_ple_next_kernel, n_mixer=n_mixer), grid=(n // tm,),
        in_specs=in_specs + mixer.in_specs(),
        out_specs=[row] + mixer.out_specs(tm), out_shape=[h_shape] + mixer.out_shapes(n),
        compiler_params=_params(1), name="ple_proj")(h2, g, w_gate, p2, w_up,
                                                     *mixer.operands())


def _row_tile(n, want):
    t = min(want, n)
    assert n % t == 0
    return t


def _col_tile(n, want):
    t = min(want, n)
    while n % t:
        t -= LANES
    return t


def kernel(x, p, attn_norm_g, fox_w_in, fox_b_f, fox_w_o, moba_w_in, moba_w_o, rel_bias_table,
           ffn_norm_g, ffn_w_in, ffn_w_out, ple_norm_g, ple_w_gate, ple_w_up, final_norm_g):
    b, s, d = x.shape
    depth = p.shape[0]
    n_heads = rel_bias_table.shape[1]
    assert d == n_heads * HEAD_DIM and n_heads % 2 == 0 and n_heads <= LANES
    n = b * s
    tm = _row_tile(n, DENSE_ROW_TILE)
    tf = _col_tile(ffn_w_out.shape[1], FFN_COL_CHUNK)

    def row_vec(v):
        return v.reshape(1, -1).astype(F32)

    col_scale = jnp.concatenate([jnp.full((d,), Q_SCALE, F32), jnp.ones((2 * d,), F32)])

    def mixer(i):
        if i % 2 == 0:
            w = fox_w_in[i // 2]
            return _Mixer(
                row_vec(attn_norm_g[i]), (w[:, :3 * d] * col_scale).astype(BF16),
                jnp.pad(w[:, 3 * d:], ((0, 0), (0, LANES - n_heads))).astype(BF16),
                jnp.pad(row_vec(fox_b_f[i // 2]), ((0, 0), (0, LANES - n_heads))))
        return _Mixer(row_vec(attn_norm_g[i]), (moba_w_in[i // 2] * col_scale).astype(BF16))

    rel_table = rel_bias_table.astype(F32)
    bias_t = _bias_tiles(rel_table, MOBA_BLOCK)

    h = x.reshape(n, d).astype(F32)
    mixed = _project(h, mixer(0), tm=tm)
    for i in range(depth):
        qkv3 = mixed[0].reshape(b, s, 3 * d)
        if i % 2 == 0:
            aq, ak = _cumsum(mixed[1].reshape(b, s, LANES))
            o = _fox_attention(qkv3, aq, ak, n_heads)
            w_o = fox_w_o[i // 2]
        else:
            o = _moba_attention(qkv3, rel_table, bias_t, n_heads)
            w_o = moba_w_o[i // 2]
        h = _oproj_ffn(h, o.reshape(n, d), w_o.astype(BF16), row_vec(ffn_norm_g[i]),
                       ffn_w_in[i].astype(BF16), ffn_w_out[i].astype(BF16), tm=tm, tf=tf)
        ple_args = (h, row_vec(ple_norm_g[i]), ple_w_gate[i].astype(BF16),
                    p[i].reshape(n, -1), ple_w_up[i].astype(BF16))
        if i + 1 < depth:
            h, *mixed = _ple(*ple_args, tm=tm, mixer=mixer(i + 1))
        else:
            h = _ple(*ple_args, tm=tm, final_g=row_vec(final_norm_g))
    return h.reshape(b, s, d).astype(x.dtype)
```

```python
import functools
import math
from typing import NamedTuple, Optional

import numpy as np
import jax
import jax.numpy as jnp
from jax import lax
from jax.experimental import pallas as pl
from jax.experimental.pallas import tpu as pltpu

F32 = jnp.float32
BF16 = jnp.bfloat16

RMS_EPS = 1e-6
HEAD_DIM = 64
MOBA_BLOCK = 256
MOBA_TOP_K = 3
REL_BUCKETS = 32
REL_MAX_DIST = 128

LANES = 128
BF16_SUBLANES = 16
V_ROWS = HEAD_DIM + BF16_SUBLANES
GATE_LANES = 8
LOG2E = math.log2(math.e)
Q_SCALE = HEAD_DIM ** -0.5 * LOG2E
FOX_BLK = 512
MOBA_Q_TILE = 2 * MOBA_BLOCK
CUMSUM_BLK = 256
DENSE_ROW_TILE = 512
PROJ_COL_CHUNK = 512
FFN_COL_CHUNK = 256
VMEM_LIMIT_BYTES = 56 * 1024 * 1024
NEG_INF = float("-inf")

_NT = (((1,), (1,)), ((), ()))


def _params(n_axes):
    return pltpu.CompilerParams(
        dimension_semantics=("arbitrary",) * n_axes,
        vmem_limit_bytes=VMEM_LIMIT_BYTES)


def _split3(x):
    x1 = x.astype(BF16)
    r1 = x - x1.astype(F32)
    x2 = r1.astype(BF16)
    x3 = (r1 - x2.astype(F32)).astype(BF16)
    return x1, x2, x3


def _rmsnorm(x, g):
    ms = jnp.mean(x * x, axis=-1, keepdims=True)
    return x * lax.rsqrt(ms + RMS_EPS) * g


def _log_sigmoid(x):
    return jnp.minimum(x, 0.0) - jnp.log1p(jnp.exp(-jnp.abs(x)))


def _resident(shape):
    return pl.BlockSpec(shape, lambda i: (0,) * len(shape), pipeline_mode=pl.Buffered(1))


class _Mixer(NamedTuple):
    g: jax.Array
    w_qkv: jax.Array
    w_f: Optional[jax.Array] = None
    b_f: Optional[jax.Array] = None

    def operands(self):
        return tuple(a for a in self if a is not None)

    def in_specs(self):
        return [_resident(a.shape) for a in self.operands()]

    def out_specs(self, tm):
        specs = [pl.BlockSpec((tm, self.w_qkv.shape[1]), lambda i: (i, 0))]
        if self.w_f is not None:
            specs.append(pl.BlockSpec((tm, LANES), lambda i: (i, 0)))
        return specs

    def out_shapes(self, n):
        shapes = [jax.ShapeDtypeStruct((n, self.w_qkv.shape[1]), BF16)]
        if self.w_f is not None:
            shapes.append(jax.ShapeDtypeStruct((n, LANES), F32))
        return shapes


def _mixer_inputs(y, mixer_refs, out_refs):
    g_ref, w_ref = mixer_refs[:2]
    u = _rmsnorm(y, g_ref[...]).astype(BF16)
    nout = w_ref.shape[1]
    tn = PROJ_COL_CHUNK if nout % PROJ_COL_CHUNK == 0 else nout
    for c in range(nout // tn):
        cols = slice(c * tn, (c + 1) * tn)
        out_refs[0][:, cols] = jnp.dot(u, w_ref[:, cols],
                                       preferred_element_type=F32).astype(BF16)
    if len(mixer_refs) > 2:
        wf_ref, bf_ref = mixer_refs[2:]
        f_logit = jnp.dot(u, wf_ref[...], preferred_element_type=F32) + bf_ref[...]
        out_refs[1][...] = _log_sigmoid(f_logit)


def _proj_kernel(x_ref, *refs, n_mixer):
    _mixer_inputs(x_ref[...], refs[:n_mixer], refs[n_mixer:])


def _project(h2, mixer, *, tm):
    n, d = h2.shape
    n_mixer = len(mixer.operands())
    return pl.pallas_call(
        functools.partial(_proj_kernel, n_mixer=n_mixer), grid=(n // tm,),
        in_specs=[pl.BlockSpec((tm, d), lambda i: (i, 0))] + mixer.in_specs(),
        out_specs=mixer.out_specs(tm), out_shape=mixer.out_shapes(n),
        compiler_params=_params(1), name="proj")(h2, *mixer.operands())


def _cumsum_kernel(lf_ref, aq_ref, ak_ref, carry_ref):
    @pl.when(pl.program_id(1) == 0)
    def _():
        carry_ref[...] = jnp.zeros_like(carry_ref)

    t = lf_ref.shape[1]
    row = lax.broadcasted_iota(jnp.int32, (t, t), 0)
    col = lax.broadcasted_iota(jnp.int32, (t, t), 1)
    tril = jnp.where(col <= row, 1.0, 0.0).astype(BF16)
    x1, x2, x3 = _split3(lf_ref[0])
    cs = (jnp.dot(tril, x1, preferred_element_type=F32)
          + jnp.dot(tril, x2, preferred_element_type=F32)
          + jnp.dot(tril, x3, preferred_element_type=F32))
    cs = cs + carry_ref[0:1, :]
    carry_ref[...] = jnp.broadcast_to(cs[t - 1:t, :], carry_ref.shape)

    src = lax.broadcasted_iota(jnp.int32, (LANES, LANES), 0)
    dst = lax.broadcasted_iota(jnp.int32, (LANES, LANES), 1)
    lane = lax.broadcasted_iota(jnp.int32, (1, LANES), 1) & (GATE_LANES - 1)
    aq = jnp.where((lane >= 3) & (lane < 6), 1.0, 0.0)
    ak = jnp.where(lane < 3, 1.0, 0.0)
    for i, part in enumerate(_split3(cs * LOG2E)):
        to_q = jnp.where(dst == GATE_LANES * src + i, 1.0, 0.0).astype(BF16)
        to_k = jnp.where(dst == GATE_LANES * src + 3 + i, 1.0, 0.0).astype(BF16)
        aq = aq + jnp.dot(part, to_q, preferred_element_type=F32)
        ak = ak - jnp.dot(part, to_k, preferred_element_type=F32)
    aq_ref[0] = aq.astype(BF16)
    ak_ref[0] = ak.astype(BF16)


def _cumsum(lf3):
    b, s, _ = lf3.shape
    t = min(CUMSUM_BLK, s)
    spec = pl.BlockSpec((1, t, LANES), lambda i, j: (i, j, 0))
    out = jax.ShapeDtypeStruct(lf3.shape, BF16)
    return pl.pallas_call(
        _cumsum_kernel, grid=(b, s // t),
        in_specs=[spec], out_specs=[spec, spec], out_shape=[out, out],
        scratch_shapes=[pltpu.VMEM((8, LANES), F32)],
        compiler_params=_params(2), name="gate_cumsum")(lf3)


def _head_lane_mask(hh):
    lane = lax.broadcasted_iota(jnp.int32, (1, LANES), 1)
    return (lane < HEAD_DIM) if hh == 0 else (lane >= HEAD_DIM)


def _value_rows(v2):
    v_t = v2.astype(F32).T
    row = lax.broadcasted_iota(jnp.int32, (V_ROWS - HEAD_DIM, v_t.shape[1]), 0)
    tail = jnp.where(row == 0, 1.0, 0.0)
    return [jnp.concatenate([v_t[hh * HEAD_DIM:(hh + 1) * HEAD_DIM], tail],
                            axis=0).astype(BF16) for hh in range(2)]


def _flash_init(m_ref, acc_ref):
    m_ref[...] = jnp.full(m_ref.shape, NEG_INF, F32)
    acc_ref[...] = jnp.zeros(acc_ref.shape, F32)


def _pipelined_blocks(n, last_block, scores_into, update, final_update=None):
    scores_into(0, 0)

    def pair(jj, carry):
        j = 2 * jj
        scores_into(jnp.minimum(j + 1, last_block), 1)
        update(j, 0)
        scores_into(jnp.minimum(j + 2, last_block), 0)
        update(j + 1, 1)
        return carry
    lax.fori_loop(0, n // 2, pair, 0)

    @pl.when(n % 2 == 1)
    def _():
        if final_update is not None:
            scores_into(n, 1)
        update(n - 1, 0)
        if final_update is not None:
            final_update(1)

    if final_update is not None:
        @pl.when(n % 2 == 0)
        def _():
            final_update(0)


def _store_scores(s_t, s_ref, mx_ref, idx):
    s_ref[idx] = s_t
    mx_ref[idx] = jnp.max(s_t, axis=0, keepdims=True)


def _flash_update(s_t, mx, v_rows, m_ref, acc_ref, hh, keep=None, const=None):
    m_old = m_ref[hh]
    if const is not None:
        mx = mx + const
    if keep is not None:
        mx = jnp.where(keep, mx, NEG_INF)
    m_new = jnp.maximum(m_old, mx)
    m_safe = jnp.where(m_new == NEG_INF, 0.0, m_new)
    shift = m_safe if const is None else m_safe - const
    if keep is not None:
        shift = jnp.where(keep, shift, float("inf"))
    p = jnp.exp2(s_t - shift).astype(BF16)
    alpha = jnp.exp2(m_old - m_safe)
    m_ref[hh] = m_new
    if not isinstance(v_rows, (list, tuple)):
        v_rows = [v_rows]
    keys = p.shape[0] // len(v_rows)
    acc = alpha * acc_ref[hh]
    for i, v_i in enumerate(v_rows):
        acc = acc + jnp.dot(v_i, p[i * keys:(i + 1) * keys], preferred_element_type=F32)
    acc_ref[hh] = acc


def _write_heads(o_ref, acc_ref):
    outs = []
    for hh in range(2):
        acc = acc_ref[hh]
        outs.append(acc[:HEAD_DIM] / acc[HEAD_DIM:HEAD_DIM + 1])
    o_t = jnp.concatenate(outs, axis=0)
    o_ref[0] = o_t.T.astype(o_ref.dtype)


def _fox_kernel(q_ref, k_ref, v_ref, aq_ref, ak_ref, o_ref,
                kaug_ref, vt_ref, s_ref, mx_ref, m_ref, acc_ref):
    hp = pl.program_id(1)
    qi = pl.program_id(2)
    blk = q_ref.shape[1]
    nblk = k_ref.shape[1] // blk
    src = lax.broadcasted_iota(jnp.int32, (LANES, LANES), 0)
    dst = lax.broadcasted_iota(jnp.int32, (LANES, LANES), 1)

    def with_gate_lanes(x2, gate_lanes, hh):
        base = HEAD_DIM * (1 - hh)
        first = GATE_LANES * (2 * hp + hh)
        move = jnp.where((src >= first) & (src < first + GATE_LANES)
                         & (dst - base == src - first), 1.0, 0.0).astype(BF16)
        moved = jnp.dot(gate_lanes, move, preferred_element_type=F32).astype(BF16)
        return jnp.where(_head_lane_mask(hh), x2, moved)

    @pl.when(qi == 0)
    def _():
        def build(jb, carry):
            rows = pl.ds(pl.multiple_of(jb * blk, blk), blk)
            k2 = k_ref[0, rows, :]
            ak = ak_ref[0, rows, :]
            for hh in range(2):
                kaug_ref[hh, rows, :] = with_gate_lanes(k2, ak, hh)
            for hh, v_rows in enumerate(_value_rows(v_ref[0, rows, :])):
                vt_ref[hh, jb] = v_rows
            return carry
        lax.fori_loop(0, nblk, build, 0)

    q_aug = [with_gate_lanes(q_ref[0], aq_ref[0], hh) for hh in range(2)]
    krow = lax.broadcasted_iota(jnp.int32, (blk, blk), 0)
    qcol = lax.broadcasted_iota(jnp.int32, (blk, blk), 1)

    def scores_into(j, slot):
        rows = pl.ds(pl.multiple_of(j * blk, blk), blk)
        for hh in range(2):
            s_t = lax.dot_general(kaug_ref[hh, rows, :], q_aug[hh], _NT,
                                  preferred_element_type=F32)
            _store_scores(s_t, s_ref, mx_ref, (slot, hh))

    def update(j, slot):
        for hh in range(2):
            _flash_update(s_ref[slot, hh], mx_ref[slot, hh], vt_ref[hh, j],
                          m_ref, acc_ref, hh)

    def diagonal_update(slot):
        for hh in range(2):
            _store_scores(jnp.where(krow <= qcol, s_ref[slot, hh], NEG_INF),
                          s_ref, mx_ref, (slot, hh))
        update(qi, slot)

    _flash_init(m_ref, acc_ref)
    _pipelined_blocks(qi, nblk - 1, scores_into, update, diagonal_update)
    _write_heads(o_ref, acc_ref)


def _fox_attention(qkv3, aq3, ak3, n_heads):
    b, s, d3 = qkv3.shape
    d = d3 // 3
    assert n_heads * GATE_LANES <= LANES
    blk = min(FOX_BLK, s)
    npair = n_heads // 2
    tile = pl.BlockSpec((1, blk, LANES), lambda bi, hp, qi: (bi, qi, hp))
    return pl.pallas_call(
        _fox_kernel, grid=(b, npair, s // blk),
        in_specs=[
            tile,
            pl.BlockSpec((1, s, LANES), lambda bi, hp, qi: (bi, 0, npair + hp)),
            pl.BlockSpec((1, s, LANES), lambda bi, hp, qi: (bi, 0, 2 * npair + hp)),
            pl.BlockSpec((1, blk, LANES), lambda bi, hp, qi: (bi, qi, 0)),
            pl.BlockSpec((1, s, LANES), lambda bi, hp, qi: (bi, 0, 0)),
        ],
        out_specs=tile,
        out_shape=jax.ShapeDtypeStruct((b, s, d), BF16),
        scratch_shapes=[
            pltpu.VMEM((2, s, LANES), BF16),
            pltpu.VMEM((2, s // blk, V_ROWS, blk), BF16),
            pltpu.VMEM((2, 2, blk, blk), F32),
            pltpu.VMEM((2, 2, 1, blk), F32),
            pltpu.VMEM((2, 1, blk), F32),
            pltpu.VMEM((2, V_ROWS, blk), F32),
        ],
        compiler_params=_params(3), name="fox_attention")(qkv3, qkv3, qkv3, aq3, ak3)


def _t5_bucket_np(n):
    max_exact = REL_BUCKETS // 2
    nf = np.maximum(n, 1).astype(np.float64)
    large = max_exact + (np.log(nf / max_exact) / math.log(REL_MAX_DIST / max_exact)
                         * (REL_BUCKETS - max_exact)).astype(np.int32)
    return np.where(n < max_exact, n, np.minimum(large, REL_BUCKETS - 1)).astype(np.int32)


def _bucket_tiles(blk):
    key = np.arange(blk)[:, None]
    qry = np.arange(blk)[None, :]
    own = np.where(key <= qry, _t5_bucket_np(np.maximum(qry - key, 0)), -1)
    prev = _t5_bucket_np(blk + qry - key)
    return np.stack([own, prev]).astype(np.int32)


def _bias_kernel(tab_ref, bucket_ref, o_ref):
    h = pl.program_id(0)
    bucket = bucket_ref[...]
    acc = jnp.where(bucket < 0, NEG_INF, 0.0).astype(F32)
    for bkt in range(REL_BUCKETS):
        acc = jnp.where(bucket == bkt, tab_ref[bkt, h] * LOG2E, acc)
    o_ref[0] = acc


def _bias_tiles(rel_table, blk):
    n_heads = rel_table.shape[1]
    buckets = jnp.asarray(_bucket_tiles(blk))
    return pl.pallas_call(
        _bias_kernel, grid=(n_heads,),
        in_specs=[pl.BlockSpec(memory_space=pltpu.SMEM),
                  pl.BlockSpec((2, blk, blk), lambda h: (0, 0, 0))],
        out_specs=pl.BlockSpec((1, 2, blk, blk), lambda h: (h, 0, 0, 0)),
        out_shape=jax.ShapeDtypeStruct((n_heads, 2, blk, blk), F32),
        compiler_params=_params(1), name="t5_bias_tiles")(rel_table, buckets)


def _moba_kernel(tab_ref, q_ref, k_ref, v_ref, bias_ref, o_ref,
                 vt_ref, km_ref, sel_ref, s_ref, mx_ref, near_ref, m_ref, acc_ref):
    hp = pl.program_id(1)
    qi = pl.program_id(2)
    blk = MOBA_BLOCK
    tq = q_ref.shape[1]
    nblk = k_ref.shape[1] // blk
    assert tq == 2 * blk and blk >= REL_MAX_DIST
    first_own = 2 * qi

    @pl.when(qi == 0)
    def _():
        def build(jb, carry):
            rows = pl.ds(pl.multiple_of(jb * blk, blk), blk)
            km_ref[pl.ds(jb, 1), :] = jnp.mean(k_ref[0, rows, :].astype(F32),
                                                axis=0, keepdims=True)
            for hh, v_rows in enumerate(_value_rows(v_ref[0, rows, :])):
                vt_ref[hh, jb] = v_rows
            return carry
        lax.fori_loop(0, nblk, build, 0)

    q2 = q_ref[0]
    km_parts = _split3(km_ref[...])
    blk_id = lax.broadcasted_iota(jnp.int32, (nblk, tq), 0)
    qpos = lax.broadcasted_iota(jnp.int32, (nblk, tq), 1)
    own = first_own + jnp.where(qpos >= blk, 1, 0)
    past = blk_id < own
    q_m = []
    for hh in range(2):
        q_m.append(jnp.where(_head_lane_mask(hh), q2, jnp.zeros_like(q2)))

        gate = None
        for part in km_parts:
            term = lax.dot_general(part, q_m[hh], _NT, preferred_element_type=F32)
            gate = term if gate is None else gate + term
        gate = jnp.where(past, gate, NEG_INF)
        rank = jnp.zeros(gate.shape, jnp.int32)
        for jp in range(nblk):
            other = gate[jp:jp + 1, :]
            beats = (other > gate) | ((other == gate) & (jp < blk_id))
            rank = rank + jnp.where(beats, 1, 0)
        keep_all = ((rank < MOBA_TOP_K) & past) | (blk_id == own) | (blk_id == first_own + 1)
        sel_ref[hh] = jnp.where(keep_all, 1.0, 0.0)

    def scores(hh, j):
        rows = pl.ds(pl.multiple_of(j * blk, blk), blk)
        return lax.dot_general(k_ref[0, rows, :], q_m[hh], _NT,
                               preferred_element_type=F32)

    def keep(hh, j):
        return sel_ref[hh, pl.ds(j, 1), :] > 0.5

    _flash_init(m_ref, acc_ref)
    far_bias = [tab_ref[REL_BUCKETS - 1, 2 * hp + hh] * LOG2E for hh in range(2)]

    j_prev = jnp.maximum(first_own - 1, 0)
    neg_tile = jnp.full((blk, blk), NEG_INF, F32)
    near_mx = []
    for hh in range(2):
        own_t, prev_t = bias_ref[hh, 0], bias_ref[hh, 1]
        mask_prev = jnp.where(keep(hh, j_prev) & (qi >= 1), 0.0, NEG_INF)
        mask_own = jnp.where(keep(hh, first_own), 0.0, NEG_INF)
        far_t = jnp.full((blk, blk), far_bias[hh], F32)
        parts = [scores(hh, j_prev) + mask_prev + jnp.concatenate([prev_t, far_t], axis=1),
                 scores(hh, first_own) + mask_own + jnp.concatenate([own_t, prev_t], axis=1),
                 scores(hh, first_own + 1) + jnp.concatenate([neg_tile, own_t], axis=1)]
        mx = None
        for i, part in enumerate(parts):
            near_ref[hh, i * blk:(i + 1) * blk] = part
            part_mx = jnp.max(part, axis=0, keepdims=True)
            mx = part_mx if mx is None else jnp.maximum(mx, part_mx)
        near_mx.append(mx)
    for hh in range(2):
        _flash_update(near_ref[hh], near_mx[hh],
                      [vt_ref[hh, j_prev], vt_ref[hh, first_own], vt_ref[hh, first_own + 1]],
                      m_ref, acc_ref, hh)

    def scores_into(j, slot):
        for hh in range(2):
            _store_scores(scores(hh, j), s_ref, mx_ref, (slot, hh))

    def update(j, slot):
        for hh in range(2):
            _flash_update(s_ref[slot, hh], mx_ref[slot, hh], vt_ref[hh, j],
                          m_ref, acc_ref, hh, keep=keep(hh, j), const=far_bias[hh])

    _pipelined_blocks(jnp.maximum(first_own - 1, 0), nblk - 1, scores_into, update)
    _write_heads(o_ref, acc_ref)


def _moba_attention(qkv3, rel_table, bias_t, n_heads):
    b, s, d3 = qkv3.shape
    d = d3 // 3
    blk = MOBA_BLOCK
    tq = MOBA_Q_TILE
    assert s % tq == 0
    npair = n_heads // 2
    nblk = s // blk
    tile = pl.BlockSpec((1, tq, LANES), lambda bi, hp, qi: (bi, qi, hp))
    return pl.pallas_call(
        _moba_kernel, grid=(b, npair, s // tq),
        in_specs=[
            pl.BlockSpec(memory_space=pltpu.SMEM),
            tile,
            pl.BlockSpec((1, s, LANES), lambda bi, hp, qi: (bi, 0, npair + hp)),
            pl.BlockSpec((1, s, LANES), lambda bi, hp, qi: (bi, 0, 2 * npair + hp)),
            pl.BlockSpec((2, 2, blk, blk), lambda bi, hp, qi: (hp, 0, 0, 0)),
        ],
        out_specs=tile,
        out_shape=jax.ShapeDtypeStruct((b, s, d), BF16),
        scratch_shapes=[
            pltpu.VMEM((2, nblk, V_ROWS, blk), BF16),
            pltpu.VMEM((nblk, LANES), F32),
            pltpu.VMEM((2, nblk, tq), F32),
            pltpu.VMEM((2, 2, blk, tq), F32),
            pltpu.VMEM((2, 2, 1, tq), F32),
            pltpu.VMEM((2, 3 * blk, tq), F32),
            pltpu.VMEM((2, 1, tq), F32),
            pltpu.VMEM((2, V_ROWS, tq), F32),
        ],
        compiler_params=_params(3), name="moba_attention")(
            rel_table, qkv3, qkv3, qkv3, bias_t)


def _oproj_ffn_kernel(h_ref, o_ref, wo_ref, g_ref, win_ref, wout_ref, out_ref, *, tf):
    d_ff = wout_ref.shape[0]
    h1 = h_ref[...] + jnp.dot(o_ref[...], wo_ref[...], preferred_element_type=F32)
    u = _rmsnorm(h1, g_ref[...]).astype(BF16)
    acc = h1
    for c in range(d_ff // tf):
        gate = jnp.dot(u, win_ref[:, c * tf:(c + 1) * tf], preferred_element_type=F32)
        up = jnp.dot(u, win_ref[:, d_ff + c * tf:d_ff + (c + 1) * tf],
                     preferred_element_type=F32)
        act = (gate * jax.nn.sigmoid(gate) * up).astype(BF16)
        acc = acc + jnp.dot(act, wout_ref[c * tf:(c + 1) * tf, :],
                            preferred_element_type=F32)
    out_ref[...] = acc


def _oproj_ffn(h2, o2, w_o, g, w_in, w_out, *, tm, tf):
    n, d = h2.shape
    d_ff = w_out.shape[0]
    assert d_ff % tf == 0
    row = pl.BlockSpec((tm, d), lambda i: (i, 0))
    return pl.pallas_call(
        functools.partial(_oproj_ffn_kernel, tf=tf), grid=(n // tm,),
        in_specs=[row, row, _resident((d, d)), _resident((1, d)),
                  _resident((d, 2 * d_ff)), _resident((d_ff, d))],
        out_specs=row,
        out_shape=jax.ShapeDtypeStruct((n, d), F32),
        compiler_params=_params(1), name="oproj_ffn")(h2, o2, w_o, g, w_in, w_out)


def _ple_update(x_ref, g_ref, wg_ref, p_ref, wu_ref):
    x = x_ref[...]
    u = _rmsnorm(x, g_ref[...]).astype(BF16)
    gate = jax.nn.sigmoid(jnp.dot(u, wg_ref[...], preferred_element_type=F32))
    up = jnp.dot(p_ref[...].astype(BF16), wu_ref[...], preferred_element_type=F32)
    return x + gate * up


def _ple_next_kernel(x_ref, g_ref, wg_ref, p_ref, wu_ref, *refs, n_mixer):
    y = _ple_update(x_ref, g_ref, wg_ref, p_ref, wu_ref)
    refs[n_mixer][...] = y
    _mixer_inputs(y, refs[:n_mixer], refs[n_mixer + 1:])


def _ple_final_kernel(x_ref, g_ref, wg_ref, p_ref, wu_ref, fg_ref, out_ref):
    y = _ple_update(x_ref, g_ref, wg_ref, p_ref, wu_ref)
    out_ref[...] = _rmsnorm(y, fg_ref[...])


def _ple(h2, g, w_gate, p2, w_up, *, tm, mixer=None, final_g=None):
    n, d = h2.shape
    pd = p2.shape[1]
    row = pl.BlockSpec((tm, d), lambda i: (i, 0))
    in_specs = [row, _resident((1, d)), _resident((d, d)),
                pl.BlockSpec((tm, pd), lambda i: (i, 0)), _resident((pd, d))]
    h_shape = jax.ShapeDtypeStruct((n, d), F32)
    if mixer is None:
        return pl.pallas_call(
            _ple_final_kernel, grid=(n // tm,),
            in_specs=in_specs + [_resident((1, d))], out_specs=row, out_shape=h_shape,
            compiler_params=_params(1), name="ple_final")(h2, g, w_gate, p2, w_up, final_g)
    n_mixer = len(mixer.operands())
    return pl.pallas_call(
        functools.partial(_ple_next_kernel, n_mixer=n_mixer), grid=(n // tm,),
        in_specs=in_specs + mixer.in_specs(),
        out_specs=[row] + mixer.out_specs(tm), out_shape=[h_shape] + mixer.out_shapes(n),
        compiler_params=_params(1), name="ple_proj")(h2, g, w_gate, p2, w_up,
                                                     *mixer.operands())


def _row_tile(n, want):
    t = min(want, n)
    assert n % t == 0
    return t


def _col_tile(n, want):
    t = min(want, n)
    while n % t:
        t -= LANES
    return t


def kernel(x, p, attn_norm_g, fox_w_in, fox_b_f, fox_w_o, moba_w_in, moba_w_o, rel_bias_table,
           ffn_norm_g, ffn_w_in, ffn_w_out, ple_norm_g, ple_w_gate, ple_w_up, final_norm_g):
    b, s, d = x.shape
    depth = p.shape[0]
    n_heads = rel_bias_table.shape[1]
    assert d == n_heads * HEAD_DIM and n_heads % 2 == 0 and n_heads <= LANES
    n = b * s
    tm = _row_tile(n, DENSE_ROW_TILE)
    tf = _col_tile(ffn_w_out.shape[1], FFN_COL_CHUNK)

    def row_vec(v):
        return v.reshape(1, -1).astype(F32)

    col_scale = jnp.concatenate([jnp.full((d,), Q_SCALE, F32), jnp.ones((2 * d,), F32)])

    def mixer(i):
        if i % 2 == 0:
            w = fox_w_in[i // 2]
            return _Mixer(
                row_vec(attn_norm_g[i]), (w[:, :3 * d] * col_scale).astype(BF16),
                jnp.pad(w[:, 3 * d:], ((0, 0), (0, LANES - n_heads))).astype(BF16),
                jnp.pad(row_vec(fox_b_f[i // 2]), ((0, 0), (0, LANES - n_heads))))
        return _Mixer(row_vec(attn_norm_g[i]), (moba_w_in[i // 2] * col_scale).astype(BF16))

    rel_table = rel_bias_table.astype(F32)
    bias_t = _bias_tiles(rel_table, MOBA_BLOCK)

    h = x.reshape(n, d).astype(F32)
    mixed = _project(h, mixer(0), tm=tm)
    for i in range(depth):
        qkv3 = mixed[0].reshape(b, s, 3 * d)
        if i % 2 == 0:
            aq, ak = _cumsum(mixed[1].reshape(b, s, LANES))
            o = _fox_attention(qkv3, aq, ak, n_heads)
            w_o = fox_w_o[i // 2]
        else:
            o = _moba_attention(qkv3, rel_table, bias_t, n_heads)
            w_o = moba_w_o[i // 2]
        h = _oproj_ffn(h, o.reshape(n, d), w_o.astype(BF16), row_vec(ffn_norm_g[i]),
                       ffn_w_in[i].astype(BF16), ffn_w_out[i].astype(BF16), tm=tm, tf=tf)
        ple_args = (h, row_vec(ple_norm_g[i]), ple_w_gate[i].astype(BF16),
                    p[i].reshape(n, -1), ple_w_up[i].astype(BF16))
        if i + 1 < depth:
            h, *mixed = _ple(*ple_args, tm=tm, mixer=mixer(i + 1))
        else:
            h = _ple(*ple_args, tm=tm, final_g=row_vec(final_norm_g))
    return h.reshape(b, s, d).astype(x.dtype)
```

```python
import functools
import math
from typing import NamedTuple, Optional

import numpy as np
import jax
import jax.numpy as jnp
from jax import lax
from jax.experimental import pallas as pl
from jax.experimental.pallas import tpu as pltpu

F32 = jnp.float32
BF16 = jnp.bfloat16

RMS_EPS = 1e-6
HEAD_DIM = 64
MOBA_BLOCK = 256
MOBA_TOP_K = 3
REL_BUCKETS = 32
REL_MAX_DIST = 128

LANES = 128
BF16_SUBLANES = 16
V_ROWS = HEAD_DIM + BF16_SUBLANES
GATE_LANES = 8
LOG2E = math.log2(math.e)
Q_SCALE = HEAD_DIM ** -0.5 * LOG2E
FOX_BLK = 512
MOBA_Q_TILE = 2 * MOBA_BLOCK
CUMSUM_BLK = 256
DENSE_ROW_TILE = 512
PROJ_COL_CHUNK = 512
FFN_COL_CHUNK = 256
VMEM_LIMIT_BYTES = 56 * 1024 * 1024
NEG_INF = float("-inf")

_NT = (((1,), (1,)), ((), ()))


def _params(n_axes):
    return pltpu.CompilerParams(
        dimension_semantics=("arbitrary",) * n_axes,
        vmem_limit_bytes=VMEM_LIMIT_BYTES)


def _split3(x):
    x1 = x.astype(BF16)
    r1 = x - x1.astype(F32)
    x2 = r1.astype(BF16)
    x3 = (r1 - x2.astype(F32)).astype(BF16)
    return x1, x2, x3


def _rmsnorm(x, g):
    ms = jnp.mean(x * x, axis=-1, keepdims=True)
    return x * lax.rsqrt(ms + RMS_EPS) * g


def _log_sigmoid(x):
    return jnp.minimum(x, 0.0) - jnp.log1p(jnp.exp(-jnp.abs(x)))


def _resident(shape):
    return pl.BlockSpec(shape, lambda i: (0,) * len(shape), pipeline_mode=pl.Buffered(1))


class _Mixer(NamedTuple):
    g: jax.Array
    w_qkv: jax.Array
    w_f: Optional[jax.Array] = None
    b_f: Optional[jax.Array] = None

    def operands(self):
        return tuple(a for a in self if a is not None)

    def in_specs(self):
        return [_resident(a.shape) for a in self.operands()]

    def out_specs(self, tm):
        specs = [pl.BlockSpec((tm, self.w_qkv.shape[1]), lambda i: (i, 0))]
        if self.w_f is not None:
            specs.append(pl.BlockSpec((tm, LANES), lambda i: (i, 0)))
        return specs

    def out_shapes(self, n):
        shapes = [jax.ShapeDtypeStruct((n, self.w_qkv.shape[1]), BF16)]
        if self.w_f is not None:
            shapes.append(jax.ShapeDtypeStruct((n, LANES), F32))
        return shapes


def _mixer_inputs(y, mixer_refs, out_refs):
    g_ref, w_ref = mixer_refs[:2]
    u = _rmsnorm(y, g_ref[...]).astype(BF16)
    nout = w_ref.shape[1]
    tn = PROJ_COL_CHUNK if nout % PROJ_COL_CHUNK == 0 else nout
    for c in range(nout // tn):
        cols = slice(c * tn, (c + 1) * tn)
        out_refs[0][:, cols] = jnp.dot(u, w_ref[:, cols],
                                       preferred_element_type=F32).astype(BF16)
    if len(mixer_refs) > 2:
        wf_ref, bf_ref = mixer_refs[2:]
        f_logit = jnp.dot(u, wf_ref[...], preferred_element_type=F32) + bf_ref[...]
        out_refs[1][...] = _log_sigmoid(f_logit)


def _proj_kernel(x_ref, *refs, n_mixer):
    _mixer_inputs(x_ref[...], refs[:n_mixer], refs[n_mixer:])


def _project(h2, mixer, *, tm):
    n, d = h2.shape
    n_mixer = len(mixer.operands())
    return pl.pallas_call(
        functools.partial(_proj_kernel, n_mixer=n_mixer), grid=(n // tm,),
        in_specs=[pl.BlockSpec((tm, d), lambda i: (i, 0))] + mixer.in_specs(),
        out_specs=mixer.out_specs(tm), out_shape=mixer.out_shapes(n),
        compiler_params=_params(1), name="proj")(h2, *mixer.operands())


def _cumsum_kernel(lf_ref, aq_ref, ak_ref, carry_ref):
    @pl.when(pl.program_id(1) == 0)
    def _():
        carry_ref[...] = jnp.zeros_like(carry_ref)

    t = lf_ref.shape[1]
    row = lax.broadcasted_iota(jnp.int32, (t, t), 0)
    col = lax.broadcasted_iota(jnp.int32, (t, t), 1)
    tril = jnp.where(col <= row, 1.0, 0.0).astype(BF16)
    x1, x2, x3 = _split3(lf_ref[0])
    cs = (jnp.dot(tril, x1, preferred_element_type=F32)
          + jnp.dot(tril, x2, preferred_element_type=F32)
          + jnp.dot(tril, x3, preferred_element_type=F32))
    cs = cs + carry_ref[0:1, :]
    carry_ref[...] = jnp.broadcast_to(cs[t - 1:t, :], carry_ref.shape)

    src = lax.broadcasted_iota(jnp.int32, (LANES, LANES), 0)
    dst = lax.broadcasted_iota(jnp.int32, (LANES, LANES), 1)
    lane = lax.broadcasted_iota(jnp.int32, (1, LANES), 1) & (GATE_LANES - 1)
    aq = jnp.where((lane >= 3) & (lane < 6), 1.0, 0.0)
    ak = jnp.where(lane < 3, 1.0, 0.0)
    for i, part in enumerate(_split3(cs * LOG2E)):
        to_q = jnp.where(dst == GATE_LANES * src + i, 1.0, 0.0).astype(BF16)
        to_k = jnp.where(dst == GATE_LANES * src + 3 + i, 1.0, 0.0).astype(BF16)
        aq = aq + jnp.dot(part, to_q, preferred_element_type=F32)
        ak = ak - jnp.dot(part, to_k, preferred_element_type=F32)
    aq_ref[0] = aq.astype(BF16)
    ak_ref[0] = ak.astype(BF16)


def _cumsum(lf3):
    b, s, _ = lf3.shape
    t = min(CUMSUM_BLK, s)
    spec = pl.BlockSpec((1, t, LANES), lambda i, j: (i, j, 0))
    out = jax.ShapeDtypeStruct(lf3.shape, BF16)
    return pl.pallas_call(
        _cumsum_kernel, grid=(b, s // t),
        in_specs=[spec], out_specs=[spec, spec], out_shape=[out, out],
        scratch_shapes=[pltpu.VMEM((8, LANES), F32)],
        compiler_params=_params(2), name="gate_cumsum")(lf3)


def _head_lane_mask(hh):
    lane = lax.broadcasted_iota(jnp.int32, (1, LANES), 1)
    return (lane < HEAD_DIM) if hh == 0 else (lane >= HEAD_DIM)


def _value_rows(v2):
    v_t = v2.astype(F32).T
    row = lax.broadcasted_iota(jnp.int32, (V_ROWS - HEAD_DIM, v_t.shape[1]), 0)
    tail = jnp.where(row == 0, 1.0, 0.0)
    return [jnp.concatenate([v_t[hh * HEAD_DIM:(hh + 1) * HEAD_DIM], tail],
                            axis=0).astype(BF16) for hh in range(2)]


def _flash_init(m_ref, acc_ref):
    m_ref[...] = jnp.full(m_ref.shape, NEG_INF, F32)
    acc_ref[...] = jnp.zeros(acc_ref.shape, F32)


def _pipelined_blocks(n, last_block, scores_into, update, final_update=None):
    def step(j_next, j, slot):
        for hh in range(2):
            scores_into(j_next, 1 - slot, hh)
            update(j, slot, hh)

    for hh in range(2):
        scores_into(0, 0, hh)

    def pair(jj, carry):
        j = 2 * jj
        step(jnp.minimum(j + 1, last_block), j, 0)
        step(jnp.minimum(j + 2, last_block), j + 1, 1)
        return carry
    lax.fori_loop(0, n // 2, pair, 0)

    @pl.when(n % 2 == 1)
    def _():
        if final_update is not None:
            step(n, n - 1, 0)
            final_update(1)
        else:
            for hh in range(2):
                update(n - 1, 0, hh)

    if final_update is not None:
        @pl.when(n % 2 == 0)
        def _():
            final_update(0)


def _store_scores(s_t, s_ref, mx_ref, idx):
    s_ref[idx] = s_t
    mx_ref[idx] = jnp.max(s_t, axis=0, keepdims=True)


def _flash_update(s_t, mx, v_rows, m_ref, acc_ref, hh, keep=None, const=None):
    m_old = m_ref[hh]
    if const is not None:
        mx = mx + const
    if keep is not None:
        mx = jnp.where(keep, mx, NEG_INF)
    m_new = jnp.maximum(m_old, mx)
    m_safe = jnp.where(m_new == NEG_INF, 0.0, m_new)
    shift = m_safe if const is None else m_safe - const
    if keep is not None:
        shift = jnp.where(keep, shift, float("inf"))
    p = jnp.exp2(s_t - shift).astype(BF16)
    alpha = jnp.exp2(m_old - m_safe)
    m_ref[hh] = m_new
    if not isinstance(v_rows, (list, tuple)):
        v_rows = [v_rows]
    keys = p.shape[0] // len(v_rows)
    acc = alpha * acc_ref[hh]
    for i, v_i in enumerate(v_rows):
        acc = acc + jnp.dot(v_i, p[i * keys:(i + 1) * keys], preferred_element_type=F32)
    acc_ref[hh] = acc


def _write_heads(o_ref, acc_ref):
    outs = []
    for hh in range(2):
        acc = acc_ref[hh]
        outs.append(acc[:HEAD_DIM] / acc[HEAD_DIM:HEAD_DIM + 1])
    o_t = jnp.concatenate(outs, axis=0)
    o_ref[0] = o_t.T.astype(o_ref.dtype)


def _fox_kernel(q_ref, k_ref, v_ref, aq_ref, ak_ref, o_ref,
                kaug_ref, vt_ref, s_ref, mx_ref, m_ref, acc_ref):
    hp = pl.program_id(1)
    qi = pl.program_id(2)
    blk = q_ref.shape[1]
    nblk = k_ref.shape[1] // blk
    src = lax.broadcasted_iota(jnp.int32, (LANES, LANES), 0)
    dst = lax.broadcasted_iota(jnp.int32, (LANES, LANES), 1)

    def with_gate_lanes(x2, gate_lanes, hh):
        base = HEAD_DIM * (1 - hh)
        first = GATE_LANES * (2 * hp + hh)
        move = jnp.where((src >= first) & (src < first + GATE_LANES)
                         & (dst - base == src - first), 1.0, 0.0).astype(BF16)
        moved = jnp.dot(gate_lanes, move, preferred_element_type=F32).astype(BF16)
        return jnp.where(_head_lane_mask(hh), x2, moved)

    @pl.when(qi == 0)
    def _():
        def build(jb, carry):
            rows = pl.ds(pl.multiple_of(jb * blk, blk), blk)
            k2 = k_ref[0, rows, :]
            ak = ak_ref[0, rows, :]
            for hh in range(2):
                kaug_ref[hh, rows, :] = with_gate_lanes(k2, ak, hh)
            for hh, v_rows in enumerate(_value_rows(v_ref[0, rows, :])):
                vt_ref[hh, jb] = v_rows
            return carry
        lax.fori_loop(0, nblk, build, 0)

    q_aug = [with_gate_lanes(q_ref[0], aq_ref[0], hh) for hh in range(2)]
    krow = lax.broadcasted_iota(jnp.int32, (blk, blk), 0)
    qcol = lax.broadcasted_iota(jnp.int32, (blk, blk), 1)

    def scores_into(j, slot, hh):
        rows = pl.ds(pl.multiple_of(j * blk, blk), blk)
        s_t = lax.dot_general(kaug_ref[hh, rows, :], q_aug[hh], _NT,
                              preferred_element_type=F32)
        _store_scores(s_t, s_ref, mx_ref, (slot, hh))

    def update(j, slot, hh):
        _flash_update(s_ref[slot, hh], mx_ref[slot, hh], vt_ref[hh, j], m_ref, acc_ref, hh)

    def diagonal_update(slot):
        for hh in range(2):
            _store_scores(jnp.where(krow <= qcol, s_ref[slot, hh], NEG_INF),
                          s_ref, mx_ref, (slot, hh))
            update(qi, slot, hh)

    _flash_init(m_ref, acc_ref)
    _pipelined_blocks(qi, nblk - 1, scores_into, update, diagonal_update)
    _write_heads(o_ref, acc_ref)


def _fox_attention(qkv3, aq3, ak3, n_heads):
    b, s, d3 = qkv3.shape
    d = d3 // 3
    assert n_heads * GATE_LANES <= LANES
    blk = min(FOX_BLK, s)
    npair = n_heads // 2
    tile = pl.BlockSpec((1, blk, LANES), lambda bi, hp, qi: (bi, qi, hp))
    return pl.pallas_call(
        _fox_kernel, grid=(b, npair, s // blk),
        in_specs=[
            tile,
            pl.BlockSpec((1, s, LANES), lambda bi, hp, qi: (bi, 0, npair + hp)),
            pl.BlockSpec((1, s, LANES), lambda bi, hp, qi: (bi, 0, 2 * npair + hp)),
            pl.BlockSpec((1, blk, LANES), lambda bi, hp, qi: (bi, qi, 0)),
            pl.BlockSpec((1, s, LANES), lambda bi, hp, qi: (bi, 0, 0)),
        ],
        out_specs=tile,
        out_shape=jax.ShapeDtypeStruct((b, s, d), BF16),
        scratch_shapes=[
            pltpu.VMEM((2, s, LANES), BF16),
            pltpu.VMEM((2, s // blk, V_ROWS, blk), BF16),
            pltpu.VMEM((2, 2, blk, blk), F32),
            pltpu.VMEM((2, 2, 1, blk), F32),
            pltpu.VMEM((2, 1, blk), F32),
            pltpu.VMEM((2, V_ROWS, blk), F32),
        ],
        compiler_params=_params(3), name="fox_attention")(qkv3, qkv3, qkv3, aq3, ak3)


def _t5_bucket_np(n):
    max_exact = REL_BUCKETS // 2
    nf = np.maximum(n, 1).astype(np.float64)
    large = max_exact + (np.log(nf / max_exact) / math.log(REL_MAX_DIST / max_exact)
                         * (REL_BUCKETS - max_exact)).astype(np.int32)
    return np.where(n < max_exact, n, np.minimum(large, REL_BUCKETS - 1)).astype(np.int32)


def _bucket_tiles(blk):
    key = np.arange(blk)[:, None]
    qry = np.arange(blk)[None, :]
    own = np.where(key <= qry, _t5_bucket_np(np.maximum(qry - key, 0)), -1)
    prev = _t5_bucket_np(blk + qry - key)
    return np.stack([own, prev]).astype(np.int32)


def _bias_kernel(tab_ref, bucket_ref, o_ref):
    h = pl.program_id(0)
    bucket = bucket_ref[...]
    acc = jnp.where(bucket < 0, NEG_INF, 0.0).astype(F32)
    for bkt in range(REL_BUCKETS):
        acc = jnp.where(bucket == bkt, tab_ref[bkt, h] * LOG2E, acc)
    o_ref[0] = acc


def _bias_tiles(rel_table, blk):
    n_heads = rel_table.shape[1]
    buckets = jnp.asarray(_bucket_tiles(blk))
    return pl.pallas_call(
        _bias_kernel, grid=(n_heads,),
        in_specs=[pl.BlockSpec(memory_space=pltpu.SMEM),
                  pl.BlockSpec((2, blk, blk), lambda h: (0, 0, 0))],
        out_specs=pl.BlockSpec((1, 2, blk, blk), lambda h: (h, 0, 0, 0)),
        out_shape=jax.ShapeDtypeStruct((n_heads, 2, blk, blk), F32),
        compiler_params=_params(1), name="t5_bias_tiles")(rel_table, buckets)


def _moba_kernel(tab_ref, q_ref, k_ref, v_ref, bias_ref, o_ref,
                 vt_ref, km_ref, sel_ref, s_ref, mx_ref, near_ref, m_ref, acc_ref):
    hp = pl.program_id(1)
    qi = pl.program_id(2)
    blk = MOBA_BLOCK
    tq = q_ref.shape[1]
    nblk = k_ref.shape[1] // blk
    assert tq == 2 * blk and blk >= REL_MAX_DIST
    first_own = 2 * qi

    @pl.when(qi == 0)
    def _():
        def build(jb, carry):
            rows = pl.ds(pl.multiple_of(jb * blk, blk), blk)
            km_ref[pl.ds(jb, 1), :] = jnp.mean(k_ref[0, rows, :].astype(F32),
                                                axis=0, keepdims=True)
            for hh, v_rows in enumerate(_value_rows(v_ref[0, rows, :])):
                vt_ref[hh, jb] = v_rows
            return carry
        lax.fori_loop(0, nblk, build, 0)

    q2 = q_ref[0]
    km_parts = _split3(km_ref[...])
    blk_id = lax.broadcasted_iota(jnp.int32, (nblk, tq), 0)
    qpos = lax.broadcasted_iota(jnp.int32, (nblk, tq), 1)
    own = first_own + jnp.where(qpos >= blk, 1, 0)
    past = blk_id < own
    q_m = []
    for hh in range(2):
        q_m.append(jnp.where(_head_lane_mask(hh), q2, jnp.zeros_like(q2)))

        gate = None
        for part in km_parts:
            term = lax.dot_general(part, q_m[hh], _NT, preferred_element_type=F32)
            gate = term if gate is None else gate + term
        gate = jnp.where(past, gate, NEG_INF)
        rank = jnp.zeros(gate.shape, jnp.int32)
        for jp in range(nblk):
            other = gate[jp:jp + 1, :]
            beats = (other > gate) | ((other == gate) & (jp < blk_id))
            rank = rank + jnp.where(beats, 1, 0)
        keep_all = ((rank < MOBA_TOP_K) & past) | (blk_id == own) | (blk_id == first_own + 1)
        sel_ref[hh] = jnp.where(keep_all, 1.0, 0.0)

    def scores(hh, j):
        rows = pl.ds(pl.multiple_of(j * blk, blk), blk)
        return lax.dot_general(k_ref[0, rows, :], q_m[hh], _NT,
                               preferred_element_type=F32)

    def keep(hh, j):
        return sel_ref[hh, pl.ds(j, 1), :] > 0.5

    _flash_init(m_ref, acc_ref)
    far_bias = [tab_ref[REL_BUCKETS - 1, 2 * hp + hh] * LOG2E for hh in range(2)]

    j_prev = jnp.maximum(first_own - 1, 0)
    neg_tile = jnp.full((blk, blk), NEG_INF, F32)
    near_mx = []
    for hh in range(2):
        own_t, prev_t = bias_ref[hh, 0], bias_ref[hh, 1]
        mask_prev = jnp.where(keep(hh, j_prev) & (qi >= 1), 0.0, NEG_INF)
        mask_own = jnp.where(keep(hh, first_own), 0.0, NEG_INF)
        far_t = jnp.full((blk, blk), far_bias[hh], F32)
        parts = [scores(hh, j_prev) + mask_prev + jnp.concatenate([prev_t, far_t], axis=1),
                 scores(hh, first_own) + mask_own + jnp.concatenate([own_t, prev_t], axis=1),
                 scores(hh, first_own + 1) + jnp.concatenate([neg_tile, own_t], axis=1)]
        mx = None
        for i, part in enumerate(parts):
            near_ref[hh, i * blk:(i + 1) * blk] = part
            part_mx = jnp.max(part, axis=0, keepdims=True)
            mx = part_mx if mx is None else jnp.maximum(mx, part_mx)
        near_mx.append(mx)
    for hh in range(2):
        _flash_update(near_ref[hh], near_mx[hh],
                      [vt_ref[hh, j_prev], vt_ref[hh, first_own], vt_ref[hh, first_own + 1]],
                      m_ref, acc_ref, hh)

    def scores_into(j, slot, hh):
        _store_scores(scores(hh, j), s_ref, mx_ref, (slot, hh))

    def update(j, slot, hh):
        _flash_update(s_ref[slot, hh], mx_ref[slot, hh], vt_ref[hh, j],
                      m_ref, acc_ref, hh, keep=keep(hh, j), const=far_bias[hh])

    _pipelined_blocks(jnp.maximum(first_own - 1, 0), nblk - 1, scores_into, update)
    _write_heads(o_ref, acc_ref)


def _moba_attention(qkv3, rel_table, bias_t, n_heads):
    b, s, d3 = qkv3.shape
    d = d3 // 3
    blk = MOBA_BLOCK
    tq = MOBA_Q_TILE
    assert s % tq == 0
    npair = n_heads // 2
    nblk = s // blk
    tile = pl.BlockSpec((1, tq, LANES), lambda bi, hp, qi: (bi, qi, hp))
    return pl.pallas_call(
        _moba_kernel, grid=(b, npair, s // tq),
        in_specs=[
            pl.BlockSpec(memory_space=pltpu.SMEM),
            tile,
            pl.BlockSpec((1, s, LANES), lambda bi, hp, qi: (bi, 0, npair + hp)),
            pl.BlockSpec((1, s, LANES), lambda bi, hp, qi: (bi, 0, 2 * npair + hp)),
            pl.BlockSpec((2, 2, blk, blk), lambda bi, hp, qi: (hp, 0, 0, 0)),
        ],
        out_specs=tile,
        out_shape=jax.ShapeDtypeStruct((b, s, d), BF16),
        scratch_shapes=[
            pltpu.VMEM((2, nblk, V_ROWS, blk), BF16),
            pltpu.VMEM((nblk, LANES), F32),
            pltpu.VMEM((2, nblk, tq), F32),
            pltpu.VMEM((2, 2, blk, tq), F32),
            pltpu.VMEM((2, 2, 1, tq), F32),
            pltpu.VMEM((2, 3 * blk, tq), F32),
            pltpu.VMEM((2, 1, tq), F32),
            pltpu.VMEM((2, V_ROWS, tq), F32),
        ],
        compiler_params=_params(3), name="moba_attention")(
            rel_table, qkv3, qkv3, qkv3, bias_t)


def _oproj_ffn_kernel(h_ref, o_ref, wo_ref, g_ref, win_ref, wout_ref, out_ref, *, tf):
    d_ff = wout_ref.shape[0]
    h1 = h_ref[...] + jnp.dot(o_ref[...], wo_ref[...], preferred_element_type=F32)
    u = _rmsnorm(h1, g_ref[...]).astype(BF16)
    acc = h1
    for c in range(d_ff // tf):
        gate = jnp.dot(u, win_ref[:, c * tf:(c + 1) * tf], preferred_element_type=F32)
        up = jnp.dot(u, win_ref[:, d_ff + c * tf:d_ff + (c + 1) * tf],
                     preferred_element_type=F32)
        act = (gate * jax.nn.sigmoid(gate) * up).astype(BF16)
        acc = acc + jnp.dot(act, wout_ref[c * tf:(c + 1) * tf, :],
                            preferred_element_type=F32)
    out_ref[...] = acc


def _oproj_ffn(h2, o2, w_o, g, w_in, w_out, *, tm, tf):
    n, d = h2.shape
    d_ff = w_out.shape[0]
    assert d_ff % tf == 0
    row = pl.BlockSpec((tm, d), lambda i: (i, 0))
    return pl.pallas_call(
        functools.partial(_oproj_ffn_kernel, tf=tf), grid=(n // tm,),
        in_specs=[row, row, _resident((d, d)), _resident((1, d)),
                  _resident((d, 2 * d_ff)), _resident((d_ff, d))],
        out_specs=row,
        out_shape=jax.ShapeDtypeStruct((n, d), F32),
        compiler_params=_params(1), name="oproj_ffn")(h2, o2, w_o, g, w_in, w_out)


def _ple_update(x_ref, g_ref, wg_ref, p_ref, wu_ref):
    x = x_ref[...]
    u = _rmsnorm(x, g_ref[...]).astype(BF16)
    gate = jax.nn.sigmoid(jnp.dot(u, wg_ref[...], preferred_element_type=F32))
    up = jnp.dot(p_ref[...].astype(BF16), wu_ref[...], preferred_element_type=F32)
    return x + gate * up


def _ple_next_kernel(x_ref, g_ref, wg_ref, p_ref, wu_ref, *refs, n_mixer):
    y = _ple_update(x_ref, g_ref, wg_ref, p_ref, wu_ref)
    refs[n_mixer][...] = y
    _mixer_inputs(y, refs[:n_mixer], refs[n_mixer + 1:])


def _ple_final_kernel(x_ref, g_ref, wg_ref, p_ref, wu_ref, fg_ref, out_ref):
    y = _ple_update(x_ref, g_ref, wg_ref, p_ref, wu_ref)
    out_ref[...] = _rmsnorm(y, fg_ref[...])


def _ple(h2, g, w_gate, p2, w_up, *, tm, mixer=None, final_g=None):
    n, d = h2.shape
    pd = p2.shape[1]
    row = pl.BlockSpec((tm, d), lambda i: (i, 0))
    in_specs = [row, _resident((1, d)), _resident((d, d)),
                pl.BlockSpec((tm, pd), lambda i: (i, 0)), _resident((pd, d))]
    h_shape = jax.ShapeDtypeStruct((n, d), F32)
    if mixer is None:
        return pl.pallas_call(
            _ple_final_kernel, grid=(n // tm,),
            in_specs=in_specs + [_resident((1, d))], out_specs=row, out_shape=h_shape,
            compiler_params=_params(1), name="ple_final")(h2, g, w_gate, p2, w_up, final_g)
    n_mixer = len(mixer.operands())
    return pl.pallas_call(
        functools.partial(_ple_next_kernel, n_mixer=n_mixer), grid=(n // tm,),
        in_specs=in_specs + mixer.in_specs(),
        out_specs=[row] + mixer.out_specs(tm), out_shape=[h_shape] + mixer.out_shapes(n),
        compiler_params=_params(1), name="ple_proj")(h2, g, w_gate, p2, w_up,
                                                     *mixer.operands())


def _row_tile(n, want):
    t = min(want, n)
    assert n % t == 0
    return t


def _col_tile(n, want):
    t = min(want, n)
    while n % t:
        t -= LANES
    return t


def kernel(x, p, attn_norm_g, fox_w_in, fox_b_f, fox_w_o, moba_w_in, moba_w_o, rel_bias_table,
           ffn_norm_g, ffn_w_in, ffn_w_out, ple_norm_g, ple_w_gate, ple_w_up, final_norm_g):
    b, s, d = x.shape
    depth = p.shape[0]
    n_heads = rel_bias_table.shape[1]
    assert d == n_heads * HEAD_DIM and n_heads % 2 == 0 and n_heads <= LANES
    n = b * s
    tm = _row_tile(n, DENSE_ROW_TILE)
    tf = _col_tile(ffn_w_out.shape[1], FFN_COL_CHUNK)

    def row_vec(v):
        return v.reshape(1, -1).astype(F32)

    col_scale = jnp.concatenate([jnp.full((d,), Q_SCALE, F32), jnp.ones((2 * d,), F32)])

    def mixer(i):
        if i % 2 == 0:
            w = fox_w_in[i // 2]
            return _Mixer(
                row_vec(attn_norm_g[i]), (w[:, :3 * d] * col_scale).astype(BF16),
                jnp.pad(w[:, 3 * d:], ((0, 0), (0, LANES - n_heads))).astype(BF16),
                jnp.pad(row_vec(fox_b_f[i // 2]), ((0, 0), (0, LANES - n_heads))))
        return _Mixer(row_vec(attn_norm_g[i]), (moba_w_in[i // 2] * col_scale).astype(BF16))

    rel_table = rel_bias_table.astype(F32)
    bias_t = _bias_tiles(rel_table, MOBA_BLOCK)

    h = x.reshape(n, d).astype(F32)
    mixed = _project(h, mixer(0), tm=tm)
    for i in range(depth):
        qkv3 = mixed[0].reshape(b, s, 3 * d)
        if i % 2 == 0:
            aq, ak = _cumsum(mixed[1].reshape(b, s, LANES))
            o = _fox_attention(qkv3, aq, ak, n_heads)
            w_o = fox_w_o[i // 2]
        else:
            o = _moba_attention(qkv3, rel_table, bias_t, n_heads)
            w_o = moba_w_o[i // 2]
        h = _oproj_ffn(h, o.reshape(n, d), w_o.astype(BF16), row_vec(ffn_norm_g[i]),
                       ffn_w_in[i].astype(BF16), ffn_w_out[i].astype(BF16), tm=tm, tf=tf)
        ple_args = (h, row_vec(ple_norm_g[i]), ple_w_gate[i].astype(BF16),
                    p[i].reshape(n, -1), ple_w_up[i].astype(BF16))
        if i + 1 < depth:
            h, *mixed = _ple(*ple_args, tm=tm, mixer=mixer(i + 1))
        else:
            h = _ple(*ple_args, tm=tm, final_g=row_vec(final_norm_g))
    return h.reshape(b, s, d).astype(x.dtype)
```

```python
import functools
import math
from typing import NamedTuple, Optional

import numpy as np
import jax
import jax.numpy as jnp
from jax import lax
from jax.experimental import pallas as pl
from jax.experimental.pallas import tpu as pltpu

F32 = jnp.float32
BF16 = jnp.bfloat16

RMS_EPS = 1e-6
HEAD_DIM = 64
MOBA_BLOCK = 256
MOBA_TOP_K = 3
REL_BUCKETS = 32
REL_MAX_DIST = 128

LANES = 128
BF16_SUBLANES = 16
V_ROWS = HEAD_DIM + BF16_SUBLANES
GATE_LANES = 8
LOG2E = math.log2(math.e)
Q_SCALE = HEAD_DIM ** -0.5 * LOG2E
FOX_BLK = 512
MOBA_Q_TILE = 2 * MOBA_BLOCK
CUMSUM_BLK = 256
DENSE_ROW_TILE = 512
PROJ_COL_CHUNK = 512
FFN_COL_CHUNK = 256
VMEM_LIMIT_BYTES = 56 * 1024 * 1024
NEG_INF = float("-inf")


def _params(n_axes):
    return pltpu.CompilerParams(
        dimension_semantics=("arbitrary",) * n_axes,
        vmem_limit_bytes=VMEM_LIMIT_BYTES)


def _split3(x):
    x1 = x.astype(BF16)
    r1 = x - x1.astype(F32)
    x2 = r1.astype(BF16)
    x3 = (r1 - x2.astype(F32)).astype(BF16)
    return x1, x2, x3


def _rmsnorm(x, g):
    ms = jnp.mean(x * x, axis=-1, keepdims=True)
    return x * lax.rsqrt(ms + RMS_EPS) * g


def _log_sigmoid(x):
    return jnp.minimum(x, 0.0) - jnp.log1p(jnp.exp(-jnp.abs(x)))


def _resident(shape):
    return pl.BlockSpec(shape, lambda i: (0,) * len(shape), pipeline_mode=pl.Buffered(1))


class _Mixer(NamedTuple):
    g: jax.Array
    w_qkv: jax.Array
    w_f: Optional[jax.Array] = None
    b_f: Optional[jax.Array] = None

    def operands(self):
        return tuple(a for a in self if a is not None)

    def in_specs(self):
        return [_resident(a.shape) for a in self.operands()]

    def out_specs(self, tm):
        specs = [pl.BlockSpec((tm, self.w_qkv.shape[1]), lambda i: (i, 0))]
        if self.w_f is not None:
            specs.append(pl.BlockSpec((tm, LANES), lambda i: (i, 0)))
        return specs

    def out_shapes(self, n):
        shapes = [jax.ShapeDtypeStruct((n, self.w_qkv.shape[1]), BF16)]
        if self.w_f is not None:
            shapes.append(jax.ShapeDtypeStruct((n, LANES), F32))
        return shapes


def _mixer_inputs(y, mixer_refs, out_refs):
    g_ref, w_ref = mixer_refs[:2]
    u = _rmsnorm(y, g_ref[...]).astype(BF16)
    nout = w_ref.shape[1]
    tn = PROJ_COL_CHUNK if nout % PROJ_COL_CHUNK == 0 else nout
    for c in range(nout // tn):
        cols = slice(c * tn, (c + 1) * tn)
        out_refs[0][:, cols] = jnp.dot(u, w_ref[:, cols],
                                       preferred_element_type=F32).astype(BF16)
    if len(mixer_refs) > 2:
        wf_ref, bf_ref = mixer_refs[2:]
        f_logit = jnp.dot(u, wf_ref[...], preferred_element_type=F32) + bf_ref[...]
        out_refs[1][...] = _log_sigmoid(f_logit)


def _proj_kernel(x_ref, *refs, n_mixer):
    _mixer_inputs(x_ref[...], refs[:n_mixer], refs[n_mixer:])


def _project(h2, mixer, *, tm):
    n, d = h2.shape
    n_mixer = len(mixer.operands())
    return pl.pallas_call(
        functools.partial(_proj_kernel, n_mixer=n_mixer), grid=(n // tm,),
        in_specs=[pl.BlockSpec((tm, d), lambda i: (i, 0))] + mixer.in_specs(),
        out_specs=mixer.out_specs(tm), out_shape=mixer.out_shapes(n),
        compiler_params=_params(1), name="proj")(h2, *mixer.operands())


def _cumsum_kernel(lf_ref, aq_ref, ak_ref, carry_ref):
    @pl.when(pl.program_id(1) == 0)
    def _():
        carry_ref[...] = jnp.zeros_like(carry_ref)

    t = lf_ref.shape[1]
    row = lax.broadcasted_iota(jnp.int32, (t, t), 0)
    col = lax.broadcasted_iota(jnp.int32, (t, t), 1)
    tril = jnp.where(col <= row, 1.0, 0.0).astype(BF16)
    x1, x2, x3 = _split3(lf_ref[0])
    cs = (jnp.dot(tril, x1, preferred_element_type=F32)
          + jnp.dot(tril, x2, preferred_element_type=F32)
          + jnp.dot(tril, x3, preferred_element_type=F32))
    cs = cs + carry_ref[0:1, :]
    carry_ref[...] = jnp.broadcast_to(cs[t - 1:t, :], carry_ref.shape)

    src = lax.broadcasted_iota(jnp.int32, (LANES, LANES), 0)
    dst = lax.broadcasted_iota(jnp.int32, (LANES, LANES), 1)
    lane = lax.broadcasted_iota(jnp.int32, (1, LANES), 1) & (GATE_LANES - 1)
    aq = jnp.where((lane >= 3) & (lane < 6), 1.0, 0.0)
    ak = jnp.where(lane < 3, 1.0, 0.0)
    for i, part in enumerate(_split3(cs * LOG2E)):
        to_q = jnp.where(dst == GATE_LANES * src + i, 1.0, 0.0).astype(BF16)
        to_k = jnp.where(dst == GATE_LANES * src + 3 + i, 1.0, 0.0).astype(BF16)
        aq = aq + jnp.dot(part, to_q, preferred_element_type=F32)
        ak = ak - jnp.dot(part, to_k, preferred_element_type=F32)
    aq_ref[0] = aq.astype(BF16)
    ak_ref[0] = ak.astype(BF16)


def _cumsum(lf3):
    b, s, _ = lf3.shape
    t = min(CUMSUM_BLK, s)
    spec = pl.BlockSpec((1, t, LANES), lambda i, j: (i, j, 0))
    out = jax.ShapeDtypeStruct(lf3.shape, BF16)
    return pl.pallas_call(
        _cumsum_kernel, grid=(b, s // t),
        in_specs=[spec], out_specs=[spec, spec], out_shape=[out, out],
        scratch_shapes=[pltpu.VMEM((8, LANES), F32)],
        compiler_params=_params(2), name="gate_cumsum")(lf3)


def _head_lane_mask(hh):
    lane = lax.broadcasted_iota(jnp.int32, (1, LANES), 1)
    return (lane < HEAD_DIM) if hh == 0 else (lane >= HEAD_DIM)


def _transpose_bf16(x):
    return x.astype(F32).T.astype(BF16)


def _value_rows(v2):
    v_t = v2.astype(F32).T
    row = lax.broadcasted_iota(jnp.int32, (V_ROWS - HEAD_DIM, v_t.shape[1]), 0)
    tail = jnp.where(row == 0, 1.0, 0.0)
    return [jnp.concatenate([v_t[hh * HEAD_DIM:(hh + 1) * HEAD_DIM], tail],
                            axis=0).astype(BF16) for hh in range(2)]


def _flash_init(m_ref, acc_ref):
    m_ref[...] = jnp.full(m_ref.shape, NEG_INF, F32)
    acc_ref[...] = jnp.zeros(acc_ref.shape, F32)


def _pipelined_blocks(n, last_block, scores_into, update, final_update=None):
    def step(j_next, j, slot):
        for hh in range(2):
            scores_into(j_next, 1 - slot, hh)
            update(j, slot, hh)

    for hh in range(2):
        scores_into(0, 0, hh)

    def pair(jj, carry):
        j = 2 * jj
        step(jnp.minimum(j + 1, last_block), j, 0)
        step(jnp.minimum(j + 2, last_block), j + 1, 1)
        return carry
    lax.fori_loop(0, n // 2, pair, 0)

    @pl.when(n % 2 == 1)
    def _():
        if final_update is not None:
            step(n, n - 1, 0)
            final_update(1)
        else:
            for hh in range(2):
                update(n - 1, 0, hh)

    if final_update is not None:
        @pl.when(n % 2 == 0)
        def _():
            final_update(0)


def _store_scores(s_t, s_ref, mx_ref, idx):
    s_ref[idx] = s_t
    mx_ref[idx] = jnp.max(s_t, axis=0, keepdims=True)


def _flash_update(s_t, mx, v_rows, m_ref, acc_ref, hh, keep=None, const=None):
    m_old = m_ref[hh]
    if const is not None:
        mx = mx + const
    if keep is not None:
        mx = jnp.where(keep, mx, NEG_INF)
    m_new = jnp.maximum(m_old, mx)
    m_safe = jnp.where(m_new == NEG_INF, 0.0, m_new)
    shift = m_safe if const is None else m_safe - const
    if keep is not None:
        shift = jnp.where(keep, shift, float("inf"))
    p = jnp.exp2(s_t - shift).astype(BF16)
    alpha = jnp.exp2(m_old - m_safe)
    m_ref[hh] = m_new
    if not isinstance(v_rows, (list, tuple)):
        v_rows = [v_rows]
    keys = p.shape[0] // len(v_rows)
    acc = alpha * acc_ref[hh]
    for i, v_i in enumerate(v_rows):
        acc = acc + jnp.dot(v_i, p[i * keys:(i + 1) * keys], preferred_element_type=F32)
    acc_ref[hh] = acc


def _write_heads(o_ref, acc_ref):
    outs = []
    for hh in range(2):
        acc = acc_ref[hh]
        outs.append(acc[:HEAD_DIM] / acc[HEAD_DIM:HEAD_DIM + 1])
    o_t = jnp.concatenate(outs, axis=0)
    o_ref[0] = o_t.T.astype(o_ref.dtype)


def _fox_kernel(q_ref, k_ref, v_ref, aq_ref, ak_ref, o_ref,
                kaug_ref, vt_ref, s_ref, mx_ref, m_ref, acc_ref):
    hp = pl.program_id(1)
    qi = pl.program_id(2)
    blk = q_ref.shape[1]
    nblk = k_ref.shape[1] // blk
    src = lax.broadcasted_iota(jnp.int32, (LANES, LANES), 0)
    dst = lax.broadcasted_iota(jnp.int32, (LANES, LANES), 1)

    def with_gate_lanes(x2, gate_lanes, hh):
        base = HEAD_DIM * (1 - hh)
        first = GATE_LANES * (2 * hp + hh)
        move = jnp.where((src >= first) & (src < first + GATE_LANES)
                         & (dst - base == src - first), 1.0, 0.0).astype(BF16)
        moved = jnp.dot(gate_lanes, move, preferred_element_type=F32).astype(BF16)
        return jnp.where(_head_lane_mask(hh), x2, moved)

    @pl.when(qi == 0)
    def _():
        def build(jb, carry):
            rows = pl.ds(pl.multiple_of(jb * blk, blk), blk)
            k2 = k_ref[0, rows, :]
            ak = ak_ref[0, rows, :]
            for hh in range(2):
                kaug_ref[hh, rows, :] = with_gate_lanes(k2, ak, hh)
            for hh, v_rows in enumerate(_value_rows(v_ref[0, rows, :])):
                vt_ref[hh, jb] = v_rows
            return carry
        lax.fori_loop(0, nblk, build, 0)

    q_aug_t = [_transpose_bf16(with_gate_lanes(q_ref[0], aq_ref[0], hh)) for hh in range(2)]
    krow = lax.broadcasted_iota(jnp.int32, (blk, blk), 0)
    qcol = lax.broadcasted_iota(jnp.int32, (blk, blk), 1)

    def scores_into(j, slot, hh):
        rows = pl.ds(pl.multiple_of(j * blk, blk), blk)
        s_t = jnp.dot(kaug_ref[hh, rows, :], q_aug_t[hh], preferred_element_type=F32)
        _store_scores(s_t, s_ref, mx_ref, (slot, hh))

    def update(j, slot, hh):
        _flash_update(s_ref[slot, hh], mx_ref[slot, hh], vt_ref[hh, j], m_ref, acc_ref, hh)

    def diagonal_update(slot):
        for hh in range(2):
            _store_scores(jnp.where(krow <= qcol, s_ref[slot, hh], NEG_INF),
                          s_ref, mx_ref, (slot, hh))
            update(qi, slot, hh)

    _flash_init(m_ref, acc_ref)
    _pipelined_blocks(qi, nblk - 1, scores_into, update, diagonal_update)
    _write_heads(o_ref, acc_ref)


def _fox_attention(qkv3, aq3, ak3, n_heads):
    b, s, d3 = qkv3.shape
    d = d3 // 3
    assert n_heads * GATE_LANES <= LANES
    blk = min(FOX_BLK, s)
    npair = n_heads // 2
    tile = pl.BlockSpec((1, blk, LANES), lambda bi, hp, qi: (bi, qi, hp))
    return pl.pallas_call(
        _fox_kernel, grid=(b, npair, s // blk),
        in_specs=[
            tile,
            pl.BlockSpec((1, s, LANES), lambda bi, hp, qi: (bi, 0, npair + hp)),
            pl.BlockSpec((1, s, LANES), lambda bi, hp, qi: (bi, 0, 2 * npair + hp)),
            pl.BlockSpec((1, blk, LANES), lambda bi, hp, qi: (bi, qi, 0)),
            pl.BlockSpec((1, s, LANES), lambda bi, hp, qi: (bi, 0, 0)),
        ],
        out_specs=tile,
        out_shape=jax.ShapeDtypeStruct((b, s, d), BF16),
        scratch_shapes=[
            pltpu.VMEM((2, s, LANES), BF16),
            pltpu.VMEM((2, s // blk, V_ROWS, blk), BF16),
            pltpu.VMEM((2, 2, blk, blk), F32),
            pltpu.VMEM((2, 2, 1, blk), F32),
            pltpu.VMEM((2, 1, blk), F32),
            pltpu.VMEM((2, V_ROWS, blk), F32),
        ],
        compiler_params=_params(3), name="fox_attention")(qkv3, qkv3, qkv3, aq3, ak3)


def _t5_bucket_np(n):
    max_exact = REL_BUCKETS // 2
    nf = np.maximum(n, 1).astype(np.float64)
    large = max_exact + (np.log(nf / max_exact) / math.log(REL_MAX_DIST / max_exact)
                         * (REL_BUCKETS - max_exact)).astype(np.int32)
    return np.where(n < max_exact, n, np.minimum(large, REL_BUCKETS - 1)).astype(np.int32)


def _bucket_tiles(blk):
    key = np.arange(blk)[:, None]
    qry = np.arange(blk)[None, :]
    own = np.where(key <= qry, _t5_bucket_np(np.maximum(qry - key, 0)), -1)
    prev = _t5_bucket_np(blk + qry - key)
    return np.stack([own, prev]).astype(np.int32)


def _bias_kernel(tab_ref, bucket_ref, o_ref):
    h = pl.program_id(0)
    bucket = bucket_ref[...]
    acc = jnp.where(bucket < 0, NEG_INF, 0.0).astype(F32)
    for bkt in range(REL_BUCKETS):
        acc = jnp.where(bucket == bkt, tab_ref[bkt, h] * LOG2E, acc)
    o_ref[0] = acc


def _bias_tiles(rel_table, blk):
    n_heads = rel_table.shape[1]
    buckets = jnp.asarray(_bucket_tiles(blk))
    return pl.pallas_call(
        _bias_kernel, grid=(n_heads,),
        in_specs=[pl.BlockSpec(memory_space=pltpu.SMEM),
                  pl.BlockSpec((2, blk, blk), lambda h: (0, 0, 0))],
        out_specs=pl.BlockSpec((1, 2, blk, blk), lambda h: (h, 0, 0, 0)),
        out_shape=jax.ShapeDtypeStruct((n_heads, 2, blk, blk), F32),
        compiler_params=_params(1), name="t5_bias_tiles")(rel_table, buckets)


def _moba_kernel(tab_ref, q_ref, k_ref, v_ref, bias_ref, o_ref,
                 vt_ref, km_ref, sel_ref, s_ref, mx_ref, near_ref, m_ref, acc_ref):
    hp = pl.program_id(1)
    qi = pl.program_id(2)
    blk = MOBA_BLOCK
    tq = q_ref.shape[1]
    nblk = k_ref.shape[1] // blk
    assert tq == 2 * blk and blk >= REL_MAX_DIST
    first_own = 2 * qi

    @pl.when(qi == 0)
    def _():
        def build(jb, carry):
            rows = pl.ds(pl.multiple_of(jb * blk, blk), blk)
            km_ref[pl.ds(jb, 1), :] = jnp.mean(k_ref[0, rows, :].astype(F32),
                                                axis=0, keepdims=True)
            for hh, v_rows in enumerate(_value_rows(v_ref[0, rows, :])):
                vt_ref[hh, jb] = v_rows
            return carry
        lax.fori_loop(0, nblk, build, 0)

    q2 = q_ref[0]
    km_parts = _split3(km_ref[...])
    blk_id = lax.broadcasted_iota(jnp.int32, (nblk, tq), 0)
    qpos = lax.broadcasted_iota(jnp.int32, (nblk, tq), 1)
    own = first_own + jnp.where(qpos >= blk, 1, 0)
    past = blk_id < own
    q_m_t = []
    for hh in range(2):
        q_m_t.append(_transpose_bf16(jnp.where(_head_lane_mask(hh), q2, jnp.zeros_like(q2))))

        gate = None
        for part in km_parts:
            term = jnp.dot(part, q_m_t[hh], preferred_element_type=F32)
            gate = term if gate is None else gate + term
        work = jnp.where(past, gate, NEG_INF)
        picked = jnp.zeros(gate.shape, F32)
        for _ in range(MOBA_TOP_K):
            best = jnp.max(work, axis=0, keepdims=True)
            first = jnp.min(jnp.where(work == best, blk_id, nblk), axis=0, keepdims=True)
            hit = blk_id == first
            picked = jnp.where(hit, 1.0, picked)
            work = jnp.where(hit, NEG_INF, work)
        keep_all = ((picked > 0.5) & past) | (blk_id == own) | (blk_id == first_own + 1)
        sel_ref[hh] = jnp.where(keep_all, 1.0, 0.0)

    def scores(hh, j):
        rows = pl.ds(pl.multiple_of(j * blk, blk), blk)
        return jnp.dot(k_ref[0, rows, :], q_m_t[hh], preferred_element_type=F32)

    def keep(hh, j):
        return sel_ref[hh, pl.ds(j, 1), :] > 0.5

    _flash_init(m_ref, acc_ref)
    far_bias = [tab_ref[REL_BUCKETS - 1, 2 * hp + hh] * LOG2E for hh in range(2)]

    j_prev = jnp.maximum(first_own - 1, 0)
    neg_tile = jnp.full((blk, blk), NEG_INF, F32)
    near_mx = []
    for hh in range(2):
        own_t, prev_t = bias_ref[hh, 0], bias_ref[hh, 1]
        mask_prev = jnp.where(keep(hh, j_prev) & (qi >= 1), 0.0, NEG_INF)
        mask_own = jnp.where(keep(hh, first_own), 0.0, NEG_INF)
        far_t = jnp.full((blk, blk), far_bias[hh], F32)
        parts = [scores(hh, j_prev) + mask_prev + jnp.concatenate([prev_t, far_t], axis=1),
                 scores(hh, first_own) + mask_own + jnp.concatenate([own_t, prev_t], axis=1),
                 scores(hh, first_own + 1) + jnp.concatenate([neg_tile, own_t], axis=1)]
        mx = None
        for i, part in enumerate(parts):
            near_ref[hh, i * blk:(i + 1) * blk] = part
            part_mx = jnp.max(part, axis=0, keepdims=True)
            mx = part_mx if mx is None else jnp.maximum(mx, part_mx)
        near_mx.append(mx)
    for hh in range(2):
        _flash_update(near_ref[hh], near_mx[hh],
                      [vt_ref[hh, j_prev], vt_ref[hh, first_own], vt_ref[hh, first_own + 1]],
                      m_ref, acc_ref, hh)

    def scores_into(j, slot, hh):
        _store_scores(scores(hh, j), s_ref, mx_ref, (slot, hh))

    def update(j, slot, hh):
        _flash_update(s_ref[slot, hh], mx_ref[slot, hh], vt_ref[hh, j],
                      m_ref, acc_ref, hh, keep=keep(hh, j), const=far_bias[hh])

    _pipelined_blocks(jnp.maximum(first_own - 1, 0), nblk - 1, scores_into, update)
    _write_heads(o_ref, acc_ref)


def _moba_attention(qkv3, rel_table, bias_t, n_heads):
    b, s, d3 = qkv3.shape
    d = d3 // 3
    blk = MOBA_BLOCK
    tq = MOBA_Q_TILE
    assert s % tq == 0
    npair = n_heads // 2
    nblk = s // blk
    tile = pl.BlockSpec((1, tq, LANES), lambda bi, hp, qi: (bi, qi, hp))
    return pl.pallas_call(
        _moba_kernel, grid=(b, npair, s // tq),
        in_specs=[
            pl.BlockSpec(memory_space=pltpu.SMEM),
            tile,
            pl.BlockSpec((1, s, LANES), lambda bi, hp, qi: (bi, 0, npair + hp)),
            pl.BlockSpec((1, s, LANES), lambda bi, hp, qi: (bi, 0, 2 * npair + hp)),
            pl.BlockSpec((2, 2, blk, blk), lambda bi, hp, qi: (hp, 0, 0, 0)),
        ],
        out_specs=tile,
        out_shape=jax.ShapeDtypeStruct((b, s, d), BF16),
        scratch_shapes=[
            pltpu.VMEM((2, nblk, V_ROWS, blk), BF16),
            pltpu.VMEM((nblk, LANES), F32),
            pltpu.VMEM((2, nblk, tq), F32),
            pltpu.VMEM((2, 2, blk, tq), F32),
            pltpu.VMEM((2, 2, 1, tq), F32),
            pltpu.VMEM((2, 3 * blk, tq), F32),
            pltpu.VMEM((2, 1, tq), F32),
            pltpu.VMEM((2, V_ROWS, tq), F32),
        ],
        compiler_params=_params(3), name="moba_attention")(
            rel_table, qkv3, qkv3, qkv3, bias_t)


def _oproj_ffn_kernel(h_ref, o_ref, wo_ref, g_ref, win_ref, wout_ref, out_ref, *, tf):
    d_ff = wout_ref.shape[0]
    h1 = h_ref[...] + jnp.dot(o_ref[...], wo_ref[...], preferred_element_type=F32)
    u = _rmsnorm(h1, g_ref[...]).astype(BF16)
    acc = h1
    for c in range(d_ff // tf):
        gate = jnp.dot(u, win_ref[:, c * tf:(c + 1) * tf], preferred_element_type=F32)
        up = jnp.dot(u, win_ref[:, d_ff + c * tf:d_ff + (c + 1) * tf],
                     preferred_element_type=F32)
        act = (gate * jax.nn.sigmoid(gate) * up).astype(BF16)
        acc = acc + jnp.dot(act, wout_ref[c * tf:(c + 1) * tf, :],
                            preferred_element_type=F32)
    out_ref[...] = acc


def _oproj_ffn(h2, o2, w_o, g, w_in, w_out, *, tm, tf):
    n, d = h2.shape
    d_ff = w_out.shape[0]
    assert d_ff % tf == 0
    row = pl.BlockSpec((tm, d), lambda i: (i, 0))
    return pl.pallas_call(
        functools.partial(_oproj_ffn_kernel, tf=tf), grid=(n // tm,),
        in_specs=[row, row, _resident((d, d)), _resident((1, d)),
                  _resident((d, 2 * d_ff)), _resident((d_ff, d))],
        out_specs=row,
        out_shape=jax.ShapeDtypeStruct((n, d), F32),
        compiler_params=_params(1), name="oproj_ffn")(h2, o2, w_o, g, w_in, w_out)


def _ple_update(x_ref, g_ref, wg_ref, p_ref, wu_ref):
    x = x_ref[...]
    u = _rmsnorm(x, g_ref[...]).astype(BF16)
    gate = jax.nn.sigmoid(jnp.dot(u, wg_ref[...], preferred_element_type=F32))
    up = jnp.dot(p_ref[...].astype(BF16), wu_ref[...], preferred_element_type=F32)
    return x + gate * up


def _ple_next_kernel(x_ref, g_ref, wg_ref, p_ref, wu_ref, *refs, n_mixer):
    y = _ple_update(x_ref, g_ref, wg_ref, p_ref, wu_ref)
    refs[n_mixer][...] = y
    _mixer_inputs(y, refs[:n_mixer], refs[n_mixer + 1:])


def _ple_final_kernel(x_ref, g_ref, wg_ref, p_ref, wu_ref, fg_ref, out_ref):
    y = _ple_update(x_ref, g_ref, wg_ref, p_ref, wu_ref)
    out_ref[...] = _rmsnorm(y, fg_ref[...])


def _ple(h2, g, w_gate, p2, w_up, *, tm, mixer=None, final_g=None):
    n, d = h2.shape
    pd = p2.shape[1]
    row = pl.BlockSpec((tm, d), lambda i: (i, 0))
    in_specs = [row, _resident((1, d)), _resident((d, d)),
                pl.BlockSpec((tm, pd), lambda i: (i, 0)), _resident((pd, d))]
    h_shape = jax.ShapeDtypeStruct((n, d), F32)
    if mixer is None:
        return pl.pallas_call(
            _ple_final_kernel, grid=(n // tm,),
            in_specs=in_specs + [_resident((1, d))], out_specs=row, out_shape=h_shape,
            compiler_params=_params(1), name="ple_final")(h2, g, w_gate, p2, w_up, final_g)
    n_mixer = len(mixer.operands())
    return pl.pallas_call(
        functools.partial(_ple_next_kernel, n_mixer=n_mixer), grid=(n // tm,),
        in_specs=in_specs + mixer.in_specs(),
        out_specs=[row] + mixer.out_specs(tm), out_shape=[h_shape] + mixer.out_shapes(n),
        compiler_params=_params(1), name="ple_proj")(h2, g, w_gate, p2, w_up,
                                                     *mixer.operands())


def _row_tile(n, want):
    t = min(want, n)
    assert n % t == 0
    return t


def _col_tile(n, want):
    t = min(want, n)
    while n % t:
        t -= LANES
    return t


def kernel(x, p, attn_norm_g, fox_w_in, fox_b_f, fox_w_o, moba_w_in, moba_w_o, rel_bias_table,
           ffn_norm_g, ffn_w_in, ffn_w_out, ple_norm_g, ple_w_gate, ple_w_up, final_norm_g):
    b, s, d = x.shape
    depth = p.shape[0]
    n_heads = rel_bias_table.shape[1]
    assert d == n_heads * HEAD_DIM and n_heads % 2 == 0 and n_heads <= LANES
    n = b * s
    tm = _row_tile(n, DENSE_ROW_TILE)
    tf = _col_tile(ffn_w_out.shape[1], FFN_COL_CHUNK)

    def row_vec(v):
        return v.reshape(1, -1).astype(F32)

    col_scale = jnp.concatenate([jnp.full((d,), Q_SCALE, F32), jnp.ones((2 * d,), F32)])

    def mixer(i):
        if i % 2 == 0:
            w = fox_w_in[i // 2]
            return _Mixer(
                row_vec(attn_norm_g[i]), (w[:, :3 * d] * col_scale).astype(BF16),
                jnp.pad(w[:, 3 * d:], ((0, 0), (0, LANES - n_heads))).astype(BF16),
                jnp.pad(row_vec(fox_b_f[i // 2]), ((0, 0), (0, LANES - n_heads))))
        return _Mixer(row_vec(attn_norm_g[i]), (moba_w_in[i // 2] * col_scale).astype(BF16))

    rel_table = rel_bias_table.astype(F32)
    bias_t = _bias_tiles(rel_table, MOBA_BLOCK)

    h = x.reshape(n, d).astype(F32)
    mixed = _project(h, mixer(0), tm=tm)
    for i in range(depth):
        qkv3 = mixed[0].reshape(b, s, 3 * d)
        if i % 2 == 0:
            aq, ak = _cumsum(mixed[1].reshape(b, s, LANES))
            o = _fox_attention(qkv3, aq, ak, n_heads)
            w_o = fox_w_o[i // 2]
        else:
            o = _moba_attention(qkv3, rel_table, bias_t, n_heads)
            w_o = moba_w_o[i // 2]
        h = _oproj_ffn(h, o.reshape(n, d), w_o.astype(BF16), row_vec(ffn_norm_g[i]),
                       ffn_w_in[i].astype(BF16), ffn_w_out[i].astype(BF16), tm=tm, tf=tf)
        ple_args = (h, row_vec(ple_norm_g[i]), ple_w_gate[i].astype(BF16),
                    p[i].reshape(n, -1), ple_w_up[i].astype(BF16))
        if i + 1 < depth:
            h, *mixed = _ple(*ple_args, tm=tm, mixer=mixer(i + 1))
        else:
            h = _ple(*ple_args, tm=tm, final_g=row_vec(final_norm_g))
    return h.reshape(b, s, d).astype(x.dtype)
```

```python
import functools
import math
from typing import NamedTuple, Optional

import numpy as np
import jax
import jax.numpy as jnp
from jax import lax
from jax.experimental import pallas as pl
from jax.experimental.pallas import tpu as pltpu

F32 = jnp.float32
BF16 = jnp.bfloat16

RMS_EPS = 1e-6
HEAD_DIM = 64
MOBA_BLOCK = 256
MOBA_TOP_K = 3
REL_BUCKETS = 32
REL_MAX_DIST = 128

LANES = 128
BF16_SUBLANES = 16
V_ROWS = HEAD_DIM + BF16_SUBLANES
GATE_LANES = 8
LOG2E = math.log2(math.e)
Q_SCALE = HEAD_DIM ** -0.5 * LOG2E
FOX_BLK = 512
MOBA_Q_TILE = 2 * MOBA_BLOCK
CUMSUM_BLK = 512
DENSE_ROW_TILE = 512
PROJ_COL_CHUNK = 512
FFN_COL_CHUNK = 256
VMEM_LIMIT_BYTES = 56 * 1024 * 1024
NEG_INF = float("-inf")


def _params(n_axes):
    return pltpu.CompilerParams(
        dimension_semantics=("arbitrary",) * n_axes,
        vmem_limit_bytes=VMEM_LIMIT_BYTES)


def _split3(x):
    x1 = x.astype(BF16)
    r1 = x - x1.astype(F32)
    x2 = r1.astype(BF16)
    x3 = (r1 - x2.astype(F32)).astype(BF16)
    return x1, x2, x3


def _rmsnorm(x, g):
    ms = jnp.mean(x * x, axis=-1, keepdims=True)
    return x * lax.rsqrt(ms + RMS_EPS) * g


def _log_sigmoid(x):
    return jnp.minimum(x, 0.0) - jnp.log1p(jnp.exp(-jnp.abs(x)))


def _resident(shape):
    return pl.BlockSpec(shape, lambda i: (0,) * len(shape), pipeline_mode=pl.Buffered(1))


class _Mixer(NamedTuple):
    g: jax.Array
    w_qt: jax.Array
    w_kv: jax.Array
    w_f: Optional[jax.Array] = None
    b_f: Optional[jax.Array] = None

    def operands(self):
        return tuple(a for a in self if a is not None)

    def in_specs(self):
        return [_resident(a.shape) for a in self.operands()]

    def out_specs(self, tm):
        d = self.w_qt.shape[0]
        specs = [pl.BlockSpec((d, tm), lambda i: (0, i)),
                 pl.BlockSpec((tm, 2 * d), lambda i: (i, 0))]
        if self.w_f is not None:
            specs.append(pl.BlockSpec((tm, LANES), lambda i: (i, 0)))
        return specs

    def out_shapes(self, n):
        d = self.w_qt.shape[0]
        shapes = [jax.ShapeDtypeStruct((d, n), BF16), jax.ShapeDtypeStruct((n, 2 * d), BF16)]
        if self.w_f is not None:
            shapes.append(jax.ShapeDtypeStruct((n, LANES), F32))
        return shapes


def _chunks(total, want):
    step = want if total % want == 0 else total
    return [slice(c * step, (c + 1) * step) for c in range(total // step)]


def _mixer_inputs(y, mixer_refs, out_refs):
    g_ref, wqt_ref, wkv_ref = mixer_refs[:3]
    u = _rmsnorm(y, g_ref[...]).astype(BF16)
    for rows in _chunks(wqt_ref.shape[0], PROJ_COL_CHUNK):
        out_refs[0][rows, :] = lax.dot_general(
            wqt_ref[rows, :], u, (((1,), (1,)), ((), ())),
            preferred_element_type=F32).astype(BF16)
    for cols in _chunks(wkv_ref.shape[1], PROJ_COL_CHUNK):
        out_refs[1][:, cols] = jnp.dot(u, wkv_ref[:, cols],
                                       preferred_element_type=F32).astype(BF16)
    if len(mixer_refs) > 3:
        wf_ref, bf_ref = mixer_refs[3:]
        f_logit = jnp.dot(u, wf_ref[...], preferred_element_type=F32) + bf_ref[...]
        out_refs[2][...] = _log_sigmoid(f_logit)


def _proj_kernel(x_ref, *refs, n_mixer):
    _mixer_inputs(x_ref[...], refs[:n_mixer], refs[n_mixer:])


def _project(h2, mixer, *, tm):
    n, d = h2.shape
    n_mixer = len(mixer.operands())
    return pl.pallas_call(
        functools.partial(_proj_kernel, n_mixer=n_mixer), grid=(n // tm,),
        in_specs=[pl.BlockSpec((tm, d), lambda i: (i, 0))] + mixer.in_specs(),
        out_specs=mixer.out_specs(tm), out_shape=mixer.out_shapes(n),
        compiler_params=_params(1), name="proj")(h2, *mixer.operands())


def _cumsum_kernel(lf_ref, aqt_ref, ak_ref, carry_ref):
    @pl.when(pl.program_id(1) == 0)
    def _():
        carry_ref[...] = jnp.zeros_like(carry_ref)

    t = lf_ref.shape[1]
    row = lax.broadcasted_iota(jnp.int32, (t, t), 0)
    col = lax.broadcasted_iota(jnp.int32, (t, t), 1)
    tril = jnp.where(col <= row, 1.0, 0.0).astype(BF16)
    x1, x2, x3 = _split3(lf_ref[0])
    cs = (jnp.dot(tril, x1, preferred_element_type=F32)
          + jnp.dot(tril, x2, preferred_element_type=F32)
          + jnp.dot(tril, x3, preferred_element_type=F32))
    cs = cs + carry_ref[0:1, :]
    carry_ref[...] = jnp.broadcast_to(cs[t - 1:t, :], carry_ref.shape)

    src = lax.broadcasted_iota(jnp.int32, (LANES, LANES), 0)
    dst = lax.broadcasted_iota(jnp.int32, (LANES, LANES), 1)
    lane = lax.broadcasted_iota(jnp.int32, (1, LANES), 1) & (GATE_LANES - 1)
    aq = jnp.where((lane >= 3) & (lane < 6), 1.0, 0.0)
    ak = jnp.where(lane < 3, 1.0, 0.0)
    for i, part in enumerate(_split3(cs * LOG2E)):
        to_q = jnp.where(dst == GATE_LANES * src + i, 1.0, 0.0).astype(BF16)
        to_k = jnp.where(dst == GATE_LANES * src + 3 + i, 1.0, 0.0).astype(BF16)
        aq = aq + jnp.dot(part, to_q, preferred_element_type=F32)
        ak = ak - jnp.dot(part, to_k, preferred_element_type=F32)
    eye = jnp.where(src == dst, 1.0, 0.0).astype(BF16)
    aqt_ref[0] = lax.dot_general(eye, aq.astype(BF16), (((1,), (1,)), ((), ())),
                                 preferred_element_type=F32).astype(BF16)
    ak_ref[0] = ak.astype(BF16)


def _cumsum(lf3):
    b, s, _ = lf3.shape
    t = min(CUMSUM_BLK, s)
    spec = pl.BlockSpec((1, t, LANES), lambda i, j: (i, j, 0))
    return pl.pallas_call(
        _cumsum_kernel, grid=(b, s // t),
        in_specs=[spec],
        out_specs=[pl.BlockSpec((1, LANES, t), lambda i, j: (i, 0, j)), spec],
        out_shape=[jax.ShapeDtypeStruct((b, LANES, s), BF16),
                   jax.ShapeDtypeStruct(lf3.shape, BF16)],
        scratch_shapes=[pltpu.VMEM((8, LANES), F32)],
        compiler_params=_params(2), name="gate_cumsum")(lf3)


def _head_lane_mask(hh):
    lane = lax.broadcasted_iota(jnp.int32, (1, LANES), 1)
    return (lane < HEAD_DIM) if hh == 0 else (lane >= HEAD_DIM)


def _transpose_bf16(x):
    return x.astype(F32).T.astype(BF16)


def _value_rows(v2):
    v_t = v2.astype(F32).T
    row = lax.broadcasted_iota(jnp.int32, (V_ROWS - HEAD_DIM, v_t.shape[1]), 0)
    tail = jnp.where(row == 0, 1.0, 0.0)
    return [jnp.concatenate([v_t[hh * HEAD_DIM:(hh + 1) * HEAD_DIM], tail],
                            axis=0).astype(BF16) for hh in range(2)]


def _flash_init(m_ref, acc_ref):
    m_ref[...] = jnp.full(m_ref.shape, NEG_INF, F32)
    acc_ref[...] = jnp.zeros(acc_ref.shape, F32)


def _pipelined_blocks(n, last_block, scores_into, update, final_update=None):
    def step(j_next, j, slot):
        for hh in range(2):
            scores_into(j_next, 1 - slot, hh)
            update(j, slot, hh)

    for hh in range(2):
        scores_into(0, 0, hh)

    def pair(jj, carry):
        j = 2 * jj
        step(jnp.minimum(j + 1, last_block), j, 0)
        step(jnp.minimum(j + 2, last_block), j + 1, 1)
        return carry
    lax.fori_loop(0, n // 2, pair, 0)

    @pl.when(n % 2 == 1)
    def _():
        if final_update is not None:
            step(n, n - 1, 0)
            final_update(1)
        else:
            for hh in range(2):
                update(n - 1, 0, hh)

    if final_update is not None:
        @pl.when(n % 2 == 0)
        def _():
            final_update(0)


def _store_scores(s_t, s_ref, mx_ref, idx):
    s_ref[idx] = s_t
    mx_ref[idx] = jnp.max(s_t, axis=0, keepdims=True)


def _flash_update(s_t, mx, v_rows, m_ref, acc_ref, hh, keep=None, const=None):
    m_old = m_ref[hh]
    if const is not None:
        mx = mx + const
    if keep is not None:
        mx = jnp.where(keep, mx, NEG_INF)
    m_new = jnp.maximum(m_old, mx)
    m_safe = jnp.where(m_new == NEG_INF, 0.0, m_new)
    shift = m_safe if const is None else m_safe - const
    if keep is not None:
        shift = jnp.where(keep, shift, float("inf"))
    p = jnp.exp2(s_t - shift).astype(BF16)
    alpha = jnp.exp2(m_old - m_safe)
    m_ref[hh] = m_new
    if not isinstance(v_rows, (list, tuple)):
        v_rows = [v_rows]
    keys = p.shape[0] // len(v_rows)
    acc = alpha * acc_ref[hh]
    for i, v_i in enumerate(v_rows):
        acc = acc + jnp.dot(v_i, p[i * keys:(i + 1) * keys], preferred_element_type=F32)
    acc_ref[hh] = acc


def _write_heads(o_ref, acc_ref):
    outs = []
    for hh in range(2):
        acc = acc_ref[hh]
        outs.append(acc[:HEAD_DIM] / acc[HEAD_DIM:HEAD_DIM + 1])
    o_t = jnp.concatenate(outs, axis=0)
    o_ref[0] = o_t.T.astype(o_ref.dtype)


def _head_row_mask(hh):
    row = lax.broadcasted_iota(jnp.int32, (LANES, 1), 0)
    return (row < HEAD_DIM) if hh == 0 else (row >= HEAD_DIM)


def _fox_kernel(qt_ref, k_ref, v_ref, aqt_ref, ak_ref, o_ref,
                kaug_ref, vt_ref, s_ref, mx_ref, m_ref, acc_ref):
    hp = pl.program_id(1)
    qi = pl.program_id(2)
    blk = qt_ref.shape[1]
    nblk = k_ref.shape[1] // blk
    row_id = lax.broadcasted_iota(jnp.int32, (LANES, LANES), 0)
    col_id = lax.broadcasted_iota(jnp.int32, (LANES, LANES), 1)

    def gate_move(hh, src, dst):
        base = HEAD_DIM * (1 - hh)
        first = GATE_LANES * (2 * hp + hh)
        hit = (src >= first) & (src < first + GATE_LANES) & (dst - base == src - first)
        return jnp.where(hit, 1.0, 0.0).astype(BF16)

    @pl.when(qi == 0)
    def _():
        def build(jb, carry):
            rows = pl.ds(pl.multiple_of(jb * blk, blk), blk)
            k2 = k_ref[0, rows, :]
            ak = ak_ref[0, rows, :]
            for hh in range(2):
                moved = jnp.dot(ak, gate_move(hh, row_id, col_id),
                                preferred_element_type=F32).astype(BF16)
                kaug_ref[hh, rows, :] = jnp.where(_head_lane_mask(hh), k2, moved)
            for hh, v_rows in enumerate(_value_rows(v_ref[0, rows, :])):
                vt_ref[hh, jb] = v_rows
            return carry
        lax.fori_loop(0, nblk, build, 0)

    q_aug_t = []
    for hh in range(2):
        moved_t = jnp.dot(gate_move(hh, col_id, row_id), aqt_ref[0],
                          preferred_element_type=F32).astype(BF16)
        q_aug_t.append(jnp.where(_head_row_mask(hh), qt_ref[...], moved_t))
    krow = lax.broadcasted_iota(jnp.int32, (blk, blk), 0)
    qcol = lax.broadcasted_iota(jnp.int32, (blk, blk), 1)

    def scores_into(j, slot, hh):
        rows = pl.ds(pl.multiple_of(j * blk, blk), blk)
        s_t = jnp.dot(kaug_ref[hh, rows, :], q_aug_t[hh], preferred_element_type=F32)
        _store_scores(s_t, s_ref, mx_ref, (slot, hh))

    def update(j, slot, hh):
        _flash_update(s_ref[slot, hh], mx_ref[slot, hh], vt_ref[hh, j], m_ref, acc_ref, hh)

    def diagonal_update(slot):
        for hh in range(2):
            _store_scores(jnp.where(krow <= qcol, s_ref[slot, hh], NEG_INF),
                          s_ref, mx_ref, (slot, hh))
            update(qi, slot, hh)

    _flash_init(m_ref, acc_ref)
    _pipelined_blocks(qi, nblk - 1, scores_into, update, diagonal_update)
    _write_heads(o_ref, acc_ref)


def _fox_attention(qt, kv3, aqt3, ak3, n_heads):
    b, s, d2 = kv3.shape
    d = d2 // 2
    assert n_heads * GATE_LANES <= LANES
    blk = min(FOX_BLK, s)
    npair = n_heads // 2
    nq = s // blk
    return pl.pallas_call(
        _fox_kernel, grid=(b, npair, nq),
        in_specs=[
            pl.BlockSpec((LANES, blk), lambda bi, hp, qi: (hp, bi * nq + qi)),
            pl.BlockSpec((1, s, LANES), lambda bi, hp, qi: (bi, 0, hp)),
            pl.BlockSpec((1, s, LANES), lambda bi, hp, qi: (bi, 0, npair + hp)),
            pl.BlockSpec((1, LANES, blk), lambda bi, hp, qi: (bi, 0, qi)),
            pl.BlockSpec((1, s, LANES), lambda bi, hp, qi: (bi, 0, 0)),
        ],
        out_specs=pl.BlockSpec((1, blk, LANES), lambda bi, hp, qi: (bi, qi, hp)),
        out_shape=jax.ShapeDtypeStruct((b, s, d), BF16),
        scratch_shapes=[
            pltpu.VMEM((2, s, LANES), BF16),
            pltpu.VMEM((2, s // blk, V_ROWS, blk), BF16),
            pltpu.VMEM((2, 2, blk, blk), F32),
            pltpu.VMEM((2, 2, 1, blk), F32),
            pltpu.VMEM((2, 1, blk), F32),
            pltpu.VMEM((2, V_ROWS, blk), F32),
        ],
        compiler_params=_params(3), name="fox_attention")(qt, kv3, kv3, aqt3, ak3)


def _t5_bucket_np(n):
    max_exact = REL_BUCKETS // 2
    nf = np.maximum(n, 1).astype(np.float64)
    large = max_exact + (np.log(nf / max_exact) / math.log(REL_MAX_DIST / max_exact)
                         * (REL_BUCKETS - max_exact)).astype(np.int32)
    return np.where(n < max_exact, n, np.minimum(large, REL_BUCKETS - 1)).astype(np.int32)


def _bucket_tiles(blk):
    key = np.arange(blk)[:, None]
    qry = np.arange(blk)[None, :]
    own = np.where(key <= qry, _t5_bucket_np(np.maximum(qry - key, 0)), -1)
    prev = _t5_bucket_np(blk + qry - key)
    return np.stack([own, prev]).astype(np.int32)


def _bias_kernel(tab_ref, bucket_ref, o_ref):
    h = pl.program_id(0)
    bucket = bucket_ref[...]
    acc = jnp.where(bucket < 0, NEG_INF, 0.0).astype(F32)
    for bkt in range(REL_BUCKETS):
        acc = jnp.where(bucket == bkt, tab_ref[bkt, h] * LOG2E, acc)
    o_ref[0] = acc


def _bias_tiles(rel_table, blk):
    n_heads = rel_table.shape[1]
    buckets = jnp.asarray(_bucket_tiles(blk))
    return pl.pallas_call(
        _bias_kernel, grid=(n_heads,),
        in_specs=[pl.BlockSpec(memory_space=pltpu.SMEM),
                  pl.BlockSpec((2, blk, blk), lambda h: (0, 0, 0))],
        out_specs=pl.BlockSpec((1, 2, blk, blk), lambda h: (h, 0, 0, 0)),
        out_shape=jax.ShapeDtypeStruct((n_heads, 2, blk, blk), F32),
        compiler_params=_params(1), name="t5_bias_tiles")(rel_table, buckets)


def _moba_kernel(tab_ref, qt_ref, k_ref, v_ref, bias_ref, o_ref,
                 vt_ref, km_ref, sel_ref, s_ref, mx_ref, near_ref, m_ref, acc_ref):
    hp = pl.program_id(1)
    qi = pl.program_id(2)
    blk = MOBA_BLOCK
    tq = qt_ref.shape[1]
    nblk = k_ref.shape[1] // blk
    assert tq == 2 * blk and blk >= REL_MAX_DIST
    first_own = 2 * qi

    @pl.when(qi == 0)
    def _():
        def build(jb, carry):
            rows = pl.ds(pl.multiple_of(jb * blk, blk), blk)
            km_ref[pl.ds(jb, 1), :] = jnp.mean(k_ref[0, rows, :].astype(F32),
                                                axis=0, keepdims=True)
            for hh, v_rows in enumerate(_value_rows(v_ref[0, rows, :])):
                vt_ref[hh, jb] = v_rows
            return carry
        lax.fori_loop(0, nblk, build, 0)

    qt = qt_ref[...]
    km_parts = _split3(km_ref[...])
    blk_id = lax.broadcasted_iota(jnp.int32, (nblk, tq), 0)
    qpos = lax.broadcasted_iota(jnp.int32, (nblk, tq), 1)
    own = first_own + jnp.where(qpos >= blk, 1, 0)
    past = blk_id < own
    q_m_t = []
    for hh in range(2):
        q_m_t.append(jnp.where(_head_row_mask(hh), qt, jnp.zeros_like(qt)))

        gate = None
        for part in km_parts:
            term = jnp.dot(part, q_m_t[hh], preferred_element_type=F32)
            gate = term if gate is None else gate + term
        work = jnp.where(past, gate, NEG_INF)
        picked = jnp.zeros(gate.shape, F32)
        for _ in range(MOBA_TOP_K):
            best = jnp.max(work, axis=0, keepdims=True)
            first = jnp.min(jnp.where(work == best, blk_id, nblk), axis=0, keepdims=True)
            hit = blk_id == first
            picked = jnp.where(hit, 1.0, picked)
            work = jnp.where(hit, NEG_INF, work)
        keep_all = ((picked > 0.5) & past) | (blk_id == own) | (blk_id == first_own + 1)
        sel_ref[hh] = jnp.where(keep_all, 1.0, 0.0)

    def scores(hh, j):
        rows = pl.ds(pl.multiple_of(j * blk, blk), blk)
        return jnp.dot(k_ref[0, rows, :], q_m_t[hh], preferred_element_type=F32)

    def keep(hh, j):
        return sel_ref[hh, pl.ds(j, 1), :] > 0.5

    _flash_init(m_ref, acc_ref)
    far_bias = [tab_ref[REL_BUCKETS - 1, 2 * hp + hh] * LOG2E for hh in range(2)]

    j_prev = jnp.maximum(first_own - 1, 0)
    neg_tile = jnp.full((blk, blk), NEG_INF, F32)
    near_mx = []
    for hh in range(2):
        own_t, prev_t = bias_ref[hh, 0], bias_ref[hh, 1]
        mask_prev = jnp.where(keep(hh, j_prev) & (qi >= 1), 0.0, NEG_INF)
        mask_own = jnp.where(keep(hh, first_own), 0.0, NEG_INF)
        far_t = jnp.full((blk, blk), far_bias[hh], F32)
        parts = [scores(hh, j_prev) + mask_prev + jnp.concatenate([prev_t, far_t], axis=1),
                 scores(hh, first_own) + mask_own + jnp.concatenate([own_t, prev_t], axis=1),
                 scores(hh, first_own + 1) + jnp.concatenate([neg_tile, own_t], axis=1)]
        mx = None
        for i, part in enumerate(parts):
            near_ref[hh, i * blk:(i + 1) * blk] = part
            part_mx = jnp.max(part, axis=0, keepdims=True)
            mx = part_mx if mx is None else jnp.maximum(mx, part_mx)
        near_mx.append(mx)
    for hh in range(2):
        _flash_update(near_ref[hh], near_mx[hh],
                      [vt_ref[hh, j_prev], vt_ref[hh, first_own], vt_ref[hh, first_own + 1]],
                      m_ref, acc_ref, hh)

    def scores_into(j, slot, hh):
        _store_scores(scores(hh, j), s_ref, mx_ref, (slot, hh))

    def update(j, slot, hh):
        _flash_update(s_ref[slot, hh], mx_ref[slot, hh], vt_ref[hh, j],
                      m_ref, acc_ref, hh, keep=keep(hh, j), const=far_bias[hh])

    _pipelined_blocks(jnp.maximum(first_own - 1, 0), nblk - 1, scores_into, update)
    _write_heads(o_ref, acc_ref)


def _moba_attention(qt, kv3, rel_table, bias_t, n_heads):
    b, s, d2 = kv3.shape
    d = d2 // 2
    blk = MOBA_BLOCK
    tq = MOBA_Q_TILE
    assert s % tq == 0
    npair = n_heads // 2
    nblk = s // blk
    nq = s // tq
    return pl.pallas_call(
        _moba_kernel, grid=(b, npair, nq),
        in_specs=[
            pl.BlockSpec(memory_space=pltpu.SMEM),
            pl.BlockSpec((LANES, tq), lambda bi, hp, qi: (hp, bi * nq + qi)),
            pl.BlockSpec((1, s, LANES), lambda bi, hp, qi: (bi, 0, hp)),
            pl.BlockSpec((1, s, LANES), lambda bi, hp, qi: (bi, 0, npair + hp)),
            pl.BlockSpec((2, 2, blk, blk), lambda bi, hp, qi: (hp, 0, 0, 0)),
        ],
        out_specs=pl.BlockSpec((1, tq, LANES), lambda bi, hp, qi: (bi, qi, hp)),
        out_shape=jax.ShapeDtypeStruct((b, s, d), BF16),
        scratch_shapes=[
            pltpu.VMEM((2, nblk, V_ROWS, blk), BF16),
            pltpu.VMEM((nblk, LANES), F32),
            pltpu.VMEM((2, nblk, tq), F32),
            pltpu.VMEM((2, 2, blk, tq), F32),
            pltpu.VMEM((2, 2, 1, tq), F32),
            pltpu.VMEM((2, 3 * blk, tq), F32),
            pltpu.VMEM((2, 1, tq), F32),
            pltpu.VMEM((2, V_ROWS, tq), F32),
        ],
        compiler_params=_params(3), name="moba_attention")(
            rel_table, qt, kv3, kv3, bias_t)


def _oproj_ffn_kernel(h_ref, o_ref, wo_ref, g_ref, win_ref, wout_ref, out_ref, *, tf):
    d_ff = wout_ref.shape[0]
    h1 = h_ref[...] + jnp.dot(o_ref[...], wo_ref[...], preferred_element_type=F32)
    u = _rmsnorm(h1, g_ref[...]).astype(BF16)
    acc = h1
    for c in range(d_ff // tf):
        gate = jnp.dot(u, win_ref[:, c * tf:(c + 1) * tf], preferred_element_type=F32)
        up = jnp.dot(u, win_ref[:, d_ff + c * tf:d_ff + (c + 1) * tf],
                     preferred_element_type=F32)
        act = (gate * jax.nn.sigmoid(gate) * up).astype(BF16)
        acc = acc + jnp.dot(act, wout_ref[c * tf:(c + 1) * tf, :],
                            preferred_element_type=F32)
    out_ref[...] = acc


def _oproj_ffn(h2, o2, w_o, g, w_in, w_out, *, tm, tf):
    n, d = h2.shape
    d_ff = w_out.shape[0]
    assert d_ff % tf == 0
    row = pl.BlockSpec((tm, d), lambda i: (i, 0))
    return pl.pallas_call(
        functools.partial(_oproj_ffn_kernel, tf=tf), grid=(n // tm,),
        in_specs=[row, row, _resident((d, d)), _resident((1, d)),
                  _resident((d, 2 * d_ff)), _resident((d_ff, d))],
        out_specs=row,
        out_shape=jax.ShapeDtypeStruct((n, d), F32),
        compiler_params=_params(1), name="oproj_ffn")(h2, o2, w_o, g, w_in, w_out)


def _ple_update(x_ref, g_ref, wg_ref, p_ref, wu_ref):
    x = x_ref[...]
    u = _rmsnorm(x, g_ref[...]).astype(BF16)
    gate = jax.nn.sigmoid(jnp.dot(u, wg_ref[...], preferred_element_type=F32))
    up = jnp.dot(p_ref[...].astype(BF16), wu_ref[...], preferred_element_type=F32)
    return x + gate * up


def _ple_next_kernel(x_ref, g_ref, wg_ref, p_ref, wu_ref, *refs, n_mixer):
    y = _ple_update(x_ref, g_ref, wg_ref, p_ref, wu_ref)
    refs[n_mixer][...] = y
    _mixer_inputs(y, refs[:n_mixer], refs[n_mixer + 1:])


def _ple_final_kernel(x_ref, g_ref, wg_ref, p_ref, wu_ref, fg_ref, out_ref):
    y = _ple_update(x_ref, g_ref, wg_ref, p_ref, wu_ref)
    out_ref[...] = _rmsnorm(y, fg_ref[...])


def _ple(h2, g, w_gate, p3, layer, w_up, *, tm, mixer=None, final_g=None):
    n, d = h2.shape
    pd = p3.shape[2]
    row = pl.BlockSpec((tm, d), lambda i: (i, 0))
    in_specs = [row, _resident((1, d)), _resident((d, d)),
                pl.BlockSpec((None, tm, pd), lambda i: (layer, i, 0)), _resident((pd, d))]
    h_shape = jax.ShapeDtypeStruct((n, d), F32)
    if mixer is None:
        return pl.pallas_call(
            _ple_final_kernel, grid=(n // tm,),
            in_specs=in_specs + [_resident((1, d))], out_specs=row, out_shape=h_shape,
            compiler_params=_params(1), name="ple_final")(h2, g, w_gate, p3, w_up, final_g)
    n_mixer = len(mixer.operands())
    return pl.pallas_call(
        functools.partial(_ple_next_kernel, n_mixer=n_mixer), grid=(n // tm,),
        in_specs=in_specs + mixer.in_specs(),
        out_specs=[row] + mixer.out_specs(tm), out_shape=[h_shape] + mixer.out_shapes(n),
        compiler_params=_params(1), name="ple_proj")(h2, g, w_gate, p3, w_up,
                                                     *mixer.operands())


def _row_tile(n, want):
    t = min(want, n)
    assert n % t == 0
    return t


def _col_tile(n, want):
    t = min(want, n)
    while n % t:
        t -= LANES
    return t


def kernel(x, p, attn_norm_g, fox_w_in, fox_b_f, fox_w_o, moba_w_in, moba_w_o, rel_bias_table,
           ffn_norm_g, ffn_w_in, ffn_w_out, ple_norm_g, ple_w_gate, ple_w_up, final_norm_g):
    b, s, d = x.shape
    depth = p.shape[0]
    n_heads = rel_bias_table.shape[1]
    assert d == n_heads * HEAD_DIM and n_heads % 2 == 0 and n_heads <= LANES
    n = b * s
    tm = _row_tile(n, DENSE_ROW_TILE)
    tf = _col_tile(ffn_w_out.shape[1], FFN_COL_CHUNK)

    def row_vec(v):
        return v.reshape(1, -1).astype(F32)

    def mixer(i):
        w = fox_w_in[i // 2] if i % 2 == 0 else moba_w_in[i // 2]
        parts = [row_vec(attn_norm_g[i]), (w[:, :d] * Q_SCALE).T.astype(BF16),
                 w[:, d:3 * d].astype(BF16)]
        if i % 2 == 0:
            parts += [jnp.pad(w[:, 3 * d:], ((0, 0), (0, LANES - n_heads))).astype(BF16),
                      jnp.pad(row_vec(fox_b_f[i // 2]), ((0, 0), (0, LANES - n_heads)))]
        return _Mixer(*parts)

    rel_table = rel_bias_table.astype(F32)
    bias_t = _bias_tiles(rel_table, MOBA_BLOCK)

    h = x.reshape(n, d).astype(F32)
    mixed = _project(h, mixer(0), tm=tm)
    for i in range(depth):
        qt, kv3 = mixed[0], mixed[1].reshape(b, s, 2 * d)
        if i % 2 == 0:
            aqt, ak = _cumsum(mixed[2].reshape(b, s, LANES))
            o = _fox_attention(qt, kv3, aqt, ak, n_heads)
            w_o = fox_w_o[i // 2]
        else:
            o = _moba_attention(qt, kv3, rel_table, bias_t, n_heads)
            w_o = moba_w_o[i // 2]
        h = _oproj_ffn(h, o.reshape(n, d), w_o.astype(BF16), row_vec(ffn_norm_g[i]),
                       ffn_w_in[i].astype(BF16), ffn_w_out[i].astype(BF16), tm=tm, tf=tf)
        ple_args = (h, row_vec(ple_norm_g[i]), ple_w_gate[i].astype(BF16),
                    p.reshape(depth, n, -1), i, ple_w_up[i].astype(BF16))
        if i + 1 < depth:
            h, *mixed = _ple(*ple_args, tm=tm, mixer=mixer(i + 1))
        else:
            h = _ple(*ple_args, tm=tm, final_g=row_vec(final_norm_g))
    return h.reshape(b, s, d).astype(x.dtype)
```

```python
import functools
import math
from typing import NamedTuple, Optional

import numpy as np
import jax
import jax.numpy as jnp
from jax import lax
from jax.experimental import pallas as pl
from jax.experimental.pallas import tpu as pltpu

F32 = jnp.float32
BF16 = jnp.bfloat16

RMS_EPS = 1e-6
HEAD_DIM = 64
MOBA_BLOCK = 256
MOBA_TOP_K = 3
REL_BUCKETS = 32
REL_MAX_DIST = 128

LANES = 128
BF16_SUBLANES = 16
V_ROWS = HEAD_DIM + BF16_SUBLANES
GATE_LANES = 8
LOG2E = math.log2(math.e)
Q_SCALE = HEAD_DIM ** -0.5 * LOG2E
FOX_BLK = 512
MOBA_Q_TILE = 2 * MOBA_BLOCK
CUMSUM_BLK = 512
DENSE_ROW_TILE = 512
PROJ_COL_CHUNK = 512
FFN_COL_CHUNK = 256
VMEM_LIMIT_BYTES = 56 * 1024 * 1024
NEG_INF = float("-inf")


def _params(n_axes):
    return pltpu.CompilerParams(
        dimension_semantics=("arbitrary",) * n_axes,
        vmem_limit_bytes=VMEM_LIMIT_BYTES)


def _split3(x):
    x1 = x.astype(BF16)
    r1 = x - x1.astype(F32)
    x2 = r1.astype(BF16)
    x3 = (r1 - x2.astype(F32)).astype(BF16)
    return x1, x2, x3


def _rmsnorm(x, g):
    ms = jnp.mean(x * x, axis=-1, keepdims=True)
    return x * lax.rsqrt(ms + RMS_EPS) * g


def _log_sigmoid(x):
    return jnp.minimum(x, 0.0) - jnp.log1p(jnp.exp(-jnp.abs(x)))


def _resident(shape):
    return pl.BlockSpec(shape, lambda i: (0,) * len(shape), pipeline_mode=pl.Buffered(1))


class _Mixer(NamedTuple):
    g: jax.Array
    w_qt: jax.Array
    w_k: jax.Array
    w_vt: jax.Array
    w_f: Optional[jax.Array] = None
    b_f: Optional[jax.Array] = None

    def operands(self):
        return tuple(a for a in self if a is not None)

    def in_specs(self):
        return [_resident(a.shape) for a in self.operands()]

    def out_specs(self, tm):
        d = self.w_qt.shape[0]
        transposed = pl.BlockSpec((d, tm), lambda i: (0, i))
        specs = [transposed, pl.BlockSpec((tm, d), lambda i: (i, 0)), transposed]
        if self.w_f is not None:
            specs.append(pl.BlockSpec((tm, LANES), lambda i: (i, 0)))
        return specs

    def out_shapes(self, n):
        d = self.w_qt.shape[0]
        shapes = [jax.ShapeDtypeStruct((d, n), BF16), jax.ShapeDtypeStruct((n, d), BF16),
                  jax.ShapeDtypeStruct((d, n), BF16)]
        if self.w_f is not None:
            shapes.append(jax.ShapeDtypeStruct((n, LANES), F32))
        return shapes


def _chunks(total, want):
    step = want if total % want == 0 else total
    return [slice(c * step, (c + 1) * step) for c in range(total // step)]


def _mixer_inputs(y, mixer_refs, out_refs):
    g_ref, wqt_ref, wk_ref, wvt_ref = mixer_refs[:4]
    u = _rmsnorm(y, g_ref[...]).astype(BF16)
    for wt_ref, out_ref in ((wqt_ref, out_refs[0]), (wvt_ref, out_refs[2])):
        for rows in _chunks(wt_ref.shape[0], PROJ_COL_CHUNK):
            out_ref[rows, :] = lax.dot_general(
                wt_ref[rows, :], u, (((1,), (1,)), ((), ())),
                preferred_element_type=F32).astype(BF16)
    for cols in _chunks(wk_ref.shape[1], PROJ_COL_CHUNK):
        out_refs[1][:, cols] = jnp.dot(u, wk_ref[:, cols],
                                       preferred_element_type=F32).astype(BF16)
    if len(mixer_refs) > 4:
        wf_ref, bf_ref = mixer_refs[4:]
        f_logit = jnp.dot(u, wf_ref[...], preferred_element_type=F32) + bf_ref[...]
        out_refs[3][...] = _log_sigmoid(f_logit)


def _proj_kernel(x_ref, *refs, n_mixer):
    _mixer_inputs(x_ref[...], refs[:n_mixer], refs[n_mixer:])


def _project(h2, mixer, *, tm):
    n, d = h2.shape
    n_mixer = len(mixer.operands())
    return pl.pallas_call(
        functools.partial(_proj_kernel, n_mixer=n_mixer), grid=(n // tm,),
        in_specs=[pl.BlockSpec((tm, d), lambda i: (i, 0))] + mixer.in_specs(),
        out_specs=mixer.out_specs(tm), out_shape=mixer.out_shapes(n),
        compiler_params=_params(1), name="proj")(h2, *mixer.operands())


def _cumsum_kernel(lf_ref, aqt_ref, ak_ref, carry_ref):
    @pl.when(pl.program_id(1) == 0)
    def _():
        carry_ref[...] = jnp.zeros_like(carry_ref)

    t = lf_ref.shape[1]
    row = lax.broadcasted_iota(jnp.int32, (t, t), 0)
    col = lax.broadcasted_iota(jnp.int32, (t, t), 1)
    tril = jnp.where(col <= row, 1.0, 0.0).astype(BF16)
    x1, x2, x3 = _split3(lf_ref[0])
    cs = (jnp.dot(tril, x1, preferred_element_type=F32)
          + jnp.dot(tril, x2, preferred_element_type=F32)
          + jnp.dot(tril, x3, preferred_element_type=F32))
    cs = cs + carry_ref[0:1, :]
    carry_ref[...] = jnp.broadcast_to(cs[t - 1:t, :], carry_ref.shape)

    src = lax.broadcasted_iota(jnp.int32, (LANES, LANES), 0)
    dst = lax.broadcasted_iota(jnp.int32, (LANES, LANES), 1)
    lane = lax.broadcasted_iota(jnp.int32, (1, LANES), 1) & (GATE_LANES - 1)
    aq = jnp.where((lane >= 3) & (lane < 6), 1.0, 0.0)
    ak = jnp.where(lane < 3, 1.0, 0.0)
    for i, part in enumerate(_split3(cs * LOG2E)):
        to_q = jnp.where(dst == GATE_LANES * src + i, 1.0, 0.0).astype(BF16)
        to_k = jnp.where(dst == GATE_LANES * src + 3 + i, 1.0, 0.0).astype(BF16)
        aq = aq + jnp.dot(part, to_q, preferred_element_type=F32)
        ak = ak - jnp.dot(part, to_k, preferred_element_type=F32)
    eye = jnp.where(src == dst, 1.0, 0.0).astype(BF16)
    aqt_ref[0] = lax.dot_general(eye, aq.astype(BF16), (((1,), (1,)), ((), ())),
                                 preferred_element_type=F32).astype(BF16)
    ak_ref[0] = ak.astype(BF16)


def _cumsum(lf3):
    b, s, _ = lf3.shape
    t = min(CUMSUM_BLK, s)
    spec = pl.BlockSpec((1, t, LANES), lambda i, j: (i, j, 0))
    return pl.pallas_call(
        _cumsum_kernel, grid=(b, s // t),
        in_specs=[spec],
        out_specs=[pl.BlockSpec((1, LANES, t), lambda i, j: (i, 0, j)), spec],
        out_shape=[jax.ShapeDtypeStruct((b, LANES, s), BF16),
                   jax.ShapeDtypeStruct(lf3.shape, BF16)],
        scratch_shapes=[pltpu.VMEM((8, LANES), F32)],
        compiler_params=_params(2), name="gate_cumsum")(lf3)


def _head_lane_mask(hh):
    lane = lax.broadcasted_iota(jnp.int32, (1, LANES), 1)
    return (lane < HEAD_DIM) if hh == 0 else (lane >= HEAD_DIM)


def _transpose_bf16(x):
    return x.astype(F32).T.astype(BF16)


def _fill_value_rows(vt_ref, v_ref, blk):
    row = lax.broadcasted_iota(jnp.int32, (V_ROWS - HEAD_DIM, blk), 0)
    tail = jnp.where(row == 0, 1.0, 0.0).astype(BF16)
    for hh in range(2):
        for jb in range(v_ref.shape[1] // blk):
            head_rows = v_ref[hh * HEAD_DIM:(hh + 1) * HEAD_DIM, jb * blk:(jb + 1) * blk]
            vt_ref[hh, jb] = jnp.concatenate([head_rows, tail], axis=0)


def _flash_init(m_ref, acc_ref):
    m_ref[...] = jnp.full(m_ref.shape, NEG_INF, F32)
    acc_ref[...] = jnp.zeros(acc_ref.shape, F32)


def _pipelined_blocks(n, last_block, scores_into, update, final_update=None):
    def step(j_next, j, slot):
        for hh in range(2):
            scores_into(j_next, 1 - slot, hh)
            update(j, slot, hh)

    for hh in range(2):
        scores_into(0, 0, hh)

    def pair(jj, carry):
        j = 2 * jj
        step(jnp.minimum(j + 1, last_block), j, 0)
        step(jnp.minimum(j + 2, last_block), j + 1, 1)
        return carry
    lax.fori_loop(0, n // 2, pair, 0)

    @pl.when(n % 2 == 1)
    def _():
        if final_update is not None:
            step(n, n - 1, 0)
            final_update(1)
        else:
            for hh in range(2):
                update(n - 1, 0, hh)

    if final_update is not None:
        @pl.when(n % 2 == 0)
        def _():
            final_update(0)


def _store_scores(s_t, s_ref, mx_ref, idx):
    s_ref[idx] = s_t
    mx_ref[idx] = jnp.max(s_t, axis=0, keepdims=True)


def _flash_update(s_t, mx, v_rows, m_ref, acc_ref, hh, keep=None, const=None):
    m_old = m_ref[hh]
    if const is not None:
        mx = mx + const
    if keep is not None:
        mx = jnp.where(keep, mx, NEG_INF)
    m_new = jnp.maximum(m_old, mx)
    m_safe = jnp.where(m_new == NEG_INF, 0.0, m_new)
    shift = m_safe if const is None else m_safe - const
    if keep is not None:
        shift = jnp.where(keep, shift, float("inf"))
    p = jnp.exp2(s_t - shift).astype(BF16)
    alpha = jnp.exp2(m_old - m_safe)
    m_ref[hh] = m_new
    if not isinstance(v_rows, (list, tuple)):
        v_rows = [v_rows]
    keys = p.shape[0] // len(v_rows)
    acc = alpha * acc_ref[hh]
    for i, v_i in enumerate(v_rows):
        acc = acc + jnp.dot(v_i, p[i * keys:(i + 1) * keys], preferred_element_type=F32)
    acc_ref[hh] = acc


def _write_heads(o_ref, acc_ref):
    outs = []
    for hh in range(2):
        acc = acc_ref[hh]
        outs.append(acc[:HEAD_DIM] / acc[HEAD_DIM:HEAD_DIM + 1])
    o_ref[...] = jnp.concatenate(outs, axis=0).astype(o_ref.dtype)


def _head_row_mask(hh):
    row = lax.broadcasted_iota(jnp.int32, (LANES, 1), 0)
    return (row < HEAD_DIM) if hh == 0 else (row >= HEAD_DIM)


def _fox_kernel(qt_ref, k_ref, v_ref, aqt_ref, ak_ref, o_ref,
                kaug_ref, vt_ref, s_ref, mx_ref, m_ref, acc_ref):
    hp = pl.program_id(1)
    qi = pl.program_id(2)
    blk = qt_ref.shape[1]
    nblk = k_ref.shape[1] // blk
    row_id = lax.broadcasted_iota(jnp.int32, (LANES, LANES), 0)
    col_id = lax.broadcasted_iota(jnp.int32, (LANES, LANES), 1)

    def gate_move(hh, src, dst):
        base = HEAD_DIM * (1 - hh)
        first = GATE_LANES * (2 * hp + hh)
        hit = (src >= first) & (src < first + GATE_LANES) & (dst - base == src - first)
        return jnp.where(hit, 1.0, 0.0).astype(BF16)

    @pl.when(qi == 0)
    def _():
        def build(jb, carry):
            rows = pl.ds(pl.multiple_of(jb * blk, blk), blk)
            k2 = k_ref[0, rows, :]
            ak = ak_ref[0, rows, :]
            for hh in range(2):
                moved = jnp.dot(ak, gate_move(hh, row_id, col_id),
                                preferred_element_type=F32).astype(BF16)
                kaug_ref[hh, rows, :] = jnp.where(_head_lane_mask(hh), k2, moved)
            return carry
        lax.fori_loop(0, nblk, build, 0)
        _fill_value_rows(vt_ref, v_ref, blk)

    q_aug_t = []
    for hh in range(2):
        moved_t = jnp.dot(gate_move(hh, col_id, row_id), aqt_ref[0],
                          preferred_element_type=F32).astype(BF16)
        q_aug_t.append(jnp.where(_head_row_mask(hh), qt_ref[...], moved_t))
    krow = lax.broadcasted_iota(jnp.int32, (blk, blk), 0)
    qcol = lax.broadcasted_iota(jnp.int32, (blk, blk), 1)

    def scores_into(j, slot, hh):
        rows = pl.ds(pl.multiple_of(j * blk, blk), blk)
        s_t = jnp.dot(kaug_ref[hh, rows, :], q_aug_t[hh], preferred_element_type=F32)
        _store_scores(s_t, s_ref, mx_ref, (slot, hh))

    def update(j, slot, hh):
        _flash_update(s_ref[slot, hh], mx_ref[slot, hh], vt_ref[hh, j], m_ref, acc_ref, hh)

    def diagonal_update(slot):
        for hh in range(2):
            _store_scores(jnp.where(krow <= qcol, s_ref[slot, hh], NEG_INF),
                          s_ref, mx_ref, (slot, hh))
            update(qi, slot, hh)

    _flash_init(m_ref, acc_ref)
    _pipelined_blocks(qi, nblk - 1, scores_into, update, diagonal_update)
    _write_heads(o_ref, acc_ref)


def _fox_attention(qt, k3, vt, aqt3, ak3, n_heads):
    b, s, d = k3.shape
    assert n_heads * GATE_LANES <= LANES
    blk = min(FOX_BLK, s)
    npair = n_heads // 2
    nq = s // blk
    tile_t = pl.BlockSpec((LANES, blk), lambda bi, hp, qi: (hp, bi * nq + qi))
    return pl.pallas_call(
        _fox_kernel, grid=(b, npair, nq),
        in_specs=[
            tile_t,
            pl.BlockSpec((1, s, LANES), lambda bi, hp, qi: (bi, 0, hp)),
            pl.BlockSpec((LANES, s), lambda bi, hp, qi: (hp, bi)),
            pl.BlockSpec((1, LANES, blk), lambda bi, hp, qi: (bi, 0, qi)),
            pl.BlockSpec((1, s, LANES), lambda bi, hp, qi: (bi, 0, 0)),
        ],
        out_specs=tile_t,
        out_shape=jax.ShapeDtypeStruct((d, b * s), BF16),
        scratch_shapes=[
            pltpu.VMEM((2, s, LANES), BF16),
            pltpu.VMEM((2, s // blk, V_ROWS, blk), BF16),
            pltpu.VMEM((2, 2, blk, blk), F32),
            pltpu.VMEM((2, 2, 1, blk), F32),
            pltpu.VMEM((2, 1, blk), F32),
            pltpu.VMEM((2, V_ROWS, blk), F32),
        ],
        compiler_params=_params(3), name="fox_attention")(qt, k3, vt, aqt3, ak3)


def _t5_bucket_np(n):
    max_exact = REL_BUCKETS // 2
    nf = np.maximum(n, 1).astype(np.float64)
    large = max_exact + (np.log(nf / max_exact) / math.log(REL_MAX_DIST / max_exact)
                         * (REL_BUCKETS - max_exact)).astype(np.int32)
    return np.where(n < max_exact, n, np.minimum(large, REL_BUCKETS - 1)).astype(np.int32)


def _bucket_tiles(blk):
    key = np.arange(blk)[:, None]
    qry = np.arange(blk)[None, :]
    own = np.where(key <= qry, _t5_bucket_np(np.maximum(qry - key, 0)), -1)
    prev = _t5_bucket_np(blk + qry - key)
    return np.stack([own, prev]).astype(np.int32)


def _bias_kernel(tab_ref, bucket_ref, o_ref):
    h = pl.program_id(0)
    bucket = bucket_ref[...]
    acc = jnp.where(bucket < 0, NEG_INF, 0.0).astype(F32)
    for bkt in range(REL_BUCKETS):
        acc = jnp.where(bucket == bkt, tab_ref[bkt, h] * LOG2E, acc)
    o_ref[0] = acc


def _bias_tiles(rel_table, blk):
    n_heads = rel_table.shape[1]
    buckets = jnp.asarray(_bucket_tiles(blk))
    return pl.pallas_call(
        _bias_kernel, grid=(n_heads,),
        in_specs=[pl.BlockSpec(memory_space=pltpu.SMEM),
                  pl.BlockSpec((2, blk, blk), lambda h: (0, 0, 0))],
        out_specs=pl.BlockSpec((1, 2, blk, blk), lambda h: (h, 0, 0, 0)),
        out_shape=jax.ShapeDtypeStruct((n_heads, 2, blk, blk), F32),
        compiler_params=_params(1), name="t5_bias_tiles")(rel_table, buckets)


def _moba_kernel(tab_ref, qt_ref, k_ref, v_ref, bias_ref, o_ref,
                 vt_ref, km_ref, sel_ref, s_ref, mx_ref, near_ref, m_ref, acc_ref):
    hp = pl.program_id(1)
    qi = pl.program_id(2)
    blk = MOBA_BLOCK
    tq = qt_ref.shape[1]
    nblk = k_ref.shape[1] // blk
    assert tq == 2 * blk and blk >= REL_MAX_DIST
    first_own = 2 * qi

    @pl.when(qi == 0)
    def _():
        def build(jb, carry):
            rows = pl.ds(pl.multiple_of(jb * blk, blk), blk)
            km_ref[pl.ds(jb, 1), :] = jnp.mean(k_ref[0, rows, :].astype(F32),
                                                axis=0, keepdims=True)
            return carry
        lax.fori_loop(0, nblk, build, 0)
        _fill_value_rows(vt_ref, v_ref, blk)

    qt = qt_ref[...]
    km_parts = _split3(km_ref[...])
    blk_id = lax.broadcasted_iota(jnp.int32, (nblk, tq), 0)
    qpos = lax.broadcasted_iota(jnp.int32, (nblk, tq), 1)
    own = first_own + jnp.where(qpos >= blk, 1, 0)
    past = blk_id < own
    q_m_t = []
    for hh in range(2):
        q_m_t.append(jnp.where(_head_row_mask(hh), qt, jnp.zeros_like(qt)))

        gate = None
        for part in km_parts:
            term = jnp.dot(part, q_m_t[hh], preferred_element_type=F32)
            gate = term if gate is None else gate + term
        work = jnp.where(past, gate, NEG_INF)
        picked = jnp.zeros(gate.shape, F32)
        for _ in range(MOBA_TOP_K):
            best = jnp.max(work, axis=0, keepdims=True)
            first = jnp.min(jnp.where(work == best, blk_id, nblk), axis=0, keepdims=True)
            hit = blk_id == first
            picked = jnp.where(hit, 1.0, picked)
            work = jnp.where(hit, NEG_INF, work)
        keep_all = ((picked > 0.5) & past) | (blk_id == own) | (blk_id == first_own + 1)
        sel_ref[hh] = jnp.where(keep_all, 1.0, 0.0)

    def scores(hh, j):
        rows = pl.ds(pl.multiple_of(j * blk, blk), blk)
        return jnp.dot(k_ref[0, rows, :], q_m_t[hh], preferred_element_type=F32)

    def keep(hh, j):
        return sel_ref[hh, pl.ds(j, 1), :] > 0.5

    _flash_init(m_ref, acc_ref)
    far_bias = [tab_ref[REL_BUCKETS - 1, 2 * hp + hh] * LOG2E for hh in range(2)]

    j_prev = jnp.maximum(first_own - 1, 0)
    neg_tile = jnp.full((blk, blk), NEG_INF, F32)
    near_mx = []
    for hh in range(2):
        own_t, prev_t = bias_ref[hh, 0], bias_ref[hh, 1]
        mask_prev = jnp.where(keep(hh, j_prev) & (qi >= 1), 0.0, NEG_INF)
        mask_own = jnp.where(keep(hh, first_own), 0.0, NEG_INF)
        far_t = jnp.full((blk, blk), far_bias[hh], F32)
        parts = [scores(hh, j_prev) + mask_prev + jnp.concatenate([prev_t, far_t], axis=1),
                 scores(hh, first_own) + mask_own + jnp.concatenate([own_t, prev_t], axis=1),
                 scores(hh, first_own + 1) + jnp.concatenate([neg_tile, own_t], axis=1)]
        mx = None
        for i, part in enumerate(parts):
            near_ref[hh, i * blk:(i + 1) * blk] = part
            part_mx = jnp.max(part, axis=0, keepdims=True)
            mx = part_mx if mx is None else jnp.maximum(mx, part_mx)
        near_mx.append(mx)
    for hh in range(2):
        _flash_update(near_ref[hh], near_mx[hh],
                      [vt_ref[hh, j_prev], vt_ref[hh, first_own], vt_ref[hh, first_own + 1]],
                      m_ref, acc_ref, hh)

    def scores_into(j, slot, hh):
        _store_scores(scores(hh, j), s_ref, mx_ref, (slot, hh))

    def update(j, slot, hh):
        _flash_update(s_ref[slot, hh], mx_ref[slot, hh], vt_ref[hh, j],
                      m_ref, acc_ref, hh, keep=keep(hh, j), const=far_bias[hh])

    _pipelined_blocks(jnp.maximum(first_own - 1, 0), nblk - 1, scores_into, update)
    _write_heads(o_ref, acc_ref)


def _moba_attention(qt, k3, vt, rel_table, bias_t, n_heads):
    b, s, d = k3.shape
    blk = MOBA_BLOCK
    tq = MOBA_Q_TILE
    assert s % tq == 0
    npair = n_heads // 2
    nblk = s // blk
    nq = s // tq
    tile_t = pl.BlockSpec((LANES, tq), lambda bi, hp, qi: (hp, bi * nq + qi))
    return pl.pallas_call(
        _moba_kernel, grid=(b, npair, nq),
        in_specs=[
            pl.BlockSpec(memory_space=pltpu.SMEM),
            tile_t,
            pl.BlockSpec((1, s, LANES), lambda bi, hp, qi: (bi, 0, hp)),
            pl.BlockSpec((LANES, s), lambda bi, hp, qi: (hp, bi)),
            pl.BlockSpec((2, 2, blk, blk), lambda bi, hp, qi: (hp, 0, 0, 0)),
        ],
        out_specs=tile_t,
        out_shape=jax.ShapeDtypeStruct((d, b * s), BF16),
        scratch_shapes=[
            pltpu.VMEM((2, nblk, V_ROWS, blk), BF16),
            pltpu.VMEM((nblk, LANES), F32),
            pltpu.VMEM((2, nblk, tq), F32),
            pltpu.VMEM((2, 2, blk, tq), F32),
            pltpu.VMEM((2, 2, 1, tq), F32),
            pltpu.VMEM((2, 3 * blk, tq), F32),
            pltpu.VMEM((2, 1, tq), F32),
            pltpu.VMEM((2, V_ROWS, tq), F32),
        ],
        compiler_params=_params(3), name="moba_attention")(
            rel_table, qt, k3, vt, bias_t)


def _oproj_ffn_kernel(h_ref, ot_ref, wo_ref, g_ref, win_ref, wout_ref, out_ref, *, tf):
    d_ff = wout_ref.shape[0]
    h1 = h_ref[...] + lax.dot_general(ot_ref[...], wo_ref[...], (((0,), (0,)), ((), ())),
                                      preferred_element_type=F32)
    u = _rmsnorm(h1, g_ref[...]).astype(BF16)
    acc = h1
    for c in range(d_ff // tf):
        gate = jnp.dot(u, win_ref[:, c * tf:(c + 1) * tf], preferred_element_type=F32)
        up = jnp.dot(u, win_ref[:, d_ff + c * tf:d_ff + (c + 1) * tf],
                     preferred_element_type=F32)
        act = (gate * jax.nn.sigmoid(gate) * up).astype(BF16)
        acc = acc + jnp.dot(act, wout_ref[c * tf:(c + 1) * tf, :],
                            preferred_element_type=F32)
    out_ref[...] = acc


def _oproj_ffn(h2, o_t, w_o, g, w_in, w_out, *, tm, tf):
    n, d = h2.shape
    d_ff = w_out.shape[0]
    assert d_ff % tf == 0
    row = pl.BlockSpec((tm, d), lambda i: (i, 0))
    return pl.pallas_call(
        functools.partial(_oproj_ffn_kernel, tf=tf), grid=(n // tm,),
        in_specs=[row, pl.BlockSpec((d, tm), lambda i: (0, i)), _resident((d, d)),
                  _resident((1, d)), _resident((d, 2 * d_ff)), _resident((d_ff, d))],
        out_specs=row,
        out_shape=jax.ShapeDtypeStruct((n, d), F32),
        compiler_params=_params(1), name="oproj_ffn")(h2, o_t, w_o, g, w_in, w_out)


def _ple_update(x_ref, g_ref, wg_ref, p_ref, wu_ref):
    x = x_ref[...]
    u = _rmsnorm(x, g_ref[...]).astype(BF16)
    gate = jax.nn.sigmoid(jnp.dot(u, wg_ref[...], preferred_element_type=F32))
    up = jnp.dot(p_ref[...].astype(BF16), wu_ref[...], preferred_element_type=F32)
    return x + gate * up


def _ple_next_kernel(x_ref, g_ref, wg_ref, p_ref, wu_ref, *refs, n_mixer):
    y = _ple_update(x_ref, g_ref, wg_ref, p_ref, wu_ref)
    refs[n_mixer][...] = y
    _mixer_inputs(y, refs[:n_mixer], refs[n_mixer + 1:])


def _ple_final_kernel(x_ref, g_ref, wg_ref, p_ref, wu_ref, fg_ref, out_ref):
    y = _ple_update(x_ref, g_ref, wg_ref, p_ref, wu_ref)
    out_ref[...] = _rmsnorm(y, fg_ref[...])


def _ple(h2, g, w_gate, p3, layer, w_up, *, tm, mixer=None, final_g=None):
    n, d = h2.shape
    pd = p3.shape[2]
    row = pl.BlockSpec((tm, d), lambda i: (i, 0))
    in_specs = [row, _resident((1, d)), _resident((d, d)),
                pl.BlockSpec((None, tm, pd), lambda i: (layer, i, 0)), _resident((pd, d))]
    h_shape = jax.ShapeDtypeStruct((n, d), F32)
    if mixer is None:
        return pl.pallas_call(
            _ple_final_kernel, grid=(n // tm,),
            in_specs=in_specs + [_resident((1, d))], out_specs=row, out_shape=h_shape,
            compiler_params=_params(1), name="ple_final")(h2, g, w_gate, p3, w_up, final_g)
    n_mixer = len(mixer.operands())
    return pl.pallas_call(
        functools.partial(_ple_next_kernel, n_mixer=n_mixer), grid=(n // tm,),
        in_specs=in_specs + mixer.in_specs(),
        out_specs=[row] + mixer.out_specs(tm), out_shape=[h_shape] + mixer.out_shapes(n),
        compiler_params=_params(1), name="ple_proj")(h2, g, w_gate, p3, w_up,
                                                     *mixer.operands())


def _row_tile(n, want):
    t = min(want, n)
    assert n % t == 0
    return t


def _col_tile(n, want):
    t = min(want, n)
    while n % t:
        t -= LANES
    return t


def kernel(x, p, attn_norm_g, fox_w_in, fox_b_f, fox_w_o, moba_w_in, moba_w_o, rel_bias_table,
           ffn_norm_g, ffn_w_in, ffn_w_out, ple_norm_g, ple_w_gate, ple_w_up, final_norm_g):
    b, s, d = x.shape
    depth = p.shape[0]
    n_heads = rel_bias_table.shape[1]
    assert d == n_heads * HEAD_DIM and n_heads % 2 == 0 and n_heads <= LANES
    n = b * s
    tm = _row_tile(n, DENSE_ROW_TILE)
    tf = _col_tile(ffn_w_out.shape[1], FFN_COL_CHUNK)

    def row_vec(v):
        return v.reshape(1, -1).astype(F32)

    def mixer(i):
        w = fox_w_in[i // 2] if i % 2 == 0 else moba_w_in[i // 2]
        parts = [row_vec(attn_norm_g[i]), (w[:, :d] * Q_SCALE).T.astype(BF16),
                 w[:, d:2 * d].astype(BF16), w[:, 2 * d:3 * d].T.astype(BF16)]
        if i % 2 == 0:
            parts += [jnp.pad(w[:, 3 * d:], ((0, 0), (0, LANES - n_heads))).astype(BF16),
                      jnp.pad(row_vec(fox_b_f[i // 2]), ((0, 0), (0, LANES - n_heads)))]
        return _Mixer(*parts)

    rel_table = rel_bias_table.astype(F32)
    bias_t = _bias_tiles(rel_table, MOBA_BLOCK)

    h = x.reshape(n, d).astype(F32)
    mixed = _project(h, mixer(0), tm=tm)
    for i in range(depth):
        qt, k3, vt = mixed[0], mixed[1].reshape(b, s, d), mixed[2]
        if i % 2 == 0:
            aqt, ak = _cumsum(mixed[3].reshape(b, s, LANES))
            o_t = _fox_attention(qt, k3, vt, aqt, ak, n_heads)
            w_o = fox_w_o[i // 2]
        else:
            o_t = _moba_attention(qt, k3, vt, rel_table, bias_t, n_heads)
            w_o = moba_w_o[i // 2]
        h = _oproj_ffn(h, o_t, w_o.astype(BF16), row_vec(ffn_norm_g[i]),
                       ffn_w_in[i].astype(BF16), ffn_w_out[i].astype(BF16), tm=tm, tf=tf)
        ple_args = (h, row_vec(ple_norm_g[i]), ple_w_gate[i].astype(BF16),
                    p.reshape(depth, n, -1), i, ple_w_up[i].astype(BF16))
        if i + 1 < depth:
            h, *mixed = _ple(*ple_args, tm=tm, mixer=mixer(i + 1))
        else:
            h = _ple(*ple_args, tm=tm, final_g=row_vec(final_norm_g))
    return h.reshape(b, s, d).astype(x.dtype)
```

```python
import functools
import math
from typing import NamedTuple, Optional

import numpy as np
import jax
import jax.numpy as jnp
from jax import lax
from jax.experimental import pallas as pl
from jax.experimental.pallas import tpu as pltpu

F32 = jnp.float32
BF16 = jnp.bfloat16

RMS_EPS = 1e-6
HEAD_DIM = 64
MOBA_BLOCK = 256
MOBA_TOP_K = 3
REL_BUCKETS = 32
REL_MAX_DIST = 128

LANES = 128
BF16_SUBLANES = 16
V_ROWS = HEAD_DIM + BF16_SUBLANES
GATE_LANES = 8
LOG2E = math.log2(math.e)
Q_SCALE = HEAD_DIM ** -0.5 * LOG2E
FOX_BLK = 512
MOBA_Q_TILE = 2 * MOBA_BLOCK
CUMSUM_BLK = 512
DENSE_ROW_TILE = 512
PROJ_COL_CHUNK = 512
FFN_COL_CHUNK = 256
VMEM_LIMIT_BYTES = 56 * 1024 * 1024
NEG_INF = float("-inf")


def _params(n_axes):
    return pltpu.CompilerParams(
        dimension_semantics=("arbitrary",) * n_axes,
        vmem_limit_bytes=VMEM_LIMIT_BYTES)


def _split3(x):
    x1 = x.astype(BF16)
    r1 = x - x1.astype(F32)
    x2 = r1.astype(BF16)
    x3 = (r1 - x2.astype(F32)).astype(BF16)
    return x1, x2, x3


def _rmsnorm(x, g):
    ms = jnp.mean(x * x, axis=-1, keepdims=True)
    return x * lax.rsqrt(ms + RMS_EPS) * g


def _log_sigmoid(x):
    return jnp.minimum(x, 0.0) - jnp.log1p(jnp.exp(-jnp.abs(x)))


def _resident(shape):
    return pl.BlockSpec(shape, lambda i: (0,) * len(shape), pipeline_mode=pl.Buffered(1))


class _Mixer(NamedTuple):
    g: jax.Array
    w_qt: jax.Array
    w_k: jax.Array
    w_vt: jax.Array
    w_f: Optional[jax.Array] = None
    b_f: Optional[jax.Array] = None

    def operands(self):
        return tuple(a for a in self if a is not None)

    def in_specs(self):
        return [_resident(a.shape) for a in self.operands()]

    def out_specs(self, tm):
        d = self.w_qt.shape[0]
        transposed = pl.BlockSpec((d, tm), lambda i: (0, i))
        specs = [transposed, pl.BlockSpec((tm, d), lambda i: (i, 0)), transposed]
        if self.w_f is not None:
            specs.append(pl.BlockSpec((tm, LANES), lambda i: (i, 0)))
        return specs

    def out_shapes(self, n):
        d = self.w_qt.shape[0]
        shapes = [jax.ShapeDtypeStruct((d, n), BF16), jax.ShapeDtypeStruct((n, d), BF16),
                  jax.ShapeDtypeStruct((d, n), BF16)]
        if self.w_f is not None:
            shapes.append(jax.ShapeDtypeStruct((n, LANES), F32))
        return shapes


def _chunks(total, want):
    step = want if total % want == 0 else total
    return [slice(c * step, (c + 1) * step) for c in range(total // step)]


def _mixer_inputs(y, mixer_refs, out_refs):
    g_ref, wqt_ref, wk_ref, wvt_ref = mixer_refs[:4]
    u = _rmsnorm(y, g_ref[...]).astype(BF16)
    for wt_ref, out_ref in ((wqt_ref, out_refs[0]), (wvt_ref, out_refs[2])):
        for rows in _chunks(wt_ref.shape[0], PROJ_COL_CHUNK):
            out_ref[rows, :] = lax.dot_general(
                wt_ref[rows, :], u, (((1,), (1,)), ((), ())),
                preferred_element_type=F32).astype(BF16)
    for cols in _chunks(wk_ref.shape[1], PROJ_COL_CHUNK):
        out_refs[1][:, cols] = jnp.dot(u, wk_ref[:, cols],
                                       preferred_element_type=F32).astype(BF16)
    if len(mixer_refs) > 4:
        wf_ref, bf_ref = mixer_refs[4:]
        f_logit = jnp.dot(u, wf_ref[...], preferred_element_type=F32) + bf_ref[...]
        out_refs[3][...] = _log_sigmoid(f_logit)


def _proj_kernel(x_ref, *refs, n_mixer):
    _mixer_inputs(x_ref[...], refs[:n_mixer], refs[n_mixer:])


def _project(h2, mixer, *, tm):
    n, d = h2.shape
    n_mixer = len(mixer.operands())
    return pl.pallas_call(
        functools.partial(_proj_kernel, n_mixer=n_mixer), grid=(n // tm,),
        in_specs=[pl.BlockSpec((tm, d), lambda i: (i, 0))] + mixer.in_specs(),
        out_specs=mixer.out_specs(tm), out_shape=mixer.out_shapes(n),
        compiler_params=_params(1), name="proj")(h2, *mixer.operands())


def _cumsum_kernel(lf_ref, aqt_ref, ak_ref, carry_ref):
    @pl.when(pl.program_id(1) == 0)
    def _():
        carry_ref[...] = jnp.zeros_like(carry_ref)

    t = lf_ref.shape[1]
    row = lax.broadcasted_iota(jnp.int32, (t, t), 0)
    col = lax.broadcasted_iota(jnp.int32, (t, t), 1)
    tril = jnp.where(col <= row, 1.0, 0.0).astype(BF16)
    x1, x2, x3 = _split3(lf_ref[0])
    cs = (jnp.dot(tril, x1, preferred_element_type=F32)
          + jnp.dot(tril, x2, preferred_element_type=F32)
          + jnp.dot(tril, x3, preferred_element_type=F32))
    cs = cs + carry_ref[0:1, :]
    carry_ref[...] = jnp.broadcast_to(cs[t - 1:t, :], carry_ref.shape)

    src = lax.broadcasted_iota(jnp.int32, (LANES, LANES), 0)
    dst = lax.broadcasted_iota(jnp.int32, (LANES, LANES), 1)
    lane = lax.broadcasted_iota(jnp.int32, (1, LANES), 1) & (GATE_LANES - 1)
    aq = jnp.where((lane >= 3) & (lane < 6), 1.0, 0.0)
    ak = jnp.where(lane < 3, 1.0, 0.0)
    for i, part in enumerate(_split3(cs * LOG2E)):
        to_q = jnp.where(dst == GATE_LANES * src + i, 1.0, 0.0).astype(BF16)
        to_k = jnp.where(dst == GATE_LANES * src + 3 + i, 1.0, 0.0).astype(BF16)
        aq = aq + jnp.dot(part, to_q, preferred_element_type=F32)
        ak = ak - jnp.dot(part, to_k, preferred_element_type=F32)
    eye = jnp.where(src == dst, 1.0, 0.0).astype(BF16)
    aqt_ref[0] = lax.dot_general(eye, aq.astype(BF16), (((1,), (1,)), ((), ())),
                                 preferred_element_type=F32).astype(BF16)
    ak_ref[0] = ak.astype(BF16)


def _cumsum(lf3):
    b, s, _ = lf3.shape
    t = min(CUMSUM_BLK, s)
    spec = pl.BlockSpec((1, t, LANES), lambda i, j: (i, j, 0))
    return pl.pallas_call(
        _cumsum_kernel, grid=(b, s // t),
        in_specs=[spec],
        out_specs=[pl.BlockSpec((1, LANES, t), lambda i, j: (i, 0, j)), spec],
        out_shape=[jax.ShapeDtypeStruct((b, LANES, s), BF16),
                   jax.ShapeDtypeStruct(lf3.shape, BF16)],
        scratch_shapes=[pltpu.VMEM((8, LANES), F32)],
        compiler_params=_params(2), name="gate_cumsum")(lf3)


def _fill_value_rows(vt_ref, v_ref, blk):
    row = lax.broadcasted_iota(jnp.int32, (V_ROWS - HEAD_DIM, blk), 0)
    tail = jnp.where(row == 0, 1.0, 0.0).astype(BF16)
    for hh in range(2):
        for jb in range(v_ref.shape[1] // blk):
            head_rows = v_ref[hh * HEAD_DIM:(hh + 1) * HEAD_DIM, jb * blk:(jb + 1) * blk]
            vt_ref[hh, jb] = jnp.concatenate([head_rows, tail], axis=0)


def _flash_init(m_ref, acc_ref):
    m_ref[...] = jnp.full(m_ref.shape, NEG_INF, F32)
    acc_ref[...] = jnp.zeros(acc_ref.shape, F32)


def _pipelined_blocks(n, last_block, scores_into, update, final_update=None):
    def step(j_next, j, slot):
        for hh in range(2):
            scores_into(j_next, 1 - slot, hh)
            update(j, slot, hh)

    for hh in range(2):
        scores_into(0, 0, hh)

    def pair(jj, carry):
        j = 2 * jj
        step(jnp.minimum(j + 1, last_block), j, 0)
        step(jnp.minimum(j + 2, last_block), j + 1, 1)
        return carry
    lax.fori_loop(0, n // 2, pair, 0)

    @pl.when(n % 2 == 1)
    def _():
        if final_update is not None:
            step(n, n - 1, 0)
            final_update(1)
        else:
            for hh in range(2):
                update(n - 1, 0, hh)

    if final_update is not None:
        @pl.when(n % 2 == 0)
        def _():
            final_update(0)


def _store_scores(s_t, s_ref, mx_ref, idx):
    s_ref[idx] = s_t
    mx_ref[idx] = jnp.max(s_t, axis=0, keepdims=True)


def _flash_update(s_t, mx, v_rows, m_ref, acc_ref, hh, keep=None, const=None):
    m_old = m_ref[hh]
    if const is not None:
        mx = mx + const
    if keep is not None:
        mx = jnp.where(keep, mx, NEG_INF)
    m_new = jnp.maximum(m_old, mx)
    m_safe = jnp.where(m_new == NEG_INF, 0.0, m_new)
    shift = m_safe if const is None else m_safe - const
    if keep is not None:
        shift = jnp.where(keep, shift, float("inf"))
    p = jnp.exp2(s_t - shift).astype(BF16)
    alpha = jnp.exp2(m_old - m_safe)
    m_ref[hh] = m_new
    if not isinstance(v_rows, (list, tuple)):
        v_rows = [v_rows]
    keys = p.shape[0] // len(v_rows)
    acc = alpha * acc_ref[hh]
    for i, v_i in enumerate(v_rows):
        acc = acc + jnp.dot(v_i, p[i * keys:(i + 1) * keys], preferred_element_type=F32)
    acc_ref[hh] = acc


def _write_heads(o_ref, acc_ref):
    outs = []
    for hh in range(2):
        acc = acc_ref[hh]
        outs.append(acc[:HEAD_DIM] / acc[HEAD_DIM:HEAD_DIM + 1])
    o_ref[...] = jnp.concatenate(outs, axis=0).astype(o_ref.dtype)


def _head_row_mask(hh):
    row = lax.broadcasted_iota(jnp.int32, (LANES, 1), 0)
    return (row < HEAD_DIM) if hh == 0 else (row >= HEAD_DIM)


def _fox_kernel(qt_ref, k_ref, v_ref, aqt_ref, ak_ref, o_ref,
                vt_ref, s_ref, mx_ref, m_ref, acc_ref):
    hp = pl.program_id(1)
    qi = pl.program_id(2)
    blk = qt_ref.shape[1]
    nblk = k_ref.shape[1] // blk

    @pl.when(qi == 0)
    def _():
        _fill_value_rows(vt_ref, v_ref, blk)

    gate_row = lax.broadcasted_iota(jnp.int32, (LANES, 1), 0)
    w_q = []
    for hh in range(2):
        first = GATE_LANES * (2 * hp + hh)
        own_gate = (gate_row >= first) & (gate_row < first + GATE_LANES)
        q_rows = jnp.where(_head_row_mask(hh), qt_ref[...], jnp.zeros_like(qt_ref[...]))
        g_rows = jnp.where(own_gate, aqt_ref[0], jnp.zeros_like(aqt_ref[0]))
        w_q.append(jnp.concatenate([q_rows, g_rows], axis=0))
    krow = lax.broadcasted_iota(jnp.int32, (blk, blk), 0)
    qcol = lax.broadcasted_iota(jnp.int32, (blk, blk), 1)

    def scores_into(j, slot, hh):
        rows = pl.ds(pl.multiple_of(j * blk, blk), blk)
        keys = jnp.concatenate([k_ref[0, rows, :], ak_ref[0, rows, :]], axis=1)
        s_t = jnp.dot(keys, w_q[hh], preferred_element_type=F32)
        _store_scores(s_t, s_ref, mx_ref, (slot, hh))

    def update(j, slot, hh):
        _flash_update(s_ref[slot, hh], mx_ref[slot, hh], vt_ref[hh, j], m_ref, acc_ref, hh)

    def diagonal_update(slot):
        for hh in range(2):
            _store_scores(jnp.where(krow <= qcol, s_ref[slot, hh], NEG_INF),
                          s_ref, mx_ref, (slot, hh))
            update(qi, slot, hh)

    _flash_init(m_ref, acc_ref)
    _pipelined_blocks(qi, nblk - 1, scores_into, update, diagonal_update)
    _write_heads(o_ref, acc_ref)


def _fox_attention(qt, k3, vt, aqt3, ak3, n_heads):
    b, s, d = k3.shape
    assert n_heads * GATE_LANES <= LANES
    blk = min(FOX_BLK, s)
    npair = n_heads // 2
    nq = s // blk
    tile_t = pl.BlockSpec((LANES, blk), lambda bi, hp, qi: (hp, bi * nq + qi))
    return pl.pallas_call(
        _fox_kernel, grid=(b, npair, nq),
        in_specs=[
            tile_t,
            pl.BlockSpec((1, s, LANES), lambda bi, hp, qi: (bi, 0, hp)),
            pl.BlockSpec((LANES, s), lambda bi, hp, qi: (hp, bi)),
            pl.BlockSpec((1, LANES, blk), lambda bi, hp, qi: (bi, 0, qi)),
            pl.BlockSpec((1, s, LANES), lambda bi, hp, qi: (bi, 0, 0)),
        ],
        out_specs=tile_t,
        out_shape=jax.ShapeDtypeStruct((d, b * s), BF16),
        scratch_shapes=[
            pltpu.VMEM((2, s // blk, V_ROWS, blk), BF16),
            pltpu.VMEM((2, 2, blk, blk), F32),
            pltpu.VMEM((2, 2, 1, blk), F32),
            pltpu.VMEM((2, 1, blk), F32),
            pltpu.VMEM((2, V_ROWS, blk), F32),
        ],
        compiler_params=_params(3), name="fox_attention")(qt, k3, vt, aqt3, ak3)


def _t5_bucket_np(n):
    max_exact = REL_BUCKETS // 2
    nf = np.maximum(n, 1).astype(np.float64)
    large = max_exact + (np.log(nf / max_exact) / math.log(REL_MAX_DIST / max_exact)
                         * (REL_BUCKETS - max_exact)).astype(np.int32)
    return np.where(n < max_exact, n, np.minimum(large, REL_BUCKETS - 1)).astype(np.int32)


def _bucket_tiles(blk):
    key = np.arange(blk)[:, None]
    qry = np.arange(blk)[None, :]
    own = np.where(key <= qry, _t5_bucket_np(np.maximum(qry - key, 0)), -1)
    prev = _t5_bucket_np(blk + qry - key)
    return np.stack([own, prev]).astype(np.int32)


def _bias_kernel(tab_ref, bucket_ref, o_ref):
    h = pl.program_id(0)
    bucket = bucket_ref[...]
    acc = jnp.where(bucket < 0, NEG_INF, 0.0).astype(F32)
    for bkt in range(REL_BUCKETS):
        acc = jnp.where(bucket == bkt, tab_ref[bkt, h] * LOG2E, acc)
    o_ref[0] = acc


def _bias_tiles(rel_table, blk):
    n_heads = rel_table.shape[1]
    buckets = jnp.asarray(_bucket_tiles(blk))
    return pl.pallas_call(
        _bias_kernel, grid=(n_heads,),
        in_specs=[pl.BlockSpec(memory_space=pltpu.SMEM),
                  pl.BlockSpec((2, blk, blk), lambda h: (0, 0, 0))],
        out_specs=pl.BlockSpec((1, 2, blk, blk), lambda h: (h, 0, 0, 0)),
        out_shape=jax.ShapeDtypeStruct((n_heads, 2, blk, blk), F32),
        compiler_params=_params(1), name="t5_bias_tiles")(rel_table, buckets)


def _moba_kernel(tab_ref, qt_ref, k_ref, v_ref, bias_ref, o_ref,
                 vt_ref, km_ref, sel_ref, s_ref, mx_ref, near_ref, m_ref, acc_ref):
    hp = pl.program_id(1)
    qi = pl.program_id(2)
    blk = MOBA_BLOCK
    tq = qt_ref.shape[1]
    nblk = k_ref.shape[1] // blk
    assert tq == 2 * blk and blk >= REL_MAX_DIST
    first_own = 2 * qi

    @pl.when(qi == 0)
    def _():
        def build(jb, carry):
            rows = pl.ds(pl.multiple_of(jb * blk, blk), blk)
            km_ref[pl.ds(jb, 1), :] = jnp.mean(k_ref[0, rows, :].astype(F32),
                                                axis=0, keepdims=True)
            return carry
        lax.fori_loop(0, nblk, build, 0)
        _fill_value_rows(vt_ref, v_ref, blk)

    qt = qt_ref[...]
    km_parts = _split3(km_ref[...])
    blk_id = lax.broadcasted_iota(jnp.int32, (nblk, tq), 0)
    qpos = lax.broadcasted_iota(jnp.int32, (nblk, tq), 1)
    own = first_own + jnp.where(qpos >= blk, 1, 0)
    past = blk_id < own
    q_m_t = []
    for hh in range(2):
        q_m_t.append(jnp.where(_head_row_mask(hh), qt, jnp.zeros_like(qt)))

        gate = None
        for part in km_parts:
            term = jnp.dot(part, q_m_t[hh], preferred_element_type=F32)
            gate = term if gate is None else gate + term
        work = jnp.where(past, gate, NEG_INF)
        picked = jnp.zeros(gate.shape, F32)
        for _ in range(MOBA_TOP_K):
            best = jnp.max(work, axis=0, keepdims=True)
            first = jnp.min(jnp.where(work == best, blk_id, nblk), axis=0, keepdims=True)
            hit = blk_id == first
            picked = jnp.where(hit, 1.0, picked)
            work = jnp.where(hit, NEG_INF, work)
        keep_all = ((picked > 0.5) & past) | (blk_id == own) | (blk_id == first_own + 1)
        sel_ref[hh] = jnp.where(keep_all, 1.0, 0.0)

    def scores(hh, j):
        rows = pl.ds(pl.multiple_of(j * blk, blk), blk)
        return jnp.dot(k_ref[0, rows, :], q_m_t[hh], preferred_element_type=F32)

    def keep(hh, j):
        return sel_ref[hh, pl.ds(j, 1), :] > 0.5

    _flash_init(m_ref, acc_ref)
    far_bias = [tab_ref[REL_BUCKETS - 1, 2 * hp + hh] * LOG2E for hh in range(2)]

    j_prev = jnp.maximum(first_own - 1, 0)
    neg_tile = jnp.full((blk, blk), NEG_INF, F32)
    near_mx = []
    for hh in range(2):
        own_t, prev_t = bias_ref[hh, 0], bias_ref[hh, 1]
        mask_prev = jnp.where(keep(hh, j_prev) & (qi >= 1), 0.0, NEG_INF)
        mask_own = jnp.where(keep(hh, first_own), 0.0, NEG_INF)
        far_t = jnp.full((blk, blk), far_bias[hh], F32)
        parts = [scores(hh, j_prev) + mask_prev + jnp.concatenate([prev_t, far_t], axis=1),
                 scores(hh, first_own) + mask_own + jnp.concatenate([own_t, prev_t], axis=1),
                 scores(hh, first_own + 1) + jnp.concatenate([neg_tile, own_t], axis=1)]
        mx = None
        for i, part in enumerate(parts):
            near_ref[hh, i * blk:(i + 1) * blk] = part
            part_mx = jnp.max(part, axis=0, keepdims=True)
            mx = part_mx if mx is None else jnp.maximum(mx, part_mx)
        near_mx.append(mx)
    for hh in range(2):
        _flash_update(near_ref[hh], near_mx[hh],
                      [vt_ref[hh, j_prev], vt_ref[hh, first_own], vt_ref[hh, first_own + 1]],
                      m_ref, acc_ref, hh)

    def scores_into(j, slot, hh):
        _store_scores(scores(hh, j), s_ref, mx_ref, (slot, hh))

    def update(j, slot, hh):
        _flash_update(s_ref[slot, hh], mx_ref[slot, hh], vt_ref[hh, j],
                      m_ref, acc_ref, hh, keep=keep(hh, j), const=far_bias[hh])

    _pipelined_blocks(jnp.maximum(first_own - 1, 0), nblk - 1, scores_into, update)
    _write_heads(o_ref, acc_ref)


def _moba_attention(qt, k3, vt, rel_table, bias_t, n_heads):
    b, s, d = k3.shape
    blk = MOBA_BLOCK
    tq = MOBA_Q_TILE
    assert s % tq == 0
    npair = n_heads // 2
    nblk = s // blk
    nq = s // tq
    tile_t = pl.BlockSpec((LANES, tq), lambda bi, hp, qi: (hp, bi * nq + qi))
    return pl.pallas_call(
        _moba_kernel, grid=(b, npair, nq),
        in_specs=[
            pl.BlockSpec(memory_space=pltpu.SMEM),
            tile_t,
            pl.BlockSpec((1, s, LANES), lambda bi, hp, qi: (bi, 0, hp)),
            pl.BlockSpec((LANES, s), lambda bi, hp, qi: (hp, bi)),
            pl.BlockSpec((2, 2, blk, blk), lambda bi, hp, qi: (hp, 0, 0, 0)),
        ],
        out_specs=tile_t,
        out_shape=jax.ShapeDtypeStruct((d, b * s), BF16),
        scratch_shapes=[
            pltpu.VMEM((2, nblk, V_ROWS, blk), BF16),
            pltpu.VMEM((nblk, LANES), F32),
            pltpu.VMEM((2, nblk, tq), F32),
            pltpu.VMEM((2, 2, blk, tq), F32),
            pltpu.VMEM((2, 2, 1, tq), F32),
            pltpu.VMEM((2, 3 * blk, tq), F32),
            pltpu.VMEM((2, 1, tq), F32),
            pltpu.VMEM((2, V_ROWS, tq), F32),
        ],
        compiler_params=_params(3), name="moba_attention")(
            rel_table, qt, k3, vt, bias_t)


def _oproj_ffn_kernel(h_ref, ot_ref, wo_ref, g_ref, win_ref, wout_ref, out_ref, *, tf):
    d_ff = wout_ref.shape[0]
    h1 = h_ref[...] + lax.dot_general(ot_ref[...], wo_ref[...], (((0,), (0,)), ((), ())),
                                      preferred_element_type=F32)
    u = _rmsnorm(h1, g_ref[...]).astype(BF16)
    acc = h1
    for c in range(d_ff // tf):
        gate = jnp.dot(u, win_ref[:, c * tf:(c + 1) * tf], preferred_element_type=F32)
        up = jnp.dot(u, win_ref[:, d_ff + c * tf:d_ff + (c + 1) * tf],
                     preferred_element_type=F32)
        act = (gate * jax.nn.sigmoid(gate) * up).astype(BF16)
        acc = acc + jnp.dot(act, wout_ref[c * tf:(c + 1) * tf, :],
                            preferred_element_type=F32)
    out_ref[...] = acc


def _oproj_ffn(h2, o_t, w_o, g, w_in, w_out, *, tm, tf):
    n, d = h2.shape
    d_ff = w_out.shape[0]
    assert d_ff % tf == 0
    row = pl.BlockSpec((tm, d), lambda i: (i, 0))
    return pl.pallas_call(
        functools.partial(_oproj_ffn_kernel, tf=tf), grid=(n // tm,),
        in_specs=[row, pl.BlockSpec((d, tm), lambda i: (0, i)), _resident((d, d)),
                  _resident((1, d)), _resident((d, 2 * d_ff)), _resident((d_ff, d))],
        out_specs=row,
        out_shape=jax.ShapeDtypeStruct((n, d), F32),
        compiler_params=_params(1), name="oproj_ffn")(h2, o_t, w_o, g, w_in, w_out)


def _ple_update(x_ref, g_ref, wg_ref, p_ref, wu_ref):
    x = x_ref[...]
    u = _rmsnorm(x, g_ref[...]).astype(BF16)
    gate = jax.nn.sigmoid(jnp.dot(u, wg_ref[...], preferred_element_type=F32))
    up = jnp.dot(p_ref[...].astype(BF16), wu_ref[...], preferred_element_type=F32)
    return x + gate * up


def _ple_next_kernel(x_ref, g_ref, wg_ref, p_ref, wu_ref, *refs, n_mixer):
    y = _ple_update(x_ref, g_ref, wg_ref, p_ref, wu_ref)
    refs[n_mixer][...] = y
    _mixer_inputs(y, refs[:n_mixer], refs[n_mixer + 1:])


def _ple_final_kernel(x_ref, g_ref, wg_ref, p_ref, wu_ref, fg_ref, out_ref):
    y = _ple_update(x_ref, g_ref, wg_ref, p_ref, wu_ref)
    out_ref[...] = _rmsnorm(y, fg_ref[...])


def _ple(h2, g, w_gate, p3, layer, w_up, *, tm, mixer=None, final_g=None):
    n, d = h2.shape
    pd = p3.shape[2]
    row = pl.BlockSpec((tm, d), lambda i: (i, 0))
    in_specs = [row, _resident((1, d)), _resident((d, d)),
                pl.BlockSpec((None, tm, pd), lambda i: (layer, i, 0)), _resident((pd, d))]
    h_shape = jax.ShapeDtypeStruct((n, d), F32)
    if mixer is None:
        return pl.pallas_call(
            _ple_final_kernel, grid=(n // tm,),
            in_specs=in_specs + [_resident((1, d))], out_specs=row, out_shape=h_shape,
            compiler_params=_params(1), name="ple_final")(h2, g, w_gate, p3, w_up, final_g)
    n_mixer = len(mixer.operands())
    return pl.pallas_call(
        functools.partial(_ple_next_kernel, n_mixer=n_mixer), grid=(n // tm,),
        in_specs=in_specs + mixer.in_specs(),
        out_specs=[row] + mixer.out_specs(tm), out_shape=[h_shape] + mixer.out_shapes(n),
        compiler_params=_params(1), name="ple_proj")(h2, g, w_gate, p3, w_up,
                                                     *mixer.operands())


def _row_tile(n, want):
    t = min(want, n)
    assert n % t == 0
    return t


def _col_tile(n, want):
    t = min(want, n)
    while n % t:
        t -= LANES
    return t


def kernel(x, p, attn_norm_g, fox_w_in, fox_b_f, fox_w_o, moba_w_in, moba_w_o, rel_bias_table,
           ffn_norm_g, ffn_w_in, ffn_w_out, ple_norm_g, ple_w_gate, ple_w_up, final_norm_g):
    b, s, d = x.shape
    depth = p.shape[0]
    n_heads = rel_bias_table.shape[1]
    assert d == n_heads * HEAD_DIM and n_heads % 2 == 0 and n_heads <= LANES
    n = b * s
    tm = _row_tile(n, DENSE_ROW_TILE)
    tf = _col_tile(ffn_w_out.shape[1], FFN_COL_CHUNK)

    def row_vec(v):
        return v.reshape(1, -1).astype(F32)

    def mixer(i):
        w = fox_w_in[i // 2] if i % 2 == 0 else moba_w_in[i // 2]
        parts = [row_vec(attn_norm_g[i]), (w[:, :d] * Q_SCALE).T.astype(BF16),
                 w[:, d:2 * d].astype(BF16), w[:, 2 * d:3 * d].T.astype(BF16)]
        if i % 2 == 0:
            parts += [jnp.pad(w[:, 3 * d:], ((0, 0), (0, LANES - n_heads))).astype(BF16),
                      jnp.pad(row_vec(fox_b_f[i // 2]), ((0, 0), (0, LANES - n_heads)))]
        return _Mixer(*parts)

    rel_table = rel_bias_table.astype(F32)
    bias_t = _bias_tiles(rel_table, MOBA_BLOCK)

    h = x.reshape(n, d).astype(F32)
    mixed = _project(h, mixer(0), tm=tm)
    for i in range(depth):
        qt, k3, vt = mixed[0], mixed[1].reshape(b, s, d), mixed[2]
        if i % 2 == 0:
            aqt, ak = _cumsum(mixed[3].reshape(b, s, LANES))
            o_t = _fox_attention(qt, k3, vt, aqt, ak, n_heads)
            w_o = fox_w_o[i // 2]
        else:
            o_t = _moba_attention(qt, k3, vt, rel_table, bias_t, n_heads)
            w_o = moba_w_o[i // 2]
        h = _oproj_ffn(h, o_t, w_o.astype(BF16), row_vec(ffn_norm_g[i]),
                       ffn_w_in[i].astype(BF16), ffn_w_out[i].astype(BF16), tm=tm, tf=tf)
        ple_args = (h, row_vec(ple_norm_g[i]), ple_w_gate[i].astype(BF16),
                    p.reshape(depth, n, -1), i, ple_w_up[i].astype(BF16))
        if i + 1 < depth:
            h, *mixed = _ple(*ple_args, tm=tm, mixer=mixer(i + 1))
        else:
            h = _ple(*ple_args, tm=tm, final_g=row_vec(final_norm_g))
    return h.reshape(b, s, d).astype(x.dtype)
```

```python
import functools
import math
from typing import NamedTuple, Optional

import numpy as np
import jax
import jax.numpy as jnp
from jax import lax
from jax.experimental import pallas as pl
from jax.experimental.pallas import tpu as pltpu

F32 = jnp.float32
BF16 = jnp.bfloat16

RMS_EPS = 1e-6
HEAD_DIM = 64
MOBA_BLOCK = 256
MOBA_TOP_K = 3
REL_BUCKETS = 32
REL_MAX_DIST = 128

LANES = 128
BF16_SUBLANES = 16
V_ROWS = HEAD_DIM + BF16_SUBLANES
GATE_LANES = 8
LOG2E = math.log2(math.e)
Q_SCALE = HEAD_DIM ** -0.5 * LOG2E
FOX_BLK = 512
MOBA_Q_TILE = 2 * MOBA_BLOCK
CUMSUM_BLK = 512
DENSE_ROW_TILE = 512
PROJ_COL_CHUNK = 512
FFN_COL_CHUNK = 256
VMEM_LIMIT_BYTES = 56 * 1024 * 1024
NEG_INF = float("-inf")


def _params(n_axes):
    return pltpu.CompilerParams(
        dimension_semantics=("arbitrary",) * n_axes,
        vmem_limit_bytes=VMEM_LIMIT_BYTES)


def _split3(x):
    x1 = x.astype(BF16)
    r1 = x - x1.astype(F32)
    x2 = r1.astype(BF16)
    x3 = (r1 - x2.astype(F32)).astype(BF16)
    return x1, x2, x3


def _rmsnorm(x, g):
    ms = jnp.mean(x * x, axis=-1, keepdims=True)
    return x * lax.rsqrt(ms + RMS_EPS) * g


def _log_sigmoid(x):
    return jnp.minimum(x, 0.0) - jnp.log1p(jnp.exp(-jnp.abs(x)))


def _resident(shape):
    return pl.BlockSpec(shape, lambda i: (0,) * len(shape), pipeline_mode=pl.Buffered(1))


class _Mixer(NamedTuple):
    g: jax.Array
    w_qt: jax.Array
    w_k: jax.Array
    w_vt: jax.Array
    w_f: Optional[jax.Array] = None
    b_f: Optional[jax.Array] = None

    def operands(self):
        return tuple(a for a in self if a is not None)

    def in_specs(self):
        return [_resident(a.shape) for a in self.operands()]

    def out_specs(self, tm):
        d = self.w_qt.shape[0]
        transposed = pl.BlockSpec((d, tm), lambda i: (0, i))
        specs = [transposed, pl.BlockSpec((tm, d), lambda i: (i, 0)), transposed]
        if self.w_f is not None:
            specs.append(pl.BlockSpec((tm, LANES), lambda i: (i, 0)))
        return specs

    def out_shapes(self, n):
        d = self.w_qt.shape[0]
        shapes = [jax.ShapeDtypeStruct((d, n), BF16), jax.ShapeDtypeStruct((n, d), BF16),
                  jax.ShapeDtypeStruct((d, n), BF16)]
        if self.w_f is not None:
            shapes.append(jax.ShapeDtypeStruct((n, LANES), F32))
        return shapes


def _chunks(total, want):
    step = want if total % want == 0 else total
    return [slice(c * step, (c + 1) * step) for c in range(total // step)]


def _mixer_inputs(y, mixer_refs, out_refs):
    g_ref, wqt_ref, wk_ref, wvt_ref = mixer_refs[:4]
    u = _rmsnorm(y, g_ref[...]).astype(BF16)
    for wt_ref, out_ref in ((wqt_ref, out_refs[0]), (wvt_ref, out_refs[2])):
        for rows in _chunks(wt_ref.shape[0], PROJ_COL_CHUNK):
            out_ref[rows, :] = lax.dot_general(
                wt_ref[rows, :], u, (((1,), (1,)), ((), ())),
                preferred_element_type=F32).astype(BF16)
    for cols in _chunks(wk_ref.shape[1], PROJ_COL_CHUNK):
        out_refs[1][:, cols] = jnp.dot(u, wk_ref[:, cols],
                                       preferred_element_type=F32).astype(BF16)
    if len(mixer_refs) > 4:
        wf_ref, bf_ref = mixer_refs[4:]
        f_logit = jnp.dot(u, wf_ref[...], preferred_element_type=F32) + bf_ref[...]
        out_refs[3][...] = _log_sigmoid(f_logit)


def _proj_kernel(x_ref, *refs, n_mixer):
    _mixer_inputs(x_ref[...], refs[:n_mixer], refs[n_mixer:])


def _project(h2, mixer, *, tm):
    n, d = h2.shape
    n_mixer = len(mixer.operands())
    return pl.pallas_call(
        functools.partial(_proj_kernel, n_mixer=n_mixer), grid=(n // tm,),
        in_specs=[pl.BlockSpec((tm, d), lambda i: (i, 0))] + mixer.in_specs(),
        out_specs=mixer.out_specs(tm), out_shape=mixer.out_shapes(n),
        compiler_params=_params(1), name="proj")(h2, *mixer.operands())


def _cumsum_kernel(lf_ref, aqt_ref, ak_ref, carry_ref):
    @pl.when(pl.program_id(1) == 0)
    def _():
        carry_ref[...] = jnp.zeros_like(carry_ref)

    t = lf_ref.shape[1]
    row = lax.broadcasted_iota(jnp.int32, (t, t), 0)
    col = lax.broadcasted_iota(jnp.int32, (t, t), 1)
    tril = jnp.where(col <= row, 1.0, 0.0).astype(BF16)
    x1, x2, x3 = _split3(lf_ref[0])
    cs = (jnp.dot(tril, x1, preferred_element_type=F32)
          + jnp.dot(tril, x2, preferred_element_type=F32)
          + jnp.dot(tril, x3, preferred_element_type=F32))
    cs = cs + carry_ref[0:1, :]
    carry_ref[...] = jnp.broadcast_to(cs[t - 1:t, :], carry_ref.shape)

    src = lax.broadcasted_iota(jnp.int32, (LANES, LANES), 0)
    dst = lax.broadcasted_iota(jnp.int32, (LANES, LANES), 1)
    lane = lax.broadcasted_iota(jnp.int32, (1, LANES), 1) & (GATE_LANES - 1)
    aq = jnp.where((lane >= 3) & (lane < 6), 1.0, 0.0)
    ak = jnp.where(lane < 3, 1.0, 0.0)
    for i, part in enumerate(_split3(cs * LOG2E)):
        to_q = jnp.where(dst == GATE_LANES * src + i, 1.0, 0.0).astype(BF16)
        to_k = jnp.where(dst == GATE_LANES * src + 3 + i, 1.0, 0.0).astype(BF16)
        aq = aq + jnp.dot(part, to_q, preferred_element_type=F32)
        ak = ak - jnp.dot(part, to_k, preferred_element_type=F32)
    eye = jnp.where(src == dst, 1.0, 0.0).astype(BF16)
    aqt_ref[0] = lax.dot_general(eye, aq.astype(BF16), (((1,), (1,)), ((), ())),
                                 preferred_element_type=F32).astype(BF16)
    ak_ref[0] = ak.astype(BF16)


def _cumsum(lf3):
    b, s, _ = lf3.shape
    t = min(CUMSUM_BLK, s)
    spec = pl.BlockSpec((1, t, LANES), lambda i, j: (i, j, 0))
    return pl.pallas_call(
        _cumsum_kernel, grid=(b, s // t),
        in_specs=[spec],
        out_specs=[pl.BlockSpec((1, LANES, t), lambda i, j: (i, 0, j)), spec],
        out_shape=[jax.ShapeDtypeStruct((b, LANES, s), BF16),
                   jax.ShapeDtypeStruct(lf3.shape, BF16)],
        scratch_shapes=[pltpu.VMEM((8, LANES), F32)],
        compiler_params=_params(2), name="gate_cumsum")(lf3)


def _fill_value_rows(vt_ref, v_ref, blk):
    row = lax.broadcasted_iota(jnp.int32, (V_ROWS - HEAD_DIM, blk), 0)
    tail = jnp.where(row == 0, 1.0, 0.0).astype(BF16)
    for hh in range(2):
        for jb in range(v_ref.shape[1] // blk):
            head_rows = v_ref[hh * HEAD_DIM:(hh + 1) * HEAD_DIM, jb * blk:(jb + 1) * blk]
            vt_ref[hh, jb] = jnp.concatenate([head_rows, tail], axis=0)


def _flash_init(m_ref, acc_ref):
    m_ref[...] = jnp.full(m_ref.shape, NEG_INF, F32)
    acc_ref[...] = jnp.zeros(acc_ref.shape, F32)


def _pipelined_blocks(n, last_block, scores_into, update, final_update=None):
    def step(j_next, j, slot):
        for hh in range(2):
            scores_into(j_next, 1 - slot, hh)
            update(j, slot, hh)

    for hh in range(2):
        scores_into(0, 0, hh)

    def pair(jj, carry):
        j = 2 * jj
        step(jnp.minimum(j + 1, last_block), j, 0)
        step(jnp.minimum(j + 2, last_block), j + 1, 1)
        return carry
    lax.fori_loop(0, n // 2, pair, 0)

    @pl.when(n % 2 == 1)
    def _():
        if final_update is not None:
            step(n, n - 1, 0)
            final_update(1)
        else:
            for hh in range(2):
                update(n - 1, 0, hh)

    if final_update is not None:
        @pl.when(n % 2 == 0)
        def _():
            final_update(0)


def _store_scores(s_t, s_ref, mx_ref, idx):
    s_ref[idx] = s_t
    mx_ref[idx] = jnp.max(s_t, axis=0, keepdims=True)


def _flash_update(s_t, mx, v_rows, m_ref, acc_ref, hh, keep=None, const=None,
                  cols=slice(None)):
    m_old = m_ref[hh, :, cols]
    if const is not None:
        mx = mx + const
    if keep is not None:
        mx = jnp.where(keep, mx, NEG_INF)
    m_new = jnp.maximum(m_old, mx)
    m_safe = jnp.where(m_new == NEG_INF, 0.0, m_new)
    shift = m_safe if const is None else m_safe - const
    if keep is not None:
        shift = jnp.where(keep, shift, float("inf"))
    p = jnp.exp2(s_t - shift).astype(BF16)
    alpha = jnp.exp2(m_old - m_safe)
    m_ref[hh, :, cols] = m_new
    if not isinstance(v_rows, (list, tuple)):
        v_rows = [v_rows]
    keys = p.shape[0] // len(v_rows)
    acc = alpha * acc_ref[hh, :, cols]
    for i, v_i in enumerate(v_rows):
        acc = acc + jnp.dot(v_i, p[i * keys:(i + 1) * keys], preferred_element_type=F32)
    acc_ref[hh, :, cols] = acc


def _write_heads(o_ref, acc_ref):
    outs = []
    for hh in range(2):
        acc = acc_ref[hh]
        outs.append(acc[:HEAD_DIM] / acc[HEAD_DIM:HEAD_DIM + 1])
    o_ref[...] = jnp.concatenate(outs, axis=0).astype(o_ref.dtype)


def _head_row_mask(hh):
    row = lax.broadcasted_iota(jnp.int32, (LANES, 1), 0)
    return (row < HEAD_DIM) if hh == 0 else (row >= HEAD_DIM)


def _fox_kernel(qt_ref, k_ref, v_ref, aqt_ref, ak_ref, o_ref,
                vt_ref, s_ref, mx_ref, m_ref, acc_ref):
    hp = pl.program_id(1)
    qi = pl.program_id(2)
    blk = qt_ref.shape[1]
    nblk = k_ref.shape[1] // blk

    @pl.when(qi == 0)
    def _():
        _fill_value_rows(vt_ref, v_ref, blk)

    gate_row = lax.broadcasted_iota(jnp.int32, (LANES, 1), 0)
    w_q = []
    for hh in range(2):
        first = GATE_LANES * (2 * hp + hh)
        own_gate = (gate_row >= first) & (gate_row < first + GATE_LANES)
        q_rows = jnp.where(_head_row_mask(hh), qt_ref[...], jnp.zeros_like(qt_ref[...]))
        g_rows = jnp.where(own_gate, aqt_ref[0], jnp.zeros_like(aqt_ref[0]))
        w_q.append(jnp.concatenate([q_rows, g_rows], axis=0))

    def scores_into(j, slot, hh):
        rows = pl.ds(pl.multiple_of(j * blk, blk), blk)
        keys = jnp.concatenate([k_ref[0, rows, :], ak_ref[0, rows, :]], axis=1)
        s_t = jnp.dot(keys, w_q[hh], preferred_element_type=F32)
        _store_scores(s_t, s_ref, mx_ref, (slot, hh))

    def update(j, slot, hh):
        _flash_update(s_ref[slot, hh], mx_ref[slot, hh], vt_ref[hh, j], m_ref, acc_ref, hh)

    def diagonal_update(slot):
        half = blk // 2
        lo, hi = slice(0, half), slice(half, blk)
        causal_lo = (lax.broadcasted_iota(jnp.int32, (half, half), 0)
                     <= lax.broadcasted_iota(jnp.int32, (half, half), 1))
        causal_hi = (lax.broadcasted_iota(jnp.int32, (blk, half), 0)
                     <= lax.broadcasted_iota(jnp.int32, (blk, half), 1) + half)
        for hh in range(2):
            v_rows = vt_ref[hh, qi]
            s_lo = jnp.where(causal_lo, s_ref[slot, hh, lo, lo], NEG_INF)
            _flash_update(s_lo, jnp.max(s_lo, axis=0, keepdims=True),
                          v_rows[:, lo], m_ref, acc_ref, hh, cols=lo)
            s_hi = jnp.where(causal_hi, s_ref[slot, hh, :, hi], NEG_INF)
            _flash_update(s_hi, jnp.max(s_hi, axis=0, keepdims=True),
                          v_rows, m_ref, acc_ref, hh, cols=hi)

    _flash_init(m_ref, acc_ref)
    _pipelined_blocks(qi, nblk - 1, scores_into, update, diagonal_update)
    _write_heads(o_ref, acc_ref)


def _fox_attention(qt, k3, vt, aqt3, ak3, n_heads):
    b, s, d = k3.shape
    assert n_heads * GATE_LANES <= LANES
    blk = min(FOX_BLK, s)
    npair = n_heads // 2
    nq = s // blk
    tile_t = pl.BlockSpec((LANES, blk), lambda bi, hp, qi: (hp, bi * nq + qi))
    return pl.pallas_call(
        _fox_kernel, grid=(b, npair, nq),
        in_specs=[
            tile_t,
            pl.BlockSpec((1, s, LANES), lambda bi, hp, qi: (bi, 0, hp)),
            pl.BlockSpec((LANES, s), lambda bi, hp, qi: (hp, bi)),
            pl.BlockSpec((1, LANES, blk), lambda bi, hp, qi: (bi, 0, qi)),
            pl.BlockSpec((1, s, LANES), lambda bi, hp, qi: (bi, 0, 0)),
        ],
        out_specs=tile_t,
        out_shape=jax.ShapeDtypeStruct((d, b * s), BF16),
        scratch_shapes=[
            pltpu.VMEM((2, s // blk, V_ROWS, blk), BF16),
            pltpu.VMEM((2, 2, blk, blk), F32),
            pltpu.VMEM((2, 2, 1, blk), F32),
            pltpu.VMEM((2, 1, blk), F32),
            pltpu.VMEM((2, V_ROWS, blk), F32),
        ],
        compiler_params=_params(3), name="fox_attention")(qt, k3, vt, aqt3, ak3)


def _t5_bucket_np(n):
    max_exact = REL_BUCKETS // 2
    nf = np.maximum(n, 1).astype(np.float64)
    large = max_exact + (np.log(nf / max_exact) / math.log(REL_MAX_DIST / max_exact)
                         * (REL_BUCKETS - max_exact)).astype(np.int32)
    return np.where(n < max_exact, n, np.minimum(large, REL_BUCKETS - 1)).astype(np.int32)


def _bucket_tiles(blk):
    key = np.arange(blk)[:, None]
    qry = np.arange(blk)[None, :]
    own = np.where(key <= qry, _t5_bucket_np(np.maximum(qry - key, 0)), -1)
    prev = _t5_bucket_np(blk + qry - key)
    return np.stack([own, prev]).astype(np.int32)


def _bias_kernel(tab_ref, bucket_ref, o_ref):
    h = pl.program_id(0)
    bucket = bucket_ref[...]
    acc = jnp.where(bucket < 0, NEG_INF, 0.0).astype(F32)
    for bkt in range(REL_BUCKETS):
        acc = jnp.where(bucket == bkt, tab_ref[bkt, h] * LOG2E, acc)
    o_ref[0] = acc


def _bias_tiles(rel_table, blk):
    n_heads = rel_table.shape[1]
    buckets = jnp.asarray(_bucket_tiles(blk))
    return pl.pallas_call(
        _bias_kernel, grid=(n_heads,),
        in_specs=[pl.BlockSpec(memory_space=pltpu.SMEM),
                  pl.BlockSpec((2, blk, blk), lambda h: (0, 0, 0))],
        out_specs=pl.BlockSpec((1, 2, blk, blk), lambda h: (h, 0, 0, 0)),
        out_shape=jax.ShapeDtypeStruct((n_heads, 2, blk, blk), F32),
        compiler_params=_params(1), name="t5_bias_tiles")(rel_table, buckets)


def _moba_kernel(tab_ref, qt_ref, k_ref, v_ref, bias_ref, o_ref,
                 vt_ref, km_ref, sel_ref, s_ref, mx_ref, near_ref, own1_ref, m_ref, acc_ref):
    hp = pl.program_id(1)
    qi = pl.program_id(2)
    blk = MOBA_BLOCK
    tq = qt_ref.shape[1]
    nblk = k_ref.shape[1] // blk
    assert tq == 2 * blk and blk >= REL_MAX_DIST
    first_own = 2 * qi

    @pl.when(qi == 0)
    def _():
        def build(jb, carry):
            rows = pl.ds(pl.multiple_of(jb * blk, blk), blk)
            km_ref[pl.ds(jb, 1), :] = jnp.mean(k_ref[0, rows, :].astype(F32),
                                                axis=0, keepdims=True)
            return carry
        lax.fori_loop(0, nblk, build, 0)
        _fill_value_rows(vt_ref, v_ref, blk)

    qt = qt_ref[...]
    km_parts = _split3(km_ref[...])
    blk_id = lax.broadcasted_iota(jnp.int32, (nblk, tq), 0)
    qpos = lax.broadcasted_iota(jnp.int32, (nblk, tq), 1)
    own = first_own + jnp.where(qpos >= blk, 1, 0)
    past = blk_id < own
    q_m_t = []
    for hh in range(2):
        q_m_t.append(jnp.where(_head_row_mask(hh), qt, jnp.zeros_like(qt)))

        gate = None
        for part in km_parts:
            term = jnp.dot(part, q_m_t[hh], preferred_element_type=F32)
            gate = term if gate is None else gate + term
        work = jnp.where(past, gate, NEG_INF)
        picked = jnp.zeros(gate.shape, F32)
        for _ in range(MOBA_TOP_K):
            best = jnp.max(work, axis=0, keepdims=True)
            first = jnp.min(jnp.where(work == best, blk_id, nblk), axis=0, keepdims=True)
            hit = blk_id == first
            picked = jnp.where(hit, 1.0, picked)
            work = jnp.where(hit, NEG_INF, work)
        keep_all = ((picked > 0.5) & past) | (blk_id == own) | (blk_id == first_own + 1)
        sel_ref[hh] = jnp.where(keep_all, 1.0, 0.0)

    def scores(hh, j):
        rows = pl.ds(pl.multiple_of(j * blk, blk), blk)
        return jnp.dot(k_ref[0, rows, :], q_m_t[hh], preferred_element_type=F32)

    def keep(hh, j):
        return sel_ref[hh, pl.ds(j, 1), :] > 0.5

    _flash_init(m_ref, acc_ref)
    far_bias = [tab_ref[REL_BUCKETS - 1, 2 * hp + hh] * LOG2E for hh in range(2)]

    j_prev = jnp.maximum(first_own - 1, 0)
    lo, hi = slice(0, blk), slice(blk, 2 * blk)
    near_mx = []
    for hh in range(2):
        own_t, prev_t = bias_ref[hh, 0], bias_ref[hh, 1]
        mask_prev = jnp.where(keep(hh, j_prev) & (qi >= 1), 0.0, NEG_INF)
        mask_own = jnp.where(keep(hh, first_own), 0.0, NEG_INF)
        far_t = jnp.full((blk, blk), far_bias[hh], F32)
        parts = [scores(hh, j_prev) + mask_prev + jnp.concatenate([prev_t, far_t], axis=1),
                 scores(hh, first_own) + mask_own + jnp.concatenate([own_t, prev_t], axis=1)]
        mx = None
        for i, part in enumerate(parts):
            near_ref[hh, i * blk:(i + 1) * blk] = part
            part_mx = jnp.max(part, axis=0, keepdims=True)
            mx = part_mx if mx is None else jnp.maximum(mx, part_mx)
        rows = pl.ds(pl.multiple_of((first_own + 1) * blk, blk), blk)
        last = jnp.dot(k_ref[0, rows, :], q_m_t[hh][:, hi], preferred_element_type=F32) + own_t
        own1_ref[hh] = last
        near_mx.append((mx[:, lo], jnp.maximum(mx[:, hi], jnp.max(last, axis=0, keepdims=True))))
    for hh in range(2):
        v_near = [vt_ref[hh, j_prev], vt_ref[hh, first_own], vt_ref[hh, first_own + 1]]
        _flash_update(near_ref[hh, :, lo], near_mx[hh][0], v_near[:2],
                      m_ref, acc_ref, hh, cols=lo)
        _flash_update(jnp.concatenate([near_ref[hh, :, hi], own1_ref[hh]], axis=0),
                      near_mx[hh][1], v_near, m_ref, acc_ref, hh, cols=hi)

    def scores_into(j, slot, hh):
        _store_scores(scores(hh, j), s_ref, mx_ref, (slot, hh))

    def update(j, slot, hh):
        _flash_update(s_ref[slot, hh], mx_ref[slot, hh], vt_ref[hh, j],
                      m_ref, acc_ref, hh, keep=keep(hh, j), const=far_bias[hh])

    _pipelined_blocks(jnp.maximum(first_own - 1, 0), nblk - 1, scores_into, update)
    _write_heads(o_ref, acc_ref)


def _moba_attention(qt, k3, vt, rel_table, bias_t, n_heads):
    b, s, d = k3.shape
    blk = MOBA_BLOCK
    tq = MOBA_Q_TILE
    assert s % tq == 0
    npair = n_heads // 2
    nblk = s // blk
    nq = s // tq
    tile_t = pl.BlockSpec((LANES, tq), lambda bi, hp, qi: (hp, bi * nq + qi))
    return pl.pallas_call(
        _moba_kernel, grid=(b, npair, nq),
        in_specs=[
            pl.BlockSpec(memory_space=pltpu.SMEM),
            tile_t,
            pl.BlockSpec((1, s, LANES), lambda bi, hp, qi: (bi, 0, hp)),
            pl.BlockSpec((LANES, s), lambda bi, hp, qi: (hp, bi)),
            pl.BlockSpec((2, 2, blk, blk), lambda bi, hp, qi: (hp, 0, 0, 0)),
        ],
        out_specs=tile_t,
        out_shape=jax.ShapeDtypeStruct((d, b * s), BF16),
        scratch_shapes=[
            pltpu.VMEM((2, nblk, V_ROWS, blk), BF16),
            pltpu.VMEM((nblk, LANES), F32),
            pltpu.VMEM((2, nblk, tq), F32),
            pltpu.VMEM((2, 2, blk, tq), F32),
            pltpu.VMEM((2, 2, 1, tq), F32),
            pltpu.VMEM((2, 2 * blk, tq), F32),
            pltpu.VMEM((2, blk, blk), F32),
            pltpu.VMEM((2, 1, tq), F32),
            pltpu.VMEM((2, V_ROWS, tq), F32),
        ],
        compiler_params=_params(3), name="moba_attention")(
            rel_table, qt, k3, vt, bias_t)


def _oproj_ffn_kernel(h_ref, ot_ref, wo_ref, g_ref, win_ref, wout_ref, out_ref, *, tf):
    d_ff = wout_ref.shape[0]
    h1 = h_ref[...] + lax.dot_general(ot_ref[...], wo_ref[...], (((0,), (0,)), ((), ())),
                                      preferred_element_type=F32)
    u = _rmsnorm(h1, g_ref[...]).astype(BF16)
    acc = h1
    for c in range(d_ff // tf):
        gate = jnp.dot(u, win_ref[:, c * tf:(c + 1) * tf], preferred_element_type=F32)
        up = jnp.dot(u, win_ref[:, d_ff + c * tf:d_ff + (c + 1) * tf],
                     preferred_element_type=F32)
        act = (gate * jax.nn.sigmoid(gate) * up).astype(BF16)
        acc = acc + jnp.dot(act, wout_ref[c * tf:(c + 1) * tf, :],
                            preferred_element_type=F32)
    out_ref[...] = acc


def _oproj_ffn(h2, o_t, w_o, g, w_in, w_out, *, tm, tf):
    n, d = h2.shape
    d_ff = w_out.shape[0]
    assert d_ff % tf == 0
    row = pl.BlockSpec((tm, d), lambda i: (i, 0))
    return pl.pallas_call(
        functools.partial(_oproj_ffn_kernel, tf=tf), grid=(n // tm,),
        in_specs=[row, pl.BlockSpec((d, tm), lambda i: (0, i)), _resident((d, d)),
                  _resident((1, d)), _resident((d, 2 * d_ff)), _resident((d_ff, d))],
        out_specs=row,
        out_shape=jax.ShapeDtypeStruct((n, d), F32),
        compiler_params=_params(1), name="oproj_ffn")(h2, o_t, w_o, g, w_in, w_out)


def _ple_update(x_ref, g_ref, wg_ref, p_ref, wu_ref):
    x = x_ref[...]
    u = _rmsnorm(x, g_ref[...]).astype(BF16)
    gate = jax.nn.sigmoid(jnp.dot(u, wg_ref[...], preferred_element_type=F32))
    up = jnp.dot(p_ref[...].astype(BF16), wu_ref[...], preferred_element_type=F32)
    return x + gate * up


def _ple_next_kernel(x_ref, g_ref, wg_ref, p_ref, wu_ref, *refs, n_mixer):
    y = _ple_update(x_ref, g_ref, wg_ref, p_ref, wu_ref)
    refs[n_mixer][...] = y
    _mixer_inputs(y, refs[:n_mixer], refs[n_mixer + 1:])


def _ple_final_kernel(x_ref, g_ref, wg_ref, p_ref, wu_ref, fg_ref, out_ref):
    y = _ple_update(x_ref, g_ref, wg_ref, p_ref, wu_ref)
    out_ref[...] = _rmsnorm(y, fg_ref[...])


def _ple(h2, g, w_gate, p3, layer, w_up, *, tm, mixer=None, final_g=None):
    n, d = h2.shape
    pd = p3.shape[2]
    row = pl.BlockSpec((tm, d), lambda i: (i, 0))
    in_specs = [row, _resident((1, d)), _resident((d, d)),
                pl.BlockSpec((None, tm, pd), lambda i: (layer, i, 0)), _resident((pd, d))]
    h_shape = jax.ShapeDtypeStruct((n, d), F32)
    if mixer is None:
        return pl.pallas_call(
            _ple_final_kernel, grid=(n // tm,),
            in_specs=in_specs + [_resident((1, d))], out_specs=row, out_shape=h_shape,
            compiler_params=_params(1), name="ple_final")(h2, g, w_gate, p3, w_up, final_g)
    n_mixer = len(mixer.operands())
    return pl.pallas_call(
        functools.partial(_ple_next_kernel, n_mixer=n_mixer), grid=(n // tm,),
        in_specs=in_specs + mixer.in_specs(),
        out_specs=[row] + mixer.out_specs(tm), out_shape=[h_shape] + mixer.out_shapes(n),
        compiler_params=_params(1), name="ple_proj")(h2, g, w_gate, p3, w_up,
                                                     *mixer.operands())


def _row_tile(n, want):
    t = min(want, n)
    assert n % t == 0
    return t


def _col_tile(n, want):
    t = min(want, n)
    while n % t:
        t -= LANES
    return t


def kernel(x, p, attn_norm_g, fox_w_in, fox_b_f, fox_w_o, moba_w_in, moba_w_o, rel_bias_table,
           ffn_norm_g, ffn_w_in, ffn_w_out, ple_norm_g, ple_w_gate, ple_w_up, final_norm_g):
    b, s, d = x.shape
    depth = p.shape[0]
    n_heads = rel_bias_table.shape[1]
    assert d == n_heads * HEAD_DIM and n_heads % 2 == 0 and n_heads <= LANES
    n = b * s
    tm = _row_tile(n, DENSE_ROW_TILE)
    tf = _col_tile(ffn_w_out.shape[1], FFN_COL_CHUNK)

    def row_vec(v):
        return v.reshape(1, -1).astype(F32)

    def mixer(i):
        w = fox_w_in[i // 2] if i % 2 == 0 else moba_w_in[i // 2]
        parts = [row_vec(attn_norm_g[i]), (w[:, :d] * Q_SCALE).T.astype(BF16),
                 w[:, d:2 * d].astype(BF16), w[:, 2 * d:3 * d].T.astype(BF16)]
        if i % 2 == 0:
            parts += [jnp.pad(w[:, 3 * d:], ((0, 0), (0, LANES - n_heads))).astype(BF16),
                      jnp.pad(row_vec(fox_b_f[i // 2]), ((0, 0), (0, LANES - n_heads)))]
        return _Mixer(*parts)

    rel_table = rel_bias_table.astype(F32)
    bias_t = _bias_tiles(rel_table, MOBA_BLOCK)

    h = x.reshape(n, d).astype(F32)
    mixed = _project(h, mixer(0), tm=tm)
    for i in range(depth):
        qt, k3, vt = mixed[0], mixed[1].reshape(b, s, d), mixed[2]
        if i % 2 == 0:
            aqt, ak = _cumsum(mixed[3].reshape(b, s, LANES))
            o_t = _fox_attention(qt, k3, vt, aqt, ak, n_heads)
            w_o = fox_w_o[i // 2]
        else:
            o_t = _moba_attention(qt, k3, vt, rel_table, bias_t, n_heads)
            w_o = moba_w_o[i // 2]
        h = _oproj_ffn(h, o_t, w_o.astype(BF16), row_vec(ffn_norm_g[i]),
                       ffn_w_in[i].astype(BF16), ffn_w_out[i].astype(BF16), tm=tm, tf=tf)
        ple_args = (h, row_vec(ple_norm_g[i]), ple_w_gate[i].astype(BF16),
                    p.reshape(depth, n, -1), i, ple_w_up[i].astype(BF16))
        if i + 1 < depth:
            h, *mixed = _ple(*ple_args, tm=tm, mixer=mixer(i + 1))
        else:
            h = _ple(*ple_args, tm=tm, final_g=row_vec(final_norm_g))
    return h.reshape(b, s, d).astype(x.dtype)
```

```python
import functools
import math
from typing import NamedTuple, Optional

import numpy as np
import jax
import jax.numpy as jnp
from jax import lax
from jax.experimental import pallas as pl
from jax.experimental.pallas import tpu as pltpu

F32 = jnp.float32
BF16 = jnp.bfloat16

RMS_EPS = 1e-6
HEAD_DIM = 64
MOBA_BLOCK = 256
MOBA_TOP_K = 3
REL_BUCKETS = 32
REL_MAX_DIST = 128

LANES = 128
BF16_SUBLANES = 16
V_ROWS = HEAD_DIM + BF16_SUBLANES
GATE_LANES = 8
LOG2E = math.log2(math.e)
Q_SCALE = HEAD_DIM ** -0.5 * LOG2E
FOX_BLK = 512
MOBA_Q_TILE = 2 * MOBA_BLOCK
CUMSUM_BLK = 512
DENSE_ROW_TILE = 512
PROJ_COL_CHUNK = 512
FFN_COL_CHUNK = 256
VMEM_LIMIT_BYTES = 56 * 1024 * 1024
NEG_INF = float("-inf")


def _params(n_axes):
    return pltpu.CompilerParams(
        dimension_semantics=("arbitrary",) * n_axes,
        vmem_limit_bytes=VMEM_LIMIT_BYTES)


def _split3(x):
    x1 = x.astype(BF16)
    r1 = x - x1.astype(F32)
    x2 = r1.astype(BF16)
    x3 = (r1 - x2.astype(F32)).astype(BF16)
    return x1, x2, x3


def _rmsnorm(x, g):
    ms = jnp.mean(x * x, axis=-1, keepdims=True)
    return x * lax.rsqrt(ms + RMS_EPS) * g


def _log_sigmoid(x):
    return jnp.minimum(x, 0.0) - jnp.log1p(jnp.exp(-jnp.abs(x)))


def _resident(shape):
    return pl.BlockSpec(shape, lambda i: (0,) * len(shape), pipeline_mode=pl.Buffered(1))


class _Mixer(NamedTuple):
    g: jax.Array
    w_qt: jax.Array
    w_k: jax.Array
    w_vt: jax.Array
    w_f: Optional[jax.Array] = None
    b_f: Optional[jax.Array] = None

    def operands(self):
        return tuple(a for a in self if a is not None)

    def in_specs(self):
        return [_resident(a.shape) for a in self.operands()]

    def out_specs(self, tm):
        d = self.w_qt.shape[0]
        transposed = pl.BlockSpec((d, tm), lambda i: (0, i))
        specs = [transposed, pl.BlockSpec((tm, d), lambda i: (i, 0)), transposed]
        if self.w_f is not None:
            specs.append(pl.BlockSpec((tm, LANES), lambda i: (i, 0)))
        return specs

    def out_shapes(self, n):
        d = self.w_qt.shape[0]
        shapes = [jax.ShapeDtypeStruct((d, n), BF16), jax.ShapeDtypeStruct((n, d), BF16),
                  jax.ShapeDtypeStruct((d, n), BF16)]
        if self.w_f is not None:
            shapes.append(jax.ShapeDtypeStruct((n, LANES), F32))
        return shapes


def _chunks(total, want):
    step = want if total % want == 0 else total
    return [slice(c * step, (c + 1) * step) for c in range(total // step)]


def _mixer_inputs(y, mixer_refs, out_refs):
    g_ref, wqt_ref, wk_ref, wvt_ref = mixer_refs[:4]
    u = _rmsnorm(y, g_ref[...]).astype(BF16)
    for wt_ref, out_ref in ((wqt_ref, out_refs[0]), (wvt_ref, out_refs[2])):
        for rows in _chunks(wt_ref.shape[0], PROJ_COL_CHUNK):
            out_ref[rows, :] = lax.dot_general(
                wt_ref[rows, :], u, (((1,), (1,)), ((), ())),
                preferred_element_type=F32).astype(BF16)
    for cols in _chunks(wk_ref.shape[1], PROJ_COL_CHUNK):
        out_refs[1][:, cols] = jnp.dot(u, wk_ref[:, cols],
                                       preferred_element_type=F32).astype(BF16)
    if len(mixer_refs) > 4:
        wf_ref, bf_ref = mixer_refs[4:]
        f_logit = jnp.dot(u, wf_ref[...], preferred_element_type=F32) + bf_ref[...]
        out_refs[3][...] = _log_sigmoid(f_logit)


def _proj_kernel(x_ref, *refs, n_mixer):
    _mixer_inputs(x_ref[...], refs[:n_mixer], refs[n_mixer:])


def _project(h2, mixer, *, tm):
    n, d = h2.shape
    n_mixer = len(mixer.operands())
    return pl.pallas_call(
        functools.partial(_proj_kernel, n_mixer=n_mixer), grid=(n // tm,),
        in_specs=[pl.BlockSpec((tm, d), lambda i: (i, 0))] + mixer.in_specs(),
        out_specs=mixer.out_specs(tm), out_shape=mixer.out_shapes(n),
        compiler_params=_params(1), name="proj")(h2, *mixer.operands())


def _cumsum_kernel(lf_ref, aqt_ref, ak_ref, carry_ref):
    @pl.when(pl.program_id(1) == 0)
    def _():
        carry_ref[...] = jnp.zeros_like(carry_ref)

    t = lf_ref.shape[1]
    row = lax.broadcasted_iota(jnp.int32, (t, t), 0)
    col = lax.broadcasted_iota(jnp.int32, (t, t), 1)
    tril = jnp.where(col <= row, 1.0, 0.0).astype(BF16)
    x1, x2, x3 = _split3(lf_ref[0])
    cs = (jnp.dot(tril, x1, preferred_element_type=F32)
          + jnp.dot(tril, x2, preferred_element_type=F32)
          + jnp.dot(tril, x3, preferred_element_type=F32))
    cs = cs + carry_ref[0:1, :]
    carry_ref[...] = jnp.broadcast_to(cs[t - 1:t, :], carry_ref.shape)

    src = lax.broadcasted_iota(jnp.int32, (LANES, LANES), 0)
    dst = lax.broadcasted_iota(jnp.int32, (LANES, LANES), 1)
    lane = lax.broadcasted_iota(jnp.int32, (1, LANES), 1) & (GATE_LANES - 1)
    aq = jnp.where((lane >= 3) & (lane < 6), 1.0, 0.0)
    ak = jnp.where(lane < 3, 1.0, 0.0)
    for i, part in enumerate(_split3(cs * LOG2E)):
        to_q = jnp.where(dst == GATE_LANES * src + i, 1.0, 0.0).astype(BF16)
        to_k = jnp.where(dst == GATE_LANES * src + 3 + i, 1.0, 0.0).astype(BF16)
        aq = aq + jnp.dot(part, to_q, preferred_element_type=F32)
        ak = ak - jnp.dot(part, to_k, preferred_element_type=F32)
    eye = jnp.where(src == dst, 1.0, 0.0).astype(BF16)
    aqt_ref[0] = lax.dot_general(eye, aq.astype(BF16), (((1,), (1,)), ((), ())),
                                 preferred_element_type=F32).astype(BF16)
    ak_ref[0] = ak.astype(BF16)


def _cumsum(lf3):
    b, s, _ = lf3.shape
    t = min(CUMSUM_BLK, s)
    spec = pl.BlockSpec((1, t, LANES), lambda i, j: (i, j, 0))
    return pl.pallas_call(
        _cumsum_kernel, grid=(b, s // t),
        in_specs=[spec],
        out_specs=[pl.BlockSpec((1, LANES, t), lambda i, j: (i, 0, j)), spec],
        out_shape=[jax.ShapeDtypeStruct((b, LANES, s), BF16),
                   jax.ShapeDtypeStruct(lf3.shape, BF16)],
        scratch_shapes=[pltpu.VMEM((8, LANES), F32)],
        compiler_params=_params(2), name="gate_cumsum")(lf3)


def _fill_value_rows(vt_ref, v_ref, blk):
    row = lax.broadcasted_iota(jnp.int32, (V_ROWS - HEAD_DIM, blk), 0)
    tail = jnp.where(row == 0, 1.0, 0.0).astype(BF16)
    for hh in range(2):
        for jb in range(v_ref.shape[1] // blk):
            head_rows = v_ref[hh * HEAD_DIM:(hh + 1) * HEAD_DIM, jb * blk:(jb + 1) * blk]
            vt_ref[hh, jb] = jnp.concatenate([head_rows, tail], axis=0)


def _flash_init(m_ref, acc_ref):
    m_ref[...] = jnp.full(m_ref.shape, NEG_INF, F32)
    acc_ref[...] = jnp.zeros(acc_ref.shape, F32)


def _pipelined_blocks(n, last_block, scores_into, update, final_update, final_scores=None):
    def step(j_next, j, slot):
        for hh in range(2):
            scores_into(j_next, 1 - slot, hh)
            update(j, slot, hh)

    for hh in range(2):
        scores_into(0, 0, hh)

    def pair(jj, carry):
        j = 2 * jj
        step(jnp.minimum(j + 1, last_block), j, 0)
        step(jnp.minimum(j + 2, last_block), j + 1, 1)
        return carry
    lax.fori_loop(0, n // 2, pair, 0)

    @pl.when(n % 2 == 1)
    def _():
        for hh in range(2):
            if final_scores is None:
                scores_into(n, 1, hh)
            else:
                final_scores(hh)
            update(n - 1, 0, hh)
        final_update(1)

    @pl.when(n % 2 == 0)
    def _():
        if final_scores is not None:
            for hh in range(2):
                final_scores(hh)
        final_update(0)


def _store_scores(s_t, s_ref, mx_ref, idx):
    s_ref[idx] = s_t
    mx_ref[idx] = jnp.max(s_t, axis=0, keepdims=True)


def _flash_update(s_t, mx, v_rows, m_ref, acc_ref, hh, keep=None, const=None,
                  cols=slice(None)):
    m_old = m_ref[hh, :, cols]
    if const is not None:
        mx = mx + const
    if keep is not None:
        mx = jnp.where(keep, mx, NEG_INF)
    m_new = jnp.maximum(m_old, mx)
    m_safe = jnp.where(m_new == NEG_INF, 0.0, m_new)
    shift = m_safe if const is None else m_safe - const
    if keep is not None:
        shift = jnp.where(keep, shift, float("inf"))
    p = jnp.exp2(s_t - shift).astype(BF16)
    alpha = jnp.exp2(m_old - m_safe)
    m_ref[hh, :, cols] = m_new
    if not isinstance(v_rows, (list, tuple)):
        v_rows = [v_rows]
    keys = p.shape[0] // len(v_rows)
    acc = alpha * acc_ref[hh, :, cols]
    for i, v_i in enumerate(v_rows):
        acc = acc + jnp.dot(v_i, p[i * keys:(i + 1) * keys], preferred_element_type=F32)
    acc_ref[hh, :, cols] = acc


def _write_heads(o_ref, acc_ref):
    outs = []
    for hh in range(2):
        acc = acc_ref[hh]
        outs.append(acc[:HEAD_DIM] / acc[HEAD_DIM:HEAD_DIM + 1])
    o_ref[...] = jnp.concatenate(outs, axis=0).astype(o_ref.dtype)


def _head_row_mask(hh):
    row = lax.broadcasted_iota(jnp.int32, (LANES, 1), 0)
    return (row < HEAD_DIM) if hh == 0 else (row >= HEAD_DIM)


def _fox_kernel(qt_ref, k_ref, v_ref, aqt_ref, ak_ref, o_ref,
                vt_ref, s_ref, mx_ref, m_ref, acc_ref):
    hp = pl.program_id(1)
    qi = pl.program_id(2)
    blk = qt_ref.shape[1]
    nblk = k_ref.shape[1] // blk

    @pl.when(qi == 0)
    def _():
        _fill_value_rows(vt_ref, v_ref, blk)

    gate_row = lax.broadcasted_iota(jnp.int32, (LANES, 1), 0)
    w_q = []
    for hh in range(2):
        first = GATE_LANES * (2 * hp + hh)
        own_gate = (gate_row >= first) & (gate_row < first + GATE_LANES)
        q_rows = jnp.where(_head_row_mask(hh), qt_ref[...], jnp.zeros_like(qt_ref[...]))
        g_rows = jnp.where(own_gate, aqt_ref[0], jnp.zeros_like(aqt_ref[0]))
        w_q.append(jnp.concatenate([q_rows, g_rows], axis=0))

    def scores_into(j, slot, hh):
        rows = pl.ds(pl.multiple_of(j * blk, blk), blk)
        keys = jnp.concatenate([k_ref[0, rows, :], ak_ref[0, rows, :]], axis=1)
        s_t = jnp.dot(keys, w_q[hh], preferred_element_type=F32)
        _store_scores(s_t, s_ref, mx_ref, (slot, hh))

    def update(j, slot, hh):
        _flash_update(s_ref[slot, hh], mx_ref[slot, hh], vt_ref[hh, j], m_ref, acc_ref, hh)

    def diagonal_update(slot):
        half = blk // 2
        lo, hi = slice(0, half), slice(half, blk)
        causal_lo = (lax.broadcasted_iota(jnp.int32, (half, half), 0)
                     <= lax.broadcasted_iota(jnp.int32, (half, half), 1))
        causal_hi = (lax.broadcasted_iota(jnp.int32, (blk, half), 0)
                     <= lax.broadcasted_iota(jnp.int32, (blk, half), 1) + half)
        for hh in range(2):
            v_rows = vt_ref[hh, qi]
            s_lo = jnp.where(causal_lo, s_ref[slot, hh, lo, lo], NEG_INF)
            _flash_update(s_lo, jnp.max(s_lo, axis=0, keepdims=True),
                          v_rows[:, lo], m_ref, acc_ref, hh, cols=lo)
            s_hi = jnp.where(causal_hi, s_ref[slot, hh, :, hi], NEG_INF)
            _flash_update(s_hi, jnp.max(s_hi, axis=0, keepdims=True),
                          v_rows, m_ref, acc_ref, hh, cols=hi)

    _flash_init(m_ref, acc_ref)
    _pipelined_blocks(qi, nblk - 1, scores_into, update, diagonal_update)
    _write_heads(o_ref, acc_ref)


def _fox_attention(qt, k3, vt, aqt3, ak3, n_heads):
    b, s, d = k3.shape
    assert n_heads * GATE_LANES <= LANES
    blk = min(FOX_BLK, s)
    npair = n_heads // 2
    nq = s // blk
    tile_t = pl.BlockSpec((LANES, blk), lambda bi, hp, qi: (hp, bi * nq + qi))
    return pl.pallas_call(
        _fox_kernel, grid=(b, npair, nq),
        in_specs=[
            tile_t,
            pl.BlockSpec((1, s, LANES), lambda bi, hp, qi: (bi, 0, hp)),
            pl.BlockSpec((LANES, s), lambda bi, hp, qi: (hp, bi)),
            pl.BlockSpec((1, LANES, blk), lambda bi, hp, qi: (bi, 0, qi)),
            pl.BlockSpec((1, s, LANES), lambda bi, hp, qi: (bi, 0, 0)),
        ],
        out_specs=tile_t,
        out_shape=jax.ShapeDtypeStruct((d, b * s), BF16),
        scratch_shapes=[
            pltpu.VMEM((2, s // blk, V_ROWS, blk), BF16),
            pltpu.VMEM((2, 2, blk, blk), F32),
            pltpu.VMEM((2, 2, 1, blk), F32),
            pltpu.VMEM((2, 1, blk), F32),
            pltpu.VMEM((2, V_ROWS, blk), F32),
        ],
        compiler_params=_params(3), name="fox_attention")(qt, k3, vt, aqt3, ak3)


def _t5_bucket_np(n):
    max_exact = REL_BUCKETS // 2
    nf = np.maximum(n, 1).astype(np.float64)
    large = max_exact + (np.log(nf / max_exact) / math.log(REL_MAX_DIST / max_exact)
                         * (REL_BUCKETS - max_exact)).astype(np.int32)
    return np.where(n < max_exact, n, np.minimum(large, REL_BUCKETS - 1)).astype(np.int32)


def _bucket_tiles(blk):
    key = np.arange(blk)[:, None]
    qry = np.arange(blk)[None, :]
    own = np.where(key <= qry, _t5_bucket_np(np.maximum(qry - key, 0)), -1)
    prev = _t5_bucket_np(blk + qry - key)
    return np.stack([own, prev]).astype(np.int32)


def _bias_kernel(tab_ref, bucket_ref, o_ref):
    h = pl.program_id(0)
    bucket = bucket_ref[...]
    acc = jnp.where(bucket < 0, NEG_INF, 0.0).astype(F32)
    for bkt in range(REL_BUCKETS):
        acc = jnp.where(bucket == bkt, tab_ref[bkt, h] * LOG2E, acc)
    o_ref[0] = acc


def _bias_tiles(rel_table, blk):
    n_heads = rel_table.shape[1]
    buckets = jnp.asarray(_bucket_tiles(blk))
    return pl.pallas_call(
        _bias_kernel, grid=(n_heads,),
        in_specs=[pl.BlockSpec(memory_space=pltpu.SMEM),
                  pl.BlockSpec((2, blk, blk), lambda h: (0, 0, 0))],
        out_specs=pl.BlockSpec((1, 2, blk, blk), lambda h: (h, 0, 0, 0)),
        out_shape=jax.ShapeDtypeStruct((n_heads, 2, blk, blk), F32),
        compiler_params=_params(1), name="t5_bias_tiles")(rel_table, buckets)


def _moba_kernel(tab_ref, qt_ref, k_ref, v_ref, bias_ref, o_ref,
                 vt_ref, km_ref, sel_ref, s_ref, mx_ref, near_ref, own1_ref, m_ref, acc_ref):
    hp = pl.program_id(1)
    qi = pl.program_id(2)
    blk = MOBA_BLOCK
    tq = qt_ref.shape[1]
    nblk = k_ref.shape[1] // blk
    assert tq == 2 * blk and blk >= REL_MAX_DIST
    first_own = 2 * qi

    @pl.when(qi == 0)
    def _():
        def build(jb, carry):
            rows = pl.ds(pl.multiple_of(jb * blk, blk), blk)
            km_ref[pl.ds(jb, 1), :] = jnp.mean(k_ref[0, rows, :].astype(F32),
                                                axis=0, keepdims=True)
            return carry
        lax.fori_loop(0, nblk, build, 0)
        _fill_value_rows(vt_ref, v_ref, blk)

    qt = qt_ref[...]
    km_parts = _split3(km_ref[...])
    blk_id = lax.broadcasted_iota(jnp.int32, (nblk, tq), 0)
    qpos = lax.broadcasted_iota(jnp.int32, (nblk, tq), 1)
    own = first_own + jnp.where(qpos >= blk, 1, 0)
    past = blk_id < own
    q_m_t = []
    for hh in range(2):
        q_m_t.append(jnp.where(_head_row_mask(hh), qt, jnp.zeros_like(qt)))

        gate = None
        for part in km_parts:
            term = jnp.dot(part, q_m_t[hh], preferred_element_type=F32)
            gate = term if gate is None else gate + term
        work = jnp.where(past, gate, NEG_INF)
        picked = jnp.zeros(gate.shape, F32)
        for _ in range(MOBA_TOP_K):
            best = jnp.max(work, axis=0, keepdims=True)
            first = jnp.min(jnp.where(work == best, blk_id, nblk), axis=0, keepdims=True)
            hit = blk_id == first
            picked = jnp.where(hit, 1.0, picked)
            work = jnp.where(hit, NEG_INF, work)
        keep_all = ((picked > 0.5) & past) | (blk_id == own) | (blk_id == first_own + 1)
        sel_ref[hh] = jnp.where(keep_all, 1.0, 0.0)

    def scores(hh, j):
        rows = pl.ds(pl.multiple_of(j * blk, blk), blk)
        return jnp.dot(k_ref[0, rows, :], q_m_t[hh], preferred_element_type=F32)

    def keep(hh, j):
        return sel_ref[hh, pl.ds(j, 1), :] > 0.5

    _flash_init(m_ref, acc_ref)
    far_bias = [tab_ref[REL_BUCKETS - 1, 2 * hp + hh] * LOG2E for hh in range(2)]

    j_prev = jnp.maximum(first_own - 1, 0)
    lo, hi = slice(0, blk), slice(blk, 2 * blk)
    near_mx = {}

    def near_scores(hh):
        own_t, prev_t = bias_ref[hh, 0], bias_ref[hh, 1]
        mask_prev = jnp.where(keep(hh, j_prev) & (qi >= 1), 0.0, NEG_INF)
        mask_own = jnp.where(keep(hh, first_own), 0.0, NEG_INF)
        far_t = jnp.full((blk, blk), far_bias[hh], F32)
        parts = [scores(hh, j_prev) + mask_prev + jnp.concatenate([prev_t, far_t], axis=1),
                 scores(hh, first_own) + mask_own + jnp.concatenate([own_t, prev_t], axis=1)]
        mx = None
        for i, part in enumerate(parts):
            near_ref[hh, i * blk:(i + 1) * blk] = part
            part_mx = jnp.max(part, axis=0, keepdims=True)
            mx = part_mx if mx is None else jnp.maximum(mx, part_mx)
        rows = pl.ds(pl.multiple_of((first_own + 1) * blk, blk), blk)
        last = jnp.dot(k_ref[0, rows, :], q_m_t[hh][:, hi], preferred_element_type=F32) + own_t
        own1_ref[hh] = last
        near_mx[hh] = (mx[:, lo], jnp.maximum(mx[:, hi], jnp.max(last, axis=0, keepdims=True)))

    def near_update(slot):
        del slot
        for hh in range(2):
            v_near = [vt_ref[hh, j_prev], vt_ref[hh, first_own], vt_ref[hh, first_own + 1]]
            _flash_update(near_ref[hh, :, lo], near_mx[hh][0], v_near[:2],
                          m_ref, acc_ref, hh, cols=lo)
            _flash_update(jnp.concatenate([near_ref[hh, :, hi], own1_ref[hh]], axis=0),
                          near_mx[hh][1], v_near, m_ref, acc_ref, hh, cols=hi)

    def scores_into(j, slot, hh):
        _store_scores(scores(hh, j), s_ref, mx_ref, (slot, hh))

    def update(j, slot, hh):
        _flash_update(s_ref[slot, hh], mx_ref[slot, hh], vt_ref[hh, j],
                      m_ref, acc_ref, hh, keep=keep(hh, j), const=far_bias[hh])

    _pipelined_blocks(jnp.maximum(first_own - 1, 0), nblk - 1, scores_into, update,
                      near_update, final_scores=near_scores)
    _write_heads(o_ref, acc_ref)


def _moba_attention(qt, k3, vt, rel_table, bias_t, n_heads):
    b, s, d = k3.shape
    blk = MOBA_BLOCK
    tq = MOBA_Q_TILE
    assert s % tq == 0
    npair = n_heads // 2
    nblk = s // blk
    nq = s // tq
    tile_t = pl.BlockSpec((LANES, tq), lambda bi, hp, qi: (hp, bi * nq + qi))
    return pl.pallas_call(
        _moba_kernel, grid=(b, npair, nq),
        in_specs=[
            pl.BlockSpec(memory_space=pltpu.SMEM),
            tile_t,
            pl.BlockSpec((1, s, LANES), lambda bi, hp, qi: (bi, 0, hp)),
            pl.BlockSpec((LANES, s), lambda bi, hp, qi: (hp, bi)),
            pl.BlockSpec((2, 2, blk, blk), lambda bi, hp, qi: (hp, 0, 0, 0)),
        ],
        out_specs=tile_t,
        out_shape=jax.ShapeDtypeStruct((d, b * s), BF16),
        scratch_shapes=[
            pltpu.VMEM((2, nblk, V_ROWS, blk), BF16),
            pltpu.VMEM((nblk, LANES), F32),
            pltpu.VMEM((2, nblk, tq), F32),
            pltpu.VMEM((2, 2, blk, tq), F32),
            pltpu.VMEM((2, 2, 1, tq), F32),
            pltpu.VMEM((2, 2 * blk, tq), F32),
            pltpu.VMEM((2, blk, blk), F32),
            pltpu.VMEM((2, 1, tq), F32),
            pltpu.VMEM((2, V_ROWS, tq), F32),
        ],
        compiler_params=_params(3), name="moba_attention")(
            rel_table, qt, k3, vt, bias_t)


def _oproj_ffn_kernel(h_ref, ot_ref, wo_ref, g_ref, win_ref, wout_ref, out_ref, *, tf):
    d_ff = wout_ref.shape[0]
    h1 = h_ref[...] + lax.dot_general(ot_ref[...], wo_ref[...], (((0,), (0,)), ((), ())),
                                      preferred_element_type=F32)
    u = _rmsnorm(h1, g_ref[...]).astype(BF16)
    acc = h1
    for c in range(d_ff // tf):
        gate = jnp.dot(u, win_ref[:, c * tf:(c + 1) * tf], preferred_element_type=F32)
        up = jnp.dot(u, win_ref[:, d_ff + c * tf:d_ff + (c + 1) * tf],
                     preferred_element_type=F32)
        act = (gate * jax.nn.sigmoid(gate) * up).astype(BF16)
        acc = acc + jnp.dot(act, wout_ref[c * tf:(c + 1) * tf, :],
                            preferred_element_type=F32)
    out_ref[...] = acc


def _oproj_ffn(h2, o_t, w_o, g, w_in, w_out, *, tm, tf):
    n, d = h2.shape
    d_ff = w_out.shape[0]
    assert d_ff % tf == 0
    row = pl.BlockSpec((tm, d), lambda i: (i, 0))
    return pl.pallas_call(
        functools.partial(_oproj_ffn_kernel, tf=tf), grid=(n // tm,),
        in_specs=[row, pl.BlockSpec((d, tm), lambda i: (0, i)), _resident((d, d)),
                  _resident((1, d)), _resident((d, 2 * d_ff)), _resident((d_ff, d))],
        out_specs=row,
        out_shape=jax.ShapeDtypeStruct((n, d), F32),
        compiler_params=_params(1), name="oproj_ffn")(h2, o_t, w_o, g, w_in, w_out)


def _ple_update(x_ref, g_ref, wg_ref, p_ref, wu_ref):
    x = x_ref[...]
    u = _rmsnorm(x, g_ref[...]).astype(BF16)
    gate = jax.nn.sigmoid(jnp.dot(u, wg_ref[...], preferred_element_type=F32))
    up = jnp.dot(p_ref[...].astype(BF16), wu_ref[...], preferred_element_type=F32)
    return x + gate * up


def _ple_next_kernel(x_ref, g_ref, wg_ref, p_ref, wu_ref, *refs, n_mixer):
    y = _ple_update(x_ref, g_ref, wg_ref, p_ref, wu_ref)
    refs[n_mixer][...] = y
    _mixer_inputs(y, refs[:n_mixer], refs[n_mixer + 1:])


def _ple_final_kernel(x_ref, g_ref, wg_ref, p_ref, wu_ref, fg_ref, out_ref):
    y = _ple_update(x_ref, g_ref, wg_ref, p_ref, wu_ref)
    out_ref[...] = _rmsnorm(y, fg_ref[...])


def _ple(h2, g, w_gate, p3, layer, w_up, *, tm, mixer=None, final_g=None):
    n, d = h2.shape
    pd = p3.shape[2]
    row = pl.BlockSpec((tm, d), lambda i: (i, 0))
    in_specs = [row, _resident((1, d)), _resident((d, d)),
                pl.BlockSpec((None, tm, pd), lambda i: (layer, i, 0)), _resident((pd, d))]
    h_shape = jax.ShapeDtypeStruct((n, d), F32)
    if mixer is None:
        return pl.pallas_call(
            _ple_final_kernel, grid=(n // tm,),
            in_specs=in_specs + [_resident((1, d))], out_specs=row, out_shape=h_shape,
            compiler_params=_params(1), name="ple_final")(h2, g, w_gate, p3, w_up, final_g)
    n_mixer = len(mixer.operands())
    return pl.pallas_call(
        functools.partial(_ple_next_kernel, n_mixer=n_mixer), grid=(n // tm,),
        in_specs=in_specs + mixer.in_specs(),
        out_specs=[row] + mixer.out_specs(tm), out_shape=[h_shape] + mixer.out_shapes(n),
        compiler_params=_params(1), name="ple_proj")(h2, g, w_gate, p3, w_up,
                                                     *mixer.operands())


def _row_tile(n, want):
    t = min(want, n)
    assert n % t == 0
    return t


def _col_tile(n, want):
    t = min(want, n)
    while n % t:
        t -= LANES
    return t


def kernel(x, p, attn_norm_g, fox_w_in, fox_b_f, fox_w_o, moba_w_in, moba_w_o, rel_bias_table,
           ffn_norm_g, ffn_w_in, ffn_w_out, ple_norm_g, ple_w_gate, ple_w_up, final_norm_g):
    b, s, d = x.shape
    depth = p.shape[0]
    n_heads = rel_bias_table.shape[1]
    assert d == n_heads * HEAD_DIM and n_heads % 2 == 0 and n_heads <= LANES
    n = b * s
    tm = _row_tile(n, DENSE_ROW_TILE)
    tf = _col_tile(ffn_w_out.shape[1], FFN_COL_CHUNK)

    def row_vec(v):
        return v.reshape(1, -1).astype(F32)

    def mixer(i):
        w = fox_w_in[i // 2] if i % 2 == 0 else moba_w_in[i // 2]
        parts = [row_vec(attn_norm_g[i]), (w[:, :d] * Q_SCALE).T.astype(BF16),
                 w[:, d:2 * d].astype(BF16), w[:, 2 * d:3 * d].T.astype(BF16)]
        if i % 2 == 0:
            parts += [jnp.pad(w[:, 3 * d:], ((0, 0), (0, LANES - n_heads))).astype(BF16),
                      jnp.pad(row_vec(fox_b_f[i // 2]), ((0, 0), (0, LANES - n_heads)))]
        return _Mixer(*parts)

    rel_table = rel_bias_table.astype(F32)
    bias_t = _bias_tiles(rel_table, MOBA_BLOCK)

    h = x.reshape(n, d).astype(F32)
    mixed = _project(h, mixer(0), tm=tm)
    for i in range(depth):
        qt, k3, vt = mixed[0], mixed[1].reshape(b, s, d), mixed[2]
        if i % 2 == 0:
            aqt, ak = _cumsum(mixed[3].reshape(b, s, LANES))
            o_t = _fox_attention(qt, k3, vt, aqt, ak, n_heads)
            w_o = fox_w_o[i // 2]
        else:
            o_t = _moba_attention(qt, k3, vt, rel_table, bias_t, n_heads)
            w_o = moba_w_o[i // 2]
        h = _oproj_ffn(h, o_t, w_o.astype(BF16), row_vec(ffn_norm_g[i]),
                       ffn_w_in[i].astype(BF16), ffn_w_out[i].astype(BF16), tm=tm, tf=tf)
        ple_args = (h, row_vec(ple_norm_g[i]), ple_w_gate[i].astype(BF16),
                    p.reshape(depth, n, -1), i, ple_w_up[i].astype(BF16))
        if i + 1 < depth:
            h, *mixed = _ple(*ple_args, tm=tm, mixer=mixer(i + 1))
        else:
            h = _ple(*ple_args, tm=tm, final_g=row_vec(final_norm_g))
    return h.reshape(b, s, d).astype(x.dtype)
```

```python
import functools
import math
from typing import NamedTuple, Optional

import numpy as np
import jax
import jax.numpy as jnp
from jax import lax
from jax.experimental import pallas as pl
from jax.experimental.pallas import tpu as pltpu

F32 = jnp.float32
BF16 = jnp.bfloat16

RMS_EPS = 1e-6
HEAD_DIM = 64
MOBA_BLOCK = 256
MOBA_TOP_K = 3
REL_BUCKETS = 32
REL_MAX_DIST = 128

LANES = 128
BF16_SUBLANES = 16
V_ROWS = HEAD_DIM + BF16_SUBLANES
GATE_LANES = 8
LOG2E = math.log2(math.e)
Q_SCALE = HEAD_DIM ** -0.5 * LOG2E
MOBA_Q_TILE = 2 * MOBA_BLOCK
ATTN_Q_TILE = MOBA_Q_TILE
DENSE_ROW_TILE = ATTN_Q_TILE
CUMSUM_BLK = ATTN_Q_TILE
PROJ_COL_CHUNK = 512
FFN_COL_CHUNK = 256
VMEM_LIMIT_BYTES = 56 * 1024 * 1024
NEG_INF = float("-inf")


def _params(n_axes):
    return pltpu.CompilerParams(
        dimension_semantics=("arbitrary",) * n_axes,
        vmem_limit_bytes=VMEM_LIMIT_BYTES)


def _split3(x):
    x1 = x.astype(BF16)
    r1 = x - x1.astype(F32)
    x2 = r1.astype(BF16)
    x3 = (r1 - x2.astype(F32)).astype(BF16)
    return x1, x2, x3


def _rmsnorm(x, g):
    ms = jnp.mean(x * x, axis=-1, keepdims=True)
    return x * lax.rsqrt(ms + RMS_EPS) * g


def _log_sigmoid(x):
    return jnp.minimum(x, 0.0) - jnp.log1p(jnp.exp(-jnp.abs(x)))


def _resident(shape):
    return pl.BlockSpec(shape, lambda i: (0,) * len(shape), pipeline_mode=pl.Buffered(1))


class _Mixer(NamedTuple):
    g: jax.Array
    w_qt: jax.Array
    w_k: jax.Array
    w_vt: jax.Array
    w_f: Optional[jax.Array] = None
    b_f: Optional[jax.Array] = None

    def operands(self):
        return tuple(a for a in self if a is not None)

    def in_specs(self):
        return [_resident(a.shape) for a in self.operands()]

    def out_specs(self, tm):
        d = self.w_qt.shape[0]
        specs = [pl.BlockSpec((None, d, tm), lambda i: (i, 0, 0)),
                 pl.BlockSpec((tm, d), lambda i: (i, 0)),
                 pl.BlockSpec((d, tm), lambda i: (0, i))]
        if self.w_f is not None:
            specs.append(pl.BlockSpec((tm, LANES), lambda i: (i, 0)))
        return specs

    def out_shapes(self, n, tm):
        d = self.w_qt.shape[0]
        shapes = [jax.ShapeDtypeStruct((n // tm, d, tm), BF16),
                  jax.ShapeDtypeStruct((n, d), BF16), jax.ShapeDtypeStruct((d, n), BF16)]
        if self.w_f is not None:
            shapes.append(jax.ShapeDtypeStruct((n, LANES), F32))
        return shapes


def _chunks(total, want):
    step = want if total % want == 0 else total
    return [slice(c * step, (c + 1) * step) for c in range(total // step)]


def _mixer_inputs(y, mixer_refs, out_refs):
    g_ref, wqt_ref, wk_ref, wvt_ref = mixer_refs[:4]
    u = _rmsnorm(y, g_ref[...]).astype(BF16)
    for wt_ref, out_ref in ((wqt_ref, out_refs[0]), (wvt_ref, out_refs[2])):
        for rows in _chunks(wt_ref.shape[0], PROJ_COL_CHUNK):
            out_ref[rows, :] = lax.dot_general(
                wt_ref[rows, :], u, (((1,), (1,)), ((), ())),
                preferred_element_type=F32).astype(BF16)
    for cols in _chunks(wk_ref.shape[1], PROJ_COL_CHUNK):
        out_refs[1][:, cols] = jnp.dot(u, wk_ref[:, cols],
                                       preferred_element_type=F32).astype(BF16)
    if len(mixer_refs) > 4:
        wf_ref, bf_ref = mixer_refs[4:]
        f_logit = jnp.dot(u, wf_ref[...], preferred_element_type=F32) + bf_ref[...]
        out_refs[3][...] = _log_sigmoid(f_logit)


def _proj_kernel(x_ref, *refs, n_mixer):
    _mixer_inputs(x_ref[...], refs[:n_mixer], refs[n_mixer:])


def _project(h2, mixer, *, tm):
    n, d = h2.shape
    n_mixer = len(mixer.operands())
    return pl.pallas_call(
        functools.partial(_proj_kernel, n_mixer=n_mixer), grid=(n // tm,),
        in_specs=[pl.BlockSpec((tm, d), lambda i: (i, 0))] + mixer.in_specs(),
        out_specs=mixer.out_specs(tm), out_shape=mixer.out_shapes(n, tm),
        compiler_params=_params(1), name="proj")(h2, *mixer.operands())


def _cumsum_kernel(lf_ref, aqt_ref, ak_ref, carry_ref):
    @pl.when(pl.program_id(1) == 0)
    def _():
        carry_ref[...] = jnp.zeros_like(carry_ref)

    t = lf_ref.shape[1]
    row = lax.broadcasted_iota(jnp.int32, (t, t), 0)
    col = lax.broadcasted_iota(jnp.int32, (t, t), 1)
    tril = jnp.where(col <= row, 1.0, 0.0).astype(BF16)
    x1, x2, x3 = _split3(lf_ref[0])
    cs = (jnp.dot(tril, x1, preferred_element_type=F32)
          + jnp.dot(tril, x2, preferred_element_type=F32)
          + jnp.dot(tril, x3, preferred_element_type=F32))
    cs = cs + carry_ref[0:1, :]
    carry_ref[...] = jnp.broadcast_to(cs[t - 1:t, :], carry_ref.shape)

    src = lax.broadcasted_iota(jnp.int32, (LANES, LANES), 0)
    dst = lax.broadcasted_iota(jnp.int32, (LANES, LANES), 1)
    lane = lax.broadcasted_iota(jnp.int32, (1, LANES), 1) & (GATE_LANES - 1)
    aq = jnp.where((lane >= 3) & (lane < 6), 1.0, 0.0)
    ak = jnp.where(lane < 3, 1.0, 0.0)
    for i, part in enumerate(_split3(cs * LOG2E)):
        to_q = jnp.where(dst == GATE_LANES * src + i, 1.0, 0.0).astype(BF16)
        to_k = jnp.where(dst == GATE_LANES * src + 3 + i, 1.0, 0.0).astype(BF16)
        aq = aq + jnp.dot(part, to_q, preferred_element_type=F32)
        ak = ak - jnp.dot(part, to_k, preferred_element_type=F32)
    eye = jnp.where(src == dst, 1.0, 0.0).astype(BF16)
    aqt_ref[0] = lax.dot_general(eye, aq.astype(BF16), (((1,), (1,)), ((), ())),
                                 preferred_element_type=F32).astype(BF16)
    ak_ref[0] = ak.astype(BF16)


def _cumsum(lf3):
    b, s, _ = lf3.shape
    t = min(CUMSUM_BLK, s)
    nt = s // t
    spec = pl.BlockSpec((1, t, LANES), lambda i, j: (i, j, 0))
    return pl.pallas_call(
        _cumsum_kernel, grid=(b, nt),
        in_specs=[spec],
        out_specs=[pl.BlockSpec((1, LANES, t), lambda i, j: (i * nt + j, 0, 0)), spec],
        out_shape=[jax.ShapeDtypeStruct((b * nt, LANES, t), BF16),
                   jax.ShapeDtypeStruct(lf3.shape, BF16)],
        scratch_shapes=[pltpu.VMEM((8, LANES), F32)],
        compiler_params=_params(2), name="gate_cumsum")(lf3)


def _fill_value_rows(vt_ref, v_ref, blk):
    row = lax.broadcasted_iota(jnp.int32, (V_ROWS - HEAD_DIM, blk), 0)
    tail = jnp.where(row == 0, 1.0, 0.0).astype(BF16)
    for hh in range(2):
        for jb in range(v_ref.shape[1] // blk):
            head_rows = v_ref[hh * HEAD_DIM:(hh + 1) * HEAD_DIM, jb * blk:(jb + 1) * blk]
            vt_ref[hh, jb] = jnp.concatenate([head_rows, tail], axis=0)


def _flash_init(m_ref, acc_ref):
    m_ref[...] = jnp.full(m_ref.shape, NEG_INF, F32)
    acc_ref[...] = jnp.zeros(acc_ref.shape, F32)


def _pipelined_blocks(n, scores_into, update, final_scores, final_update, next_tile_scores):
    def step(j_next, j, slot):
        for hh in range(2):
            scores_into(j_next, 1 - slot, hh)
            update(j, slot, hh)

    def last_update_and_final(j, slot):
        for hh in range(2):
            final_scores(hh)
            update(j, slot, hh)
        next_tile_scores()
        final_update()

    def pair(jj, carry):
        j = 2 * jj
        step(j + 1, j, 0)
        step(j + 2, j + 1, 1)
        return carry
    lax.fori_loop(0, jnp.maximum(n - 1, 0) // 2, pair, 0)

    @pl.when(n % 2 == 1)
    def _():
        last_update_and_final(n - 1, 0)

    @pl.when((n % 2 == 0) & (n > 0))
    def _():
        step(n - 1, n - 2, 0)
        last_update_and_final(n - 1, 1)

    @pl.when(n == 0)
    def _():
        for hh in range(2):
            final_scores(hh)
        next_tile_scores()
        final_update()


def _store_scores(s_t, s_ref, mx_ref, idx):
    s_ref[idx] = s_t
    mx_ref[idx] = jnp.max(s_t, axis=0, keepdims=True)


def _flash_update(s_t, mx, v_rows, m_ref, acc_ref, hh, keep=None, const=None,
                  cols=slice(None)):
    m_old = m_ref[hh, :, cols]
    if const is not None:
        mx = mx + const
    if keep is not None:
        mx = jnp.where(keep, mx, NEG_INF)
    m_new = jnp.maximum(m_old, mx)
    m_safe = jnp.where(m_new == NEG_INF, 0.0, m_new)
    shift = m_safe if const is None else m_safe - const
    if keep is not None:
        shift = jnp.where(keep, shift, float("inf"))
    p = jnp.exp2(s_t - shift).astype(BF16)
    alpha = jnp.exp2(m_old - m_safe)
    m_ref[hh, :, cols] = m_new
    if not isinstance(v_rows, (list, tuple)):
        v_rows = [v_rows]
    keys = p.shape[0] // len(v_rows)
    acc = alpha * acc_ref[hh, :, cols]
    for i, v_i in enumerate(v_rows):
        acc = acc + jnp.dot(v_i, p[i * keys:(i + 1) * keys], preferred_element_type=F32)
    acc_ref[hh, :, cols] = acc


def _write_heads(o_ref, acc_ref):
    outs = []
    for hh in range(2):
        acc = acc_ref[hh]
        outs.append(acc[:HEAD_DIM] / acc[HEAD_DIM:HEAD_DIM + 1])
    o_ref[...] = jnp.concatenate(outs, axis=0).astype(o_ref.dtype)


def _head_row_mask(hh):
    row = lax.broadcasted_iota(jnp.int32, (LANES, 1), 0)
    return (row < HEAD_DIM) if hh == 0 else (row >= HEAD_DIM)


def _fox_kernel(qt_ref, k_ref, v_ref, aqt_ref, ak_ref, o_ref,
                vt_ref, s_ref, mx_ref, diag_ref, m_ref, acc_ref):
    hp = pl.program_id(1)
    qi = pl.program_id(2)
    nq, _, blk = qt_ref.shape
    nblk = k_ref.shape[1] // blk
    gate_row = lax.broadcasted_iota(jnp.int32, (LANES, 1), 0)

    def query_operands(tile):
        qt, aqt = qt_ref[tile], aqt_ref[tile]
        out = []
        for hh in range(2):
            first = GATE_LANES * (2 * hp + hh)
            own_gate = (gate_row >= first) & (gate_row < first + GATE_LANES)
            q_rows = jnp.where(_head_row_mask(hh), qt, jnp.zeros_like(qt))
            g_rows = jnp.where(own_gate, aqt, jnp.zeros_like(aqt))
            out.append(jnp.concatenate([q_rows, g_rows], axis=0))
        return out

    def block_scores(w, j, hh):
        rows = pl.ds(pl.multiple_of(j * blk, blk), blk)
        keys = jnp.concatenate([k_ref[0, rows, :], ak_ref[0, rows, :]], axis=1)
        return jnp.dot(keys, w[hh], preferred_element_type=F32)

    w_q = query_operands(qi)

    @pl.when(qi == 0)
    def _():
        _fill_value_rows(vt_ref, v_ref, blk)

    def scores_into(j, slot, hh):
        _store_scores(block_scores(w_q, j, hh), s_ref, mx_ref, (slot, hh))

    def next_tile_scores():
        w_next = query_operands(jnp.minimum(qi + 1, nq - 1))
        for hh in range(2):
            _store_scores(block_scores(w_next, 0, hh), s_ref, mx_ref, (0, hh))

    def update(j, slot, hh):
        _flash_update(s_ref[slot, hh], mx_ref[slot, hh], vt_ref[hh, j], m_ref, acc_ref, hh)

    def diagonal_scores(hh):
        diag_ref[hh] = block_scores(w_q, qi, hh)

    def diagonal_update():
        half = blk // 2
        lo, hi = slice(0, half), slice(half, blk)
        causal_lo = (lax.broadcasted_iota(jnp.int32, (half, half), 0)
                     <= lax.broadcasted_iota(jnp.int32, (half, half), 1))
        causal_hi = (lax.broadcasted_iota(jnp.int32, (blk, half), 0)
                     <= lax.broadcasted_iota(jnp.int32, (blk, half), 1) + half)
        for hh in range(2):
            v_rows = vt_ref[hh, qi]
            s_lo = jnp.where(causal_lo, diag_ref[hh, lo, lo], NEG_INF)
            _flash_update(s_lo, jnp.max(s_lo, axis=0, keepdims=True),
                          v_rows[:, lo], m_ref, acc_ref, hh, cols=lo)
            s_hi = jnp.where(causal_hi, diag_ref[hh, :, hi], NEG_INF)
            _flash_update(s_hi, jnp.max(s_hi, axis=0, keepdims=True),
                          v_rows, m_ref, acc_ref, hh, cols=hi)

    _flash_init(m_ref, acc_ref)
    _pipelined_blocks(qi, scores_into, update, diagonal_scores, diagonal_update,
                      next_tile_scores)
    _write_heads(o_ref, acc_ref)


def _fox_attention(qt3, k3, vt, aqt3, ak3, n_heads):
    b, s, d = k3.shape
    assert n_heads * GATE_LANES <= LANES
    blk = qt3.shape[2]
    assert aqt3.shape[2] == blk and s % blk == 0
    npair = n_heads // 2
    nq = s // blk
    return pl.pallas_call(
        _fox_kernel, grid=(b, npair, nq),
        in_specs=[
            pl.BlockSpec((nq, LANES, blk), lambda bi, hp, qi: (bi, hp, 0)),
            pl.BlockSpec((1, s, LANES), lambda bi, hp, qi: (bi, 0, hp)),
            pl.BlockSpec((LANES, s), lambda bi, hp, qi: (hp, bi)),
            pl.BlockSpec((nq, LANES, blk), lambda bi, hp, qi: (bi, 0, 0)),
            pl.BlockSpec((1, s, LANES), lambda bi, hp, qi: (bi, 0, 0)),
        ],
        out_specs=pl.BlockSpec((LANES, blk), lambda bi, hp, qi: (hp, bi * nq + qi)),
        out_shape=jax.ShapeDtypeStruct((d, b * s), BF16),
        scratch_shapes=[
            pltpu.VMEM((2, s // blk, V_ROWS, blk), BF16),
            pltpu.VMEM((2, 2, blk, blk), F32),
            pltpu.VMEM((2, 2, 1, blk), F32),
            pltpu.VMEM((2, blk, blk), F32),
            pltpu.VMEM((2, 1, blk), F32),
            pltpu.VMEM((2, V_ROWS, blk), F32),
        ],
        compiler_params=_params(3), name="fox_attention")(qt3, k3, vt, aqt3, ak3)


def _t5_bucket_np(n):
    max_exact = REL_BUCKETS // 2
    nf = np.maximum(n, 1).astype(np.float64)
    large = max_exact + (np.log(nf / max_exact) / math.log(REL_MAX_DIST / max_exact)
                         * (REL_BUCKETS - max_exact)).astype(np.int32)
    return np.where(n < max_exact, n, np.minimum(large, REL_BUCKETS - 1)).astype(np.int32)


def _bucket_tiles(blk):
    key = np.arange(blk)[:, None]
    qry = np.arange(blk)[None, :]
    own = np.where(key <= qry, _t5_bucket_np(np.maximum(qry - key, 0)), -1)
    prev = _t5_bucket_np(blk + qry - key)
    return np.stack([own, prev]).astype(np.int32)


def _bias_kernel(tab_ref, bucket_ref, o_ref):
    h = pl.program_id(0)
    bucket = bucket_ref[...]
    acc = jnp.where(bucket < 0, NEG_INF, 0.0).astype(F32)
    for bkt in range(REL_BUCKETS):
        acc = jnp.where(bucket == bkt, tab_ref[bkt, h] * LOG2E, acc)
    o_ref[0] = acc


def _bias_tiles(rel_table, blk):
    n_heads = rel_table.shape[1]
    buckets = jnp.asarray(_bucket_tiles(blk))
    return pl.pallas_call(
        _bias_kernel, grid=(n_heads,),
        in_specs=[pl.BlockSpec(memory_space=pltpu.SMEM),
                  pl.BlockSpec((2, blk, blk), lambda h: (0, 0, 0))],
        out_specs=pl.BlockSpec((1, 2, blk, blk), lambda h: (h, 0, 0, 0)),
        out_shape=jax.ShapeDtypeStruct((n_heads, 2, blk, blk), F32),
        compiler_params=_params(1), name="t5_bias_tiles")(rel_table, buckets)


def _moba_kernel(tab_ref, qt_ref, k_ref, v_ref, bias_ref, o_ref,
                 vt_ref, km_ref, sel_ref, s_ref, mx_ref, near_ref, own1_ref, m_ref, acc_ref):
    hp = pl.program_id(1)
    qi = pl.program_id(2)
    blk = MOBA_BLOCK
    nq, _, tq = qt_ref.shape
    nblk = k_ref.shape[1] // blk
    assert tq == 2 * blk and blk >= REL_MAX_DIST
    first_own = 2 * qi

    @pl.when(qi == 0)
    def _():
        def build(jb, carry):
            rows = pl.ds(pl.multiple_of(jb * blk, blk), blk)
            km_ref[pl.ds(jb, 1), :] = jnp.mean(k_ref[0, rows, :].astype(F32),
                                                axis=0, keepdims=True)
            return carry
        lax.fori_loop(0, nblk, build, 0)
        _fill_value_rows(vt_ref, v_ref, blk)

    def head_queries(tile):
        qt = qt_ref[tile]
        return [jnp.where(_head_row_mask(hh), qt, jnp.zeros_like(qt)) for hh in range(2)]

    q_m_t = head_queries(qi)
    km_parts = _split3(km_ref[...])
    blk_id = lax.broadcasted_iota(jnp.int32, (nblk, tq), 0)
    qpos = lax.broadcasted_iota(jnp.int32, (nblk, tq), 1)
    own = first_own + jnp.where(qpos >= blk, 1, 0)
    past = blk_id < own
    for hh in range(2):
        gate = None
        for part in km_parts:
            term = jnp.dot(part, q_m_t[hh], preferred_element_type=F32)
            gate = term if gate is None else gate + term
        work = jnp.where(past, gate, NEG_INF)
        picked = jnp.zeros(gate.shape, F32)
        for _ in range(MOBA_TOP_K):
            best = jnp.max(work, axis=0, keepdims=True)
            first = jnp.min(jnp.where(work == best, blk_id, nblk), axis=0, keepdims=True)
            hit = blk_id == first
            picked = jnp.where(hit, 1.0, picked)
            work = jnp.where(hit, NEG_INF, work)
        keep_all = ((picked > 0.5) & past) | (blk_id == own) | (blk_id == first_own + 1)
        sel_ref[hh] = jnp.where(keep_all, 1.0, 0.0)

    def scores(hh, j, queries=q_m_t):
        rows = pl.ds(pl.multiple_of(j * blk, blk), blk)
        return jnp.dot(k_ref[0, rows, :], queries[hh], preferred_element_type=F32)

    def next_tile_scores():
        q_next = head_queries(jnp.minimum(qi + 1, nq - 1))
        for hh in range(2):
            _store_scores(scores(hh, 0, q_next), s_ref, mx_ref, (0, hh))

    def keep(hh, j):
        return sel_ref[hh, pl.ds(j, 1), :] > 0.5

    _flash_init(m_ref, acc_ref)
    far_bias = [tab_ref[REL_BUCKETS - 1, 2 * hp + hh] * LOG2E for hh in range(2)]

    j_prev = jnp.maximum(first_own - 1, 0)
    lo, hi = slice(0, blk), slice(blk, 2 * blk)
    near_mx = {}

    def near_scores(hh):
        own_t, prev_t = bias_ref[hh, 0], bias_ref[hh, 1]
        mask_prev = jnp.where(keep(hh, j_prev) & (qi >= 1), 0.0, NEG_INF)
        mask_own = jnp.where(keep(hh, first_own), 0.0, NEG_INF)
        far_t = jnp.full((blk, blk), far_bias[hh], F32)
        parts = [scores(hh, j_prev) + mask_prev + jnp.concatenate([prev_t, far_t], axis=1),
                 scores(hh, first_own) + mask_own + jnp.concatenate([own_t, prev_t], axis=1)]
        mx = None
        for i, part in enumerate(parts):
            near_ref[hh, i * blk:(i + 1) * blk] = part
            part_mx = jnp.max(part, axis=0, keepdims=True)
            mx = part_mx if mx is None else jnp.maximum(mx, part_mx)
        rows = pl.ds(pl.multiple_of((first_own + 1) * blk, blk), blk)
        last = jnp.dot(k_ref[0, rows, :], q_m_t[hh][:, hi], preferred_element_type=F32) + own_t
        own1_ref[hh] = last
        near_mx[hh] = (mx[:, lo], jnp.maximum(mx[:, hi], jnp.max(last, axis=0, keepdims=True)))

    def near_update():
        for hh in range(2):
            v_near = [vt_ref[hh, j_prev], vt_ref[hh, first_own], vt_ref[hh, first_own + 1]]
            _flash_update(near_ref[hh, :, lo], near_mx[hh][0], v_near[:2],
                          m_ref, acc_ref, hh, cols=lo)
            _flash_update(jnp.concatenate([near_ref[hh, :, hi], own1_ref[hh]], axis=0),
                          near_mx[hh][1], v_near, m_ref, acc_ref, hh, cols=hi)

    def scores_into(j, slot, hh):
        _store_scores(scores(hh, j), s_ref, mx_ref, (slot, hh))

    def update(j, slot, hh):
        _flash_update(s_ref[slot, hh], mx_ref[slot, hh], vt_ref[hh, j],
                      m_ref, acc_ref, hh, keep=keep(hh, j), const=far_bias[hh])

    _pipelined_blocks(jnp.maximum(first_own - 1, 0), scores_into, update,
                      near_scores, near_update, next_tile_scores)
    _write_heads(o_ref, acc_ref)


def _moba_attention(qt3, k3, vt, rel_table, bias_t, n_heads):
    b, s, d = k3.shape
    blk = MOBA_BLOCK
    tq = qt3.shape[2]
    assert tq == MOBA_Q_TILE and s % tq == 0
    npair = n_heads // 2
    nblk = s // blk
    nq = s // tq
    return pl.pallas_call(
        _moba_kernel, grid=(b, npair, nq),
        in_specs=[
            pl.BlockSpec(memory_space=pltpu.SMEM),
            pl.BlockSpec((nq, LANES, tq), lambda bi, hp, qi: (bi, hp, 0)),
            pl.BlockSpec((1, s, LANES), lambda bi, hp, qi: (bi, 0, hp)),
            pl.BlockSpec((LANES, s), lambda bi, hp, qi: (hp, bi)),
            pl.BlockSpec((2, 2, blk, blk), lambda bi, hp, qi: (hp, 0, 0, 0)),
        ],
        out_specs=pl.BlockSpec((LANES, tq), lambda bi, hp, qi: (hp, bi * nq + qi)),
        out_shape=jax.ShapeDtypeStruct((d, b * s), BF16),
        scratch_shapes=[
            pltpu.VMEM((2, nblk, V_ROWS, blk), BF16),
            pltpu.VMEM((nblk, LANES), F32),
            pltpu.VMEM((2, nblk, tq), F32),
            pltpu.VMEM((2, 2, blk, tq), F32),
            pltpu.VMEM((2, 2, 1, tq), F32),
            pltpu.VMEM((2, 2 * blk, tq), F32),
            pltpu.VMEM((2, blk, blk), F32),
            pltpu.VMEM((2, 1, tq), F32),
            pltpu.VMEM((2, V_ROWS, tq), F32),
        ],
        compiler_params=_params(3), name="moba_attention")(
            rel_table, qt3, k3, vt, bias_t)


def _oproj_ffn_kernel(h_ref, ot_ref, wo_ref, g_ref, win_ref, wout_ref, out_ref, *, tf):
    d_ff = wout_ref.shape[0]
    h1 = h_ref[...] + lax.dot_general(ot_ref[...], wo_ref[...], (((0,), (0,)), ((), ())),
                                      preferred_element_type=F32)
    u = _rmsnorm(h1, g_ref[...]).astype(BF16)
    acc = h1
    for c in range(d_ff // tf):
        gate = jnp.dot(u, win_ref[:, c * tf:(c + 1) * tf], preferred_element_type=F32)
        up = jnp.dot(u, win_ref[:, d_ff + c * tf:d_ff + (c + 1) * tf],
                     preferred_element_type=F32)
        act = (gate * jax.nn.sigmoid(gate) * up).astype(BF16)
        acc = acc + jnp.dot(act, wout_ref[c * tf:(c + 1) * tf, :],
                            preferred_element_type=F32)
    out_ref[...] = acc


def _oproj_ffn(h2, o_t, w_o, g, w_in, w_out, *, tm, tf):
    n, d = h2.shape
    d_ff = w_out.shape[0]
    assert d_ff % tf == 0
    row = pl.BlockSpec((tm, d), lambda i: (i, 0))
    return pl.pallas_call(
        functools.partial(_oproj_ffn_kernel, tf=tf), grid=(n // tm,),
        in_specs=[row, pl.BlockSpec((d, tm), lambda i: (0, i)), _resident((d, d)),
                  _resident((1, d)), _resident((d, 2 * d_ff)), _resident((d_ff, d))],
        out_specs=row,
        out_shape=jax.ShapeDtypeStruct((n, d), F32),
        compiler_params=_params(1), name="oproj_ffn")(h2, o_t, w_o, g, w_in, w_out)


def _ple_update(x_ref, g_ref, wg_ref, p_ref, wu_ref):
    x = x_ref[...]
    u = _rmsnorm(x, g_ref[...]).astype(BF16)
    gate = jax.nn.sigmoid(jnp.dot(u, wg_ref[...], preferred_element_type=F32))
    up = jnp.dot(p_ref[...].astype(BF16), wu_ref[...], preferred_element_type=F32)
    return x + gate * up


def _ple_next_kernel(x_ref, g_ref, wg_ref, p_ref, wu_ref, *refs, n_mixer):
    y = _ple_update(x_ref, g_ref, wg_ref, p_ref, wu_ref)
    refs[n_mixer][...] = y
    _mixer_inputs(y, refs[:n_mixer], refs[n_mixer + 1:])


def _ple_final_kernel(x_ref, g_ref, wg_ref, p_ref, wu_ref, fg_ref, out_ref):
    y = _ple_update(x_ref, g_ref, wg_ref, p_ref, wu_ref)
    out_ref[...] = _rmsnorm(y, fg_ref[...])


def _ple(h2, g, w_gate, p3, layer, w_up, *, tm, mixer=None, final_g=None):
    n, d = h2.shape
    pd = p3.shape[2]
    row = pl.BlockSpec((tm, d), lambda i: (i, 0))
    in_specs = [row, _resident((1, d)), _resident((d, d)),
                pl.BlockSpec((None, tm, pd), lambda i: (layer, i, 0)), _resident((pd, d))]
    h_shape = jax.ShapeDtypeStruct((n, d), F32)
    if mixer is None:
        return pl.pallas_call(
            _ple_final_kernel, grid=(n // tm,),
            in_specs=in_specs + [_resident((1, d))], out_specs=row, out_shape=h_shape,
            compiler_params=_params(1), name="ple_final")(h2, g, w_gate, p3, w_up, final_g)
    n_mixer = len(mixer.operands())
    return pl.pallas_call(
        functools.partial(_ple_next_kernel, n_mixer=n_mixer), grid=(n // tm,),
        in_specs=in_specs + mixer.in_specs(),
        out_specs=[row] + mixer.out_specs(tm),
        out_shape=[h_shape] + mixer.out_shapes(n, tm),
        compiler_params=_params(1), name="ple_proj")(h2, g, w_gate, p3, w_up,
                                                     *mixer.operands())


def _row_tile(n, want):
    t = min(want, n)
    assert n % t == 0
    return t


def _col_tile(n, want):
    t = min(want, n)
    while n % t:
        t -= LANES
    return t


def kernel(x, p, attn_norm_g, fox_w_in, fox_b_f, fox_w_o, moba_w_in, moba_w_o, rel_bias_table,
           ffn_norm_g, ffn_w_in, ffn_w_out, ple_norm_g, ple_w_gate, ple_w_up, final_norm_g):
    b, s, d = x.shape
    depth = p.shape[0]
    n_heads = rel_bias_table.shape[1]
    assert d == n_heads * HEAD_DIM and n_heads % 2 == 0 and n_heads <= LANES
    n = b * s
    tm = _row_tile(n, DENSE_ROW_TILE)
    tf = _col_tile(ffn_w_out.shape[1], FFN_COL_CHUNK)

    def row_vec(v):
        return v.reshape(1, -1).astype(F32)

    def mixer(i):
        w = fox_w_in[i // 2] if i % 2 == 0 else moba_w_in[i // 2]
        parts = [row_vec(attn_norm_g[i]), (w[:, :d] * Q_SCALE).T.astype(BF16),
                 w[:, d:2 * d].astype(BF16), w[:, 2 * d:3 * d].T.astype(BF16)]
        if i % 2 == 0:
            parts += [jnp.pad(w[:, 3 * d:], ((0, 0), (0, LANES - n_heads))).astype(BF16),
                      jnp.pad(row_vec(fox_b_f[i // 2]), ((0, 0), (0, LANES - n_heads)))]
        return _Mixer(*parts)

    rel_table = rel_bias_table.astype(F32)
    bias_t = _bias_tiles(rel_table, MOBA_BLOCK)

    h = x.reshape(n, d).astype(F32)
    mixed = _project(h, mixer(0), tm=tm)
    for i in range(depth):
        qt, k3, vt = mixed[0], mixed[1].reshape(b, s, d), mixed[2]
        if i % 2 == 0:
            aqt, ak = _cumsum(mixed[3].reshape(b, s, LANES))
            o_t = _fox_attention(qt, k3, vt, aqt, ak, n_heads)
            w_o = fox_w_o[i // 2]
        else:
            o_t = _moba_attention(qt, k3, vt, rel_table, bias_t, n_heads)
            w_o = moba_w_o[i // 2]
        h = _oproj_ffn(h, o_t, w_o.astype(BF16), row_vec(ffn_norm_g[i]),
                       ffn_w_in[i].astype(BF16), ffn_w_out[i].astype(BF16), tm=tm, tf=tf)
        ple_args = (h, row_vec(ple_norm_g[i]), ple_w_gate[i].astype(BF16),
                    p.reshape(depth, n, -1), i, ple_w_up[i].astype(BF16))
        if i + 1 < depth:
            h, *mixed = _ple(*ple_args, tm=tm, mixer=mixer(i + 1))
        else:
            h = _ple(*ple_args, tm=tm, final_g=row_vec(final_norm_g))
    return h.reshape(b, s, d).astype(x.dtype)
```

```python
import functools
import math
from typing import NamedTuple, Optional

import numpy as np
import jax
import jax.numpy as jnp
from jax import lax
from jax.experimental import pallas as pl
from jax.experimental.pallas import tpu as pltpu

F32 = jnp.float32
BF16 = jnp.bfloat16

RMS_EPS = 1e-6
HEAD_DIM = 64
MOBA_BLOCK = 256
MOBA_TOP_K = 3
REL_BUCKETS = 32
REL_MAX_DIST = 128

LANES = 128
BF16_SUBLANES = 16
V_ROWS = HEAD_DIM + BF16_SUBLANES
GATE_LANES = 8
SCORE_PITCH_PAD = LANES
LOG2E = math.log2(math.e)
Q_SCALE = HEAD_DIM ** -0.5 * LOG2E
MOBA_Q_TILE = 2 * MOBA_BLOCK
ATTN_Q_TILE = MOBA_Q_TILE
DENSE_ROW_TILE = ATTN_Q_TILE
CUMSUM_BLK = ATTN_Q_TILE
PROJ_COL_CHUNK = 512
FFN_COL_CHUNK = 256
VMEM_LIMIT_BYTES = 56 * 1024 * 1024
NEG_INF = float("-inf")


def _params(n_axes):
    return pltpu.CompilerParams(
        dimension_semantics=("arbitrary",) * n_axes,
        vmem_limit_bytes=VMEM_LIMIT_BYTES)


def _split3(x):
    x1 = x.astype(BF16)
    r1 = x - x1.astype(F32)
    x2 = r1.astype(BF16)
    x3 = (r1 - x2.astype(F32)).astype(BF16)
    return x1, x2, x3


def _rmsnorm(x, g):
    ms = jnp.mean(x * x, axis=-1, keepdims=True)
    return x * lax.rsqrt(ms + RMS_EPS) * g


def _log_sigmoid(x):
    return jnp.minimum(x, 0.0) - jnp.log1p(jnp.exp(-jnp.abs(x)))


def _resident(shape):
    return pl.BlockSpec(shape, lambda i: (0,) * len(shape), pipeline_mode=pl.Buffered(1))


class _Mixer(NamedTuple):
    g: jax.Array
    w_qt: jax.Array
    w_k: jax.Array
    w_vt: jax.Array
    w_f: Optional[jax.Array] = None
    b_f: Optional[jax.Array] = None

    def operands(self):
        return tuple(a for a in self if a is not None)

    def in_specs(self):
        return [_resident(a.shape) for a in self.operands()]

    def out_specs(self, tm):
        d = self.w_qt.shape[0]
        specs = [pl.BlockSpec((None, d, tm), lambda i: (i, 0, 0)),
                 pl.BlockSpec((tm, d), lambda i: (i, 0)),
                 pl.BlockSpec((d, tm), lambda i: (0, i))]
        if self.w_f is not None:
            specs.append(pl.BlockSpec((tm, LANES), lambda i: (i, 0)))
        return specs

    def out_shapes(self, n, tm):
        d = self.w_qt.shape[0]
        shapes = [jax.ShapeDtypeStruct((n // tm, d, tm), BF16),
                  jax.ShapeDtypeStruct((n, d), BF16), jax.ShapeDtypeStruct((d, n), BF16)]
        if self.w_f is not None:
            shapes.append(jax.ShapeDtypeStruct((n, LANES), F32))
        return shapes


def _chunks(total, want):
    step = want if total % want == 0 else total
    return [slice(c * step, (c + 1) * step) for c in range(total // step)]


def _mixer_inputs(y, mixer_refs, out_refs):
    g_ref, wqt_ref, wk_ref, wvt_ref = mixer_refs[:4]
    u = _rmsnorm(y, g_ref[...]).astype(BF16)
    for wt_ref, out_ref in ((wqt_ref, out_refs[0]), (wvt_ref, out_refs[2])):
        for rows in _chunks(wt_ref.shape[0], PROJ_COL_CHUNK):
            out_ref[rows, :] = lax.dot_general(
                wt_ref[rows, :], u, (((1,), (1,)), ((), ())),
                preferred_element_type=F32).astype(BF16)
    for cols in _chunks(wk_ref.shape[1], PROJ_COL_CHUNK):
        out_refs[1][:, cols] = jnp.dot(u, wk_ref[:, cols],
                                       preferred_element_type=F32).astype(BF16)
    if len(mixer_refs) > 4:
        wf_ref, bf_ref = mixer_refs[4:]
        f_logit = jnp.dot(u, wf_ref[...], preferred_element_type=F32) + bf_ref[...]
        out_refs[3][...] = _log_sigmoid(f_logit)


def _proj_kernel(x_ref, *refs, n_mixer):
    _mixer_inputs(x_ref[...], refs[:n_mixer], refs[n_mixer:])


def _project(h2, mixer, *, tm):
    n, d = h2.shape
    n_mixer = len(mixer.operands())
    return pl.pallas_call(
        functools.partial(_proj_kernel, n_mixer=n_mixer), grid=(n // tm,),
        in_specs=[pl.BlockSpec((tm, d), lambda i: (i, 0))] + mixer.in_specs(),
        out_specs=mixer.out_specs(tm), out_shape=mixer.out_shapes(n, tm),
        compiler_params=_params(1), name="proj")(h2, *mixer.operands())


def _cumsum_kernel(lf_ref, aqt_ref, ak_ref, carry_ref):
    @pl.when(pl.program_id(1) == 0)
    def _():
        carry_ref[...] = jnp.zeros_like(carry_ref)

    t = lf_ref.shape[1]
    row = lax.broadcasted_iota(jnp.int32, (t, t), 0)
    col = lax.broadcasted_iota(jnp.int32, (t, t), 1)
    tril = jnp.where(col <= row, 1.0, 0.0).astype(BF16)
    x1, x2, x3 = _split3(lf_ref[0])
    cs = (jnp.dot(tril, x1, preferred_element_type=F32)
          + jnp.dot(tril, x2, preferred_element_type=F32)
          + jnp.dot(tril, x3, preferred_element_type=F32))
    cs = cs + carry_ref[0:1, :]
    carry_ref[...] = jnp.broadcast_to(cs[t - 1:t, :], carry_ref.shape)

    src = lax.broadcasted_iota(jnp.int32, (LANES, LANES), 0)
    dst = lax.broadcasted_iota(jnp.int32, (LANES, LANES), 1)
    lane = lax.broadcasted_iota(jnp.int32, (1, LANES), 1) & (GATE_LANES - 1)
    aq = jnp.where((lane >= 3) & (lane < 6), 1.0, 0.0)
    ak = jnp.where(lane < 3, 1.0, 0.0)
    for i, part in enumerate(_split3(cs * LOG2E)):
        to_q = jnp.where(dst == GATE_LANES * src + i, 1.0, 0.0).astype(BF16)
        to_k = jnp.where(dst == GATE_LANES * src + 3 + i, 1.0, 0.0).astype(BF16)
        aq = aq + jnp.dot(part, to_q, preferred_element_type=F32)
        ak = ak - jnp.dot(part, to_k, preferred_element_type=F32)
    eye = jnp.where(src == dst, 1.0, 0.0).astype(BF16)
    aqt_ref[0] = lax.dot_general(eye, aq.astype(BF16), (((1,), (1,)), ((), ())),
                                 preferred_element_type=F32).astype(BF16)
    ak_ref[0] = ak.astype(BF16)


def _cumsum(lf3):
    b, s, _ = lf3.shape
    t = min(CUMSUM_BLK, s)
    nt = s // t
    spec = pl.BlockSpec((1, t, LANES), lambda i, j: (i, j, 0))
    return pl.pallas_call(
        _cumsum_kernel, grid=(b, nt),
        in_specs=[spec],
        out_specs=[pl.BlockSpec((1, LANES, t), lambda i, j: (i * nt + j, 0, 0)), spec],
        out_shape=[jax.ShapeDtypeStruct((b * nt, LANES, t), BF16),
                   jax.ShapeDtypeStruct(lf3.shape, BF16)],
        scratch_shapes=[pltpu.VMEM((8, LANES), F32)],
        compiler_params=_params(2), name="gate_cumsum")(lf3)


def _fill_value_rows(vt_ref, v_ref, blk):
    row = lax.broadcasted_iota(jnp.int32, (V_ROWS - HEAD_DIM, blk), 0)
    tail = jnp.where(row == 0, 1.0, 0.0).astype(BF16)
    for hh in range(2):
        for jb in range(v_ref.shape[1] // blk):
            head_rows = v_ref[hh * HEAD_DIM:(hh + 1) * HEAD_DIM, jb * blk:(jb + 1) * blk]
            vt_ref[hh, jb] = jnp.concatenate([head_rows, tail], axis=0)


def _flash_init(m_ref, acc_ref):
    m_ref[...] = jnp.full(m_ref.shape, NEG_INF, F32)
    acc_ref[...] = jnp.zeros(acc_ref.shape, F32)


def _pipelined_blocks(n, scores_into, update, final_scores, final_update, next_tile_scores):
    def step(j_next, j, slot):
        for hh in range(2):
            scores_into(j_next, 1 - slot, hh)
            update(j, slot, hh)

    def last_update_and_final(j, slot):
        for hh in range(2):
            final_scores(hh)
            update(j, slot, hh)
        next_tile_scores()
        final_update()

    def pair(jj, carry):
        j = 2 * jj
        step(j + 1, j, 0)
        step(j + 2, j + 1, 1)
        return carry
    lax.fori_loop(0, jnp.maximum(n - 1, 0) // 2, pair, 0)

    @pl.when(n % 2 == 1)
    def _():
        last_update_and_final(n - 1, 0)

    @pl.when((n % 2 == 0) & (n > 0))
    def _():
        step(n - 1, n - 2, 0)
        last_update_and_final(n - 1, 1)

    @pl.when(n == 0)
    def _():
        for hh in range(2):
            final_scores(hh)
        next_tile_scores()
        final_update()


def _store_scores(s_t, s_ref, mx_ref, idx):
    s_ref[idx + (slice(None), slice(0, s_t.shape[1]))] = s_t
    mx_ref[idx] = jnp.max(s_t, axis=0, keepdims=True)


def _flash_update(s_t, mx, v_rows, m_ref, acc_ref, hh, keep=None, const=None,
                  cols=slice(None)):
    m_old = m_ref[hh, :, cols]
    if const is not None:
        mx = mx + const
    if keep is not None:
        mx = jnp.where(keep, mx, NEG_INF)
    m_new = jnp.maximum(m_old, mx)
    m_safe = jnp.where(m_new == NEG_INF, 0.0, m_new)
    shift = m_safe if const is None else m_safe - const
    if keep is not None:
        shift = jnp.where(keep, shift, float("inf"))
    p = jnp.exp2(s_t - shift).astype(BF16)
    alpha = jnp.exp2(m_old - m_safe)
    m_ref[hh, :, cols] = m_new
    if not isinstance(v_rows, (list, tuple)):
        v_rows = [v_rows]
    keys = p.shape[0] // len(v_rows)
    acc = alpha * acc_ref[hh, :, cols]
    for i, v_i in enumerate(v_rows):
        acc = acc + jnp.dot(v_i, p[i * keys:(i + 1) * keys], preferred_element_type=F32)
    acc_ref[hh, :, cols] = acc


def _write_heads(o_ref, acc_ref):
    outs = []
    for hh in range(2):
        acc = acc_ref[hh]
        outs.append(acc[:HEAD_DIM] / acc[HEAD_DIM:HEAD_DIM + 1])
    o_ref[...] = jnp.concatenate(outs, axis=0).astype(o_ref.dtype)


def _head_row_mask(hh):
    row = lax.broadcasted_iota(jnp.int32, (LANES, 1), 0)
    return (row < HEAD_DIM) if hh == 0 else (row >= HEAD_DIM)


def _fox_kernel(qt_ref, k_ref, v_ref, aqt_ref, ak_ref, o_ref,
                vt_ref, s_ref, mx_ref, diag_ref, m_ref, acc_ref):
    hp = pl.program_id(1)
    qi = pl.program_id(2)
    nq, _, blk = qt_ref.shape
    nblk = k_ref.shape[1] // blk
    gate_row = lax.broadcasted_iota(jnp.int32, (LANES, 1), 0)

    def query_operands(tile):
        qt, aqt = qt_ref[tile], aqt_ref[tile]
        out = []
        for hh in range(2):
            first = GATE_LANES * (2 * hp + hh)
            own_gate = (gate_row >= first) & (gate_row < first + GATE_LANES)
            q_rows = jnp.where(_head_row_mask(hh), qt, jnp.zeros_like(qt))
            g_rows = jnp.where(own_gate, aqt, jnp.zeros_like(aqt))
            out.append(jnp.concatenate([q_rows, g_rows], axis=0))
        return out

    def block_scores(w, j, hh):
        rows = pl.ds(pl.multiple_of(j * blk, blk), blk)
        keys = jnp.concatenate([k_ref[0, rows, :], ak_ref[0, rows, :]], axis=1)
        return jnp.dot(keys, w[hh], preferred_element_type=F32)

    w_q = query_operands(qi)

    @pl.when(qi == 0)
    def _():
        _fill_value_rows(vt_ref, v_ref, blk)

    def scores_into(j, slot, hh):
        _store_scores(block_scores(w_q, j, hh), s_ref, mx_ref, (slot, hh))

    def next_tile_scores():
        w_next = query_operands(jnp.minimum(qi + 1, nq - 1))
        for hh in range(2):
            _store_scores(block_scores(w_next, 0, hh), s_ref, mx_ref, (0, hh))

    def update(j, slot, hh):
        _flash_update(s_ref[slot, hh, :, :blk], mx_ref[slot, hh], vt_ref[hh, j],
                      m_ref, acc_ref, hh)

    def diagonal_scores(hh):
        diag_ref[hh] = block_scores(w_q, qi, hh)

    def diagonal_update():
        half = blk // 2
        lo, hi = slice(0, half), slice(half, blk)
        causal_lo = (lax.broadcasted_iota(jnp.int32, (half, half), 0)
                     <= lax.broadcasted_iota(jnp.int32, (half, half), 1))
        causal_hi = (lax.broadcasted_iota(jnp.int32, (blk, half), 0)
                     <= lax.broadcasted_iota(jnp.int32, (blk, half), 1) + half)
        for hh in range(2):
            v_rows = vt_ref[hh, qi]
            s_lo = jnp.where(causal_lo, diag_ref[hh, lo, lo], NEG_INF)
            _flash_update(s_lo, jnp.max(s_lo, axis=0, keepdims=True),
                          v_rows[:, lo], m_ref, acc_ref, hh, cols=lo)
            s_hi = jnp.where(causal_hi, diag_ref[hh, :, hi], NEG_INF)
            _flash_update(s_hi, jnp.max(s_hi, axis=0, keepdims=True),
                          v_rows, m_ref, acc_ref, hh, cols=hi)

    _flash_init(m_ref, acc_ref)
    _pipelined_blocks(qi, scores_into, update, diagonal_scores, diagonal_update,
                      next_tile_scores)
    _write_heads(o_ref, acc_ref)


def _fox_attention(qt3, k3, vt, aqt3, ak3, n_heads):
    b, s, d = k3.shape
    assert n_heads * GATE_LANES <= LANES
    blk = qt3.shape[2]
    assert aqt3.shape[2] == blk and s % blk == 0
    npair = n_heads // 2
    nq = s // blk
    return pl.pallas_call(
        _fox_kernel, grid=(b, npair, nq),
        in_specs=[
            pl.BlockSpec((nq, LANES, blk), lambda bi, hp, qi: (bi, hp, 0)),
            pl.BlockSpec((1, s, LANES), lambda bi, hp, qi: (bi, 0, hp)),
            pl.BlockSpec((LANES, s), lambda bi, hp, qi: (hp, bi)),
            pl.BlockSpec((nq, LANES, blk), lambda bi, hp, qi: (bi, 0, 0)),
            pl.BlockSpec((1, s, LANES), lambda bi, hp, qi: (bi, 0, 0)),
        ],
        out_specs=pl.BlockSpec((LANES, blk), lambda bi, hp, qi: (hp, bi * nq + qi)),
        out_shape=jax.ShapeDtypeStruct((d, b * s), BF16),
        scratch_shapes=[
            pltpu.VMEM((2, s // blk, V_ROWS, blk), BF16),
            pltpu.VMEM((2, 2, blk, blk + SCORE_PITCH_PAD), F32),
            pltpu.VMEM((2, 2, 1, blk), F32),
            pltpu.VMEM((2, blk, blk), F32),
            pltpu.VMEM((2, 1, blk), F32),
            pltpu.VMEM((2, V_ROWS, blk), F32),
        ],
        compiler_params=_params(3), name="fox_attention")(qt3, k3, vt, aqt3, ak3)


def _t5_bucket_np(n):
    max_exact = REL_BUCKETS // 2
    nf = np.maximum(n, 1).astype(np.float64)
    large = max_exact + (np.log(nf / max_exact) / math.log(REL_MAX_DIST / max_exact)
                         * (REL_BUCKETS - max_exact)).astype(np.int32)
    return np.where(n < max_exact, n, np.minimum(large, REL_BUCKETS - 1)).astype(np.int32)


def _bucket_tiles(blk):
    key = np.arange(blk)[:, None]
    qry = np.arange(blk)[None, :]
    own = np.where(key <= qry, _t5_bucket_np(np.maximum(qry - key, 0)), -1)
    prev = _t5_bucket_np(blk + qry - key)
    return np.stack([own, prev]).astype(np.int32)


def _bias_kernel(tab_ref, bucket_ref, o_ref):
    h = pl.program_id(0)
    bucket = bucket_ref[...]
    acc = jnp.where(bucket < 0, NEG_INF, 0.0).astype(F32)
    for bkt in range(REL_BUCKETS):
        acc = jnp.where(bucket == bkt, tab_ref[bkt, h] * LOG2E, acc)
    o_ref[0] = acc


def _bias_tiles(rel_table, blk):
    n_heads = rel_table.shape[1]
    buckets = jnp.asarray(_bucket_tiles(blk))
    return pl.pallas_call(
        _bias_kernel, grid=(n_heads,),
        in_specs=[pl.BlockSpec(memory_space=pltpu.SMEM),
                  pl.BlockSpec((2, blk, blk), lambda h: (0, 0, 0))],
        out_specs=pl.BlockSpec((1, 2, blk, blk), lambda h: (h, 0, 0, 0)),
        out_shape=jax.ShapeDtypeStruct((n_heads, 2, blk, blk), F32),
        compiler_params=_params(1), name="t5_bias_tiles")(rel_table, buckets)


def _moba_kernel(tab_ref, qt_ref, k_ref, v_ref, bias_ref, o_ref,
                 vt_ref, km_ref, sel_ref, s_ref, mx_ref, near_ref, own1_ref, m_ref, acc_ref):
    hp = pl.program_id(1)
    qi = pl.program_id(2)
    blk = MOBA_BLOCK
    nq, _, tq = qt_ref.shape
    nblk = k_ref.shape[1] // blk
    assert tq == 2 * blk and blk >= REL_MAX_DIST
    first_own = 2 * qi

    @pl.when(qi == 0)
    def _():
        def build(jb, carry):
            rows = pl.ds(pl.multiple_of(jb * blk, blk), blk)
            km_ref[pl.ds(jb, 1), :] = jnp.mean(k_ref[0, rows, :].astype(F32),
                                                axis=0, keepdims=True)
            return carry
        lax.fori_loop(0, nblk, build, 0)
        _fill_value_rows(vt_ref, v_ref, blk)

    def head_queries(tile):
        qt = qt_ref[tile]
        return [jnp.where(_head_row_mask(hh), qt, jnp.zeros_like(qt)) for hh in range(2)]

    q_m_t = head_queries(qi)
    km_parts = _split3(km_ref[...])
    blk_id = lax.broadcasted_iota(jnp.int32, (nblk, tq), 0)
    qpos = lax.broadcasted_iota(jnp.int32, (nblk, tq), 1)
    own = first_own + jnp.where(qpos >= blk, 1, 0)
    past = blk_id < own
    for hh in range(2):
        gate = None
        for part in km_parts:
            term = jnp.dot(part, q_m_t[hh], preferred_element_type=F32)
            gate = term if gate is None else gate + term
        work = jnp.where(past, gate, NEG_INF)
        picked = jnp.zeros(gate.shape, F32)
        for _ in range(MOBA_TOP_K):
            best = jnp.max(work, axis=0, keepdims=True)
            first = jnp.min(jnp.where(work == best, blk_id, nblk), axis=0, keepdims=True)
            hit = blk_id == first
            picked = jnp.where(hit, 1.0, picked)
            work = jnp.where(hit, NEG_INF, work)
        keep_all = ((picked > 0.5) & past) | (blk_id == own) | (blk_id == first_own + 1)
        sel_ref[hh] = jnp.where(keep_all, 1.0, 0.0)

    def scores(hh, j, queries=q_m_t):
        rows = pl.ds(pl.multiple_of(j * blk, blk), blk)
        return jnp.dot(k_ref[0, rows, :], queries[hh], preferred_element_type=F32)

    def next_tile_scores():
        q_next = head_queries(jnp.minimum(qi + 1, nq - 1))
        for hh in range(2):
            _store_scores(scores(hh, 0, q_next), s_ref, mx_ref, (0, hh))

    def keep(hh, j):
        return sel_ref[hh, pl.ds(j, 1), :] > 0.5

    _flash_init(m_ref, acc_ref)
    far_bias = [tab_ref[REL_BUCKETS - 1, 2 * hp + hh] * LOG2E for hh in range(2)]

    j_prev = jnp.maximum(first_own - 1, 0)
    lo, hi = slice(0, blk), slice(blk, 2 * blk)
    near_mx = {}

    def near_scores(hh):
        own_t, prev_t = bias_ref[hh, 0], bias_ref[hh, 1]
        mask_prev = jnp.where(keep(hh, j_prev) & (qi >= 1), 0.0, NEG_INF)
        mask_own = jnp.where(keep(hh, first_own), 0.0, NEG_INF)
        far_t = jnp.full((blk, blk), far_bias[hh], F32)
        parts = [scores(hh, j_prev) + mask_prev + jnp.concatenate([prev_t, far_t], axis=1),
                 scores(hh, first_own) + mask_own + jnp.concatenate([own_t, prev_t], axis=1)]
        mx = None
        for i, part in enumerate(parts):
            near_ref[hh, i * blk:(i + 1) * blk] = part
            part_mx = jnp.max(part, axis=0, keepdims=True)
            mx = part_mx if mx is None else jnp.maximum(mx, part_mx)
        rows = pl.ds(pl.multiple_of((first_own + 1) * blk, blk), blk)
        last = jnp.dot(k_ref[0, rows, :], q_m_t[hh][:, hi], preferred_element_type=F32) + own_t
        own1_ref[hh] = last
        near_mx[hh] = (mx[:, lo], jnp.maximum(mx[:, hi], jnp.max(last, axis=0, keepdims=True)))

    def near_update():
        for hh in range(2):
            v_near = [vt_ref[hh, j_prev], vt_ref[hh, first_own], vt_ref[hh, first_own + 1]]
            _flash_update(near_ref[hh, :, lo], near_mx[hh][0], v_near[:2],
                          m_ref, acc_ref, hh, cols=lo)
            _flash_update(jnp.concatenate([near_ref[hh, :, hi], own1_ref[hh]], axis=0),
                          near_mx[hh][1], v_near, m_ref, acc_ref, hh, cols=hi)

    def scores_into(j, slot, hh):
        _store_scores(scores(hh, j), s_ref, mx_ref, (slot, hh))

    def update(j, slot, hh):
        _flash_update(s_ref[slot, hh, :, :tq], mx_ref[slot, hh], vt_ref[hh, j],
                      m_ref, acc_ref, hh, keep=keep(hh, j), const=far_bias[hh])

    _pipelined_blocks(jnp.maximum(first_own - 1, 0), scores_into, update,
                      near_scores, near_update, next_tile_scores)
    _write_heads(o_ref, acc_ref)


def _moba_attention(qt3, k3, vt, rel_table, bias_t, n_heads):
    b, s, d = k3.shape
    blk = MOBA_BLOCK
    tq = qt3.shape[2]
    assert tq == MOBA_Q_TILE and s % tq == 0
    npair = n_heads // 2
    nblk = s // blk
    nq = s // tq
    return pl.pallas_call(
        _moba_kernel, grid=(b, npair, nq),
        in_specs=[
            pl.BlockSpec(memory_space=pltpu.SMEM),
            pl.BlockSpec((nq, LANES, tq), lambda bi, hp, qi: (bi, hp, 0)),
            pl.BlockSpec((1, s, LANES), lambda bi, hp, qi: (bi, 0, hp)),
            pl.BlockSpec((LANES, s), lambda bi, hp, qi: (hp, bi)),
            pl.BlockSpec((2, 2, blk, blk), lambda bi, hp, qi: (hp, 0, 0, 0)),
        ],
        out_specs=pl.BlockSpec((LANES, tq), lambda bi, hp, qi: (hp, bi * nq + qi)),
        out_shape=jax.ShapeDtypeStruct((d, b * s), BF16),
        scratch_shapes=[
            pltpu.VMEM((2, nblk, V_ROWS, blk), BF16),
            pltpu.VMEM((nblk, LANES), F32),
            pltpu.VMEM((2, nblk, tq), F32),
            pltpu.VMEM((2, 2, blk, tq + SCORE_PITCH_PAD), F32),
            pltpu.VMEM((2, 2, 1, tq), F32),
            pltpu.VMEM((2, 2 * blk, tq), F32),
            pltpu.VMEM((2, blk, blk), F32),
            pltpu.VMEM((2, 1, tq), F32),
            pltpu.VMEM((2, V_ROWS, tq), F32),
        ],
        compiler_params=_params(3), name="moba_attention")(
            rel_table, qt3, k3, vt, bias_t)


def _oproj_ffn_kernel(h_ref, ot_ref, wo_ref, g_ref, win_ref, wout_ref, out_ref, *, tf):
    d_ff = wout_ref.shape[0]
    h1 = h_ref[...] + lax.dot_general(ot_ref[...], wo_ref[...], (((0,), (0,)), ((), ())),
                                      preferred_element_type=F32)
    u = _rmsnorm(h1, g_ref[...]).astype(BF16)
    acc = h1
    for c in range(d_ff // tf):
        gate = jnp.dot(u, win_ref[:, c * tf:(c + 1) * tf], preferred_element_type=F32)
        up = jnp.dot(u, win_ref[:, d_ff + c * tf:d_ff + (c + 1) * tf],
                     preferred_element_type=F32)
        act = (gate * jax.nn.sigmoid(gate) * up).astype(BF16)
        acc = acc + jnp.dot(act, wout_ref[c * tf:(c + 1) * tf, :],
                            preferred_element_type=F32)
    out_ref[...] = acc


def _oproj_ffn(h2, o_t, w_o, g, w_in, w_out, *, tm, tf):
    n, d = h2.shape
    d_ff = w_out.shape[0]
    assert d_ff % tf == 0
    row = pl.BlockSpec((tm, d), lambda i: (i, 0))
    return pl.pallas_call(
        functools.partial(_oproj_ffn_kernel, tf=tf), grid=(n // tm,),
        in_specs=[row, pl.BlockSpec((d, tm), lambda i: (0, i)), _resident((d, d)),
                  _resident((1, d)), _resident((d, 2 * d_ff)), _resident((d_ff, d))],
        out_specs=row,
        out_shape=jax.ShapeDtypeStruct((n, d), F32),
        compiler_params=_params(1), name="oproj_ffn")(h2, o_t, w_o, g, w_in, w_out)


def _ple_update(x_ref, g_ref, wg_ref, p_ref, wu_ref):
    x = x_ref[...]
    u = _rmsnorm(x, g_ref[...]).astype(BF16)
    gate = jax.nn.sigmoid(jnp.dot(u, wg_ref[...], preferred_element_type=F32))
    up = jnp.dot(p_ref[...].astype(BF16), wu_ref[...], preferred_element_type=F32)
    return x + gate * up


def _ple_next_kernel(x_ref, g_ref, wg_ref, p_ref, wu_ref, *refs, n_mixer):
    y = _ple_update(x_ref, g_ref, wg_ref, p_ref, wu_ref)
    refs[n_mixer][...] = y
    _mixer_inputs(y, refs[:n_mixer], refs[n_mixer + 1:])


def _ple_final_kernel(x_ref, g_ref, wg_ref, p_ref, wu_ref, fg_ref, out_ref):
    y = _ple_update(x_ref, g_ref, wg_ref, p_ref, wu_ref)
    out_ref[...] = _rmsnorm(y, fg_ref[...])


def _ple(h2, g, w_gate, p3, layer, w_up, *, tm, mixer=None, final_g=None):
    n, d = h2.shape
    pd = p3.shape[2]
    row = pl.BlockSpec((tm, d), lambda i: (i, 0))
    in_specs = [row, _resident((1, d)), _resident((d, d)),
                pl.BlockSpec((None, tm, pd), lambda i: (layer, i, 0)), _resident((pd, d))]
    h_shape = jax.ShapeDtypeStruct((n, d), F32)
    if mixer is None:
        return pl.pallas_call(
            _ple_final_kernel, grid=(n // tm,),
            in_specs=in_specs + [_resident((1, d))], out_specs=row, out_shape=h_shape,
            compiler_params=_params(1), name="ple_final")(h2, g, w_gate, p3, w_up, final_g)
    n_mixer = len(mixer.operands())
    return pl.pallas_call(
        functools.partial(_ple_next_kernel, n_mixer=n_mixer), grid=(n // tm,),
        in_specs=in_specs + mixer.in_specs(),
        out_specs=[row] + mixer.out_specs(tm),
        out_shape=[h_shape] + mixer.out_shapes(n, tm),
        compiler_params=_params(1), name="ple_proj")(h2, g, w_gate, p3, w_up,
                                                     *mixer.operands())


def _row_tile(n, want):
    t = min(want, n)
    assert n % t == 0
    return t


def _col_tile(n, want):
    t = min(want, n)
    while n % t:
        t -= LANES
    return t


def kernel(x, p, attn_norm_g, fox_w_in, fox_b_f, fox_w_o, moba_w_in, moba_w_o, rel_bias_table,
           ffn_norm_g, ffn_w_in, ffn_w_out, ple_norm_g, ple_w_gate, ple_w_up, final_norm_g):
    b, s, d = x.shape
    depth = p.shape[0]
    n_heads = rel_bias_table.shape[1]
    assert d == n_heads * HEAD_DIM and n_heads % 2 == 0 and n_heads <= LANES
    n = b * s
    tm = _row_tile(n, DENSE_ROW_TILE)
    tf = _col_tile(ffn_w_out.shape[1], FFN_COL_CHUNK)

    def row_vec(v):
        return v.reshape(1, -1).astype(F32)

    def mixer(i):
        w = fox_w_in[i // 2] if i % 2 == 0 else moba_w_in[i // 2]
        parts = [row_vec(attn_norm_g[i]), (w[:, :d] * Q_SCALE).T.astype(BF16),
                 w[:, d:2 * d].astype(BF16), w[:, 2 * d:3 * d].T.astype(BF16)]
        if i % 2 == 0:
            parts += [jnp.pad(w[:, 3 * d:], ((0, 0), (0, LANES - n_heads))).astype(BF16),
                      jnp.pad(row_vec(fox_b_f[i // 2]), ((0, 0), (0, LANES - n_heads)))]
        return _Mixer(*parts)

    rel_table = rel_bias_table.astype(F32)
    bias_t = _bias_tiles(rel_table, MOBA_BLOCK)

    h = x.reshape(n, d).astype(F32)
    mixed = _project(h, mixer(0), tm=tm)
    for i in range(depth):
        qt, k3, vt = mixed[0], mixed[1].reshape(b, s, d), mixed[2]
        if i % 2 == 0:
            aqt, ak = _cumsum(mixed[3].reshape(b, s, LANES))
            o_t = _fox_attention(qt, k3, vt, aqt, ak, n_heads)
            w_o = fox_w_o[i // 2]
        else:
            o_t = _moba_attention(qt, k3, vt, rel_table, bias_t, n_heads)
            w_o = moba_w_o[i // 2]
        h = _oproj_ffn(h, o_t, w_o.astype(BF16), row_vec(ffn_norm_g[i]),
                       ffn_w_in[i].astype(BF16), ffn_w_out[i].astype(BF16), tm=tm, tf=tf)
        ple_args = (h, row_vec(ple_norm_g[i]), ple_w_gate[i].astype(BF16),
                    p.reshape(depth, n, -1), i, ple_w_up[i].astype(BF16))
        if i + 1 < depth:
            h, *mixed = _ple(*ple_args, tm=tm, mixer=mixer(i + 1))
        else:
            h = _ple(*ple_args, tm=tm, final_g=row_vec(final_norm_g))
    return h.reshape(b, s, d).astype(x.dtype)
```

```python
import functools
import math
from typing import NamedTuple, Optional

import numpy as np
import jax
import jax.numpy as jnp
from jax import lax
from jax.experimental import pallas as pl
from jax.experimental.pallas import tpu as pltpu

F32 = jnp.float32
BF16 = jnp.bfloat16

RMS_EPS = 1e-6
HEAD_DIM = 64
MOBA_BLOCK = 256
MOBA_TOP_K = 3
REL_BUCKETS = 32
REL_MAX_DIST = 128

LANES = 128
F32_SUBLANES = 8
BF16_SUBLANES = 16
V7X_VMEM_BYTES = 64 * 1024 * 1024
V_ROWS = HEAD_DIM + BF16_SUBLANES
SPLIT_TERMS = 3
GATE_LANES = 8
SCORE_PITCH_PAD = LANES
LOG2E = math.log2(math.e)
Q_SCALE = HEAD_DIM ** -0.5 * LOG2E
MOBA_Q_TILE = 2 * MOBA_BLOCK
ATTN_Q_TILE = MOBA_Q_TILE
DENSE_ROW_TILE = ATTN_Q_TILE
CUMSUM_BLK = ATTN_Q_TILE
PROJ_COL_CHUNK = 512
FFN_COL_CHUNK = 256
VMEM_LIMIT_BYTES = V7X_VMEM_BYTES // 8 * 7
NEG_INF = float("-inf")


def _params(n_axes):
    return pltpu.CompilerParams(
        dimension_semantics=("arbitrary",) * n_axes,
        vmem_limit_bytes=VMEM_LIMIT_BYTES)


def _split3(x):
    x1 = x.astype(BF16)
    r1 = x - x1.astype(F32)
    x2 = r1.astype(BF16)
    x3 = (r1 - x2.astype(F32)).astype(BF16)
    return x1, x2, x3


def _rmsnorm(x, g):
    ms = jnp.mean(x * x, axis=-1, keepdims=True)
    return x * lax.rsqrt(ms + RMS_EPS) * g


def _log_sigmoid(x):
    return jnp.minimum(x, 0.0) - jnp.log1p(jnp.exp(-jnp.abs(x)))


def _resident(shape):
    return pl.BlockSpec(shape, lambda i: (0,) * len(shape), pipeline_mode=pl.Buffered(1))


class _Mixer(NamedTuple):
    g: jax.Array
    w_qt: jax.Array
    w_k: jax.Array
    w_vt: jax.Array
    w_f: Optional[jax.Array] = None
    b_f: Optional[jax.Array] = None

    def operands(self):
        return tuple(a for a in self if a is not None)

    def in_specs(self):
        return [_resident(a.shape) for a in self.operands()]

    def out_specs(self, tm):
        d = self.w_qt.shape[0]
        specs = [pl.BlockSpec((None, d, tm), lambda i: (i, 0, 0)),
                 pl.BlockSpec((tm, d), lambda i: (i, 0)),
                 pl.BlockSpec((d, tm), lambda i: (0, i))]
        if self.w_f is not None:
            specs.append(pl.BlockSpec((tm, LANES), lambda i: (i, 0)))
        return specs

    def out_shapes(self, n, tm):
        d = self.w_qt.shape[0]
        shapes = [jax.ShapeDtypeStruct((n // tm, d, tm), BF16),
                  jax.ShapeDtypeStruct((n, d), BF16), jax.ShapeDtypeStruct((d, n), BF16)]
        if self.w_f is not None:
            shapes.append(jax.ShapeDtypeStruct((n, LANES), F32))
        return shapes


def _chunks(total, want):
    step = want if total % want == 0 else total
    return [slice(c * step, (c + 1) * step) for c in range(total // step)]


def _mixer_inputs(y, mixer_refs, out_refs):
    g_ref, wqt_ref, wk_ref, wvt_ref = mixer_refs[:4]
    u = _rmsnorm(y, g_ref[...]).astype(BF16)
    for wt_ref, out_ref in ((wqt_ref, out_refs[0]), (wvt_ref, out_refs[2])):
        for rows in _chunks(wt_ref.shape[0], PROJ_COL_CHUNK):
            out_ref[rows, :] = lax.dot_general(
                wt_ref[rows, :], u, (((1,), (1,)), ((), ())),
                preferred_element_type=F32).astype(BF16)
    for cols in _chunks(wk_ref.shape[1], PROJ_COL_CHUNK):
        out_refs[1][:, cols] = jnp.dot(u, wk_ref[:, cols],
                                       preferred_element_type=F32).astype(BF16)
    if len(mixer_refs) > 4:
        wf_ref, bf_ref = mixer_refs[4:]
        f_logit = jnp.dot(u, wf_ref[...], preferred_element_type=F32) + bf_ref[...]
        out_refs[3][...] = _log_sigmoid(f_logit)


def _proj_kernel(x_ref, *refs, n_mixer):
    _mixer_inputs(x_ref[...], refs[:n_mixer], refs[n_mixer:])


def _project(h2, mixer, *, tm):
    n, d = h2.shape
    n_mixer = len(mixer.operands())
    return pl.pallas_call(
        functools.partial(_proj_kernel, n_mixer=n_mixer), grid=(n // tm,),
        in_specs=[pl.BlockSpec((tm, d), lambda i: (i, 0))] + mixer.in_specs(),
        out_specs=mixer.out_specs(tm), out_shape=mixer.out_shapes(n, tm),
        compiler_params=_params(1), name="proj")(h2, *mixer.operands())


def _cumsum_kernel(lf_ref, aqt_ref, ak_ref, carry_ref):
    @pl.when(pl.program_id(1) == 0)
    def _():
        carry_ref[...] = jnp.zeros_like(carry_ref)

    t = lf_ref.shape[1]
    row = lax.broadcasted_iota(jnp.int32, (t, t), 0)
    col = lax.broadcasted_iota(jnp.int32, (t, t), 1)
    tril = jnp.where(col <= row, 1.0, 0.0).astype(BF16)
    x1, x2, x3 = _split3(lf_ref[0])
    cs = (jnp.dot(tril, x1, preferred_element_type=F32)
          + jnp.dot(tril, x2, preferred_element_type=F32)
          + jnp.dot(tril, x3, preferred_element_type=F32))
    cs = cs + carry_ref[0:1, :]
    carry_ref[...] = jnp.broadcast_to(cs[t - 1:t, :], carry_ref.shape)

    src = lax.broadcasted_iota(jnp.int32, (LANES, LANES), 0)
    dst = lax.broadcasted_iota(jnp.int32, (LANES, LANES), 1)
    lane = lax.broadcasted_iota(jnp.int32, (1, LANES), 1) & (GATE_LANES - 1)
    aq = jnp.where((lane >= SPLIT_TERMS) & (lane < 2 * SPLIT_TERMS), 1.0, 0.0)
    ak = jnp.where(lane < SPLIT_TERMS, 1.0, 0.0)
    for i, part in enumerate(_split3(cs * LOG2E)):
        to_q = jnp.where(dst == GATE_LANES * src + i, 1.0, 0.0).astype(BF16)
        to_k = jnp.where(dst == GATE_LANES * src + SPLIT_TERMS + i, 1.0, 0.0).astype(BF16)
        aq = aq + jnp.dot(part, to_q, preferred_element_type=F32)
        ak = ak - jnp.dot(part, to_k, preferred_element_type=F32)
    eye = jnp.where(src == dst, 1.0, 0.0).astype(BF16)
    aqt_ref[0] = lax.dot_general(eye, aq.astype(BF16), (((1,), (1,)), ((), ())),
                                 preferred_element_type=F32).astype(BF16)
    ak_ref[0] = ak.astype(BF16)


def _cumsum(lf3):
    b, s, _ = lf3.shape
    t = min(CUMSUM_BLK, s)
    nt = s // t
    spec = pl.BlockSpec((1, t, LANES), lambda i, j: (i, j, 0))
    return pl.pallas_call(
        _cumsum_kernel, grid=(b, nt),
        in_specs=[spec],
        out_specs=[pl.BlockSpec((1, LANES, t), lambda i, j: (i * nt + j, 0, 0)), spec],
        out_shape=[jax.ShapeDtypeStruct((b * nt, LANES, t), BF16),
                   jax.ShapeDtypeStruct(lf3.shape, BF16)],
        scratch_shapes=[pltpu.VMEM((F32_SUBLANES, LANES), F32)],
        compiler_params=_params(2), name="gate_cumsum")(lf3)


def _fill_value_rows(vt_ref, v_ref, blk):
    row = lax.broadcasted_iota(jnp.int32, (V_ROWS - HEAD_DIM, blk), 0)
    tail = jnp.where(row == 0, 1.0, 0.0).astype(BF16)
    for hh in range(2):
        for jb in range(v_ref.shape[1] // blk):
            head_rows = v_ref[hh * HEAD_DIM:(hh + 1) * HEAD_DIM, jb * blk:(jb + 1) * blk]
            vt_ref[hh, jb] = jnp.concatenate([head_rows, tail], axis=0)


def _flash_init(m_ref, acc_ref):
    m_ref[...] = jnp.full(m_ref.shape, NEG_INF, F32)
    acc_ref[...] = jnp.zeros(acc_ref.shape, F32)


def _pipelined_blocks(n, scores_into, update, final_scores, final_update, next_tile_scores):
    def step(j_next, j, slot):
        for hh in range(2):
            scores_into(j_next, 1 - slot, hh)
            update(j, slot, hh)

    def last_update_and_final(j, slot):
        for hh in range(2):
            final_scores(hh)
            update(j, slot, hh)
        next_tile_scores()
        final_update()

    def pair(jj, carry):
        j = 2 * jj
        step(j + 1, j, 0)
        step(j + 2, j + 1, 1)
        return carry
    lax.fori_loop(0, jnp.maximum(n - 1, 0) // 2, pair, 0)

    @pl.when(n % 2 == 1)
    def _():
        last_update_and_final(n - 1, 0)

    @pl.when((n % 2 == 0) & (n > 0))
    def _():
        step(n - 1, n - 2, 0)
        last_update_and_final(n - 1, 1)

    @pl.when(n == 0)
    def _():
        for hh in range(2):
            final_scores(hh)
        next_tile_scores()
        final_update()


def _store_scores(s_t, s_ref, mx_ref, idx):
    s_ref[idx + (slice(None), slice(0, s_t.shape[1]))] = s_t
    mx_ref[idx] = jnp.max(s_t, axis=0, keepdims=True)


def _flash_update(s_t, mx, v_rows, m_ref, acc_ref, hh, keep=None, const=None,
                  cols=slice(None)):
    m_old = m_ref[hh, :, cols]
    if const is not None:
        mx = mx + const
    if keep is not None:
        mx = jnp.where(keep, mx, NEG_INF)
    m_new = jnp.maximum(m_old, mx)
    m_safe = jnp.where(m_new == NEG_INF, 0.0, m_new)
    shift = m_safe if const is None else m_safe - const
    if keep is not None:
        shift = jnp.where(keep, shift, float("inf"))
    p = jnp.exp2(s_t - shift).astype(BF16)
    alpha = jnp.exp2(m_old - m_safe)
    m_ref[hh, :, cols] = m_new
    if not isinstance(v_rows, (list, tuple)):
        v_rows = [v_rows]
    keys = p.shape[0] // len(v_rows)
    acc = alpha * acc_ref[hh, :, cols]
    for i, v_i in enumerate(v_rows):
        acc = acc + jnp.dot(v_i, p[i * keys:(i + 1) * keys], preferred_element_type=F32)
    acc_ref[hh, :, cols] = acc


def _write_heads(o_ref, acc_ref):
    outs = []
    for hh in range(2):
        acc = acc_ref[hh]
        outs.append(acc[:HEAD_DIM] / acc[HEAD_DIM:HEAD_DIM + 1])
    o_ref[...] = jnp.concatenate(outs, axis=0).astype(o_ref.dtype)


def _head_row_mask(hh):
    row = lax.broadcasted_iota(jnp.int32, (LANES, 1), 0)
    return (row < HEAD_DIM) if hh == 0 else (row >= HEAD_DIM)


def _fox_kernel(qt_ref, k_ref, v_ref, aqt_ref, ak_ref, o_ref,
                vt_ref, s_ref, mx_ref, diag_ref, m_ref, acc_ref):
    hp = pl.program_id(1)
    qi = pl.program_id(2)
    nq, _, blk = qt_ref.shape
    nblk = k_ref.shape[1] // blk
    gate_row = lax.broadcasted_iota(jnp.int32, (LANES, 1), 0)

    def query_operands(tile):
        qt, aqt = qt_ref[tile], aqt_ref[tile]
        out = []
        for hh in range(2):
            first = GATE_LANES * (2 * hp + hh)
            own_gate = (gate_row >= first) & (gate_row < first + GATE_LANES)
            q_rows = jnp.where(_head_row_mask(hh), qt, jnp.zeros_like(qt))
            g_rows = jnp.where(own_gate, aqt, jnp.zeros_like(aqt))
            out.append(jnp.concatenate([q_rows, g_rows], axis=0))
        return out

    def block_scores(w, j, hh):
        rows = pl.ds(pl.multiple_of(j * blk, blk), blk)
        keys = jnp.concatenate([k_ref[0, rows, :], ak_ref[0, rows, :]], axis=1)
        return jnp.dot(keys, w[hh], preferred_element_type=F32)

    w_q = query_operands(qi)

    @pl.when(qi == 0)
    def _():
        _fill_value_rows(vt_ref, v_ref, blk)

    def scores_into(j, slot, hh):
        _store_scores(block_scores(w_q, j, hh), s_ref, mx_ref, (slot, hh))

    def next_tile_scores():
        w_next = query_operands(jnp.minimum(qi + 1, nq - 1))
        for hh in range(2):
            _store_scores(block_scores(w_next, 0, hh), s_ref, mx_ref, (0, hh))

    def update(j, slot, hh):
        _flash_update(s_ref[slot, hh, :, :blk], mx_ref[slot, hh], vt_ref[hh, j],
                      m_ref, acc_ref, hh)

    def diagonal_scores(hh):
        diag_ref[hh] = block_scores(w_q, qi, hh)

    def diagonal_update():
        half = blk // 2
        lo, hi = slice(0, half), slice(half, blk)
        causal_lo = (lax.broadcasted_iota(jnp.int32, (half, half), 0)
                     <= lax.broadcasted_iota(jnp.int32, (half, half), 1))
        causal_hi = (lax.broadcasted_iota(jnp.int32, (blk, half), 0)
                     <= lax.broadcasted_iota(jnp.int32, (blk, half), 1) + half)
        for hh in range(2):
            v_rows = vt_ref[hh, qi]
            s_lo = jnp.where(causal_lo, diag_ref[hh, lo, lo], NEG_INF)
            _flash_update(s_lo, jnp.max(s_lo, axis=0, keepdims=True),
                          v_rows[:, lo], m_ref, acc_ref, hh, cols=lo)
            s_hi = jnp.where(causal_hi, diag_ref[hh, :, hi], NEG_INF)
            _flash_update(s_hi, jnp.max(s_hi, axis=0, keepdims=True),
                          v_rows, m_ref, acc_ref, hh, cols=hi)

    _flash_init(m_ref, acc_ref)
    _pipelined_blocks(qi, scores_into, update, diagonal_scores, diagonal_update,
                      next_tile_scores)
    _write_heads(o_ref, acc_ref)


def _fox_attention(qt3, k3, vt, aqt3, ak3, n_heads):
    b, s, d = k3.shape
    assert n_heads * GATE_LANES <= LANES
    blk = qt3.shape[2]
    assert aqt3.shape[2] == blk and s % blk == 0
    npair = n_heads // 2
    nq = s // blk
    return pl.pallas_call(
        _fox_kernel, grid=(b, npair, nq),
        in_specs=[
            pl.BlockSpec((nq, LANES, blk), lambda bi, hp, qi: (bi, hp, 0)),
            pl.BlockSpec((1, s, LANES), lambda bi, hp, qi: (bi, 0, hp)),
            pl.BlockSpec((LANES, s), lambda bi, hp, qi: (hp, bi)),
            pl.BlockSpec((nq, LANES, blk), lambda bi, hp, qi: (bi, 0, 0)),
            pl.BlockSpec((1, s, LANES), lambda bi, hp, qi: (bi, 0, 0)),
        ],
        out_specs=pl.BlockSpec((LANES, blk), lambda bi, hp, qi: (hp, bi * nq + qi)),
        out_shape=jax.ShapeDtypeStruct((d, b * s), BF16),
        scratch_shapes=[
            pltpu.VMEM((2, s // blk, V_ROWS, blk), BF16),
            pltpu.VMEM((2, 2, blk, blk + SCORE_PITCH_PAD), F32),
            pltpu.VMEM((2, 2, 1, blk), F32),
            pltpu.VMEM((2, blk, blk), F32),
            pltpu.VMEM((2, 1, blk), F32),
            pltpu.VMEM((2, V_ROWS, blk), F32),
        ],
        compiler_params=_params(3), name="fox_attention")(qt3, k3, vt, aqt3, ak3)


def _t5_bucket_np(n):
    max_exact = REL_BUCKETS // 2
    nf = np.maximum(n, 1).astype(np.float64)
    large = max_exact + (np.log(nf / max_exact) / math.log(REL_MAX_DIST / max_exact)
                         * (REL_BUCKETS - max_exact)).astype(np.int32)
    return np.where(n < max_exact, n, np.minimum(large, REL_BUCKETS - 1)).astype(np.int32)


def _bucket_tiles(blk):
    key = np.arange(blk)[:, None]
    qry = np.arange(blk)[None, :]
    own = np.where(key <= qry, _t5_bucket_np(np.maximum(qry - key, 0)), -1)
    prev = _t5_bucket_np(blk + qry - key)
    return np.stack([own, prev]).astype(np.int32)


def _bias_kernel(tab_ref, bucket_ref, o_ref):
    h = pl.program_id(0)
    bucket = bucket_ref[...]
    acc = jnp.where(bucket < 0, NEG_INF, 0.0).astype(F32)
    for bkt in range(REL_BUCKETS):
        acc = jnp.where(bucket == bkt, tab_ref[bkt, h] * LOG2E, acc)
    o_ref[0] = acc


def _bias_tiles(rel_table, blk):
    n_heads = rel_table.shape[1]
    buckets = jnp.asarray(_bucket_tiles(blk))
    return pl.pallas_call(
        _bias_kernel, grid=(n_heads,),
        in_specs=[pl.BlockSpec(memory_space=pltpu.SMEM),
                  pl.BlockSpec((2, blk, blk), lambda h: (0, 0, 0))],
        out_specs=pl.BlockSpec((1, 2, blk, blk), lambda h: (h, 0, 0, 0)),
        out_shape=jax.ShapeDtypeStruct((n_heads, 2, blk, blk), F32),
        compiler_params=_params(1), name="t5_bias_tiles")(rel_table, buckets)


def _moba_kernel(tab_ref, qt_ref, k_ref, v_ref, bias_ref, o_ref,
                 vt_ref, km_ref, sel_ref, s_ref, mx_ref, near_ref, own1_ref, m_ref, acc_ref):
    hp = pl.program_id(1)
    qi = pl.program_id(2)
    blk = MOBA_BLOCK
    nq, _, tq = qt_ref.shape
    nblk = k_ref.shape[1] // blk
    assert tq == 2 * blk and blk >= REL_MAX_DIST
    first_own = 2 * qi

    @pl.when(qi == 0)
    def _():
        def build(jb, carry):
            rows = pl.ds(pl.multiple_of(jb * blk, blk), blk)
            km_ref[pl.ds(jb, 1), :] = jnp.mean(k_ref[0, rows, :].astype(F32),
                                                axis=0, keepdims=True)
            return carry
        lax.fori_loop(0, nblk, build, 0)
        _fill_value_rows(vt_ref, v_ref, blk)

    def head_queries(tile):
        qt = qt_ref[tile]
        return [jnp.where(_head_row_mask(hh), qt, jnp.zeros_like(qt)) for hh in range(2)]

    q_m_t = head_queries(qi)
    km_parts = _split3(km_ref[...])
    blk_id = lax.broadcasted_iota(jnp.int32, (nblk, tq), 0)
    qpos = lax.broadcasted_iota(jnp.int32, (nblk, tq), 1)
    own = first_own + jnp.where(qpos >= blk, 1, 0)
    past = blk_id < own
    for hh in range(2):
        gate = None
        for part in km_parts:
            term = jnp.dot(part, q_m_t[hh], preferred_element_type=F32)
            gate = term if gate is None else gate + term
        work = jnp.where(past, gate, NEG_INF)
        picked = jnp.zeros(gate.shape, F32)
        for _ in range(MOBA_TOP_K):
            best = jnp.max(work, axis=0, keepdims=True)
            first = jnp.min(jnp.where(work == best, blk_id, nblk), axis=0, keepdims=True)
            hit = blk_id == first
            picked = jnp.where(hit, 1.0, picked)
            work = jnp.where(hit, NEG_INF, work)
        keep_all = ((picked > 0.5) & past) | (blk_id == own) | (blk_id == first_own + 1)
        sel_ref[hh] = jnp.where(keep_all, 1.0, 0.0)

    def scores(hh, j, queries=q_m_t):
        rows = pl.ds(pl.multiple_of(j * blk, blk), blk)
        return jnp.dot(k_ref[0, rows, :], queries[hh], preferred_element_type=F32)

    def next_tile_scores():
        q_next = head_queries(jnp.minimum(qi + 1, nq - 1))
        for hh in range(2):
            _store_scores(scores(hh, 0, q_next), s_ref, mx_ref, (0, hh))

    def keep(hh, j):
        return sel_ref[hh, pl.ds(j, 1), :] > 0.5

    _flash_init(m_ref, acc_ref)
    far_bias = [tab_ref[REL_BUCKETS - 1, 2 * hp + hh] * LOG2E for hh in range(2)]

    j_prev = jnp.maximum(first_own - 1, 0)
    lo, hi = slice(0, blk), slice(blk, 2 * blk)
    near_mx = {}

    def near_scores(hh):
        own_t, prev_t = bias_ref[hh, 0], bias_ref[hh, 1]
        mask_prev = jnp.where(keep(hh, j_prev) & (qi >= 1), 0.0, NEG_INF)
        mask_own = jnp.where(keep(hh, first_own), 0.0, NEG_INF)
        far_t = jnp.full((blk, blk), far_bias[hh], F32)
        parts = [scores(hh, j_prev) + mask_prev + jnp.concatenate([prev_t, far_t], axis=1),
                 scores(hh, first_own) + mask_own + jnp.concatenate([own_t, prev_t], axis=1)]
        mx = None
        for i, part in enumerate(parts):
            near_ref[hh, i * blk:(i + 1) * blk] = part
            part_mx = jnp.max(part, axis=0, keepdims=True)
            mx = part_mx if mx is None else jnp.maximum(mx, part_mx)
        rows = pl.ds(pl.multiple_of((first_own + 1) * blk, blk), blk)
        last = jnp.dot(k_ref[0, rows, :], q_m_t[hh][:, hi], preferred_element_type=F32) + own_t
        own1_ref[hh] = last
        near_mx[hh] = (mx[:, lo], jnp.maximum(mx[:, hi], jnp.max(last, axis=0, keepdims=True)))

    def near_update():
        for hh in range(2):
            v_near = [vt_ref[hh, j_prev], vt_ref[hh, first_own], vt_ref[hh, first_own + 1]]
            _flash_update(near_ref[hh, :, lo], near_mx[hh][0], v_near[:2],
                          m_ref, acc_ref, hh, cols=lo)
            _flash_update(jnp.concatenate([near_ref[hh, :, hi], own1_ref[hh]], axis=0),
                          near_mx[hh][1], v_near, m_ref, acc_ref, hh, cols=hi)

    def scores_into(j, slot, hh):
        _store_scores(scores(hh, j), s_ref, mx_ref, (slot, hh))

    def update(j, slot, hh):
        _flash_update(s_ref[slot, hh, :, :tq], mx_ref[slot, hh], vt_ref[hh, j],
                      m_ref, acc_ref, hh, keep=keep(hh, j), const=far_bias[hh])

    _pipelined_blocks(jnp.maximum(first_own - 1, 0), scores_into, update,
                      near_scores, near_update, next_tile_scores)
    _write_heads(o_ref, acc_ref)


def _moba_attention(qt3, k3, vt, rel_table, bias_t, n_heads):
    b, s, d = k3.shape
    blk = MOBA_BLOCK
    tq = qt3.shape[2]
    assert tq == MOBA_Q_TILE and s % tq == 0
    npair = n_heads // 2
    nblk = s // blk
    nq = s // tq
    return pl.pallas_call(
        _moba_kernel, grid=(b, npair, nq),
        in_specs=[
            pl.BlockSpec(memory_space=pltpu.SMEM),
            pl.BlockSpec((nq, LANES, tq), lambda bi, hp, qi: (bi, hp, 0)),
            pl.BlockSpec((1, s, LANES), lambda bi, hp, qi: (bi, 0, hp)),
            pl.BlockSpec((LANES, s), lambda bi, hp, qi: (hp, bi)),
            pl.BlockSpec((2, 2, blk, blk), lambda bi, hp, qi: (hp, 0, 0, 0)),
        ],
        out_specs=pl.BlockSpec((LANES, tq), lambda bi, hp, qi: (hp, bi * nq + qi)),
        out_shape=jax.ShapeDtypeStruct((d, b * s), BF16),
        scratch_shapes=[
            pltpu.VMEM((2, nblk, V_ROWS, blk), BF16),
            pltpu.VMEM((nblk, LANES), F32),
            pltpu.VMEM((2, nblk, tq), F32),
            pltpu.VMEM((2, 2, blk, tq + SCORE_PITCH_PAD), F32),
            pltpu.VMEM((2, 2, 1, tq), F32),
            pltpu.VMEM((2, 2 * blk, tq), F32),
            pltpu.VMEM((2, blk, blk), F32),
            pltpu.VMEM((2, 1, tq), F32),
            pltpu.VMEM((2, V_ROWS, tq), F32),
        ],
        compiler_params=_params(3), name="moba_attention")(
            rel_table, qt3, k3, vt, bias_t)


def _oproj_ffn_kernel(h_ref, ot_ref, wo_ref, g_ref, win_ref, wout_ref, out_ref, *, tf):
    d_ff = wout_ref.shape[0]
    h1 = h_ref[...] + lax.dot_general(ot_ref[...], wo_ref[...], (((0,), (0,)), ((), ())),
                                      preferred_element_type=F32)
    u = _rmsnorm(h1, g_ref[...]).astype(BF16)
    acc = h1
    for c in range(d_ff // tf):
        gate = jnp.dot(u, win_ref[:, c * tf:(c + 1) * tf], preferred_element_type=F32)
        up = jnp.dot(u, win_ref[:, d_ff + c * tf:d_ff + (c + 1) * tf],
                     preferred_element_type=F32)
        act = (gate * jax.nn.sigmoid(gate) * up).astype(BF16)
        acc = acc + jnp.dot(act, wout_ref[c * tf:(c + 1) * tf, :],
                            preferred_element_type=F32)
    out_ref[...] = acc


def _oproj_ffn(h2, o_t, w_o, g, w_in, w_out, *, tm, tf):
    n, d = h2.shape
    d_ff = w_out.shape[0]
    assert d_ff % tf == 0
    row = pl.BlockSpec((tm, d), lambda i: (i, 0))
    return pl.pallas_call(
        functools.partial(_oproj_ffn_kernel, tf=tf), grid=(n // tm,),
        in_specs=[row, pl.BlockSpec((d, tm), lambda i: (0, i)), _resident((d, d)),
                  _resident((1, d)), _resident((d, 2 * d_ff)), _resident((d_ff, d))],
        out_specs=row,
        out_shape=jax.ShapeDtypeStruct((n, d), F32),
        compiler_params=_params(1), name="oproj_ffn")(h2, o_t, w_o, g, w_in, w_out)


def _ple_update(x_ref, g_ref, wg_ref, p_ref, wu_ref):
    x = x_ref[...]
    u = _rmsnorm(x, g_ref[...]).astype(BF16)
    gate = jax.nn.sigmoid(jnp.dot(u, wg_ref[...], preferred_element_type=F32))
    up = jnp.dot(p_ref[...].astype(BF16), wu_ref[...], preferred_element_type=F32)
    return x + gate * up


def _ple_next_kernel(x_ref, g_ref, wg_ref, p_ref, wu_ref, *refs, n_mixer):
    y = _ple_update(x_ref, g_ref, wg_ref, p_ref, wu_ref)
    refs[n_mixer][...] = y
    _mixer_inputs(y, refs[:n_mixer], refs[n_mixer + 1:])


def _ple_final_kernel(x_ref, g_ref, wg_ref, p_ref, wu_ref, fg_ref, out_ref):
    y = _ple_update(x_ref, g_ref, wg_ref, p_ref, wu_ref)
    out_ref[...] = _rmsnorm(y, fg_ref[...])


def _ple(h2, g, w_gate, p3, layer, w_up, *, tm, mixer=None, final_g=None):
    n, d = h2.shape
    pd = p3.shape[2]
    row = pl.BlockSpec((tm, d), lambda i: (i, 0))
    in_specs = [row, _resident((1, d)), _resident((d, d)),
                pl.BlockSpec((None, tm, pd), lambda i: (layer, i, 0)), _resident((pd, d))]
    h_shape = jax.ShapeDtypeStruct((n, d), F32)
    if mixer is None:
        return pl.pallas_call(
            _ple_final_kernel, grid=(n // tm,),
            in_specs=in_specs + [_resident((1, d))], out_specs=row, out_shape=h_shape,
            compiler_params=_params(1), name="ple_final")(h2, g, w_gate, p3, w_up, final_g)
    n_mixer = len(mixer.operands())
    return pl.pallas_call(
        functools.partial(_ple_next_kernel, n_mixer=n_mixer), grid=(n // tm,),
        in_specs=in_specs + mixer.in_specs(),
        out_specs=[row] + mixer.out_specs(tm),
        out_shape=[h_shape] + mixer.out_shapes(n, tm),
        compiler_params=_params(1), name="ple_proj")(h2, g, w_gate, p3, w_up,
                                                     *mixer.operands())


def _row_tile(n, want):
    t = min(want, n)
    assert n % t == 0
    return t


def _col_tile(n, want):
    t = min(want, n)
    while n % t:
        t -= LANES
    return t


def kernel(x, p, attn_norm_g, fox_w_in, fox_b_f, fox_w_o, moba_w_in, moba_w_o, rel_bias_table,
           ffn_norm_g, ffn_w_in, ffn_w_out, ple_norm_g, ple_w_gate, ple_w_up, final_norm_g):
    b, s, d = x.shape
    depth = p.shape[0]
    n_heads = rel_bias_table.shape[1]
    assert d == n_heads * HEAD_DIM and n_heads % 2 == 0 and n_heads <= LANES
    n = b * s
    tm = _row_tile(n, DENSE_ROW_TILE)
    tf = _col_tile(ffn_w_out.shape[1], FFN_COL_CHUNK)

    def row_vec(v):
        return v.reshape(1, -1).astype(F32)

    def mixer(i):
        w = fox_w_in[i // 2] if i % 2 == 0 else moba_w_in[i // 2]
        parts = [row_vec(attn_norm_g[i]), (w[:, :d] * Q_SCALE).T.astype(BF16),
                 w[:, d:2 * d].astype(BF16), w[:, 2 * d:3 * d].T.astype(BF16)]
        if i % 2 == 0:
            parts += [jnp.pad(w[:, 3 * d:], ((0, 0), (0, LANES - n_heads))).astype(BF16),
                      jnp.pad(row_vec(fox_b_f[i // 2]), ((0, 0), (0, LANES - n_heads)))]
        return _Mixer(*parts)

    rel_table = rel_bias_table.astype(F32)
    bias_t = _bias_tiles(rel_table, MOBA_BLOCK)

    h = x.reshape(n, d).astype(F32)
    mixed = _project(h, mixer(0), tm=tm)
    for i in range(depth):
        qt, k3, vt = mixed[0], mixed[1].reshape(b, s, d), mixed[2]
        if i % 2 == 0:
            aqt, ak = _cumsum(mixed[3].reshape(b, s, LANES))
            o_t = _fox_attention(qt, k3, vt, aqt, ak, n_heads)
            w_o = fox_w_o[i // 2]
        else:
            o_t = _moba_attention(qt, k3, vt, rel_table, bias_t, n_heads)
            w_o = moba_w_o[i // 2]
        h = _oproj_ffn(h, o_t, w_o.astype(BF16), row_vec(ffn_norm_g[i]),
                       ffn_w_in[i].astype(BF16), ffn_w_out[i].astype(BF16), tm=tm, tf=tf)
        ple_args = (h, row_vec(ple_norm_g[i]), ple_w_gate[i].astype(BF16),
                    p.reshape(depth, n, -1), i, ple_w_up[i].astype(BF16))
        if i + 1 < depth:
            h, *mixed = _ple(*ple_args, tm=tm, mixer=mixer(i + 1))
        else:
            h = _ple(*ple_args, tm=tm, final_g=row_vec(final_norm_g))
    return h.reshape(b, s, d).astype(x.dtype)
```

```python
import functools
import math
from typing import NamedTuple, Optional

import numpy as np
import jax
import jax.numpy as jnp
from jax import lax
from jax.experimental import pallas as pl
from jax.experimental.pallas import tpu as pltpu

F32 = jnp.float32
BF16 = jnp.bfloat16

RMS_EPS = 1e-6
HEAD_DIM = 64
MOBA_BLOCK = 256
MOBA_TOP_K = 3
REL_BUCKETS = 32
REL_MAX_DIST = 128

LANES = 128
F32_SUBLANES = 8
BF16_SUBLANES = 16
V7X_VMEM_BYTES = 64 * 1024 * 1024
V_ROWS = HEAD_DIM + BF16_SUBLANES
SPLIT_TERMS = 3
GATE_LANES = 8
SCORE_PITCH_PAD = LANES
LOG2E = math.log2(math.e)
Q_SCALE = HEAD_DIM ** -0.5 * LOG2E
MOBA_Q_TILE = 2 * MOBA_BLOCK
ATTN_Q_TILE = MOBA_Q_TILE
DENSE_ROW_TILE = ATTN_Q_TILE
CUMSUM_BLK = ATTN_Q_TILE
PROJ_COL_CHUNK = 512
FFN_COL_CHUNK = 256
VMEM_LIMIT_BYTES = V7X_VMEM_BYTES // 8 * 7
NEG_INF = float("-inf")


def _params(n_axes):
    return pltpu.CompilerParams(
        dimension_semantics=("arbitrary",) * n_axes,
        vmem_limit_bytes=VMEM_LIMIT_BYTES)


def _split3(x):
    x1 = x.astype(BF16)
    r1 = x - x1.astype(F32)
    x2 = r1.astype(BF16)
    x3 = (r1 - x2.astype(F32)).astype(BF16)
    return x1, x2, x3


def _rmsnorm(x, g):
    ms = jnp.mean(x * x, axis=-1, keepdims=True)
    return x * lax.rsqrt(ms + RMS_EPS) * g


def _log_sigmoid(x):
    return jnp.minimum(x, 0.0) - jnp.log1p(jnp.exp(-jnp.abs(x)))


def _resident(shape):
    return pl.BlockSpec(shape, lambda i: (0,) * len(shape), pipeline_mode=pl.Buffered(1))


class _Mixer(NamedTuple):
    g: jax.Array
    w_q: jax.Array
    w_k: jax.Array
    w_v: jax.Array
    w_f: Optional[jax.Array] = None
    b_f: Optional[jax.Array] = None

    def operands(self):
        return tuple(a for a in self if a is not None)

    def in_specs(self):
        return [_resident(a.shape) for a in self.operands()]

    def out_specs(self, tm):
        d = self.w_q.shape[1]
        specs = [pl.BlockSpec((None, d, tm), lambda i: (i, 0, 0)),
                 pl.BlockSpec((tm, d), lambda i: (i, 0)),
                 pl.BlockSpec((d, tm), lambda i: (0, i))]
        if self.w_f is not None:
            specs.append(pl.BlockSpec((tm, LANES), lambda i: (i, 0)))
        return specs

    def out_shapes(self, n, tm):
        d = self.w_q.shape[1]
        shapes = [jax.ShapeDtypeStruct((n // tm, d, tm), BF16),
                  jax.ShapeDtypeStruct((n, d), BF16), jax.ShapeDtypeStruct((d, n), BF16)]
        if self.w_f is not None:
            shapes.append(jax.ShapeDtypeStruct((n, LANES), F32))
        return shapes


def _chunks(total, want):
    step = want if total % want == 0 else total
    return [slice(c * step, (c + 1) * step) for c in range(total // step)]


def _mixer_inputs(y, mixer_refs, out_refs):
    g_ref, wq_ref, wk_ref, wv_ref = mixer_refs[:4]
    u = _rmsnorm(y, g_ref[...]).astype(BF16)
    for w_ref, out_ref in ((wq_ref, out_refs[0]), (wv_ref, out_refs[2])):
        for cols in _chunks(w_ref.shape[1], PROJ_COL_CHUNK):
            out_ref[cols, :] = lax.dot_general(
                w_ref[:, cols], u, (((0,), (1,)), ((), ())),
                preferred_element_type=F32).astype(BF16)
    for cols in _chunks(wk_ref.shape[1], PROJ_COL_CHUNK):
        out_refs[1][:, cols] = jnp.dot(u, wk_ref[:, cols],
                                       preferred_element_type=F32).astype(BF16)
    if len(mixer_refs) > 4:
        wf_ref, bf_ref = mixer_refs[4:]
        f_logit = jnp.dot(u, wf_ref[...], preferred_element_type=F32) + bf_ref[...]
        out_refs[3][...] = _log_sigmoid(f_logit)


def _proj_kernel(x_ref, *refs, n_mixer):
    _mixer_inputs(x_ref[...], refs[:n_mixer], refs[n_mixer:])


def _project(h2, mixer, *, tm):
    n, d = h2.shape
    n_mixer = len(mixer.operands())
    return pl.pallas_call(
        functools.partial(_proj_kernel, n_mixer=n_mixer), grid=(n // tm,),
        in_specs=[pl.BlockSpec((tm, d), lambda i: (i, 0))] + mixer.in_specs(),
        out_specs=mixer.out_specs(tm), out_shape=mixer.out_shapes(n, tm),
        compiler_params=_params(1), name="proj")(h2, *mixer.operands())


def _cumsum_kernel(lf_ref, aqt_ref, ak_ref, carry_ref):
    @pl.when(pl.program_id(1) == 0)
    def _():
        carry_ref[...] = jnp.zeros_like(carry_ref)

    t = lf_ref.shape[1]
    row = lax.broadcasted_iota(jnp.int32, (t, t), 0)
    col = lax.broadcasted_iota(jnp.int32, (t, t), 1)
    tril = jnp.where(col <= row, 1.0, 0.0).astype(BF16)
    x1, x2, x3 = _split3(lf_ref[0])
    cs = (jnp.dot(tril, x1, preferred_element_type=F32)
          + jnp.dot(tril, x2, preferred_element_type=F32)
          + jnp.dot(tril, x3, preferred_element_type=F32))
    cs = cs + carry_ref[0:1, :]
    carry_ref[...] = jnp.broadcast_to(cs[t - 1:t, :], carry_ref.shape)

    src = lax.broadcasted_iota(jnp.int32, (LANES, LANES), 0)
    dst = lax.broadcasted_iota(jnp.int32, (LANES, LANES), 1)
    lane = lax.broadcasted_iota(jnp.int32, (1, LANES), 1) & (GATE_LANES - 1)
    aq = jnp.where((lane >= SPLIT_TERMS) & (lane < 2 * SPLIT_TERMS), 1.0, 0.0)
    ak = jnp.where(lane < SPLIT_TERMS, 1.0, 0.0)
    for i, part in enumerate(_split3(cs * LOG2E)):
        to_q = jnp.where(dst == GATE_LANES * src + i, 1.0, 0.0).astype(BF16)
        to_k = jnp.where(dst == GATE_LANES * src + SPLIT_TERMS + i, 1.0, 0.0).astype(BF16)
        aq = aq + jnp.dot(part, to_q, preferred_element_type=F32)
        ak = ak - jnp.dot(part, to_k, preferred_element_type=F32)
    eye = jnp.where(src == dst, 1.0, 0.0).astype(BF16)
    aqt_ref[0] = lax.dot_general(eye, aq.astype(BF16), (((1,), (1,)), ((), ())),
                                 preferred_element_type=F32).astype(BF16)
    ak_ref[0] = ak.astype(BF16)


def _cumsum(lf3):
    b, s, _ = lf3.shape
    t = min(CUMSUM_BLK, s)
    nt = s // t
    spec = pl.BlockSpec((1, t, LANES), lambda i, j: (i, j, 0))
    return pl.pallas_call(
        _cumsum_kernel, grid=(b, nt),
        in_specs=[spec],
        out_specs=[pl.BlockSpec((1, LANES, t), lambda i, j: (i * nt + j, 0, 0)), spec],
        out_shape=[jax.ShapeDtypeStruct((b * nt, LANES, t), BF16),
                   jax.ShapeDtypeStruct(lf3.shape, BF16)],
        scratch_shapes=[pltpu.VMEM((F32_SUBLANES, LANES), F32)],
        compiler_params=_params(2), name="gate_cumsum")(lf3)


def _fill_value_rows(vt_ref, v_ref, blk):
    row = lax.broadcasted_iota(jnp.int32, (V_ROWS - HEAD_DIM, blk), 0)
    tail = jnp.where(row == 0, 1.0, 0.0).astype(BF16)
    for hh in range(2):
        for jb in range(v_ref.shape[1] // blk):
            head_rows = v_ref[hh * HEAD_DIM:(hh + 1) * HEAD_DIM, jb * blk:(jb + 1) * blk]
            vt_ref[hh, jb] = jnp.concatenate([head_rows, tail], axis=0)


def _flash_init(m_ref, acc_ref):
    m_ref[...] = jnp.full(m_ref.shape, NEG_INF, F32)
    acc_ref[...] = jnp.zeros(acc_ref.shape, F32)


def _pipelined_blocks(n, scores_into, update, final_scores, final_update, next_tile_scores):
    def step(j_next, j, slot):
        for hh in range(2):
            scores_into(j_next, 1 - slot, hh)
            update(j, slot, hh)

    def last_update_and_final(j, slot):
        for hh in range(2):
            final_scores(hh)
            update(j, slot, hh)
        next_tile_scores()
        final_update()

    def pair(jj, carry):
        j = 2 * jj
        step(j + 1, j, 0)
        step(j + 2, j + 1, 1)
        return carry
    lax.fori_loop(0, jnp.maximum(n - 1, 0) // 2, pair, 0)

    @pl.when(n % 2 == 1)
    def _():
        last_update_and_final(n - 1, 0)

    @pl.when((n % 2 == 0) & (n > 0))
    def _():
        step(n - 1, n - 2, 0)
        last_update_and_final(n - 1, 1)

    @pl.when(n == 0)
    def _():
        for hh in range(2):
            final_scores(hh)
        next_tile_scores()
        final_update()


def _store_scores(s_t, s_ref, mx_ref, idx):
    s_ref[idx + (slice(None), slice(0, s_t.shape[1]))] = s_t
    mx_ref[idx] = jnp.max(s_t, axis=0, keepdims=True)


def _flash_update(s_t, mx, v_rows, m_ref, acc_ref, hh, keep=None, const=None,
                  cols=slice(None)):
    m_old = m_ref[hh, :, cols]
    if const is not None:
        mx = mx + const
    if keep is not None:
        mx = jnp.where(keep, mx, NEG_INF)
    m_new = jnp.maximum(m_old, mx)
    m_safe = jnp.where(m_new == NEG_INF, 0.0, m_new)
    shift = m_safe if const is None else m_safe - const
    if keep is not None:
        shift = jnp.where(keep, shift, float("inf"))
    p = jnp.exp2(s_t - shift).astype(BF16)
    alpha = jnp.exp2(m_old - m_safe)
    m_ref[hh, :, cols] = m_new
    if not isinstance(v_rows, (list, tuple)):
        v_rows = [v_rows]
    keys = p.shape[0] // len(v_rows)
    acc = alpha * acc_ref[hh, :, cols]
    for i, v_i in enumerate(v_rows):
        acc = acc + jnp.dot(v_i, p[i * keys:(i + 1) * keys], preferred_element_type=F32)
    acc_ref[hh, :, cols] = acc


def _write_heads(o_ref, acc_ref):
    outs = []
    for hh in range(2):
        acc = acc_ref[hh]
        outs.append(acc[:HEAD_DIM] / acc[HEAD_DIM:HEAD_DIM + 1])
    o_ref[...] = jnp.concatenate(outs, axis=0).astype(o_ref.dtype)


def _head_row_mask(hh):
    row = lax.broadcasted_iota(jnp.int32, (LANES, 1), 0)
    return (row < HEAD_DIM) if hh == 0 else (row >= HEAD_DIM)


def _fox_kernel(qt_ref, k_ref, v_ref, aqt_ref, ak_ref, o_ref,
                vt_ref, s_ref, mx_ref, diag_ref, m_ref, acc_ref):
    hp = pl.program_id(1)
    qi = pl.program_id(2)
    nq, _, blk = qt_ref.shape
    nblk = k_ref.shape[1] // blk
    gate_row = lax.broadcasted_iota(jnp.int32, (LANES, 1), 0)

    def query_operands(tile):
        qt, aqt = qt_ref[tile], aqt_ref[tile]
        out = []
        for hh in range(2):
            first = GATE_LANES * (2 * hp + hh)
            own_gate = (gate_row >= first) & (gate_row < first + GATE_LANES)
            q_rows = jnp.where(_head_row_mask(hh), qt, jnp.zeros_like(qt))
            g_rows = jnp.where(own_gate, aqt, jnp.zeros_like(aqt))
            out.append(jnp.concatenate([q_rows, g_rows], axis=0))
        return out

    def block_scores(w, j, hh):
        rows = pl.ds(pl.multiple_of(j * blk, blk), blk)
        keys = jnp.concatenate([k_ref[0, rows, :], ak_ref[0, rows, :]], axis=1)
        return jnp.dot(keys, w[hh], preferred_element_type=F32)

    w_q = query_operands(qi)

    @pl.when(qi == 0)
    def _():
        _fill_value_rows(vt_ref, v_ref, blk)

    def scores_into(j, slot, hh):
        _store_scores(block_scores(w_q, j, hh), s_ref, mx_ref, (slot, hh))

    def next_tile_scores():
        w_next = query_operands(jnp.minimum(qi + 1, nq - 1))
        for hh in range(2):
            _store_scores(block_scores(w_next, 0, hh), s_ref, mx_ref, (0, hh))

    def update(j, slot, hh):
        _flash_update(s_ref[slot, hh, :, :blk], mx_ref[slot, hh], vt_ref[hh, j],
                      m_ref, acc_ref, hh)

    def diagonal_scores(hh):
        diag_ref[hh] = block_scores(w_q, qi, hh)

    def diagonal_update():
        half = blk // 2
        lo, hi = slice(0, half), slice(half, blk)
        causal_lo = (lax.broadcasted_iota(jnp.int32, (half, half), 0)
                     <= lax.broadcasted_iota(jnp.int32, (half, half), 1))
        causal_hi = (lax.broadcasted_iota(jnp.int32, (blk, half), 0)
                     <= lax.broadcasted_iota(jnp.int32, (blk, half), 1) + half)
        for hh in range(2):
            v_rows = vt_ref[hh, qi]
            s_lo = jnp.where(causal_lo, diag_ref[hh, lo, lo], NEG_INF)
            _flash_update(s_lo, jnp.max(s_lo, axis=0, keepdims=True),
                          v_rows[:, lo], m_ref, acc_ref, hh, cols=lo)
            s_hi = jnp.where(causal_hi, diag_ref[hh, :, hi], NEG_INF)
            _flash_update(s_hi, jnp.max(s_hi, axis=0, keepdims=True),
                          v_rows, m_ref, acc_ref, hh, cols=hi)

    _flash_init(m_ref, acc_ref)
    _pipelined_blocks(qi, scores_into, update, diagonal_scores, diagonal_update,
                      next_tile_scores)
    _write_heads(o_ref, acc_ref)


def _fox_attention(qt3, k3, vt, aqt3, ak3, n_heads):
    b, s, d = k3.shape
    assert n_heads * GATE_LANES <= LANES
    blk = qt3.shape[2]
    assert aqt3.shape[2] == blk and s % blk == 0
    npair = n_heads // 2
    nq = s // blk
    return pl.pallas_call(
        _fox_kernel, grid=(b, npair, nq),
        in_specs=[
            pl.BlockSpec((nq, LANES, blk), lambda bi, hp, qi: (bi, hp, 0)),
            pl.BlockSpec((1, s, LANES), lambda bi, hp, qi: (bi, 0, hp)),
            pl.BlockSpec((LANES, s), lambda bi, hp, qi: (hp, bi)),
            pl.BlockSpec((nq, LANES, blk), lambda bi, hp, qi: (bi, 0, 0)),
            pl.BlockSpec((1, s, LANES), lambda bi, hp, qi: (bi, 0, 0)),
        ],
        out_specs=pl.BlockSpec((LANES, blk), lambda bi, hp, qi: (hp, bi * nq + qi)),
        out_shape=jax.ShapeDtypeStruct((d, b * s), BF16),
        scratch_shapes=[
            pltpu.VMEM((2, s // blk, V_ROWS, blk), BF16),
            pltpu.VMEM((2, 2, blk, blk + SCORE_PITCH_PAD), F32),
            pltpu.VMEM((2, 2, 1, blk), F32),
            pltpu.VMEM((2, blk, blk), F32),
            pltpu.VMEM((2, 1, blk), F32),
            pltpu.VMEM((2, V_ROWS, blk), F32),
        ],
        compiler_params=_params(3), name="fox_attention")(qt3, k3, vt, aqt3, ak3)


def _t5_bucket_np(n):
    max_exact = REL_BUCKETS // 2
    nf = np.maximum(n, 1).astype(np.float64)
    large = max_exact + (np.log(nf / max_exact) / math.log(REL_MAX_DIST / max_exact)
                         * (REL_BUCKETS - max_exact)).astype(np.int32)
    return np.where(n < max_exact, n, np.minimum(large, REL_BUCKETS - 1)).astype(np.int32)


def _bucket_tiles(blk):
    key = np.arange(blk)[:, None]
    qry = np.arange(blk)[None, :]
    own = np.where(key <= qry, _t5_bucket_np(np.maximum(qry - key, 0)), -1)
    prev = _t5_bucket_np(blk + qry - key)
    return np.stack([own, prev]).astype(np.int32)


def _bias_kernel(tab_ref, bucket_ref, o_ref):
    h = pl.program_id(0)
    bucket = bucket_ref[...]
    acc = jnp.where(bucket < 0, NEG_INF, 0.0).astype(F32)
    for bkt in range(REL_BUCKETS):
        acc = jnp.where(bucket == bkt, tab_ref[bkt, h] * LOG2E, acc)
    o_ref[0] = acc


def _bias_tiles(rel_table, blk):
    n_heads = rel_table.shape[1]
    buckets = jnp.asarray(_bucket_tiles(blk))
    return pl.pallas_call(
        _bias_kernel, grid=(n_heads,),
        in_specs=[pl.BlockSpec(memory_space=pltpu.SMEM),
                  pl.BlockSpec((2, blk, blk), lambda h: (0, 0, 0))],
        out_specs=pl.BlockSpec((1, 2, blk, blk), lambda h: (h, 0, 0, 0)),
        out_shape=jax.ShapeDtypeStruct((n_heads, 2, blk, blk), F32),
        compiler_params=_params(1), name="t5_bias_tiles")(rel_table, buckets)


def _moba_kernel(tab_ref, qt_ref, k_ref, v_ref, bias_ref, o_ref,
                 vt_ref, km_ref, sel_ref, s_ref, mx_ref, near_ref, own1_ref, m_ref, acc_ref):
    hp = pl.program_id(1)
    qi = pl.program_id(2)
    blk = MOBA_BLOCK
    nq, _, tq = qt_ref.shape
    nblk = k_ref.shape[1] // blk
    assert tq == 2 * blk and blk >= REL_MAX_DIST
    first_own = 2 * qi

    @pl.when(qi == 0)
    def _():
        def build(jb, carry):
            rows = pl.ds(pl.multiple_of(jb * blk, blk), blk)
            km_ref[pl.ds(jb, 1), :] = jnp.mean(k_ref[0, rows, :].astype(F32),
                                                axis=0, keepdims=True)
            return carry
        lax.fori_loop(0, nblk, build, 0)
        _fill_value_rows(vt_ref, v_ref, blk)

    def head_queries(tile):
        qt = qt_ref[tile]
        return [jnp.where(_head_row_mask(hh), qt, jnp.zeros_like(qt)) for hh in range(2)]

    q_m_t = head_queries(qi)
    km_parts = _split3(km_ref[...])
    blk_id = lax.broadcasted_iota(jnp.int32, (nblk, tq), 0)
    qpos = lax.broadcasted_iota(jnp.int32, (nblk, tq), 1)
    own = first_own + jnp.where(qpos >= blk, 1, 0)
    past = blk_id < own
    for hh in range(2):
        gate = None
        for part in km_parts:
            term = jnp.dot(part, q_m_t[hh], preferred_element_type=F32)
            gate = term if gate is None else gate + term
        work = jnp.where(past, gate, NEG_INF)
        picked = jnp.zeros(gate.shape, F32)
        for _ in range(MOBA_TOP_K):
            best = jnp.max(work, axis=0, keepdims=True)
            first = jnp.min(jnp.where(work == best, blk_id, nblk), axis=0, keepdims=True)
            hit = blk_id == first
            picked = jnp.where(hit, 1.0, picked)
            work = jnp.where(hit, NEG_INF, work)
        keep_all = ((picked > 0.5) & past) | (blk_id == own) | (blk_id == first_own + 1)
        sel_ref[hh] = jnp.where(keep_all, 1.0, 0.0)

    def scores(hh, j, queries=q_m_t):
        rows = pl.ds(pl.multiple_of(j * blk, blk), blk)
        return jnp.dot(k_ref[0, rows, :], queries[hh], preferred_element_type=F32)

    def next_tile_scores():
        q_next = head_queries(jnp.minimum(qi + 1, nq - 1))
        for hh in range(2):
            _store_scores(scores(hh, 0, q_next), s_ref, mx_ref, (0, hh))

    def keep(hh, j):
        return sel_ref[hh, pl.ds(j, 1), :] > 0.5

    _flash_init(m_ref, acc_ref)
    far_bias = [tab_ref[REL_BUCKETS - 1, 2 * hp + hh] * LOG2E for hh in range(2)]

    j_prev = jnp.maximum(first_own - 1, 0)
    lo, hi = slice(0, blk), slice(blk, 2 * blk)
    near_mx = {}

    def near_scores(hh):
        own_t, prev_t = bias_ref[hh, 0], bias_ref[hh, 1]
        mask_prev = jnp.where(keep(hh, j_prev) & (qi >= 1), 0.0, NEG_INF)
        mask_own = jnp.where(keep(hh, first_own), 0.0, NEG_INF)
        far_t = jnp.full((blk, blk), far_bias[hh], F32)
        parts = [scores(hh, j_prev) + mask_prev + jnp.concatenate([prev_t, far_t], axis=1),
                 scores(hh, first_own) + mask_own + jnp.concatenate([own_t, prev_t], axis=1)]
        mx = None
        for i, part in enumerate(parts):
            near_ref[hh, i * blk:(i + 1) * blk] = part
            part_mx = jnp.max(part, axis=0, keepdims=True)
            mx = part_mx if mx is None else jnp.maximum(mx, part_mx)
        rows = pl.ds(pl.multiple_of((first_own + 1) * blk, blk), blk)
        last = jnp.dot(k_ref[0, rows, :], q_m_t[hh][:, hi], preferred_element_type=F32) + own_t
        own1_ref[hh] = last
        near_mx[hh] = (mx[:, lo], jnp.maximum(mx[:, hi], jnp.max(last, axis=0, keepdims=True)))

    def near_update():
        for hh in range(2):
            v_near = [vt_ref[hh, j_prev], vt_ref[hh, first_own], vt_ref[hh, first_own + 1]]
            _flash_update(near_ref[hh, :, lo], near_mx[hh][0], v_near[:2],
                          m_ref, acc_ref, hh, cols=lo)
            _flash_update(jnp.concatenate([near_ref[hh, :, hi], own1_ref[hh]], axis=0),
                          near_mx[hh][1], v_near, m_ref, acc_ref, hh, cols=hi)

    def scores_into(j, slot, hh):
        _store_scores(scores(hh, j), s_ref, mx_ref, (slot, hh))

    def update(j, slot, hh):
        _flash_update(s_ref[slot, hh, :, :tq], mx_ref[slot, hh], vt_ref[hh, j],
                      m_ref, acc_ref, hh, keep=keep(hh, j), const=far_bias[hh])

    _pipelined_blocks(jnp.maximum(first_own - 1, 0), scores_into, update,
                      near_scores, near_update, next_tile_scores)
    _write_heads(o_ref, acc_ref)


def _moba_attention(qt3, k3, vt, rel_table, bias_t, n_heads):
    b, s, d = k3.shape
    blk = MOBA_BLOCK
    tq = qt3.shape[2]
    assert tq == MOBA_Q_TILE and s % tq == 0
    npair = n_heads // 2
    nblk = s // blk
    nq = s // tq
    return pl.pallas_call(
        _moba_kernel, grid=(b, npair, nq),
        in_specs=[
            pl.BlockSpec(memory_space=pltpu.SMEM),
            pl.BlockSpec((nq, LANES, tq), lambda bi, hp, qi: (bi, hp, 0)),
            pl.BlockSpec((1, s, LANES), lambda bi, hp, qi: (bi, 0, hp)),
            pl.BlockSpec((LANES, s), lambda bi, hp, qi: (hp, bi)),
            pl.BlockSpec((2, 2, blk, blk), lambda bi, hp, qi: (hp, 0, 0, 0)),
        ],
        out_specs=pl.BlockSpec((LANES, tq), lambda bi, hp, qi: (hp, bi * nq + qi)),
        out_shape=jax.ShapeDtypeStruct((d, b * s), BF16),
        scratch_shapes=[
            pltpu.VMEM((2, nblk, V_ROWS, blk), BF16),
            pltpu.VMEM((nblk, LANES), F32),
            pltpu.VMEM((2, nblk, tq), F32),
            pltpu.VMEM((2, 2, blk, tq + SCORE_PITCH_PAD), F32),
            pltpu.VMEM((2, 2, 1, tq), F32),
            pltpu.VMEM((2, 2 * blk, tq), F32),
            pltpu.VMEM((2, blk, blk), F32),
            pltpu.VMEM((2, 1, tq), F32),
            pltpu.VMEM((2, V_ROWS, tq), F32),
        ],
        compiler_params=_params(3), name="moba_attention")(
            rel_table, qt3, k3, vt, bias_t)


def _oproj_ffn_kernel(h_ref, ot_ref, wo_ref, g_ref, win_ref, wout_ref, out_ref, *, tf):
    d_ff = wout_ref.shape[0]
    h1 = h_ref[...] + lax.dot_general(ot_ref[...], wo_ref[...], (((0,), (0,)), ((), ())),
                                      preferred_element_type=F32)
    u = _rmsnorm(h1, g_ref[...]).astype(BF16)
    acc = h1
    for c in range(d_ff // tf):
        gate = jnp.dot(u, win_ref[:, c * tf:(c + 1) * tf], preferred_element_type=F32)
        up = jnp.dot(u, win_ref[:, d_ff + c * tf:d_ff + (c + 1) * tf],
                     preferred_element_type=F32)
        act = (gate * jax.nn.sigmoid(gate) * up).astype(BF16)
        acc = acc + jnp.dot(act, wout_ref[c * tf:(c + 1) * tf, :],
                            preferred_element_type=F32)
    out_ref[...] = acc


def _oproj_ffn(h2, o_t, w_o, g, w_in, w_out, *, tm, tf):
    n, d = h2.shape
    d_ff = w_out.shape[0]
    assert d_ff % tf == 0
    row = pl.BlockSpec((tm, d), lambda i: (i, 0))
    return pl.pallas_call(
        functools.partial(_oproj_ffn_kernel, tf=tf), grid=(n // tm,),
        in_specs=[row, pl.BlockSpec((d, tm), lambda i: (0, i)), _resident((d, d)),
                  _resident((1, d)), _resident((d, 2 * d_ff)), _resident((d_ff, d))],
        out_specs=row,
        out_shape=jax.ShapeDtypeStruct((n, d), F32),
        compiler_params=_params(1), name="oproj_ffn")(h2, o_t, w_o, g, w_in, w_out)


def _ple_update(x_ref, g_ref, wg_ref, p_ref, wu_ref):
    x = x_ref[...]
    u = _rmsnorm(x, g_ref[...]).astype(BF16)
    gate = jax.nn.sigmoid(jnp.dot(u, wg_ref[...], preferred_element_type=F32))
    up = jnp.dot(p_ref[...].astype(BF16), wu_ref[...], preferred_element_type=F32)
    return x + gate * up


def _ple_next_kernel(x_ref, g_ref, wg_ref, p_ref, wu_ref, *refs, n_mixer):
    y = _ple_update(x_ref, g_ref, wg_ref, p_ref, wu_ref)
    refs[n_mixer][...] = y
    _mixer_inputs(y, refs[:n_mixer], refs[n_mixer + 1:])


def _ple_final_kernel(x_ref, g_ref, wg_ref, p_ref, wu_ref, fg_ref, out_ref):
    y = _ple_update(x_ref, g_ref, wg_ref, p_ref, wu_ref)
    out_ref[...] = _rmsnorm(y, fg_ref[...])


def _ple(h2, g, w_gate, p3, layer, w_up, *, tm, mixer=None, final_g=None):
    n, d = h2.shape
    pd = p3.shape[2]
    row = pl.BlockSpec((tm, d), lambda i: (i, 0))
    in_specs = [row, _resident((1, d)), _resident((d, d)),
                pl.BlockSpec((None, tm, pd), lambda i: (layer, i, 0)), _resident((pd, d))]
    h_shape = jax.ShapeDtypeStruct((n, d), F32)
    if mixer is None:
        return pl.pallas_call(
            _ple_final_kernel, grid=(n // tm,),
            in_specs=in_specs + [_resident((1, d))], out_specs=row, out_shape=h_shape,
            compiler_params=_params(1), name="ple_final")(h2, g, w_gate, p3, w_up, final_g)
    n_mixer = len(mixer.operands())
    return pl.pallas_call(
        functools.partial(_ple_next_kernel, n_mixer=n_mixer), grid=(n // tm,),
        in_specs=in_specs + mixer.in_specs(),
        out_specs=[row] + mixer.out_specs(tm),
        out_shape=[h_shape] + mixer.out_shapes(n, tm),
        compiler_params=_params(1), name="ple_proj")(h2, g, w_gate, p3, w_up,
                                                     *mixer.operands())


def _row_tile(n, want):
    t = min(want, n)
    assert n % t == 0
    return t


def _col_tile(n, want):
    t = min(want, n)
    while n % t:
        t -= LANES
    return t


def kernel(x, p, attn_norm_g, fox_w_in, fox_b_f, fox_w_o, moba_w_in, moba_w_o, rel_bias_table,
           ffn_norm_g, ffn_w_in, ffn_w_out, ple_norm_g, ple_w_gate, ple_w_up, final_norm_g):
    b, s, d = x.shape
    depth = p.shape[0]
    n_heads = rel_bias_table.shape[1]
    assert d == n_heads * HEAD_DIM and n_heads % 2 == 0 and n_heads <= LANES
    n = b * s
    tm = _row_tile(n, DENSE_ROW_TILE)
    tf = _col_tile(ffn_w_out.shape[1], FFN_COL_CHUNK)

    def row_vec(v):
        return v.reshape(1, -1).astype(F32)

    def mixer(i):
        w = fox_w_in[i // 2] if i % 2 == 0 else moba_w_in[i // 2]
        parts = [row_vec(attn_norm_g[i]), (w[:, :d] * Q_SCALE).astype(BF16),
                 w[:, d:2 * d].astype(BF16), w[:, 2 * d:3 * d].astype(BF16)]
        if i % 2 == 0:
            parts += [jnp.pad(w[:, 3 * d:], ((0, 0), (0, LANES - n_heads))).astype(BF16),
                      jnp.pad(row_vec(fox_b_f[i // 2]), ((0, 0), (0, LANES - n_heads)))]
        return _Mixer(*parts)

    rel_table = rel_bias_table.astype(F32)
    bias_t = _bias_tiles(rel_table, MOBA_BLOCK)

    h = x.reshape(n, d).astype(F32)
    mixed = _project(h, mixer(0), tm=tm)
    for i in range(depth):
        qt, k3, vt = mixed[0], mixed[1].reshape(b, s, d), mixed[2]
        if i % 2 == 0:
            aqt, ak = _cumsum(mixed[3].reshape(b, s, LANES))
            o_t = _fox_attention(qt, k3, vt, aqt, ak, n_heads)
            w_o = fox_w_o[i // 2]
        else:
            o_t = _moba_attention(qt, k3, vt, rel_table, bias_t, n_heads)
            w_o = moba_w_o[i // 2]
        h = _oproj_ffn(h, o_t, w_o.astype(BF16), row_vec(ffn_norm_g[i]),
                       ffn_w_in[i].astype(BF16), ffn_w_out[i].astype(BF16), tm=tm, tf=tf)
        ple_args = (h, row_vec(ple_norm_g[i]), ple_w_gate[i].astype(BF16),
                    p.reshape(depth, n, -1), i, ple_w_up[i].astype(BF16))
        if i + 1 < depth:
            h, *mixed = _ple(*ple_args, tm=tm, mixer=mixer(i + 1))
        else:
            h = _ple(*ple_args, tm=tm, final_g=row_vec(final_norm_g))
    return h.reshape(b, s, d).astype(x.dtype)
```

```python
import functools
import math
from typing import NamedTuple, Optional

import numpy as np
import jax
import jax.numpy as jnp
from jax import lax
from jax.experimental import pallas as pl
from jax.experimental.pallas import tpu as pltpu

F32 = jnp.float32
BF16 = jnp.bfloat16

RMS_EPS = 1e-6
HEAD_DIM = 64
MOBA_BLOCK = 256
MOBA_TOP_K = 3
REL_BUCKETS = 32
REL_MAX_DIST = 128

LANES = 128
F32_SUBLANES = 8
BF16_SUBLANES = 16
V7X_VMEM_BYTES = 64 * 1024 * 1024
V_ROWS = HEAD_DIM + BF16_SUBLANES
SPLIT_TERMS = 3
GATE_LANES = 8
SCORE_PITCH_PAD = LANES
LOG2E = math.log2(math.e)
Q_SCALE = HEAD_DIM ** -0.5 * LOG2E
MOBA_Q_TILE = 2 * MOBA_BLOCK
ATTN_Q_TILE = MOBA_Q_TILE
DENSE_ROW_TILE = ATTN_Q_TILE
CUMSUM_BLK = ATTN_Q_TILE
PROJ_COL_CHUNK = 512
FFN_COL_CHUNK = 256
VMEM_LIMIT_BYTES = V7X_VMEM_BYTES // 8 * 7
NEG_INF = float("-inf")


def _params(n_axes):
    return pltpu.CompilerParams(
        dimension_semantics=("arbitrary",) * n_axes,
        vmem_limit_bytes=VMEM_LIMIT_BYTES)


def _split3(x):
    x1 = x.astype(BF16)
    r1 = x - x1.astype(F32)
    x2 = r1.astype(BF16)
    x3 = (r1 - x2.astype(F32)).astype(BF16)
    return x1, x2, x3


def _rmsnorm(x, g):
    ms = jnp.mean(x * x, axis=-1, keepdims=True)
    return x * lax.rsqrt(ms + RMS_EPS) * g


def _log_sigmoid(x):
    return jnp.minimum(x, 0.0) - jnp.log1p(jnp.exp(-jnp.abs(x)))


class _Slab(NamedTuple):
    stack: jax.Array
    layer: int

    @property
    def shape(self):
        return self.stack.shape[1:]


def _resident(x):
    if isinstance(x, _Slab):
        layer, zeros = x.layer, (0,) * len(x.shape)
        return pl.BlockSpec((None,) + x.shape, lambda i: (layer,) + zeros,
                            pipeline_mode=pl.Buffered(1))
    zeros = (0,) * x.ndim
    return pl.BlockSpec(x.shape, lambda i: zeros, pipeline_mode=pl.Buffered(1))


def _array(x):
    return x.stack if isinstance(x, _Slab) else x


class _Mixer(NamedTuple):
    g: jax.Array
    w_qkv: _Slab
    w_f: Optional[jax.Array] = None
    b_f: Optional[jax.Array] = None

    def operands(self):
        return tuple(_array(a) for a in self if a is not None)

    def in_specs(self):
        return [_resident(a) for a in self if a is not None]

    def out_specs(self, tm):
        d = self.w_qkv.shape[0]
        specs = [pl.BlockSpec((None, d, tm), lambda i: (i, 0, 0)),
                 pl.BlockSpec((tm, d), lambda i: (i, 0)),
                 pl.BlockSpec((d, tm), lambda i: (0, i))]
        if self.w_f is not None:
            specs.append(pl.BlockSpec((tm, LANES), lambda i: (i, 0)))
        return specs

    def out_shapes(self, n, tm):
        d = self.w_qkv.shape[0]
        shapes = [jax.ShapeDtypeStruct((n // tm, d, tm), BF16),
                  jax.ShapeDtypeStruct((n, d), BF16), jax.ShapeDtypeStruct((d, n), BF16)]
        if self.w_f is not None:
            shapes.append(jax.ShapeDtypeStruct((n, LANES), F32))
        return shapes


def _chunks(total, want):
    step = want if total % want == 0 else total
    return [slice(c * step, (c + 1) * step) for c in range(total // step)]


def _mixer_inputs(y, mixer_refs, out_refs):
    g_ref, w_ref = mixer_refs[:2]
    d = w_ref.shape[0]
    u = _rmsnorm(y, g_ref[...]).astype(BF16)
    for first_col, out_ref in ((0, out_refs[0]), (2 * d, out_refs[2])):
        for cols in _chunks(d, PROJ_COL_CHUNK):
            w_cols = w_ref[:, first_col + cols.start:first_col + cols.stop]
            out_ref[cols, :] = lax.dot_general(
                w_cols, u, (((0,), (1,)), ((), ())),
                preferred_element_type=F32).astype(BF16)
    for cols in _chunks(d, PROJ_COL_CHUNK):
        out_refs[1][:, cols] = jnp.dot(u, w_ref[:, d + cols.start:d + cols.stop],
                                       preferred_element_type=F32).astype(BF16)
    if len(mixer_refs) > 2:
        wf_ref, bf_ref = mixer_refs[2:]
        f_logit = jnp.dot(u, wf_ref[...], preferred_element_type=F32) + bf_ref[...]
        out_refs[3][...] = _log_sigmoid(f_logit)


def _proj_kernel(x_ref, *refs, n_mixer):
    _mixer_inputs(x_ref[...], refs[:n_mixer], refs[n_mixer:])


def _project(h2, mixer, *, tm):
    n, d = h2.shape
    n_mixer = len(mixer.operands())
    return pl.pallas_call(
        functools.partial(_proj_kernel, n_mixer=n_mixer), grid=(n // tm,),
        in_specs=[pl.BlockSpec((tm, d), lambda i: (i, 0))] + mixer.in_specs(),
        out_specs=mixer.out_specs(tm), out_shape=mixer.out_shapes(n, tm),
        compiler_params=_params(1), name="proj")(h2, *mixer.operands())


def _cumsum_kernel(lf_ref, aqt_ref, ak_ref, carry_ref):
    @pl.when(pl.program_id(1) == 0)
    def _():
        carry_ref[...] = jnp.zeros_like(carry_ref)

    t = lf_ref.shape[1]
    row = lax.broadcasted_iota(jnp.int32, (t, t), 0)
    col = lax.broadcasted_iota(jnp.int32, (t, t), 1)
    tril = jnp.where(col <= row, 1.0, 0.0).astype(BF16)
    x1, x2, x3 = _split3(lf_ref[0])
    cs = (jnp.dot(tril, x1, preferred_element_type=F32)
          + jnp.dot(tril, x2, preferred_element_type=F32)
          + jnp.dot(tril, x3, preferred_element_type=F32))
    cs = cs + carry_ref[0:1, :]
    carry_ref[...] = jnp.broadcast_to(cs[t - 1:t, :], carry_ref.shape)

    src = lax.broadcasted_iota(jnp.int32, (LANES, LANES), 0)
    dst = lax.broadcasted_iota(jnp.int32, (LANES, LANES), 1)
    lane = lax.broadcasted_iota(jnp.int32, (1, LANES), 1) & (GATE_LANES - 1)
    aq = jnp.where((lane >= SPLIT_TERMS) & (lane < 2 * SPLIT_TERMS), 1.0, 0.0)
    ak = jnp.where(lane < SPLIT_TERMS, 1.0, 0.0)
    for i, part in enumerate(_split3(cs * LOG2E)):
        to_q = jnp.where(dst == GATE_LANES * src + i, 1.0, 0.0).astype(BF16)
        to_k = jnp.where(dst == GATE_LANES * src + SPLIT_TERMS + i, 1.0, 0.0).astype(BF16)
        aq = aq + jnp.dot(part, to_q, preferred_element_type=F32)
        ak = ak - jnp.dot(part, to_k, preferred_element_type=F32)
    eye = jnp.where(src == dst, 1.0, 0.0).astype(BF16)
    aqt_ref[0] = lax.dot_general(eye, aq.astype(BF16), (((1,), (1,)), ((), ())),
                                 preferred_element_type=F32).astype(BF16)
    ak_ref[0] = ak.astype(BF16)


def _cumsum(lf3):
    b, s, _ = lf3.shape
    t = min(CUMSUM_BLK, s)
    nt = s // t
    spec = pl.BlockSpec((1, t, LANES), lambda i, j: (i, j, 0))
    return pl.pallas_call(
        _cumsum_kernel, grid=(b, nt),
        in_specs=[spec],
        out_specs=[pl.BlockSpec((1, LANES, t), lambda i, j: (i * nt + j, 0, 0)), spec],
        out_shape=[jax.ShapeDtypeStruct((b * nt, LANES, t), BF16),
                   jax.ShapeDtypeStruct(lf3.shape, BF16)],
        scratch_shapes=[pltpu.VMEM((F32_SUBLANES, LANES), F32)],
        compiler_params=_params(2), name="gate_cumsum")(lf3)


def _fill_value_rows(vt_ref, v_ref, blk):
    row = lax.broadcasted_iota(jnp.int32, (V_ROWS - HEAD_DIM, blk), 0)
    tail = jnp.where(row == 0, 1.0, 0.0).astype(BF16)
    for hh in range(2):
        for jb in range(v_ref.shape[1] // blk):
            head_rows = v_ref[hh * HEAD_DIM:(hh + 1) * HEAD_DIM, jb * blk:(jb + 1) * blk]
            vt_ref[hh, jb] = jnp.concatenate([head_rows, tail], axis=0)


def _flash_init(m_ref, acc_ref):
    m_ref[...] = jnp.full(m_ref.shape, NEG_INF, F32)
    acc_ref[...] = jnp.zeros(acc_ref.shape, F32)


def _pipelined_blocks(n, scores_into, update, final_scores, final_update, next_tile_scores):
    def step(j_next, j, slot):
        for hh in range(2):
            scores_into(j_next, 1 - slot, hh)
            update(j, slot, hh)

    def last_update_and_final(j, slot):
        for hh in range(2):
            final_scores(hh)
            update(j, slot, hh)
        next_tile_scores()
        final_update()

    def pair(jj, carry):
        j = 2 * jj
        step(j + 1, j, 0)
        step(j + 2, j + 1, 1)
        return carry
    lax.fori_loop(0, jnp.maximum(n - 1, 0) // 2, pair, 0)

    @pl.when(n % 2 == 1)
    def _():
        last_update_and_final(n - 1, 0)

    @pl.when((n % 2 == 0) & (n > 0))
    def _():
        step(n - 1, n - 2, 0)
        last_update_and_final(n - 1, 1)

    @pl.when(n == 0)
    def _():
        for hh in range(2):
            final_scores(hh)
        next_tile_scores()
        final_update()


def _store_scores(s_t, s_ref, mx_ref, idx):
    s_ref[idx + (slice(None), slice(0, s_t.shape[1]))] = s_t
    mx_ref[idx] = jnp.max(s_t, axis=0, keepdims=True)


def _flash_update(s_t, mx, v_rows, m_ref, acc_ref, hh, keep=None, const=None,
                  cols=slice(None)):
    m_old = m_ref[hh, :, cols]
    if const is not None:
        mx = mx + const
    if keep is not None:
        mx = jnp.where(keep, mx, NEG_INF)
    m_new = jnp.maximum(m_old, mx)
    m_safe = jnp.where(m_new == NEG_INF, 0.0, m_new)
    shift = m_safe if const is None else m_safe - const
    if keep is not None:
        shift = jnp.where(keep, shift, float("inf"))
    p = jnp.exp2(s_t - shift).astype(BF16)
    alpha = jnp.exp2(m_old - m_safe)
    m_ref[hh, :, cols] = m_new
    if not isinstance(v_rows, (list, tuple)):
        v_rows = [v_rows]
    keys = p.shape[0] // len(v_rows)
    acc = alpha * acc_ref[hh, :, cols]
    for i, v_i in enumerate(v_rows):
        acc = acc + jnp.dot(v_i, p[i * keys:(i + 1) * keys], preferred_element_type=F32)
    acc_ref[hh, :, cols] = acc


def _write_heads(o_ref, acc_ref):
    outs = []
    for hh in range(2):
        acc = acc_ref[hh]
        outs.append(acc[:HEAD_DIM] / acc[HEAD_DIM:HEAD_DIM + 1])
    o_ref[...] = jnp.concatenate(outs, axis=0).astype(o_ref.dtype)


def _head_row_mask(hh):
    row = lax.broadcasted_iota(jnp.int32, (LANES, 1), 0)
    return (row < HEAD_DIM) if hh == 0 else (row >= HEAD_DIM)


def _fox_kernel(qt_ref, k_ref, v_ref, aqt_ref, ak_ref, o_ref,
                vt_ref, s_ref, mx_ref, diag_ref, m_ref, acc_ref):
    hp = pl.program_id(1)
    qi = pl.program_id(2)
    nq, _, blk = qt_ref.shape
    nblk = k_ref.shape[1] // blk
    gate_row = lax.broadcasted_iota(jnp.int32, (LANES, 1), 0)

    def query_operands(tile):
        qt, aqt = qt_ref[tile], aqt_ref[tile]
        out = []
        for hh in range(2):
            first = GATE_LANES * (2 * hp + hh)
            own_gate = (gate_row >= first) & (gate_row < first + GATE_LANES)
            q_rows = jnp.where(_head_row_mask(hh), qt, jnp.zeros_like(qt))
            g_rows = jnp.where(own_gate, aqt, jnp.zeros_like(aqt))
            out.append(jnp.concatenate([q_rows, g_rows], axis=0))
        return out

    def block_scores(w, j, hh):
        rows = pl.ds(pl.multiple_of(j * blk, blk), blk)
        keys = jnp.concatenate([k_ref[0, rows, :], ak_ref[0, rows, :]], axis=1)
        return jnp.dot(keys, w[hh], preferred_element_type=F32)

    w_q = query_operands(qi)

    @pl.when(qi == 0)
    def _():
        _fill_value_rows(vt_ref, v_ref, blk)

    def scores_into(j, slot, hh):
        _store_scores(block_scores(w_q, j, hh), s_ref, mx_ref, (slot, hh))

    def next_tile_scores():
        w_next = query_operands(jnp.minimum(qi + 1, nq - 1))
        for hh in range(2):
            _store_scores(block_scores(w_next, 0, hh), s_ref, mx_ref, (0, hh))

    def update(j, slot, hh):
        _flash_update(s_ref[slot, hh, :, :blk], mx_ref[slot, hh], vt_ref[hh, j],
                      m_ref, acc_ref, hh)

    def diagonal_scores(hh):
        diag_ref[hh] = block_scores(w_q, qi, hh)

    def diagonal_update():
        half = blk // 2
        lo, hi = slice(0, half), slice(half, blk)
        causal_lo = (lax.broadcasted_iota(jnp.int32, (half, half), 0)
                     <= lax.broadcasted_iota(jnp.int32, (half, half), 1))
        causal_hi = (lax.broadcasted_iota(jnp.int32, (blk, half), 0)
                     <= lax.broadcasted_iota(jnp.int32, (blk, half), 1) + half)
        for hh in range(2):
            v_rows = vt_ref[hh, qi]
            s_lo = jnp.where(causal_lo, diag_ref[hh, lo, lo], NEG_INF)
            _flash_update(s_lo, jnp.max(s_lo, axis=0, keepdims=True),
                          v_rows[:, lo], m_ref, acc_ref, hh, cols=lo)
            s_hi = jnp.where(causal_hi, diag_ref[hh, :, hi], NEG_INF)
            _flash_update(s_hi, jnp.max(s_hi, axis=0, keepdims=True),
                          v_rows, m_ref, acc_ref, hh, cols=hi)

    _flash_init(m_ref, acc_ref)
    _pipelined_blocks(qi, scores_into, update, diagonal_scores, diagonal_update,
                      next_tile_scores)
    _write_heads(o_ref, acc_ref)


def _fox_attention(qt3, k3, vt, aqt3, ak3, n_heads):
    b, s, d = k3.shape
    assert n_heads * GATE_LANES <= LANES
    blk = qt3.shape[2]
    assert aqt3.shape[2] == blk and s % blk == 0
    npair = n_heads // 2
    nq = s // blk
    return pl.pallas_call(
        _fox_kernel, grid=(b, npair, nq),
        in_specs=[
            pl.BlockSpec((nq, LANES, blk), lambda bi, hp, qi: (bi, hp, 0)),
            pl.BlockSpec((1, s, LANES), lambda bi, hp, qi: (bi, 0, hp)),
            pl.BlockSpec((LANES, s), lambda bi, hp, qi: (hp, bi)),
            pl.BlockSpec((nq, LANES, blk), lambda bi, hp, qi: (bi, 0, 0)),
            pl.BlockSpec((1, s, LANES), lambda bi, hp, qi: (bi, 0, 0)),
        ],
        out_specs=pl.BlockSpec((LANES, blk), lambda bi, hp, qi: (hp, bi * nq + qi)),
        out_shape=jax.ShapeDtypeStruct((d, b * s), BF16),
        scratch_shapes=[
            pltpu.VMEM((2, s // blk, V_ROWS, blk), BF16),
            pltpu.VMEM((2, 2, blk, blk + SCORE_PITCH_PAD), F32),
            pltpu.VMEM((2, 2, 1, blk), F32),
            pltpu.VMEM((2, blk, blk), F32),
            pltpu.VMEM((2, 1, blk), F32),
            pltpu.VMEM((2, V_ROWS, blk), F32),
        ],
        compiler_params=_params(3), name="fox_attention")(qt3, k3, vt, aqt3, ak3)


def _t5_bucket_np(n):
    max_exact = REL_BUCKETS // 2
    nf = np.maximum(n, 1).astype(np.float64)
    large = max_exact + (np.log(nf / max_exact) / math.log(REL_MAX_DIST / max_exact)
                         * (REL_BUCKETS - max_exact)).astype(np.int32)
    return np.where(n < max_exact, n, np.minimum(large, REL_BUCKETS - 1)).astype(np.int32)


def _bucket_tiles(blk):
    key = np.arange(blk)[:, None]
    qry = np.arange(blk)[None, :]
    own = np.where(key <= qry, _t5_bucket_np(np.maximum(qry - key, 0)), -1)
    prev = _t5_bucket_np(blk + qry - key)
    return np.stack([own, prev]).astype(np.int32)


def _bias_kernel(tab_ref, bucket_ref, o_ref):
    h = pl.program_id(0)
    bucket = bucket_ref[...]
    acc = jnp.where(bucket < 0, NEG_INF, 0.0).astype(F32)
    for bkt in range(REL_BUCKETS):
        acc = jnp.where(bucket == bkt, tab_ref[bkt, h] * LOG2E, acc)
    o_ref[0] = acc


def _bias_tiles(rel_table, blk):
    n_heads = rel_table.shape[1]
    buckets = jnp.asarray(_bucket_tiles(blk))
    return pl.pallas_call(
        _bias_kernel, grid=(n_heads,),
        in_specs=[pl.BlockSpec(memory_space=pltpu.SMEM),
                  pl.BlockSpec((2, blk, blk), lambda h: (0, 0, 0))],
        out_specs=pl.BlockSpec((1, 2, blk, blk), lambda h: (h, 0, 0, 0)),
        out_shape=jax.ShapeDtypeStruct((n_heads, 2, blk, blk), F32),
        compiler_params=_params(1), name="t5_bias_tiles")(rel_table, buckets)


def _moba_kernel(tab_ref, qt_ref, k_ref, v_ref, bias_ref, o_ref,
                 vt_ref, km_ref, sel_ref, s_ref, mx_ref, near_ref, own1_ref, m_ref, acc_ref):
    hp = pl.program_id(1)
    qi = pl.program_id(2)
    blk = MOBA_BLOCK
    nq, _, tq = qt_ref.shape
    nblk = k_ref.shape[1] // blk
    assert tq == 2 * blk and blk >= REL_MAX_DIST
    first_own = 2 * qi

    @pl.when(qi == 0)
    def _():
        def build(jb, carry):
            rows = pl.ds(pl.multiple_of(jb * blk, blk), blk)
            km_ref[pl.ds(jb, 1), :] = jnp.mean(k_ref[0, rows, :].astype(F32),
                                                axis=0, keepdims=True)
            return carry
        lax.fori_loop(0, nblk, build, 0)
        _fill_value_rows(vt_ref, v_ref, blk)

    def head_queries(tile):
        qt = qt_ref[tile]
        return [jnp.where(_head_row_mask(hh), qt, jnp.zeros_like(qt)) for hh in range(2)]

    q_m_t = head_queries(qi)
    km_parts = _split3(km_ref[...])
    blk_id = lax.broadcasted_iota(jnp.int32, (nblk, tq), 0)
    qpos = lax.broadcasted_iota(jnp.int32, (nblk, tq), 1)
    own = first_own + jnp.where(qpos >= blk, 1, 0)
    past = blk_id < own
    for hh in range(2):
        gate = None
        for part in km_parts:
            term = jnp.dot(part, q_m_t[hh], preferred_element_type=F32)
            gate = term if gate is None else gate + term
        work = jnp.where(past, gate, NEG_INF)
        picked = jnp.zeros(gate.shape, F32)
        for _ in range(MOBA_TOP_K):
            best = jnp.max(work, axis=0, keepdims=True)
            first = jnp.min(jnp.where(work == best, blk_id, nblk), axis=0, keepdims=True)
            hit = blk_id == first
            picked = jnp.where(hit, 1.0, picked)
            work = jnp.where(hit, NEG_INF, work)
        keep_all = ((picked > 0.5) & past) | (blk_id == own) | (blk_id == first_own + 1)
        sel_ref[hh] = jnp.where(keep_all, 1.0, 0.0)

    def scores(hh, j, queries=q_m_t):
        rows = pl.ds(pl.multiple_of(j * blk, blk), blk)
        return jnp.dot(k_ref[0, rows, :], queries[hh], preferred_element_type=F32)

    def next_tile_scores():
        q_next = head_queries(jnp.minimum(qi + 1, nq - 1))
        for hh in range(2):
            _store_scores(scores(hh, 0, q_next), s_ref, mx_ref, (0, hh))

    def keep(hh, j):
        return sel_ref[hh, pl.ds(j, 1), :] > 0.5

    _flash_init(m_ref, acc_ref)
    far_bias = [tab_ref[REL_BUCKETS - 1, 2 * hp + hh] * LOG2E for hh in range(2)]

    j_prev = jnp.maximum(first_own - 1, 0)
    lo, hi = slice(0, blk), slice(blk, 2 * blk)
    near_mx = {}

    def near_scores(hh):
        own_t, prev_t = bias_ref[hh, 0], bias_ref[hh, 1]
        mask_prev = jnp.where(keep(hh, j_prev) & (qi >= 1), 0.0, NEG_INF)
        mask_own = jnp.where(keep(hh, first_own), 0.0, NEG_INF)
        far_t = jnp.full((blk, blk), far_bias[hh], F32)
        parts = [scores(hh, j_prev) + mask_prev + jnp.concatenate([prev_t, far_t], axis=1),
                 scores(hh, first_own) + mask_own + jnp.concatenate([own_t, prev_t], axis=1)]
        mx = None
        for i, part in enumerate(parts):
            near_ref[hh, i * blk:(i + 1) * blk] = part
            part_mx = jnp.max(part, axis=0, keepdims=True)
            mx = part_mx if mx is None else jnp.maximum(mx, part_mx)
        rows = pl.ds(pl.multiple_of((first_own + 1) * blk, blk), blk)
        last = jnp.dot(k_ref[0, rows, :], q_m_t[hh][:, hi], preferred_element_type=F32) + own_t
        own1_ref[hh] = last
        near_mx[hh] = (mx[:, lo], jnp.maximum(mx[:, hi], jnp.max(last, axis=0, keepdims=True)))

    def near_update():
        for hh in range(2):
            v_near = [vt_ref[hh, j_prev], vt_ref[hh, first_own], vt_ref[hh, first_own + 1]]
            _flash_update(near_ref[hh, :, lo], near_mx[hh][0], v_near[:2],
                          m_ref, acc_ref, hh, cols=lo)
            _flash_update(jnp.concatenate([near_ref[hh, :, hi], own1_ref[hh]], axis=0),
                          near_mx[hh][1], v_near, m_ref, acc_ref, hh, cols=hi)

    def scores_into(j, slot, hh):
        _store_scores(scores(hh, j), s_ref, mx_ref, (slot, hh))

    def update(j, slot, hh):
        _flash_update(s_ref[slot, hh, :, :tq], mx_ref[slot, hh], vt_ref[hh, j],
                      m_ref, acc_ref, hh, keep=keep(hh, j), const=far_bias[hh])

    _pipelined_blocks(jnp.maximum(first_own - 1, 0), scores_into, update,
                      near_scores, near_update, next_tile_scores)
    _write_heads(o_ref, acc_ref)


def _moba_attention(qt3, k3, vt, rel_table, bias_t, n_heads):
    b, s, d = k3.shape
    blk = MOBA_BLOCK
    tq = qt3.shape[2]
    assert tq == MOBA_Q_TILE and s % tq == 0
    npair = n_heads // 2
    nblk = s // blk
    nq = s // tq
    return pl.pallas_call(
        _moba_kernel, grid=(b, npair, nq),
        in_specs=[
            pl.BlockSpec(memory_space=pltpu.SMEM),
            pl.BlockSpec((nq, LANES, tq), lambda bi, hp, qi: (bi, hp, 0)),
            pl.BlockSpec((1, s, LANES), lambda bi, hp, qi: (bi, 0, hp)),
            pl.BlockSpec((LANES, s), lambda bi, hp, qi: (hp, bi)),
            pl.BlockSpec((2, 2, blk, blk), lambda bi, hp, qi: (hp, 0, 0, 0)),
        ],
        out_specs=pl.BlockSpec((LANES, tq), lambda bi, hp, qi: (hp, bi * nq + qi)),
        out_shape=jax.ShapeDtypeStruct((d, b * s), BF16),
        scratch_shapes=[
            pltpu.VMEM((2, nblk, V_ROWS, blk), BF16),
            pltpu.VMEM((nblk, LANES), F32),
            pltpu.VMEM((2, nblk, tq), F32),
            pltpu.VMEM((2, 2, blk, tq + SCORE_PITCH_PAD), F32),
            pltpu.VMEM((2, 2, 1, tq), F32),
            pltpu.VMEM((2, 2 * blk, tq), F32),
            pltpu.VMEM((2, blk, blk), F32),
            pltpu.VMEM((2, 1, tq), F32),
            pltpu.VMEM((2, V_ROWS, tq), F32),
        ],
        compiler_params=_params(3), name="moba_attention")(
            rel_table, qt3, k3, vt, bias_t)


def _oproj_ffn_kernel(h_ref, ot_ref, wo_ref, g_ref, win_ref, wout_ref, out_ref, *, tf):
    d_ff = wout_ref.shape[0]
    h1 = h_ref[...] + lax.dot_general(ot_ref[...], wo_ref[...], (((0,), (0,)), ((), ())),
                                      preferred_element_type=F32)
    u = _rmsnorm(h1, g_ref[...]).astype(BF16)
    acc = h1
    for c in range(d_ff // tf):
        gate = jnp.dot(u, win_ref[:, c * tf:(c + 1) * tf], preferred_element_type=F32)
        up = jnp.dot(u, win_ref[:, d_ff + c * tf:d_ff + (c + 1) * tf],
                     preferred_element_type=F32)
        act = (gate * jax.nn.sigmoid(gate) * up).astype(BF16)
        acc = acc + jnp.dot(act, wout_ref[c * tf:(c + 1) * tf, :],
                            preferred_element_type=F32)
    out_ref[...] = acc


def _oproj_ffn(h2, o_t, w_o, g, w_in, w_out, *, tm, tf):
    n, d = h2.shape
    d_ff = w_out.shape[0]
    assert d_ff % tf == 0
    row = pl.BlockSpec((tm, d), lambda i: (i, 0))
    weights = (w_o, g, w_in, w_out)
    return pl.pallas_call(
        functools.partial(_oproj_ffn_kernel, tf=tf), grid=(n // tm,),
        in_specs=[row, pl.BlockSpec((d, tm), lambda i: (0, i))]
        + [_resident(w) for w in weights],
        out_specs=row,
        out_shape=jax.ShapeDtypeStruct((n, d), F32),
        compiler_params=_params(1), name="oproj_ffn")(h2, o_t, *map(_array, weights))


def _ple_update(x_ref, g_ref, wg_ref, p_ref, wu_ref):
    x = x_ref[...]
    u = _rmsnorm(x, g_ref[...]).astype(BF16)
    gate = jax.nn.sigmoid(jnp.dot(u, wg_ref[...], preferred_element_type=F32))
    up = jnp.dot(p_ref[...].astype(BF16), wu_ref[...], preferred_element_type=F32)
    return x + gate * up


def _ple_next_kernel(x_ref, g_ref, wg_ref, p_ref, wu_ref, *refs, n_mixer):
    y = _ple_update(x_ref, g_ref, wg_ref, p_ref, wu_ref)
    refs[n_mixer][...] = y
    _mixer_inputs(y, refs[:n_mixer], refs[n_mixer + 1:])


def _ple_final_kernel(x_ref, g_ref, wg_ref, p_ref, wu_ref, fg_ref, out_ref):
    y = _ple_update(x_ref, g_ref, wg_ref, p_ref, wu_ref)
    out_ref[...] = _rmsnorm(y, fg_ref[...])


def _ple(h2, g, w_gate, p, w_up, *, tm, mixer=None, final_g=None):
    n, d = h2.shape
    pd, layer = p.shape[1], p.layer
    row = pl.BlockSpec((tm, d), lambda i: (i, 0))
    in_specs = [row, _resident(g), _resident(w_gate),
                pl.BlockSpec((None, tm, pd), lambda i: (layer, i, 0)), _resident(w_up)]
    operands = (h2, g, _array(w_gate), p.stack, _array(w_up))
    h_shape = jax.ShapeDtypeStruct((n, d), F32)
    if mixer is None:
        return pl.pallas_call(
            _ple_final_kernel, grid=(n // tm,),
            in_specs=in_specs + [_resident(final_g)], out_specs=row, out_shape=h_shape,
            compiler_params=_params(1), name="ple_final")(*operands, final_g)
    n_mixer = len(mixer.operands())
    return pl.pallas_call(
        functools.partial(_ple_next_kernel, n_mixer=n_mixer), grid=(n // tm,),
        in_specs=in_specs + mixer.in_specs(),
        out_specs=[row] + mixer.out_specs(tm),
        out_shape=[h_shape] + mixer.out_shapes(n, tm),
        compiler_params=_params(1), name="ple_proj")(*operands, *mixer.operands())


def _row_tile(n, want):
    t = min(want, n)
    assert n % t == 0
    return t


def _col_tile(n, want):
    t = min(want, n)
    while n % t:
        t -= LANES
    return t


def kernel(x, p, attn_norm_g, fox_w_in, fox_b_f, fox_w_o, moba_w_in, moba_w_o, rel_bias_table,
           ffn_norm_g, ffn_w_in, ffn_w_out, ple_norm_g, ple_w_gate, ple_w_up, final_norm_g):
    b, s, d = x.shape
    depth = p.shape[0]
    n_heads = rel_bias_table.shape[1]
    assert d == n_heads * HEAD_DIM and n_heads % 2 == 0 and n_heads <= LANES
    n = b * s
    tm = _row_tile(n, DENSE_ROW_TILE)
    tf = _col_tile(ffn_w_out.shape[1], FFN_COL_CHUNK)

    def row_vec(v):
        return v.reshape(1, -1).astype(F32)

    def bf16_stack(w, scaled_cols=0):
        if scaled_cols:
            col = lax.broadcasted_iota(jnp.int32, (1, 1, w.shape[2]), 2)
            w = w * jnp.where(col < scaled_cols, Q_SCALE, 1.0)
        return w.astype(BF16)

    w_in_mix = (bf16_stack(fox_w_in, d), bf16_stack(moba_w_in, d))
    w_o_mix = (bf16_stack(fox_w_o), bf16_stack(moba_w_o))
    w_ffn_in, w_ffn_out = bf16_stack(ffn_w_in), bf16_stack(ffn_w_out)
    w_ple_gate, w_ple_up = bf16_stack(ple_w_gate), bf16_stack(ple_w_up)
    p3 = p.reshape(depth, n, -1)

    def mixer(i):
        parts = [row_vec(attn_norm_g[i]), _Slab(w_in_mix[i % 2], i // 2)]
        if i % 2 == 0:
            w_f = fox_w_in[i // 2][:, 3 * d:]
            parts += [jnp.pad(w_f, ((0, 0), (0, LANES - n_heads))).astype(BF16),
                      jnp.pad(row_vec(fox_b_f[i // 2]), ((0, 0), (0, LANES - n_heads)))]
        return _Mixer(*parts)

    rel_table = rel_bias_table.astype(F32)
    bias_t = _bias_tiles(rel_table, MOBA_BLOCK)

    h = x.reshape(n, d).astype(F32)
    mixed = _project(h, mixer(0), tm=tm)
    for i in range(depth):
        qt, k3, vt = mixed[0], mixed[1].reshape(b, s, d), mixed[2]
        if i % 2 == 0:
            aqt, ak = _cumsum(mixed[3].reshape(b, s, LANES))
            o_t = _fox_attention(qt, k3, vt, aqt, ak, n_heads)
        else:
            o_t = _moba_attention(qt, k3, vt, rel_table, bias_t, n_heads)
        h = _oproj_ffn(h, o_t, _Slab(w_o_mix[i % 2], i // 2), row_vec(ffn_norm_g[i]),
                       _Slab(w_ffn_in, i), _Slab(w_ffn_out, i), tm=tm, tf=tf)
        ple_args = (h, row_vec(ple_norm_g[i]), _Slab(w_ple_gate, i), _Slab(p3, i),
                    _Slab(w_ple_up, i))
        if i + 1 < depth:
            h, *mixed = _ple(*ple_args, tm=tm, mixer=mixer(i + 1))
        else:
            h = _ple(*ple_args, tm=tm, final_g=row_vec(final_norm_g))
    return h.reshape(b, s, d).astype(x.dtype)
```

```python
import functools
import math
from typing import NamedTuple, Optional

import numpy as np
import jax
import jax.numpy as jnp
from jax import lax
from jax.experimental import pallas as pl
from jax.experimental.pallas import tpu as pltpu

F32 = jnp.float32
BF16 = jnp.bfloat16

RMS_EPS = 1e-6
HEAD_DIM = 64
MOBA_BLOCK = 256
MOBA_TOP_K = 3
REL_BUCKETS = 32
REL_MAX_DIST = 128

LANES = 128
F32_SUBLANES = 8
BF16_SUBLANES = 16
V7X_VMEM_BYTES = 64 * 1024 * 1024
V_ROWS = HEAD_DIM + BF16_SUBLANES
SPLIT_TERMS = 3
GATE_LANES = 8
SCORE_PITCH_PAD = LANES
LOG2E = math.log2(math.e)
Q_SCALE = HEAD_DIM ** -0.5 * LOG2E
MOBA_Q_TILE = 2 * MOBA_BLOCK
ATTN_Q_TILE = MOBA_Q_TILE
DENSE_ROW_TILE = ATTN_Q_TILE
CUMSUM_BLK = ATTN_Q_TILE
PROJ_COL_CHUNK = 512
FFN_COL_CHUNK = 256
VMEM_LIMIT_BYTES = V7X_VMEM_BYTES // 8 * 7
NEG_INF = float("-inf")


def _params(n_axes):
    return pltpu.CompilerParams(
        dimension_semantics=("arbitrary",) * n_axes,
        vmem_limit_bytes=VMEM_LIMIT_BYTES)


def _split3(x):
    x1 = x.astype(BF16)
    r1 = x - x1.astype(F32)
    x2 = r1.astype(BF16)
    x3 = (r1 - x2.astype(F32)).astype(BF16)
    return x1, x2, x3


def _rmsnorm(x, g):
    ms = jnp.mean(x * x, axis=-1, keepdims=True)
    return x * lax.rsqrt(ms + RMS_EPS) * g


def _log_sigmoid(x):
    return jnp.minimum(x, 0.0) - jnp.log1p(jnp.exp(-jnp.abs(x)))


class _Slab(NamedTuple):
    stack: jax.Array
    layer: int

    @property
    def shape(self):
        return self.stack.shape[1:]


def _resident(x):
    if isinstance(x, _Slab):
        layer, zeros = x.layer, (0,) * len(x.shape)
        return pl.BlockSpec((None,) + x.shape, lambda i: (layer,) + zeros,
                            pipeline_mode=pl.Buffered(1))
    zeros = (0,) * x.ndim
    return pl.BlockSpec(x.shape, lambda i: zeros, pipeline_mode=pl.Buffered(1))


def _array(x):
    return x.stack if isinstance(x, _Slab) else x


class _Mixer(NamedTuple):
    g: jax.Array
    w_qkv: _Slab
    w_f: Optional[jax.Array] = None
    b_f: Optional[jax.Array] = None

    def operands(self):
        return tuple(_array(a) for a in self if a is not None)

    def in_specs(self):
        return [_resident(a) for a in self if a is not None]

    def out_specs(self, tm):
        d = self.w_qkv.shape[0]
        specs = [pl.BlockSpec((None, d, tm), lambda i: (i, 0, 0)),
                 pl.BlockSpec((tm, d), lambda i: (i, 0)),
                 pl.BlockSpec((d, tm), lambda i: (0, i))]
        if self.w_f is not None:
            specs.append(pl.BlockSpec((tm, LANES), lambda i: (i, 0)))
        return specs

    def out_shapes(self, n, tm):
        d = self.w_qkv.shape[0]
        shapes = [jax.ShapeDtypeStruct((n // tm, d, tm), BF16),
                  jax.ShapeDtypeStruct((n, d), BF16), jax.ShapeDtypeStruct((d, n), BF16)]
        if self.w_f is not None:
            shapes.append(jax.ShapeDtypeStruct((n, LANES), F32))
        return shapes


def _chunks(total, want):
    step = want if total % want == 0 else total
    return [slice(c * step, (c + 1) * step) for c in range(total // step)]


def _mixer_inputs(y, mixer_refs, out_refs):
    g_ref, w_ref = mixer_refs[:2]
    d = w_ref.shape[0]
    u = _rmsnorm(y, g_ref[...]).astype(BF16)
    for first_col, out_ref in ((0, out_refs[0]), (2 * d, out_refs[2])):
        for cols in _chunks(d, PROJ_COL_CHUNK):
            w_cols = w_ref[:, first_col + cols.start:first_col + cols.stop]
            out_ref[cols, :] = lax.dot_general(
                w_cols, u, (((0,), (1,)), ((), ())),
                preferred_element_type=F32).astype(BF16)
    for cols in _chunks(d, PROJ_COL_CHUNK):
        out_refs[1][:, cols] = jnp.dot(u, w_ref[:, d + cols.start:d + cols.stop],
                                       preferred_element_type=F32).astype(BF16)
    if len(mixer_refs) > 2:
        wf_ref, bf_ref = mixer_refs[2:]
        f_logit = jnp.dot(u, wf_ref[...], preferred_element_type=F32) + bf_ref[...]
        out_refs[3][...] = _log_sigmoid(f_logit)


def _proj_kernel(x_ref, *refs, n_mixer):
    _mixer_inputs(x_ref[...], refs[:n_mixer], refs[n_mixer:])


def _project(h2, mixer, *, tm):
    n, d = h2.shape
    n_mixer = len(mixer.operands())
    return pl.pallas_call(
        functools.partial(_proj_kernel, n_mixer=n_mixer), grid=(n // tm,),
        in_specs=[pl.BlockSpec((tm, d), lambda i: (i, 0))] + mixer.in_specs(),
        out_specs=mixer.out_specs(tm), out_shape=mixer.out_shapes(n, tm),
        compiler_params=_params(1), name="proj")(h2, *mixer.operands())


def _cumsum_kernel(lf_ref, aqt_ref, ak_ref, carry_ref):
    @pl.when(pl.program_id(1) == 0)
    def _():
        carry_ref[...] = jnp.zeros_like(carry_ref)

    t = lf_ref.shape[1]
    row = lax.broadcasted_iota(jnp.int32, (t, t), 0)
    col = lax.broadcasted_iota(jnp.int32, (t, t), 1)
    tril = jnp.where(col <= row, 1.0, 0.0).astype(BF16)
    x1, x2, x3 = _split3(lf_ref[0])
    cs = (jnp.dot(tril, x1, preferred_element_type=F32)
          + jnp.dot(tril, x2, preferred_element_type=F32)
          + jnp.dot(tril, x3, preferred_element_type=F32))
    cs = cs + carry_ref[0:1, :]
    carry_ref[...] = jnp.broadcast_to(cs[t - 1:t, :], carry_ref.shape)

    src = lax.broadcasted_iota(jnp.int32, (LANES, LANES), 0)
    dst = lax.broadcasted_iota(jnp.int32, (LANES, LANES), 1)
    lane = lax.broadcasted_iota(jnp.int32, (1, LANES), 1) & (GATE_LANES - 1)
    aq = jnp.where((lane >= SPLIT_TERMS) & (lane < 2 * SPLIT_TERMS), 1.0, 0.0)
    ak = jnp.where(lane < SPLIT_TERMS, 1.0, 0.0)
    for i, part in enumerate(_split3(cs * LOG2E)):
        to_q = jnp.where(dst == GATE_LANES * src + i, 1.0, 0.0).astype(BF16)
        to_k = jnp.where(dst == GATE_LANES * src + SPLIT_TERMS + i, 1.0, 0.0).astype(BF16)
        aq = aq + jnp.dot(part, to_q, preferred_element_type=F32)
        ak = ak - jnp.dot(part, to_k, preferred_element_type=F32)
    eye = jnp.where(src == dst, 1.0, 0.0).astype(BF16)
    aqt_ref[0] = lax.dot_general(eye, aq.astype(BF16), (((1,), (1,)), ((), ())),
                                 preferred_element_type=F32).astype(BF16)
    ak_ref[0] = ak.astype(BF16)


def _cumsum(lf3):
    b, s, _ = lf3.shape
    t = min(CUMSUM_BLK, s)
    nt = s // t
    spec = pl.BlockSpec((1, t, LANES), lambda i, j: (i, j, 0))
    return pl.pallas_call(
        _cumsum_kernel, grid=(b, nt),
        in_specs=[spec],
        out_specs=[pl.BlockSpec((1, LANES, t), lambda i, j: (i * nt + j, 0, 0)), spec],
        out_shape=[jax.ShapeDtypeStruct((b * nt, LANES, t), BF16),
                   jax.ShapeDtypeStruct(lf3.shape, BF16)],
        scratch_shapes=[pltpu.VMEM((F32_SUBLANES, LANES), F32)],
        compiler_params=_params(2), name="gate_cumsum")(lf3)


def _fill_value_rows(vt_ref, v_ref, blk):
    row = lax.broadcasted_iota(jnp.int32, (V_ROWS - HEAD_DIM, blk), 0)
    tail = jnp.where(row == 0, 1.0, 0.0).astype(BF16)
    for hh in range(2):
        for jb in range(v_ref.shape[1] // blk):
            head_rows = v_ref[hh * HEAD_DIM:(hh + 1) * HEAD_DIM, jb * blk:(jb + 1) * blk]
            vt_ref[hh, jb] = jnp.concatenate([head_rows, tail], axis=0)


def _flash_init(m_ref, acc_ref):
    m_ref[...] = jnp.full(m_ref.shape, NEG_INF, F32)
    acc_ref[...] = jnp.zeros(acc_ref.shape, F32)


def _pipelined_blocks(n, scores_into, update, final_scores, final_update, next_tile_scores):
    def step(j_next, j, slot):
        for hh in range(2):
            scores_into(j_next, 1 - slot, hh)
            update(j, slot, hh)

    def last_update_and_final(j, slot):
        for hh in range(2):
            final_scores(hh)
            update(j, slot, hh)
        next_tile_scores()
        final_update()

    def pair(jj, carry):
        j = 2 * jj
        step(j + 1, j, 0)
        step(j + 2, j + 1, 1)
        return carry
    lax.fori_loop(0, jnp.maximum(n - 1, 0) // 2, pair, 0)

    @pl.when(n % 2 == 1)
    def _():
        last_update_and_final(n - 1, 0)

    @pl.when((n % 2 == 0) & (n > 0))
    def _():
        step(n - 1, n - 2, 0)
        last_update_and_final(n - 1, 1)

    @pl.when(n == 0)
    def _():
        for hh in range(2):
            final_scores(hh)
        next_tile_scores()
        final_update()


def _store_scores(s_t, s_ref, mx_ref, idx):
    s_ref[idx + (slice(None), slice(0, s_t.shape[1]))] = s_t
    mx_ref[idx] = jnp.max(s_t, axis=0, keepdims=True)


def _flash_update(s_t, mx, v_rows, m_ref, acc_ref, hh, keep=None, const=None,
                  cols=slice(None)):
    m_old = m_ref[hh, :, cols]
    if const is not None:
        mx = mx + const
    if keep is not None:
        mx = jnp.where(keep, mx, NEG_INF)
    m_new = jnp.maximum(m_old, mx)
    m_safe = jnp.where(m_new == NEG_INF, 0.0, m_new)
    shift = m_safe if const is None else m_safe - const
    if keep is not None:
        shift = jnp.where(keep, shift, float("inf"))
    p = jnp.exp2(s_t - shift).astype(BF16)
    alpha = jnp.exp2(m_old - m_safe)
    m_ref[hh, :, cols] = m_new
    if not isinstance(v_rows, (list, tuple)):
        v_rows = [v_rows]
    keys = p.shape[0] // len(v_rows)
    acc = alpha * acc_ref[hh, :, cols]
    for i, v_i in enumerate(v_rows):
        acc = acc + jnp.dot(v_i, p[i * keys:(i + 1) * keys], preferred_element_type=F32)
    acc_ref[hh, :, cols] = acc


def _head_outputs(acc_ref):
    outs = []
    for hh in range(2):
        acc = acc_ref[hh]
        outs.append(acc[:HEAD_DIM] / acc[HEAD_DIM:HEAD_DIM + 1])
    return jnp.concatenate(outs, axis=0).astype(BF16)


def _for_each_query_tile(nq, tile):
    def body(qi, carry):
        tile(qi)
        return carry
    lax.fori_loop(0, nq, body, 0)


def _head_row_mask(hh):
    row = lax.broadcasted_iota(jnp.int32, (LANES, 1), 0)
    return (row < HEAD_DIM) if hh == 0 else (row >= HEAD_DIM)


def _fox_kernel(qt_ref, k_ref, v_ref, *refs):
    vt_ref = refs[3]
    nq, _, blk = qt_ref.shape
    _fill_value_rows(vt_ref, v_ref, blk)
    hp = pl.program_id(1)
    _for_each_query_tile(nq, lambda qi: _fox_tile(qi, hp, qt_ref, k_ref, *refs))


def _fox_tile(qi, hp, qt_ref, k_ref, aqt_ref, ak_ref, o_ref,
              vt_ref, s_ref, mx_ref, diag_ref, m_ref, acc_ref):
    nq, _, blk = qt_ref.shape
    nblk = k_ref.shape[1] // blk
    gate_row = lax.broadcasted_iota(jnp.int32, (LANES, 1), 0)

    def query_operands(tile):
        qt, aqt = qt_ref[tile], aqt_ref[tile]
        out = []
        for hh in range(2):
            first = GATE_LANES * (2 * hp + hh)
            own_gate = (gate_row >= first) & (gate_row < first + GATE_LANES)
            q_rows = jnp.where(_head_row_mask(hh), qt, jnp.zeros_like(qt))
            g_rows = jnp.where(own_gate, aqt, jnp.zeros_like(aqt))
            out.append(jnp.concatenate([q_rows, g_rows], axis=0))
        return out

    def block_scores(w, j, hh):
        rows = pl.ds(pl.multiple_of(j * blk, blk), blk)
        keys = jnp.concatenate([k_ref[0, rows, :], ak_ref[0, rows, :]], axis=1)
        return jnp.dot(keys, w[hh], preferred_element_type=F32)

    w_q = query_operands(qi)

    def scores_into(j, slot, hh):
        _store_scores(block_scores(w_q, j, hh), s_ref, mx_ref, (slot, hh))

    def next_tile_scores():
        w_next = query_operands(jnp.minimum(qi + 1, nq - 1))
        for hh in range(2):
            _store_scores(block_scores(w_next, 0, hh), s_ref, mx_ref, (0, hh))

    def update(j, slot, hh):
        _flash_update(s_ref[slot, hh, :, :blk], mx_ref[slot, hh], vt_ref[hh, j],
                      m_ref, acc_ref, hh)

    def diagonal_scores(hh):
        diag_ref[hh] = block_scores(w_q, qi, hh)

    def diagonal_update():
        half = blk // 2
        lo, hi = slice(0, half), slice(half, blk)
        causal_lo = (lax.broadcasted_iota(jnp.int32, (half, half), 0)
                     <= lax.broadcasted_iota(jnp.int32, (half, half), 1))
        causal_hi = (lax.broadcasted_iota(jnp.int32, (blk, half), 0)
                     <= lax.broadcasted_iota(jnp.int32, (blk, half), 1) + half)
        for hh in range(2):
            v_rows = vt_ref[hh, qi]
            s_lo = jnp.where(causal_lo, diag_ref[hh, lo, lo], NEG_INF)
            _flash_update(s_lo, jnp.max(s_lo, axis=0, keepdims=True),
                          v_rows[:, lo], m_ref, acc_ref, hh, cols=lo)
            s_hi = jnp.where(causal_hi, diag_ref[hh, :, hi], NEG_INF)
            _flash_update(s_hi, jnp.max(s_hi, axis=0, keepdims=True),
                          v_rows, m_ref, acc_ref, hh, cols=hi)

    _flash_init(m_ref, acc_ref)
    _pipelined_blocks(qi, scores_into, update, diagonal_scores, diagonal_update,
                      next_tile_scores)
    o_ref[qi] = _head_outputs(acc_ref)


def _fox_attention(qt3, k3, vt, aqt3, ak3, n_heads):
    b, s, d = k3.shape
    assert n_heads * GATE_LANES <= LANES
    blk = qt3.shape[2]
    assert aqt3.shape[2] == blk and s % blk == 0
    npair = n_heads // 2
    nq = s // blk
    head_pair_tiles = pl.BlockSpec((nq, LANES, blk), lambda bi, hp: (bi, hp, 0))
    return pl.pallas_call(
        _fox_kernel, grid=(b, npair),
        in_specs=[
            head_pair_tiles,
            pl.BlockSpec((1, s, LANES), lambda bi, hp: (bi, 0, hp)),
            pl.BlockSpec((LANES, s), lambda bi, hp: (hp, bi)),
            pl.BlockSpec((nq, LANES, blk), lambda bi, hp: (bi, 0, 0)),
            pl.BlockSpec((1, s, LANES), lambda bi, hp: (bi, 0, 0)),
        ],
        out_specs=head_pair_tiles,
        out_shape=jax.ShapeDtypeStruct(qt3.shape, BF16),
        scratch_shapes=[
            pltpu.VMEM((2, s // blk, V_ROWS, blk), BF16),
            pltpu.VMEM((2, 2, blk, blk + SCORE_PITCH_PAD), F32),
            pltpu.VMEM((2, 2, 1, blk), F32),
            pltpu.VMEM((2, blk, blk), F32),
            pltpu.VMEM((2, 1, blk), F32),
            pltpu.VMEM((2, V_ROWS, blk), F32),
        ],
        compiler_params=_params(2), name="fox_attention")(qt3, k3, vt, aqt3, ak3)


def _t5_bucket_np(n):
    max_exact = REL_BUCKETS // 2
    nf = np.maximum(n, 1).astype(np.float64)
    large = max_exact + (np.log(nf / max_exact) / math.log(REL_MAX_DIST / max_exact)
                         * (REL_BUCKETS - max_exact)).astype(np.int32)
    return np.where(n < max_exact, n, np.minimum(large, REL_BUCKETS - 1)).astype(np.int32)


def _bucket_tiles(blk):
    key = np.arange(blk)[:, None]
    qry = np.arange(blk)[None, :]
    own = np.where(key <= qry, _t5_bucket_np(np.maximum(qry - key, 0)), -1)
    prev = _t5_bucket_np(blk + qry - key)
    return np.stack([own, prev]).astype(np.int32)


def _bias_kernel(tab_ref, bucket_ref, o_ref):
    h = pl.program_id(0)
    bucket = bucket_ref[...]
    acc = jnp.where(bucket < 0, NEG_INF, 0.0).astype(F32)
    for bkt in range(REL_BUCKETS):
        acc = jnp.where(bucket == bkt, tab_ref[bkt, h] * LOG2E, acc)
    o_ref[0] = acc


def _bias_tiles(rel_table, blk):
    n_heads = rel_table.shape[1]
    buckets = jnp.asarray(_bucket_tiles(blk))
    return pl.pallas_call(
        _bias_kernel, grid=(n_heads,),
        in_specs=[pl.BlockSpec(memory_space=pltpu.SMEM),
                  pl.BlockSpec((2, blk, blk), lambda h: (0, 0, 0))],
        out_specs=pl.BlockSpec((1, 2, blk, blk), lambda h: (h, 0, 0, 0)),
        out_shape=jax.ShapeDtypeStruct((n_heads, 2, blk, blk), F32),
        compiler_params=_params(1), name="t5_bias_tiles")(rel_table, buckets)


def _moba_kernel(tab_ref, qt_ref, k_ref, v_ref, bias_ref, o_ref, vt_ref, km_ref, *scratch):
    blk = MOBA_BLOCK

    def block_mean(jb, carry):
        rows = pl.ds(pl.multiple_of(jb * blk, blk), blk)
        km_ref[pl.ds(jb, 1), :] = jnp.mean(k_ref[0, rows, :].astype(F32),
                                            axis=0, keepdims=True)
        return carry
    lax.fori_loop(0, k_ref.shape[1] // blk, block_mean, 0)
    _fill_value_rows(vt_ref, v_ref, blk)
    hp = pl.program_id(1)
    _for_each_query_tile(
        qt_ref.shape[0],
        lambda qi: _moba_tile(qi, hp, tab_ref, qt_ref, k_ref, bias_ref, o_ref,
                              vt_ref, km_ref, *scratch))


def _moba_tile(qi, hp, tab_ref, qt_ref, k_ref, bias_ref, o_ref,
               vt_ref, km_ref, sel_ref, s_ref, mx_ref, near_ref, own1_ref, m_ref, acc_ref):
    blk = MOBA_BLOCK
    nq, _, tq = qt_ref.shape
    nblk = k_ref.shape[1] // blk
    assert tq == 2 * blk and blk >= REL_MAX_DIST
    first_own = 2 * qi

    def head_queries(tile):
        qt = qt_ref[tile]
        return [jnp.where(_head_row_mask(hh), qt, jnp.zeros_like(qt)) for hh in range(2)]

    q_m_t = head_queries(qi)
    km_parts = _split3(km_ref[...])
    blk_id = lax.broadcasted_iota(jnp.int32, (nblk, tq), 0)
    qpos = lax.broadcasted_iota(jnp.int32, (nblk, tq), 1)
    own = first_own + jnp.where(qpos >= blk, 1, 0)
    past = blk_id < own
    for hh in range(2):
        gate = None
        for part in km_parts:
            term = jnp.dot(part, q_m_t[hh], preferred_element_type=F32)
            gate = term if gate is None else gate + term
        work = jnp.where(past, gate, NEG_INF)
        picked = jnp.zeros(gate.shape, F32)
        for _ in range(MOBA_TOP_K):
            best = jnp.max(work, axis=0, keepdims=True)
            first = jnp.min(jnp.where(work == best, blk_id, nblk), axis=0, keepdims=True)
            hit = blk_id == first
            picked = jnp.where(hit, 1.0, picked)
            work = jnp.where(hit, NEG_INF, work)
        keep_all = ((picked > 0.5) & past) | (blk_id == own) | (blk_id == first_own + 1)
        sel_ref[hh] = jnp.where(keep_all, 1.0, 0.0)

    def scores(hh, j, queries=q_m_t):
        rows = pl.ds(pl.multiple_of(j * blk, blk), blk)
        return jnp.dot(k_ref[0, rows, :], queries[hh], preferred_element_type=F32)

    def next_tile_scores():
        q_next = head_queries(jnp.minimum(qi + 1, nq - 1))
        for hh in range(2):
            _store_scores(scores(hh, 0, q_next), s_ref, mx_ref, (0, hh))

    def keep(hh, j):
        return sel_ref[hh, pl.ds(j, 1), :] > 0.5

    _flash_init(m_ref, acc_ref)
    far_bias = [tab_ref[REL_BUCKETS - 1, 2 * hp + hh] * LOG2E for hh in range(2)]

    j_prev = jnp.maximum(first_own - 1, 0)
    lo, hi = slice(0, blk), slice(blk, 2 * blk)
    near_mx = {}

    def near_scores(hh):
        own_t, prev_t = bias_ref[hh, 0], bias_ref[hh, 1]
        mask_prev = jnp.where(keep(hh, j_prev) & (qi >= 1), 0.0, NEG_INF)
        mask_own = jnp.where(keep(hh, first_own), 0.0, NEG_INF)
        far_t = jnp.full((blk, blk), far_bias[hh], F32)
        parts = [scores(hh, j_prev) + mask_prev + jnp.concatenate([prev_t, far_t], axis=1),
                 scores(hh, first_own) + mask_own + jnp.concatenate([own_t, prev_t], axis=1)]
        mx = None
        for i, part in enumerate(parts):
            near_ref[hh, i * blk:(i + 1) * blk] = part
            part_mx = jnp.max(part, axis=0, keepdims=True)
            mx = part_mx if mx is None else jnp.maximum(mx, part_mx)
        rows = pl.ds(pl.multiple_of((first_own + 1) * blk, blk), blk)
        last = jnp.dot(k_ref[0, rows, :], q_m_t[hh][:, hi], preferred_element_type=F32) + own_t
        own1_ref[hh] = last
        near_mx[hh] = (mx[:, lo], jnp.maximum(mx[:, hi], jnp.max(last, axis=0, keepdims=True)))

    def near_update():
        for hh in range(2):
            v_near = [vt_ref[hh, j_prev], vt_ref[hh, first_own], vt_ref[hh, first_own + 1]]
            _flash_update(near_ref[hh, :, lo], near_mx[hh][0], v_near[:2],
                          m_ref, acc_ref, hh, cols=lo)
            _flash_update(jnp.concatenate([near_ref[hh, :, hi], own1_ref[hh]], axis=0),
                          near_mx[hh][1], v_near, m_ref, acc_ref, hh, cols=hi)

    def scores_into(j, slot, hh):
        _store_scores(scores(hh, j), s_ref, mx_ref, (slot, hh))

    def update(j, slot, hh):
        _flash_update(s_ref[slot, hh, :, :tq], mx_ref[slot, hh], vt_ref[hh, j],
                      m_ref, acc_ref, hh, keep=keep(hh, j), const=far_bias[hh])

    _pipelined_blocks(jnp.maximum(first_own - 1, 0), scores_into, update,
                      near_scores, near_update, next_tile_scores)
    o_ref[qi] = _head_outputs(acc_ref)


def _moba_attention(qt3, k3, vt, rel_table, bias_t, n_heads):
    b, s, d = k3.shape
    blk = MOBA_BLOCK
    tq = qt3.shape[2]
    assert tq == MOBA_Q_TILE and s % tq == 0
    npair = n_heads // 2
    nblk = s // blk
    nq = s // tq
    head_pair_tiles = pl.BlockSpec((nq, LANES, tq), lambda bi, hp: (bi, hp, 0))
    return pl.pallas_call(
        _moba_kernel, grid=(b, npair),
        in_specs=[
            pl.BlockSpec(memory_space=pltpu.SMEM),
            head_pair_tiles,
            pl.BlockSpec((1, s, LANES), lambda bi, hp: (bi, 0, hp)),
            pl.BlockSpec((LANES, s), lambda bi, hp: (hp, bi)),
            pl.BlockSpec((2, 2, blk, blk), lambda bi, hp: (hp, 0, 0, 0)),
        ],
        out_specs=head_pair_tiles,
        out_shape=jax.ShapeDtypeStruct(qt3.shape, BF16),
        scratch_shapes=[
            pltpu.VMEM((2, nblk, V_ROWS, blk), BF16),
            pltpu.VMEM((nblk, LANES), F32),
            pltpu.VMEM((2, nblk, tq), F32),
            pltpu.VMEM((2, 2, blk, tq + SCORE_PITCH_PAD), F32),
            pltpu.VMEM((2, 2, 1, tq), F32),
            pltpu.VMEM((2, 2 * blk, tq), F32),
            pltpu.VMEM((2, blk, blk), F32),
            pltpu.VMEM((2, 1, tq), F32),
            pltpu.VMEM((2, V_ROWS, tq), F32),
        ],
        compiler_params=_params(2), name="moba_attention")(
            rel_table, qt3, k3, vt, bias_t)


def _oproj_ffn_kernel(h_ref, ot_ref, wo_ref, g_ref, win_ref, wout_ref, out_ref, *, tf):
    d_ff = wout_ref.shape[0]
    h1 = h_ref[...] + lax.dot_general(ot_ref[...], wo_ref[...], (((0,), (0,)), ((), ())),
                                      preferred_element_type=F32)
    u = _rmsnorm(h1, g_ref[...]).astype(BF16)
    acc = h1
    for c in range(d_ff // tf):
        gate = jnp.dot(u, win_ref[:, c * tf:(c + 1) * tf], preferred_element_type=F32)
        up = jnp.dot(u, win_ref[:, d_ff + c * tf:d_ff + (c + 1) * tf],
                     preferred_element_type=F32)
        act = (gate * jax.nn.sigmoid(gate) * up).astype(BF16)
        acc = acc + jnp.dot(act, wout_ref[c * tf:(c + 1) * tf, :],
                            preferred_element_type=F32)
    out_ref[...] = acc


def _oproj_ffn(h2, o_t, w_o, g, w_in, w_out, *, tm, tf):
    n, d = h2.shape
    d_ff = w_out.shape[0]
    assert d_ff % tf == 0 and o_t.shape == (n // tm, d, tm)
    row = pl.BlockSpec((tm, d), lambda i: (i, 0))
    weights = (w_o, g, w_in, w_out)
    return pl.pallas_call(
        functools.partial(_oproj_ffn_kernel, tf=tf), grid=(n // tm,),
        in_specs=[row, pl.BlockSpec((None, d, tm), lambda i: (i, 0, 0))]
        + [_resident(w) for w in weights],
        out_specs=row,
        out_shape=jax.ShapeDtypeStruct((n, d), F32),
        compiler_params=_params(1), name="oproj_ffn")(h2, o_t, *map(_array, weights))


def _ple_update(x_ref, g_ref, wg_ref, p_ref, wu_ref):
    x = x_ref[...]
    u = _rmsnorm(x, g_ref[...]).astype(BF16)
    gate = jax.nn.sigmoid(jnp.dot(u, wg_ref[...], preferred_element_type=F32))
    up = jnp.dot(p_ref[...].astype(BF16), wu_ref[...], preferred_element_type=F32)
    return x + gate * up


def _ple_next_kernel(x_ref, g_ref, wg_ref, p_ref, wu_ref, *refs, n_mixer):
    y = _ple_update(x_ref, g_ref, wg_ref, p_ref, wu_ref)
    refs[n_mixer][...] = y
    _mixer_inputs(y, refs[:n_mixer], refs[n_mixer + 1:])


def _ple_final_kernel(x_ref, g_ref, wg_ref, p_ref, wu_ref, fg_ref, out_ref):
    y = _ple_update(x_ref, g_ref, wg_ref, p_ref, wu_ref)
    out_ref[...] = _rmsnorm(y, fg_ref[...])


def _ple(h2, g, w_gate, p, w_up, *, tm, mixer=None, final_g=None):
    n, d = h2.shape
    pd, layer = p.shape[1], p.layer
    row = pl.BlockSpec((tm, d), lambda i: (i, 0))
    in_specs = [row, _resident(g), _resident(w_gate),
                pl.BlockSpec((None, tm, pd), lambda i: (layer, i, 0)), _resident(w_up)]
    operands = (h2, g, _array(w_gate), p.stack, _array(w_up))
    h_shape = jax.ShapeDtypeStruct((n, d), F32)
    if mixer is None:
        return pl.pallas_call(
            _ple_final_kernel, grid=(n // tm,),
            in_specs=in_specs + [_resident(final_g)], out_specs=row, out_shape=h_shape,
            compiler_params=_params(1), name="ple_final")(*operands, final_g)
    n_mixer = len(mixer.operands())
    return pl.pallas_call(
        functools.partial(_ple_next_kernel, n_mixer=n_mixer), grid=(n // tm,),
        in_specs=in_specs + mixer.in_specs(),
        out_specs=[row] + mixer.out_specs(tm),
        out_shape=[h_shape] + mixer.out_shapes(n, tm),
        compiler_params=_params(1), name="ple_proj")(*operands, *mixer.operands())


def _row_tile(n, want):
    t = min(want, n)
    assert n % t == 0
    return t


def _col_tile(n, want):
    t = min(want, n)
    while n % t:
        t -= LANES
    return t


def kernel(x, p, attn_norm_g, fox_w_in, fox_b_f, fox_w_o, moba_w_in, moba_w_o, rel_bias_table,
           ffn_norm_g, ffn_w_in, ffn_w_out, ple_norm_g, ple_w_gate, ple_w_up, final_norm_g):
    b, s, d = x.shape
    depth = p.shape[0]
    n_heads = rel_bias_table.shape[1]
    assert d == n_heads * HEAD_DIM and n_heads % 2 == 0 and n_heads <= LANES
    n = b * s
    tm = _row_tile(n, DENSE_ROW_TILE)
    tf = _col_tile(ffn_w_out.shape[1], FFN_COL_CHUNK)

    def row_vec(v):
        return v.reshape(1, -1).astype(F32)

    def bf16_stack(w, scaled_cols=0):
        if scaled_cols:
            col = lax.broadcasted_iota(jnp.int32, (1, 1, w.shape[2]), 2)
            w = w * jnp.where(col < scaled_cols, Q_SCALE, 1.0)
        return w.astype(BF16)

    w_in_mix = (bf16_stack(fox_w_in, d), bf16_stack(moba_w_in, d))
    w_o_mix = (bf16_stack(fox_w_o), bf16_stack(moba_w_o))
    w_ffn_in, w_ffn_out = bf16_stack(ffn_w_in), bf16_stack(ffn_w_out)
    w_ple_gate, w_ple_up = bf16_stack(ple_w_gate), bf16_stack(ple_w_up)
    p3 = p.reshape(depth, n, -1)

    def mixer(i):
        parts = [row_vec(attn_norm_g[i]), _Slab(w_in_mix[i % 2], i // 2)]
        if i % 2 == 0:
            w_f = fox_w_in[i // 2][:, 3 * d:]
            parts += [jnp.pad(w_f, ((0, 0), (0, LANES - n_heads))).astype(BF16),
                      jnp.pad(row_vec(fox_b_f[i // 2]), ((0, 0), (0, LANES - n_heads)))]
        return _Mixer(*parts)

    rel_table = rel_bias_table.astype(F32)
    bias_t = _bias_tiles(rel_table, MOBA_BLOCK)

    h = x.reshape(n, d).astype(F32)
    mixed = _project(h, mixer(0), tm=tm)
    for i in range(depth):
        qt, k3, vt = mixed[0], mixed[1].reshape(b, s, d), mixed[2]
        if i % 2 == 0:
            aqt, ak = _cumsum(mixed[3].reshape(b, s, LANES))
            o_t = _fox_attention(qt, k3, vt, aqt, ak, n_heads)
        else:
            o_t = _moba_attention(qt, k3, vt, rel_table, bias_t, n_heads)
        h = _oproj_ffn(h, o_t, _Slab(w_o_mix[i % 2], i // 2), row_vec(ffn_norm_g[i]),
                       _Slab(w_ffn_in, i), _Slab(w_ffn_out, i), tm=tm, tf=tf)
        ple_args = (h, row_vec(ple_norm_g[i]), _Slab(w_ple_gate, i), _Slab(p3, i),
                    _Slab(w_ple_up, i))
        if i + 1 < depth:
            h, *mixed = _ple(*ple_args, tm=tm, mixer=mixer(i + 1))
        else:
            h = _ple(*ple_args, tm=tm, final_g=row_vec(final_norm_g))
    return h.reshape(b, s, d).astype(x.dtype)
```

```python
import functools
import math
from typing import NamedTuple, Optional

import numpy as np
import jax
import jax.numpy as jnp
from jax import lax
from jax.experimental import pallas as pl
from jax.experimental.pallas import tpu as pltpu

F32 = jnp.float32
BF16 = jnp.bfloat16

RMS_EPS = 1e-6
HEAD_DIM = 64
MOBA_BLOCK = 256
MOBA_TOP_K = 3
REL_BUCKETS = 32
REL_MAX_DIST = 128

LANES = 128
F32_SUBLANES = 8
BF16_SUBLANES = 16
V7X_VMEM_BYTES = 64 * 1024 * 1024
V_ROWS = HEAD_DIM + BF16_SUBLANES
SPLIT_TERMS = 3
GATE_LANES = 8
SCORE_PITCH_PAD = LANES
LOG2E = math.log2(math.e)
Q_SCALE = HEAD_DIM ** -0.5 * LOG2E
MOBA_Q_TILE = 2 * MOBA_BLOCK
ATTN_Q_TILE = MOBA_Q_TILE
DENSE_ROW_TILE = ATTN_Q_TILE
CUMSUM_BLK = ATTN_Q_TILE
PROJ_COL_CHUNK = 512
FFN_COL_CHUNK = 256
VMEM_LIMIT_BYTES = V7X_VMEM_BYTES // 8 * 7
NEG_INF = float("-inf")


def _params(n_axes):
    return pltpu.CompilerParams(
        dimension_semantics=("arbitrary",) * n_axes,
        vmem_limit_bytes=VMEM_LIMIT_BYTES)


def _split3(x):
    x1 = x.astype(BF16)
    r1 = x - x1.astype(F32)
    x2 = r1.astype(BF16)
    x3 = (r1 - x2.astype(F32)).astype(BF16)
    return x1, x2, x3


def _rmsnorm(x, g):
    ms = jnp.mean(x * x, axis=-1, keepdims=True)
    return x * lax.rsqrt(ms + RMS_EPS) * g


def _log_sigmoid(x):
    return jnp.minimum(x, 0.0) - jnp.log1p(jnp.exp(-jnp.abs(x)))


class _Slab(NamedTuple):
    stack: jax.Array
    layer: int

    @property
    def shape(self):
        return self.stack.shape[1:]


def _resident(x):
    if isinstance(x, _Slab):
        layer, zeros = x.layer, (0,) * len(x.shape)
        return pl.BlockSpec((None,) + x.shape, lambda i: (layer,) + zeros,
                            pipeline_mode=pl.Buffered(1))
    zeros = (0,) * x.ndim
    return pl.BlockSpec(x.shape, lambda i: zeros, pipeline_mode=pl.Buffered(1))


def _array(x):
    return x.stack if isinstance(x, _Slab) else x


class _Mixer(NamedTuple):
    g: jax.Array
    w_qkv: _Slab
    w_f: Optional[jax.Array] = None
    b_f: Optional[jax.Array] = None

    def operands(self):
        return tuple(_array(a) for a in self if a is not None)

    def in_specs(self):
        return [_resident(a) for a in self if a is not None]

    def out_specs(self, tm):
        d = self.w_qkv.shape[0]
        specs = [pl.BlockSpec((None, d, tm), lambda i: (i, 0, 0)),
                 pl.BlockSpec((tm, d), lambda i: (i, 0)),
                 pl.BlockSpec((d, tm), lambda i: (0, i))]
        if self.w_f is not None:
            specs.append(pl.BlockSpec((tm, LANES), lambda i: (i, 0)))
        return specs

    def out_shapes(self, n, tm):
        d = self.w_qkv.shape[0]
        shapes = [jax.ShapeDtypeStruct((n // tm, d, tm), BF16),
                  jax.ShapeDtypeStruct((n, d), BF16), jax.ShapeDtypeStruct((d, n), BF16)]
        if self.w_f is not None:
            shapes.append(jax.ShapeDtypeStruct((n, LANES), F32))
        return shapes


def _chunks(total, want):
    step = want if total % want == 0 else total
    return [slice(c * step, (c + 1) * step) for c in range(total // step)]


def _mixer_inputs(y, mixer_refs, out_refs):
    g_ref, w_ref = mixer_refs[:2]
    d = w_ref.shape[0]
    u = _rmsnorm(y, g_ref[...]).astype(BF16)
    for first_col, out_ref in ((0, out_refs[0]), (2 * d, out_refs[2])):
        for cols in _chunks(d, PROJ_COL_CHUNK):
            w_cols = w_ref[:, first_col + cols.start:first_col + cols.stop]
            out_ref[cols, :] = lax.dot_general(
                w_cols, u, (((0,), (1,)), ((), ())),
                preferred_element_type=F32).astype(BF16)
    for cols in _chunks(d, PROJ_COL_CHUNK):
        out_refs[1][:, cols] = jnp.dot(u, w_ref[:, d + cols.start:d + cols.stop],
                                       preferred_element_type=F32).astype(BF16)
    if len(mixer_refs) > 2:
        wf_ref, bf_ref = mixer_refs[2:]
        f_logit = jnp.dot(u, wf_ref[...], preferred_element_type=F32) + bf_ref[...]
        out_refs[3][...] = _log_sigmoid(f_logit)


def _proj_kernel(x_ref, *refs, n_mixer):
    _mixer_inputs(x_ref[...], refs[:n_mixer], refs[n_mixer:])


def _project(h2, mixer, *, tm):
    n, d = h2.shape
    n_mixer = len(mixer.operands())
    return pl.pallas_call(
        functools.partial(_proj_kernel, n_mixer=n_mixer), grid=(n // tm,),
        in_specs=[pl.BlockSpec((tm, d), lambda i: (i, 0))] + mixer.in_specs(),
        out_specs=mixer.out_specs(tm), out_shape=mixer.out_shapes(n, tm),
        compiler_params=_params(1), name="proj")(h2, *mixer.operands())


def _cumsum_kernel(lf_ref, aqt_ref, ak_ref, carry_ref):
    @pl.when(pl.program_id(1) == 0)
    def _():
        carry_ref[...] = jnp.zeros_like(carry_ref)

    t = lf_ref.shape[1]
    row = lax.broadcasted_iota(jnp.int32, (t, t), 0)
    col = lax.broadcasted_iota(jnp.int32, (t, t), 1)
    tril = jnp.where(col <= row, 1.0, 0.0).astype(BF16)
    x1, x2, x3 = _split3(lf_ref[0])
    cs = (jnp.dot(tril, x1, preferred_element_type=F32)
          + jnp.dot(tril, x2, preferred_element_type=F32)
          + jnp.dot(tril, x3, preferred_element_type=F32))
    cs = cs + carry_ref[0:1, :]
    carry_ref[...] = jnp.broadcast_to(cs[t - 1:t, :], carry_ref.shape)

    src = lax.broadcasted_iota(jnp.int32, (LANES, LANES), 0)
    dst = lax.broadcasted_iota(jnp.int32, (LANES, LANES), 1)
    lane = lax.broadcasted_iota(jnp.int32, (1, LANES), 1) & (GATE_LANES - 1)
    aq = jnp.where((lane >= SPLIT_TERMS) & (lane < 2 * SPLIT_TERMS), 1.0, 0.0)
    ak = jnp.where(lane < SPLIT_TERMS, 1.0, 0.0)
    for i, part in enumerate(_split3(cs * LOG2E)):
        to_q = jnp.where(dst == GATE_LANES * src + i, 1.0, 0.0).astype(BF16)
        to_k = jnp.where(dst == GATE_LANES * src + SPLIT_TERMS + i, 1.0, 0.0).astype(BF16)
        aq = aq + jnp.dot(part, to_q, preferred_element_type=F32)
        ak = ak - jnp.dot(part, to_k, preferred_element_type=F32)
    eye = jnp.where(src == dst, 1.0, 0.0).astype(BF16)
    aqt_ref[0] = lax.dot_general(eye, aq.astype(BF16), (((1,), (1,)), ((), ())),
                                 preferred_element_type=F32).astype(BF16)
    ak_ref[0] = ak.astype(BF16)


def _cumsum(lf3):
    b, s, _ = lf3.shape
    t = min(CUMSUM_BLK, s)
    nt = s // t
    spec = pl.BlockSpec((1, t, LANES), lambda i, j: (i, j, 0))
    return pl.pallas_call(
        _cumsum_kernel, grid=(b, nt),
        in_specs=[spec],
        out_specs=[pl.BlockSpec((1, LANES, t), lambda i, j: (i * nt + j, 0, 0)), spec],
        out_shape=[jax.ShapeDtypeStruct((b * nt, LANES, t), BF16),
                   jax.ShapeDtypeStruct(lf3.shape, BF16)],
        scratch_shapes=[pltpu.VMEM((F32_SUBLANES, LANES), F32)],
        compiler_params=_params(2), name="gate_cumsum")(lf3)


def _fill_value_rows(vt_ref, v_ref, blk):
    row = lax.broadcasted_iota(jnp.int32, (V_ROWS - HEAD_DIM, blk), 0)
    tail = jnp.where(row == 0, 1.0, 0.0).astype(BF16)
    for hh in range(2):
        for jb in range(v_ref.shape[1] // blk):
            head_rows = v_ref[hh * HEAD_DIM:(hh + 1) * HEAD_DIM, jb * blk:(jb + 1) * blk]
            vt_ref[hh, jb] = jnp.concatenate([head_rows, tail], axis=0)


def _flash_init(m_ref, acc_ref):
    m_ref[...] = jnp.full(m_ref.shape, NEG_INF, F32)
    acc_ref[...] = jnp.zeros(acc_ref.shape, F32)


def _pipelined_blocks(n, scores_into, update, final_scores, final_update, next_tile_scores):
    def step(j_next, j, slot):
        for hh in range(2):
            scores_into(j_next, 1 - slot, hh)
            update(j, slot, hh)

    def last_update_and_final(j, slot):
        for hh in range(2):
            final_scores(hh)
            update(j, slot, hh)
        next_tile_scores()
        final_update()

    def pair(jj, carry):
        j = 2 * jj
        step(j + 1, j, 0)
        step(j + 2, j + 1, 1)
        return carry
    lax.fori_loop(0, jnp.maximum(n - 1, 0) // 2, pair, 0)

    @pl.when(n % 2 == 1)
    def _():
        last_update_and_final(n - 1, 0)

    @pl.when((n % 2 == 0) & (n > 0))
    def _():
        step(n - 1, n - 2, 0)
        last_update_and_final(n - 1, 1)

    @pl.when(n == 0)
    def _():
        for hh in range(2):
            final_scores(hh)
        next_tile_scores()
        final_update()


def _store_scores(s_t, s_ref, mx_ref, idx):
    s_ref[idx + (slice(None), slice(0, s_t.shape[1]))] = s_t
    mx_ref[idx] = jnp.max(s_t, axis=0, keepdims=True)


def _flash_update(s_t, mx, v_rows, m_ref, acc_ref, hh, keep=None, const=None,
                  cols=slice(None)):
    m_old = m_ref[hh, :, cols]
    if const is not None:
        mx = mx + const
    if keep is not None:
        mx = jnp.where(keep, mx, NEG_INF)
    m_new = jnp.maximum(m_old, mx)
    m_safe = jnp.where(m_new == NEG_INF, 0.0, m_new)
    shift = m_safe if const is None else m_safe - const
    if keep is not None:
        shift = jnp.where(keep, shift, float("inf"))
    p = jnp.exp2(s_t - shift).astype(BF16)
    alpha = jnp.exp2(m_old - m_safe)
    m_ref[hh, :, cols] = m_new
    if not isinstance(v_rows, (list, tuple)):
        v_rows = [v_rows]
    keys = p.shape[0] // len(v_rows)
    acc = alpha * acc_ref[hh, :, cols]
    for i, v_i in enumerate(v_rows):
        acc = acc + jnp.dot(v_i, p[i * keys:(i + 1) * keys], preferred_element_type=F32)
    acc_ref[hh, :, cols] = acc


def _head_outputs(acc_ref):
    outs = []
    for hh in range(2):
        acc = acc_ref[hh]
        outs.append(acc[:HEAD_DIM] / acc[HEAD_DIM:HEAD_DIM + 1])
    return jnp.concatenate(outs, axis=0).astype(BF16)


def _for_each_query_tile(nq, tile):
    def body(qi, carry):
        tile(qi)
        return carry
    lax.fori_loop(0, nq, body, 0)


def _head_row_mask(hh):
    row = lax.broadcasted_iota(jnp.int32, (LANES, 1), 0)
    return (row < HEAD_DIM) if hh == 0 else (row >= HEAD_DIM)


def _fox_kernel(qt_ref, k_ref, v_ref, *refs):
    vt_ref = refs[3]
    nq, _, blk = qt_ref.shape
    _fill_value_rows(vt_ref, v_ref, blk)
    hp = pl.program_id(1)
    _for_each_query_tile(nq, lambda qi: _fox_tile(qi, hp, qt_ref, k_ref, *refs))


def _fox_tile(qi, hp, qt_ref, k_ref, aqt_ref, ak_ref, o_ref,
              vt_ref, s_ref, mx_ref, diag_ref, m_ref, acc_ref):
    nq, _, blk = qt_ref.shape
    gate_row = lax.broadcasted_iota(jnp.int32, (LANES, 1), 0)

    def query_operands(tile):
        qt, aqt = qt_ref[tile], aqt_ref[tile]
        out = []
        for hh in range(2):
            first = GATE_LANES * (2 * hp + hh)
            own_gate = (gate_row >= first) & (gate_row < first + GATE_LANES)
            q_rows = jnp.where(_head_row_mask(hh), qt, jnp.zeros_like(qt))
            g_rows = jnp.where(own_gate, aqt, jnp.zeros_like(aqt))
            out.append(jnp.concatenate([q_rows, g_rows], axis=0))
        return out

    def block_scores(w, j, hh):
        rows = pl.ds(pl.multiple_of(j * blk, blk), blk)
        keys = jnp.concatenate([k_ref[0, rows, :], ak_ref[0, rows, :]], axis=1)
        return jnp.dot(keys, w[hh], preferred_element_type=F32)

    w_q = query_operands(qi)

    def scores_into(j, slot, hh):
        _store_scores(block_scores(w_q, j, hh), s_ref, mx_ref, (slot, hh))

    def next_tile_scores():
        w_next = query_operands(jnp.minimum(qi + 1, nq - 1))
        for hh in range(2):
            _store_scores(block_scores(w_next, 0, hh), s_ref, mx_ref, (0, hh))

    def update(j, slot, hh):
        _flash_update(s_ref[slot, hh, :, :blk], mx_ref[slot, hh], vt_ref[hh, j],
                      m_ref, acc_ref, hh)

    def diagonal_scores(hh):
        diag_ref[hh] = block_scores(w_q, qi, hh)

    def diagonal_update():
        half = blk // 2
        lo, hi = slice(0, half), slice(half, blk)
        causal_lo = (lax.broadcasted_iota(jnp.int32, (half, half), 0)
                     <= lax.broadcasted_iota(jnp.int32, (half, half), 1))
        causal_hi = (lax.broadcasted_iota(jnp.int32, (blk, half), 0)
                     <= lax.broadcasted_iota(jnp.int32, (blk, half), 1) + half)
        for hh in range(2):
            v_rows = vt_ref[hh, qi]
            s_lo = jnp.where(causal_lo, diag_ref[hh, lo, lo], NEG_INF)
            _flash_update(s_lo, jnp.max(s_lo, axis=0, keepdims=True),
                          v_rows[:, lo], m_ref, acc_ref, hh, cols=lo)
            s_hi = jnp.where(causal_hi, diag_ref[hh, :, hi], NEG_INF)
            _flash_update(s_hi, jnp.max(s_hi, axis=0, keepdims=True),
                          v_rows, m_ref, acc_ref, hh, cols=hi)

    _flash_init(m_ref, acc_ref)
    _pipelined_blocks(qi, scores_into, update, diagonal_scores, diagonal_update,
                      next_tile_scores)
    o_ref[qi] = _head_outputs(acc_ref)


def _fox_attention(qt3, k3, vt, aqt3, ak3, n_heads):
    b, s, d = k3.shape
    assert n_heads * GATE_LANES <= LANES
    blk = qt3.shape[2]
    assert aqt3.shape[2] == blk and s % blk == 0
    npair = n_heads // 2
    nq = s // blk
    head_pair_tiles = pl.BlockSpec((nq, LANES, blk), lambda bi, hp: (bi, hp, 0))
    return pl.pallas_call(
        _fox_kernel, grid=(b, npair),
        in_specs=[
            head_pair_tiles,
            pl.BlockSpec((1, s, LANES), lambda bi, hp: (bi, 0, hp)),
            pl.BlockSpec((LANES, s), lambda bi, hp: (hp, bi)),
            pl.BlockSpec((nq, LANES, blk), lambda bi, hp: (bi, 0, 0)),
            pl.BlockSpec((1, s, LANES), lambda bi, hp: (bi, 0, 0)),
        ],
        out_specs=head_pair_tiles,
        out_shape=jax.ShapeDtypeStruct(qt3.shape, BF16),
        scratch_shapes=[
            pltpu.VMEM((2, s // blk, V_ROWS, blk), BF16),
            pltpu.VMEM((2, 2, blk, blk + SCORE_PITCH_PAD), F32),
            pltpu.VMEM((2, 2, 1, blk), F32),
            pltpu.VMEM((2, blk, blk), F32),
            pltpu.VMEM((2, 1, blk), F32),
            pltpu.VMEM((2, V_ROWS, blk), F32),
        ],
        compiler_params=_params(2), name="fox_attention")(qt3, k3, vt, aqt3, ak3)


def _t5_bucket_np(n):
    max_exact = REL_BUCKETS // 2
    nf = np.maximum(n, 1).astype(np.float64)
    large = max_exact + (np.log(nf / max_exact) / math.log(REL_MAX_DIST / max_exact)
                         * (REL_BUCKETS - max_exact)).astype(np.int32)
    return np.where(n < max_exact, n, np.minimum(large, REL_BUCKETS - 1)).astype(np.int32)


def _bucket_tiles(blk):
    key = np.arange(blk)[:, None]
    qry = np.arange(blk)[None, :]
    own = np.where(key <= qry, _t5_bucket_np(np.maximum(qry - key, 0)), -1)
    prev = _t5_bucket_np(blk + qry - key)
    return np.stack([own, prev]).astype(np.int32)


def _bias_kernel(tab_ref, bucket_ref, o_ref):
    h = pl.program_id(0)
    bucket = bucket_ref[...]
    acc = jnp.where(bucket < 0, NEG_INF, 0.0).astype(F32)
    for bkt in range(REL_BUCKETS):
        acc = jnp.where(bucket == bkt, tab_ref[bkt, h] * LOG2E, acc)
    o_ref[0] = acc


def _bias_tiles(rel_table, blk):
    n_heads = rel_table.shape[1]
    buckets = jnp.asarray(_bucket_tiles(blk))
    return pl.pallas_call(
        _bias_kernel, grid=(n_heads,),
        in_specs=[pl.BlockSpec(memory_space=pltpu.SMEM),
                  pl.BlockSpec((2, blk, blk), lambda h: (0, 0, 0))],
        out_specs=pl.BlockSpec((1, 2, blk, blk), lambda h: (h, 0, 0, 0)),
        out_shape=jax.ShapeDtypeStruct((n_heads, 2, blk, blk), F32),
        compiler_params=_params(1), name="t5_bias_tiles")(rel_table, buckets)


def _moba_kernel(tab_ref, qt_ref, k_ref, v_ref, bias_ref, o_ref, vt_ref, km_ref, *scratch):
    blk = MOBA_BLOCK

    def block_mean(jb, carry):
        rows = pl.ds(pl.multiple_of(jb * blk, blk), blk)
        km_ref[pl.ds(jb, 1), :] = jnp.mean(k_ref[0, rows, :].astype(F32),
                                            axis=0, keepdims=True)
        return carry
    lax.fori_loop(0, k_ref.shape[1] // blk, block_mean, 0)
    _fill_value_rows(vt_ref, v_ref, blk)
    hp = pl.program_id(1)
    _for_each_query_tile(
        qt_ref.shape[0],
        lambda qi: _moba_tile(qi, hp, tab_ref, qt_ref, k_ref, bias_ref, o_ref,
                              vt_ref, km_ref, *scratch))


def _moba_tile(qi, hp, tab_ref, qt_ref, k_ref, bias_ref, o_ref,
               vt_ref, km_ref, sel_ref, s_ref, mx_ref, near_ref, own1_ref, m_ref, acc_ref):
    blk = MOBA_BLOCK
    nq, _, tq = qt_ref.shape
    nblk = k_ref.shape[1] // blk
    assert tq == 2 * blk and blk >= REL_MAX_DIST
    first_own = 2 * qi

    def head_queries(tile):
        qt = qt_ref[tile]
        return [jnp.where(_head_row_mask(hh), qt, jnp.zeros_like(qt)) for hh in range(2)]

    q_m_t = head_queries(qi)
    km_parts = _split3(km_ref[...])
    blk_id = lax.broadcasted_iota(jnp.int32, (nblk, tq), 0)
    qpos = lax.broadcasted_iota(jnp.int32, (nblk, tq), 1)
    own = first_own + jnp.where(qpos >= blk, 1, 0)
    past = blk_id < own
    for hh in range(2):
        gate = None
        for part in km_parts:
            term = jnp.dot(part, q_m_t[hh], preferred_element_type=F32)
            gate = term if gate is None else gate + term
        work = jnp.where(past, gate, NEG_INF)
        picked = jnp.zeros(gate.shape, F32)
        for _ in range(MOBA_TOP_K):
            best = jnp.max(work, axis=0, keepdims=True)
            first = jnp.min(jnp.where(work == best, blk_id, nblk), axis=0, keepdims=True)
            hit = blk_id == first
            picked = jnp.where(hit, 1.0, picked)
            work = jnp.where(hit, NEG_INF, work)
        keep_all = ((picked > 0.5) & past) | (blk_id == own) | (blk_id == first_own + 1)
        sel_ref[hh] = jnp.where(keep_all, 1.0, 0.0)

    def scores(hh, j, queries=q_m_t):
        rows = pl.ds(pl.multiple_of(j * blk, blk), blk)
        return jnp.dot(k_ref[0, rows, :], queries[hh], preferred_element_type=F32)

    def next_tile_scores():
        q_next = head_queries(jnp.minimum(qi + 1, nq - 1))
        for hh in range(2):
            _store_scores(scores(hh, 0, q_next), s_ref, mx_ref, (0, hh))

    def keep(hh, j):
        return sel_ref[hh, pl.ds(j, 1), :] > 0.5

    _flash_init(m_ref, acc_ref)
    far_bias = [tab_ref[REL_BUCKETS - 1, 2 * hp + hh] * LOG2E for hh in range(2)]

    j_prev = jnp.maximum(first_own - 1, 0)
    lo, hi = slice(0, blk), slice(blk, 2 * blk)
    near_mx = {}

    def near_scores(hh):
        own_t, prev_t = bias_ref[hh, 0], bias_ref[hh, 1]
        mask_prev = jnp.where(keep(hh, j_prev) & (qi >= 1), 0.0, NEG_INF)
        mask_own = jnp.where(keep(hh, first_own), 0.0, NEG_INF)
        far_t = jnp.full((blk, blk), far_bias[hh], F32)
        parts = [scores(hh, j_prev) + mask_prev + jnp.concatenate([prev_t, far_t], axis=1),
                 scores(hh, first_own) + mask_own + jnp.concatenate([own_t, prev_t], axis=1)]
        mx = None
        for i, part in enumerate(parts):
            near_ref[hh, i * blk:(i + 1) * blk] = part
            part_mx = jnp.max(part, axis=0, keepdims=True)
            mx = part_mx if mx is None else jnp.maximum(mx, part_mx)
        rows = pl.ds(pl.multiple_of((first_own + 1) * blk, blk), blk)
        last = jnp.dot(k_ref[0, rows, :], q_m_t[hh][:, hi], preferred_element_type=F32) + own_t
        own1_ref[hh] = last
        near_mx[hh] = (mx[:, lo], jnp.maximum(mx[:, hi], jnp.max(last, axis=0, keepdims=True)))

    def near_update():
        for hh in range(2):
            v_near = [vt_ref[hh, j_prev], vt_ref[hh, first_own], vt_ref[hh, first_own + 1]]
            _flash_update(near_ref[hh, :, lo], near_mx[hh][0], v_near[:2],
                          m_ref, acc_ref, hh, cols=lo)
            _flash_update(jnp.concatenate([near_ref[hh, :, hi], own1_ref[hh]], axis=0),
                          near_mx[hh][1], v_near, m_ref, acc_ref, hh, cols=hi)

    def scores_into(j, slot, hh):
        _store_scores(scores(hh, j), s_ref, mx_ref, (slot, hh))

    def update(j, slot, hh):
        _flash_update(s_ref[slot, hh, :, :tq], mx_ref[slot, hh], vt_ref[hh, j],
                      m_ref, acc_ref, hh, keep=keep(hh, j), const=far_bias[hh])

    _pipelined_blocks(jnp.maximum(first_own - 1, 0), scores_into, update,
                      near_scores, near_update, next_tile_scores)
    o_ref[qi] = _head_outputs(acc_ref)


def _moba_attention(qt3, k3, vt, rel_table, bias_t, n_heads):
    b, s, d = k3.shape
    blk = MOBA_BLOCK
    tq = qt3.shape[2]
    assert tq == MOBA_Q_TILE and s % tq == 0
    npair = n_heads // 2
    nblk = s // blk
    nq = s // tq
    head_pair_tiles = pl.BlockSpec((nq, LANES, tq), lambda bi, hp: (bi, hp, 0))
    return pl.pallas_call(
        _moba_kernel, grid=(b, npair),
        in_specs=[
            pl.BlockSpec(memory_space=pltpu.SMEM),
            head_pair_tiles,
            pl.BlockSpec((1, s, LANES), lambda bi, hp: (bi, 0, hp)),
            pl.BlockSpec((LANES, s), lambda bi, hp: (hp, bi)),
            pl.BlockSpec((2, 2, blk, blk), lambda bi, hp: (hp, 0, 0, 0)),
        ],
        out_specs=head_pair_tiles,
        out_shape=jax.ShapeDtypeStruct(qt3.shape, BF16),
        scratch_shapes=[
            pltpu.VMEM((2, nblk, V_ROWS, blk), BF16),
            pltpu.VMEM((nblk, LANES), F32),
            pltpu.VMEM((2, nblk, tq), F32),
            pltpu.VMEM((2, 2, blk, tq + SCORE_PITCH_PAD), F32),
            pltpu.VMEM((2, 2, 1, tq), F32),
            pltpu.VMEM((2, 2 * blk, tq), F32),
            pltpu.VMEM((2, blk, blk), F32),
            pltpu.VMEM((2, 1, tq), F32),
            pltpu.VMEM((2, V_ROWS, tq), F32),
        ],
        compiler_params=_params(2), name="moba_attention")(
            rel_table, qt3, k3, vt, bias_t)


class _Ple(NamedTuple):
    g: jax.Array
    w_gate: _Slab
    p: _Slab
    w_up: _Slab

    def in_specs(self, tm):
        layer = self.p.layer
        return [_resident(self.g), _resident(self.w_gate),
                pl.BlockSpec((None, tm, self.p.shape[1]), lambda i: (layer, i, 0)),
                _resident(self.w_up)]

    def operands(self):
        return tuple(_array(a) for a in self)


def _ple_update(x, g_ref, wg_ref, p_ref, wu_ref):
    u = _rmsnorm(x, g_ref[...]).astype(BF16)
    gate = jax.nn.sigmoid(jnp.dot(u, wg_ref[...], preferred_element_type=F32))
    up = jnp.dot(p_ref[...].astype(BF16), wu_ref[...], preferred_element_type=F32)
    return x + gate * up


def _oproj_ffn_kernel(h_ref, ot_ref, wo_ref, g_ref, win_ref, wout_ref, *rest, tf):
    d_ff = wout_ref.shape[0]
    h1 = h_ref[...] + lax.dot_general(ot_ref[...], wo_ref[...], (((0,), (0,)), ((), ())),
                                      preferred_element_type=F32)
    u = _rmsnorm(h1, g_ref[...]).astype(BF16)
    acc = h1
    for c in range(d_ff // tf):
        gate = jnp.dot(u, win_ref[:, c * tf:(c + 1) * tf], preferred_element_type=F32)
        up = jnp.dot(u, win_ref[:, d_ff + c * tf:d_ff + (c + 1) * tf],
                     preferred_element_type=F32)
        act = (gate * jax.nn.sigmoid(gate) * up).astype(BF16)
        acc = acc + jnp.dot(act, wout_ref[c * tf:(c + 1) * tf, :],
                            preferred_element_type=F32)
    *ple_refs, out_ref = rest
    if ple_refs:
        *ple_refs, final_g_ref = ple_refs
        acc = _rmsnorm(_ple_update(acc, *ple_refs), final_g_ref[...])
    out_ref[...] = acc


def _oproj_ffn(h2, o_t, w_o, g, w_in, w_out, *, tm, tf, last=None):
    n, d = h2.shape
    d_ff = w_out.shape[0]
    assert d_ff % tf == 0 and o_t.shape == (n // tm, d, tm)
    row = pl.BlockSpec((tm, d), lambda i: (i, 0))
    weights = (w_o, g, w_in, w_out)
    in_specs = ([row, pl.BlockSpec((None, d, tm), lambda i: (i, 0, 0))]
                + [_resident(w) for w in weights])
    operands = (h2, o_t, *map(_array, weights))
    if last is not None:
        ple, final_g = last
        in_specs += ple.in_specs(tm) + [_resident(final_g)]
        operands += ple.operands() + (final_g,)
    return pl.pallas_call(
        functools.partial(_oproj_ffn_kernel, tf=tf), grid=(n // tm,),
        in_specs=in_specs, out_specs=row,
        out_shape=jax.ShapeDtypeStruct((n, d), F32),
        compiler_params=_params(1), name="oproj_ffn")(*operands)


def _ple_next_kernel(x_ref, g_ref, wg_ref, p_ref, wu_ref, *refs, n_mixer):
    y = _ple_update(x_ref[...], g_ref, wg_ref, p_ref, wu_ref)
    refs[n_mixer][...] = y
    _mixer_inputs(y, refs[:n_mixer], refs[n_mixer + 1:])


def _ple_next(h2, ple, mixer, *, tm):
    n, d = h2.shape
    row = pl.BlockSpec((tm, d), lambda i: (i, 0))
    n_mixer = len(mixer.operands())
    return pl.pallas_call(
        functools.partial(_ple_next_kernel, n_mixer=n_mixer), grid=(n // tm,),
        in_specs=[row] + ple.in_specs(tm) + mixer.in_specs(),
        out_specs=[row] + mixer.out_specs(tm),
        out_shape=[jax.ShapeDtypeStruct((n, d), F32)] + mixer.out_shapes(n, tm),
        compiler_params=_params(1), name="ple_proj")(h2, *ple.operands(), *mixer.operands())


def _row_tile(n, want):
    t = min(want, n)
    assert n % t == 0
    return t


def _col_tile(n, want):
    t = min(want, n)
    while n % t:
        t -= LANES
    return t


def kernel(x, p, attn_norm_g, fox_w_in, fox_b_f, fox_w_o, moba_w_in, moba_w_o, rel_bias_table,
           ffn_norm_g, ffn_w_in, ffn_w_out, ple_norm_g, ple_w_gate, ple_w_up, final_norm_g):
    b, s, d = x.shape
    depth = p.shape[0]
    n_heads = rel_bias_table.shape[1]
    assert d == n_heads * HEAD_DIM and n_heads % 2 == 0 and n_heads <= LANES
    n = b * s
    tm = _row_tile(n, DENSE_ROW_TILE)
    tf = _col_tile(ffn_w_out.shape[1], FFN_COL_CHUNK)

    def row_vec(v):
        return v.reshape(1, -1).astype(F32)

    def bf16_stack(w, scaled_cols=0):
        if scaled_cols:
            col = lax.broadcasted_iota(jnp.int32, (1, 1, w.shape[2]), 2)
            w = w * jnp.where(col < scaled_cols, Q_SCALE, 1.0)
        return w.astype(BF16)

    w_in_mix = (bf16_stack(fox_w_in, d), bf16_stack(moba_w_in, d))
    w_o_mix = (bf16_stack(fox_w_o), bf16_stack(moba_w_o))
    w_ffn_in, w_ffn_out = bf16_stack(ffn_w_in), bf16_stack(ffn_w_out)
    w_ple_gate, w_ple_up = bf16_stack(ple_w_gate), bf16_stack(ple_w_up)
    p3 = p.reshape(depth, n, -1)

    def mixer(i):
        parts = [row_vec(attn_norm_g[i]), _Slab(w_in_mix[i % 2], i // 2)]
        if i % 2 == 0:
            w_f = fox_w_in[i // 2][:, 3 * d:]
            parts += [jnp.pad(w_f, ((0, 0), (0, LANES - n_heads))).astype(BF16),
                      jnp.pad(row_vec(fox_b_f[i // 2]), ((0, 0), (0, LANES - n_heads)))]
        return _Mixer(*parts)

    rel_table = rel_bias_table.astype(F32)
    bias_t = _bias_tiles(rel_table, MOBA_BLOCK)

    h = x.reshape(n, d).astype(F32)
    mixed = _project(h, mixer(0), tm=tm)
    for i in range(depth):
        qt, k3, vt = mixed[0], mixed[1].reshape(b, s, d), mixed[2]
        if i % 2 == 0:
            aqt, ak = _cumsum(mixed[3].reshape(b, s, LANES))
            o_t = _fox_attention(qt, k3, vt, aqt, ak, n_heads)
        else:
            o_t = _moba_attention(qt, k3, vt, rel_table, bias_t, n_heads)
        ple = _Ple(row_vec(ple_norm_g[i]), _Slab(w_ple_gate, i), _Slab(p3, i),
                   _Slab(w_ple_up, i))
        last = (ple, row_vec(final_norm_g)) if i + 1 == depth else None
        h = _oproj_ffn(h, o_t, _Slab(w_o_mix[i % 2], i // 2), row_vec(ffn_norm_g[i]),
                       _Slab(w_ffn_in, i), _Slab(w_ffn_out, i), tm=tm, tf=tf, last=last)
        if last is None:
            h, *mixed = _ple_next(h, ple, mixer(i + 1), tm=tm)
    return h.reshape(b, s, d).astype(x.dtype)
```

```python
import functools
import math
from typing import NamedTuple, Optional

import numpy as np
import jax
import jax.numpy as jnp
from jax import lax
from jax.experimental import pallas as pl
from jax.experimental.pallas import tpu as pltpu

F32 = jnp.float32
BF16 = jnp.bfloat16

RMS_EPS = 1e-6
HEAD_DIM = 64
MOBA_BLOCK = 256
MOBA_TOP_K = 3
REL_BUCKETS = 32
REL_MAX_DIST = 128

LANES = 128
F32_SUBLANES = 8
BF16_SUBLANES = 16
V7X_VMEM_BYTES = 64 * 1024 * 1024
V_ROWS = HEAD_DIM + BF16_SUBLANES
SPLIT_TERMS = 3
GATE_LANES = 8
SCORE_PITCH_PAD = LANES
LOG2E = math.log2(math.e)
Q_SCALE = HEAD_DIM ** -0.5 * LOG2E
MOBA_Q_TILE = 2 * MOBA_BLOCK
ATTN_Q_TILE = MOBA_Q_TILE
DENSE_ROW_TILE = ATTN_Q_TILE
CUMSUM_BLK = ATTN_Q_TILE
PROJ_COL_CHUNK = 512
FFN_COL_CHUNK = 256
VMEM_LIMIT_BYTES = V7X_VMEM_BYTES // 8 * 7
NEG_INF = float("-inf")


def _params(n_axes):
    return pltpu.CompilerParams(
        dimension_semantics=("arbitrary",) * n_axes,
        vmem_limit_bytes=VMEM_LIMIT_BYTES)


def _split3(x):
    x1 = x.astype(BF16)
    r1 = x - x1.astype(F32)
    x2 = r1.astype(BF16)
    x3 = (r1 - x2.astype(F32)).astype(BF16)
    return x1, x2, x3


def _rmsnorm(x, g):
    ms = jnp.mean(x * x, axis=-1, keepdims=True)
    return x * lax.rsqrt(ms + RMS_EPS) * g


def _log_sigmoid(x):
    return jnp.minimum(x, 0.0) - jnp.log1p(jnp.exp(-jnp.abs(x)))


class _Slab(NamedTuple):
    stack: jax.Array
    layer: int

    @property
    def shape(self):
        return self.stack.shape[1:]


def _resident(x):
    if isinstance(x, _Slab):
        layer, zeros = x.layer, (0,) * len(x.shape)
        return pl.BlockSpec((None,) + x.shape, lambda i: (layer,) + zeros,
                            pipeline_mode=pl.Buffered(1))
    zeros = (0,) * x.ndim
    return pl.BlockSpec(x.shape, lambda i: zeros, pipeline_mode=pl.Buffered(1))


def _array(x):
    return x.stack if isinstance(x, _Slab) else x


class _Mixer(NamedTuple):
    g: jax.Array
    w_qkv: _Slab
    w_f: Optional[jax.Array] = None
    b_f: Optional[jax.Array] = None

    def operands(self):
        return tuple(_array(a) for a in self if a is not None)

    def in_specs(self):
        return [_resident(a) for a in self if a is not None]

    def out_specs(self, tm):
        d = self.w_qkv.shape[0]
        specs = [pl.BlockSpec((None, d, tm), lambda i: (i, 0, 0)),
                 pl.BlockSpec((tm, d), lambda i: (i, 0)),
                 pl.BlockSpec((d, tm), lambda i: (0, i))]
        if self.w_f is not None:
            specs.append(pl.BlockSpec((tm, LANES), lambda i: (i, 0)))
        return specs

    def out_shapes(self, n, tm):
        d = self.w_qkv.shape[0]
        shapes = [jax.ShapeDtypeStruct((n // tm, d, tm), BF16),
                  jax.ShapeDtypeStruct((n, d), BF16), jax.ShapeDtypeStruct((d, n), BF16)]
        if self.w_f is not None:
            shapes.append(jax.ShapeDtypeStruct((n, LANES), F32))
        return shapes


def _chunks(total, want):
    step = want if total % want == 0 else total
    return [slice(c * step, (c + 1) * step) for c in range(total // step)]


def _mixer_inputs(y, mixer_refs, out_refs):
    g_ref, w_ref = mixer_refs[:2]
    d = w_ref.shape[0]
    u = _rmsnorm(y, g_ref[...]).astype(BF16)
    for first_col, out_ref in ((0, out_refs[0]), (2 * d, out_refs[2])):
        for cols in _chunks(d, PROJ_COL_CHUNK):
            w_cols = w_ref[:, first_col + cols.start:first_col + cols.stop]
            out_ref[cols, :] = lax.dot_general(
                w_cols, u, (((0,), (1,)), ((), ())),
                preferred_element_type=F32).astype(BF16)
    for cols in _chunks(d, PROJ_COL_CHUNK):
        out_refs[1][:, cols] = jnp.dot(u, w_ref[:, d + cols.start:d + cols.stop],
                                       preferred_element_type=F32).astype(BF16)
    if len(mixer_refs) > 2:
        wf_ref, bf_ref = mixer_refs[2:]
        f_logit = jnp.dot(u, wf_ref[...], preferred_element_type=F32) + bf_ref[...]
        out_refs[3][...] = _log_sigmoid(f_logit)


def _proj_kernel(x_ref, *refs, n_mixer):
    _mixer_inputs(x_ref[...], refs[:n_mixer], refs[n_mixer:])


def _project(h2, mixer, *, tm):
    n, d = h2.shape
    n_mixer = len(mixer.operands())
    return pl.pallas_call(
        functools.partial(_proj_kernel, n_mixer=n_mixer), grid=(n // tm,),
        in_specs=[pl.BlockSpec((tm, d), lambda i: (i, 0))] + mixer.in_specs(),
        out_specs=mixer.out_specs(tm), out_shape=mixer.out_shapes(n, tm),
        compiler_params=_params(1), name="proj")(h2, *mixer.operands())


def _cumsum_kernel(lf_ref, aqt_ref, ak_ref, carry_ref):
    @pl.when(pl.program_id(1) == 0)
    def _():
        carry_ref[...] = jnp.zeros_like(carry_ref)

    t = lf_ref.shape[1]
    row = lax.broadcasted_iota(jnp.int32, (t, t), 0)
    col = lax.broadcasted_iota(jnp.int32, (t, t), 1)
    tril = jnp.where(col <= row, 1.0, 0.0).astype(BF16)
    x1, x2, x3 = _split3(lf_ref[0])
    cs = (jnp.dot(tril, x1, preferred_element_type=F32)
          + jnp.dot(tril, x2, preferred_element_type=F32)
          + jnp.dot(tril, x3, preferred_element_type=F32))
    cs = cs + carry_ref[0:1, :]
    carry_ref[...] = jnp.broadcast_to(cs[t - 1:t, :], carry_ref.shape)

    src = lax.broadcasted_iota(jnp.int32, (LANES, LANES), 0)
    dst = lax.broadcasted_iota(jnp.int32, (LANES, LANES), 1)
    lane = lax.broadcasted_iota(jnp.int32, (1, LANES), 1) & (GATE_LANES - 1)
    aq = jnp.where((lane >= SPLIT_TERMS) & (lane < 2 * SPLIT_TERMS), 1.0, 0.0)
    ak = jnp.where(lane < SPLIT_TERMS, 1.0, 0.0)
    for i, part in enumerate(_split3(cs * LOG2E)):
        to_q = jnp.where(dst == GATE_LANES * src + i, 1.0, 0.0).astype(BF16)
        to_k = jnp.where(dst == GATE_LANES * src + SPLIT_TERMS + i, 1.0, 0.0).astype(BF16)
        aq = aq + jnp.dot(part, to_q, preferred_element_type=F32)
        ak = ak - jnp.dot(part, to_k, preferred_element_type=F32)
    eye = jnp.where(src == dst, 1.0, 0.0).astype(BF16)
    aqt_ref[0] = lax.dot_general(eye, aq.astype(BF16), (((1,), (1,)), ((), ())),
                                 preferred_element_type=F32).astype(BF16)
    ak_ref[0] = ak.astype(BF16)


def _cumsum(lf3):
    b, s, _ = lf3.shape
    t = min(CUMSUM_BLK, s)
    nt = s // t
    spec = pl.BlockSpec((1, t, LANES), lambda i, j: (i, j, 0))
    return pl.pallas_call(
        _cumsum_kernel, grid=(b, nt),
        in_specs=[spec],
        out_specs=[pl.BlockSpec((1, LANES, t), lambda i, j: (i * nt + j, 0, 0)), spec],
        out_shape=[jax.ShapeDtypeStruct((b * nt, LANES, t), BF16),
                   jax.ShapeDtypeStruct(lf3.shape, BF16)],
        scratch_shapes=[pltpu.VMEM((F32_SUBLANES, LANES), F32)],
        compiler_params=_params(2), name="gate_cumsum")(lf3)


def _fill_value_rows(vt_ref, v_ref, blk):
    row = lax.broadcasted_iota(jnp.int32, (V_ROWS - HEAD_DIM, blk), 0)
    tail = jnp.where(row == 0, 1.0, 0.0).astype(BF16)
    for hh in range(2):
        for jb in range(v_ref.shape[1] // blk):
            head_rows = v_ref[hh * HEAD_DIM:(hh + 1) * HEAD_DIM, jb * blk:(jb + 1) * blk]
            vt_ref[hh, jb] = jnp.concatenate([head_rows, tail], axis=0)


def _flash_init(m_ref, acc_ref):
    m_ref[...] = jnp.full(m_ref.shape, NEG_INF, F32)
    acc_ref[...] = jnp.zeros(acc_ref.shape, F32)


def _pipelined_blocks(n, scores_into, update, final_scores, final_update, next_tile_scores):
    def step(j_next, j, slot):
        for hh in range(2):
            scores_into(j_next, 1 - slot, hh)
            update(j, slot, hh)

    def last_update_and_final(j, slot):
        for hh in range(2):
            final_scores(hh)
            update(j, slot, hh)
        next_tile_scores()
        final_update()

    def pair(jj, carry):
        j = 2 * jj
        step(j + 1, j, 0)
        step(j + 2, j + 1, 1)
        return carry
    lax.fori_loop(0, jnp.maximum(n - 1, 0) // 2, pair, 0)

    @pl.when(n % 2 == 1)
    def _():
        last_update_and_final(n - 1, 0)

    @pl.when((n % 2 == 0) & (n > 0))
    def _():
        step(n - 1, n - 2, 0)
        last_update_and_final(n - 1, 1)

    @pl.when(n == 0)
    def _():
        for hh in range(2):
            final_scores(hh)
        next_tile_scores()
        final_update()


def _store_scores(s_t, s_ref, mx_ref, idx):
    s_ref[idx + (slice(None), slice(0, s_t.shape[1]))] = s_t
    mx_ref[idx] = jnp.max(s_t, axis=0, keepdims=True)


def _flash_update(s_t, mx, v_rows, m_ref, acc_ref, hh, keep=None, const=None,
                  cols=slice(None)):
    m_old = m_ref[hh, :, cols]
    if const is not None:
        mx = mx + const
    if keep is not None:
        mx = jnp.where(keep, mx, NEG_INF)
    m_new = jnp.maximum(m_old, mx)
    m_safe = jnp.where(m_new == NEG_INF, 0.0, m_new)
    shift = m_safe if const is None else m_safe - const
    if keep is not None:
        shift = jnp.where(keep, shift, float("inf"))
    p = jnp.exp2(s_t - shift).astype(BF16)
    alpha = jnp.exp2(m_old - m_safe)
    m_ref[hh, :, cols] = m_new
    if not isinstance(v_rows, (list, tuple)):
        v_rows = [v_rows]
    keys = p.shape[0] // len(v_rows)
    acc = alpha * acc_ref[hh, :, cols]
    for i, v_i in enumerate(v_rows):
        acc = acc + jnp.dot(v_i, p[i * keys:(i + 1) * keys], preferred_element_type=F32)
    acc_ref[hh, :, cols] = acc


def _head_outputs(acc_ref):
    outs = []
    for hh in range(2):
        acc = acc_ref[hh]
        outs.append(acc[:HEAD_DIM] / acc[HEAD_DIM:HEAD_DIM + 1])
    return jnp.concatenate(outs, axis=0).astype(BF16)


def _for_each_query_tile(nq, tile):
    def body(qi, carry):
        tile(qi)
        return carry
    lax.fori_loop(0, nq, body, 0)


def _head_row_mask(hh):
    row = lax.broadcasted_iota(jnp.int32, (LANES, 1), 0)
    return (row < HEAD_DIM) if hh == 0 else (row >= HEAD_DIM)


def _fox_kernel(qt_ref, k_ref, v_ref, *refs):
    vt_ref = refs[3]
    nq, _, blk = qt_ref.shape
    _fill_value_rows(vt_ref, v_ref, blk)
    hp = pl.program_id(1)
    _for_each_query_tile(nq, lambda qi: _fox_tile(qi, hp, qt_ref, k_ref, *refs))


def _fox_tile(qi, hp, qt_ref, k_ref, aqt_ref, ak_ref, o_ref,
              vt_ref, s_ref, mx_ref, diag_ref, m_ref, acc_ref):
    nq, _, blk = qt_ref.shape
    gate_row = lax.broadcasted_iota(jnp.int32, (LANES, 1), 0)

    def query_operands(tile):
        qt, aqt = qt_ref[tile], aqt_ref[tile]
        out = []
        for hh in range(2):
            first = GATE_LANES * (2 * hp + hh)
            own_gate = (gate_row >= first) & (gate_row < first + GATE_LANES)
            q_rows = jnp.where(_head_row_mask(hh), qt, jnp.zeros_like(qt))
            g_rows = jnp.where(own_gate, aqt, jnp.zeros_like(aqt))
            out.append(jnp.concatenate([q_rows, g_rows], axis=0))
        return out

    def block_scores(w, j, hh):
        rows = pl.ds(pl.multiple_of(j * blk, blk), blk)
        keys = jnp.concatenate([k_ref[0, rows, :], ak_ref[0, rows, :]], axis=1)
        return jnp.dot(keys, w[hh], preferred_element_type=F32)

    w_q = query_operands(qi)

    def scores_into(j, slot, hh):
        _store_scores(block_scores(w_q, j, hh), s_ref, mx_ref, (slot, hh))

    def next_tile_scores():
        w_next = query_operands(jnp.minimum(qi + 1, nq - 1))
        for hh in range(2):
            _store_scores(block_scores(w_next, 0, hh), s_ref, mx_ref, (0, hh))

    def update(j, slot, hh):
        _flash_update(s_ref[slot, hh, :, :blk], mx_ref[slot, hh], vt_ref[hh, j],
                      m_ref, acc_ref, hh)

    def diagonal_scores(hh):
        diag_ref[hh] = block_scores(w_q, qi, hh)

    def diagonal_update():
        half = blk // 2
        lo, hi = slice(0, half), slice(half, blk)
        causal_lo = (lax.broadcasted_iota(jnp.int32, (half, half), 0)
                     <= lax.broadcasted_iota(jnp.int32, (half, half), 1))
        causal_hi = (lax.broadcasted_iota(jnp.int32, (blk, half), 0)
                     <= lax.broadcasted_iota(jnp.int32, (blk, half), 1) + half)
        for hh in range(2):
            v_rows = vt_ref[hh, qi]
            s_lo = jnp.where(causal_lo, diag_ref[hh, lo, lo], NEG_INF)
            _flash_update(s_lo, jnp.max(s_lo, axis=0, keepdims=True),
                          v_rows[:, lo], m_ref, acc_ref, hh, cols=lo)
            s_hi = jnp.where(causal_hi, diag_ref[hh, :, hi], NEG_INF)
            _flash_update(s_hi, jnp.max(s_hi, axis=0, keepdims=True),
                          v_rows, m_ref, acc_ref, hh, cols=hi)

    _flash_init(m_ref, acc_ref)
    _pipelined_blocks(qi, scores_into, update, diagonal_scores, diagonal_update,
                      next_tile_scores)
    o_ref[qi] = _head_outputs(acc_ref)


def _fox_attention(qt3, k3, vt, aqt3, ak3, n_heads):
    b, s, d = k3.shape
    assert n_heads * GATE_LANES <= LANES
    blk = qt3.shape[2]
    assert aqt3.shape[2] == blk and s % blk == 0
    npair = n_heads // 2
    nq = s // blk
    head_pair_tiles = pl.BlockSpec((nq, LANES, blk), lambda bi, hp: (bi, hp, 0))
    return pl.pallas_call(
        _fox_kernel, grid=(b, npair),
        in_specs=[
            head_pair_tiles,
            pl.BlockSpec((1, s, LANES), lambda bi, hp: (bi, 0, hp)),
            pl.BlockSpec((LANES, s), lambda bi, hp: (hp, bi)),
            pl.BlockSpec((nq, LANES, blk), lambda bi, hp: (bi, 0, 0)),
            pl.BlockSpec((1, s, LANES), lambda bi, hp: (bi, 0, 0)),
        ],
        out_specs=head_pair_tiles,
        out_shape=jax.ShapeDtypeStruct(qt3.shape, BF16),
        scratch_shapes=[
            pltpu.VMEM((2, s // blk, V_ROWS, blk), BF16),
            pltpu.VMEM((2, 2, blk, blk + SCORE_PITCH_PAD), F32),
            pltpu.VMEM((2, 2, 1, blk), F32),
            pltpu.VMEM((2, blk, blk), F32),
            pltpu.VMEM((2, 1, blk), F32),
            pltpu.VMEM((2, V_ROWS, blk), F32),
        ],
        compiler_params=_params(2), name="fox_attention")(qt3, k3, vt, aqt3, ak3)


def _t5_bucket_np(n):
    max_exact = REL_BUCKETS // 2
    nf = np.maximum(n, 1).astype(np.float64)
    large = max_exact + (np.log(nf / max_exact) / math.log(REL_MAX_DIST / max_exact)
                         * (REL_BUCKETS - max_exact)).astype(np.int32)
    return np.where(n < max_exact, n, np.minimum(large, REL_BUCKETS - 1)).astype(np.int32)


def _bucket_tiles(blk):
    key = np.arange(blk)[:, None]
    qry = np.arange(blk)[None, :]
    own = np.where(key <= qry, _t5_bucket_np(np.maximum(qry - key, 0)), -1)
    prev = _t5_bucket_np(blk + qry - key)
    return np.stack([own, prev]).astype(np.int32)


def _bias_kernel(tab_ref, bucket_ref, o_ref):
    h = pl.program_id(0)
    bucket = bucket_ref[...]
    acc = jnp.where(bucket < 0, NEG_INF, 0.0).astype(F32)
    for bkt in range(REL_BUCKETS):
        acc = jnp.where(bucket == bkt, tab_ref[bkt, h] * LOG2E, acc)
    o_ref[0] = acc


def _bias_tiles(rel_table, blk):
    n_heads = rel_table.shape[1]
    buckets = jnp.asarray(_bucket_tiles(blk))
    return pl.pallas_call(
        _bias_kernel, grid=(n_heads,),
        in_specs=[pl.BlockSpec(memory_space=pltpu.SMEM),
                  pl.BlockSpec((2, blk, blk), lambda h: (0, 0, 0))],
        out_specs=pl.BlockSpec((1, 2, blk, blk), lambda h: (h, 0, 0, 0)),
        out_shape=jax.ShapeDtypeStruct((n_heads, 2, blk, blk), F32),
        compiler_params=_params(1), name="t5_bias_tiles")(rel_table, buckets)


def _select_blocks(qt_ref, km_ref, sel_ref):
    nq, _, tq = qt_ref.shape
    blk = MOBA_BLOCK
    nblk = km_ref.shape[0]
    assert blk & (blk - 1) == 0
    qt_all = jnp.concatenate([qt_ref[t] for t in range(nq)], axis=1)
    km_parts = _split3(km_ref[...])
    blk_id = lax.broadcasted_iota(jnp.int32, (nblk, nq * tq), 0)
    own = lax.shift_right_logical(lax.broadcasted_iota(jnp.int32, (nblk, nq * tq), 1),
                                  blk.bit_length() - 1)
    past = blk_id < own
    for hh in range(2):
        q_h = jnp.where(_head_row_mask(hh), qt_all, jnp.zeros_like(qt_all))
        gate = None
        for part in km_parts:
            term = jnp.dot(part, q_h, preferred_element_type=F32)
            gate = term if gate is None else gate + term
        work = jnp.where(past, gate, NEG_INF)
        picked = jnp.zeros(gate.shape, F32)
        for _ in range(MOBA_TOP_K):
            best = jnp.max(work, axis=0, keepdims=True)
            first = jnp.min(jnp.where(work == best, blk_id, nblk), axis=0, keepdims=True)
            hit = blk_id == first
            picked = jnp.where(hit, 1.0, picked)
            work = jnp.where(hit, NEG_INF, work)
        keep_all = ((picked > 0.5) & past) | (blk_id == own) | (blk_id == (own | 1))
        keep_f = jnp.where(keep_all, 1.0, 0.0)
        for t in range(nq):
            sel_ref[hh, t] = keep_f[:, t * tq:(t + 1) * tq]


def _moba_kernel(tab_ref, qt_ref, k_ref, v_ref, bias_ref, o_ref,
                 vt_ref, km_ref, sel_ref, *scratch):
    blk = MOBA_BLOCK

    def block_mean(jb, carry):
        rows = pl.ds(pl.multiple_of(jb * blk, blk), blk)
        km_ref[pl.ds(jb, 1), :] = jnp.mean(k_ref[0, rows, :].astype(F32),
                                            axis=0, keepdims=True)
        return carry
    lax.fori_loop(0, k_ref.shape[1] // blk, block_mean, 0)
    _fill_value_rows(vt_ref, v_ref, blk)
    _select_blocks(qt_ref, km_ref, sel_ref)
    hp = pl.program_id(1)
    _for_each_query_tile(
        qt_ref.shape[0],
        lambda qi: _moba_tile(qi, hp, tab_ref, qt_ref, k_ref, bias_ref, o_ref,
                              vt_ref, sel_ref, *scratch))


def _moba_tile(qi, hp, tab_ref, qt_ref, k_ref, bias_ref, o_ref,
               vt_ref, sel_ref, s_ref, mx_ref, near_ref, own1_ref, m_ref, acc_ref):
    blk = MOBA_BLOCK
    nq, _, tq = qt_ref.shape
    assert tq == 2 * blk and blk >= REL_MAX_DIST
    first_own = 2 * qi

    def head_queries(tile):
        qt = qt_ref[tile]
        return [jnp.where(_head_row_mask(hh), qt, jnp.zeros_like(qt)) for hh in range(2)]

    q_m_t = head_queries(qi)

    def scores(hh, j, queries=q_m_t):
        rows = pl.ds(pl.multiple_of(j * blk, blk), blk)
        return jnp.dot(k_ref[0, rows, :], queries[hh], preferred_element_type=F32)

    def next_tile_scores():
        q_next = head_queries(jnp.minimum(qi + 1, nq - 1))
        for hh in range(2):
            _store_scores(scores(hh, 0, q_next), s_ref, mx_ref, (0, hh))

    def keep(hh, j):
        return sel_ref[hh, qi, pl.ds(j, 1), :] > 0.5

    _flash_init(m_ref, acc_ref)
    far_bias = [tab_ref[REL_BUCKETS - 1, 2 * hp + hh] * LOG2E for hh in range(2)]

    j_prev = jnp.maximum(first_own - 1, 0)
    lo, hi = slice(0, blk), slice(blk, 2 * blk)
    near_mx = {}

    def near_scores(hh):
        own_t, prev_t = bias_ref[hh, 0], bias_ref[hh, 1]
        mask_prev = jnp.where(keep(hh, j_prev) & (qi >= 1), 0.0, NEG_INF)
        mask_own = jnp.where(keep(hh, first_own), 0.0, NEG_INF)
        far_t = jnp.full((blk, blk), far_bias[hh], F32)
        parts = [scores(hh, j_prev) + mask_prev + jnp.concatenate([prev_t, far_t], axis=1),
                 scores(hh, first_own) + mask_own + jnp.concatenate([own_t, prev_t], axis=1)]
        mx = None
        for i, part in enumerate(parts):
            near_ref[hh, i * blk:(i + 1) * blk] = part
            part_mx = jnp.max(part, axis=0, keepdims=True)
            mx = part_mx if mx is None else jnp.maximum(mx, part_mx)
        rows = pl.ds(pl.multiple_of((first_own + 1) * blk, blk), blk)
        last = jnp.dot(k_ref[0, rows, :], q_m_t[hh][:, hi], preferred_element_type=F32) + own_t
        own1_ref[hh] = last
        near_mx[hh] = (mx[:, lo], jnp.maximum(mx[:, hi], jnp.max(last, axis=0, keepdims=True)))

    def near_update():
        for hh in range(2):
            v_near = [vt_ref[hh, j_prev], vt_ref[hh, first_own], vt_ref[hh, first_own + 1]]
            _flash_update(near_ref[hh, :, lo], near_mx[hh][0], v_near[:2],
                          m_ref, acc_ref, hh, cols=lo)
            _flash_update(jnp.concatenate([near_ref[hh, :, hi], own1_ref[hh]], axis=0),
                          near_mx[hh][1], v_near, m_ref, acc_ref, hh, cols=hi)

    def scores_into(j, slot, hh):
        _store_scores(scores(hh, j), s_ref, mx_ref, (slot, hh))

    def update(j, slot, hh):
        _flash_update(s_ref[slot, hh, :, :tq], mx_ref[slot, hh], vt_ref[hh, j],
                      m_ref, acc_ref, hh, keep=keep(hh, j), const=far_bias[hh])

    _pipelined_blocks(jnp.maximum(first_own - 1, 0), scores_into, update,
                      near_scores, near_update, next_tile_scores)
    o_ref[qi] = _head_outputs(acc_ref)


def _moba_attention(qt3, k3, vt, rel_table, bias_t, n_heads):
    b, s, d = k3.shape
    blk = MOBA_BLOCK
    tq = qt3.shape[2]
    assert tq == MOBA_Q_TILE and s % tq == 0
    npair = n_heads // 2
    nblk = s // blk
    nq = s // tq
    head_pair_tiles = pl.BlockSpec((nq, LANES, tq), lambda bi, hp: (bi, hp, 0))
    return pl.pallas_call(
        _moba_kernel, grid=(b, npair),
        in_specs=[
            pl.BlockSpec(memory_space=pltpu.SMEM),
            head_pair_tiles,
            pl.BlockSpec((1, s, LANES), lambda bi, hp: (bi, 0, hp)),
            pl.BlockSpec((LANES, s), lambda bi, hp: (hp, bi)),
            pl.BlockSpec((2, 2, blk, blk), lambda bi, hp: (hp, 0, 0, 0)),
        ],
        out_specs=head_pair_tiles,
        out_shape=jax.ShapeDtypeStruct(qt3.shape, BF16),
        scratch_shapes=[
            pltpu.VMEM((2, nblk, V_ROWS, blk), BF16),
            pltpu.VMEM((nblk, LANES), F32),
            pltpu.VMEM((2, nq, nblk, tq), F32),
            pltpu.VMEM((2, 2, blk, tq + SCORE_PITCH_PAD), F32),
            pltpu.VMEM((2, 2, 1, tq), F32),
            pltpu.VMEM((2, 2 * blk, tq), F32),
            pltpu.VMEM((2, blk, blk), F32),
            pltpu.VMEM((2, 1, tq), F32),
            pltpu.VMEM((2, V_ROWS, tq), F32),
        ],
        compiler_params=_params(2), name="moba_attention")(
            rel_table, qt3, k3, vt, bias_t)


class _Ple(NamedTuple):
    g: jax.Array
    w_gate: _Slab
    p: _Slab
    w_up: _Slab

    def in_specs(self, tm):
        layer = self.p.layer
        return [_resident(self.g), _resident(self.w_gate),
                pl.BlockSpec((None, tm, self.p.shape[1]), lambda i: (layer, i, 0)),
                _resident(self.w_up)]

    def operands(self):
        return tuple(_array(a) for a in self)


def _ple_update(x, g_ref, wg_ref, p_ref, wu_ref):
    u = _rmsnorm(x, g_ref[...]).astype(BF16)
    gate = jax.nn.sigmoid(jnp.dot(u, wg_ref[...], preferred_element_type=F32))
    up = jnp.dot(p_ref[...].astype(BF16), wu_ref[...], preferred_element_type=F32)
    return x + gate * up


def _oproj_ffn_kernel(h_ref, ot_ref, wo_ref, g_ref, win_ref, wout_ref, *rest, tf):
    d_ff = wout_ref.shape[0]
    h1 = h_ref[...] + lax.dot_general(ot_ref[...], wo_ref[...], (((0,), (0,)), ((), ())),
                                      preferred_element_type=F32)
    u = _rmsnorm(h1, g_ref[...]).astype(BF16)
    acc = h1
    for c in range(d_ff // tf):
        gate = jnp.dot(u, win_ref[:, c * tf:(c + 1) * tf], preferred_element_type=F32)
        up = jnp.dot(u, win_ref[:, d_ff + c * tf:d_ff + (c + 1) * tf],
                     preferred_element_type=F32)
        act = (gate * jax.nn.sigmoid(gate) * up).astype(BF16)
        acc = acc + jnp.dot(act, wout_ref[c * tf:(c + 1) * tf, :],
                            preferred_element_type=F32)
    *ple_refs, out_ref = rest
    if ple_refs:
        *ple_refs, final_g_ref = ple_refs
        acc = _rmsnorm(_ple_update(acc, *ple_refs), final_g_ref[...])
    out_ref[...] = acc


def _oproj_ffn(h2, o_t, w_o, g, w_in, w_out, *, tm, tf, last=None):
    n, d = h2.shape
    d_ff = w_out.shape[0]
    assert d_ff % tf == 0 and o_t.shape == (n // tm, d, tm)
    row = pl.BlockSpec((tm, d), lambda i: (i, 0))
    weights = (w_o, g, w_in, w_out)
    in_specs = ([row, pl.BlockSpec((None, d, tm), lambda i: (i, 0, 0))]
                + [_resident(w) for w in weights])
    operands = (h2, o_t, *map(_array, weights))
    if last is not None:
        ple, final_g = last
        in_specs += ple.in_specs(tm) + [_resident(final_g)]
        operands += ple.operands() + (final_g,)
    return pl.pallas_call(
        functools.partial(_oproj_ffn_kernel, tf=tf), grid=(n // tm,),
        in_specs=in_specs, out_specs=row,
        out_shape=jax.ShapeDtypeStruct((n, d), F32),
        compiler_params=_params(1), name="oproj_ffn")(*operands)


def _ple_next_kernel(x_ref, g_ref, wg_ref, p_ref, wu_ref, *refs, n_mixer):
    y = _ple_update(x_ref[...], g_ref, wg_ref, p_ref, wu_ref)
    refs[n_mixer][...] = y
    _mixer_inputs(y, refs[:n_mixer], refs[n_mixer + 1:])


def _ple_next(h2, ple, mixer, *, tm):
    n, d = h2.shape
    row = pl.BlockSpec((tm, d), lambda i: (i, 0))
    n_mixer = len(mixer.operands())
    return pl.pallas_call(
        functools.partial(_ple_next_kernel, n_mixer=n_mixer), grid=(n // tm,),
        in_specs=[row] + ple.in_specs(tm) + mixer.in_specs(),
        out_specs=[row] + mixer.out_specs(tm),
        out_shape=[jax.ShapeDtypeStruct((n, d), F32)] + mixer.out_shapes(n, tm),
        compiler_params=_params(1), name="ple_proj")(h2, *ple.operands(), *mixer.operands())


def _row_tile(n, want):
    t = min(want, n)
    assert n % t == 0
    return t


def _col_tile(n, want):
    t = min(want, n)
    while n % t:
        t -= LANES
    return t


def kernel(x, p, attn_norm_g, fox_w_in, fox_b_f, fox_w_o, moba_w_in, moba_w_o, rel_bias_table,
           ffn_norm_g, ffn_w_in, ffn_w_out, ple_norm_g, ple_w_gate, ple_w_up, final_norm_g):
    b, s, d = x.shape
    depth = p.shape[0]
    n_heads = rel_bias_table.shape[1]
    assert d == n_heads * HEAD_DIM and n_heads % 2 == 0 and n_heads <= LANES
    n = b * s
    tm = _row_tile(n, DENSE_ROW_TILE)
    tf = _col_tile(ffn_w_out.shape[1], FFN_COL_CHUNK)

    def row_vec(v):
        return v.reshape(1, -1).astype(F32)

    def bf16_stack(w, scaled_cols=0):
        if scaled_cols:
            col = lax.broadcasted_iota(jnp.int32, (1, 1, w.shape[2]), 2)
            w = w * jnp.where(col < scaled_cols, Q_SCALE, 1.0)
        return w.astype(BF16)

    w_in_mix = (bf16_stack(fox_w_in, d), bf16_stack(moba_w_in, d))
    w_o_mix = (bf16_stack(fox_w_o), bf16_stack(moba_w_o))
    w_ffn_in, w_ffn_out = bf16_stack(ffn_w_in), bf16_stack(ffn_w_out)
    w_ple_gate, w_ple_up = bf16_stack(ple_w_gate), bf16_stack(ple_w_up)
    p3 = p.reshape(depth, n, -1)

    def mixer(i):
        parts = [row_vec(attn_norm_g[i]), _Slab(w_in_mix[i % 2], i // 2)]
        if i % 2 == 0:
            w_f = fox_w_in[i // 2][:, 3 * d:]
            parts += [jnp.pad(w_f, ((0, 0), (0, LANES - n_heads))).astype(BF16),
                      jnp.pad(row_vec(fox_b_f[i // 2]), ((0, 0), (0, LANES - n_heads)))]
        return _Mixer(*parts)

    rel_table = rel_bias_table.astype(F32)
    bias_t = _bias_tiles(rel_table, MOBA_BLOCK)

    h = x.reshape(n, d).astype(F32)
    mixed = _project(h, mixer(0), tm=tm)
    for i in range(depth):
        qt, k3, vt = mixed[0], mixed[1].reshape(b, s, d), mixed[2]
        if i % 2 == 0:
            aqt, ak = _cumsum(mixed[3].reshape(b, s, LANES))
            o_t = _fox_attention(qt, k3, vt, aqt, ak, n_heads)
        else:
            o_t = _moba_attention(qt, k3, vt, rel_table, bias_t, n_heads)
        ple = _Ple(row_vec(ple_norm_g[i]), _Slab(w_ple_gate, i), _Slab(p3, i),
                   _Slab(w_ple_up, i))
        last = (ple, row_vec(final_norm_g)) if i + 1 == depth else None
        h = _oproj_ffn(h, o_t, _Slab(w_o_mix[i % 2], i // 2), row_vec(ffn_norm_g[i]),
                       _Slab(w_ffn_in, i), _Slab(w_ffn_out, i), tm=tm, tf=tf, last=last)
        if last is None:
            h, *mixed = _ple_next(h, ple, mixer(i + 1), tm=tm)
    return h.reshape(b, s, d).astype(x.dtype)
```

```python
import functools
import math
from typing import NamedTuple, Optional

import numpy as np
import jax
import jax.numpy as jnp
from jax import lax
from jax.experimental import pallas as pl
from jax.experimental.pallas import tpu as pltpu

F32 = jnp.float32
BF16 = jnp.bfloat16

RMS_EPS = 1e-6
HEAD_DIM = 64
MOBA_BLOCK = 256
MOBA_TOP_K = 3
REL_BUCKETS = 32
REL_MAX_DIST = 128

LANES = 128
F32_SUBLANES = 8
BF16_SUBLANES = 16
V7X_VMEM_BYTES = 64 * 1024 * 1024
V_ROWS = HEAD_DIM + BF16_SUBLANES
SPLIT_TERMS = 3
GATE_LANES = 8
SCORE_PITCH_PAD = LANES
LOG2E = math.log2(math.e)
Q_SCALE = HEAD_DIM ** -0.5 * LOG2E
MOBA_Q_TILE = 2 * MOBA_BLOCK
ATTN_Q_TILE = MOBA_Q_TILE
DENSE_ROW_TILE = ATTN_Q_TILE
CUMSUM_BLK = ATTN_Q_TILE
PROJ_COL_CHUNK = 512
FFN_COL_CHUNK = 256
VMEM_LIMIT_BYTES = V7X_VMEM_BYTES // 8 * 7
NEG_INF = float("-inf")


def _params(n_axes):
    return pltpu.CompilerParams(
        dimension_semantics=("arbitrary",) * n_axes,
        vmem_limit_bytes=VMEM_LIMIT_BYTES)


def _split3(x):
    x1 = x.astype(BF16)
    r1 = x - x1.astype(F32)
    x2 = r1.astype(BF16)
    x3 = (r1 - x2.astype(F32)).astype(BF16)
    return x1, x2, x3


def _rmsnorm(x, g):
    ms = jnp.mean(x * x, axis=-1, keepdims=True)
    return x * lax.rsqrt(ms + RMS_EPS) * g


def _log_sigmoid(x):
    return jnp.minimum(x, 0.0) - jnp.log1p(jnp.exp(-jnp.abs(x)))


class _Slab(NamedTuple):
    stack: jax.Array
    layer: int

    @property
    def shape(self):
        return self.stack.shape[1:]


def _resident(x):
    if isinstance(x, _Slab):
        layer, zeros = x.layer, (0,) * len(x.shape)
        return pl.BlockSpec((None,) + x.shape, lambda i: (layer,) + zeros,
                            pipeline_mode=pl.Buffered(1))
    zeros = (0,) * x.ndim
    return pl.BlockSpec(x.shape, lambda i: zeros, pipeline_mode=pl.Buffered(1))


def _array(x):
    return x.stack if isinstance(x, _Slab) else x


class _Cast(NamedTuple):
    w: _Slab
    cols: int
    scaled_cols: int = 0

    def specs(self, steps):
        rows = self.w.shape[0]
        assert rows % BF16_SUBLANES == 0 and self.cols % LANES == 0
        chunks = math.gcd(steps, rows // BF16_SUBLANES)
        span, layer = steps // chunks, self.w.layer
        return (pl.BlockSpec((None, rows // chunks, self.cols),
                             lambda i: (layer, i // span, 0)),
                pl.BlockSpec((rows // chunks, self.cols), lambda i: (i // span, 0)),
                jax.ShapeDtypeStruct((rows, self.cols), BF16))


def _cast_specs(casts, steps):
    return [list(part) for part in zip(*(c.specs(steps) for c in casts))] or [[], [], []]


def _cast_rows(scaled_cols, in_refs, out_refs):
    for scaled, in_ref, out_ref in zip(scaled_cols, in_refs, out_refs):
        if scaled:
            out_ref[:, :scaled] = (in_ref[:, :scaled] * Q_SCALE).astype(BF16)
        out_ref[:, scaled:] = in_ref[:, scaled:].astype(BF16)


class _Mixer(NamedTuple):
    g: jax.Array
    w_qkv: jax.Array
    w_f: Optional[jax.Array] = None
    b_f: Optional[jax.Array] = None

    def operands(self):
        return tuple(_array(a) for a in self if a is not None)

    def in_specs(self):
        return [_resident(a) for a in self if a is not None]

    def out_specs(self, tm):
        d = self.w_qkv.shape[0]
        specs = [pl.BlockSpec((None, d, tm), lambda i: (i, 0, 0)),
                 pl.BlockSpec((tm, d), lambda i: (i, 0)),
                 pl.BlockSpec((d, tm), lambda i: (0, i))]
        if self.w_f is not None:
            specs.append(pl.BlockSpec((tm, LANES), lambda i: (i, 0)))
        return specs

    def out_shapes(self, n, tm):
        d = self.w_qkv.shape[0]
        shapes = [jax.ShapeDtypeStruct((n // tm, d, tm), BF16),
                  jax.ShapeDtypeStruct((n, d), BF16), jax.ShapeDtypeStruct((d, n), BF16)]
        if self.w_f is not None:
            shapes.append(jax.ShapeDtypeStruct((n, LANES), F32))
        return shapes


def _chunks(total, want):
    step = want if total % want == 0 else total
    return [slice(c * step, (c + 1) * step) for c in range(total // step)]


def _mixer_inputs(y, mixer_refs, out_refs):
    g_ref, w_ref = mixer_refs[:2]
    d = w_ref.shape[0]
    u = _rmsnorm(y, g_ref[...]).astype(BF16)
    for first_col, out_ref in ((0, out_refs[0]), (2 * d, out_refs[2])):
        for cols in _chunks(d, PROJ_COL_CHUNK):
            w_cols = w_ref[:, first_col + cols.start:first_col + cols.stop]
            out_ref[cols, :] = lax.dot_general(
                w_cols, u, (((0,), (1,)), ((), ())),
                preferred_element_type=F32).astype(BF16)
    for cols in _chunks(d, PROJ_COL_CHUNK):
        out_refs[1][:, cols] = jnp.dot(u, w_ref[:, d + cols.start:d + cols.stop],
                                       preferred_element_type=F32).astype(BF16)
    if len(mixer_refs) > 2:
        wf_ref, bf_ref = mixer_refs[2:]
        f_logit = jnp.dot(u, wf_ref[...], preferred_element_type=F32) + bf_ref[...]
        out_refs[3][...] = _log_sigmoid(f_logit)


def _split_refs(refs, n_mixer, n_cast):
    n_out = len(refs) - n_cast
    return (refs[:n_mixer], refs[n_mixer:n_mixer + n_cast],
            refs[n_mixer + n_cast:n_out], refs[n_out:])


def _proj_kernel(x_ref, *refs, n_mixer, scaled_cols):
    mixer_refs, cast_in, out_refs, cast_out = _split_refs(refs, n_mixer, len(scaled_cols))
    _mixer_inputs(x_ref[...], mixer_refs, out_refs)
    _cast_rows(scaled_cols, cast_in, cast_out)


def _project(h2, mixer, casts, *, tm):
    n, d = h2.shape
    n_mixer = len(mixer.operands())
    cast_in, cast_out, cast_shapes = _cast_specs(casts, n // tm)
    outs = pl.pallas_call(
        functools.partial(_proj_kernel, n_mixer=n_mixer,
                          scaled_cols=tuple(c.scaled_cols for c in casts)),
        grid=(n // tm,),
        in_specs=[pl.BlockSpec((tm, d), lambda i: (i, 0))] + mixer.in_specs() + cast_in,
        out_specs=mixer.out_specs(tm) + cast_out,
        out_shape=mixer.out_shapes(n, tm) + cast_shapes,
        compiler_params=_params(1), name="proj")(
            h2, *mixer.operands(), *(c.w.stack for c in casts))
    return outs[:len(outs) - len(casts)], outs[len(outs) - len(casts):]


def _cumsum_kernel(lf_ref, aqt_ref, ak_ref, carry_ref):
    @pl.when(pl.program_id(1) == 0)
    def _():
        carry_ref[...] = jnp.zeros_like(carry_ref)

    t = lf_ref.shape[1]
    row = lax.broadcasted_iota(jnp.int32, (t, t), 0)
    col = lax.broadcasted_iota(jnp.int32, (t, t), 1)
    tril = jnp.where(col <= row, 1.0, 0.0).astype(BF16)
    x1, x2, x3 = _split3(lf_ref[0])
    cs = (jnp.dot(tril, x1, preferred_element_type=F32)
          + jnp.dot(tril, x2, preferred_element_type=F32)
          + jnp.dot(tril, x3, preferred_element_type=F32))
    cs = cs + carry_ref[0:1, :]
    carry_ref[...] = jnp.broadcast_to(cs[t - 1:t, :], carry_ref.shape)

    src = lax.broadcasted_iota(jnp.int32, (LANES, LANES), 0)
    dst = lax.broadcasted_iota(jnp.int32, (LANES, LANES), 1)
    lane = lax.broadcasted_iota(jnp.int32, (1, LANES), 1) & (GATE_LANES - 1)
    aq = jnp.where((lane >= SPLIT_TERMS) & (lane < 2 * SPLIT_TERMS), 1.0, 0.0)
    ak = jnp.where(lane < SPLIT_TERMS, 1.0, 0.0)
    for i, part in enumerate(_split3(cs * LOG2E)):
        to_q = jnp.where(dst == GATE_LANES * src + i, 1.0, 0.0).astype(BF16)
        to_k = jnp.where(dst == GATE_LANES * src + SPLIT_TERMS + i, 1.0, 0.0).astype(BF16)
        aq = aq + jnp.dot(part, to_q, preferred_element_type=F32)
        ak = ak - jnp.dot(part, to_k, preferred_element_type=F32)
    eye = jnp.where(src == dst, 1.0, 0.0).astype(BF16)
    aqt_ref[0] = lax.dot_general(eye, aq.astype(BF16), (((1,), (1,)), ((), ())),
                                 preferred_element_type=F32).astype(BF16)
    ak_ref[0] = ak.astype(BF16)


def _cumsum(lf3):
    b, s, _ = lf3.shape
    t = min(CUMSUM_BLK, s)
    nt = s // t
    spec = pl.BlockSpec((1, t, LANES), lambda i, j: (i, j, 0))
    return pl.pallas_call(
        _cumsum_kernel, grid=(b, nt),
        in_specs=[spec],
        out_specs=[pl.BlockSpec((1, LANES, t), lambda i, j: (i * nt + j, 0, 0)), spec],
        out_shape=[jax.ShapeDtypeStruct((b * nt, LANES, t), BF16),
                   jax.ShapeDtypeStruct(lf3.shape, BF16)],
        scratch_shapes=[pltpu.VMEM((F32_SUBLANES, LANES), F32)],
        compiler_params=_params(2), name="gate_cumsum")(lf3)


def _fill_value_rows(vt_ref, v_ref, blk):
    row = lax.broadcasted_iota(jnp.int32, (V_ROWS - HEAD_DIM, blk), 0)
    tail = jnp.where(row == 0, 1.0, 0.0).astype(BF16)
    for hh in range(2):
        for jb in range(v_ref.shape[1] // blk):
            head_rows = v_ref[hh * HEAD_DIM:(hh + 1) * HEAD_DIM, jb * blk:(jb + 1) * blk]
            vt_ref[hh, jb] = jnp.concatenate([head_rows, tail], axis=0)


def _flash_init(m_ref, acc_ref):
    m_ref[...] = jnp.full(m_ref.shape, NEG_INF, F32)
    acc_ref[...] = jnp.zeros(acc_ref.shape, F32)


def _pipelined_blocks(n, scores_into, update, final_scores, final_update, next_tile_scores):
    def step(j_next, j, slot):
        for hh in range(2):
            scores_into(j_next, 1 - slot, hh)
            update(j, slot, hh)

    def last_update_and_final(j, slot):
        for hh in range(2):
            final_scores(hh)
            update(j, slot, hh)
        next_tile_scores()
        final_update()

    def pair(jj, carry):
        j = 2 * jj
        step(j + 1, j, 0)
        step(j + 2, j + 1, 1)
        return carry
    lax.fori_loop(0, jnp.maximum(n - 1, 0) // 2, pair, 0)

    @pl.when(n % 2 == 1)
    def _():
        last_update_and_final(n - 1, 0)

    @pl.when((n % 2 == 0) & (n > 0))
    def _():
        step(n - 1, n - 2, 0)
        last_update_and_final(n - 1, 1)

    @pl.when(n == 0)
    def _():
        for hh in range(2):
            final_scores(hh)
        next_tile_scores()
        final_update()


def _store_scores(s_t, s_ref, mx_ref, idx):
    s_ref[idx + (slice(None), slice(0, s_t.shape[1]))] = s_t
    mx_ref[idx] = jnp.max(s_t, axis=0, keepdims=True)


def _flash_update(s_t, mx, v_rows, m_ref, acc_ref, hh, keep=None, const=None,
                  cols=slice(None)):
    m_old = m_ref[hh, :, cols]
    if const is not None:
        mx = mx + const
    if keep is not None:
        mx = jnp.where(keep, mx, NEG_INF)
    m_new = jnp.maximum(m_old, mx)
    m_safe = jnp.where(m_new == NEG_INF, 0.0, m_new)
    shift = m_safe if const is None else m_safe - const
    if keep is not None:
        shift = jnp.where(keep, shift, float("inf"))
    p = jnp.exp2(s_t - shift).astype(BF16)
    alpha = jnp.exp2(m_old - m_safe)
    m_ref[hh, :, cols] = m_new
    if not isinstance(v_rows, (list, tuple)):
        v_rows = [v_rows]
    keys = p.shape[0] // len(v_rows)
    acc = alpha * acc_ref[hh, :, cols]
    for i, v_i in enumerate(v_rows):
        acc = acc + jnp.dot(v_i, p[i * keys:(i + 1) * keys], preferred_element_type=F32)
    acc_ref[hh, :, cols] = acc


def _head_outputs(acc_ref):
    outs = []
    for hh in range(2):
        acc = acc_ref[hh]
        outs.append(acc[:HEAD_DIM] / acc[HEAD_DIM:HEAD_DIM + 1])
    return jnp.concatenate(outs, axis=0).astype(BF16)


def _for_each_query_tile(nq, tile):
    def body(qi, carry):
        tile(qi)
        return carry
    lax.fori_loop(0, nq, body, 0)


def _head_row_mask(hh):
    row = lax.broadcasted_iota(jnp.int32, (LANES, 1), 0)
    return (row < HEAD_DIM) if hh == 0 else (row >= HEAD_DIM)


def _fox_kernel(qt_ref, k_ref, v_ref, *refs):
    vt_ref = refs[3]
    nq, _, blk = qt_ref.shape
    _fill_value_rows(vt_ref, v_ref, blk)
    hp = pl.program_id(1)
    _for_each_query_tile(nq, lambda qi: _fox_tile(qi, hp, qt_ref, k_ref, *refs))


def _fox_tile(qi, hp, qt_ref, k_ref, aqt_ref, ak_ref, o_ref,
              vt_ref, s_ref, mx_ref, diag_ref, m_ref, acc_ref):
    nq, _, blk = qt_ref.shape
    gate_row = lax.broadcasted_iota(jnp.int32, (LANES, 1), 0)

    def query_operands(tile):
        qt, aqt = qt_ref[tile], aqt_ref[tile]
        out = []
        for hh in range(2):
            first = GATE_LANES * (2 * hp + hh)
            own_gate = (gate_row >= first) & (gate_row < first + GATE_LANES)
            q_rows = jnp.where(_head_row_mask(hh), qt, jnp.zeros_like(qt))
            g_rows = jnp.where(own_gate, aqt, jnp.zeros_like(aqt))
            out.append(jnp.concatenate([q_rows, g_rows], axis=0))
        return out

    def block_scores(w, j, hh):
        rows = pl.ds(pl.multiple_of(j * blk, blk), blk)
        keys = jnp.concatenate([k_ref[0, rows, :], ak_ref[0, rows, :]], axis=1)
        return jnp.dot(keys, w[hh], preferred_element_type=F32)

    w_q = query_operands(qi)

    def scores_into(j, slot, hh):
        _store_scores(block_scores(w_q, j, hh), s_ref, mx_ref, (slot, hh))

    def next_tile_scores():
        w_next = query_operands(jnp.minimum(qi + 1, nq - 1))
        for hh in range(2):
            _store_scores(block_scores(w_next, 0, hh), s_ref, mx_ref, (0, hh))

    def update(j, slot, hh):
        _flash_update(s_ref[slot, hh, :, :blk], mx_ref[slot, hh], vt_ref[hh, j],
                      m_ref, acc_ref, hh)

    def diagonal_scores(hh):
        diag_ref[hh] = block_scores(w_q, qi, hh)

    def diagonal_update():
        half = blk // 2
        lo, hi = slice(0, half), slice(half, blk)
        causal_lo = (lax.broadcasted_iota(jnp.int32, (half, half), 0)
                     <= lax.broadcasted_iota(jnp.int32, (half, half), 1))
        causal_hi = (lax.broadcasted_iota(jnp.int32, (blk, half), 0)
                     <= lax.broadcasted_iota(jnp.int32, (blk, half), 1) + half)
        for hh in range(2):
            v_rows = vt_ref[hh, qi]
            s_lo = jnp.where(causal_lo, diag_ref[hh, lo, lo], NEG_INF)
            _flash_update(s_lo, jnp.max(s_lo, axis=0, keepdims=True),
                          v_rows[:, lo], m_ref, acc_ref, hh, cols=lo)
            s_hi = jnp.where(causal_hi, diag_ref[hh, :, hi], NEG_INF)
            _flash_update(s_hi, jnp.max(s_hi, axis=0, keepdims=True),
                          v_rows, m_ref, acc_ref, hh, cols=hi)

    _flash_init(m_ref, acc_ref)
    _pipelined_blocks(qi, scores_into, update, diagonal_scores, diagonal_update,
                      next_tile_scores)
    o_ref[qi] = _head_outputs(acc_ref)


def _fox_attention(qt3, k3, vt, aqt3, ak3, n_heads):
    b, s, d = k3.shape
    assert n_heads * GATE_LANES <= LANES
    blk = qt3.shape[2]
    assert aqt3.shape[2] == blk and s % blk == 0
    npair = n_heads // 2
    nq = s // blk
    head_pair_tiles = pl.BlockSpec((nq, LANES, blk), lambda bi, hp: (bi, hp, 0))
    return pl.pallas_call(
        _fox_kernel, grid=(b, npair),
        in_specs=[
            head_pair_tiles,
            pl.BlockSpec((1, s, LANES), lambda bi, hp: (bi, 0, hp)),
            pl.BlockSpec((LANES, s), lambda bi, hp: (hp, bi)),
            pl.BlockSpec((nq, LANES, blk), lambda bi, hp: (bi, 0, 0)),
            pl.BlockSpec((1, s, LANES), lambda bi, hp: (bi, 0, 0)),
        ],
        out_specs=head_pair_tiles,
        out_shape=jax.ShapeDtypeStruct(qt3.shape, BF16),
        scratch_shapes=[
            pltpu.VMEM((2, s // blk, V_ROWS, blk), BF16),
            pltpu.VMEM((2, 2, blk, blk + SCORE_PITCH_PAD), F32),
            pltpu.VMEM((2, 2, 1, blk), F32),
            pltpu.VMEM((2, blk, blk), F32),
            pltpu.VMEM((2, 1, blk), F32),
            pltpu.VMEM((2, V_ROWS, blk), F32),
        ],
        compiler_params=_params(2), name="fox_attention")(qt3, k3, vt, aqt3, ak3)


def _t5_bucket_np(n):
    max_exact = REL_BUCKETS // 2
    nf = np.maximum(n, 1).astype(np.float64)
    large = max_exact + (np.log(nf / max_exact) / math.log(REL_MAX_DIST / max_exact)
                         * (REL_BUCKETS - max_exact)).astype(np.int32)
    return np.where(n < max_exact, n, np.minimum(large, REL_BUCKETS - 1)).astype(np.int32)


def _bucket_tiles(blk):
    key = np.arange(blk)[:, None]
    qry = np.arange(blk)[None, :]
    own = np.where(key <= qry, _t5_bucket_np(np.maximum(qry - key, 0)), -1)
    prev = _t5_bucket_np(blk + qry - key)
    return np.stack([own, prev]).astype(np.int32)


def _bias_kernel(tab_ref, bucket_ref, o_ref):
    h = pl.program_id(0)
    bucket = bucket_ref[...]
    acc = jnp.where(bucket < 0, NEG_INF, 0.0).astype(F32)
    for bkt in range(REL_BUCKETS):
        acc = jnp.where(bucket == bkt, tab_ref[bkt, h] * LOG2E, acc)
    o_ref[0] = acc


def _bias_tiles(rel_table, blk):
    n_heads = rel_table.shape[1]
    buckets = jnp.asarray(_bucket_tiles(blk))
    return pl.pallas_call(
        _bias_kernel, grid=(n_heads,),
        in_specs=[pl.BlockSpec(memory_space=pltpu.SMEM),
                  pl.BlockSpec((2, blk, blk), lambda h: (0, 0, 0))],
        out_specs=pl.BlockSpec((1, 2, blk, blk), lambda h: (h, 0, 0, 0)),
        out_shape=jax.ShapeDtypeStruct((n_heads, 2, blk, blk), F32),
        compiler_params=_params(1), name="t5_bias_tiles")(rel_table, buckets)


def _select_blocks(qt_ref, km_ref, sel_ref):
    nq, _, tq = qt_ref.shape
    blk = MOBA_BLOCK
    nblk = km_ref.shape[0]
    assert blk & (blk - 1) == 0
    qt_all = jnp.concatenate([qt_ref[t] for t in range(nq)], axis=1)
    km_parts = _split3(km_ref[...])
    blk_id = lax.broadcasted_iota(jnp.int32, (nblk, nq * tq), 0)
    own = lax.shift_right_logical(lax.broadcasted_iota(jnp.int32, (nblk, nq * tq), 1),
                                  blk.bit_length() - 1)
    past = blk_id < own
    for hh in range(2):
        q_h = jnp.where(_head_row_mask(hh), qt_all, jnp.zeros_like(qt_all))
        gate = None
        for part in km_parts:
            term = jnp.dot(part, q_h, preferred_element_type=F32)
            gate = term if gate is None else gate + term
        work = jnp.where(past, gate, NEG_INF)
        picked = jnp.zeros(gate.shape, F32)
        for _ in range(MOBA_TOP_K):
            best = jnp.max(work, axis=0, keepdims=True)
            first = jnp.min(jnp.where(work == best, blk_id, nblk), axis=0, keepdims=True)
            hit = blk_id == first
            picked = jnp.where(hit, 1.0, picked)
            work = jnp.where(hit, NEG_INF, work)
        keep_all = ((picked > 0.5) & past) | (blk_id == own) | (blk_id == (own | 1))
        keep_f = jnp.where(keep_all, 1.0, 0.0)
        for t in range(nq):
            sel_ref[hh, t] = keep_f[:, t * tq:(t + 1) * tq]


def _moba_kernel(tab_ref, qt_ref, k_ref, v_ref, bias_ref, o_ref,
                 vt_ref, km_ref, sel_ref, *scratch):
    blk = MOBA_BLOCK

    def block_mean(jb, carry):
        rows = pl.ds(pl.multiple_of(jb * blk, blk), blk)
        km_ref[pl.ds(jb, 1), :] = jnp.mean(k_ref[0, rows, :].astype(F32),
                                            axis=0, keepdims=True)
        return carry
    lax.fori_loop(0, k_ref.shape[1] // blk, block_mean, 0)
    _fill_value_rows(vt_ref, v_ref, blk)
    _select_blocks(qt_ref, km_ref, sel_ref)
    hp = pl.program_id(1)
    _for_each_query_tile(
        qt_ref.shape[0],
        lambda qi: _moba_tile(qi, hp, tab_ref, qt_ref, k_ref, bias_ref, o_ref,
                              vt_ref, sel_ref, *scratch))


def _moba_tile(qi, hp, tab_ref, qt_ref, k_ref, bias_ref, o_ref,
               vt_ref, sel_ref, s_ref, mx_ref, near_ref, own1_ref, m_ref, acc_ref):
    blk = MOBA_BLOCK
    nq, _, tq = qt_ref.shape
    assert tq == 2 * blk and blk >= REL_MAX_DIST
    first_own = 2 * qi

    def head_queries(tile):
        qt = qt_ref[tile]
        return [jnp.where(_head_row_mask(hh), qt, jnp.zeros_like(qt)) for hh in range(2)]

    q_m_t = head_queries(qi)

    def scores(hh, j, queries=q_m_t):
        rows = pl.ds(pl.multiple_of(j * blk, blk), blk)
        return jnp.dot(k_ref[0, rows, :], queries[hh], preferred_element_type=F32)

    def next_tile_scores():
        q_next = head_queries(jnp.minimum(qi + 1, nq - 1))
        for hh in range(2):
            _store_scores(scores(hh, 0, q_next), s_ref, mx_ref, (0, hh))

    def keep(hh, j):
        return sel_ref[hh, qi, pl.ds(j, 1), :] > 0.5

    _flash_init(m_ref, acc_ref)
    far_bias = [tab_ref[REL_BUCKETS - 1, 2 * hp + hh] * LOG2E for hh in range(2)]

    j_prev = jnp.maximum(first_own - 1, 0)
    lo, hi = slice(0, blk), slice(blk, 2 * blk)
    near_mx = {}

    def near_scores(hh):
        own_t, prev_t = bias_ref[hh, 0], bias_ref[hh, 1]
        mask_prev = jnp.where(keep(hh, j_prev) & (qi >= 1), 0.0, NEG_INF)
        mask_own = jnp.where(keep(hh, first_own), 0.0, NEG_INF)
        far_t = jnp.full((blk, blk), far_bias[hh], F32)
        parts = [scores(hh, j_prev) + mask_prev + jnp.concatenate([prev_t, far_t], axis=1),
                 scores(hh, first_own) + mask_own + jnp.concatenate([own_t, prev_t], axis=1)]
        mx = None
        for i, part in enumerate(parts):
            near_ref[hh, i * blk:(i + 1) * blk] = part
            part_mx = jnp.max(part, axis=0, keepdims=True)
            mx = part_mx if mx is None else jnp.maximum(mx, part_mx)
        rows = pl.ds(pl.multiple_of((first_own + 1) * blk, blk), blk)
        last = jnp.dot(k_ref[0, rows, :], q_m_t[hh][:, hi], preferred_element_type=F32) + own_t
        own1_ref[hh] = last
        near_mx[hh] = (mx[:, lo], jnp.maximum(mx[:, hi], jnp.max(last, axis=0, keepdims=True)))

    def near_update():
        for hh in range(2):
            v_near = [vt_ref[hh, j_prev], vt_ref[hh, first_own], vt_ref[hh, first_own + 1]]
            _flash_update(near_ref[hh, :, lo], near_mx[hh][0], v_near[:2],
                          m_ref, acc_ref, hh, cols=lo)
            _flash_update(jnp.concatenate([near_ref[hh, :, hi], own1_ref[hh]], axis=0),
                          near_mx[hh][1], v_near, m_ref, acc_ref, hh, cols=hi)

    def scores_into(j, slot, hh):
        _store_scores(scores(hh, j), s_ref, mx_ref, (slot, hh))

    def update(j, slot, hh):
        _flash_update(s_ref[slot, hh, :, :tq], mx_ref[slot, hh], vt_ref[hh, j],
                      m_ref, acc_ref, hh, keep=keep(hh, j), const=far_bias[hh])

    _pipelined_blocks(jnp.maximum(first_own - 1, 0), scores_into, update,
                      near_scores, near_update, next_tile_scores)
    o_ref[qi] = _head_outputs(acc_ref)


def _moba_attention(qt3, k3, vt, rel_table, bias_t, n_heads):
    b, s, d = k3.shape
    blk = MOBA_BLOCK
    tq = qt3.shape[2]
    assert tq == MOBA_Q_TILE and s % tq == 0
    npair = n_heads // 2
    nblk = s // blk
    nq = s // tq
    head_pair_tiles = pl.BlockSpec((nq, LANES, tq), lambda bi, hp: (bi, hp, 0))
    return pl.pallas_call(
        _moba_kernel, grid=(b, npair),
        in_specs=[
            pl.BlockSpec(memory_space=pltpu.SMEM),
            head_pair_tiles,
            pl.BlockSpec((1, s, LANES), lambda bi, hp: (bi, 0, hp)),
            pl.BlockSpec((LANES, s), lambda bi, hp: (hp, bi)),
            pl.BlockSpec((2, 2, blk, blk), lambda bi, hp: (hp, 0, 0, 0)),
        ],
        out_specs=head_pair_tiles,
        out_shape=jax.ShapeDtypeStruct(qt3.shape, BF16),
        scratch_shapes=[
            pltpu.VMEM((2, nblk, V_ROWS, blk), BF16),
            pltpu.VMEM((nblk, LANES), F32),
            pltpu.VMEM((2, nq, nblk, tq), F32),
            pltpu.VMEM((2, 2, blk, tq + SCORE_PITCH_PAD), F32),
            pltpu.VMEM((2, 2, 1, tq), F32),
            pltpu.VMEM((2, 2 * blk, tq), F32),
            pltpu.VMEM((2, blk, blk), F32),
            pltpu.VMEM((2, 1, tq), F32),
            pltpu.VMEM((2, V_ROWS, tq), F32),
        ],
        compiler_params=_params(2), name="moba_attention")(
            rel_table, qt3, k3, vt, bias_t)


class _Ple(NamedTuple):
    g: jax.Array
    w_gate: jax.Array
    p: _Slab
    w_up: jax.Array

    def in_specs(self, tm):
        layer = self.p.layer
        return [_resident(self.g), _resident(self.w_gate),
                pl.BlockSpec((None, tm, self.p.shape[1]), lambda i: (layer, i, 0)),
                _resident(self.w_up)]

    def operands(self):
        return tuple(_array(a) for a in self)


def _ple_update(x, g_ref, wg_ref, p_ref, wu_ref):
    u = _rmsnorm(x, g_ref[...]).astype(BF16)
    gate = jax.nn.sigmoid(jnp.dot(u, wg_ref[...], preferred_element_type=F32))
    up = jnp.dot(p_ref[...].astype(BF16), wu_ref[...], preferred_element_type=F32)
    return x + gate * up


def _oproj_ffn_kernel(h_ref, ot_ref, wo_ref, g_ref, win_ref, wout_ref, *rest, tf):
    d_ff = wout_ref.shape[0]
    h1 = h_ref[...] + lax.dot_general(ot_ref[...], wo_ref[...], (((0,), (0,)), ((), ())),
                                      preferred_element_type=F32)
    u = _rmsnorm(h1, g_ref[...]).astype(BF16)
    acc = h1
    for c in range(d_ff // tf):
        gate = jnp.dot(u, win_ref[:, c * tf:(c + 1) * tf], preferred_element_type=F32)
        up = jnp.dot(u, win_ref[:, d_ff + c * tf:d_ff + (c + 1) * tf],
                     preferred_element_type=F32)
        act = (gate * jax.nn.sigmoid(gate) * up).astype(BF16)
        acc = acc + jnp.dot(act, wout_ref[c * tf:(c + 1) * tf, :],
                            preferred_element_type=F32)
    *ple_refs, out_ref = rest
    if ple_refs:
        *ple_refs, final_g_ref = ple_refs
        acc = _rmsnorm(_ple_update(acc, *ple_refs), final_g_ref[...])
    out_ref[...] = acc


def _oproj_ffn(h2, o_t, w_o, g, w_in, w_out, *, tm, tf, last=None):
    n, d = h2.shape
    d_ff = w_out.shape[0]
    assert d_ff % tf == 0 and o_t.shape == (n // tm, d, tm)
    row = pl.BlockSpec((tm, d), lambda i: (i, 0))
    weights = (w_o, g, w_in, w_out)
    in_specs = ([row, pl.BlockSpec((None, d, tm), lambda i: (i, 0, 0))]
                + [_resident(w) for w in weights])
    operands = (h2, o_t, *map(_array, weights))
    if last is not None:
        ple, final_g = last
        in_specs += ple.in_specs(tm) + [_resident(final_g)]
        operands += ple.operands() + (final_g,)
    return pl.pallas_call(
        functools.partial(_oproj_ffn_kernel, tf=tf), grid=(n // tm,),
        in_specs=in_specs, out_specs=row,
        out_shape=jax.ShapeDtypeStruct((n, d), F32),
        compiler_params=_params(1), name="oproj_ffn")(*operands)


def _ple_next_kernel(x_ref, g_ref, wg_ref, p_ref, wu_ref, *refs, n_mixer, scaled_cols):
    mixer_refs, cast_in, out_refs, cast_out = _split_refs(refs, n_mixer, len(scaled_cols))
    y = _ple_update(x_ref[...], g_ref, wg_ref, p_ref, wu_ref)
    out_refs[0][...] = y
    _mixer_inputs(y, mixer_refs, out_refs[1:])
    _cast_rows(scaled_cols, cast_in, cast_out)


def _ple_next(h2, ple, mixer, casts, *, tm):
    n, d = h2.shape
    row = pl.BlockSpec((tm, d), lambda i: (i, 0))
    n_mixer = len(mixer.operands())
    cast_in, cast_out, cast_shapes = _cast_specs(casts, n // tm)
    h, *outs = pl.pallas_call(
        functools.partial(_ple_next_kernel, n_mixer=n_mixer,
                          scaled_cols=tuple(c.scaled_cols for c in casts)),
        grid=(n // tm,),
        in_specs=[row] + ple.in_specs(tm) + mixer.in_specs() + cast_in,
        out_specs=[row] + mixer.out_specs(tm) + cast_out,
        out_shape=[jax.ShapeDtypeStruct((n, d), F32)] + mixer.out_shapes(n, tm) + cast_shapes,
        compiler_params=_params(1), name="ple_proj")(
            h2, *ple.operands(), *mixer.operands(), *(c.w.stack for c in casts))
    return h, outs[:len(outs) - len(casts)], outs[len(outs) - len(casts):]


def _row_tile(n, want):
    t = min(want, n)
    assert n % t == 0
    return t


def _col_tile(n, want):
    t = min(want, n)
    while n % t:
        t -= LANES
    return t


def kernel(x, p, attn_norm_g, fox_w_in, fox_b_f, fox_w_o, moba_w_in, moba_w_o, rel_bias_table,
           ffn_norm_g, ffn_w_in, ffn_w_out, ple_norm_g, ple_w_gate, ple_w_up, final_norm_g):
    b, s, d = x.shape
    depth = p.shape[0]
    n_heads = rel_bias_table.shape[1]
    assert d == n_heads * HEAD_DIM and n_heads % 2 == 0 and n_heads <= LANES
    n = b * s
    tm = _row_tile(n, DENSE_ROW_TILE)
    tf = _col_tile(ffn_w_out.shape[1], FFN_COL_CHUNK)

    def row_vec(v):
        return v.reshape(1, -1).astype(F32)

    w_in_mix, w_o_mix = (fox_w_in, moba_w_in), (fox_w_o, moba_w_o)
    p3 = p.reshape(depth, n, -1)

    def layer_casts(i):
        jobs = [_Cast(_Slab(w_o_mix[i % 2], i // 2), d),
                _Cast(_Slab(ffn_w_in, i), ffn_w_in.shape[2]), _Cast(_Slab(ffn_w_out, i), d),
                _Cast(_Slab(ple_w_gate, i), d), _Cast(_Slab(ple_w_up, i), d)]
        if i + 1 < depth:
            jobs.append(_Cast(_Slab(w_in_mix[(i + 1) % 2], (i + 1) // 2), 3 * d, d))
        return jobs

    def mixer(i, w_qkv):
        parts = [row_vec(attn_norm_g[i]), w_qkv]
        if i % 2 == 0:
            w_f = fox_w_in[i // 2][:, 3 * d:]
            parts += [jnp.pad(w_f, ((0, 0), (0, LANES - n_heads))).astype(BF16),
                      jnp.pad(row_vec(fox_b_f[i // 2]), ((0, 0), (0, LANES - n_heads)))]
        return _Mixer(*parts)

    rel_table = rel_bias_table.astype(F32)
    bias_t = _bias_tiles(rel_table, MOBA_BLOCK)

    h = x.reshape(n, d).astype(F32)
    col = lax.broadcasted_iota(jnp.int32, (1, 3 * d), 1)
    w_qkv = (fox_w_in[0][:, :3 * d] * jnp.where(col < d, Q_SCALE, 1.0)).astype(BF16)
    mixed, weights = _project(h, mixer(0, w_qkv), layer_casts(0), tm=tm)
    for i in range(depth):
        w_o, w_ffn_in, w_ffn_out, w_ple_gate, w_ple_up, *w_qkv_next = weights
        qt, k3, vt = mixed[0], mixed[1].reshape(b, s, d), mixed[2]
        if i % 2 == 0:
            aqt, ak = _cumsum(mixed[3].reshape(b, s, LANES))
            o_t = _fox_attention(qt, k3, vt, aqt, ak, n_heads)
        else:
            o_t = _moba_attention(qt, k3, vt, rel_table, bias_t, n_heads)
        ple = _Ple(row_vec(ple_norm_g[i]), w_ple_gate, _Slab(p3, i), w_ple_up)
        last = (ple, row_vec(final_norm_g)) if i + 1 == depth else None
        h = _oproj_ffn(h, o_t, w_o, row_vec(ffn_norm_g[i]), w_ffn_in, w_ffn_out,
                       tm=tm, tf=tf, last=last)
        if last is None:
            h, mixed, weights = _ple_next(h, ple, mixer(i + 1, *w_qkv_next),
                                          layer_casts(i + 1), tm=tm)
    return h.reshape(b, s, d).astype(x.dtype)
```

```python
import functools
import math
from typing import NamedTuple, Optional

import numpy as np
import jax
import jax.numpy as jnp
from jax import lax
from jax.experimental import pallas as pl
from jax.experimental.pallas import tpu as pltpu

F32 = jnp.float32
BF16 = jnp.bfloat16

RMS_EPS = 1e-6
HEAD_DIM = 64
MOBA_BLOCK = 256
MOBA_TOP_K = 3
REL_BUCKETS = 32
REL_MAX_DIST = 128

LANES = 128
F32_SUBLANES = 8
BF16_SUBLANES = 16
V7X_VMEM_BYTES = 64 * 1024 * 1024
V_ROWS = HEAD_DIM + BF16_SUBLANES
SPLIT_TERMS = 3
GATE_LANES = 8
SCORE_PITCH_PAD = LANES
LOG2E = math.log2(math.e)
Q_SCALE = HEAD_DIM ** -0.5 * LOG2E
MOBA_Q_TILE = 2 * MOBA_BLOCK
ATTN_Q_TILE = MOBA_Q_TILE
DENSE_ROW_TILE = ATTN_Q_TILE
CUMSUM_BLK = ATTN_Q_TILE
PROJ_COL_CHUNK = 512
FFN_COL_CHUNK = 256
VMEM_LIMIT_BYTES = V7X_VMEM_BYTES // 8 * 7
NEG_INF = float("-inf")


def _params(n_axes):
    return pltpu.CompilerParams(
        dimension_semantics=("arbitrary",) * n_axes,
        vmem_limit_bytes=VMEM_LIMIT_BYTES)


def _split3(x):
    x1 = x.astype(BF16)
    r1 = x - x1.astype(F32)
    x2 = r1.astype(BF16)
    x3 = (r1 - x2.astype(F32)).astype(BF16)
    return x1, x2, x3


def _rmsnorm(x, g):
    ms = jnp.mean(x * x, axis=-1, keepdims=True)
    return x * lax.rsqrt(ms + RMS_EPS) * g


def _log_sigmoid(x):
    return jnp.minimum(x, 0.0) - jnp.log1p(jnp.exp(-jnp.abs(x)))


class _Slab(NamedTuple):
    stack: jax.Array
    layer: int

    @property
    def shape(self):
        return self.stack.shape[1:]


def _resident(x):
    if isinstance(x, _Slab):
        layer, zeros = x.layer, (0,) * len(x.shape)
        return pl.BlockSpec((None,) + x.shape, lambda i: (layer,) + zeros,
                            pipeline_mode=pl.Buffered(1))
    zeros = (0,) * x.ndim
    return pl.BlockSpec(x.shape, lambda i: zeros, pipeline_mode=pl.Buffered(1))


def _array(x):
    return x.stack if isinstance(x, _Slab) else x


class _Cast(NamedTuple):
    w: _Slab
    rows: int
    scaled: int = 0
    transpose: bool = False

    def plan(self, steps):
        cols = self.w.shape[1]
        min_rows = LANES if self.transpose else BF16_SUBLANES
        assert self.rows % min_rows == 0 and cols % LANES == 0
        chunks = math.gcd(steps, self.rows // min_rows)
        span, layer, chunk = steps // chunks, self.w.layer, self.rows // chunks
        in_spec = pl.BlockSpec((None, chunk, cols), lambda i: (layer, i // span, 0))
        if self.transpose:
            return (in_spec, pl.BlockSpec((cols, chunk), lambda i: (0, i // span)),
                    jax.ShapeDtypeStruct((cols, self.rows), BF16), span)
        return (in_spec, pl.BlockSpec((chunk, cols), lambda i: (i // span, 0)),
                jax.ShapeDtypeStruct((self.rows, cols), BF16), span)


def _cast_plans(casts, steps):
    plans = [c.plan(steps) for c in casts]
    return ([p[0] for p in plans], [p[1] for p in plans], [p[2] for p in plans],
            tuple((c.transpose, c.scaled, p[3]) for c, p in zip(casts, plans)))


def _cast_rows(jobs, in_refs, out_refs):
    for (transpose, scaled, span), in_ref, out_ref in zip(jobs, in_refs, out_refs):
        w = in_ref[...].T if transpose else in_ref[...]
        if scaled:
            first_row = 0 if transpose else pl.program_id(0) // span * w.shape[0]
            row = first_row + lax.broadcasted_iota(jnp.int32, (w.shape[0], 1), 0)
            w = w * jnp.where(row < scaled, Q_SCALE, 1.0)
        out_ref[...] = w.astype(BF16)


class _Mixer(NamedTuple):
    g: jax.Array
    w_qkv: jax.Array
    w_f: Optional[jax.Array] = None
    b_f: Optional[jax.Array] = None

    def operands(self):
        return tuple(_array(a) for a in self if a is not None)

    def in_specs(self):
        return [_resident(a) for a in self if a is not None]

    def out_specs(self, tm):
        d = self.w_qkv.shape[1]
        specs = [pl.BlockSpec((None, d, tm), lambda i: (i, 0, 0)),
                 pl.BlockSpec((tm, d), lambda i: (i, 0)),
                 pl.BlockSpec((d, tm), lambda i: (0, i))]
        if self.w_f is not None:
            specs.append(pl.BlockSpec((tm, LANES), lambda i: (i, 0)))
        return specs

    def out_shapes(self, n, tm):
        d = self.w_qkv.shape[1]
        shapes = [jax.ShapeDtypeStruct((n // tm, d, tm), BF16),
                  jax.ShapeDtypeStruct((n, d), BF16), jax.ShapeDtypeStruct((d, n), BF16)]
        if self.w_f is not None:
            shapes.append(jax.ShapeDtypeStruct((n, LANES), F32))
        return shapes


def _chunks(total, want):
    step = want if total % want == 0 else total
    return [slice(c * step, (c + 1) * step) for c in range(total // step)]


def _mixer_inputs(y, mixer_refs, out_refs):
    g_ref, wt_ref = mixer_refs[:2]
    d = wt_ref.shape[1]
    u = _rmsnorm(y, g_ref[...]).astype(BF16)

    def contract_last(a, b):
        return lax.dot_general(a, b, (((1,), (1,)), ((), ())), preferred_element_type=F32)

    for first_row, out_ref in ((0, out_refs[0]), (2 * d, out_refs[2])):
        for rows in _chunks(d, PROJ_COL_CHUNK):
            w_rows = wt_ref[first_row + rows.start:first_row + rows.stop, :]
            out_ref[rows, :] = contract_last(w_rows, u).astype(BF16)
    for cols in _chunks(d, PROJ_COL_CHUNK):
        out_refs[1][:, cols] = contract_last(
            u, wt_ref[d + cols.start:d + cols.stop, :]).astype(BF16)
    if len(mixer_refs) > 2:
        wft_ref, bf_ref = mixer_refs[2:]
        f_logit = contract_last(u, wft_ref[...].astype(BF16)) + bf_ref[...]
        out_refs[3][...] = _log_sigmoid(f_logit)


def _split_refs(refs, n_mixer, n_cast):
    n_out = len(refs) - n_cast
    return (refs[:n_mixer], refs[n_mixer:n_mixer + n_cast],
            refs[n_mixer + n_cast:n_out], refs[n_out:])


def _proj_kernel(x_ref, *refs, n_mixer, jobs):
    mixer_refs, cast_in, out_refs, cast_out = _split_refs(refs, n_mixer, len(jobs))
    _mixer_inputs(x_ref[...], mixer_refs, out_refs)
    _cast_rows(jobs, cast_in, cast_out)


def _project(h2, mixer, casts, *, tm):
    n, d = h2.shape
    n_mixer = len(mixer.operands())
    cast_in, cast_out, cast_shapes, jobs = _cast_plans(casts, n // tm)
    outs = pl.pallas_call(
        functools.partial(_proj_kernel, n_mixer=n_mixer, jobs=jobs),
        grid=(n // tm,),
        in_specs=[pl.BlockSpec((tm, d), lambda i: (i, 0))] + mixer.in_specs() + cast_in,
        out_specs=mixer.out_specs(tm) + cast_out,
        out_shape=mixer.out_shapes(n, tm) + cast_shapes,
        compiler_params=_params(1), name="proj")(
            h2, *mixer.operands(), *(c.w.stack for c in casts))
    return outs[:len(outs) - len(casts)], outs[len(outs) - len(casts):]


def _cumsum_kernel(lf_ref, aqt_ref, ak_ref, carry_ref):
    @pl.when(pl.program_id(1) == 0)
    def _():
        carry_ref[...] = jnp.zeros_like(carry_ref)

    t = lf_ref.shape[1]
    row = lax.broadcasted_iota(jnp.int32, (t, t), 0)
    col = lax.broadcasted_iota(jnp.int32, (t, t), 1)
    tril = jnp.where(col <= row, 1.0, 0.0).astype(BF16)
    x1, x2, x3 = _split3(lf_ref[0])
    cs = (jnp.dot(tril, x1, preferred_element_type=F32)
          + jnp.dot(tril, x2, preferred_element_type=F32)
          + jnp.dot(tril, x3, preferred_element_type=F32))
    cs = cs + carry_ref[0:1, :]
    carry_ref[...] = jnp.broadcast_to(cs[t - 1:t, :], carry_ref.shape)

    src = lax.broadcasted_iota(jnp.int32, (LANES, LANES), 0)
    dst = lax.broadcasted_iota(jnp.int32, (LANES, LANES), 1)
    lane = lax.broadcasted_iota(jnp.int32, (1, LANES), 1) & (GATE_LANES - 1)
    aq = jnp.where((lane >= SPLIT_TERMS) & (lane < 2 * SPLIT_TERMS), 1.0, 0.0)
    ak = jnp.where(lane < SPLIT_TERMS, 1.0, 0.0)
    for i, part in enumerate(_split3(cs * LOG2E)):
        to_q = jnp.where(dst == GATE_LANES * src + i, 1.0, 0.0).astype(BF16)
        to_k = jnp.where(dst == GATE_LANES * src + SPLIT_TERMS + i, 1.0, 0.0).astype(BF16)
        aq = aq + jnp.dot(part, to_q, preferred_element_type=F32)
        ak = ak - jnp.dot(part, to_k, preferred_element_type=F32)
    eye = jnp.where(src == dst, 1.0, 0.0).astype(BF16)
    aqt_ref[0] = lax.dot_general(eye, aq.astype(BF16), (((1,), (1,)), ((), ())),
                                 preferred_element_type=F32).astype(BF16)
    ak_ref[0] = ak.astype(BF16)


def _cumsum(lf3):
    b, s, _ = lf3.shape
    t = min(CUMSUM_BLK, s)
    nt = s // t
    spec = pl.BlockSpec((1, t, LANES), lambda i, j: (i, j, 0))
    return pl.pallas_call(
        _cumsum_kernel, grid=(b, nt),
        in_specs=[spec],
        out_specs=[pl.BlockSpec((1, LANES, t), lambda i, j: (i * nt + j, 0, 0)), spec],
        out_shape=[jax.ShapeDtypeStruct((b * nt, LANES, t), BF16),
                   jax.ShapeDtypeStruct(lf3.shape, BF16)],
        scratch_shapes=[pltpu.VMEM((F32_SUBLANES, LANES), F32)],
        compiler_params=_params(2), name="gate_cumsum")(lf3)


def _fill_value_rows(vt_ref, v_ref, blk):
    row = lax.broadcasted_iota(jnp.int32, (V_ROWS - HEAD_DIM, blk), 0)
    tail = jnp.where(row == 0, 1.0, 0.0).astype(BF16)
    for hh in range(2):
        for jb in range(v_ref.shape[1] // blk):
            head_rows = v_ref[hh * HEAD_DIM:(hh + 1) * HEAD_DIM, jb * blk:(jb + 1) * blk]
            vt_ref[hh, jb] = jnp.concatenate([head_rows, tail], axis=0)


def _flash_init(m_ref, acc_ref):
    m_ref[...] = jnp.full(m_ref.shape, NEG_INF, F32)
    acc_ref[...] = jnp.zeros(acc_ref.shape, F32)


def _pipelined_blocks(n, scores_into, update, final_scores, final_update, next_tile_scores):
    def step(j_next, j, slot):
        for hh in range(2):
            scores_into(j_next, 1 - slot, hh)
            update(j, slot, hh)

    def last_update_and_final(j, slot):
        for hh in range(2):
            final_scores(hh)
            update(j, slot, hh)
        next_tile_scores()
        final_update()

    def pair(jj, carry):
        j = 2 * jj
        step(j + 1, j, 0)
        step(j + 2, j + 1, 1)
        return carry
    lax.fori_loop(0, jnp.maximum(n - 1, 0) // 2, pair, 0)

    @pl.when(n % 2 == 1)
    def _():
        last_update_and_final(n - 1, 0)

    @pl.when((n % 2 == 0) & (n > 0))
    def _():
        step(n - 1, n - 2, 0)
        last_update_and_final(n - 1, 1)

    @pl.when(n == 0)
    def _():
        for hh in range(2):
            final_scores(hh)
        next_tile_scores()
        final_update()


def _store_scores(s_t, s_ref, mx_ref, idx):
    s_ref[idx + (slice(None), slice(0, s_t.shape[1]))] = s_t
    mx_ref[idx] = jnp.max(s_t, axis=0, keepdims=True)


def _flash_update(s_t, mx, v_rows, m_ref, acc_ref, hh, keep=None, const=None,
                  cols=slice(None)):
    m_old = m_ref[hh, :, cols]
    if const is not None:
        mx = mx + const
    if keep is not None:
        mx = jnp.where(keep, mx, NEG_INF)
    m_new = jnp.maximum(m_old, mx)
    m_safe = jnp.where(m_new == NEG_INF, 0.0, m_new)
    shift = m_safe if const is None else m_safe - const
    if keep is not None:
        shift = jnp.where(keep, shift, float("inf"))
    p = jnp.exp2(s_t - shift).astype(BF16)
    alpha = jnp.exp2(m_old - m_safe)
    m_ref[hh, :, cols] = m_new
    if not isinstance(v_rows, (list, tuple)):
        v_rows = [v_rows]
    keys = p.shape[0] // len(v_rows)
    acc = alpha * acc_ref[hh, :, cols]
    for i, v_i in enumerate(v_rows):
        acc = acc + jnp.dot(v_i, p[i * keys:(i + 1) * keys], preferred_element_type=F32)
    acc_ref[hh, :, cols] = acc


def _head_outputs(acc_ref):
    outs = []
    for hh in range(2):
        acc = acc_ref[hh]
        outs.append(acc[:HEAD_DIM] / acc[HEAD_DIM:HEAD_DIM + 1])
    return jnp.concatenate(outs, axis=0).astype(BF16)


def _for_each_query_tile(nq, tile):
    def body(qi, carry):
        tile(qi)
        return carry
    lax.fori_loop(0, nq, body, 0)


def _head_row_mask(hh):
    row = lax.broadcasted_iota(jnp.int32, (LANES, 1), 0)
    return (row < HEAD_DIM) if hh == 0 else (row >= HEAD_DIM)


def _fox_kernel(qt_ref, k_ref, v_ref, *refs):
    vt_ref = refs[3]
    nq, _, blk = qt_ref.shape
    _fill_value_rows(vt_ref, v_ref, blk)
    hp = pl.program_id(1)
    _for_each_query_tile(nq, lambda qi: _fox_tile(qi, hp, qt_ref, k_ref, *refs))


def _fox_tile(qi, hp, qt_ref, k_ref, aqt_ref, ak_ref, o_ref,
              vt_ref, s_ref, mx_ref, diag_ref, m_ref, acc_ref):
    nq, _, blk = qt_ref.shape
    gate_row = lax.broadcasted_iota(jnp.int32, (LANES, 1), 0)

    def query_operands(tile):
        qt, aqt = qt_ref[tile], aqt_ref[tile]
        out = []
        for hh in range(2):
            first = GATE_LANES * (2 * hp + hh)
            own_gate = (gate_row >= first) & (gate_row < first + GATE_LANES)
            q_rows = jnp.where(_head_row_mask(hh), qt, jnp.zeros_like(qt))
            g_rows = jnp.where(own_gate, aqt, jnp.zeros_like(aqt))
            out.append(jnp.concatenate([q_rows, g_rows], axis=0))
        return out

    def block_scores(w, j, hh):
        rows = pl.ds(pl.multiple_of(j * blk, blk), blk)
        keys = jnp.concatenate([k_ref[0, rows, :], ak_ref[0, rows, :]], axis=1)
        return jnp.dot(keys, w[hh], preferred_element_type=F32)

    w_q = query_operands(qi)

    def scores_into(j, slot, hh):
        _store_scores(block_scores(w_q, j, hh), s_ref, mx_ref, (slot, hh))

    def next_tile_scores():
        w_next = query_operands(jnp.minimum(qi + 1, nq - 1))
        for hh in range(2):
            _store_scores(block_scores(w_next, 0, hh), s_ref, mx_ref, (0, hh))

    def update(j, slot, hh):
        _flash_update(s_ref[slot, hh, :, :blk], mx_ref[slot, hh], vt_ref[hh, j],
                      m_ref, acc_ref, hh)

    def diagonal_scores(hh):
        diag_ref[hh] = block_scores(w_q, qi, hh)

    def diagonal_update():
        half = blk // 2
        lo, hi = slice(0, half), slice(half, blk)
        causal_lo = (lax.broadcasted_iota(jnp.int32, (half, half), 0)
                     <= lax.broadcasted_iota(jnp.int32, (half, half), 1))
        causal_hi = (lax.broadcasted_iota(jnp.int32, (blk, half), 0)
                     <= lax.broadcasted_iota(jnp.int32, (blk, half), 1) + half)
        for hh in range(2):
            v_rows = vt_ref[hh, qi]
            s_lo = jnp.where(causal_lo, diag_ref[hh, lo, lo], NEG_INF)
            _flash_update(s_lo, jnp.max(s_lo, axis=0, keepdims=True),
                          v_rows[:, lo], m_ref, acc_ref, hh, cols=lo)
            s_hi = jnp.where(causal_hi, diag_ref[hh, :, hi], NEG_INF)
            _flash_update(s_hi, jnp.max(s_hi, axis=0, keepdims=True),
                          v_rows, m_ref, acc_ref, hh, cols=hi)

    _flash_init(m_ref, acc_ref)
    _pipelined_blocks(qi, scores_into, update, diagonal_scores, diagonal_update,
                      next_tile_scores)
    o_ref[qi] = _head_outputs(acc_ref)


def _fox_attention(qt3, k3, vt, aqt3, ak3, n_heads):
    b, s, d = k3.shape
    assert n_heads * GATE_LANES <= LANES
    blk = qt3.shape[2]
    assert aqt3.shape[2] == blk and s % blk == 0
    npair = n_heads // 2
    nq = s // blk
    head_pair_tiles = pl.BlockSpec((nq, LANES, blk), lambda bi, hp: (bi, hp, 0))
    return pl.pallas_call(
        _fox_kernel, grid=(b, npair),
        in_specs=[
            head_pair_tiles,
            pl.BlockSpec((1, s, LANES), lambda bi, hp: (bi, 0, hp)),
            pl.BlockSpec((LANES, s), lambda bi, hp: (hp, bi)),
            pl.BlockSpec((nq, LANES, blk), lambda bi, hp: (bi, 0, 0)),
            pl.BlockSpec((1, s, LANES), lambda bi, hp: (bi, 0, 0)),
        ],
        out_specs=head_pair_tiles,
        out_shape=jax.ShapeDtypeStruct(qt3.shape, BF16),
        scratch_shapes=[
            pltpu.VMEM((2, s // blk, V_ROWS, blk), BF16),
            pltpu.VMEM((2, 2, blk, blk + SCORE_PITCH_PAD), F32),
            pltpu.VMEM((2, 2, 1, blk), F32),
            pltpu.VMEM((2, blk, blk), F32),
            pltpu.VMEM((2, 1, blk), F32),
            pltpu.VMEM((2, V_ROWS, blk), F32),
        ],
        compiler_params=_params(2), name="fox_attention")(qt3, k3, vt, aqt3, ak3)


def _t5_bucket_np(n):
    max_exact = REL_BUCKETS // 2
    nf = np.maximum(n, 1).astype(np.float64)
    large = max_exact + (np.log(nf / max_exact) / math.log(REL_MAX_DIST / max_exact)
                         * (REL_BUCKETS - max_exact)).astype(np.int32)
    return np.where(n < max_exact, n, np.minimum(large, REL_BUCKETS - 1)).astype(np.int32)


def _bucket_tiles(blk):
    key = np.arange(blk)[:, None]
    qry = np.arange(blk)[None, :]
    own = np.where(key <= qry, _t5_bucket_np(np.maximum(qry - key, 0)), -1)
    prev = _t5_bucket_np(blk + qry - key)
    return np.stack([own, prev]).astype(np.int32)


def _bias_kernel(tab_ref, bucket_ref, o_ref):
    h = pl.program_id(0)
    bucket = bucket_ref[...]
    acc = jnp.where(bucket < 0, NEG_INF, 0.0).astype(F32)
    for bkt in range(REL_BUCKETS):
        acc = jnp.where(bucket == bkt, tab_ref[bkt, h] * LOG2E, acc)
    o_ref[0] = acc


def _bias_tiles(rel_table, blk):
    n_heads = rel_table.shape[1]
    buckets = jnp.asarray(_bucket_tiles(blk))
    return pl.pallas_call(
        _bias_kernel, grid=(n_heads,),
        in_specs=[pl.BlockSpec(memory_space=pltpu.SMEM),
                  pl.BlockSpec((2, blk, blk), lambda h: (0, 0, 0))],
        out_specs=pl.BlockSpec((1, 2, blk, blk), lambda h: (h, 0, 0, 0)),
        out_shape=jax.ShapeDtypeStruct((n_heads, 2, blk, blk), F32),
        compiler_params=_params(1), name="t5_bias_tiles")(rel_table, buckets)


def _select_blocks(qt_ref, km_ref, sel_ref):
    nq, _, tq = qt_ref.shape
    blk = MOBA_BLOCK
    nblk = km_ref.shape[0]
    assert blk & (blk - 1) == 0
    qt_all = jnp.concatenate([qt_ref[t] for t in range(nq)], axis=1)
    km_parts = _split3(km_ref[...])
    blk_id = lax.broadcasted_iota(jnp.int32, (nblk, nq * tq), 0)
    own = lax.shift_right_logical(lax.broadcasted_iota(jnp.int32, (nblk, nq * tq), 1),
                                  blk.bit_length() - 1)
    past = blk_id < own
    for hh in range(2):
        q_h = jnp.where(_head_row_mask(hh), qt_all, jnp.zeros_like(qt_all))
        gate = None
        for part in km_parts:
            term = jnp.dot(part, q_h, preferred_element_type=F32)
            gate = term if gate is None else gate + term
        work = jnp.where(past, gate, NEG_INF)
        picked = jnp.zeros(gate.shape, F32)
        for _ in range(MOBA_TOP_K):
            best = jnp.max(work, axis=0, keepdims=True)
            first = jnp.min(jnp.where(work == best, blk_id, nblk), axis=0, keepdims=True)
            hit = blk_id == first
            picked = jnp.where(hit, 1.0, picked)
            work = jnp.where(hit, NEG_INF, work)
        keep_all = ((picked > 0.5) & past) | (blk_id == own) | (blk_id == (own | 1))
        keep_f = jnp.where(keep_all, 1.0, 0.0)
        for t in range(nq):
            sel_ref[hh, t] = keep_f[:, t * tq:(t + 1) * tq]


def _moba_kernel(tab_ref, qt_ref, k_ref, v_ref, bias_ref, o_ref,
                 vt_ref, km_ref, sel_ref, *scratch):
    blk = MOBA_BLOCK

    def block_mean(jb, carry):
        rows = pl.ds(pl.multiple_of(jb * blk, blk), blk)
        km_ref[pl.ds(jb, 1), :] = jnp.mean(k_ref[0, rows, :].astype(F32),
                                            axis=0, keepdims=True)
        return carry
    lax.fori_loop(0, k_ref.shape[1] // blk, block_mean, 0)
    _fill_value_rows(vt_ref, v_ref, blk)
    _select_blocks(qt_ref, km_ref, sel_ref)
    hp = pl.program_id(1)
    _for_each_query_tile(
        qt_ref.shape[0],
        lambda qi: _moba_tile(qi, hp, tab_ref, qt_ref, k_ref, bias_ref, o_ref,
                              vt_ref, sel_ref, *scratch))


def _moba_tile(qi, hp, tab_ref, qt_ref, k_ref, bias_ref, o_ref,
               vt_ref, sel_ref, s_ref, mx_ref, near_ref, own1_ref, m_ref, acc_ref):
    blk = MOBA_BLOCK
    nq, _, tq = qt_ref.shape
    assert tq == 2 * blk and blk >= REL_MAX_DIST
    first_own = 2 * qi

    def head_queries(tile):
        qt = qt_ref[tile]
        return [jnp.where(_head_row_mask(hh), qt, jnp.zeros_like(qt)) for hh in range(2)]

    q_m_t = head_queries(qi)

    def scores(hh, j, queries=q_m_t):
        rows = pl.ds(pl.multiple_of(j * blk, blk), blk)
        return jnp.dot(k_ref[0, rows, :], queries[hh], preferred_element_type=F32)

    def next_tile_scores():
        q_next = head_queries(jnp.minimum(qi + 1, nq - 1))
        for hh in range(2):
            _store_scores(scores(hh, 0, q_next), s_ref, mx_ref, (0, hh))

    def keep(hh, j):
        return sel_ref[hh, qi, pl.ds(j, 1), :] > 0.5

    _flash_init(m_ref, acc_ref)
    far_bias = [tab_ref[REL_BUCKETS - 1, 2 * hp + hh] * LOG2E for hh in range(2)]

    j_prev = jnp.maximum(first_own - 1, 0)
    lo, hi = slice(0, blk), slice(blk, 2 * blk)
    near_mx = {}

    def near_scores(hh):
        own_t, prev_t = bias_ref[hh, 0], bias_ref[hh, 1]
        mask_prev = jnp.where(keep(hh, j_prev) & (qi >= 1), 0.0, NEG_INF)
        mask_own = jnp.where(keep(hh, first_own), 0.0, NEG_INF)
        far_t = jnp.full((blk, blk), far_bias[hh], F32)
        parts = [scores(hh, j_prev) + mask_prev + jnp.concatenate([prev_t, far_t], axis=1),
                 scores(hh, first_own) + mask_own + jnp.concatenate([own_t, prev_t], axis=1)]
        mx = None
        for i, part in enumerate(parts):
            near_ref[hh, i * blk:(i + 1) * blk] = part
            part_mx = jnp.max(part, axis=0, keepdims=True)
            mx = part_mx if mx is None else jnp.maximum(mx, part_mx)
        rows = pl.ds(pl.multiple_of((first_own + 1) * blk, blk), blk)
        last = jnp.dot(k_ref[0, rows, :], q_m_t[hh][:, hi], preferred_element_type=F32) + own_t
        own1_ref[hh] = last
        near_mx[hh] = (mx[:, lo], jnp.maximum(mx[:, hi], jnp.max(last, axis=0, keepdims=True)))

    def near_update():
        for hh in range(2):
            v_near = [vt_ref[hh, j_prev], vt_ref[hh, first_own], vt_ref[hh, first_own + 1]]
            _flash_update(near_ref[hh, :, lo], near_mx[hh][0], v_near[:2],
                          m_ref, acc_ref, hh, cols=lo)
            _flash_update(jnp.concatenate([near_ref[hh, :, hi], own1_ref[hh]], axis=0),
                          near_mx[hh][1], v_near, m_ref, acc_ref, hh, cols=hi)

    def scores_into(j, slot, hh):
        _store_scores(scores(hh, j), s_ref, mx_ref, (slot, hh))

    def update(j, slot, hh):
        _flash_update(s_ref[slot, hh, :, :tq], mx_ref[slot, hh], vt_ref[hh, j],
                      m_ref, acc_ref, hh, keep=keep(hh, j), const=far_bias[hh])

    _pipelined_blocks(jnp.maximum(first_own - 1, 0), scores_into, update,
                      near_scores, near_update, next_tile_scores)
    o_ref[qi] = _head_outputs(acc_ref)


def _moba_attention(qt3, k3, vt, rel_table, bias_t, n_heads):
    b, s, d = k3.shape
    blk = MOBA_BLOCK
    tq = qt3.shape[2]
    assert tq == MOBA_Q_TILE and s % tq == 0
    npair = n_heads // 2
    nblk = s // blk
    nq = s // tq
    head_pair_tiles = pl.BlockSpec((nq, LANES, tq), lambda bi, hp: (bi, hp, 0))
    return pl.pallas_call(
        _moba_kernel, grid=(b, npair),
        in_specs=[
            pl.BlockSpec(memory_space=pltpu.SMEM),
            head_pair_tiles,
            pl.BlockSpec((1, s, LANES), lambda bi, hp: (bi, 0, hp)),
            pl.BlockSpec((LANES, s), lambda bi, hp: (hp, bi)),
            pl.BlockSpec((2, 2, blk, blk), lambda bi, hp: (hp, 0, 0, 0)),
        ],
        out_specs=head_pair_tiles,
        out_shape=jax.ShapeDtypeStruct(qt3.shape, BF16),
        scratch_shapes=[
            pltpu.VMEM((2, nblk, V_ROWS, blk), BF16),
            pltpu.VMEM((nblk, LANES), F32),
            pltpu.VMEM((2, nq, nblk, tq), F32),
            pltpu.VMEM((2, 2, blk, tq + SCORE_PITCH_PAD), F32),
            pltpu.VMEM((2, 2, 1, tq), F32),
            pltpu.VMEM((2, 2 * blk, tq), F32),
            pltpu.VMEM((2, blk, blk), F32),
            pltpu.VMEM((2, 1, tq), F32),
            pltpu.VMEM((2, V_ROWS, tq), F32),
        ],
        compiler_params=_params(2), name="moba_attention")(
            rel_table, qt3, k3, vt, bias_t)


class _Ple(NamedTuple):
    g: jax.Array
    w_gate: jax.Array
    p: _Slab
    w_up: jax.Array

    def in_specs(self, tm):
        layer = self.p.layer
        return [_resident(self.g), _resident(self.w_gate),
                pl.BlockSpec((None, tm, self.p.shape[1]), lambda i: (layer, i, 0)),
                _resident(self.w_up)]

    def operands(self):
        return tuple(_array(a) for a in self)


def _ple_update(x, g_ref, wg_ref, p_ref, wu_ref):
    u = _rmsnorm(x, g_ref[...]).astype(BF16)
    gate = jax.nn.sigmoid(jnp.dot(u, wg_ref[...], preferred_element_type=F32))
    up = jnp.dot(p_ref[...].astype(BF16), wu_ref[...], preferred_element_type=F32)
    return x + gate * up


def _oproj_ffn_kernel(h_ref, ot_ref, wo_ref, g_ref, win_ref, wout_ref, *rest, tf):
    d_ff = wout_ref.shape[0]
    h1 = h_ref[...] + lax.dot_general(ot_ref[...], wo_ref[...], (((0,), (0,)), ((), ())),
                                      preferred_element_type=F32)
    u = _rmsnorm(h1, g_ref[...]).astype(BF16)
    acc = h1
    for c in range(d_ff // tf):
        gate = jnp.dot(u, win_ref[:, c * tf:(c + 1) * tf], preferred_element_type=F32)
        up = jnp.dot(u, win_ref[:, d_ff + c * tf:d_ff + (c + 1) * tf],
                     preferred_element_type=F32)
        act = (gate * jax.nn.sigmoid(gate) * up).astype(BF16)
        acc = acc + jnp.dot(act, wout_ref[c * tf:(c + 1) * tf, :],
                            preferred_element_type=F32)
    *ple_refs, out_ref = rest
    if ple_refs:
        *ple_refs, final_g_ref = ple_refs
        acc = _rmsnorm(_ple_update(acc, *ple_refs), final_g_ref[...])
    out_ref[...] = acc


def _oproj_ffn(h2, o_t, w_o, g, w_in, w_out, *, tm, tf, last=None):
    n, d = h2.shape
    d_ff = w_out.shape[0]
    assert d_ff % tf == 0 and o_t.shape == (n // tm, d, tm)
    row = pl.BlockSpec((tm, d), lambda i: (i, 0))
    weights = (w_o, g, w_in, w_out)
    in_specs = ([row, pl.BlockSpec((None, d, tm), lambda i: (i, 0, 0))]
                + [_resident(w) for w in weights])
    operands = (h2, o_t, *map(_array, weights))
    if last is not None:
        ple, final_g = last
        in_specs += ple.in_specs(tm) + [_resident(final_g)]
        operands += ple.operands() + (final_g,)
    return pl.pallas_call(
        functools.partial(_oproj_ffn_kernel, tf=tf), grid=(n // tm,),
        in_specs=in_specs, out_specs=row,
        out_shape=jax.ShapeDtypeStruct((n, d), F32),
        compiler_params=_params(1), name="oproj_ffn")(*operands)


def _ple_next_kernel(x_ref, g_ref, wg_ref, p_ref, wu_ref, *refs, n_mixer, jobs):
    mixer_refs, cast_in, out_refs, cast_out = _split_refs(refs, n_mixer, len(jobs))
    y = _ple_update(x_ref[...], g_ref, wg_ref, p_ref, wu_ref)
    out_refs[0][...] = y
    _mixer_inputs(y, mixer_refs, out_refs[1:])
    _cast_rows(jobs, cast_in, cast_out)


def _ple_next(h2, ple, mixer, casts, *, tm):
    n, d = h2.shape
    row = pl.BlockSpec((tm, d), lambda i: (i, 0))
    n_mixer = len(mixer.operands())
    cast_in, cast_out, cast_shapes, jobs = _cast_plans(casts, n // tm)
    h, *outs = pl.pallas_call(
        functools.partial(_ple_next_kernel, n_mixer=n_mixer, jobs=jobs),
        grid=(n // tm,),
        in_specs=[row] + ple.in_specs(tm) + mixer.in_specs() + cast_in,
        out_specs=[row] + mixer.out_specs(tm) + cast_out,
        out_shape=[jax.ShapeDtypeStruct((n, d), F32)] + mixer.out_shapes(n, tm) + cast_shapes,
        compiler_params=_params(1), name="ple_proj")(
            h2, *ple.operands(), *mixer.operands(), *(c.w.stack for c in casts))
    return h, outs[:len(outs) - len(casts)], outs[len(outs) - len(casts):]


def _row_tile(n, want):
    t = min(want, n)
    assert n % t == 0
    return t


def _col_tile(n, want):
    t = min(want, n)
    while n % t:
        t -= LANES
    return t


def kernel(x, p, attn_norm_g, fox_w_in, fox_b_f, fox_w_o, moba_w_in, moba_w_o, rel_bias_table,
           ffn_norm_g, ffn_w_in, ffn_w_out, ple_norm_g, ple_w_gate, ple_w_up, final_norm_g):
    b, s, d = x.shape
    depth = p.shape[0]
    n_heads = rel_bias_table.shape[1]
    assert d == n_heads * HEAD_DIM and n_heads % 2 == 0 and n_heads <= LANES
    n = b * s
    tm = _row_tile(n, DENSE_ROW_TILE)
    tf = _col_tile(ffn_w_out.shape[1], FFN_COL_CHUNK)

    def row_vec(v):
        return v.reshape(1, -1).astype(F32)

    w_o_mix = (fox_w_o, moba_w_o)
    p3 = p.reshape(depth, n, -1)
    fox_wt = jnp.swapaxes(fox_w_in, 1, 2)

    def qkv_cast(i):
        if i % 2 == 0:
            return _Cast(_Slab(fox_wt, i // 2), 3 * d, scaled=d)
        return _Cast(_Slab(moba_w_in, i // 2), d, scaled=d, transpose=True)

    def layer_casts(i):
        jobs = [_Cast(_Slab(w, layer), w.shape[1]) for w, layer in (
            (w_o_mix[i % 2], i // 2), (ffn_w_in, i), (ffn_w_out, i), (ple_w_gate, i),
            (ple_w_up, i))]
        return jobs + ([qkv_cast(i + 1)] if i + 1 < depth else [])

    def mixer(i, w_qkv):
        parts = [row_vec(attn_norm_g[i]), w_qkv]
        if i % 2 == 0:
            pad_heads = LANES - n_heads
            parts += [jnp.pad(fox_wt[i // 2, 3 * d:], ((0, pad_heads), (0, 0))),
                      jnp.pad(row_vec(fox_b_f[i // 2]), ((0, 0), (0, pad_heads)))]
        return _Mixer(*parts)

    rel_table = rel_bias_table.astype(F32)
    bias_t = _bias_tiles(rel_table, MOBA_BLOCK)

    h = x.reshape(n, d).astype(F32)
    row = lax.broadcasted_iota(jnp.int32, (3 * d, 1), 0)
    w_qkv = (fox_wt[0, :3 * d] * jnp.where(row < d, Q_SCALE, 1.0)).astype(BF16)
    mixed, weights = _project(h, mixer(0, w_qkv), layer_casts(0), tm=tm)
    for i in range(depth):
        w_o, w_ffn_in, w_ffn_out, w_ple_gate, w_ple_up, *w_qkv_next = weights
        qt, k3, vt = mixed[0], mixed[1].reshape(b, s, d), mixed[2]
        if i % 2 == 0:
            aqt, ak = _cumsum(mixed[3].reshape(b, s, LANES))
            o_t = _fox_attention(qt, k3, vt, aqt, ak, n_heads)
        else:
            o_t = _moba_attention(qt, k3, vt, rel_table, bias_t, n_heads)
        ple = _Ple(row_vec(ple_norm_g[i]), w_ple_gate, _Slab(p3, i), w_ple_up)
        last = (ple, row_vec(final_norm_g)) if i + 1 == depth else None
        h = _oproj_ffn(h, o_t, w_o, row_vec(ffn_norm_g[i]), w_ffn_in, w_ffn_out,
                       tm=tm, tf=tf, last=last)
        if last is None:
            h, mixed, weights = _ple_next(h, ple, mixer(i + 1, *w_qkv_next),
                                          layer_casts(i + 1), tm=tm)
    return h.reshape(b, s, d).astype(x.dtype)
```

```python
import functools
import math
from typing import NamedTuple, Optional

import numpy as np
import jax
import jax.numpy as jnp
from jax import lax
from jax.experimental import pallas as pl
from jax.experimental.pallas import tpu as pltpu

F32 = jnp.float32
BF16 = jnp.bfloat16

RMS_EPS = 1e-6
HEAD_DIM = 64
MOBA_BLOCK = 256
MOBA_TOP_K = 3
REL_BUCKETS = 32
REL_MAX_DIST = 128

LANES = 128
F32_SUBLANES = 8
BF16_SUBLANES = 16
V7X_VMEM_BYTES = 64 * 1024 * 1024
V_ROWS = HEAD_DIM + BF16_SUBLANES
SPLIT_TERMS = 3
GATE_LANES = 8
SCORE_PITCH_PAD = LANES
LOG2E = math.log2(math.e)
Q_SCALE = HEAD_DIM ** -0.5 * LOG2E
MOBA_Q_TILE = 2 * MOBA_BLOCK
ATTN_Q_TILE = MOBA_Q_TILE
DENSE_ROW_TILE = ATTN_Q_TILE
CUMSUM_BLK = ATTN_Q_TILE
PROJ_COL_CHUNK = 512
FFN_COL_CHUNK = 256
VMEM_LIMIT_BYTES = V7X_VMEM_BYTES // 8 * 7
NEG_INF = float("-inf")


def _params(n_axes):
    return pltpu.CompilerParams(
        dimension_semantics=("arbitrary",) * n_axes,
        vmem_limit_bytes=VMEM_LIMIT_BYTES)


def _split3(x):
    x1 = x.astype(BF16)
    r1 = x - x1.astype(F32)
    x2 = r1.astype(BF16)
    x3 = (r1 - x2.astype(F32)).astype(BF16)
    return x1, x2, x3


def _rmsnorm(x, g):
    ms = jnp.mean(x * x, axis=-1, keepdims=True)
    return x * lax.rsqrt(ms + RMS_EPS) * g


def _log_sigmoid(x):
    return jnp.minimum(x, 0.0) - jnp.log1p(jnp.exp(-jnp.abs(x)))


class _Slab(NamedTuple):
    stack: jax.Array
    layer: int

    @property
    def shape(self):
        return self.stack.shape[1:]


def _resident(x):
    if isinstance(x, _Slab):
        layer, zeros = x.layer, (0,) * len(x.shape)
        return pl.BlockSpec((None,) + x.shape, lambda i: (layer,) + zeros,
                            pipeline_mode=pl.Buffered(1))
    zeros = (0,) * x.ndim
    return pl.BlockSpec(x.shape, lambda i: zeros, pipeline_mode=pl.Buffered(1))


def _array(x):
    return x.stack if isinstance(x, _Slab) else x


class _Cast(NamedTuple):
    w: _Slab
    rows: int
    scaled: int = 0
    transpose: bool = False

    def plan(self, steps):
        cols = self.w.shape[1]
        min_rows = LANES if self.transpose else BF16_SUBLANES
        assert self.rows % min_rows == 0 and cols % LANES == 0
        chunks = math.gcd(steps, self.rows // min_rows)
        span, layer, chunk = steps // chunks, self.w.layer, self.rows // chunks

        def at(i):
            return jnp.minimum(i, steps - 1) // span

        in_spec = pl.BlockSpec((None, chunk, cols), lambda i: (layer, at(i), 0))
        if self.transpose:
            return (in_spec, pl.BlockSpec((cols, chunk), lambda i: (0, at(i))),
                    jax.ShapeDtypeStruct((cols, self.rows), BF16), span)
        return (in_spec, pl.BlockSpec((chunk, cols), lambda i: (at(i), 0)),
                jax.ShapeDtypeStruct((self.rows, cols), BF16), span)


def _cast_plans(casts, steps):
    plans = [c.plan(steps) for c in casts]
    return ([p[0] for p in plans], [p[1] for p in plans], [p[2] for p in plans],
            tuple((c.transpose, c.scaled, p[3], steps) for c, p in zip(casts, plans)))


def _cast_rows(jobs, in_refs, out_refs):
    for (transpose, scaled, span, steps), in_ref, out_ref in zip(jobs, in_refs, out_refs):
        w = in_ref[...].T if transpose else in_ref[...]
        if scaled:
            chunk = jnp.minimum(pl.program_id(0), steps - 1) // span
            first_row = 0 if transpose else chunk * w.shape[0]
            row = first_row + lax.broadcasted_iota(jnp.int32, (w.shape[0], 1), 0)
            w = w * jnp.where(row < scaled, Q_SCALE, 1.0)
        out_ref[...] = w.astype(BF16)


class _Mixer(NamedTuple):
    g: jax.Array
    w_qkv: jax.Array
    w_f: Optional[jax.Array] = None
    b_f: Optional[jax.Array] = None

    def operands(self):
        return tuple(_array(a) for a in self if a is not None)

    def in_specs(self):
        return [_resident(a) for a in self if a is not None]

    def out_specs(self, tm):
        d = self.w_qkv.shape[1]

        def tile(i):
            return jnp.maximum(i - 1, 0)

        specs = [pl.BlockSpec((None, d, tm), lambda i: (tile(i), 0, 0)),
                 pl.BlockSpec((tm, d), lambda i: (tile(i), 0)),
                 pl.BlockSpec((d, tm), lambda i: (0, tile(i)))]
        if self.w_f is not None:
            specs.append(pl.BlockSpec((tm, LANES), lambda i: (tile(i), 0)))
        return specs

    def out_shapes(self, n, tm):
        d = self.w_qkv.shape[1]
        shapes = [jax.ShapeDtypeStruct((n // tm, d, tm), BF16),
                  jax.ShapeDtypeStruct((n, d), BF16), jax.ShapeDtypeStruct((d, n), BF16)]
        if self.w_f is not None:
            shapes.append(jax.ShapeDtypeStruct((n, LANES), F32))
        return shapes


def _chunks(total, want):
    step = want if total % want == 0 else total
    return [slice(c * step, (c + 1) * step) for c in range(total // step)]


def _projection_pieces(u_of, mixer_refs, out_refs):
    wt_ref = mixer_refs[1]
    d = wt_ref.shape[1]

    def contract_last(a, b):
        return lax.dot_general(a, b, (((1,), (1,)), ((), ())), preferred_element_type=F32)

    def transposed(out_ref, first_row, rows):
        def piece():
            w_rows = wt_ref[first_row + rows.start:first_row + rows.stop, :]
            out_ref[rows, :] = contract_last(w_rows, u_of()).astype(BF16)
        return piece

    def keys(cols):
        def piece():
            out_refs[1][:, cols] = contract_last(
                u_of(), wt_ref[d + cols.start:d + cols.stop, :]).astype(BF16)
        return piece

    def forget_gates():
        wft_ref, bf_ref = mixer_refs[2:]
        f_logit = contract_last(u_of(), wft_ref[...].astype(BF16)) + bf_ref[...]
        out_refs[3][...] = _log_sigmoid(f_logit)

    chunks = _chunks(d, PROJ_COL_CHUNK)
    pieces = [transposed(out_refs[0], 0, rows) for rows in chunks]
    pieces += [keys(cols) for cols in chunks]
    pieces += [transposed(out_refs[2], 2 * d, rows) for rows in chunks]
    return pieces + ([forget_gates] if len(mixer_refs) > 2 else [])


def _skewed_projection(stages, u_ref, mixer_refs, out_refs):
    step = pl.program_id(0)

    @pl.when(step == 0)
    def _():
        u_ref[1] = jnp.zeros(u_ref.shape[1:], u_ref.dtype)

    def body(slot):
        pieces = _projection_pieces(lambda: u_ref[1 - slot], mixer_refs, out_refs)
        for k in range(max(len(pieces), len(stages))):
            if k < len(pieces):
                pieces[k]()
            if k < len(stages):
                stages[k](slot)

    for slot in range(2):
        pl.when(step % 2 == slot)(functools.partial(body, slot))


def _split_refs(refs, n_mixer, n_cast):
    n_out = len(refs) - 1 - n_cast
    return (refs[:n_mixer], refs[n_mixer:n_mixer + n_cast],
            refs[n_mixer + n_cast:n_out], refs[n_out:-1], refs[-1])


def _proj_kernel(x_ref, *refs, n_mixer, jobs):
    mixer_refs, cast_in, out_refs, cast_out, u_ref = _split_refs(refs, n_mixer, len(jobs))

    def normalise(slot):
        u_ref[slot] = _rmsnorm(x_ref[...], mixer_refs[0][...]).astype(BF16)

    def cast(slot):
        _cast_rows(jobs, cast_in, cast_out)

    _skewed_projection([normalise, cast], u_ref, mixer_refs, out_refs)


def _skewed_rows(n, tm, d):
    last = n // tm - 1
    return ((n // tm + 1,), pl.BlockSpec((tm, d), lambda i: (jnp.minimum(i, last), 0)),
            [pltpu.VMEM((2, tm, d), BF16)])


def _project(h2, mixer, casts, *, tm):
    n, d = h2.shape
    n_mixer = len(mixer.operands())
    cast_in, cast_out, cast_shapes, jobs = _cast_plans(casts, n // tm)
    grid, row, scratch = _skewed_rows(n, tm, d)
    outs = pl.pallas_call(
        functools.partial(_proj_kernel, n_mixer=n_mixer, jobs=jobs),
        grid=grid,
        in_specs=[row] + mixer.in_specs() + cast_in,
        out_specs=mixer.out_specs(tm) + cast_out,
        out_shape=mixer.out_shapes(n, tm) + cast_shapes, scratch_shapes=scratch,
        compiler_params=_params(1), name="proj")(
            h2, *mixer.operands(), *(c.w.stack for c in casts))
    return outs[:len(outs) - len(casts)], outs[len(outs) - len(casts):]


def _cumsum_kernel(lf_ref, aqt_ref, ak_ref, carry_ref):
    @pl.when(pl.program_id(1) == 0)
    def _():
        carry_ref[...] = jnp.zeros_like(carry_ref)

    t = lf_ref.shape[1]
    row = lax.broadcasted_iota(jnp.int32, (t, t), 0)
    col = lax.broadcasted_iota(jnp.int32, (t, t), 1)
    tril = jnp.where(col <= row, 1.0, 0.0).astype(BF16)
    x1, x2, x3 = _split3(lf_ref[0])
    cs = (jnp.dot(tril, x1, preferred_element_type=F32)
          + jnp.dot(tril, x2, preferred_element_type=F32)
          + jnp.dot(tril, x3, preferred_element_type=F32))
    cs = cs + carry_ref[0:1, :]
    carry_ref[...] = jnp.broadcast_to(cs[t - 1:t, :], carry_ref.shape)

    src = lax.broadcasted_iota(jnp.int32, (LANES, LANES), 0)
    dst = lax.broadcasted_iota(jnp.int32, (LANES, LANES), 1)
    lane = lax.broadcasted_iota(jnp.int32, (1, LANES), 1) & (GATE_LANES - 1)
    aq = jnp.where((lane >= SPLIT_TERMS) & (lane < 2 * SPLIT_TERMS), 1.0, 0.0)
    ak = jnp.where(lane < SPLIT_TERMS, 1.0, 0.0)
    for i, part in enumerate(_split3(cs * LOG2E)):
        to_q = jnp.where(dst == GATE_LANES * src + i, 1.0, 0.0).astype(BF16)
        to_k = jnp.where(dst == GATE_LANES * src + SPLIT_TERMS + i, 1.0, 0.0).astype(BF16)
        aq = aq + jnp.dot(part, to_q, preferred_element_type=F32)
        ak = ak - jnp.dot(part, to_k, preferred_element_type=F32)
    eye = jnp.where(src == dst, 1.0, 0.0).astype(BF16)
    aqt_ref[0] = lax.dot_general(eye, aq.astype(BF16), (((1,), (1,)), ((), ())),
                                 preferred_element_type=F32).astype(BF16)
    ak_ref[0] = ak.astype(BF16)


def _cumsum(lf3):
    b, s, _ = lf3.shape
    t = min(CUMSUM_BLK, s)
    nt = s // t
    spec = pl.BlockSpec((1, t, LANES), lambda i, j: (i, j, 0))
    return pl.pallas_call(
        _cumsum_kernel, grid=(b, nt),
        in_specs=[spec],
        out_specs=[pl.BlockSpec((1, LANES, t), lambda i, j: (i * nt + j, 0, 0)), spec],
        out_shape=[jax.ShapeDtypeStruct((b * nt, LANES, t), BF16),
                   jax.ShapeDtypeStruct(lf3.shape, BF16)],
        scratch_shapes=[pltpu.VMEM((F32_SUBLANES, LANES), F32)],
        compiler_params=_params(2), name="gate_cumsum")(lf3)


def _fill_value_rows(vt_ref, v_ref, blk):
    row = lax.broadcasted_iota(jnp.int32, (V_ROWS - HEAD_DIM, blk), 0)
    tail = jnp.where(row == 0, 1.0, 0.0).astype(BF16)
    for hh in range(2):
        for jb in range(v_ref.shape[1] // blk):
            head_rows = v_ref[hh * HEAD_DIM:(hh + 1) * HEAD_DIM, jb * blk:(jb + 1) * blk]
            vt_ref[hh, jb] = jnp.concatenate([head_rows, tail], axis=0)


def _flash_init(m_ref, acc_ref):
    m_ref[...] = jnp.full(m_ref.shape, NEG_INF, F32)
    acc_ref[...] = jnp.zeros(acc_ref.shape, F32)


def _pipelined_blocks(n, scores_into, update, final_scores, final_update, next_tile_scores):
    def step(j_next, j, slot):
        for hh in range(2):
            scores_into(j_next, 1 - slot, hh)
            update(j, slot, hh)

    def last_update_and_final(j, slot):
        for hh in range(2):
            final_scores(hh)
            update(j, slot, hh)
        next_tile_scores()
        final_update()

    def pair(jj, carry):
        j = 2 * jj
        step(j + 1, j, 0)
        step(j + 2, j + 1, 1)
        return carry
    lax.fori_loop(0, jnp.maximum(n - 1, 0) // 2, pair, 0)

    @pl.when(n % 2 == 1)
    def _():
        last_update_and_final(n - 1, 0)

    @pl.when((n % 2 == 0) & (n > 0))
    def _():
        step(n - 1, n - 2, 0)
        last_update_and_final(n - 1, 1)

    @pl.when(n == 0)
    def _():
        for hh in range(2):
            final_scores(hh)
        next_tile_scores()
        final_update()


def _store_scores(s_t, s_ref, mx_ref, idx):
    s_ref[idx + (slice(None), slice(0, s_t.shape[1]))] = s_t
    mx_ref[idx] = jnp.max(s_t, axis=0, keepdims=True)


def _flash_update(s_t, mx, v_rows, m_ref, acc_ref, hh, keep=None, const=None,
                  cols=slice(None)):
    m_old = m_ref[hh, :, cols]
    if const is not None:
        mx = mx + const
    if keep is not None:
        mx = jnp.where(keep, mx, NEG_INF)
    m_new = jnp.maximum(m_old, mx)
    m_safe = jnp.where(m_new == NEG_INF, 0.0, m_new)
    shift = m_safe if const is None else m_safe - const
    if keep is not None:
        shift = jnp.where(keep, shift, float("inf"))
    p = jnp.exp2(s_t - shift).astype(BF16)
    alpha = jnp.exp2(m_old - m_safe)
    m_ref[hh, :, cols] = m_new
    if not isinstance(v_rows, (list, tuple)):
        v_rows = [v_rows]
    keys = p.shape[0] // len(v_rows)
    acc = alpha * acc_ref[hh, :, cols]
    for i, v_i in enumerate(v_rows):
        acc = acc + jnp.dot(v_i, p[i * keys:(i + 1) * keys], preferred_element_type=F32)
    acc_ref[hh, :, cols] = acc


def _head_outputs(acc_ref):
    outs = []
    for hh in range(2):
        acc = acc_ref[hh]
        outs.append(acc[:HEAD_DIM] / acc[HEAD_DIM:HEAD_DIM + 1])
    return jnp.concatenate(outs, axis=0).astype(BF16)


def _for_each_query_tile(nq, tile):
    def body(qi, carry):
        tile(qi)
        return carry
    lax.fori_loop(0, nq, body, 0)


def _head_row_mask(hh):
    row = lax.broadcasted_iota(jnp.int32, (LANES, 1), 0)
    return (row < HEAD_DIM) if hh == 0 else (row >= HEAD_DIM)


def _fox_kernel(qt_ref, k_ref, v_ref, *refs):
    vt_ref = refs[3]
    nq, _, blk = qt_ref.shape
    _fill_value_rows(vt_ref, v_ref, blk)
    hp = pl.program_id(1)
    _for_each_query_tile(nq, lambda qi: _fox_tile(qi, hp, qt_ref, k_ref, *refs))


def _fox_tile(qi, hp, qt_ref, k_ref, aqt_ref, ak_ref, o_ref,
              vt_ref, s_ref, mx_ref, diag_ref, m_ref, acc_ref):
    nq, _, blk = qt_ref.shape
    gate_row = lax.broadcasted_iota(jnp.int32, (LANES, 1), 0)

    def query_operands(tile):
        qt, aqt = qt_ref[tile], aqt_ref[tile]
        out = []
        for hh in range(2):
            first = GATE_LANES * (2 * hp + hh)
            own_gate = (gate_row >= first) & (gate_row < first + GATE_LANES)
            q_rows = jnp.where(_head_row_mask(hh), qt, jnp.zeros_like(qt))
            g_rows = jnp.where(own_gate, aqt, jnp.zeros_like(aqt))
            out.append(jnp.concatenate([q_rows, g_rows], axis=0))
        return out

    def block_scores(w, j, hh):
        rows = pl.ds(pl.multiple_of(j * blk, blk), blk)
        keys = jnp.concatenate([k_ref[0, rows, :], ak_ref[0, rows, :]], axis=1)
        return jnp.dot(keys, w[hh], preferred_element_type=F32)

    w_q = query_operands(qi)

    def scores_into(j, slot, hh):
        _store_scores(block_scores(w_q, j, hh), s_ref, mx_ref, (slot, hh))

    def next_tile_scores():
        w_next = query_operands(jnp.minimum(qi + 1, nq - 1))
        for hh in range(2):
            _store_scores(block_scores(w_next, 0, hh), s_ref, mx_ref, (0, hh))

    def update(j, slot, hh):
        _flash_update(s_ref[slot, hh, :, :blk], mx_ref[slot, hh], vt_ref[hh, j],
                      m_ref, acc_ref, hh)

    def diagonal_scores(hh):
        diag_ref[hh] = block_scores(w_q, qi, hh)

    def diagonal_update():
        half = blk // 2
        lo, hi = slice(0, half), slice(half, blk)
        causal_lo = (lax.broadcasted_iota(jnp.int32, (half, half), 0)
                     <= lax.broadcasted_iota(jnp.int32, (half, half), 1))
        causal_hi = (lax.broadcasted_iota(jnp.int32, (blk, half), 0)
                     <= lax.broadcasted_iota(jnp.int32, (blk, half), 1) + half)
        for hh in range(2):
            v_rows = vt_ref[hh, qi]
            s_lo = jnp.where(causal_lo, diag_ref[hh, lo, lo], NEG_INF)
            _flash_update(s_lo, jnp.max(s_lo, axis=0, keepdims=True),
                          v_rows[:, lo], m_ref, acc_ref, hh, cols=lo)
            s_hi = jnp.where(causal_hi, diag_ref[hh, :, hi], NEG_INF)
            _flash_update(s_hi, jnp.max(s_hi, axis=0, keepdims=True),
                          v_rows, m_ref, acc_ref, hh, cols=hi)

    _flash_init(m_ref, acc_ref)
    _pipelined_blocks(qi, scores_into, update, diagonal_scores, diagonal_update,
                      next_tile_scores)
    o_ref[qi] = _head_outputs(acc_ref)


def _fox_attention(qt3, k3, vt, aqt3, ak3, n_heads):
    b, s, d = k3.shape
    assert n_heads * GATE_LANES <= LANES
    blk = qt3.shape[2]
    assert aqt3.shape[2] == blk and s % blk == 0
    npair = n_heads // 2
    nq = s // blk
    head_pair_tiles = pl.BlockSpec((nq, LANES, blk), lambda bi, hp: (bi, hp, 0))
    return pl.pallas_call(
        _fox_kernel, grid=(b, npair),
        in_specs=[
            head_pair_tiles,
            pl.BlockSpec((1, s, LANES), lambda bi, hp: (bi, 0, hp)),
            pl.BlockSpec((LANES, s), lambda bi, hp: (hp, bi)),
            pl.BlockSpec((nq, LANES, blk), lambda bi, hp: (bi, 0, 0)),
            pl.BlockSpec((1, s, LANES), lambda bi, hp: (bi, 0, 0)),
        ],
        out_specs=head_pair_tiles,
        out_shape=jax.ShapeDtypeStruct(qt3.shape, BF16),
        scratch_shapes=[
            pltpu.VMEM((2, s // blk, V_ROWS, blk), BF16),
            pltpu.VMEM((2, 2, blk, blk + SCORE_PITCH_PAD), F32),
            pltpu.VMEM((2, 2, 1, blk), F32),
            pltpu.VMEM((2, blk, blk), F32),
            pltpu.VMEM((2, 1, blk), F32),
            pltpu.VMEM((2, V_ROWS, blk), F32),
        ],
        compiler_params=_params(2), name="fox_attention")(qt3, k3, vt, aqt3, ak3)


def _t5_bucket_np(n):
    max_exact = REL_BUCKETS // 2
    nf = np.maximum(n, 1).astype(np.float64)
    large = max_exact + (np.log(nf / max_exact) / math.log(REL_MAX_DIST / max_exact)
                         * (REL_BUCKETS - max_exact)).astype(np.int32)
    return np.where(n < max_exact, n, np.minimum(large, REL_BUCKETS - 1)).astype(np.int32)


def _bucket_tiles(blk):
    key = np.arange(blk)[:, None]
    qry = np.arange(blk)[None, :]
    own = np.where(key <= qry, _t5_bucket_np(np.maximum(qry - key, 0)), -1)
    prev = _t5_bucket_np(blk + qry - key)
    return np.stack([own, prev]).astype(np.int32)


def _bias_kernel(tab_ref, bucket_ref, o_ref):
    h = pl.program_id(0)
    bucket = bucket_ref[...]
    acc = jnp.where(bucket < 0, NEG_INF, 0.0).astype(F32)
    for bkt in range(REL_BUCKETS):
        acc = jnp.where(bucket == bkt, tab_ref[bkt, h] * LOG2E, acc)
    o_ref[0] = acc


def _bias_tiles(rel_table, blk):
    n_heads = rel_table.shape[1]
    buckets = jnp.asarray(_bucket_tiles(blk))
    return pl.pallas_call(
        _bias_kernel, grid=(n_heads,),
        in_specs=[pl.BlockSpec(memory_space=pltpu.SMEM),
                  pl.BlockSpec((2, blk, blk), lambda h: (0, 0, 0))],
        out_specs=pl.BlockSpec((1, 2, blk, blk), lambda h: (h, 0, 0, 0)),
        out_shape=jax.ShapeDtypeStruct((n_heads, 2, blk, blk), F32),
        compiler_params=_params(1), name="t5_bias_tiles")(rel_table, buckets)


def _select_blocks(qt_ref, km_ref, sel_ref):
    nq, _, tq = qt_ref.shape
    blk = MOBA_BLOCK
    nblk = km_ref.shape[0]
    assert blk & (blk - 1) == 0
    qt_all = jnp.concatenate([qt_ref[t] for t in range(nq)], axis=1)
    km_parts = _split3(km_ref[...])
    blk_id = lax.broadcasted_iota(jnp.int32, (nblk, nq * tq), 0)
    own = lax.shift_right_logical(lax.broadcasted_iota(jnp.int32, (nblk, nq * tq), 1),
                                  blk.bit_length() - 1)
    past = blk_id < own
    for hh in range(2):
        q_h = jnp.where(_head_row_mask(hh), qt_all, jnp.zeros_like(qt_all))
        gate = None
        for part in km_parts:
            term = jnp.dot(part, q_h, preferred_element_type=F32)
            gate = term if gate is None else gate + term
        work = jnp.where(past, gate, NEG_INF)
        picked = jnp.zeros(gate.shape, F32)
        for _ in range(MOBA_TOP_K):
            best = jnp.max(work, axis=0, keepdims=True)
            first = jnp.min(jnp.where(work == best, blk_id, nblk), axis=0, keepdims=True)
            hit = blk_id == first
            picked = jnp.where(hit, 1.0, picked)
            work = jnp.where(hit, NEG_INF, work)
        keep_all = ((picked > 0.5) & past) | (blk_id == own) | (blk_id == (own | 1))
        keep_f = jnp.where(keep_all, 1.0, 0.0)
        for t in range(nq):
            sel_ref[hh, t] = keep_f[:, t * tq:(t + 1) * tq]


def _moba_kernel(tab_ref, qt_ref, k_ref, v_ref, bias_ref, o_ref,
                 vt_ref, km_ref, sel_ref, *scratch):
    blk = MOBA_BLOCK

    def block_mean(jb, carry):
        rows = pl.ds(pl.multiple_of(jb * blk, blk), blk)
        km_ref[pl.ds(jb, 1), :] = jnp.mean(k_ref[0, rows, :].astype(F32),
                                            axis=0, keepdims=True)
        return carry
    lax.fori_loop(0, k_ref.shape[1] // blk, block_mean, 0)
    _fill_value_rows(vt_ref, v_ref, blk)
    _select_blocks(qt_ref, km_ref, sel_ref)
    hp = pl.program_id(1)
    _for_each_query_tile(
        qt_ref.shape[0],
        lambda qi: _moba_tile(qi, hp, tab_ref, qt_ref, k_ref, bias_ref, o_ref,
                              vt_ref, sel_ref, *scratch))


def _moba_tile(qi, hp, tab_ref, qt_ref, k_ref, bias_ref, o_ref,
               vt_ref, sel_ref, s_ref, mx_ref, near_ref, own1_ref, m_ref, acc_ref):
    blk = MOBA_BLOCK
    nq, _, tq = qt_ref.shape
    assert tq == 2 * blk and blk >= REL_MAX_DIST
    first_own = 2 * qi

    def head_queries(tile):
        qt = qt_ref[tile]
        return [jnp.where(_head_row_mask(hh), qt, jnp.zeros_like(qt)) for hh in range(2)]

    q_m_t = head_queries(qi)

    def scores(hh, j, queries=q_m_t):
        rows = pl.ds(pl.multiple_of(j * blk, blk), blk)
        return jnp.dot(k_ref[0, rows, :], queries[hh], preferred_element_type=F32)

    def next_tile_scores():
        q_next = head_queries(jnp.minimum(qi + 1, nq - 1))
        for hh in range(2):
            _store_scores(scores(hh, 0, q_next), s_ref, mx_ref, (0, hh))

    def keep(hh, j):
        return sel_ref[hh, qi, pl.ds(j, 1), :] > 0.5

    _flash_init(m_ref, acc_ref)
    far_bias = [tab_ref[REL_BUCKETS - 1, 2 * hp + hh] * LOG2E for hh in range(2)]

    j_prev = jnp.maximum(first_own - 1, 0)
    lo, hi = slice(0, blk), slice(blk, 2 * blk)
    near_mx = {}

    def near_scores(hh):
        own_t, prev_t = bias_ref[hh, 0], bias_ref[hh, 1]
        mask_prev = jnp.where(keep(hh, j_prev) & (qi >= 1), 0.0, NEG_INF)
        mask_own = jnp.where(keep(hh, first_own), 0.0, NEG_INF)
        far_t = jnp.full((blk, blk), far_bias[hh], F32)
        parts = [scores(hh, j_prev) + mask_prev + jnp.concatenate([prev_t, far_t], axis=1),
                 scores(hh, first_own) + mask_own + jnp.concatenate([own_t, prev_t], axis=1)]
        mx = None
        for i, part in enumerate(parts):
            near_ref[hh, i * blk:(i + 1) * blk] = part
            part_mx = jnp.max(part, axis=0, keepdims=True)
            mx = part_mx if mx is None else jnp.maximum(mx, part_mx)
        rows = pl.ds(pl.multiple_of((first_own + 1) * blk, blk), blk)
        last = jnp.dot(k_ref[0, rows, :], q_m_t[hh][:, hi], preferred_element_type=F32) + own_t
        own1_ref[hh] = last
        near_mx[hh] = (mx[:, lo], jnp.maximum(mx[:, hi], jnp.max(last, axis=0, keepdims=True)))

    def near_update():
        for hh in range(2):
            v_near = [vt_ref[hh, j_prev], vt_ref[hh, first_own], vt_ref[hh, first_own + 1]]
            _flash_update(near_ref[hh, :, lo], near_mx[hh][0], v_near[:2],
                          m_ref, acc_ref, hh, cols=lo)
            _flash_update(jnp.concatenate([near_ref[hh, :, hi], own1_ref[hh]], axis=0),
                          near_mx[hh][1], v_near, m_ref, acc_ref, hh, cols=hi)

    def scores_into(j, slot, hh):
        _store_scores(scores(hh, j), s_ref, mx_ref, (slot, hh))

    def update(j, slot, hh):
        _flash_update(s_ref[slot, hh, :, :tq], mx_ref[slot, hh], vt_ref[hh, j],
                      m_ref, acc_ref, hh, keep=keep(hh, j), const=far_bias[hh])

    _pipelined_blocks(jnp.maximum(first_own - 1, 0), scores_into, update,
                      near_scores, near_update, next_tile_scores)
    o_ref[qi] = _head_outputs(acc_ref)


def _moba_attention(qt3, k3, vt, rel_table, bias_t, n_heads):
    b, s, d = k3.shape
    blk = MOBA_BLOCK
    tq = qt3.shape[2]
    assert tq == MOBA_Q_TILE and s % tq == 0
    npair = n_heads // 2
    nblk = s // blk
    nq = s // tq
    head_pair_tiles = pl.BlockSpec((nq, LANES, tq), lambda bi, hp: (bi, hp, 0))
    return pl.pallas_call(
        _moba_kernel, grid=(b, npair),
        in_specs=[
            pl.BlockSpec(memory_space=pltpu.SMEM),
            head_pair_tiles,
            pl.BlockSpec((1, s, LANES), lambda bi, hp: (bi, 0, hp)),
            pl.BlockSpec((LANES, s), lambda bi, hp: (hp, bi)),
            pl.BlockSpec((2, 2, blk, blk), lambda bi, hp: (hp, 0, 0, 0)),
        ],
        out_specs=head_pair_tiles,
        out_shape=jax.ShapeDtypeStruct(qt3.shape, BF16),
        scratch_shapes=[
            pltpu.VMEM((2, nblk, V_ROWS, blk), BF16),
            pltpu.VMEM((nblk, LANES), F32),
            pltpu.VMEM((2, nq, nblk, tq), F32),
            pltpu.VMEM((2, 2, blk, tq + SCORE_PITCH_PAD), F32),
            pltpu.VMEM((2, 2, 1, tq), F32),
            pltpu.VMEM((2, 2 * blk, tq), F32),
            pltpu.VMEM((2, blk, blk), F32),
            pltpu.VMEM((2, 1, tq), F32),
            pltpu.VMEM((2, V_ROWS, tq), F32),
        ],
        compiler_params=_params(2), name="moba_attention")(
            rel_table, qt3, k3, vt, bias_t)


class _Ple(NamedTuple):
    g: jax.Array
    w_gate: jax.Array
    p: _Slab
    w_up: jax.Array

    def in_specs(self, tm):
        layer, last = self.p.layer, self.p.shape[0] // tm - 1
        return [_resident(self.g), _resident(self.w_gate),
                pl.BlockSpec((None, tm, self.p.shape[1]),
                             lambda i: (layer, jnp.minimum(i, last), 0)),
                _resident(self.w_up)]

    def operands(self):
        return tuple(_array(a) for a in self)


def _ple_update(x, g_ref, wg_ref, p_ref, wu_ref):
    u = _rmsnorm(x, g_ref[...]).astype(BF16)
    gate = jax.nn.sigmoid(jnp.dot(u, wg_ref[...], preferred_element_type=F32))
    up = jnp.dot(p_ref[...].astype(BF16), wu_ref[...], preferred_element_type=F32)
    return x + gate * up


def _oproj_ffn_kernel(h_ref, ot_ref, wo_ref, g_ref, win_ref, wout_ref, *rest, tf):
    d_ff = wout_ref.shape[0]
    h1 = h_ref[...] + lax.dot_general(ot_ref[...], wo_ref[...], (((0,), (0,)), ((), ())),
                                      preferred_element_type=F32)
    u = _rmsnorm(h1, g_ref[...]).astype(BF16)
    acc = h1
    for c in range(d_ff // tf):
        gate = jnp.dot(u, win_ref[:, c * tf:(c + 1) * tf], preferred_element_type=F32)
        up = jnp.dot(u, win_ref[:, d_ff + c * tf:d_ff + (c + 1) * tf],
                     preferred_element_type=F32)
        act = (gate * jax.nn.sigmoid(gate) * up).astype(BF16)
        acc = acc + jnp.dot(act, wout_ref[c * tf:(c + 1) * tf, :],
                            preferred_element_type=F32)
    *ple_refs, out_ref = rest
    if ple_refs:
        *ple_refs, final_g_ref = ple_refs
        acc = _rmsnorm(_ple_update(acc, *ple_refs), final_g_ref[...])
    out_ref[...] = acc


def _oproj_ffn(h2, o_t, w_o, g, w_in, w_out, *, tm, tf, last=None):
    n, d = h2.shape
    d_ff = w_out.shape[0]
    assert d_ff % tf == 0 and o_t.shape == (n // tm, d, tm)
    row = pl.BlockSpec((tm, d), lambda i: (i, 0))
    weights = (w_o, g, w_in, w_out)
    in_specs = ([row, pl.BlockSpec((None, d, tm), lambda i: (i, 0, 0))]
                + [_resident(w) for w in weights])
    operands = (h2, o_t, *map(_array, weights))
    if last is not None:
        ple, final_g = last
        in_specs += ple.in_specs(tm) + [_resident(final_g)]
        operands += ple.operands() + (final_g,)
    return pl.pallas_call(
        functools.partial(_oproj_ffn_kernel, tf=tf), grid=(n // tm,),
        in_specs=in_specs, out_specs=row,
        out_shape=jax.ShapeDtypeStruct((n, d), F32),
        compiler_params=_params(1), name="oproj_ffn")(*operands)


def _ple_next_kernel(x_ref, g_ref, wg_ref, p_ref, wu_ref, *refs, n_mixer, jobs):
    mixer_refs, cast_in, (y_ref, *out_refs), cast_out, u_ref = _split_refs(
        refs, n_mixer, len(jobs))
    held = {}

    def normalise(slot):
        held["u"] = _rmsnorm(x_ref[...], g_ref[...]).astype(BF16)

    def gate(slot):
        held["gate"] = jax.nn.sigmoid(
            jnp.dot(held["u"], wg_ref[...], preferred_element_type=F32))

    def update(slot):
        up = jnp.dot(p_ref[...].astype(BF16), wu_ref[...], preferred_element_type=F32)
        y_ref[...] = x_ref[...] + held["gate"] * up

    def normalise_next(slot):
        u_ref[slot] = _rmsnorm(y_ref[...], mixer_refs[0][...]).astype(BF16)

    def cast(slot):
        _cast_rows(jobs, cast_in, cast_out)

    _skewed_projection([normalise, gate, update, normalise_next, cast], u_ref, mixer_refs,
                       out_refs)


def _ple_next(h2, ple, mixer, casts, *, tm):
    n, d = h2.shape
    n_mixer = len(mixer.operands())
    cast_in, cast_out, cast_shapes, jobs = _cast_plans(casts, n // tm)
    grid, row, scratch = _skewed_rows(n, tm, d)
    h, *outs = pl.pallas_call(
        functools.partial(_ple_next_kernel, n_mixer=n_mixer, jobs=jobs),
        grid=grid,
        in_specs=[row] + ple.in_specs(tm) + mixer.in_specs() + cast_in,
        out_specs=[row] + mixer.out_specs(tm) + cast_out,
        out_shape=[jax.ShapeDtypeStruct((n, d), F32)] + mixer.out_shapes(n, tm) + cast_shapes,
        scratch_shapes=scratch,
        compiler_params=_params(1), name="ple_proj")(
            h2, *ple.operands(), *mixer.operands(), *(c.w.stack for c in casts))
    return h, outs[:len(outs) - len(casts)], outs[len(outs) - len(casts):]


def _row_tile(n, want):
    t = min(want, n)
    assert n % t == 0
    return t


def _col_tile(n, want):
    t = min(want, n)
    while n % t:
        t -= LANES
    return t


def kernel(x, p, attn_norm_g, fox_w_in, fox_b_f, fox_w_o, moba_w_in, moba_w_o, rel_bias_table,
           ffn_norm_g, ffn_w_in, ffn_w_out, ple_norm_g, ple_w_gate, ple_w_up, final_norm_g):
    b, s, d = x.shape
    depth = p.shape[0]
    n_heads = rel_bias_table.shape[1]
    assert d == n_heads * HEAD_DIM and n_heads % 2 == 0 and n_heads <= LANES
    n = b * s
    tm = _row_tile(n, DENSE_ROW_TILE)
    tf = _col_tile(ffn_w_out.shape[1], FFN_COL_CHUNK)

    def row_vec(v):
        return v.reshape(1, -1).astype(F32)

    w_o_mix = (fox_w_o, moba_w_o)
    p3 = p.reshape(depth, n, -1)
    fox_wt = jnp.swapaxes(fox_w_in, 1, 2)

    def qkv_cast(i):
        if i % 2 == 0:
            return _Cast(_Slab(fox_wt, i // 2), 3 * d, scaled=d)
        return _Cast(_Slab(moba_w_in, i // 2), d, scaled=d, transpose=True)

    def layer_casts(i):
        jobs = [_Cast(_Slab(w, layer), w.shape[1]) for w, layer in (
            (w_o_mix[i % 2], i // 2), (ffn_w_in, i), (ffn_w_out, i), (ple_w_gate, i),
            (ple_w_up, i))]
        return jobs + ([qkv_cast(i + 1)] if i + 1 < depth else [])

    def mixer(i, w_qkv):
        parts = [row_vec(attn_norm_g[i]), w_qkv]
        if i % 2 == 0:
            pad_heads = LANES - n_heads
            parts += [jnp.pad(fox_wt[i // 2, 3 * d:], ((0, pad_heads), (0, 0))),
                      jnp.pad(row_vec(fox_b_f[i // 2]), ((0, 0), (0, pad_heads)))]
        return _Mixer(*parts)

    rel_table = rel_bias_table.astype(F32)
    bias_t = _bias_tiles(rel_table, MOBA_BLOCK)

    h = x.reshape(n, d).astype(F32)
    row = lax.broadcasted_iota(jnp.int32, (3 * d, 1), 0)
    w_qkv = (fox_wt[0, :3 * d] * jnp.where(row < d, Q_SCALE, 1.0)).astype(BF16)
    mixed, weights = _project(h, mixer(0, w_qkv), layer_casts(0), tm=tm)
    for i in range(depth):
        w_o, w_ffn_in, w_ffn_out, w_ple_gate, w_ple_up, *w_qkv_next = weights
        qt, k3, vt = mixed[0], mixed[1].reshape(b, s, d), mixed[2]
        if i % 2 == 0:
            aqt, ak = _cumsum(mixed[3].reshape(b, s, LANES))
            o_t = _fox_attention(qt, k3, vt, aqt, ak, n_heads)
        else:
            o_t = _moba_attention(qt, k3, vt, rel_table, bias_t, n_heads)
        ple = _Ple(row_vec(ple_norm_g[i]), w_ple_gate, _Slab(p3, i), w_ple_up)
        last = (ple, row_vec(final_norm_g)) if i + 1 == depth else None
        h = _oproj_ffn(h, o_t, w_o, row_vec(ffn_norm_g[i]), w_ffn_in, w_ffn_out,
                       tm=tm, tf=tf, last=last)
        if last is None:
            h, mixed, weights = _ple_next(h, ple, mixer(i + 1, *w_qkv_next),
                                          layer_casts(i + 1), tm=tm)
    return h.reshape(b, s, d).astype(x.dtype)
```

```python
import functools
import math
from typing import NamedTuple, Optional

import numpy as np
import jax
import jax.numpy as jnp
from jax import lax
from jax.experimental import pallas as pl
from jax.experimental.pallas import tpu as pltpu

F32 = jnp.float32
BF16 = jnp.bfloat16

RMS_EPS = 1e-6
HEAD_DIM = 64
MOBA_BLOCK = 256
MOBA_TOP_K = 3
REL_BUCKETS = 32
REL_MAX_DIST = 128

LANES = 128
F32_SUBLANES = 8
BF16_SUBLANES = 16
V7X_VMEM_BYTES = 64 * 1024 * 1024
V_ROWS = HEAD_DIM + BF16_SUBLANES
SPLIT_TERMS = 3
GATE_LANES = 8
SCORE_PITCH_PAD = LANES
LOG2E = math.log2(math.e)
Q_SCALE = HEAD_DIM ** -0.5 * LOG2E
MOBA_Q_TILE = 2 * MOBA_BLOCK
ATTN_Q_TILE = MOBA_Q_TILE
DENSE_ROW_TILE = ATTN_Q_TILE
CUMSUM_BLK = ATTN_Q_TILE
PROJ_COL_CHUNK = 512
FFN_COL_CHUNK = 256
VMEM_LIMIT_BYTES = V7X_VMEM_BYTES // 8 * 7
NEG_INF = float("-inf")


def _params(n_axes):
    return pltpu.CompilerParams(
        dimension_semantics=("arbitrary",) * n_axes,
        vmem_limit_bytes=VMEM_LIMIT_BYTES)


def _split3(x):
    x1 = x.astype(BF16)
    r1 = x - x1.astype(F32)
    x2 = r1.astype(BF16)
    x3 = (r1 - x2.astype(F32)).astype(BF16)
    return x1, x2, x3


def _rmsnorm(x, g):
    ms = jnp.mean(x * x, axis=-1, keepdims=True)
    return x * lax.rsqrt(ms + RMS_EPS) * g


def _log_sigmoid(x):
    return jnp.minimum(x, 0.0) - jnp.log1p(jnp.exp(-jnp.abs(x)))


class _Slab(NamedTuple):
    stack: jax.Array
    layer: int

    @property
    def shape(self):
        return self.stack.shape[1:]


def _resident(x):
    if isinstance(x, _Slab):
        layer, zeros = x.layer, (0,) * len(x.shape)
        return pl.BlockSpec((None,) + x.shape, lambda i: (layer,) + zeros,
                            pipeline_mode=pl.Buffered(1))
    zeros = (0,) * x.ndim
    return pl.BlockSpec(x.shape, lambda i: zeros, pipeline_mode=pl.Buffered(1))


def _array(x):
    return x.stack if isinstance(x, _Slab) else x


class _Cast(NamedTuple):
    w: _Slab
    rows: int
    scaled: int = 0
    transpose: bool = False

    def plan(self, steps):
        cols = self.w.shape[1]
        min_rows = LANES if self.transpose else BF16_SUBLANES
        assert self.rows % min_rows == 0 and cols % LANES == 0
        chunks = math.gcd(steps, self.rows // min_rows)
        span, layer, chunk = steps // chunks, self.w.layer, self.rows // chunks

        def at(i):
            return jnp.minimum(i, steps - 1) // span

        in_spec = pl.BlockSpec((None, chunk, cols), lambda i: (layer, at(i), 0))
        if self.transpose:
            return (in_spec, pl.BlockSpec((cols, chunk), lambda i: (0, at(i))),
                    jax.ShapeDtypeStruct((cols, self.rows), BF16), span)
        return (in_spec, pl.BlockSpec((chunk, cols), lambda i: (at(i), 0)),
                jax.ShapeDtypeStruct((self.rows, cols), BF16), span)


def _cast_plans(casts, steps):
    plans = [c.plan(steps) for c in casts]
    return ([p[0] for p in plans], [p[1] for p in plans], [p[2] for p in plans],
            tuple((c.transpose, c.scaled, p[3], steps) for c, p in zip(casts, plans)))


def _cast_rows(jobs, in_refs, out_refs):
    for (transpose, scaled, span, steps), in_ref, out_ref in zip(jobs, in_refs, out_refs):
        w = in_ref[...].T if transpose else in_ref[...]
        if scaled:
            chunk = jnp.minimum(pl.program_id(0), steps - 1) // span
            first_row = 0 if transpose else chunk * w.shape[0]
            row = first_row + lax.broadcasted_iota(jnp.int32, (w.shape[0], 1), 0)
            w = w * jnp.where(row < scaled, Q_SCALE, 1.0)
        out_ref[...] = w.astype(BF16)


class _Mixer(NamedTuple):
    g: jax.Array
    w_qkv: jax.Array
    w_f: Optional[jax.Array] = None
    b_f: Optional[jax.Array] = None

    def operands(self):
        return tuple(_array(a) for a in self if a is not None)

    def in_specs(self):
        return [_resident(a) for a in self if a is not None]

    def out_specs(self, tm):
        d = self.w_qkv.shape[1]

        def tile(i):
            return jnp.maximum(i - 1, 0)

        slab = pl.BlockSpec((None, d, tm), lambda i: (tile(i), 0, 0))
        specs = [slab, pl.BlockSpec((tm, d), lambda i: (tile(i), 0)), slab]
        if self.w_f is not None:
            specs.append(pl.BlockSpec((tm, LANES), lambda i: (tile(i), 0)))
        return specs

    def out_shapes(self, n, tm):
        d = self.w_qkv.shape[1]
        slabs = jax.ShapeDtypeStruct((n // tm, d, tm), BF16)
        shapes = [slabs, jax.ShapeDtypeStruct((n, d), BF16), slabs]
        if self.w_f is not None:
            shapes.append(jax.ShapeDtypeStruct((n, LANES), F32))
        return shapes


def _chunks(total, want):
    step = want if total % want == 0 else total
    return [slice(c * step, (c + 1) * step) for c in range(total // step)]


def _projection_pieces(u_of, mixer_refs, out_refs):
    wt_ref = mixer_refs[1]
    d = wt_ref.shape[1]

    def contract_last(a, b):
        return lax.dot_general(a, b, (((1,), (1,)), ((), ())), preferred_element_type=F32)

    def transposed(out_ref, first_row, rows):
        def piece():
            w_rows = wt_ref[first_row + rows.start:first_row + rows.stop, :]
            out_ref[rows, :] = contract_last(w_rows, u_of()).astype(BF16)
        return piece

    def keys(cols):
        def piece():
            out_refs[1][:, cols] = contract_last(
                u_of(), wt_ref[d + cols.start:d + cols.stop, :]).astype(BF16)
        return piece

    def forget_gates():
        wft_ref, bf_ref = mixer_refs[2:]
        f_logit = contract_last(u_of(), wft_ref[...].astype(BF16)) + bf_ref[...]
        out_refs[3][...] = _log_sigmoid(f_logit)

    chunks = _chunks(d, PROJ_COL_CHUNK)
    pieces = [transposed(out_refs[0], 0, rows) for rows in chunks]
    pieces += [keys(cols) for cols in chunks]
    pieces += [transposed(out_refs[2], 2 * d, rows) for rows in chunks]
    return pieces + ([forget_gates] if len(mixer_refs) > 2 else [])


def _skewed_projection(stages, u_ref, mixer_refs, out_refs):
    step = pl.program_id(0)

    @pl.when(step == 0)
    def _():
        u_ref[1] = jnp.zeros(u_ref.shape[1:], u_ref.dtype)

    def body(slot):
        pieces = _projection_pieces(lambda: u_ref[1 - slot], mixer_refs, out_refs)
        for k in range(max(len(pieces), len(stages))):
            if k < len(pieces):
                pieces[k]()
            if k < len(stages):
                stages[k](slot)

    for slot in range(2):
        pl.when(step % 2 == slot)(functools.partial(body, slot))


def _split_refs(refs, n_mixer, n_cast):
    n_out = len(refs) - 1 - n_cast
    return (refs[:n_mixer], refs[n_mixer:n_mixer + n_cast],
            refs[n_mixer + n_cast:n_out], refs[n_out:-1], refs[-1])


def _proj_kernel(x_ref, *refs, n_mixer, jobs):
    mixer_refs, cast_in, out_refs, cast_out, u_ref = _split_refs(refs, n_mixer, len(jobs))

    def normalise(slot):
        u_ref[slot] = _rmsnorm(x_ref[...], mixer_refs[0][...]).astype(BF16)

    def cast(slot):
        _cast_rows(jobs, cast_in, cast_out)

    _skewed_projection([normalise, cast], u_ref, mixer_refs, out_refs)


def _skewed_rows(n, tm, d):
    last = n // tm - 1
    return ((n // tm + 1,), pl.BlockSpec((tm, d), lambda i: (jnp.minimum(i, last), 0)),
            [pltpu.VMEM((2, tm, d), BF16)])


def _project(h2, mixer, casts, *, tm):
    n, d = h2.shape
    n_mixer = len(mixer.operands())
    cast_in, cast_out, cast_shapes, jobs = _cast_plans(casts, n // tm)
    grid, row, scratch = _skewed_rows(n, tm, d)
    outs = pl.pallas_call(
        functools.partial(_proj_kernel, n_mixer=n_mixer, jobs=jobs),
        grid=grid,
        in_specs=[row] + mixer.in_specs() + cast_in,
        out_specs=mixer.out_specs(tm) + cast_out,
        out_shape=mixer.out_shapes(n, tm) + cast_shapes, scratch_shapes=scratch,
        compiler_params=_params(1), name="proj")(
            h2, *mixer.operands(), *(c.w.stack for c in casts))
    return outs[:len(outs) - len(casts)], outs[len(outs) - len(casts):]


def _cumsum_kernel(lf_ref, aqt_ref, ak_ref, carry_ref):
    @pl.when(pl.program_id(1) == 0)
    def _():
        carry_ref[...] = jnp.zeros_like(carry_ref)

    t = lf_ref.shape[1]
    row = lax.broadcasted_iota(jnp.int32, (t, t), 0)
    col = lax.broadcasted_iota(jnp.int32, (t, t), 1)
    tril = jnp.where(col <= row, 1.0, 0.0).astype(BF16)
    x1, x2, x3 = _split3(lf_ref[0])
    cs = (jnp.dot(tril, x1, preferred_element_type=F32)
          + jnp.dot(tril, x2, preferred_element_type=F32)
          + jnp.dot(tril, x3, preferred_element_type=F32))
    cs = cs + carry_ref[0:1, :]
    carry_ref[...] = jnp.broadcast_to(cs[t - 1:t, :], carry_ref.shape)

    src = lax.broadcasted_iota(jnp.int32, (LANES, LANES), 0)
    dst = lax.broadcasted_iota(jnp.int32, (LANES, LANES), 1)
    lane = lax.broadcasted_iota(jnp.int32, (1, LANES), 1) & (GATE_LANES - 1)
    aq = jnp.where((lane >= SPLIT_TERMS) & (lane < 2 * SPLIT_TERMS), 1.0, 0.0)
    ak = jnp.where(lane < SPLIT_TERMS, 1.0, 0.0)
    for i, part in enumerate(_split3(cs * LOG2E)):
        to_q = jnp.where(dst == GATE_LANES * src + i, 1.0, 0.0).astype(BF16)
        to_k = jnp.where(dst == GATE_LANES * src + SPLIT_TERMS + i, 1.0, 0.0).astype(BF16)
        aq = aq + jnp.dot(part, to_q, preferred_element_type=F32)
        ak = ak - jnp.dot(part, to_k, preferred_element_type=F32)
    eye = jnp.where(src == dst, 1.0, 0.0).astype(BF16)
    aqt_ref[0] = lax.dot_general(eye, aq.astype(BF16), (((1,), (1,)), ((), ())),
                                 preferred_element_type=F32).astype(BF16)
    ak_ref[0] = ak.astype(BF16)


def _cumsum(lf3):
    b, s, _ = lf3.shape
    t = min(CUMSUM_BLK, s)
    nt = s // t
    spec = pl.BlockSpec((1, t, LANES), lambda i, j: (i, j, 0))
    return pl.pallas_call(
        _cumsum_kernel, grid=(b, nt),
        in_specs=[spec],
        out_specs=[pl.BlockSpec((1, LANES, t), lambda i, j: (i * nt + j, 0, 0)), spec],
        out_shape=[jax.ShapeDtypeStruct((b * nt, LANES, t), BF16),
                   jax.ShapeDtypeStruct(lf3.shape, BF16)],
        scratch_shapes=[pltpu.VMEM((F32_SUBLANES, LANES), F32)],
        compiler_params=_params(2), name="gate_cumsum")(lf3)


def _fill_value_rows(vt_ref, v_ref, blk):
    row = lax.broadcasted_iota(jnp.int32, (V_ROWS - HEAD_DIM, blk), 0)
    tail = jnp.where(row == 0, 1.0, 0.0).astype(BF16)
    slabs, _, width = v_ref.shape
    assert width % blk == 0
    for hh in range(2):
        for jb in range(slabs * width // blk):
            first = jb * blk % width
            head_rows = v_ref[jb * blk // width, hh * HEAD_DIM:(hh + 1) * HEAD_DIM,
                              first:first + blk]
            vt_ref[hh, jb] = jnp.concatenate([head_rows, tail], axis=0)


def _flash_init(m_ref, acc_ref):
    m_ref[...] = jnp.full(m_ref.shape, NEG_INF, F32)
    acc_ref[...] = jnp.zeros(acc_ref.shape, F32)


def _pipelined_blocks(n, scores_into, update, final_scores, final_update, next_tile_scores):
    def step(j_next, j, slot):
        for hh in range(2):
            scores_into(j_next, 1 - slot, hh)
            update(j, slot, hh)

    def last_update_and_final(j, slot):
        for hh in range(2):
            final_scores(hh)
            update(j, slot, hh)
        next_tile_scores()
        final_update()

    def pair(jj, carry):
        j = 2 * jj
        step(j + 1, j, 0)
        step(j + 2, j + 1, 1)
        return carry
    lax.fori_loop(0, jnp.maximum(n - 1, 0) // 2, pair, 0)

    @pl.when(n % 2 == 1)
    def _():
        last_update_and_final(n - 1, 0)

    @pl.when((n % 2 == 0) & (n > 0))
    def _():
        step(n - 1, n - 2, 0)
        last_update_and_final(n - 1, 1)

    @pl.when(n == 0)
    def _():
        for hh in range(2):
            final_scores(hh)
        next_tile_scores()
        final_update()


def _store_scores(s_t, s_ref, mx_ref, idx):
    s_ref[idx + (slice(None), slice(0, s_t.shape[1]))] = s_t
    mx_ref[idx] = jnp.max(s_t, axis=0, keepdims=True)


def _flash_update(s_t, mx, v_rows, m_ref, acc_ref, hh, keep=None, const=None,
                  cols=slice(None)):
    m_old = m_ref[hh, :, cols]
    if const is not None:
        mx = mx + const
    if keep is not None:
        mx = jnp.where(keep, mx, NEG_INF)
    m_new = jnp.maximum(m_old, mx)
    m_safe = jnp.where(m_new == NEG_INF, 0.0, m_new)
    shift = m_safe if const is None else m_safe - const
    if keep is not None:
        shift = jnp.where(keep, shift, float("inf"))
    p = jnp.exp2(s_t - shift).astype(BF16)
    alpha = jnp.exp2(m_old - m_safe)
    m_ref[hh, :, cols] = m_new
    if not isinstance(v_rows, (list, tuple)):
        v_rows = [v_rows]
    keys = p.shape[0] // len(v_rows)
    acc = alpha * acc_ref[hh, :, cols]
    for i, v_i in enumerate(v_rows):
        acc = acc + jnp.dot(v_i, p[i * keys:(i + 1) * keys], preferred_element_type=F32)
    acc_ref[hh, :, cols] = acc


def _head_outputs(acc_ref):
    outs = []
    for hh in range(2):
        acc = acc_ref[hh]
        outs.append(acc[:HEAD_DIM] / acc[HEAD_DIM:HEAD_DIM + 1])
    return jnp.concatenate(outs, axis=0).astype(BF16)


def _for_each_query_tile(nq, tile):
    def body(qi, carry):
        tile(qi)
        return carry
    lax.fori_loop(0, nq, body, 0)


def _head_row_mask(hh):
    row = lax.broadcasted_iota(jnp.int32, (LANES, 1), 0)
    return (row < HEAD_DIM) if hh == 0 else (row >= HEAD_DIM)


def _fox_kernel(qt_ref, k_ref, v_ref, *refs):
    vt_ref = refs[3]
    nq, _, blk = qt_ref.shape
    _fill_value_rows(vt_ref, v_ref, blk)
    hp = pl.program_id(1)
    _for_each_query_tile(nq, lambda qi: _fox_tile(qi, hp, qt_ref, k_ref, *refs))


def _fox_tile(qi, hp, qt_ref, k_ref, aqt_ref, ak_ref, o_ref,
              vt_ref, s_ref, mx_ref, diag_ref, m_ref, acc_ref):
    nq, _, blk = qt_ref.shape
    gate_row = lax.broadcasted_iota(jnp.int32, (LANES, 1), 0)

    def query_operands(tile):
        qt, aqt = qt_ref[tile], aqt_ref[tile]
        out = []
        for hh in range(2):
            first = GATE_LANES * (2 * hp + hh)
            own_gate = (gate_row >= first) & (gate_row < first + GATE_LANES)
            q_rows = jnp.where(_head_row_mask(hh), qt, jnp.zeros_like(qt))
            g_rows = jnp.where(own_gate, aqt, jnp.zeros_like(aqt))
            out.append(jnp.concatenate([q_rows, g_rows], axis=0))
        return out

    def block_scores(w, j, hh):
        rows = pl.ds(pl.multiple_of(j * blk, blk), blk)
        keys = jnp.concatenate([k_ref[0, rows, :], ak_ref[0, rows, :]], axis=1)
        return jnp.dot(keys, w[hh], preferred_element_type=F32)

    w_q = query_operands(qi)

    def scores_into(j, slot, hh):
        _store_scores(block_scores(w_q, j, hh), s_ref, mx_ref, (slot, hh))

    def next_tile_scores():
        w_next = query_operands(jnp.minimum(qi + 1, nq - 1))
        for hh in range(2):
            _store_scores(block_scores(w_next, 0, hh), s_ref, mx_ref, (0, hh))

    def update(j, slot, hh):
        _flash_update(s_ref[slot, hh, :, :blk], mx_ref[slot, hh], vt_ref[hh, j],
                      m_ref, acc_ref, hh)

    def diagonal_scores(hh):
        diag_ref[hh] = block_scores(w_q, qi, hh)

    def diagonal_update():
        half = blk // 2
        lo, hi = slice(0, half), slice(half, blk)
        causal_lo = (lax.broadcasted_iota(jnp.int32, (half, half), 0)
                     <= lax.broadcasted_iota(jnp.int32, (half, half), 1))
        causal_hi = (lax.broadcasted_iota(jnp.int32, (blk, half), 0)
                     <= lax.broadcasted_iota(jnp.int32, (blk, half), 1) + half)
        for hh in range(2):
            v_rows = vt_ref[hh, qi]
            s_lo = jnp.where(causal_lo, diag_ref[hh, lo, lo], NEG_INF)
            _flash_update(s_lo, jnp.max(s_lo, axis=0, keepdims=True),
                          v_rows[:, lo], m_ref, acc_ref, hh, cols=lo)
            s_hi = jnp.where(causal_hi, diag_ref[hh, :, hi], NEG_INF)
            _flash_update(s_hi, jnp.max(s_hi, axis=0, keepdims=True),
                          v_rows, m_ref, acc_ref, hh, cols=hi)

    _flash_init(m_ref, acc_ref)
    _pipelined_blocks(qi, scores_into, update, diagonal_scores, diagonal_update,
                      next_tile_scores)
    o_ref[qi] = _head_outputs(acc_ref)


def _fox_attention(qt3, k3, vt, aqt3, ak3, n_heads):
    b, s, d = k3.shape
    assert n_heads * GATE_LANES <= LANES
    blk = qt3.shape[2]
    assert aqt3.shape[2] == blk and s % blk == 0 and vt.shape == qt3.shape
    npair = n_heads // 2
    nq = s // blk
    head_pair_tiles = pl.BlockSpec((nq, LANES, blk), lambda bi, hp: (bi, hp, 0))
    return pl.pallas_call(
        _fox_kernel, grid=(b, npair),
        in_specs=[
            head_pair_tiles,
            pl.BlockSpec((1, s, LANES), lambda bi, hp: (bi, 0, hp)),
            head_pair_tiles,
            pl.BlockSpec((nq, LANES, blk), lambda bi, hp: (bi, 0, 0)),
            pl.BlockSpec((1, s, LANES), lambda bi, hp: (bi, 0, 0)),
        ],
        out_specs=head_pair_tiles,
        out_shape=jax.ShapeDtypeStruct(qt3.shape, BF16),
        scratch_shapes=[
            pltpu.VMEM((2, s // blk, V_ROWS, blk), BF16),
            pltpu.VMEM((2, 2, blk, blk + SCORE_PITCH_PAD), F32),
            pltpu.VMEM((2, 2, 1, blk), F32),
            pltpu.VMEM((2, blk, blk), F32),
            pltpu.VMEM((2, 1, blk), F32),
            pltpu.VMEM((2, V_ROWS, blk), F32),
        ],
        compiler_params=_params(2), name="fox_attention")(qt3, k3, vt, aqt3, ak3)


def _t5_bucket_np(n):
    max_exact = REL_BUCKETS // 2
    nf = np.maximum(n, 1).astype(np.float64)
    large = max_exact + (np.log(nf / max_exact) / math.log(REL_MAX_DIST / max_exact)
                         * (REL_BUCKETS - max_exact)).astype(np.int32)
    return np.where(n < max_exact, n, np.minimum(large, REL_BUCKETS - 1)).astype(np.int32)


def _bucket_tiles(blk):
    key = np.arange(blk)[:, None]
    qry = np.arange(blk)[None, :]
    own = np.where(key <= qry, _t5_bucket_np(np.maximum(qry - key, 0)), -1)
    prev = _t5_bucket_np(blk + qry - key)
    return np.stack([own, prev]).astype(np.int32)


def _bias_kernel(tab_ref, bucket_ref, o_ref):
    h = pl.program_id(0)
    bucket = bucket_ref[...]
    acc = jnp.where(bucket < 0, NEG_INF, 0.0).astype(F32)
    for bkt in range(REL_BUCKETS):
        acc = jnp.where(bucket == bkt, tab_ref[bkt, h] * LOG2E, acc)
    o_ref[0] = acc


def _bias_tiles(rel_table, blk):
    n_heads = rel_table.shape[1]
    buckets = jnp.asarray(_bucket_tiles(blk))
    return pl.pallas_call(
        _bias_kernel, grid=(n_heads,),
        in_specs=[pl.BlockSpec(memory_space=pltpu.SMEM),
                  pl.BlockSpec((2, blk, blk), lambda h: (0, 0, 0))],
        out_specs=pl.BlockSpec((1, 2, blk, blk), lambda h: (h, 0, 0, 0)),
        out_shape=jax.ShapeDtypeStruct((n_heads, 2, blk, blk), F32),
        compiler_params=_params(1), name="t5_bias_tiles")(rel_table, buckets)


def _select_blocks(qt_ref, km_ref, sel_ref):
    nq, _, tq = qt_ref.shape
    blk = MOBA_BLOCK
    nblk = km_ref.shape[0]
    assert blk & (blk - 1) == 0
    qt_all = jnp.concatenate([qt_ref[t] for t in range(nq)], axis=1)
    km_parts = _split3(km_ref[...])
    blk_id = lax.broadcasted_iota(jnp.int32, (nblk, nq * tq), 0)
    own = lax.shift_right_logical(lax.broadcasted_iota(jnp.int32, (nblk, nq * tq), 1),
                                  blk.bit_length() - 1)
    past = blk_id < own
    for hh in range(2):
        q_h = jnp.where(_head_row_mask(hh), qt_all, jnp.zeros_like(qt_all))
        gate = None
        for part in km_parts:
            term = jnp.dot(part, q_h, preferred_element_type=F32)
            gate = term if gate is None else gate + term
        work = jnp.where(past, gate, NEG_INF)
        picked = jnp.zeros(gate.shape, F32)
        for _ in range(MOBA_TOP_K):
            best = jnp.max(work, axis=0, keepdims=True)
            first = jnp.min(jnp.where(work == best, blk_id, nblk), axis=0, keepdims=True)
            hit = blk_id == first
            picked = jnp.where(hit, 1.0, picked)
            work = jnp.where(hit, NEG_INF, work)
        keep_all = ((picked > 0.5) & past) | (blk_id == own) | (blk_id == (own | 1))
        keep_f = jnp.where(keep_all, 1.0, 0.0)
        for t in range(nq):
            sel_ref[hh, t] = keep_f[:, t * tq:(t + 1) * tq]


def _moba_kernel(tab_ref, qt_ref, k_ref, v_ref, bias_ref, o_ref,
                 vt_ref, km_ref, sel_ref, *scratch):
    blk = MOBA_BLOCK

    def block_mean(jb, carry):
        rows = pl.ds(pl.multiple_of(jb * blk, blk), blk)
        km_ref[pl.ds(jb, 1), :] = jnp.mean(k_ref[0, rows, :].astype(F32),
                                            axis=0, keepdims=True)
        return carry
    lax.fori_loop(0, k_ref.shape[1] // blk, block_mean, 0)
    _fill_value_rows(vt_ref, v_ref, blk)
    _select_blocks(qt_ref, km_ref, sel_ref)
    hp = pl.program_id(1)
    _for_each_query_tile(
        qt_ref.shape[0],
        lambda qi: _moba_tile(qi, hp, tab_ref, qt_ref, k_ref, bias_ref, o_ref,
                              vt_ref, sel_ref, *scratch))


def _moba_tile(qi, hp, tab_ref, qt_ref, k_ref, bias_ref, o_ref,
               vt_ref, sel_ref, s_ref, mx_ref, near_ref, own1_ref, m_ref, acc_ref):
    blk = MOBA_BLOCK
    nq, _, tq = qt_ref.shape
    assert tq == 2 * blk and blk >= REL_MAX_DIST
    first_own = 2 * qi

    def head_queries(tile):
        qt = qt_ref[tile]
        return [jnp.where(_head_row_mask(hh), qt, jnp.zeros_like(qt)) for hh in range(2)]

    q_m_t = head_queries(qi)

    def scores(hh, j, queries=q_m_t):
        rows = pl.ds(pl.multiple_of(j * blk, blk), blk)
        return jnp.dot(k_ref[0, rows, :], queries[hh], preferred_element_type=F32)

    def next_tile_scores():
        q_next = head_queries(jnp.minimum(qi + 1, nq - 1))
        for hh in range(2):
            _store_scores(scores(hh, 0, q_next), s_ref, mx_ref, (0, hh))

    def keep(hh, j):
        return sel_ref[hh, qi, pl.ds(j, 1), :] > 0.5

    _flash_init(m_ref, acc_ref)
    far_bias = [tab_ref[REL_BUCKETS - 1, 2 * hp + hh] * LOG2E for hh in range(2)]

    j_prev = jnp.maximum(first_own - 1, 0)
    lo, hi = slice(0, blk), slice(blk, 2 * blk)
    near_mx = {}

    def near_scores(hh):
        own_t, prev_t = bias_ref[hh, 0], bias_ref[hh, 1]
        mask_prev = jnp.where(keep(hh, j_prev) & (qi >= 1), 0.0, NEG_INF)
        mask_own = jnp.where(keep(hh, first_own), 0.0, NEG_INF)
        far_t = jnp.full((blk, blk), far_bias[hh], F32)
        parts = [scores(hh, j_prev) + mask_prev + jnp.concatenate([prev_t, far_t], axis=1),
                 scores(hh, first_own) + mask_own + jnp.concatenate([own_t, prev_t], axis=1)]
        mx = None
        for i, part in enumerate(parts):
            near_ref[hh, i * blk:(i + 1) * blk] = part
            part_mx = jnp.max(part, axis=0, keepdims=True)
            mx = part_mx if mx is None else jnp.maximum(mx, part_mx)
        rows = pl.ds(pl.multiple_of((first_own + 1) * blk, blk), blk)
        last = jnp.dot(k_ref[0, rows, :], q_m_t[hh][:, hi], preferred_element_type=F32) + own_t
        own1_ref[hh] = last
        near_mx[hh] = (mx[:, lo], jnp.maximum(mx[:, hi], jnp.max(last, axis=0, keepdims=True)))

    def near_update():
        for hh in range(2):
            v_near = [vt_ref[hh, j_prev], vt_ref[hh, first_own], vt_ref[hh, first_own + 1]]
            _flash_update(near_ref[hh, :, lo], near_mx[hh][0], v_near[:2],
                          m_ref, acc_ref, hh, cols=lo)
            _flash_update(jnp.concatenate([near_ref[hh, :, hi], own1_ref[hh]], axis=0),
                          near_mx[hh][1], v_near, m_ref, acc_ref, hh, cols=hi)

    def scores_into(j, slot, hh):
        _store_scores(scores(hh, j), s_ref, mx_ref, (slot, hh))

    def update(j, slot, hh):
        _flash_update(s_ref[slot, hh, :, :tq], mx_ref[slot, hh], vt_ref[hh, j],
                      m_ref, acc_ref, hh, keep=keep(hh, j), const=far_bias[hh])

    _pipelined_blocks(jnp.maximum(first_own - 1, 0), scores_into, update,
                      near_scores, near_update, next_tile_scores)
    o_ref[qi] = _head_outputs(acc_ref)


def _moba_attention(qt3, k3, vt, rel_table, bias_t, n_heads):
    b, s, d = k3.shape
    blk = MOBA_BLOCK
    tq = qt3.shape[2]
    assert tq == MOBA_Q_TILE and s % tq == 0 and vt.shape == qt3.shape
    npair = n_heads // 2
    nblk = s // blk
    nq = s // tq
    head_pair_tiles = pl.BlockSpec((nq, LANES, tq), lambda bi, hp: (bi, hp, 0))
    return pl.pallas_call(
        _moba_kernel, grid=(b, npair),
        in_specs=[
            pl.BlockSpec(memory_space=pltpu.SMEM),
            head_pair_tiles,
            pl.BlockSpec((1, s, LANES), lambda bi, hp: (bi, 0, hp)),
            head_pair_tiles,
            pl.BlockSpec((2, 2, blk, blk), lambda bi, hp: (hp, 0, 0, 0)),
        ],
        out_specs=head_pair_tiles,
        out_shape=jax.ShapeDtypeStruct(qt3.shape, BF16),
        scratch_shapes=[
            pltpu.VMEM((2, nblk, V_ROWS, blk), BF16),
            pltpu.VMEM((nblk, LANES), F32),
            pltpu.VMEM((2, nq, nblk, tq), F32),
            pltpu.VMEM((2, 2, blk, tq + SCORE_PITCH_PAD), F32),
            pltpu.VMEM((2, 2, 1, tq), F32),
            pltpu.VMEM((2, 2 * blk, tq), F32),
            pltpu.VMEM((2, blk, blk), F32),
            pltpu.VMEM((2, 1, tq), F32),
            pltpu.VMEM((2, V_ROWS, tq), F32),
        ],
        compiler_params=_params(2), name="moba_attention")(
            rel_table, qt3, k3, vt, bias_t)


class _Ple(NamedTuple):
    g: jax.Array
    w_gate: jax.Array
    p: _Slab
    w_up: jax.Array

    def in_specs(self, tm):
        layer, last = self.p.layer, self.p.shape[0] // tm - 1
        return [_resident(self.g), _resident(self.w_gate),
                pl.BlockSpec((None, tm, self.p.shape[1]),
                             lambda i: (layer, jnp.minimum(i, last), 0)),
                _resident(self.w_up)]

    def operands(self):
        return tuple(_array(a) for a in self)


def _ple_update(x, g_ref, wg_ref, p_ref, wu_ref):
    u = _rmsnorm(x, g_ref[...]).astype(BF16)
    gate = jax.nn.sigmoid(jnp.dot(u, wg_ref[...], preferred_element_type=F32))
    up = jnp.dot(p_ref[...].astype(BF16), wu_ref[...], preferred_element_type=F32)
    return x + gate * up


def _oproj_ffn_kernel(h_ref, ot_ref, wo_ref, g_ref, win_ref, wout_ref, *rest, tf):
    d_ff = wout_ref.shape[0]
    h1 = h_ref[...] + lax.dot_general(ot_ref[...], wo_ref[...], (((0,), (0,)), ((), ())),
                                      preferred_element_type=F32)
    u = _rmsnorm(h1, g_ref[...]).astype(BF16)
    acc = h1
    for c in range(d_ff // tf):
        gate = jnp.dot(u, win_ref[:, c * tf:(c + 1) * tf], preferred_element_type=F32)
        up = jnp.dot(u, win_ref[:, d_ff + c * tf:d_ff + (c + 1) * tf],
                     preferred_element_type=F32)
        act = (gate * jax.nn.sigmoid(gate) * up).astype(BF16)
        acc = acc + jnp.dot(act, wout_ref[c * tf:(c + 1) * tf, :],
                            preferred_element_type=F32)
    *ple_refs, out_ref = rest
    if ple_refs:
        *ple_refs, final_g_ref = ple_refs
        acc = _rmsnorm(_ple_update(acc, *ple_refs), final_g_ref[...])
    out_ref[...] = acc


def _oproj_ffn(h2, o_t, w_o, g, w_in, w_out, *, tm, tf, last=None):
    n, d = h2.shape
    d_ff = w_out.shape[0]
    assert d_ff % tf == 0 and o_t.shape == (n // tm, d, tm)
    row = pl.BlockSpec((tm, d), lambda i: (i, 0))
    weights = (w_o, g, w_in, w_out)
    in_specs = ([row, pl.BlockSpec((None, d, tm), lambda i: (i, 0, 0))]
                + [_resident(w) for w in weights])
    operands = (h2, o_t, *map(_array, weights))
    if last is not None:
        ple, final_g = last
        in_specs += ple.in_specs(tm) + [_resident(final_g)]
        operands += ple.operands() + (final_g,)
    return pl.pallas_call(
        functools.partial(_oproj_ffn_kernel, tf=tf), grid=(n // tm,),
        in_specs=in_specs, out_specs=row,
        out_shape=jax.ShapeDtypeStruct((n, d), F32),
        compiler_params=_params(1), name="oproj_ffn")(*operands)


def _ple_next_kernel(x_ref, g_ref, wg_ref, p_ref, wu_ref, *refs, n_mixer, jobs):
    mixer_refs, cast_in, (y_ref, *out_refs), cast_out, u_ref = _split_refs(
        refs, n_mixer, len(jobs))
    held = {}

    def normalise(slot):
        held["u"] = _rmsnorm(x_ref[...], g_ref[...]).astype(BF16)

    def gate(slot):
        held["gate"] = jax.nn.sigmoid(
            jnp.dot(held["u"], wg_ref[...], preferred_element_type=F32))

    def update(slot):
        up = jnp.dot(p_ref[...].astype(BF16), wu_ref[...], preferred_element_type=F32)
        y_ref[...] = x_ref[...] + held["gate"] * up

    def normalise_next(slot):
        u_ref[slot] = _rmsnorm(y_ref[...], mixer_refs[0][...]).astype(BF16)

    def cast(slot):
        _cast_rows(jobs, cast_in, cast_out)

    _skewed_projection([normalise, gate, update, normalise_next, cast], u_ref, mixer_refs,
                       out_refs)


def _ple_next(h2, ple, mixer, casts, *, tm):
    n, d = h2.shape
    n_mixer = len(mixer.operands())
    cast_in, cast_out, cast_shapes, jobs = _cast_plans(casts, n // tm)
    grid, row, scratch = _skewed_rows(n, tm, d)
    h, *outs = pl.pallas_call(
        functools.partial(_ple_next_kernel, n_mixer=n_mixer, jobs=jobs),
        grid=grid,
        in_specs=[row] + ple.in_specs(tm) + mixer.in_specs() + cast_in,
        out_specs=[row] + mixer.out_specs(tm) + cast_out,
        out_shape=[jax.ShapeDtypeStruct((n, d), F32)] + mixer.out_shapes(n, tm) + cast_shapes,
        scratch_shapes=scratch,
        compiler_params=_params(1), name="ple_proj")(
            h2, *ple.operands(), *mixer.operands(), *(c.w.stack for c in casts))
    return h, outs[:len(outs) - len(casts)], outs[len(outs) - len(casts):]


def _row_tile(n, want):
    t = min(want, n)
    assert n % t == 0
    return t


def _col_tile(n, want):
    t = min(want, n)
    while n % t:
        t -= LANES
    return t


def kernel(x, p, attn_norm_g, fox_w_in, fox_b_f, fox_w_o, moba_w_in, moba_w_o, rel_bias_table,
           ffn_norm_g, ffn_w_in, ffn_w_out, ple_norm_g, ple_w_gate, ple_w_up, final_norm_g):
    b, s, d = x.shape
    depth = p.shape[0]
    n_heads = rel_bias_table.shape[1]
    assert d == n_heads * HEAD_DIM and n_heads % 2 == 0 and n_heads <= LANES
    n = b * s
    tm = _row_tile(n, DENSE_ROW_TILE)
    tf = _col_tile(ffn_w_out.shape[1], FFN_COL_CHUNK)

    def row_vec(v):
        return v.reshape(1, -1).astype(F32)

    w_o_mix = (fox_w_o, moba_w_o)
    p3 = p.reshape(depth, n, -1)
    fox_wt = jnp.swapaxes(fox_w_in, 1, 2)

    def qkv_cast(i):
        if i % 2 == 0:
            return _Cast(_Slab(fox_wt, i // 2), 3 * d, scaled=d)
        return _Cast(_Slab(moba_w_in, i // 2), d, scaled=d, transpose=True)

    def layer_casts(i):
        jobs = [_Cast(_Slab(w, layer), w.shape[1]) for w, layer in (
            (w_o_mix[i % 2], i // 2), (ffn_w_in, i), (ffn_w_out, i), (ple_w_gate, i),
            (ple_w_up, i))]
        return jobs + ([qkv_cast(i + 1)] if i + 1 < depth else [])

    def mixer(i, w_qkv):
        parts = [row_vec(attn_norm_g[i]), w_qkv]
        if i % 2 == 0:
            pad_heads = LANES - n_heads
            parts += [jnp.pad(fox_wt[i // 2, 3 * d:], ((0, pad_heads), (0, 0))),
                      jnp.pad(row_vec(fox_b_f[i // 2]), ((0, 0), (0, pad_heads)))]
        return _Mixer(*parts)

    rel_table = rel_bias_table.astype(F32)
    bias_t = _bias_tiles(rel_table, MOBA_BLOCK)

    h = x.reshape(n, d).astype(F32)
    row = lax.broadcasted_iota(jnp.int32, (3 * d, 1), 0)
    w_qkv = (fox_wt[0, :3 * d] * jnp.where(row < d, Q_SCALE, 1.0)).astype(BF16)
    mixed, weights = _project(h, mixer(0, w_qkv), layer_casts(0), tm=tm)
    for i in range(depth):
        w_o, w_ffn_in, w_ffn_out, w_ple_gate, w_ple_up, *w_qkv_next = weights
        qt, k3, vt = mixed[0], mixed[1].reshape(b, s, d), mixed[2]
        if i % 2 == 0:
            aqt, ak = _cumsum(mixed[3].reshape(b, s, LANES))
            o_t = _fox_attention(qt, k3, vt, aqt, ak, n_heads)
        else:
            o_t = _moba_attention(qt, k3, vt, rel_table, bias_t, n_heads)
        ple = _Ple(row_vec(ple_norm_g[i]), w_ple_gate, _Slab(p3, i), w_ple_up)
        last = (ple, row_vec(final_norm_g)) if i + 1 == depth else None
        h = _oproj_ffn(h, o_t, w_o, row_vec(ffn_norm_g[i]), w_ffn_in, w_ffn_out,
                       tm=tm, tf=tf, last=last)
        if last is None:
            h, mixed, weights = _ple_next(h, ple, mixer(i + 1, *w_qkv_next),
                                          layer_casts(i + 1), tm=tm)
    return h.reshape(b, s, d).astype(x.dtype)
```

```python
import functools
import math
from typing import NamedTuple, Optional

import numpy as np
import jax
import jax.numpy as jnp
from jax import lax
from jax.experimental import pallas as pl
from jax.experimental.pallas import tpu as pltpu

F32 = jnp.float32
BF16 = jnp.bfloat16

RMS_EPS = 1e-6
HEAD_DIM = 64
MOBA_BLOCK = 256
MOBA_TOP_K = 3
REL_BUCKETS = 32
REL_MAX_DIST = 128

LANES = 128
F32_SUBLANES = 8
BF16_SUBLANES = 16
V7X_VMEM_BYTES = 64 * 1024 * 1024
V_ROWS = HEAD_DIM + BF16_SUBLANES
SPLIT_TERMS = 3
GATE_LANES = 8
SCORE_PITCH_PAD = LANES
LOG2E = math.log2(math.e)
Q_SCALE = HEAD_DIM ** -0.5 * LOG2E
MOBA_Q_TILE = 2 * MOBA_BLOCK
ATTN_Q_TILE = MOBA_Q_TILE
DENSE_ROW_TILE = ATTN_Q_TILE
CUMSUM_BLK = ATTN_Q_TILE
PROJ_COL_CHUNK = 512
FFN_COL_CHUNK = 256
VMEM_LIMIT_BYTES = V7X_VMEM_BYTES // 8 * 7
NEG_INF = float("-inf")


def _params(n_axes):
    return pltpu.CompilerParams(
        dimension_semantics=("arbitrary",) * n_axes,
        vmem_limit_bytes=VMEM_LIMIT_BYTES)


def _split3(x):
    x1 = x.astype(BF16)
    r1 = x - x1.astype(F32)
    x2 = r1.astype(BF16)
    x3 = (r1 - x2.astype(F32)).astype(BF16)
    return x1, x2, x3


def _rmsnorm(x, g):
    ms = jnp.mean(x * x, axis=-1, keepdims=True)
    return x * lax.rsqrt(ms + RMS_EPS) * g


def _log_sigmoid(x):
    return jnp.minimum(x, 0.0) - jnp.log1p(jnp.exp(-jnp.abs(x)))


class _Slab(NamedTuple):
    stack: jax.Array
    layer: int

    @property
    def shape(self):
        return self.stack.shape[1:]


def _resident(x):
    if isinstance(x, _Slab):
        layer, zeros = x.layer, (0,) * len(x.shape)
        return pl.BlockSpec((None,) + x.shape, lambda i: (layer,) + zeros,
                            pipeline_mode=pl.Buffered(1))
    zeros = (0,) * x.ndim
    return pl.BlockSpec(x.shape, lambda i: zeros, pipeline_mode=pl.Buffered(1))


def _array(x):
    return x.stack if isinstance(x, _Slab) else x


class _Cast(NamedTuple):
    w: _Slab
    rows: int
    scaled: int = 0
    transpose: bool = False

    def plan(self, steps):
        cols = self.w.shape[1]
        min_rows = LANES if self.transpose else BF16_SUBLANES
        assert self.rows % min_rows == 0 and cols % LANES == 0
        chunks = math.gcd(steps, self.rows // min_rows)
        span, layer, chunk = steps // chunks, self.w.layer, self.rows // chunks
        in_spec = pl.BlockSpec((None, chunk, cols), lambda i: (layer, i // span, 0))
        if self.transpose:
            return (in_spec, pl.BlockSpec((cols, chunk), lambda i: (0, i // span)),
                    jax.ShapeDtypeStruct((cols, self.rows), BF16), span)
        return (in_spec, pl.BlockSpec((chunk, cols), lambda i: (i // span, 0)),
                jax.ShapeDtypeStruct((self.rows, cols), BF16), span)


def _cast_plans(casts, steps):
    plans = [c.plan(steps) for c in casts]
    return ([p[0] for p in plans], [p[1] for p in plans], [p[2] for p in plans],
            tuple((c.transpose, c.scaled, p[3]) for c, p in zip(casts, plans)))


def _cast_rows(jobs, in_refs, out_refs):
    for (transpose, scaled, span), in_ref, out_ref in zip(jobs, in_refs, out_refs):
        w = in_ref[...].T if transpose else in_ref[...]
        if scaled:
            first_row = 0 if transpose else pl.program_id(0) // span * w.shape[0]
            row = first_row + lax.broadcasted_iota(jnp.int32, (w.shape[0], 1), 0)
            w = w * jnp.where(row < scaled, Q_SCALE, 1.0)
        out_ref[...] = w.astype(BF16)


class _Mixer(NamedTuple):
    g: jax.Array
    w_qkv: jax.Array
    w_f: Optional[jax.Array] = None
    b_f: Optional[jax.Array] = None

    def operands(self):
        return tuple(_array(a) for a in self if a is not None)

    def in_specs(self):
        return [_resident(a) for a in self if a is not None]

    def out_specs(self, tm):
        d = self.w_qkv.shape[1]
        slab = pl.BlockSpec((None, d, tm), lambda i: (i, 0, 0))
        specs = [slab, pl.BlockSpec((tm, d), lambda i: (i, 0)), slab]
        if self.w_f is not None:
            specs.append(pl.BlockSpec((tm, LANES), lambda i: (i, 0)))
        return specs

    def out_shapes(self, n, tm):
        d = self.w_qkv.shape[1]
        slabs = jax.ShapeDtypeStruct((n // tm, d, tm), BF16)
        shapes = [slabs, jax.ShapeDtypeStruct((n, d), BF16), slabs]
        if self.w_f is not None:
            shapes.append(jax.ShapeDtypeStruct((n, LANES), F32))
        return shapes


def _chunks(total, want):
    step = want if total % want == 0 else total
    return [slice(c * step, (c + 1) * step) for c in range(total // step)]


def _mixer_inputs(y, mixer_refs, out_refs, beside):
    g_ref, wt_ref = mixer_refs[:2]
    d = wt_ref.shape[1]
    u = _rmsnorm(y, g_ref[...]).astype(BF16)

    def contract_last(a, b):
        return lax.dot_general(a, b, (((1,), (1,)), ((), ())), preferred_element_type=F32)

    for first_row, out_ref in ((0, out_refs[0]), (2 * d, out_refs[2])):
        for rows in _chunks(d, PROJ_COL_CHUNK):
            w_rows = wt_ref[first_row + rows.start:first_row + rows.stop, :]
            out_ref[rows, :] = contract_last(w_rows, u).astype(BF16)
    beside()
    for cols in _chunks(d, PROJ_COL_CHUNK):
        out_refs[1][:, cols] = contract_last(
            u, wt_ref[d + cols.start:d + cols.stop, :]).astype(BF16)
    if len(mixer_refs) > 2:
        wft_ref, bf_ref = mixer_refs[2:]
        f_logit = contract_last(u, wft_ref[...].astype(BF16)) + bf_ref[...]
        out_refs[3][...] = _log_sigmoid(f_logit)


def _split_refs(refs, n_mixer, n_cast):
    n_out = len(refs) - n_cast
    return (refs[:n_mixer], refs[n_mixer:n_mixer + n_cast],
            refs[n_mixer + n_cast:n_out], refs[n_out:])


def _proj_kernel(x_ref, *refs, n_mixer, jobs):
    mixer_refs, cast_in, out_refs, cast_out = _split_refs(refs, n_mixer, len(jobs))
    _mixer_inputs(x_ref[...], mixer_refs, out_refs,
                  functools.partial(_cast_rows, jobs, cast_in, cast_out))


def _project(h2, mixer, casts, *, tm):
    n, d = h2.shape
    n_mixer = len(mixer.operands())
    cast_in, cast_out, cast_shapes, jobs = _cast_plans(casts, n // tm)
    outs = pl.pallas_call(
        functools.partial(_proj_kernel, n_mixer=n_mixer, jobs=jobs),
        grid=(n // tm,),
        in_specs=[pl.BlockSpec((tm, d), lambda i: (i, 0))] + mixer.in_specs() + cast_in,
        out_specs=mixer.out_specs(tm) + cast_out,
        out_shape=mixer.out_shapes(n, tm) + cast_shapes,
        compiler_params=_params(1), name="proj")(
            h2, *mixer.operands(), *(c.w.stack for c in casts))
    return outs[:len(outs) - len(casts)], outs[len(outs) - len(casts):]


def _cumsum_kernel(lf_ref, aqt_ref, ak_ref, carry_ref):
    @pl.when(pl.program_id(1) == 0)
    def _():
        carry_ref[...] = jnp.zeros_like(carry_ref)

    t = lf_ref.shape[1]
    row = lax.broadcasted_iota(jnp.int32, (t, t), 0)
    col = lax.broadcasted_iota(jnp.int32, (t, t), 1)
    tril = jnp.where(col <= row, 1.0, 0.0).astype(BF16)
    x1, x2, x3 = _split3(lf_ref[0])
    cs = (jnp.dot(tril, x1, preferred_element_type=F32)
          + jnp.dot(tril, x2, preferred_element_type=F32)
          + jnp.dot(tril, x3, preferred_element_type=F32))
    cs = cs + carry_ref[0:1, :]
    carry_ref[...] = jnp.broadcast_to(cs[t - 1:t, :], carry_ref.shape)

    src = lax.broadcasted_iota(jnp.int32, (LANES, LANES), 0)
    dst = lax.broadcasted_iota(jnp.int32, (LANES, LANES), 1)
    lane = lax.broadcasted_iota(jnp.int32, (1, LANES), 1) & (GATE_LANES - 1)
    aq = jnp.where((lane >= SPLIT_TERMS) & (lane < 2 * SPLIT_TERMS), 1.0, 0.0)
    ak = jnp.where(lane < SPLIT_TERMS, 1.0, 0.0)
    for i, part in enumerate(_split3(cs * LOG2E)):
        to_q = jnp.where(dst == GATE_LANES * src + i, 1.0, 0.0).astype(BF16)
        to_k = jnp.where(dst == GATE_LANES * src + SPLIT_TERMS + i, 1.0, 0.0).astype(BF16)
        aq = aq + jnp.dot(part, to_q, preferred_element_type=F32)
        ak = ak - jnp.dot(part, to_k, preferred_element_type=F32)
    eye = jnp.where(src == dst, 1.0, 0.0).astype(BF16)
    aqt_ref[0] = lax.dot_general(eye, aq.astype(BF16), (((1,), (1,)), ((), ())),
                                 preferred_element_type=F32).astype(BF16)
    ak_ref[0] = ak.astype(BF16)


def _cumsum(lf3):
    b, s, _ = lf3.shape
    t = min(CUMSUM_BLK, s)
    nt = s // t
    spec = pl.BlockSpec((1, t, LANES), lambda i, j: (i, j, 0))
    return pl.pallas_call(
        _cumsum_kernel, grid=(b, nt),
        in_specs=[spec],
        out_specs=[pl.BlockSpec((1, LANES, t), lambda i, j: (i * nt + j, 0, 0)), spec],
        out_shape=[jax.ShapeDtypeStruct((b * nt, LANES, t), BF16),
                   jax.ShapeDtypeStruct(lf3.shape, BF16)],
        scratch_shapes=[pltpu.VMEM((F32_SUBLANES, LANES), F32)],
        compiler_params=_params(2), name="gate_cumsum")(lf3)


def _fill_value_rows(vt_ref, v_ref, blk):
    row = lax.broadcasted_iota(jnp.int32, (V_ROWS - HEAD_DIM, blk), 0)
    tail = jnp.where(row == 0, 1.0, 0.0).astype(BF16)
    slabs, _, width = v_ref.shape
    assert width % blk == 0
    for hh in range(2):
        for jb in range(slabs * width // blk):
            first = jb * blk % width
            head_rows = v_ref[jb * blk // width, hh * HEAD_DIM:(hh + 1) * HEAD_DIM,
                              first:first + blk]
            vt_ref[hh, jb] = jnp.concatenate([head_rows, tail], axis=0)


def _flash_init(m_ref, acc_ref):
    m_ref[...] = jnp.full(m_ref.shape, NEG_INF, F32)
    acc_ref[...] = jnp.zeros(acc_ref.shape, F32)


def _pipelined_blocks(n, scores_into, update, final_scores, final_update, next_tile_scores):
    def step(j_next, j, slot):
        for hh in range(2):
            scores_into(j_next, 1 - slot, hh)
            update(j, slot, hh)

    def last_update_and_final(j, slot):
        for hh in range(2):
            final_scores(hh)
            update(j, slot, hh)
        next_tile_scores()
        final_update()

    def pair(jj, carry):
        j = 2 * jj
        step(j + 1, j, 0)
        step(j + 2, j + 1, 1)
        return carry
    lax.fori_loop(0, jnp.maximum(n - 1, 0) // 2, pair, 0)

    @pl.when(n % 2 == 1)
    def _():
        last_update_and_final(n - 1, 0)

    @pl.when((n % 2 == 0) & (n > 0))
    def _():
        step(n - 1, n - 2, 0)
        last_update_and_final(n - 1, 1)

    @pl.when(n == 0)
    def _():
        for hh in range(2):
            final_scores(hh)
        next_tile_scores()
        final_update()


def _store_scores(s_t, s_ref, mx_ref, idx):
    s_ref[idx + (slice(None), slice(0, s_t.shape[1]))] = s_t
    mx_ref[idx] = jnp.max(s_t, axis=0, keepdims=True)


def _flash_update(s_t, mx, v_rows, m_ref, acc_ref, hh, keep=None, const=None,
                  cols=slice(None)):
    m_old = m_ref[hh, :, cols]
    if const is not None:
        mx = mx + const
    if keep is not None:
        mx = jnp.where(keep, mx, NEG_INF)
    m_new = jnp.maximum(m_old, mx)
    m_safe = jnp.where(m_new == NEG_INF, 0.0, m_new)
    shift = m_safe if const is None else m_safe - const
    if keep is not None:
        shift = jnp.where(keep, shift, float("inf"))
    p = jnp.exp2(s_t - shift).astype(BF16)
    alpha = jnp.exp2(m_old - m_safe)
    m_ref[hh, :, cols] = m_new
    if not isinstance(v_rows, (list, tuple)):
        v_rows = [v_rows]
    keys = p.shape[0] // len(v_rows)
    acc = alpha * acc_ref[hh, :, cols]
    for i, v_i in enumerate(v_rows):
        acc = acc + jnp.dot(v_i, p[i * keys:(i + 1) * keys], preferred_element_type=F32)
    acc_ref[hh, :, cols] = acc


def _head_outputs(acc_ref):
    outs = []
    for hh in range(2):
        acc = acc_ref[hh]
        outs.append(acc[:HEAD_DIM] / acc[HEAD_DIM:HEAD_DIM + 1])
    return jnp.concatenate(outs, axis=0).astype(BF16)


def _for_each_query_tile(nq, tile):
    def body(qi, carry):
        tile(qi)
        return carry
    lax.fori_loop(0, nq, body, 0)


def _head_row_mask(hh):
    row = lax.broadcasted_iota(jnp.int32, (LANES, 1), 0)
    return (row < HEAD_DIM) if hh == 0 else (row >= HEAD_DIM)


def _fox_kernel(qt_ref, k_ref, v_ref, *refs):
    vt_ref = refs[3]
    nq, _, blk = qt_ref.shape
    _fill_value_rows(vt_ref, v_ref, blk)
    hp = pl.program_id(1)
    _for_each_query_tile(nq, lambda qi: _fox_tile(qi, hp, qt_ref, k_ref, *refs))


def _fox_tile(qi, hp, qt_ref, k_ref, aqt_ref, ak_ref, o_ref,
              vt_ref, s_ref, mx_ref, diag_ref, m_ref, acc_ref):
    nq, _, blk = qt_ref.shape
    gate_row = lax.broadcasted_iota(jnp.int32, (LANES, 1), 0)

    def query_operands(tile):
        qt, aqt = qt_ref[tile], aqt_ref[tile]
        out = []
        for hh in range(2):
            first = GATE_LANES * (2 * hp + hh)
            own_gate = (gate_row >= first) & (gate_row < first + GATE_LANES)
            q_rows = jnp.where(_head_row_mask(hh), qt, jnp.zeros_like(qt))
            g_rows = jnp.where(own_gate, aqt, jnp.zeros_like(aqt))
            out.append(jnp.concatenate([q_rows, g_rows], axis=0))
        return out

    def block_scores(w, j, hh):
        rows = pl.ds(pl.multiple_of(j * blk, blk), blk)
        keys = jnp.concatenate([k_ref[0, rows, :], ak_ref[0, rows, :]], axis=1)
        return jnp.dot(keys, w[hh], preferred_element_type=F32)

    w_q = query_operands(qi)

    def scores_into(j, slot, hh):
        _store_scores(block_scores(w_q, j, hh), s_ref, mx_ref, (slot, hh))

    def next_tile_scores():
        w_next = query_operands(jnp.minimum(qi + 1, nq - 1))
        for hh in range(2):
            _store_scores(block_scores(w_next, 0, hh), s_ref, mx_ref, (0, hh))

    def update(j, slot, hh):
        _flash_update(s_ref[slot, hh, :, :blk], mx_ref[slot, hh], vt_ref[hh, j],
                      m_ref, acc_ref, hh)

    def diagonal_scores(hh):
        diag_ref[hh] = block_scores(w_q, qi, hh)

    def diagonal_update():
        half = blk // 2
        lo, hi = slice(0, half), slice(half, blk)
        causal_lo = (lax.broadcasted_iota(jnp.int32, (half, half), 0)
                     <= lax.broadcasted_iota(jnp.int32, (half, half), 1))
        causal_hi = (lax.broadcasted_iota(jnp.int32, (blk, half), 0)
                     <= lax.broadcasted_iota(jnp.int32, (blk, half), 1) + half)
        for hh in range(2):
            v_rows = vt_ref[hh, qi]
            s_lo = jnp.where(causal_lo, diag_ref[hh, lo, lo], NEG_INF)
            _flash_update(s_lo, jnp.max(s_lo, axis=0, keepdims=True),
                          v_rows[:, lo], m_ref, acc_ref, hh, cols=lo)
            s_hi = jnp.where(causal_hi, diag_ref[hh, :, hi], NEG_INF)
            _flash_update(s_hi, jnp.max(s_hi, axis=0, keepdims=True),
                          v_rows, m_ref, acc_ref, hh, cols=hi)

    _flash_init(m_ref, acc_ref)
    _pipelined_blocks(qi, scores_into, update, diagonal_scores, diagonal_update,
                      next_tile_scores)
    o_ref[qi] = _head_outputs(acc_ref)


def _fox_attention(qt3, k3, vt, aqt3, ak3, n_heads):
    b, s, d = k3.shape
    assert n_heads * GATE_LANES <= LANES
    blk = qt3.shape[2]
    assert aqt3.shape[2] == blk and s % blk == 0 and vt.shape == qt3.shape
    npair = n_heads // 2
    nq = s // blk
    head_pair_tiles = pl.BlockSpec((nq, LANES, blk), lambda bi, hp: (bi, hp, 0))
    return pl.pallas_call(
        _fox_kernel, grid=(b, npair),
        in_specs=[
            head_pair_tiles,
            pl.BlockSpec((1, s, LANES), lambda bi, hp: (bi, 0, hp)),
            head_pair_tiles,
            pl.BlockSpec((nq, LANES, blk), lambda bi, hp: (bi, 0, 0)),
            pl.BlockSpec((1, s, LANES), lambda bi, hp: (bi, 0, 0)),
        ],
        out_specs=head_pair_tiles,
        out_shape=jax.ShapeDtypeStruct(qt3.shape, BF16),
        scratch_shapes=[
            pltpu.VMEM((2, s // blk, V_ROWS, blk), BF16),
            pltpu.VMEM((2, 2, blk, blk + SCORE_PITCH_PAD), F32),
            pltpu.VMEM((2, 2, 1, blk), F32),
            pltpu.VMEM((2, blk, blk), F32),
            pltpu.VMEM((2, 1, blk), F32),
            pltpu.VMEM((2, V_ROWS, blk), F32),
        ],
        compiler_params=_params(2), name="fox_attention")(qt3, k3, vt, aqt3, ak3)


def _t5_bucket_np(n):
    max_exact = REL_BUCKETS // 2
    nf = np.maximum(n, 1).astype(np.float64)
    large = max_exact + (np.log(nf / max_exact) / math.log(REL_MAX_DIST / max_exact)
                         * (REL_BUCKETS - max_exact)).astype(np.int32)
    return np.where(n < max_exact, n, np.minimum(large, REL_BUCKETS - 1)).astype(np.int32)


def _bucket_tiles(blk):
    key = np.arange(blk)[:, None]
    qry = np.arange(blk)[None, :]
    own = np.where(key <= qry, _t5_bucket_np(np.maximum(qry - key, 0)), -1)
    prev = _t5_bucket_np(blk + qry - key)
    return np.stack([own, prev]).astype(np.int32)


def _bias_kernel(tab_ref, bucket_ref, o_ref):
    h = pl.program_id(0)
    bucket = bucket_ref[...]
    acc = jnp.where(bucket < 0, NEG_INF, 0.0).astype(F32)
    for bkt in range(REL_BUCKETS):
        acc = jnp.where(bucket == bkt, tab_ref[bkt, h] * LOG2E, acc)
    o_ref[0] = acc


def _bias_tiles(rel_table, blk):
    n_heads = rel_table.shape[1]
    buckets = jnp.asarray(_bucket_tiles(blk))
    return pl.pallas_call(
        _bias_kernel, grid=(n_heads,),
        in_specs=[pl.BlockSpec(memory_space=pltpu.SMEM),
                  pl.BlockSpec((2, blk, blk), lambda h: (0, 0, 0))],
        out_specs=pl.BlockSpec((1, 2, blk, blk), lambda h: (h, 0, 0, 0)),
        out_shape=jax.ShapeDtypeStruct((n_heads, 2, blk, blk), F32),
        compiler_params=_params(1), name="t5_bias_tiles")(rel_table, buckets)


def _select_blocks(qt_ref, km_ref, sel_ref):
    nq, _, tq = qt_ref.shape
    blk = MOBA_BLOCK
    nblk = km_ref.shape[0]
    assert blk & (blk - 1) == 0
    qt_all = jnp.concatenate([qt_ref[t] for t in range(nq)], axis=1)
    km_parts = _split3(km_ref[...])
    blk_id = lax.broadcasted_iota(jnp.int32, (nblk, nq * tq), 0)
    own = lax.shift_right_logical(lax.broadcasted_iota(jnp.int32, (nblk, nq * tq), 1),
                                  blk.bit_length() - 1)
    past = blk_id < own
    for hh in range(2):
        q_h = jnp.where(_head_row_mask(hh), qt_all, jnp.zeros_like(qt_all))
        gate = None
        for part in km_parts:
            term = jnp.dot(part, q_h, preferred_element_type=F32)
            gate = term if gate is None else gate + term
        work = jnp.where(past, gate, NEG_INF)
        picked = jnp.zeros(gate.shape, F32)
        for _ in range(MOBA_TOP_K):
            best = jnp.max(work, axis=0, keepdims=True)
            first = jnp.min(jnp.where(work == best, blk_id, nblk), axis=0, keepdims=True)
            hit = blk_id == first
            picked = jnp.where(hit, 1.0, picked)
            work = jnp.where(hit, NEG_INF, work)
        keep_all = ((picked > 0.5) & past) | (blk_id == own) | (blk_id == (own | 1))
        keep_f = jnp.where(keep_all, 1.0, 0.0)
        for t in range(nq):
            sel_ref[hh, t] = keep_f[:, t * tq:(t + 1) * tq]


def _moba_kernel(tab_ref, qt_ref, k_ref, v_ref, bias_ref, o_ref,
                 vt_ref, km_ref, sel_ref, *scratch):
    blk = MOBA_BLOCK

    def block_mean(jb, carry):
        rows = pl.ds(pl.multiple_of(jb * blk, blk), blk)
        km_ref[pl.ds(jb, 1), :] = jnp.mean(k_ref[0, rows, :].astype(F32),
                                            axis=0, keepdims=True)
        return carry
    lax.fori_loop(0, k_ref.shape[1] // blk, block_mean, 0)
    _fill_value_rows(vt_ref, v_ref, blk)
    _select_blocks(qt_ref, km_ref, sel_ref)
    hp = pl.program_id(1)
    _for_each_query_tile(
        qt_ref.shape[0],
        lambda qi: _moba_tile(qi, hp, tab_ref, qt_ref, k_ref, bias_ref, o_ref,
                              vt_ref, sel_ref, *scratch))


def _moba_tile(qi, hp, tab_ref, qt_ref, k_ref, bias_ref, o_ref,
               vt_ref, sel_ref, s_ref, mx_ref, near_ref, own1_ref, m_ref, acc_ref):
    blk = MOBA_BLOCK
    nq, _, tq = qt_ref.shape
    assert tq == 2 * blk and blk >= REL_MAX_DIST
    first_own = 2 * qi

    def head_queries(tile):
        qt = qt_ref[tile]
        return [jnp.where(_head_row_mask(hh), qt, jnp.zeros_like(qt)) for hh in range(2)]

    q_m_t = head_queries(qi)

    def scores(hh, j, queries=q_m_t):
        rows = pl.ds(pl.multiple_of(j * blk, blk), blk)
        return jnp.dot(k_ref[0, rows, :], queries[hh], preferred_element_type=F32)

    def next_tile_scores():
        q_next = head_queries(jnp.minimum(qi + 1, nq - 1))
        for hh in range(2):
            _store_scores(scores(hh, 0, q_next), s_ref, mx_ref, (0, hh))

    def keep(hh, j):
        return sel_ref[hh, qi, pl.ds(j, 1), :] > 0.5

    _flash_init(m_ref, acc_ref)
    far_bias = [tab_ref[REL_BUCKETS - 1, 2 * hp + hh] * LOG2E for hh in range(2)]

    j_prev = jnp.maximum(first_own - 1, 0)
    lo, hi = slice(0, blk), slice(blk, 2 * blk)
    near_mx = {}

    def near_scores(hh):
        own_t, prev_t = bias_ref[hh, 0], bias_ref[hh, 1]
        mask_prev = jnp.where(keep(hh, j_prev) & (qi >= 1), 0.0, NEG_INF)
        mask_own = jnp.where(keep(hh, first_own), 0.0, NEG_INF)
        far_t = jnp.full((blk, blk), far_bias[hh], F32)
        parts = [scores(hh, j_prev) + mask_prev + jnp.concatenate([prev_t, far_t], axis=1),
                 scores(hh, first_own) + mask_own + jnp.concatenate([own_t, prev_t], axis=1)]
        mx = None
        for i, part in enumerate(parts):
            near_ref[hh, i * blk:(i + 1) * blk] = part
            part_mx = jnp.max(part, axis=0, keepdims=True)
            mx = part_mx if mx is None else jnp.maximum(mx, part_mx)
        rows = pl.ds(pl.multiple_of((first_own + 1) * blk, blk), blk)
        last = jnp.dot(k_ref[0, rows, :], q_m_t[hh][:, hi], preferred_element_type=F32) + own_t
        own1_ref[hh] = last
        near_mx[hh] = (mx[:, lo], jnp.maximum(mx[:, hi], jnp.max(last, axis=0, keepdims=True)))

    def near_update():
        for hh in range(2):
            v_near = [vt_ref[hh, j_prev], vt_ref[hh, first_own], vt_ref[hh, first_own + 1]]
            _flash_update(near_ref[hh, :, lo], near_mx[hh][0], v_near[:2],
                          m_ref, acc_ref, hh, cols=lo)
            _flash_update(jnp.concatenate([near_ref[hh, :, hi], own1_ref[hh]], axis=0),
                          near_mx[hh][1], v_near, m_ref, acc_ref, hh, cols=hi)

    def scores_into(j, slot, hh):
        _store_scores(scores(hh, j), s_ref, mx_ref, (slot, hh))

    def update(j, slot, hh):
        _flash_update(s_ref[slot, hh, :, :tq], mx_ref[slot, hh], vt_ref[hh, j],
                      m_ref, acc_ref, hh, keep=keep(hh, j), const=far_bias[hh])

    _pipelined_blocks(jnp.maximum(first_own - 1, 0), scores_into, update,
                      near_scores, near_update, next_tile_scores)
    o_ref[qi] = _head_outputs(acc_ref)


def _moba_attention(qt3, k3, vt, rel_table, bias_t, n_heads):
    b, s, d = k3.shape
    blk = MOBA_BLOCK
    tq = qt3.shape[2]
    assert tq == MOBA_Q_TILE and s % tq == 0 and vt.shape == qt3.shape
    npair = n_heads // 2
    nblk = s // blk
    nq = s // tq
    head_pair_tiles = pl.BlockSpec((nq, LANES, tq), lambda bi, hp: (bi, hp, 0))
    return pl.pallas_call(
        _moba_kernel, grid=(b, npair),
        in_specs=[
            pl.BlockSpec(memory_space=pltpu.SMEM),
            head_pair_tiles,
            pl.BlockSpec((1, s, LANES), lambda bi, hp: (bi, 0, hp)),
            head_pair_tiles,
            pl.BlockSpec((2, 2, blk, blk), lambda bi, hp: (hp, 0, 0, 0)),
        ],
        out_specs=head_pair_tiles,
        out_shape=jax.ShapeDtypeStruct(qt3.shape, BF16),
        scratch_shapes=[
            pltpu.VMEM((2, nblk, V_ROWS, blk), BF16),
            pltpu.VMEM((nblk, LANES), F32),
            pltpu.VMEM((2, nq, nblk, tq), F32),
            pltpu.VMEM((2, 2, blk, tq + SCORE_PITCH_PAD), F32),
            pltpu.VMEM((2, 2, 1, tq), F32),
            pltpu.VMEM((2, 2 * blk, tq), F32),
            pltpu.VMEM((2, blk, blk), F32),
            pltpu.VMEM((2, 1, tq), F32),
            pltpu.VMEM((2, V_ROWS, tq), F32),
        ],
        compiler_params=_params(2), name="moba_attention")(
            rel_table, qt3, k3, vt, bias_t)


class _Ple(NamedTuple):
    g: jax.Array
    w_gate: jax.Array
    p: _Slab
    w_up: jax.Array

    def in_specs(self, tm):
        layer = self.p.layer
        return [_resident(self.g), _resident(self.w_gate),
                pl.BlockSpec((None, tm, self.p.shape[1]), lambda i: (layer, i, 0)),
                _resident(self.w_up)]

    def operands(self):
        return tuple(_array(a) for a in self)


def _ple_update(x, g_ref, wg_ref, p_ref, wu_ref):
    u = _rmsnorm(x, g_ref[...]).astype(BF16)
    gate = jax.nn.sigmoid(jnp.dot(u, wg_ref[...], preferred_element_type=F32))
    up = jnp.dot(p_ref[...].astype(BF16), wu_ref[...], preferred_element_type=F32)
    return x + gate * up


def _oproj_ffn_kernel(h_ref, ot_ref, wo_ref, g_ref, win_ref, wout_ref, *rest, tf):
    d_ff = wout_ref.shape[0]
    h1 = h_ref[...] + lax.dot_general(ot_ref[...], wo_ref[...], (((0,), (0,)), ((), ())),
                                      preferred_element_type=F32)
    u = _rmsnorm(h1, g_ref[...]).astype(BF16)
    acc = h1
    for c in range(d_ff // tf):
        gate = jnp.dot(u, win_ref[:, c * tf:(c + 1) * tf], preferred_element_type=F32)
        up = jnp.dot(u, win_ref[:, d_ff + c * tf:d_ff + (c + 1) * tf],
                     preferred_element_type=F32)
        act = (gate * jax.nn.sigmoid(gate) * up).astype(BF16)
        acc = acc + jnp.dot(act, wout_ref[c * tf:(c + 1) * tf, :],
                            preferred_element_type=F32)
    *ple_refs, out_ref = rest
    if ple_refs:
        *ple_refs, final_g_ref = ple_refs
        acc = _rmsnorm(_ple_update(acc, *ple_refs), final_g_ref[...])
    out_ref[...] = acc


def _oproj_ffn(h2, o_t, w_o, g, w_in, w_out, *, tm, tf, last=None):
    n, d = h2.shape
    d_ff = w_out.shape[0]
    assert d_ff % tf == 0 and o_t.shape == (n // tm, d, tm)
    row = pl.BlockSpec((tm, d), lambda i: (i, 0))
    weights = (w_o, g, w_in, w_out)
    in_specs = ([row, pl.BlockSpec((None, d, tm), lambda i: (i, 0, 0))]
                + [_resident(w) for w in weights])
    operands = (h2, o_t, *map(_array, weights))
    if last is not None:
        ple, final_g = last
        in_specs += ple.in_specs(tm) + [_resident(final_g)]
        operands += ple.operands() + (final_g,)
    return pl.pallas_call(
        functools.partial(_oproj_ffn_kernel, tf=tf), grid=(n // tm,),
        in_specs=in_specs, out_specs=row,
        out_shape=jax.ShapeDtypeStruct((n, d), F32),
        compiler_params=_params(1), name="oproj_ffn")(*operands)


def _ple_next_kernel(x_ref, g_ref, wg_ref, p_ref, wu_ref, *refs, n_mixer, jobs):
    mixer_refs, cast_in, (y_ref, *out_refs), cast_out = _split_refs(
        refs, n_mixer, len(jobs))
    y = _ple_update(x_ref[...], g_ref, wg_ref, p_ref, wu_ref)
    y_ref[...] = y
    _mixer_inputs(y, mixer_refs, out_refs,
                  functools.partial(_cast_rows, jobs, cast_in, cast_out))


def _ple_next(h2, ple, mixer, casts, *, tm):
    n, d = h2.shape
    row = pl.BlockSpec((tm, d), lambda i: (i, 0))
    n_mixer = len(mixer.operands())
    cast_in, cast_out, cast_shapes, jobs = _cast_plans(casts, n // tm)
    h, *outs = pl.pallas_call(
        functools.partial(_ple_next_kernel, n_mixer=n_mixer, jobs=jobs),
        grid=(n // tm,),
        in_specs=[row] + ple.in_specs(tm) + mixer.in_specs() + cast_in,
        out_specs=[row] + mixer.out_specs(tm) + cast_out,
        out_shape=[jax.ShapeDtypeStruct((n, d), F32)] + mixer.out_shapes(n, tm) + cast_shapes,
        compiler_params=_params(1), name="ple_proj")(
            h2, *ple.operands(), *mixer.operands(), *(c.w.stack for c in casts))
    return h, outs[:len(outs) - len(casts)], outs[len(outs) - len(casts):]


def _row_tile(n, want):
    t = min(want, n)
    assert n % t == 0
    return t


def _col_tile(n, want):
    t = min(want, n)
    while n % t:
        t -= LANES
    return t


def kernel(x, p, attn_norm_g, fox_w_in, fox_b_f, fox_w_o, moba_w_in, moba_w_o, rel_bias_table,
           ffn_norm_g, ffn_w_in, ffn_w_out, ple_norm_g, ple_w_gate, ple_w_up, final_norm_g):
    b, s, d = x.shape
    depth = p.shape[0]
    n_heads = rel_bias_table.shape[1]
    assert d == n_heads * HEAD_DIM and n_heads % 2 == 0 and n_heads <= LANES
    n = b * s
    tm = _row_tile(n, DENSE_ROW_TILE)
    tf = _col_tile(ffn_w_out.shape[1], FFN_COL_CHUNK)

    def row_vec(v):
        return v.reshape(1, -1).astype(F32)

    w_o_mix = (fox_w_o, moba_w_o)
    p3 = p.reshape(depth, n, -1)
    fox_wt = jnp.swapaxes(fox_w_in, 1, 2)

    def qkv_cast(i):
        if i % 2 == 0:
            return _Cast(_Slab(fox_wt, i // 2), 3 * d, scaled=d)
        return _Cast(_Slab(moba_w_in, i // 2), d, scaled=d, transpose=True)

    def layer_casts(i):
        jobs = [_Cast(_Slab(w, layer), w.shape[1]) for w, layer in (
            (w_o_mix[i % 2], i // 2), (ffn_w_in, i), (ffn_w_out, i), (ple_w_gate, i),
            (ple_w_up, i))]
        return jobs + ([qkv_cast(i + 1)] if i + 1 < depth else [])

    def mixer(i, w_qkv):
        parts = [row_vec(attn_norm_g[i]), w_qkv]
        if i % 2 == 0:
            pad_heads = LANES - n_heads
            parts += [jnp.pad(fox_wt[i // 2, 3 * d:], ((0, pad_heads), (0, 0))),
                      jnp.pad(row_vec(fox_b_f[i // 2]), ((0, 0), (0, pad_heads)))]
        return _Mixer(*parts)

    rel_table = rel_bias_table.astype(F32)
    bias_t = _bias_tiles(rel_table, MOBA_BLOCK)

    h = x.reshape(n, d).astype(F32)
    row = lax.broadcasted_iota(jnp.int32, (3 * d, 1), 0)
    w_qkv = (fox_wt[0, :3 * d] * jnp.where(row < d, Q_SCALE, 1.0)).astype(BF16)
    mixed, weights = _project(h, mixer(0, w_qkv), layer_casts(0), tm=tm)
    for i in range(depth):
        w_o, w_ffn_in, w_ffn_out, w_ple_gate, w_ple_up, *w_qkv_next = weights
        qt, k3, vt = mixed[0], mixed[1].reshape(b, s, d), mixed[2]
        if i % 2 == 0:
            aqt, ak = _cumsum(mixed[3].reshape(b, s, LANES))
            o_t = _fox_attention(qt, k3, vt, aqt, ak, n_heads)
        else:
            o_t = _moba_attention(qt, k3, vt, rel_table, bias_t, n_heads)
        ple = _Ple(row_vec(ple_norm_g[i]), w_ple_gate, _Slab(p3, i), w_ple_up)
        last = (ple, row_vec(final_norm_g)) if i + 1 == depth else None
        h = _oproj_ffn(h, o_t, w_o, row_vec(ffn_norm_g[i]), w_ffn_in, w_ffn_out,
                       tm=tm, tf=tf, last=last)
        if last is None:
            h, mixed, weights = _ple_next(h, ple, mixer(i + 1, *w_qkv_next),
                                          layer_casts(i + 1), tm=tm)
    return h.reshape(b, s, d).astype(x.dtype)
```

```python
import functools
import math
from typing import NamedTuple, Optional

import numpy as np
import jax
import jax.numpy as jnp
from jax import lax
from jax.experimental import pallas as pl
from jax.experimental.pallas import tpu as pltpu

F32 = jnp.float32
BF16 = jnp.bfloat16

RMS_EPS = 1e-6
HEAD_DIM = 64
MOBA_BLOCK = 256
MOBA_TOP_K = 3
REL_BUCKETS = 32
REL_MAX_DIST = 128

LANES = 128
F32_SUBLANES = 8
BF16_SUBLANES = 16
V7X_VMEM_BYTES = 64 * 1024 * 1024
V_ROWS = HEAD_DIM + BF16_SUBLANES
SPLIT_TERMS = 3
GATE_LANES = 8
SCORE_PITCH_PAD = LANES
LOG2E = math.log2(math.e)
Q_SCALE = HEAD_DIM ** -0.5 * LOG2E
MOBA_Q_TILE = 2 * MOBA_BLOCK
ATTN_Q_TILE = MOBA_Q_TILE
DENSE_ROW_TILE = ATTN_Q_TILE
CUMSUM_BLK = ATTN_Q_TILE
PROJ_COL_CHUNK = 512
FFN_COL_CHUNK = 256
VMEM_LIMIT_BYTES = V7X_VMEM_BYTES // 8 * 7
NEG_INF = float("-inf")


def _params(n_axes):
    return pltpu.CompilerParams(
        dimension_semantics=("arbitrary",) * n_axes,
        vmem_limit_bytes=VMEM_LIMIT_BYTES)


def _split3(x):
    x1 = x.astype(BF16)
    r1 = x - x1.astype(F32)
    x2 = r1.astype(BF16)
    x3 = (r1 - x2.astype(F32)).astype(BF16)
    return x1, x2, x3


def _rmsnorm(x, g):
    ms = jnp.mean(x * x, axis=-1, keepdims=True)
    return x * lax.rsqrt(ms + RMS_EPS) * g


def _log_sigmoid(x):
    return jnp.minimum(x, 0.0) - jnp.log1p(jnp.exp(-jnp.abs(x)))


class _Slab(NamedTuple):
    stack: jax.Array
    layer: int

    @property
    def shape(self):
        return self.stack.shape[1:]


def _resident(x):
    if isinstance(x, _Slab):
        layer, zeros = x.layer, (0,) * len(x.shape)
        return pl.BlockSpec((None,) + x.shape, lambda i: (layer,) + zeros,
                            pipeline_mode=pl.Buffered(1))
    zeros = (0,) * x.ndim
    return pl.BlockSpec(x.shape, lambda i: zeros, pipeline_mode=pl.Buffered(1))


def _array(x):
    return x.stack if isinstance(x, _Slab) else x


class _Cast(NamedTuple):
    w: _Slab
    rows: int
    scaled: int = 0
    transpose: bool = False

    def plan(self, steps):
        cols = self.w.shape[1]
        min_rows = LANES if self.transpose else BF16_SUBLANES
        assert self.rows % min_rows == 0 and cols % LANES == 0
        chunks = math.gcd(steps, self.rows // min_rows)
        span, layer, chunk = steps // chunks, self.w.layer, self.rows // chunks
        in_spec = pl.BlockSpec((None, chunk, cols), lambda i: (layer, i // span, 0))
        if self.transpose:
            return (in_spec, pl.BlockSpec((cols, chunk), lambda i: (0, i // span)),
                    jax.ShapeDtypeStruct((cols, self.rows), BF16), span)
        return (in_spec, pl.BlockSpec((chunk, cols), lambda i: (i // span, 0)),
                jax.ShapeDtypeStruct((self.rows, cols), BF16), span)


def _cast_plans(casts, steps):
    plans = [c.plan(steps) for c in casts]
    return ([p[0] for p in plans], [p[1] for p in plans], [p[2] for p in plans],
            tuple((c.transpose, c.scaled, p[3]) for c, p in zip(casts, plans)))


def _cast_rows(jobs, in_refs, out_refs):
    for (transpose, scaled, span), in_ref, out_ref in zip(jobs, in_refs, out_refs):
        w = in_ref[...].T if transpose else in_ref[...]
        if scaled:
            first_row = 0 if transpose else pl.program_id(0) // span * w.shape[0]
            row = first_row + lax.broadcasted_iota(jnp.int32, (w.shape[0], 1), 0)
            w = w * jnp.where(row < scaled, Q_SCALE, 1.0)
        out_ref[...] = w.astype(BF16)


class _Mixer(NamedTuple):
    g: jax.Array
    w_qkv: jax.Array
    w_f: Optional[jax.Array] = None
    b_f: Optional[jax.Array] = None

    def operands(self):
        return tuple(_array(a) for a in self if a is not None)

    def in_specs(self):
        return [_resident(a) for a in self if a is not None]

    def out_specs(self, tm):
        d = self.w_qkv.shape[1]
        specs = [pl.BlockSpec((None, d, tm), lambda i: (i, 0, 0)),
                 pl.BlockSpec((tm, d), lambda i: (i, 0)),
                 pl.BlockSpec((d, tm), lambda i: (0, i))]
        if self.w_f is not None:
            specs.append(pl.BlockSpec((tm, LANES), lambda i: (i, 0)))
        return specs

    def out_shapes(self, n, tm):
        d = self.w_qkv.shape[1]
        shapes = [jax.ShapeDtypeStruct((n // tm, d, tm), BF16),
                  jax.ShapeDtypeStruct((n, d), BF16), jax.ShapeDtypeStruct((d, n), BF16)]
        if self.w_f is not None:
            shapes.append(jax.ShapeDtypeStruct((n, LANES), F32))
        return shapes


def _chunks(total, want):
    step = want if total % want == 0 else total
    return [slice(c * step, (c + 1) * step) for c in range(total // step)]


def _mixer_inputs(y, mixer_refs, out_refs):
    g_ref, wt_ref = mixer_refs[:2]
    d = wt_ref.shape[1]
    u = _rmsnorm(y, g_ref[...]).astype(BF16)

    def contract_last(a, b):
        return lax.dot_general(a, b, (((1,), (1,)), ((), ())), preferred_element_type=F32)

    for first_row, out_ref in ((0, out_refs[0]), (2 * d, out_refs[2])):
        for rows in _chunks(d, PROJ_COL_CHUNK):
            w_rows = wt_ref[first_row + rows.start:first_row + rows.stop, :]
            out_ref[rows, :] = contract_last(w_rows, u).astype(BF16)
    for cols in _chunks(d, PROJ_COL_CHUNK):
        out_refs[1][:, cols] = contract_last(
            u, wt_ref[d + cols.start:d + cols.stop, :]).astype(BF16)
    if len(mixer_refs) > 2:
        wft_ref, bf_ref = mixer_refs[2:]
        f_logit = contract_last(u, wft_ref[...].astype(BF16)) + bf_ref[...]
        out_refs[3][...] = _log_sigmoid(f_logit)


def _split_refs(refs, n_mixer, n_cast):
    n_out = len(refs) - n_cast
    return (refs[:n_mixer], refs[n_mixer:n_mixer + n_cast],
            refs[n_mixer + n_cast:n_out], refs[n_out:])


def _proj_kernel(x_ref, *refs, n_mixer, jobs):
    mixer_refs, cast_in, out_refs, cast_out = _split_refs(refs, n_mixer, len(jobs))
    _mixer_inputs(x_ref[...], mixer_refs, out_refs)
    _cast_rows(jobs, cast_in, cast_out)


def _project(h2, mixer, casts, *, tm):
    n, d = h2.shape
    n_mixer = len(mixer.operands())
    cast_in, cast_out, cast_shapes, jobs = _cast_plans(casts, n // tm)
    outs = pl.pallas_call(
        functools.partial(_proj_kernel, n_mixer=n_mixer, jobs=jobs),
        grid=(n // tm,),
        in_specs=[pl.BlockSpec((tm, d), lambda i: (i, 0))] + mixer.in_specs() + cast_in,
        out_specs=mixer.out_specs(tm) + cast_out,
        out_shape=mixer.out_shapes(n, tm) + cast_shapes,
        compiler_params=_params(1), name="proj")(
            h2, *mixer.operands(), *(c.w.stack for c in casts))
    return outs[:len(outs) - len(casts)], outs[len(outs) - len(casts):]


def _cumsum_kernel(lf_ref, aqt_ref, ak_ref, carry_ref):
    @pl.when(pl.program_id(1) == 0)
    def _():
        carry_ref[...] = jnp.zeros_like(carry_ref)

    t = lf_ref.shape[1]
    row = lax.broadcasted_iota(jnp.int32, (t, t), 0)
    col = lax.broadcasted_iota(jnp.int32, (t, t), 1)
    tril = jnp.where(col <= row, 1.0, 0.0).astype(BF16)
    x1, x2, x3 = _split3(lf_ref[0])
    cs = (jnp.dot(tril, x1, preferred_element_type=F32)
          + jnp.dot(tril, x2, preferred_element_type=F32)
          + jnp.dot(tril, x3, preferred_element_type=F32))
    cs = cs + carry_ref[0:1, :]
    carry_ref[...] = jnp.broadcast_to(cs[t - 1:t, :], carry_ref.shape)

    src = lax.broadcasted_iota(jnp.int32, (LANES, LANES), 0)
    dst = lax.broadcasted_iota(jnp.int32, (LANES, LANES), 1)
    lane = lax.broadcasted_iota(jnp.int32, (1, LANES), 1) & (GATE_LANES - 1)
    aq = jnp.where((lane >= SPLIT_TERMS) & (lane < 2 * SPLIT_TERMS), 1.0, 0.0)
    ak = jnp.where(lane < SPLIT_TERMS, 1.0, 0.0)
    for i, part in enumerate(_split3(cs * LOG2E)):
        to_q = jnp.where(dst == GATE_LANES * src + i, 1.0, 0.0).astype(BF16)
        to_k = jnp.where(dst == GATE_LANES * src + SPLIT_TERMS + i, 1.0, 0.0).astype(BF16)
        aq = aq + jnp.dot(part, to_q, preferred_element_type=F32)
        ak = ak - jnp.dot(part, to_k, preferred_element_type=F32)
    eye = jnp.where(src == dst, 1.0, 0.0).astype(BF16)
    aqt_ref[0] = lax.dot_general(eye, aq.astype(BF16), (((1,), (1,)), ((), ())),
                                 preferred_element_type=F32).astype(BF16)
    ak_ref[0] = ak.astype(BF16)


def _cumsum(lf3):
    b, s, _ = lf3.shape
    t = min(CUMSUM_BLK, s)
    nt = s // t
    spec = pl.BlockSpec((1, t, LANES), lambda i, j: (i, j, 0))
    return pl.pallas_call(
        _cumsum_kernel, grid=(b, nt),
        in_specs=[spec],
        out_specs=[pl.BlockSpec((1, LANES, t), lambda i, j: (i * nt + j, 0, 0)), spec],
        out_shape=[jax.ShapeDtypeStruct((b * nt, LANES, t), BF16),
                   jax.ShapeDtypeStruct(lf3.shape, BF16)],
        scratch_shapes=[pltpu.VMEM((F32_SUBLANES, LANES), F32)],
        compiler_params=_params(2), name="gate_cumsum")(lf3)


def _fill_value_rows(vt_ref, v_ref, blk):
    row = lax.broadcasted_iota(jnp.int32, (V_ROWS - HEAD_DIM, blk), 0)
    tail = jnp.where(row == 0, 1.0, 0.0).astype(BF16)
    for hh in range(2):
        for jb in range(v_ref.shape[1] // blk):
            head_rows = v_ref[hh * HEAD_DIM:(hh + 1) * HEAD_DIM, jb * blk:(jb + 1) * blk]
            vt_ref[hh, jb] = jnp.concatenate([head_rows, tail], axis=0)


def _flash_init(m_ref, acc_ref):
    m_ref[...] = jnp.full(m_ref.shape, NEG_INF, F32)
    acc_ref[...] = jnp.zeros(acc_ref.shape, F32)


def _pipelined_blocks(n, scores_into, update, final_scores, final_update, next_tile_scores):
    def step(j_next, j, slot):
        for hh in range(2):
            scores_into(j_next, 1 - slot, hh)
            update(j, slot, hh)

    def last_update_and_final(j, slot):
        for hh in range(2):
            final_scores(hh)
            update(j, slot, hh)
        next_tile_scores()
        final_update()

    def pair(jj, carry):
        j = 2 * jj
        step(j + 1, j, 0)
        step(j + 2, j + 1, 1)
        return carry
    lax.fori_loop(0, jnp.maximum(n - 1, 0) // 2, pair, 0)

    @pl.when(n % 2 == 1)
    def _():
        last_update_and_final(n - 1, 0)

    @pl.when((n % 2 == 0) & (n > 0))
    def _():
        step(n - 1, n - 2, 0)
        last_update_and_final(n - 1, 1)

    @pl.when(n == 0)
    def _():
        for hh in range(2):
            final_scores(hh)
        next_tile_scores()
        final_update()


def _store_scores(s_t, s_ref, mx_ref, idx):
    s_ref[idx + (slice(None), slice(0, s_t.shape[1]))] = s_t
    mx_ref[idx] = jnp.max(s_t, axis=0, keepdims=True)


def _flash_update(s_t, mx, v_rows, m_ref, acc_ref, hh, keep=None, const=None,
                  cols=slice(None)):
    m_old = m_ref[hh, :, cols]
    if const is not None:
        mx = mx + const
    if keep is not None:
        mx = jnp.where(keep, mx, NEG_INF)
    m_new = jnp.maximum(m_old, mx)
    m_safe = jnp.where(m_new == NEG_INF, 0.0, m_new)
    shift = m_safe if const is None else m_safe - const
    if keep is not None:
        shift = jnp.where(keep, shift, float("inf"))
    p = jnp.exp2(s_t - shift).astype(BF16)
    alpha = jnp.exp2(m_old - m_safe)
    m_ref[hh, :, cols] = m_new
    if not isinstance(v_rows, (list, tuple)):
        v_rows = [v_rows]
    keys = p.shape[0] // len(v_rows)
    acc = alpha * acc_ref[hh, :, cols]
    for i, v_i in enumerate(v_rows):
        acc = acc + jnp.dot(v_i, p[i * keys:(i + 1) * keys], preferred_element_type=F32)
    acc_ref[hh, :, cols] = acc


def _head_outputs(acc_ref):
    outs = []
    for hh in range(2):
        acc = acc_ref[hh]
        outs.append(acc[:HEAD_DIM] / acc[HEAD_DIM:HEAD_DIM + 1])
    return jnp.concatenate(outs, axis=0).astype(BF16)


def _for_each_query_tile(nq, tile):
    def body(qi, carry):
        tile(qi)
        return carry
    lax.fori_loop(0, nq, body, 0)


def _head_row_mask(hh):
    row = lax.broadcasted_iota(jnp.int32, (LANES, 1), 0)
    return (row < HEAD_DIM) if hh == 0 else (row >= HEAD_DIM)


def _fox_kernel(qt_ref, k_ref, v_ref, *refs):
    vt_ref = refs[3]
    nq, _, blk = qt_ref.shape
    _fill_value_rows(vt_ref, v_ref, blk)
    hp = pl.program_id(1)
    _for_each_query_tile(nq, lambda qi: _fox_tile(qi, hp, qt_ref, k_ref, *refs))


def _fox_tile(qi, hp, qt_ref, k_ref, aqt_ref, ak_ref, o_ref,
              vt_ref, s_ref, mx_ref, diag_ref, m_ref, acc_ref):
    nq, _, blk = qt_ref.shape
    gate_row = lax.broadcasted_iota(jnp.int32, (LANES, 1), 0)

    def query_operands(tile):
        qt, aqt = qt_ref[tile], aqt_ref[tile]
        out = []
        for hh in range(2):
            first = GATE_LANES * (2 * hp + hh)
            own_gate = (gate_row >= first) & (gate_row < first + GATE_LANES)
            q_rows = jnp.where(_head_row_mask(hh), qt, jnp.zeros_like(qt))
            g_rows = jnp.where(own_gate, aqt, jnp.zeros_like(aqt))
            out.append(jnp.concatenate([q_rows, g_rows], axis=0))
        return out

    def block_scores(w, j, hh):
        rows = pl.ds(pl.multiple_of(j * blk, blk), blk)
        keys = jnp.concatenate([k_ref[0, rows, :], ak_ref[0, rows, :]], axis=1)
        return jnp.dot(keys, w[hh], preferred_element_type=F32)

    w_q = query_operands(qi)

    def scores_into(j, slot, hh):
        _store_scores(block_scores(w_q, j, hh), s_ref, mx_ref, (slot, hh))

    def next_tile_scores():
        w_next = query_operands(jnp.minimum(qi + 1, nq - 1))
        for hh in range(2):
            _store_scores(block_scores(w_next, 0, hh), s_ref, mx_ref, (0, hh))

    def update(j, slot, hh):
        _flash_update(s_ref[slot, hh, :, :blk], mx_ref[slot, hh], vt_ref[hh, j],
                      m_ref, acc_ref, hh)

    def diagonal_scores(hh):
        diag_ref[hh] = block_scores(w_q, qi, hh)

    def diagonal_update():
        half = blk // 2
        lo, hi = slice(0, half), slice(half, blk)
        causal_lo = (lax.broadcasted_iota(jnp.int32, (half, half), 0)
                     <= lax.broadcasted_iota(jnp.int32, (half, half), 1))
        causal_hi = (lax.broadcasted_iota(jnp.int32, (blk, half), 0)
                     <= lax.broadcasted_iota(jnp.int32, (blk, half), 1) + half)
        for hh in range(2):
            v_rows = vt_ref[hh, qi]
            s_lo = jnp.where(causal_lo, diag_ref[hh, lo, lo], NEG_INF)
            _flash_update(s_lo, jnp.max(s_lo, axis=0, keepdims=True),
                          v_rows[:, lo], m_ref, acc_ref, hh, cols=lo)
            s_hi = jnp.where(causal_hi, diag_ref[hh, :, hi], NEG_INF)
            _flash_update(s_hi, jnp.max(s_hi, axis=0, keepdims=True),
                          v_rows, m_ref, acc_ref, hh, cols=hi)

    _flash_init(m_ref, acc_ref)
    _pipelined_blocks(qi, scores_into, update, diagonal_scores, diagonal_update,
                      next_tile_scores)
    o_ref[qi] = _head_outputs(acc_ref)


def _fox_attention(qt3, k3, vt, aqt3, ak3, n_heads):
    b, s, d = k3.shape
    assert n_heads * GATE_LANES <= LANES
    blk = qt3.shape[2]
    assert aqt3.shape[2] == blk and s % blk == 0
    npair = n_heads // 2
    nq = s // blk
    head_pair_tiles = pl.BlockSpec((nq, LANES, blk), lambda bi, hp: (bi, hp, 0))
    return pl.pallas_call(
        _fox_kernel, grid=(b, npair),
        in_specs=[
            head_pair_tiles,
            pl.BlockSpec((1, s, LANES), lambda bi, hp: (bi, 0, hp)),
            pl.BlockSpec((LANES, s), lambda bi, hp: (hp, bi)),
            pl.BlockSpec((nq, LANES, blk), lambda bi, hp: (bi, 0, 0)),
            pl.BlockSpec((1, s, LANES), lambda bi, hp: (bi, 0, 0)),
        ],
        out_specs=head_pair_tiles,
        out_shape=jax.ShapeDtypeStruct(qt3.shape, BF16),
        scratch_shapes=[
            pltpu.VMEM((2, s // blk, V_ROWS, blk), BF16),
            pltpu.VMEM((2, 2, blk, blk + SCORE_PITCH_PAD), F32),
            pltpu.VMEM((2, 2, 1, blk), F32),
            pltpu.VMEM((2, blk, blk), F32),
            pltpu.VMEM((2, 1, blk), F32),
            pltpu.VMEM((2, V_ROWS, blk), F32),
        ],
        compiler_params=_params(2), name="fox_attention")(qt3, k3, vt, aqt3, ak3)


def _t5_bucket_np(n):
    max_exact = REL_BUCKETS // 2
    nf = np.maximum(n, 1).astype(np.float64)
    large = max_exact + (np.log(nf / max_exact) / math.log(REL_MAX_DIST / max_exact)
                         * (REL_BUCKETS - max_exact)).astype(np.int32)
    return np.where(n < max_exact, n, np.minimum(large, REL_BUCKETS - 1)).astype(np.int32)


def _bucket_tiles(blk):
    key = np.arange(blk)[:, None]
    qry = np.arange(blk)[None, :]
    own = np.where(key <= qry, _t5_bucket_np(np.maximum(qry - key, 0)), -1)
    prev = _t5_bucket_np(blk + qry - key)
    return np.stack([own, prev]).astype(np.int32)


def _bias_kernel(tab_ref, bucket_ref, o_ref):
    h = pl.program_id(0)
    bucket = bucket_ref[...]
    acc = jnp.where(bucket < 0, NEG_INF, 0.0).astype(F32)
    for bkt in range(REL_BUCKETS):
        acc = jnp.where(bucket == bkt, tab_ref[bkt, h] * LOG2E, acc)
    o_ref[0] = acc


def _bias_tiles(rel_table, blk):
    n_heads = rel_table.shape[1]
    buckets = jnp.asarray(_bucket_tiles(blk))
    return pl.pallas_call(
        _bias_kernel, grid=(n_heads,),
        in_specs=[pl.BlockSpec(memory_space=pltpu.SMEM),
                  pl.BlockSpec((2, blk, blk), lambda h: (0, 0, 0))],
        out_specs=pl.BlockSpec((1, 2, blk, blk), lambda h: (h, 0, 0, 0)),
        out_shape=jax.ShapeDtypeStruct((n_heads, 2, blk, blk), F32),
        compiler_params=_params(1), name="t5_bias_tiles")(rel_table, buckets)


def _select_blocks(qt_ref, km_ref, sel_ref):
    nq, _, tq = qt_ref.shape
    blk = MOBA_BLOCK
    nblk = km_ref.shape[0]
    assert blk & (blk - 1) == 0
    qt_all = jnp.concatenate([qt_ref[t] for t in range(nq)], axis=1)
    lane = lax.broadcasted_iota(jnp.int32, (1, LANES), 1)
    km_parts = _split3(km_ref[...])
    terms = jnp.dot(
        jnp.concatenate([jnp.where((lane < HEAD_DIM) == (hh == 0), part, jnp.zeros_like(part))
                         for hh in range(2) for part in km_parts], axis=0),
        qt_all, preferred_element_type=F32)
    blk_id = lax.broadcasted_iota(jnp.int32, (nblk, nq * tq), 0)
    own = lax.shift_right_logical(lax.broadcasted_iota(jnp.int32, (nblk, nq * tq), 1),
                                  blk.bit_length() - 1)
    past = blk_id < own
    for hh in range(2):
        gate = None
        for i in range(SPLIT_TERMS):
            first = (SPLIT_TERMS * hh + i) * nblk
            term = terms[first:first + nblk]
            gate = term if gate is None else gate + term
        work = jnp.where(past, gate, NEG_INF)
        picked = jnp.zeros(gate.shape, F32)
        for _ in range(MOBA_TOP_K):
            best = jnp.max(work, axis=0, keepdims=True)
            first = jnp.min(jnp.where(work == best, blk_id, nblk), axis=0, keepdims=True)
            hit = blk_id == first
            picked = jnp.where(hit, 1.0, picked)
            work = jnp.where(hit, NEG_INF, work)
        keep_all = ((picked > 0.5) & past) | (blk_id == own) | (blk_id == (own | 1))
        keep_f = jnp.where(keep_all, 1.0, 0.0)
        for t in range(nq):
            sel_ref[hh, t] = keep_f[:, t * tq:(t + 1) * tq]


def _moba_kernel(tab_ref, qt_ref, k_ref, v_ref, bias_ref, o_ref,
                 vt_ref, km_ref, sel_ref, *scratch):
    blk = MOBA_BLOCK

    def block_mean(jb, carry):
        rows = pl.ds(pl.multiple_of(jb * blk, blk), blk)
        km_ref[pl.ds(jb, 1), :] = jnp.mean(k_ref[0, rows, :].astype(F32),
                                            axis=0, keepdims=True)
        return carry
    lax.fori_loop(0, k_ref.shape[1] // blk, block_mean, 0)
    _fill_value_rows(vt_ref, v_ref, blk)
    _select_blocks(qt_ref, km_ref, sel_ref)
    hp = pl.program_id(1)
    _for_each_query_tile(
        qt_ref.shape[0],
        lambda qi: _moba_tile(qi, hp, tab_ref, qt_ref, k_ref, bias_ref, o_ref,
                              vt_ref, sel_ref, *scratch))


def _moba_tile(qi, hp, tab_ref, qt_ref, k_ref, bias_ref, o_ref,
               vt_ref, sel_ref, s_ref, mx_ref, near_ref, own1_ref, m_ref, acc_ref):
    blk = MOBA_BLOCK
    nq, _, tq = qt_ref.shape
    assert tq == 2 * blk and blk >= REL_MAX_DIST
    first_own = 2 * qi

    def head_queries(tile):
        qt = qt_ref[tile]
        return [jnp.where(_head_row_mask(hh), qt, jnp.zeros_like(qt)) for hh in range(2)]

    q_m_t = head_queries(qi)

    def scores(hh, j, queries=q_m_t):
        rows = pl.ds(pl.multiple_of(j * blk, blk), blk)
        return jnp.dot(k_ref[0, rows, :], queries[hh], preferred_element_type=F32)

    def next_tile_scores():
        q_next = head_queries(jnp.minimum(qi + 1, nq - 1))
        for hh in range(2):
            _store_scores(scores(hh, 0, q_next), s_ref, mx_ref, (0, hh))

    def keep(hh, j):
        return sel_ref[hh, qi, pl.ds(j, 1), :] > 0.5

    _flash_init(m_ref, acc_ref)
    far_bias = [tab_ref[REL_BUCKETS - 1, 2 * hp + hh] * LOG2E for hh in range(2)]

    j_prev = jnp.maximum(first_own - 1, 0)
    lo, hi = slice(0, blk), slice(blk, 2 * blk)
    near_mx = {}

    def near_scores(hh):
        own_t, prev_t = bias_ref[hh, 0], bias_ref[hh, 1]
        mask_prev = jnp.where(keep(hh, j_prev) & (qi >= 1), 0.0, NEG_INF)
        mask_own = jnp.where(keep(hh, first_own), 0.0, NEG_INF)
        far_t = jnp.full((blk, blk), far_bias[hh], F32)
        parts = [scores(hh, j_prev) + mask_prev + jnp.concatenate([prev_t, far_t], axis=1),
                 scores(hh, first_own) + mask_own + jnp.concatenate([own_t, prev_t], axis=1)]
        mx = None
        for i, part in enumerate(parts):
            near_ref[hh, i * blk:(i + 1) * blk] = part
            part_mx = jnp.max(part, axis=0, keepdims=True)
            mx = part_mx if mx is None else jnp.maximum(mx, part_mx)
        rows = pl.ds(pl.multiple_of((first_own + 1) * blk, blk), blk)
        last = jnp.dot(k_ref[0, rows, :], q_m_t[hh][:, hi], preferred_element_type=F32) + own_t
        own1_ref[hh] = last
        near_mx[hh] = (mx[:, lo], jnp.maximum(mx[:, hi], jnp.max(last, axis=0, keepdims=True)))

    def near_update():
        for hh in range(2):
            v_near = [vt_ref[hh, j_prev], vt_ref[hh, first_own], vt_ref[hh, first_own + 1]]
            _flash_update(near_ref[hh, :, lo], near_mx[hh][0], v_near[:2],
                          m_ref, acc_ref, hh, cols=lo)
            _flash_update(jnp.concatenate([near_ref[hh, :, hi], own1_ref[hh]], axis=0),
                          near_mx[hh][1], v_near, m_ref, acc_ref, hh, cols=hi)

    def scores_into(j, slot, hh):
        _store_scores(scores(hh, j), s_ref, mx_ref, (slot, hh))

    def update(j, slot, hh):
        _flash_update(s_ref[slot, hh, :, :tq], mx_ref[slot, hh], vt_ref[hh, j],
                      m_ref, acc_ref, hh, keep=keep(hh, j), const=far_bias[hh])

    _pipelined_blocks(jnp.maximum(first_own - 1, 0), scores_into, update,
                      near_scores, near_update, next_tile_scores)
    o_ref[qi] = _head_outputs(acc_ref)


def _moba_attention(qt3, k3, vt, rel_table, bias_t, n_heads):
    b, s, d = k3.shape
    blk = MOBA_BLOCK
    tq = qt3.shape[2]
    assert tq == MOBA_Q_TILE and s % tq == 0
    npair = n_heads // 2
    nblk = s // blk
    nq = s // tq
    head_pair_tiles = pl.BlockSpec((nq, LANES, tq), lambda bi, hp: (bi, hp, 0))
    return pl.pallas_call(
        _moba_kernel, grid=(b, npair),
        in_specs=[
            pl.BlockSpec(memory_space=pltpu.SMEM),
            head_pair_tiles,
            pl.BlockSpec((1, s, LANES), lambda bi, hp: (bi, 0, hp)),
            pl.BlockSpec((LANES, s), lambda bi, hp: (hp, bi)),
            pl.BlockSpec((2, 2, blk, blk), lambda bi, hp: (hp, 0, 0, 0)),
        ],
        out_specs=head_pair_tiles,
        out_shape=jax.ShapeDtypeStruct(qt3.shape, BF16),
        scratch_shapes=[
            pltpu.VMEM((2, nblk, V_ROWS, blk), BF16),
            pltpu.VMEM((nblk, LANES), F32),
            pltpu.VMEM((2, nq, nblk, tq), F32),
            pltpu.VMEM((2, 2, blk, tq + SCORE_PITCH_PAD), F32),
            pltpu.VMEM((2, 2, 1, tq), F32),
            pltpu.VMEM((2, 2 * blk, tq), F32),
            pltpu.VMEM((2, blk, blk), F32),
            pltpu.VMEM((2, 1, tq), F32),
            pltpu.VMEM((2, V_ROWS, tq), F32),
        ],
        compiler_params=_params(2), name="moba_attention")(
            rel_table, qt3, k3, vt, bias_t)


class _Ple(NamedTuple):
    g: jax.Array
    w_gate: jax.Array
    p: _Slab
    w_up: jax.Array

    def in_specs(self, tm):
        layer = self.p.layer
        return [_resident(self.g), _resident(self.w_gate),
                pl.BlockSpec((None, tm, self.p.shape[1]), lambda i: (layer, i, 0)),
                _resident(self.w_up)]

    def operands(self):
        return tuple(_array(a) for a in self)


def _ple_update(x, g_ref, wg_ref, p_ref, wu_ref):
    u = _rmsnorm(x, g_ref[...]).astype(BF16)
    gate = jax.nn.sigmoid(jnp.dot(u, wg_ref[...], preferred_element_type=F32))
    up = jnp.dot(p_ref[...].astype(BF16), wu_ref[...], preferred_element_type=F32)
    return x + gate * up


def _oproj_ffn_kernel(h_ref, ot_ref, wo_ref, g_ref, win_ref, wout_ref, *rest, tf):
    d_ff = wout_ref.shape[0]
    h1 = h_ref[...] + lax.dot_general(ot_ref[...], wo_ref[...], (((0,), (0,)), ((), ())),
                                      preferred_element_type=F32)
    u = _rmsnorm(h1, g_ref[...]).astype(BF16)
    acc = h1
    for c in range(d_ff // tf):
        gate = jnp.dot(u, win_ref[:, c * tf:(c + 1) * tf], preferred_element_type=F32)
        up = jnp.dot(u, win_ref[:, d_ff + c * tf:d_ff + (c + 1) * tf],
                     preferred_element_type=F32)
        act = (gate * jax.nn.sigmoid(gate) * up).astype(BF16)
        acc = acc + jnp.dot(act, wout_ref[c * tf:(c + 1) * tf, :],
                            preferred_element_type=F32)
    *ple_refs, out_ref = rest
    if ple_refs:
        *ple_refs, final_g_ref = ple_refs
        acc = _rmsnorm(_ple_update(acc, *ple_refs), final_g_ref[...])
    out_ref[...] = acc


def _oproj_ffn(h2, o_t, w_o, g, w_in, w_out, *, tm, tf, last=None):
    n, d = h2.shape
    d_ff = w_out.shape[0]
    assert d_ff % tf == 0 and o_t.shape == (n // tm, d, tm)
    row = pl.BlockSpec((tm, d), lambda i: (i, 0))
    weights = (w_o, g, w_in, w_out)
    in_specs = ([row, pl.BlockSpec((None, d, tm), lambda i: (i, 0, 0))]
                + [_resident(w) for w in weights])
    operands = (h2, o_t, *map(_array, weights))
    if last is not None:
        ple, final_g = last
        in_specs += ple.in_specs(tm) + [_resident(final_g)]
        operands += ple.operands() + (final_g,)
    return pl.pallas_call(
        functools.partial(_oproj_ffn_kernel, tf=tf), grid=(n // tm,),
        in_specs=in_specs, out_specs=row,
        out_shape=jax.ShapeDtypeStruct((n, d), F32),
        compiler_params=_params(1), name="oproj_ffn")(*operands)


def _ple_next_kernel(x_ref, g_ref, wg_ref, p_ref, wu_ref, *refs, n_mixer, jobs):
    mixer_refs, cast_in, out_refs, cast_out = _split_refs(refs, n_mixer, len(jobs))
    y = _ple_update(x_ref[...], g_ref, wg_ref, p_ref, wu_ref)
    out_refs[0][...] = y
    _mixer_inputs(y, mixer_refs, out_refs[1:])
    _cast_rows(jobs, cast_in, cast_out)


def _ple_next(h2, ple, mixer, casts, *, tm):
    n, d = h2.shape
    row = pl.BlockSpec((tm, d), lambda i: (i, 0))
    n_mixer = len(mixer.operands())
    cast_in, cast_out, cast_shapes, jobs = _cast_plans(casts, n // tm)
    h, *outs = pl.pallas_call(
        functools.partial(_ple_next_kernel, n_mixer=n_mixer, jobs=jobs),
        grid=(n // tm,),
        in_specs=[row] + ple.in_specs(tm) + mixer.in_specs() + cast_in,
        out_specs=[row] + mixer.out_specs(tm) + cast_out,
        out_shape=[jax.ShapeDtypeStruct((n, d), F32)] + mixer.out_shapes(n, tm) + cast_shapes,
        compiler_params=_params(1), name="ple_proj")(
            h2, *ple.operands(), *mixer.operands(), *(c.w.stack for c in casts))
    return h, outs[:len(outs) - len(casts)], outs[len(outs) - len(casts):]


def _row_tile(n, want):
    t = min(want, n)
    assert n % t == 0
    return t


def _col_tile(n, want):
    t = min(want, n)
    while n % t:
        t -= LANES
    return t


def kernel(x, p, attn_norm_g, fox_w_in, fox_b_f, fox_w_o, moba_w_in, moba_w_o, rel_bias_table,
           ffn_norm_g, ffn_w_in, ffn_w_out, ple_norm_g, ple_w_gate, ple_w_up, final_norm_g):
    b, s, d = x.shape
    depth = p.shape[0]
    n_heads = rel_bias_table.shape[1]
    assert d == n_heads * HEAD_DIM and n_heads % 2 == 0 and n_heads <= LANES
    n = b * s
    tm = _row_tile(n, DENSE_ROW_TILE)
    tf = _col_tile(ffn_w_out.shape[1], FFN_COL_CHUNK)

    def row_vec(v):
        return v.reshape(1, -1).astype(F32)

    w_o_mix = (fox_w_o, moba_w_o)
    p3 = p.reshape(depth, n, -1)
    fox_wt = jnp.swapaxes(fox_w_in, 1, 2)

    def qkv_cast(i):
        if i % 2 == 0:
            return _Cast(_Slab(fox_wt, i // 2), 3 * d, scaled=d)
        return _Cast(_Slab(moba_w_in, i // 2), d, scaled=d, transpose=True)

    def layer_casts(i):
        jobs = [_Cast(_Slab(w, layer), w.shape[1]) for w, layer in (
            (w_o_mix[i % 2], i // 2), (ffn_w_in, i), (ffn_w_out, i), (ple_w_gate, i),
            (ple_w_up, i))]
        return jobs + ([qkv_cast(i + 1)] if i + 1 < depth else [])

    def mixer(i, w_qkv):
        parts = [row_vec(attn_norm_g[i]), w_qkv]
        if i % 2 == 0:
            pad_heads = LANES - n_heads
            parts += [jnp.pad(fox_wt[i // 2, 3 * d:], ((0, pad_heads), (0, 0))),
                      jnp.pad(row_vec(fox_b_f[i // 2]), ((0, 0), (0, pad_heads)))]
        return _Mixer(*parts)

    rel_table = rel_bias_table.astype(F32)
    bias_t = _bias_tiles(rel_table, MOBA_BLOCK)

    h = x.reshape(n, d).astype(F32)
    row = lax.broadcasted_iota(jnp.int32, (3 * d, 1), 0)
    w_qkv = (fox_wt[0, :3 * d] * jnp.where(row < d, Q_SCALE, 1.0)).astype(BF16)
    mixed, weights = _project(h, mixer(0, w_qkv), layer_casts(0), tm=tm)
    for i in range(depth):
        w_o, w_ffn_in, w_ffn_out, w_ple_gate, w_ple_up, *w_qkv_next = weights
        qt, k3, vt = mixed[0], mixed[1].reshape(b, s, d), mixed[2]
        if i % 2 == 0:
            aqt, ak = _cumsum(mixed[3].reshape(b, s, LANES))
            o_t = _fox_attention(qt, k3, vt, aqt, ak, n_heads)
        else:
            o_t = _moba_attention(qt, k3, vt, rel_table, bias_t, n_heads)
        ple = _Ple(row_vec(ple_norm_g[i]), w_ple_gate, _Slab(p3, i), w_ple_up)
        last = (ple, row_vec(final_norm_g)) if i + 1 == depth else None
        h = _oproj_ffn(h, o_t, w_o, row_vec(ffn_norm_g[i]), w_ffn_in, w_ffn_out,
                       tm=tm, tf=tf, last=last)
        if last is None:
            h, mixed, weights = _ple_next(h, ple, mixer(i + 1, *w_qkv_next),
                                          layer_casts(i + 1), tm=tm)
    return h.reshape(b, s, d).astype(x.dtype)
```

```python
import functools
import math
from typing import NamedTuple, Optional

import numpy as np
import jax
import jax.numpy as jnp
from jax import lax
from jax.experimental import pallas as pl
from jax.experimental.pallas import tpu as pltpu

F32 = jnp.float32
BF16 = jnp.bfloat16

RMS_EPS = 1e-6
HEAD_DIM = 64
MOBA_BLOCK = 256
MOBA_TOP_K = 3
REL_BUCKETS = 32
REL_MAX_DIST = 128

LANES = 128
F32_SUBLANES = 8
BF16_SUBLANES = 16
V7X_VMEM_BYTES = 64 * 1024 * 1024
V_ROWS = HEAD_DIM + BF16_SUBLANES
SPLIT_TERMS = 3
GATE_LANES = 8
SCORE_PITCH_PAD = LANES
LOG2E = math.log2(math.e)
Q_SCALE = HEAD_DIM ** -0.5 * LOG2E
MOBA_Q_TILE = 2 * MOBA_BLOCK
ATTN_Q_TILE = MOBA_Q_TILE
DENSE_ROW_TILE = ATTN_Q_TILE
CUMSUM_BLK = ATTN_Q_TILE
PROJ_COL_CHUNK = 512
FFN_COL_CHUNK = 256
VMEM_LIMIT_BYTES = V7X_VMEM_BYTES // 8 * 7
NEG_INF = float("-inf")


def _params(n_axes):
    return pltpu.CompilerParams(
        dimension_semantics=("arbitrary",) * n_axes,
        vmem_limit_bytes=VMEM_LIMIT_BYTES)


def _split3(x):
    x1 = x.astype(BF16)
    r1 = x - x1.astype(F32)
    x2 = r1.astype(BF16)
    x3 = (r1 - x2.astype(F32)).astype(BF16)
    return x1, x2, x3


def _rmsnorm(x, g):
    ms = jnp.mean(x * x, axis=-1, keepdims=True)
    return x * lax.rsqrt(ms + RMS_EPS) * g


def _log_sigmoid(x):
    return jnp.minimum(x, 0.0) - jnp.log1p(jnp.exp(-jnp.abs(x)))


class _Slab(NamedTuple):
    stack: jax.Array
    layer: int

    @property
    def shape(self):
        return self.stack.shape[1:]


def _resident(x):
    if isinstance(x, _Slab):
        layer, zeros = x.layer, (0,) * len(x.shape)
        return pl.BlockSpec((None,) + x.shape, lambda i: (layer,) + zeros,
                            pipeline_mode=pl.Buffered(1))
    zeros = (0,) * x.ndim
    return pl.BlockSpec(x.shape, lambda i: zeros, pipeline_mode=pl.Buffered(1))


def _array(x):
    return x.stack if isinstance(x, _Slab) else x


class _Cast(NamedTuple):
    w: _Slab
    rows: int
    scaled: int = 0
    transpose: bool = False

    def plan(self, steps):
        cols = self.w.shape[1]
        min_rows = LANES if self.transpose else BF16_SUBLANES
        assert self.rows % min_rows == 0 and cols % LANES == 0
        chunks = math.gcd(steps, self.rows // min_rows)
        span, layer, chunk = steps // chunks, self.w.layer, self.rows // chunks
        in_spec = pl.BlockSpec((None, chunk, cols), lambda i: (layer, i // span, 0))
        if self.transpose:
            return (in_spec, pl.BlockSpec((cols, chunk), lambda i: (0, i // span)),
                    jax.ShapeDtypeStruct((cols, self.rows), BF16), span)
        return (in_spec, pl.BlockSpec((chunk, cols), lambda i: (i // span, 0)),
                jax.ShapeDtypeStruct((self.rows, cols), BF16), span)


def _cast_plans(casts, steps):
    plans = [c.plan(steps) for c in casts]
    return ([p[0] for p in plans], [p[1] for p in plans], [p[2] for p in plans],
            tuple((c.transpose, c.scaled, p[3]) for c, p in zip(casts, plans)))


def _cast_rows(jobs, in_refs, out_refs):
    for (transpose, scaled, span), in_ref, out_ref in zip(jobs, in_refs, out_refs):
        w = in_ref[...].T if transpose else in_ref[...]
        if scaled:
            first_row = 0 if transpose else pl.program_id(0) // span * w.shape[0]
            row = first_row + lax.broadcasted_iota(jnp.int32, (w.shape[0], 1), 0)
            w = w * jnp.where(row < scaled, Q_SCALE, 1.0)
        out_ref[...] = w.astype(BF16)


class _Mixer(NamedTuple):
    g: jax.Array
    w_qkv: jax.Array
    w_f: Optional[jax.Array] = None
    b_f: Optional[jax.Array] = None

    def operands(self):
        return tuple(_array(a) for a in self if a is not None)

    def in_specs(self):
        return [_resident(a) for a in self if a is not None]

    def out_specs(self, tm):
        d = self.w_qkv.shape[1]
        specs = [pl.BlockSpec((None, d, tm), lambda i: (i, 0, 0)),
                 pl.BlockSpec((tm, d), lambda i: (i, 0)),
                 pl.BlockSpec((d, tm), lambda i: (0, i))]
        if self.w_f is not None:
            specs.append(pl.BlockSpec((tm, LANES), lambda i: (i, 0)))
        return specs

    def out_shapes(self, n, tm):
        d = self.w_qkv.shape[1]
        shapes = [jax.ShapeDtypeStruct((n // tm, d, tm), BF16),
                  jax.ShapeDtypeStruct((n, d), BF16), jax.ShapeDtypeStruct((d, n), BF16)]
        if self.w_f is not None:
            shapes.append(jax.ShapeDtypeStruct((n, LANES), F32))
        return shapes


def _chunks(total, want):
    step = want if total % want == 0 else total
    return [slice(c * step, (c + 1) * step) for c in range(total // step)]


def _mixer_inputs(y, mixer_refs, out_refs):
    g_ref, wt_ref = mixer_refs[:2]
    d = wt_ref.shape[1]
    u = _rmsnorm(y, g_ref[...]).astype(BF16)

    def contract_last(a, b):
        return lax.dot_general(a, b, (((1,), (1,)), ((), ())), preferred_element_type=F32)

    for first_row, out_ref in ((0, out_refs[0]), (2 * d, out_refs[2])):
        for rows in _chunks(d, PROJ_COL_CHUNK):
            w_rows = wt_ref[first_row + rows.start:first_row + rows.stop, :]
            out_ref[rows, :] = contract_last(w_rows, u).astype(BF16)
    for cols in _chunks(d, PROJ_COL_CHUNK):
        out_refs[1][:, cols] = contract_last(
            u, wt_ref[d + cols.start:d + cols.stop, :]).astype(BF16)
    if len(mixer_refs) > 2:
        wft_ref, bf_ref = mixer_refs[2:]
        f_logit = contract_last(u, wft_ref[...].astype(BF16)) + bf_ref[...]
        out_refs[3][...] = _log_sigmoid(f_logit)


def _split_refs(refs, n_mixer, n_cast):
    n_out = len(refs) - n_cast
    return (refs[:n_mixer], refs[n_mixer:n_mixer + n_cast],
            refs[n_mixer + n_cast:n_out], refs[n_out:])


def _proj_kernel(x_ref, *refs, n_mixer, jobs):
    mixer_refs, cast_in, out_refs, cast_out = _split_refs(refs, n_mixer, len(jobs))
    _mixer_inputs(x_ref[...], mixer_refs, out_refs)
    _cast_rows(jobs, cast_in, cast_out)


def _project(h2, mixer, casts, *, tm):
    n, d = h2.shape
    n_mixer = len(mixer.operands())
    cast_in, cast_out, cast_shapes, jobs = _cast_plans(casts, n // tm)
    outs = pl.pallas_call(
        functools.partial(_proj_kernel, n_mixer=n_mixer, jobs=jobs),
        grid=(n // tm,),
        in_specs=[pl.BlockSpec((tm, d), lambda i: (i, 0))] + mixer.in_specs() + cast_in,
        out_specs=mixer.out_specs(tm) + cast_out,
        out_shape=mixer.out_shapes(n, tm) + cast_shapes,
        compiler_params=_params(1), name="proj")(
            h2, *mixer.operands(), *(c.w.stack for c in casts))
    return outs[:len(outs) - len(casts)], outs[len(outs) - len(casts):]


def _cumsum_kernel(lf_ref, aqt_ref, ak_ref, carry_ref):
    @pl.when(pl.program_id(1) == 0)
    def _():
        carry_ref[...] = jnp.zeros_like(carry_ref)

    t = lf_ref.shape[1]
    row = lax.broadcasted_iota(jnp.int32, (t, t), 0)
    col = lax.broadcasted_iota(jnp.int32, (t, t), 1)
    tril = jnp.where(col <= row, 1.0, 0.0).astype(BF16)
    x1, x2, x3 = _split3(lf_ref[0])
    cs = (jnp.dot(tril, x1, preferred_element_type=F32)
          + jnp.dot(tril, x2, preferred_element_type=F32)
          + jnp.dot(tril, x3, preferred_element_type=F32))
    cs = cs + carry_ref[0:1, :]
    carry_ref[...] = jnp.broadcast_to(cs[t - 1:t, :], carry_ref.shape)

    src = lax.broadcasted_iota(jnp.int32, (LANES, LANES), 0)
    dst = lax.broadcasted_iota(jnp.int32, (LANES, LANES), 1)
    lane = lax.broadcasted_iota(jnp.int32, (1, LANES), 1) & (GATE_LANES - 1)
    aq = jnp.where((lane >= SPLIT_TERMS) & (lane < 2 * SPLIT_TERMS), 1.0, 0.0)
    ak = jnp.where(lane < SPLIT_TERMS, 1.0, 0.0)
    for i, part in enumerate(_split3(cs * LOG2E)):
        to_q = jnp.where(dst == GATE_LANES * src + i, 1.0, 0.0).astype(BF16)
        to_k = jnp.where(dst == GATE_LANES * src + SPLIT_TERMS + i, 1.0, 0.0).astype(BF16)
        aq = aq + jnp.dot(part, to_q, preferred_element_type=F32)
        ak = ak - jnp.dot(part, to_k, preferred_element_type=F32)
    eye = jnp.where(src == dst, 1.0, 0.0).astype(BF16)
    aqt_ref[0] = lax.dot_general(eye, aq.astype(BF16), (((1,), (1,)), ((), ())),
                                 preferred_element_type=F32).astype(BF16)
    ak_ref[0] = ak.astype(BF16)


def _cumsum(lf3):
    b, s, _ = lf3.shape
    t = min(CUMSUM_BLK, s)
    nt = s // t
    spec = pl.BlockSpec((1, t, LANES), lambda i, j: (i, j, 0))
    return pl.pallas_call(
        _cumsum_kernel, grid=(b, nt),
        in_specs=[spec],
        out_specs=[pl.BlockSpec((1, LANES, t), lambda i, j: (i * nt + j, 0, 0)), spec],
        out_shape=[jax.ShapeDtypeStruct((b * nt, LANES, t), BF16),
                   jax.ShapeDtypeStruct(lf3.shape, BF16)],
        scratch_shapes=[pltpu.VMEM((F32_SUBLANES, LANES), F32)],
        compiler_params=_params(2), name="gate_cumsum")(lf3)


def _fill_value_rows(vt_ref, v_ref, blk):
    row = lax.broadcasted_iota(jnp.int32, (V_ROWS - HEAD_DIM, blk), 0)
    tail = jnp.where(row == 0, 1.0, 0.0).astype(BF16)
    for hh in range(2):
        for jb in range(v_ref.shape[1] // blk):
            head_rows = v_ref[hh * HEAD_DIM:(hh + 1) * HEAD_DIM, jb * blk:(jb + 1) * blk]
            vt_ref[hh, jb] = jnp.concatenate([head_rows, tail], axis=0)


def _flash_init(m_ref, acc_ref):
    m_ref[...] = jnp.full(m_ref.shape, NEG_INF, F32)
    acc_ref[...] = jnp.zeros(acc_ref.shape, F32)


def _pipelined_blocks(n, scores_into, update, final_scores, final_update, next_tile_scores):
    def step(j_next, j, slot):
        for hh in range(2):
            scores_into(j_next, 1 - slot, hh)
            update(j, slot, hh)

    def last_update_and_final(j, slot):
        for hh in range(2):
            final_scores(hh)
            update(j, slot, hh)
        next_tile_scores()
        final_update()

    def pair(jj, carry):
        j = 2 * jj
        step(j + 1, j, 0)
        step(j + 2, j + 1, 1)
        return carry
    lax.fori_loop(0, jnp.maximum(n - 1, 0) // 2, pair, 0)

    @pl.when(n % 2 == 1)
    def _():
        last_update_and_final(n - 1, 0)

    @pl.when((n % 2 == 0) & (n > 0))
    def _():
        step(n - 1, n - 2, 0)
        last_update_and_final(n - 1, 1)

    @pl.when(n == 0)
    def _():
        for hh in range(2):
            final_scores(hh)
        next_tile_scores()
        final_update()


def _store_scores(s_t, s_ref, mx_ref, idx):
    s_ref[idx + (slice(None), slice(0, s_t.shape[1]))] = s_t
    mx_ref[idx] = jnp.max(s_t, axis=0, keepdims=True)


def _flash_update(s_t, mx, v_rows, m_ref, acc_ref, hh, keep=None, const=None,
                  cols=slice(None)):
    m_old = m_ref[hh, :, cols]
    if const is not None:
        mx = mx + const
    if keep is not None:
        mx = jnp.where(keep, mx, NEG_INF)
    m_new = jnp.maximum(m_old, mx)
    m_safe = jnp.where(m_new == NEG_INF, 0.0, m_new)
    shift = m_safe if const is None else m_safe - const
    if keep is not None:
        shift = jnp.where(keep, shift, float("inf"))
    p = jnp.exp2(s_t - shift).astype(BF16)
    alpha = jnp.exp2(m_old - m_safe)
    m_ref[hh, :, cols] = m_new
    if not isinstance(v_rows, (list, tuple)):
        v_rows = [v_rows]
    keys = p.shape[0] // len(v_rows)
    acc = alpha * acc_ref[hh, :, cols]
    for i, v_i in enumerate(v_rows):
        acc = acc + jnp.dot(v_i, p[i * keys:(i + 1) * keys], preferred_element_type=F32)
    acc_ref[hh, :, cols] = acc


def _head_outputs(acc_ref):
    outs = []
    for hh in range(2):
        acc = acc_ref[hh]
        outs.append(acc[:HEAD_DIM] / acc[HEAD_DIM:HEAD_DIM + 1])
    return jnp.concatenate(outs, axis=0).astype(BF16)


def _for_each_query_tile(nq, tile):
    def body(qi, carry):
        tile(qi)
        return carry
    lax.fori_loop(0, nq, body, 0)


def _head_row_mask(hh):
    row = lax.broadcasted_iota(jnp.int32, (LANES, 1), 0)
    return (row < HEAD_DIM) if hh == 0 else (row >= HEAD_DIM)


def _fox_kernel(qt_ref, k_ref, v_ref, *refs):
    vt_ref = refs[3]
    nq, _, blk = qt_ref.shape
    _fill_value_rows(vt_ref, v_ref, blk)
    hp = pl.program_id(1)
    _for_each_query_tile(nq, lambda qi: _fox_tile(qi, hp, qt_ref, k_ref, *refs))


def _fox_tile(qi, hp, qt_ref, k_ref, aqt_ref, ak_ref, o_ref,
              vt_ref, s_ref, mx_ref, diag_ref, m_ref, acc_ref):
    nq, _, blk = qt_ref.shape
    gate_row = lax.broadcasted_iota(jnp.int32, (LANES, 1), 0)

    def query_operands(tile):
        qt, aqt = qt_ref[tile], aqt_ref[tile]
        out = []
        for hh in range(2):
            first = GATE_LANES * (2 * hp + hh)
            own_gate = (gate_row >= first) & (gate_row < first + GATE_LANES)
            q_rows = jnp.where(_head_row_mask(hh), qt, jnp.zeros_like(qt))
            g_rows = jnp.where(own_gate, aqt, jnp.zeros_like(aqt))
            out.append(jnp.concatenate([q_rows, g_rows], axis=0))
        return out

    def block_scores(w, j, hh):
        rows = pl.ds(pl.multiple_of(j * blk, blk), blk)
        keys = jnp.concatenate([k_ref[0, rows, :], ak_ref[0, rows, :]], axis=1)
        return jnp.dot(keys, w[hh], preferred_element_type=F32)

    w_q = query_operands(qi)

    def scores_into(j, slot, hh):
        _store_scores(block_scores(w_q, j, hh), s_ref, mx_ref, (slot, hh))

    def next_tile_scores():
        w_next = query_operands(jnp.minimum(qi + 1, nq - 1))
        for hh in range(2):
            _store_scores(block_scores(w_next, 0, hh), s_ref, mx_ref, (0, hh))

    def update(j, slot, hh):
        _flash_update(s_ref[slot, hh, :, :blk], mx_ref[slot, hh], vt_ref[hh, j],
                      m_ref, acc_ref, hh)

    def diagonal_scores(hh):
        diag_ref[hh] = block_scores(w_q, qi, hh)

    def diagonal_update():
        half = blk // 2
        lo, hi = slice(0, half), slice(half, blk)
        causal_lo = (lax.broadcasted_iota(jnp.int32, (half, half), 0)
                     <= lax.broadcasted_iota(jnp.int32, (half, half), 1))
        causal_hi = (lax.broadcasted_iota(jnp.int32, (blk, half), 0)
                     <= lax.broadcasted_iota(jnp.int32, (blk, half), 1) + half)
        for hh in range(2):
            v_rows = vt_ref[hh, qi]
            s_lo = jnp.where(causal_lo, diag_ref[hh, lo, lo], NEG_INF)
            _flash_update(s_lo, jnp.max(s_lo, axis=0, keepdims=True),
                          v_rows[:, lo], m_ref, acc_ref, hh, cols=lo)
            s_hi = jnp.where(causal_hi, diag_ref[hh, :, hi], NEG_INF)
            _flash_update(s_hi, jnp.max(s_hi, axis=0, keepdims=True),
                          v_rows, m_ref, acc_ref, hh, cols=hi)

    _flash_init(m_ref, acc_ref)
    _pipelined_blocks(qi, scores_into, update, diagonal_scores, diagonal_update,
                      next_tile_scores)
    o_ref[qi] = _head_outputs(acc_ref)


def _fox_attention(qt3, k3, vt, aqt3, ak3, n_heads):
    b, s, d = k3.shape
    assert n_heads * GATE_LANES <= LANES
    blk = qt3.shape[2]
    assert aqt3.shape[2] == blk and s % blk == 0
    npair = n_heads // 2
    nq = s // blk
    head_pair_tiles = pl.BlockSpec((nq, LANES, blk), lambda bi, hp: (bi, hp, 0))
    return pl.pallas_call(
        _fox_kernel, grid=(b, npair),
        in_specs=[
            head_pair_tiles,
            pl.BlockSpec((1, s, LANES), lambda bi, hp: (bi, 0, hp)),
            pl.BlockSpec((LANES, s), lambda bi, hp: (hp, bi)),
            pl.BlockSpec((nq, LANES, blk), lambda bi, hp: (bi, 0, 0)),
            pl.BlockSpec((1, s, LANES), lambda bi, hp: (bi, 0, 0)),
        ],
        out_specs=head_pair_tiles,
        out_shape=jax.ShapeDtypeStruct(qt3.shape, BF16),
        scratch_shapes=[
            pltpu.VMEM((2, s // blk, V_ROWS, blk), BF16),
            pltpu.VMEM((2, 2, blk, blk + SCORE_PITCH_PAD), F32),
            pltpu.VMEM((2, 2, 1, blk), F32),
            pltpu.VMEM((2, blk, blk), F32),
            pltpu.VMEM((2, 1, blk), F32),
            pltpu.VMEM((2, V_ROWS, blk), F32),
        ],
        compiler_params=_params(2), name="fox_attention")(qt3, k3, vt, aqt3, ak3)


def _t5_bucket_np(n):
    max_exact = REL_BUCKETS // 2
    nf = np.maximum(n, 1).astype(np.float64)
    large = max_exact + (np.log(nf / max_exact) / math.log(REL_MAX_DIST / max_exact)
                         * (REL_BUCKETS - max_exact)).astype(np.int32)
    return np.where(n < max_exact, n, np.minimum(large, REL_BUCKETS - 1)).astype(np.int32)


def _bucket_tiles(blk):
    key = np.arange(blk)[:, None]
    qry = np.arange(blk)[None, :]
    own = np.where(key <= qry, _t5_bucket_np(np.maximum(qry - key, 0)), -1)
    prev = _t5_bucket_np(blk + qry - key)
    return np.stack([own, prev]).astype(np.int32)


def _bias_kernel(tab_ref, bucket_ref, o_ref):
    h = pl.program_id(0)
    bucket = bucket_ref[...]
    acc = jnp.where(bucket < 0, NEG_INF, 0.0).astype(F32)
    for bkt in range(REL_BUCKETS):
        acc = jnp.where(bucket == bkt, tab_ref[bkt, h] * LOG2E, acc)
    o_ref[0] = acc


def _bias_tiles(rel_table, blk):
    n_heads = rel_table.shape[1]
    buckets = jnp.asarray(_bucket_tiles(blk))
    return pl.pallas_call(
        _bias_kernel, grid=(n_heads,),
        in_specs=[pl.BlockSpec(memory_space=pltpu.SMEM),
                  pl.BlockSpec((2, blk, blk), lambda h: (0, 0, 0))],
        out_specs=pl.BlockSpec((1, 2, blk, blk), lambda h: (h, 0, 0, 0)),
        out_shape=jax.ShapeDtypeStruct((n_heads, 2, blk, blk), F32),
        compiler_params=_params(1), name="t5_bias_tiles")(rel_table, buckets)


def _select_blocks(qt_ref, km_ref, sel_ref):
    nq, _, tq = qt_ref.shape
    blk = MOBA_BLOCK
    nblk = km_ref.shape[0]
    assert blk & (blk - 1) == 0
    qt_all = jnp.concatenate([qt_ref[t] for t in range(nq)], axis=1)
    lane = lax.broadcasted_iota(jnp.int32, (1, LANES), 1)
    km_parts = _split3(km_ref[...])
    terms = jnp.dot(
        jnp.concatenate([jnp.where((lane < HEAD_DIM) == (hh == 0), part, jnp.zeros_like(part))
                         for hh in range(2) for part in km_parts], axis=0),
        qt_all, preferred_element_type=F32)
    blk_id = lax.broadcasted_iota(jnp.int32, (nblk, nq * tq), 0)
    own = lax.shift_right_logical(lax.broadcasted_iota(jnp.int32, (nblk, nq * tq), 1),
                                  blk.bit_length() - 1)
    past = blk_id < own
    for hh in range(2):
        gate = None
        for i in range(SPLIT_TERMS):
            first = (SPLIT_TERMS * hh + i) * nblk
            term = terms[first:first + nblk]
            gate = term if gate is None else gate + term
        work = jnp.where(past, gate, NEG_INF)
        picked = jnp.zeros(gate.shape, F32)
        for _ in range(MOBA_TOP_K):
            best = jnp.max(work, axis=0, keepdims=True)
            first = jnp.min(jnp.where(work == best, blk_id, nblk), axis=0, keepdims=True)
            hit = blk_id == first
            picked = jnp.where(hit, 1.0, picked)
            work = jnp.where(hit, NEG_INF, work)
        keep_all = ((picked > 0.5) & past) | (blk_id == own) | (blk_id == (own | 1))
        keep_f = jnp.where(keep_all, 1.0, 0.0)
        for t in range(nq):
            sel_ref[hh, t] = keep_f[:, t * tq:(t + 1) * tq]


def _moba_kernel(tab_ref, qt_ref, k_ref, v_ref, bias_ref, o_ref,
                 vt_ref, km_ref, sel_ref, *scratch):
    blk = MOBA_BLOCK
    member = (lax.shift_right_logical(lax.broadcasted_iota(jnp.int32, (km_ref.shape[0],
                                                                      k_ref.shape[1]), 1),
                                      blk.bit_length() - 1)
              == lax.broadcasted_iota(jnp.int32, (km_ref.shape[0], k_ref.shape[1]), 0))
    km_ref[...] = jnp.dot(jnp.where(member, 1.0, 0.0).astype(BF16), k_ref[0],
                          preferred_element_type=F32) * (1.0 / blk)
    _fill_value_rows(vt_ref, v_ref, blk)
    _select_blocks(qt_ref, km_ref, sel_ref)
    hp = pl.program_id(1)
    _for_each_query_tile(
        qt_ref.shape[0],
        lambda qi: _moba_tile(qi, hp, tab_ref, qt_ref, k_ref, bias_ref, o_ref,
                              vt_ref, sel_ref, *scratch))


def _moba_tile(qi, hp, tab_ref, qt_ref, k_ref, bias_ref, o_ref,
               vt_ref, sel_ref, s_ref, mx_ref, near_ref, own1_ref, m_ref, acc_ref):
    blk = MOBA_BLOCK
    nq, _, tq = qt_ref.shape
    assert tq == 2 * blk and blk >= REL_MAX_DIST
    first_own = 2 * qi

    def head_queries(tile):
        qt = qt_ref[tile]
        return [jnp.where(_head_row_mask(hh), qt, jnp.zeros_like(qt)) for hh in range(2)]

    q_m_t = head_queries(qi)

    def scores(hh, j, queries=q_m_t):
        rows = pl.ds(pl.multiple_of(j * blk, blk), blk)
        return jnp.dot(k_ref[0, rows, :], queries[hh], preferred_element_type=F32)

    def next_tile_scores():
        q_next = head_queries(jnp.minimum(qi + 1, nq - 1))
        for hh in range(2):
            _store_scores(scores(hh, 0, q_next), s_ref, mx_ref, (0, hh))

    def keep(hh, j):
        return sel_ref[hh, qi, pl.ds(j, 1), :] > 0.5

    _flash_init(m_ref, acc_ref)
    far_bias = [tab_ref[REL_BUCKETS - 1, 2 * hp + hh] * LOG2E for hh in range(2)]

    j_prev = jnp.maximum(first_own - 1, 0)
    lo, hi = slice(0, blk), slice(blk, 2 * blk)
    near_mx = {}

    def near_scores(hh):
        own_t, prev_t = bias_ref[hh, 0], bias_ref[hh, 1]
        mask_prev = jnp.where(keep(hh, j_prev) & (qi >= 1), 0.0, NEG_INF)
        mask_own = jnp.where(keep(hh, first_own), 0.0, NEG_INF)
        far_t = jnp.full((blk, blk), far_bias[hh], F32)
        parts = [scores(hh, j_prev) + mask_prev + jnp.concatenate([prev_t, far_t], axis=1),
                 scores(hh, first_own) + mask_own + jnp.concatenate([own_t, prev_t], axis=1)]
        mx = None
        for i, part in enumerate(parts):
            near_ref[hh, i * blk:(i + 1) * blk] = part
            part_mx = jnp.max(part, axis=0, keepdims=True)
            mx = part_mx if mx is None else jnp.maximum(mx, part_mx)
        rows = pl.ds(pl.multiple_of((first_own + 1) * blk, blk), blk)
        last = jnp.dot(k_ref[0, rows, :], q_m_t[hh][:, hi], preferred_element_type=F32) + own_t
        own1_ref[hh] = last
        near_mx[hh] = (mx[:, lo], jnp.maximum(mx[:, hi], jnp.max(last, axis=0, keepdims=True)))

    def near_update():
        for hh in range(2):
            v_near = [vt_ref[hh, j_prev], vt_ref[hh, first_own], vt_ref[hh, first_own + 1]]
            _flash_update(near_ref[hh, :, lo], near_mx[hh][0], v_near[:2],
                          m_ref, acc_ref, hh, cols=lo)
            _flash_update(jnp.concatenate([near_ref[hh, :, hi], own1_ref[hh]], axis=0),
                          near_mx[hh][1], v_near, m_ref, acc_ref, hh, cols=hi)

    def scores_into(j, slot, hh):
        _store_scores(scores(hh, j), s_ref, mx_ref, (slot, hh))

    def update(j, slot, hh):
        _flash_update(s_ref[slot, hh, :, :tq], mx_ref[slot, hh], vt_ref[hh, j],
                      m_ref, acc_ref, hh, keep=keep(hh, j), const=far_bias[hh])

    _pipelined_blocks(jnp.maximum(first_own - 1, 0), scores_into, update,
                      near_scores, near_update, next_tile_scores)
    o_ref[qi] = _head_outputs(acc_ref)


def _moba_attention(qt3, k3, vt, rel_table, bias_t, n_heads):
    b, s, d = k3.shape
    blk = MOBA_BLOCK
    tq = qt3.shape[2]
    assert tq == MOBA_Q_TILE and s % tq == 0
    npair = n_heads // 2
    nblk = s // blk
    nq = s // tq
    head_pair_tiles = pl.BlockSpec((nq, LANES, tq), lambda bi, hp: (bi, hp, 0))
    return pl.pallas_call(
        _moba_kernel, grid=(b, npair),
        in_specs=[
            pl.BlockSpec(memory_space=pltpu.SMEM),
            head_pair_tiles,
            pl.BlockSpec((1, s, LANES), lambda bi, hp: (bi, 0, hp)),
            pl.BlockSpec((LANES, s), lambda bi, hp: (hp, bi)),
            pl.BlockSpec((2, 2, blk, blk), lambda bi, hp: (hp, 0, 0, 0)),
        ],
        out_specs=head_pair_tiles,
        out_shape=jax.ShapeDtypeStruct(qt3.shape, BF16),
        scratch_shapes=[
            pltpu.VMEM((2, nblk, V_ROWS, blk), BF16),
            pltpu.VMEM((nblk, LANES), F32),
            pltpu.VMEM((2, nq, nblk, tq), F32),
            pltpu.VMEM((2, 2, blk, tq + SCORE_PITCH_PAD), F32),
            pltpu.VMEM((2, 2, 1, tq), F32),
            pltpu.VMEM((2, 2 * blk, tq), F32),
            pltpu.VMEM((2, blk, blk), F32),
            pltpu.VMEM((2, 1, tq), F32),
            pltpu.VMEM((2, V_ROWS, tq), F32),
        ],
        compiler_params=_params(2), name="moba_attention")(
            rel_table, qt3, k3, vt, bias_t)


class _Ple(NamedTuple):
    g: jax.Array
    w_gate: jax.Array
    p: _Slab
    w_up: jax.Array

    def in_specs(self, tm):
        layer = self.p.layer
        return [_resident(self.g), _resident(self.w_gate),
                pl.BlockSpec((None, tm, self.p.shape[1]), lambda i: (layer, i, 0)),
                _resident(self.w_up)]

    def operands(self):
        return tuple(_array(a) for a in self)


def _ple_update(x, g_ref, wg_ref, p_ref, wu_ref):
    u = _rmsnorm(x, g_ref[...]).astype(BF16)
    gate = jax.nn.sigmoid(jnp.dot(u, wg_ref[...], preferred_element_type=F32))
    up = jnp.dot(p_ref[...].astype(BF16), wu_ref[...], preferred_element_type=F32)
    return x + gate * up


def _oproj_ffn_kernel(h_ref, ot_ref, wo_ref, g_ref, win_ref, wout_ref, *rest, tf):
    d_ff = wout_ref.shape[0]
    h1 = h_ref[...] + lax.dot_general(ot_ref[...], wo_ref[...], (((0,), (0,)), ((), ())),
                                      preferred_element_type=F32)
    u = _rmsnorm(h1, g_ref[...]).astype(BF16)
    acc = h1
    for c in range(d_ff // tf):
        gate = jnp.dot(u, win_ref[:, c * tf:(c + 1) * tf], preferred_element_type=F32)
        up = jnp.dot(u, win_ref[:, d_ff + c * tf:d_ff + (c + 1) * tf],
                     preferred_element_type=F32)
        act = (gate * jax.nn.sigmoid(gate) * up).astype(BF16)
        acc = acc + jnp.dot(act, wout_ref[c * tf:(c + 1) * tf, :],
                            preferred_element_type=F32)
    *ple_refs, out_ref = rest
    if ple_refs:
        *ple_refs, final_g_ref = ple_refs
        acc = _rmsnorm(_ple_update(acc, *ple_refs), final_g_ref[...])
    out_ref[...] = acc


def _oproj_ffn(h2, o_t, w_o, g, w_in, w_out, *, tm, tf, last=None):
    n, d = h2.shape
    d_ff = w_out.shape[0]
    assert d_ff % tf == 0 and o_t.shape == (n // tm, d, tm)
    row = pl.BlockSpec((tm, d), lambda i: (i, 0))
    weights = (w_o, g, w_in, w_out)
    in_specs = ([row, pl.BlockSpec((None, d, tm), lambda i: (i, 0, 0))]
                + [_resident(w) for w in weights])
    operands = (h2, o_t, *map(_array, weights))
    if last is not None:
        ple, final_g = last
        in_specs += ple.in_specs(tm) + [_resident(final_g)]
        operands += ple.operands() + (final_g,)
    return pl.pallas_call(
        functools.partial(_oproj_ffn_kernel, tf=tf), grid=(n // tm,),
        in_specs=in_specs, out_specs=row,
        out_shape=jax.ShapeDtypeStruct((n, d), F32),
        compiler_params=_params(1), name="oproj_ffn")(*operands)


def _ple_next_kernel(x_ref, g_ref, wg_ref, p_ref, wu_ref, *refs, n_mixer, jobs):
    mixer_refs, cast_in, out_refs, cast_out = _split_refs(refs, n_mixer, len(jobs))
    y = _ple_update(x_ref[...], g_ref, wg_ref, p_ref, wu_ref)
    out_refs[0][...] = y
    _mixer_inputs(y, mixer_refs, out_refs[1:])
    _cast_rows(jobs, cast_in, cast_out)


def _ple_next(h2, ple, mixer, casts, *, tm):
    n, d = h2.shape
    row = pl.BlockSpec((tm, d), lambda i: (i, 0))
    n_mixer = len(mixer.operands())
    cast_in, cast_out, cast_shapes, jobs = _cast_plans(casts, n // tm)
    h, *outs = pl.pallas_call(
        functools.partial(_ple_next_kernel, n_mixer=n_mixer, jobs=jobs),
        grid=(n // tm,),
        in_specs=[row] + ple.in_specs(tm) + mixer.in_specs() + cast_in,
        out_specs=[row] + mixer.out_specs(tm) + cast_out,
        out_shape=[jax.ShapeDtypeStruct((n, d), F32)] + mixer.out_shapes(n, tm) + cast_shapes,
        compiler_params=_params(1), name="ple_proj")(
            h2, *ple.operands(), *mixer.operands(), *(c.w.stack for c in casts))
    return h, outs[:len(outs) - len(casts)], outs[len(outs) - len(casts):]


def _row_tile(n, want):
    t = min(want, n)
    assert n % t == 0
    return t


def _col_tile(n, want):
    t = min(want, n)
    while n % t:
        t -= LANES
    return t


def kernel(x, p, attn_norm_g, fox_w_in, fox_b_f, fox_w_o, moba_w_in, moba_w_o, rel_bias_table,
           ffn_norm_g, ffn_w_in, ffn_w_out, ple_norm_g, ple_w_gate, ple_w_up, final_norm_g):
    b, s, d = x.shape
    depth = p.shape[0]
    n_heads = rel_bias_table.shape[1]
    assert d == n_heads * HEAD_DIM and n_heads % 2 == 0 and n_heads <= LANES
    n = b * s
    tm = _row_tile(n, DENSE_ROW_TILE)
    tf = _col_tile(ffn_w_out.shape[1], FFN_COL_CHUNK)

    def row_vec(v):
        return v.reshape(1, -1).astype(F32)

    w_o_mix = (fox_w_o, moba_w_o)
    p3 = p.reshape(depth, n, -1)
    fox_wt = jnp.swapaxes(fox_w_in, 1, 2)

    def qkv_cast(i):
        if i % 2 == 0:
            return _Cast(_Slab(fox_wt, i // 2), 3 * d, scaled=d)
        return _Cast(_Slab(moba_w_in, i // 2), d, scaled=d, transpose=True)

    def layer_casts(i):
        jobs = [_Cast(_Slab(w, layer), w.shape[1]) for w, layer in (
            (w_o_mix[i % 2], i // 2), (ffn_w_in, i), (ffn_w_out, i), (ple_w_gate, i),
            (ple_w_up, i))]
        return jobs + ([qkv_cast(i + 1)] if i + 1 < depth else [])

    def mixer(i, w_qkv):
        parts = [row_vec(attn_norm_g[i]), w_qkv]
        if i % 2 == 0:
            pad_heads = LANES - n_heads
            parts += [jnp.pad(fox_wt[i // 2, 3 * d:], ((0, pad_heads), (0, 0))),
                      jnp.pad(row_vec(fox_b_f[i // 2]), ((0, 0), (0, pad_heads)))]
        return _Mixer(*parts)

    rel_table = rel_bias_table.astype(F32)
    bias_t = _bias_tiles(rel_table, MOBA_BLOCK)

    h = x.reshape(n, d).astype(F32)
    row = lax.broadcasted_iota(jnp.int32, (3 * d, 1), 0)
    w_qkv = (fox_wt[0, :3 * d] * jnp.where(row < d, Q_SCALE, 1.0)).astype(BF16)
    mixed, weights = _project(h, mixer(0, w_qkv), layer_casts(0), tm=tm)
    for i in range(depth):
        w_o, w_ffn_in, w_ffn_out, w_ple_gate, w_ple_up, *w_qkv_next = weights
        qt, k3, vt = mixed[0], mixed[1].reshape(b, s, d), mixed[2]
        if i % 2 == 0:
            aqt, ak = _cumsum(mixed[3].reshape(b, s, LANES))
            o_t = _fox_attention(qt, k3, vt, aqt, ak, n_heads)
        else:
            o_t = _moba_attention(qt, k3, vt, rel_table, bias_t, n_heads)
        ple = _Ple(row_vec(ple_norm_g[i]), w_ple_gate, _Slab(p3, i), w_ple_up)
        last = (ple, row_vec(final_norm_g)) if i + 1 == depth else None
        h = _oproj_ffn(h, o_t, w_o, row_vec(ffn_norm_g[i]), w_ffn_in, w_ffn_out,
                       tm=tm, tf=tf, last=last)
        if last is None:
            h, mixed, weights = _ple_next(h, ple, mixer(i + 1, *w_qkv_next),
                                          layer_casts(i + 1), tm=tm)
    return h.reshape(b, s, d).astype(x.dtype)
```

```python
import functools
import math
from typing import NamedTuple, Optional

import numpy as np
import jax
import jax.numpy as jnp
from jax import lax
from jax.experimental import pallas as pl
from jax.experimental.pallas import tpu as pltpu

F32 = jnp.float32
BF16 = jnp.bfloat16

RMS_EPS = 1e-6
HEAD_DIM = 64
MOBA_BLOCK = 256
MOBA_TOP_K = 3
REL_BUCKETS = 32
REL_MAX_DIST = 128

LANES = 128
F32_SUBLANES = 8
BF16_SUBLANES = 16
V7X_VMEM_BYTES = 64 * 1024 * 1024
V_ROWS = HEAD_DIM + BF16_SUBLANES
SPLIT_TERMS = 3
GATE_LANES = 8
SCORE_PITCH_PAD = LANES
LOG2E = math.log2(math.e)
Q_SCALE = HEAD_DIM ** -0.5 * LOG2E
MOBA_Q_TILE = 2 * MOBA_BLOCK
ATTN_Q_TILE = MOBA_Q_TILE
DENSE_ROW_TILE = ATTN_Q_TILE
CUMSUM_BLK = ATTN_Q_TILE
PROJ_COL_CHUNK = 512
FFN_COL_CHUNK = 256
VMEM_LIMIT_BYTES = V7X_VMEM_BYTES // 8 * 7
NEG_INF = float("-inf")


def _params(n_axes):
    return pltpu.CompilerParams(
        dimension_semantics=("arbitrary",) * n_axes,
        vmem_limit_bytes=VMEM_LIMIT_BYTES)


def _split3(x):
    x1 = x.astype(BF16)
    r1 = x - x1.astype(F32)
    x2 = r1.astype(BF16)
    x3 = (r1 - x2.astype(F32)).astype(BF16)
    return x1, x2, x3


def _rmsnorm(x, g):
    ms = jnp.mean(x * x, axis=-1, keepdims=True)
    return x * lax.rsqrt(ms + RMS_EPS) * g


def _log_sigmoid(x):
    return jnp.minimum(x, 0.0) - jnp.log1p(jnp.exp(-jnp.abs(x)))


class _Slab(NamedTuple):
    stack: jax.Array
    layer: int

    @property
    def shape(self):
        return self.stack.shape[1:]


def _resident(x):
    if isinstance(x, _Slab):
        layer, zeros = x.layer, (0,) * len(x.shape)
        return pl.BlockSpec((None,) + x.shape, lambda i: (layer,) + zeros,
                            pipeline_mode=pl.Buffered(1))
    zeros = (0,) * x.ndim
    return pl.BlockSpec(x.shape, lambda i: zeros, pipeline_mode=pl.Buffered(1))


def _array(x):
    return x.stack if isinstance(x, _Slab) else x


class _Cast(NamedTuple):
    w: _Slab
    rows: int
    scaled: int = 0
    transpose: bool = False

    def plan(self, steps):
        cols = self.w.shape[1]
        min_rows = LANES if self.transpose else BF16_SUBLANES
        assert self.rows % min_rows == 0 and cols % LANES == 0
        chunks = math.gcd(steps, self.rows // min_rows)
        span, layer, chunk = steps // chunks, self.w.layer, self.rows // chunks
        in_spec = pl.BlockSpec((None, chunk, cols), lambda i: (layer, i // span, 0))
        if self.transpose:
            return (in_spec, pl.BlockSpec((cols, chunk), lambda i: (0, i // span)),
                    jax.ShapeDtypeStruct((cols, self.rows), BF16), span)
        return (in_spec, pl.BlockSpec((chunk, cols), lambda i: (i // span, 0)),
                jax.ShapeDtypeStruct((self.rows, cols), BF16), span)


def _cast_plans(casts, steps):
    plans = [c.plan(steps) for c in casts]
    return ([p[0] for p in plans], [p[1] for p in plans], [p[2] for p in plans],
            tuple((c.transpose, c.scaled, p[3]) for c, p in zip(casts, plans)))


def _cast_rows(jobs, in_refs, out_refs):
    for (transpose, scaled, span), in_ref, out_ref in zip(jobs, in_refs, out_refs):
        w = in_ref[...].T if transpose else in_ref[...]
        if scaled:
            first_row = 0 if transpose else pl.program_id(0) // span * w.shape[0]
            row = first_row + lax.broadcasted_iota(jnp.int32, (w.shape[0], 1), 0)
            w = w * jnp.where(row < scaled, Q_SCALE, 1.0)
        out_ref[...] = w.astype(BF16)


class _Mixer(NamedTuple):
    g: jax.Array
    w_qkv: jax.Array
    w_f: Optional[jax.Array] = None
    b_f: Optional[jax.Array] = None

    def operands(self):
        return tuple(_array(a) for a in self if a is not None)

    def in_specs(self):
        return [_resident(a) for a in self if a is not None]

    def out_specs(self, tm):
        d = self.w_qkv.shape[1]
        specs = [pl.BlockSpec((None, d, tm), lambda i: (i, 0, 0)),
                 pl.BlockSpec((tm, d), lambda i: (i, 0)),
                 pl.BlockSpec((d, tm), lambda i: (0, i))]
        if self.w_f is not None:
            specs.append(pl.BlockSpec((tm, LANES), lambda i: (i, 0)))
        return specs

    def out_shapes(self, n, tm):
        d = self.w_qkv.shape[1]
        shapes = [jax.ShapeDtypeStruct((n // tm, d, tm), BF16),
                  jax.ShapeDtypeStruct((n, d), BF16), jax.ShapeDtypeStruct((d, n), BF16)]
        if self.w_f is not None:
            shapes.append(jax.ShapeDtypeStruct((n, LANES), F32))
        return shapes


def _chunks(total, want):
    step = want if total % want == 0 else total
    return [slice(c * step, (c + 1) * step) for c in range(total // step)]


def _mixer_inputs(y, mixer_refs, out_refs):
    g_ref, wt_ref = mixer_refs[:2]
    d = wt_ref.shape[1]
    u = _rmsnorm(y, g_ref[...]).astype(BF16)

    def contract_last(a, b):
        return lax.dot_general(a, b, (((1,), (1,)), ((), ())), preferred_element_type=F32)

    if len(mixer_refs) > 2:
        wft_ref, bf_ref = mixer_refs[2:]
        f_logit = contract_last(u, wft_ref[...].astype(BF16)) + bf_ref[...]
        out_refs[3][...] = _log_sigmoid(f_logit)
    for first_row, out_ref in ((0, out_refs[0]), (2 * d, out_refs[2])):
        for rows in _chunks(d, PROJ_COL_CHUNK):
            w_rows = wt_ref[first_row + rows.start:first_row + rows.stop, :]
            out_ref[rows, :] = contract_last(w_rows, u).astype(BF16)
    for cols in _chunks(d, PROJ_COL_CHUNK):
        out_refs[1][:, cols] = contract_last(
            u, wt_ref[d + cols.start:d + cols.stop, :]).astype(BF16)


def _split_refs(refs, n_mixer, n_cast):
    n_out = len(refs) - n_cast
    return (refs[:n_mixer], refs[n_mixer:n_mixer + n_cast],
            refs[n_mixer + n_cast:n_out], refs[n_out:])


def _proj_kernel(x_ref, *refs, n_mixer, jobs):
    mixer_refs, cast_in, out_refs, cast_out = _split_refs(refs, n_mixer, len(jobs))
    _mixer_inputs(x_ref[...], mixer_refs, out_refs)
    _cast_rows(jobs, cast_in, cast_out)


def _project(h2, mixer, casts, *, tm):
    n, d = h2.shape
    n_mixer = len(mixer.operands())
    cast_in, cast_out, cast_shapes, jobs = _cast_plans(casts, n // tm)
    outs = pl.pallas_call(
        functools.partial(_proj_kernel, n_mixer=n_mixer, jobs=jobs),
        grid=(n // tm,),
        in_specs=[pl.BlockSpec((tm, d), lambda i: (i, 0))] + mixer.in_specs() + cast_in,
        out_specs=mixer.out_specs(tm) + cast_out,
        out_shape=mixer.out_shapes(n, tm) + cast_shapes,
        compiler_params=_params(1), name="proj")(
            h2, *mixer.operands(), *(c.w.stack for c in casts))
    return outs[:len(outs) - len(casts)], outs[len(outs) - len(casts):]


def _cumsum_kernel(lf_ref, aqt_ref, ak_ref, carry_ref):
    @pl.when(pl.program_id(1) == 0)
    def _():
        carry_ref[...] = jnp.zeros_like(carry_ref)

    t = lf_ref.shape[1]
    row = lax.broadcasted_iota(jnp.int32, (t, t), 0)
    col = lax.broadcasted_iota(jnp.int32, (t, t), 1)
    tril = jnp.where(col <= row, 1.0, 0.0).astype(BF16)
    x1, x2, x3 = _split3(lf_ref[0])
    cs = (jnp.dot(tril, x1, preferred_element_type=F32)
          + jnp.dot(tril, x2, preferred_element_type=F32)
          + jnp.dot(tril, x3, preferred_element_type=F32))
    cs = cs + carry_ref[0:1, :]
    carry_ref[...] = jnp.broadcast_to(cs[t - 1:t, :], carry_ref.shape)

    src = lax.broadcasted_iota(jnp.int32, (LANES, LANES), 0)
    dst = lax.broadcasted_iota(jnp.int32, (LANES, LANES), 1)
    lane = lax.broadcasted_iota(jnp.int32, (1, LANES), 1) & (GATE_LANES - 1)
    aq = jnp.where((lane >= SPLIT_TERMS) & (lane < 2 * SPLIT_TERMS), 1.0, 0.0)
    ak = jnp.where(lane < SPLIT_TERMS, 1.0, 0.0)
    for i, part in enumerate(_split3(cs * LOG2E)):
        to_q = jnp.where(dst == GATE_LANES * src + i, 1.0, 0.0).astype(BF16)
        to_k = jnp.where(dst == GATE_LANES * src + SPLIT_TERMS + i, 1.0, 0.0).astype(BF16)
        aq = aq + jnp.dot(part, to_q, preferred_element_type=F32)
        ak = ak - jnp.dot(part, to_k, preferred_element_type=F32)
    eye = jnp.where(src == dst, 1.0, 0.0).astype(BF16)
    aqt_ref[0] = lax.dot_general(eye, aq.astype(BF16), (((1,), (1,)), ((), ())),
                                 preferred_element_type=F32).astype(BF16)
    ak_ref[0] = ak.astype(BF16)


def _cumsum(lf3):
    b, s, _ = lf3.shape
    t = min(CUMSUM_BLK, s)
    nt = s // t
    spec = pl.BlockSpec((1, t, LANES), lambda i, j: (i, j, 0))
    return pl.pallas_call(
        _cumsum_kernel, grid=(b, nt),
        in_specs=[spec],
        out_specs=[pl.BlockSpec((1, LANES, t), lambda i, j: (i * nt + j, 0, 0)), spec],
        out_shape=[jax.ShapeDtypeStruct((b * nt, LANES, t), BF16),
                   jax.ShapeDtypeStruct(lf3.shape, BF16)],
        scratch_shapes=[pltpu.VMEM((F32_SUBLANES, LANES), F32)],
        compiler_params=_params(2), name="gate_cumsum")(lf3)


def _fill_value_rows(vt_ref, v_ref, blk):
    row = lax.broadcasted_iota(jnp.int32, (V_ROWS - HEAD_DIM, blk), 0)
    tail = jnp.where(row == 0, 1.0, 0.0).astype(BF16)
    for hh in range(2):
        for jb in range(v_ref.shape[1] // blk):
            head_rows = v_ref[hh * HEAD_DIM:(hh + 1) * HEAD_DIM, jb * blk:(jb + 1) * blk]
            vt_ref[hh, jb] = jnp.concatenate([head_rows, tail], axis=0)


def _flash_init(m_ref, acc_ref):
    m_ref[...] = jnp.full(m_ref.shape, NEG_INF, F32)
    acc_ref[...] = jnp.zeros(acc_ref.shape, F32)


def _pipelined_blocks(n, scores_into, update, final_scores, final_update, next_tile_scores):
    def step(j_next, j, slot):
        for hh in range(2):
            scores_into(j_next, 1 - slot, hh)
            update(j, slot, hh)

    def last_update_and_final(j, slot):
        for hh in range(2):
            final_scores(hh)
            update(j, slot, hh)
        next_tile_scores()
        final_update()

    def pair(jj, carry):
        j = 2 * jj
        step(j + 1, j, 0)
        step(j + 2, j + 1, 1)
        return carry
    lax.fori_loop(0, jnp.maximum(n - 1, 0) // 2, pair, 0)

    @pl.when(n % 2 == 1)
    def _():
        last_update_and_final(n - 1, 0)

    @pl.when((n % 2 == 0) & (n > 0))
    def _():
        step(n - 1, n - 2, 0)
        last_update_and_final(n - 1, 1)

    @pl.when(n == 0)
    def _():
        for hh in range(2):
            final_scores(hh)
        next_tile_scores()
        final_update()


def _store_scores(s_t, s_ref, mx_ref, idx):
    s_ref[idx + (slice(None), slice(0, s_t.shape[1]))] = s_t
    mx_ref[idx] = jnp.max(s_t, axis=0, keepdims=True)


def _flash_update(s_t, mx, v_rows, m_ref, acc_ref, hh, keep=None, const=None,
                  cols=slice(None)):
    m_old = m_ref[hh, :, cols]
    if const is not None:
        mx = mx + const
    if keep is not None:
        mx = jnp.where(keep, mx, NEG_INF)
    m_new = jnp.maximum(m_old, mx)
    m_safe = jnp.where(m_new == NEG_INF, 0.0, m_new)
    shift = m_safe if const is None else m_safe - const
    if keep is not None:
        shift = jnp.where(keep, shift, float("inf"))
    p = jnp.exp2(s_t - shift).astype(BF16)
    alpha = jnp.exp2(m_old - m_safe)
    m_ref[hh, :, cols] = m_new
    if not isinstance(v_rows, (list, tuple)):
        v_rows = [v_rows]
    keys = p.shape[0] // len(v_rows)
    acc = alpha * acc_ref[hh, :, cols]
    for i, v_i in enumerate(v_rows):
        acc = acc + jnp.dot(v_i, p[i * keys:(i + 1) * keys], preferred_element_type=F32)
    acc_ref[hh, :, cols] = acc


def _head_outputs(acc_ref):
    outs = []
    for hh in range(2):
        acc = acc_ref[hh]
        outs.append(acc[:HEAD_DIM] / acc[HEAD_DIM:HEAD_DIM + 1])
    return jnp.concatenate(outs, axis=0).astype(BF16)


def _for_each_query_tile(nq, tile):
    def body(qi, carry):
        tile(qi)
        return carry
    lax.fori_loop(0, nq, body, 0)


def _head_row_mask(hh):
    row = lax.broadcasted_iota(jnp.int32, (LANES, 1), 0)
    return (row < HEAD_DIM) if hh == 0 else (row >= HEAD_DIM)


def _fox_kernel(qt_ref, k_ref, v_ref, *refs):
    vt_ref = refs[3]
    nq, _, blk = qt_ref.shape
    _fill_value_rows(vt_ref, v_ref, blk)
    hp = pl.program_id(1)
    _for_each_query_tile(nq, lambda qi: _fox_tile(qi, hp, qt_ref, k_ref, *refs))


def _fox_tile(qi, hp, qt_ref, k_ref, aqt_ref, ak_ref, o_ref,
              vt_ref, s_ref, mx_ref, diag_ref, m_ref, acc_ref):
    nq, _, blk = qt_ref.shape
    gate_row = lax.broadcasted_iota(jnp.int32, (LANES, 1), 0)

    def query_operands(tile):
        qt, aqt = qt_ref[tile], aqt_ref[tile]
        out = []
        for hh in range(2):
            first = GATE_LANES * (2 * hp + hh)
            own_gate = (gate_row >= first) & (gate_row < first + GATE_LANES)
            q_rows = jnp.where(_head_row_mask(hh), qt, jnp.zeros_like(qt))
            g_rows = jnp.where(own_gate, aqt, jnp.zeros_like(aqt))
            out.append(jnp.concatenate([q_rows, g_rows], axis=0))
        return out

    def block_scores(w, j, hh):
        rows = pl.ds(pl.multiple_of(j * blk, blk), blk)
        keys = jnp.concatenate([k_ref[0, rows, :], ak_ref[0, rows, :]], axis=1)
        return jnp.dot(keys, w[hh], preferred_element_type=F32)

    w_q = query_operands(qi)

    def scores_into(j, slot, hh):
        _store_scores(block_scores(w_q, j, hh), s_ref, mx_ref, (slot, hh))

    def next_tile_scores():
        w_next = query_operands(jnp.minimum(qi + 1, nq - 1))
        for hh in range(2):
            _store_scores(block_scores(w_next, 0, hh), s_ref, mx_ref, (0, hh))

    def update(j, slot, hh):
        _flash_update(s_ref[slot, hh, :, :blk], mx_ref[slot, hh], vt_ref[hh, j],
                      m_ref, acc_ref, hh)

    def diagonal_scores(hh):
        diag_ref[hh] = block_scores(w_q, qi, hh)

    def diagonal_update():
        half = blk // 2
        lo, hi = slice(0, half), slice(half, blk)
        causal_lo = (lax.broadcasted_iota(jnp.int32, (half, half), 0)
                     <= lax.broadcasted_iota(jnp.int32, (half, half), 1))
        causal_hi = (lax.broadcasted_iota(jnp.int32, (blk, half), 0)
                     <= lax.broadcasted_iota(jnp.int32, (blk, half), 1) + half)
        for hh in range(2):
            v_rows = vt_ref[hh, qi]
            s_lo = jnp.where(causal_lo, diag_ref[hh, lo, lo], NEG_INF)
            _flash_update(s_lo, jnp.max(s_lo, axis=0, keepdims=True),
                          v_rows[:, lo], m_ref, acc_ref, hh, cols=lo)
            s_hi = jnp.where(causal_hi, diag_ref[hh, :, hi], NEG_INF)
            _flash_update(s_hi, jnp.max(s_hi, axis=0, keepdims=True),
                          v_rows, m_ref, acc_ref, hh, cols=hi)

    _flash_init(m_ref, acc_ref)
    _pipelined_blocks(qi, scores_into, update, diagonal_scores, diagonal_update,
                      next_tile_scores)
    o_ref[qi] = _head_outputs(acc_ref)


def _fox_attention(qt3, k3, vt, aqt3, ak3, n_heads):
    b, s, d = k3.shape
    assert n_heads * GATE_LANES <= LANES
    blk = qt3.shape[2]
    assert aqt3.shape[2] == blk and s % blk == 0
    npair = n_heads // 2
    nq = s // blk
    head_pair_tiles = pl.BlockSpec((nq, LANES, blk), lambda bi, hp: (bi, hp, 0))
    return pl.pallas_call(
        _fox_kernel, grid=(b, npair),
        in_specs=[
            head_pair_tiles,
            pl.BlockSpec((1, s, LANES), lambda bi, hp: (bi, 0, hp)),
            pl.BlockSpec((LANES, s), lambda bi, hp: (hp, bi)),
            pl.BlockSpec((nq, LANES, blk), lambda bi, hp: (bi, 0, 0)),
            pl.BlockSpec((1, s, LANES), lambda bi, hp: (bi, 0, 0)),
        ],
        out_specs=head_pair_tiles,
        out_shape=jax.ShapeDtypeStruct(qt3.shape, BF16),
        scratch_shapes=[
            pltpu.VMEM((2, s // blk, V_ROWS, blk), BF16),
            pltpu.VMEM((2, 2, blk, blk + SCORE_PITCH_PAD), F32),
            pltpu.VMEM((2, 2, 1, blk), F32),
            pltpu.VMEM((2, blk, blk), F32),
            pltpu.VMEM((2, 1, blk), F32),
            pltpu.VMEM((2, V_ROWS, blk), F32),
        ],
        compiler_params=_params(2), name="fox_attention")(qt3, k3, vt, aqt3, ak3)


def _t5_bucket_np(n):
    max_exact = REL_BUCKETS // 2
    nf = np.maximum(n, 1).astype(np.float64)
    large = max_exact + (np.log(nf / max_exact) / math.log(REL_MAX_DIST / max_exact)
                         * (REL_BUCKETS - max_exact)).astype(np.int32)
    return np.where(n < max_exact, n, np.minimum(large, REL_BUCKETS - 1)).astype(np.int32)


def _bucket_tiles(blk):
    key = np.arange(blk)[:, None]
    qry = np.arange(blk)[None, :]
    own = np.where(key <= qry, _t5_bucket_np(np.maximum(qry - key, 0)), -1)
    prev = _t5_bucket_np(blk + qry - key)
    return np.stack([own, prev]).astype(np.int32)


def _bias_kernel(tab_ref, bucket_ref, o_ref):
    h = pl.program_id(0)
    bucket = bucket_ref[...]
    acc = jnp.where(bucket < 0, NEG_INF, 0.0).astype(F32)
    for bkt in range(REL_BUCKETS):
        acc = jnp.where(bucket == bkt, tab_ref[bkt, h] * LOG2E, acc)
    o_ref[0] = acc


def _bias_tiles(rel_table, blk):
    n_heads = rel_table.shape[1]
    buckets = jnp.asarray(_bucket_tiles(blk))
    return pl.pallas_call(
        _bias_kernel, grid=(n_heads,),
        in_specs=[pl.BlockSpec(memory_space=pltpu.SMEM),
                  pl.BlockSpec((2, blk, blk), lambda h: (0, 0, 0))],
        out_specs=pl.BlockSpec((1, 2, blk, blk), lambda h: (h, 0, 0, 0)),
        out_shape=jax.ShapeDtypeStruct((n_heads, 2, blk, blk), F32),
        compiler_params=_params(1), name="t5_bias_tiles")(rel_table, buckets)


def _select_blocks(qt_ref, km_ref, sel_ref):
    nq, _, tq = qt_ref.shape
    blk = MOBA_BLOCK
    nblk = km_ref.shape[0]
    assert blk & (blk - 1) == 0
    qt_all = jnp.concatenate([qt_ref[t] for t in range(nq)], axis=1)
    lane = lax.broadcasted_iota(jnp.int32, (1, LANES), 1)
    km_parts = _split3(km_ref[...])
    terms = jnp.dot(
        jnp.concatenate([jnp.where((lane < HEAD_DIM) == (hh == 0), part, jnp.zeros_like(part))
                         for hh in range(2) for part in km_parts], axis=0),
        qt_all, preferred_element_type=F32)
    blk_id = lax.broadcasted_iota(jnp.int32, (nblk, nq * tq), 0)
    own = lax.shift_right_logical(lax.broadcasted_iota(jnp.int32, (nblk, nq * tq), 1),
                                  blk.bit_length() - 1)
    past = blk_id < own
    for hh in range(2):
        gate = None
        for i in range(SPLIT_TERMS):
            first = (SPLIT_TERMS * hh + i) * nblk
            term = terms[first:first + nblk]
            gate = term if gate is None else gate + term
        work = jnp.where(past, gate, NEG_INF)
        picked = jnp.zeros(gate.shape, F32)
        for _ in range(MOBA_TOP_K):
            best = jnp.max(work, axis=0, keepdims=True)
            first = jnp.min(jnp.where(work == best, blk_id, nblk), axis=0, keepdims=True)
            hit = blk_id == first
            picked = jnp.where(hit, 1.0, picked)
            work = jnp.where(hit, NEG_INF, work)
        keep_all = ((picked > 0.5) & past) | (blk_id == own) | (blk_id == (own | 1))
        keep_f = jnp.where(keep_all, 1.0, 0.0)
        for t in range(nq):
            sel_ref[hh, t] = keep_f[:, t * tq:(t + 1) * tq]


def _moba_kernel(tab_ref, qt_ref, k_ref, v_ref, bias_ref, o_ref,
                 vt_ref, km_ref, sel_ref, *scratch):
    blk = MOBA_BLOCK
    member = (lax.shift_right_logical(lax.broadcasted_iota(jnp.int32, (km_ref.shape[0],
                                                                      k_ref.shape[1]), 1),
                                      blk.bit_length() - 1)
              == lax.broadcasted_iota(jnp.int32, (km_ref.shape[0], k_ref.shape[1]), 0))
    km_ref[...] = jnp.dot(jnp.where(member, 1.0, 0.0).astype(BF16), k_ref[0],
                          preferred_element_type=F32) * (1.0 / blk)
    _fill_value_rows(vt_ref, v_ref, blk)
    _select_blocks(qt_ref, km_ref, sel_ref)
    hp = pl.program_id(1)
    _for_each_query_tile(
        qt_ref.shape[0],
        lambda qi: _moba_tile(qi, hp, tab_ref, qt_ref, k_ref, bias_ref, o_ref,
                              vt_ref, sel_ref, *scratch))


def _moba_tile(qi, hp, tab_ref, qt_ref, k_ref, bias_ref, o_ref,
               vt_ref, sel_ref, s_ref, mx_ref, near_ref, own1_ref, m_ref, acc_ref):
    blk = MOBA_BLOCK
    nq, _, tq = qt_ref.shape
    assert tq == 2 * blk and blk >= REL_MAX_DIST
    first_own = 2 * qi

    def head_queries(tile):
        qt = qt_ref[tile]
        return [jnp.where(_head_row_mask(hh), qt, jnp.zeros_like(qt)) for hh in range(2)]

    q_m_t = head_queries(qi)

    def scores(hh, j, queries=q_m_t):
        rows = pl.ds(pl.multiple_of(j * blk, blk), blk)
        return jnp.dot(k_ref[0, rows, :], queries[hh], preferred_element_type=F32)

    def next_tile_scores():
        q_next = head_queries(jnp.minimum(qi + 1, nq - 1))
        for hh in range(2):
            _store_scores(scores(hh, 0, q_next), s_ref, mx_ref, (0, hh))

    def keep(hh, j):
        return sel_ref[hh, qi, pl.ds(j, 1), :] > 0.5

    _flash_init(m_ref, acc_ref)
    far_bias = [tab_ref[REL_BUCKETS - 1, 2 * hp + hh] * LOG2E for hh in range(2)]

    j_prev = jnp.maximum(first_own - 1, 0)
    lo, hi = slice(0, blk), slice(blk, 2 * blk)
    near_mx = {}

    def near_scores(hh):
        own_t, prev_t = bias_ref[hh, 0], bias_ref[hh, 1]
        mask_prev = jnp.where(keep(hh, j_prev) & (qi >= 1), 0.0, NEG_INF)
        mask_own = jnp.where(keep(hh, first_own), 0.0, NEG_INF)
        far_t = jnp.full((blk, blk), far_bias[hh], F32)
        parts = [scores(hh, j_prev) + mask_prev + jnp.concatenate([prev_t, far_t], axis=1),
                 scores(hh, first_own) + mask_own + jnp.concatenate([own_t, prev_t], axis=1)]
        mx = None
        for i, part in enumerate(parts):
            near_ref[hh, i * blk:(i + 1) * blk] = part
            part_mx = jnp.max(part, axis=0, keepdims=True)
            mx = part_mx if mx is None else jnp.maximum(mx, part_mx)
        rows = pl.ds(pl.multiple_of((first_own + 1) * blk, blk), blk)
        last = jnp.dot(k_ref[0, rows, :], q_m_t[hh][:, hi], preferred_element_type=F32) + own_t
        own1_ref[hh] = last
        near_mx[hh] = (mx[:, lo], jnp.maximum(mx[:, hi], jnp.max(last, axis=0, keepdims=True)))

    def near_update():
        for hh in range(2):
            v_near = [vt_ref[hh, j_prev], vt_ref[hh, first_own], vt_ref[hh, first_own + 1]]
            _flash_update(near_ref[hh, :, lo], near_mx[hh][0], v_near[:2],
                          m_ref, acc_ref, hh, cols=lo)
            _flash_update(jnp.concatenate([near_ref[hh, :, hi], own1_ref[hh]], axis=0),
                          near_mx[hh][1], v_near, m_ref, acc_ref, hh, cols=hi)

    def scores_into(j, slot, hh):
        _store_scores(scores(hh, j), s_ref, mx_ref, (slot, hh))

    def update(j, slot, hh):
        _flash_update(s_ref[slot, hh, :, :tq], mx_ref[slot, hh], vt_ref[hh, j],
                      m_ref, acc_ref, hh, keep=keep(hh, j), const=far_bias[hh])

    _pipelined_blocks(jnp.maximum(first_own - 1, 0), scores_into, update,
                      near_scores, near_update, next_tile_scores)
    o_ref[qi] = _head_outputs(acc_ref)


def _moba_attention(qt3, k3, vt, rel_table, bias_t, n_heads):
    b, s, d = k3.shape
    blk = MOBA_BLOCK
    tq = qt3.shape[2]
    assert tq == MOBA_Q_TILE and s % tq == 0
    npair = n_heads // 2
    nblk = s // blk
    nq = s // tq
    head_pair_tiles = pl.BlockSpec((nq, LANES, tq), lambda bi, hp: (bi, hp, 0))
    return pl.pallas_call(
        _moba_kernel, grid=(b, npair),
        in_specs=[
            pl.BlockSpec(memory_space=pltpu.SMEM),
            head_pair_tiles,
            pl.BlockSpec((1, s, LANES), lambda bi, hp: (bi, 0, hp)),
            pl.BlockSpec((LANES, s), lambda bi, hp: (hp, bi)),
            pl.BlockSpec((2, 2, blk, blk), lambda bi, hp: (hp, 0, 0, 0)),
        ],
        out_specs=head_pair_tiles,
        out_shape=jax.ShapeDtypeStruct(qt3.shape, BF16),
        scratch_shapes=[
            pltpu.VMEM((2, nblk, V_ROWS, blk), BF16),
            pltpu.VMEM((nblk, LANES), F32),
            pltpu.VMEM((2, nq, nblk, tq), F32),
            pltpu.VMEM((2, 2, blk, tq + SCORE_PITCH_PAD), F32),
            pltpu.VMEM((2, 2, 1, tq), F32),
            pltpu.VMEM((2, 2 * blk, tq), F32),
            pltpu.VMEM((2, blk, blk), F32),
            pltpu.VMEM((2, 1, tq), F32),
            pltpu.VMEM((2, V_ROWS, tq), F32),
        ],
        compiler_params=_params(2), name="moba_attention")(
            rel_table, qt3, k3, vt, bias_t)


class _Ple(NamedTuple):
    g: jax.Array
    w_gate: jax.Array
    p: _Slab
    w_up: jax.Array

    def in_specs(self, tm):
        layer = self.p.layer
        return [_resident(self.g), _resident(self.w_gate),
                pl.BlockSpec((None, tm, self.p.shape[1]), lambda i: (layer, i, 0)),
                _resident(self.w_up)]

    def operands(self):
        return tuple(_array(a) for a in self)


def _ple_update(x, g_ref, wg_ref, p_ref, wu_ref):
    u = _rmsnorm(x, g_ref[...]).astype(BF16)
    gate = jax.nn.sigmoid(jnp.dot(u, wg_ref[...], preferred_element_type=F32))
    up = jnp.dot(p_ref[...].astype(BF16), wu_ref[...], preferred_element_type=F32)
    return x + gate * up


def _oproj_ffn_kernel(h_ref, ot_ref, wo_ref, g_ref, win_ref, wout_ref, *rest, tf):
    d_ff = wout_ref.shape[0]
    h1 = h_ref[...] + lax.dot_general(ot_ref[...], wo_ref[...], (((0,), (0,)), ((), ())),
                                      preferred_element_type=F32)
    u = _rmsnorm(h1, g_ref[...]).astype(BF16)
    acc = h1
    for c in range(d_ff // tf):
        gate = jnp.dot(u, win_ref[:, c * tf:(c + 1) * tf], preferred_element_type=F32)
        up = jnp.dot(u, win_ref[:, d_ff + c * tf:d_ff + (c + 1) * tf],
                     preferred_element_type=F32)
        act = (gate * jax.nn.sigmoid(gate) * up).astype(BF16)
        acc = acc + jnp.dot(act, wout_ref[c * tf:(c + 1) * tf, :],
                            preferred_element_type=F32)
    *ple_refs, out_ref = rest
    if ple_refs:
        *ple_refs, final_g_ref = ple_refs
        acc = _rmsnorm(_ple_update(acc, *ple_refs), final_g_ref[...])
    out_ref[...] = acc


def _oproj_ffn(h2, o_t, w_o, g, w_in, w_out, *, tm, tf, last=None):
    n, d = h2.shape
    d_ff = w_out.shape[0]
    assert d_ff % tf == 0 and o_t.shape == (n // tm, d, tm)
    row = pl.BlockSpec((tm, d), lambda i: (i, 0))
    weights = (w_o, g, w_in, w_out)
    in_specs = ([row, pl.BlockSpec((None, d, tm), lambda i: (i, 0, 0))]
                + [_resident(w) for w in weights])
    operands = (h2, o_t, *map(_array, weights))
    if last is not None:
        ple, final_g = last
        in_specs += ple.in_specs(tm) + [_resident(final_g)]
        operands += ple.operands() + (final_g,)
    return pl.pallas_call(
        functools.partial(_oproj_ffn_kernel, tf=tf), grid=(n // tm,),
        in_specs=in_specs, out_specs=row,
        out_shape=jax.ShapeDtypeStruct((n, d), F32),
        compiler_params=_params(1), name="oproj_ffn")(*operands)


def _ple_next_kernel(x_ref, g_ref, wg_ref, p_ref, wu_ref, *refs, n_mixer, jobs):
    mixer_refs, cast_in, out_refs, cast_out = _split_refs(refs, n_mixer, len(jobs))
    y = _ple_update(x_ref[...], g_ref, wg_ref, p_ref, wu_ref)
    out_refs[0][...] = y
    _mixer_inputs(y, mixer_refs, out_refs[1:])
    _cast_rows(jobs, cast_in, cast_out)


def _ple_next(h2, ple, mixer, casts, *, tm):
    n, d = h2.shape
    row = pl.BlockSpec((tm, d), lambda i: (i, 0))
    n_mixer = len(mixer.operands())
    cast_in, cast_out, cast_shapes, jobs = _cast_plans(casts, n // tm)
    h, *outs = pl.pallas_call(
        functools.partial(_ple_next_kernel, n_mixer=n_mixer, jobs=jobs),
        grid=(n // tm,),
        in_specs=[row] + ple.in_specs(tm) + mixer.in_specs() + cast_in,
        out_specs=[row] + mixer.out_specs(tm) + cast_out,
        out_shape=[jax.ShapeDtypeStruct((n, d), F32)] + mixer.out_shapes(n, tm) + cast_shapes,
        compiler_params=_params(1), name="ple_proj")(
            h2, *ple.operands(), *mixer.operands(), *(c.w.stack for c in casts))
    return h, outs[:len(outs) - len(casts)], outs[len(outs) - len(casts):]


def _row_tile(n, want):
    t = min(want, n)
    assert n % t == 0
    return t


def _col_tile(n, want):
    t = min(want, n)
    while n % t:
        t -= LANES
    return t


def kernel(x, p, attn_norm_g, fox_w_in, fox_b_f, fox_w_o, moba_w_in, moba_w_o, rel_bias_table,
           ffn_norm_g, ffn_w_in, ffn_w_out, ple_norm_g, ple_w_gate, ple_w_up, final_norm_g):
    b, s, d = x.shape
    depth = p.shape[0]
    n_heads = rel_bias_table.shape[1]
    assert d == n_heads * HEAD_DIM and n_heads % 2 == 0 and n_heads <= LANES
    n = b * s
    tm = _row_tile(n, DENSE_ROW_TILE)
    tf = _col_tile(ffn_w_out.shape[1], FFN_COL_CHUNK)

    def row_vec(v):
        return v.reshape(1, -1).astype(F32)

    w_o_mix = (fox_w_o, moba_w_o)
    p3 = p.reshape(depth, n, -1)
    fox_wt = jnp.swapaxes(fox_w_in, 1, 2)

    def qkv_cast(i):
        if i % 2 == 0:
            return _Cast(_Slab(fox_wt, i // 2), 3 * d, scaled=d)
        return _Cast(_Slab(moba_w_in, i // 2), d, scaled=d, transpose=True)

    def layer_casts(i):
        jobs = [_Cast(_Slab(w, layer), w.shape[1]) for w, layer in (
            (w_o_mix[i % 2], i // 2), (ffn_w_in, i), (ffn_w_out, i), (ple_w_gate, i),
            (ple_w_up, i))]
        return jobs + ([qkv_cast(i + 1)] if i + 1 < depth else [])

    def mixer(i, w_qkv):
        parts = [row_vec(attn_norm_g[i]), w_qkv]
        if i % 2 == 0:
            pad_heads = LANES - n_heads
            parts += [jnp.pad(fox_wt[i // 2, 3 * d:], ((0, pad_heads), (0, 0))),
                      jnp.pad(row_vec(fox_b_f[i // 2]), ((0, 0), (0, pad_heads)))]
        return _Mixer(*parts)

    rel_table = rel_bias_table.astype(F32)
    bias_t = _bias_tiles(rel_table, MOBA_BLOCK)

    h = x.reshape(n, d).astype(F32)
    row = lax.broadcasted_iota(jnp.int32, (3 * d, 1), 0)
    w_qkv = (fox_wt[0, :3 * d] * jnp.where(row < d, Q_SCALE, 1.0)).astype(BF16)
    mixed, weights = _project(h, mixer(0, w_qkv), layer_casts(0), tm=tm)
    for i in range(depth):
        w_o, w_ffn_in, w_ffn_out, w_ple_gate, w_ple_up, *w_qkv_next = weights
        qt, k3, vt = mixed[0], mixed[1].reshape(b, s, d), mixed[2]
        if i % 2 == 0:
            aqt, ak = _cumsum(mixed[3].reshape(b, s, LANES))
            o_t = _fox_attention(qt, k3, vt, aqt, ak, n_heads)
        else:
            o_t = _moba_attention(qt, k3, vt, rel_table, bias_t, n_heads)
        ple = _Ple(row_vec(ple_norm_g[i]), w_ple_gate, _Slab(p3, i), w_ple_up)
        last = (ple, row_vec(final_norm_g)) if i + 1 == depth else None
        h = _oproj_ffn(h, o_t, w_o, row_vec(ffn_norm_g[i]), w_ffn_in, w_ffn_out,
                       tm=tm, tf=tf, last=last)
        if last is None:
            h, mixed, weights = _ple_next(h, ple, mixer(i + 1, *w_qkv_next),
                                          layer_casts(i + 1), tm=tm)
    return h.reshape(b, s, d).astype(x.dtype)
```

```python
import functools
import math
from typing import NamedTuple, Optional

import numpy as np
import jax
import jax.numpy as jnp
from jax import lax
from jax.experimental import pallas as pl
from jax.experimental.pallas import tpu as pltpu

F32 = jnp.float32
BF16 = jnp.bfloat16

RMS_EPS = 1e-6
HEAD_DIM = 64
MOBA_BLOCK = 256
MOBA_TOP_K = 3
REL_BUCKETS = 32
REL_MAX_DIST = 128

LANES = 128
F32_SUBLANES = 8
BF16_SUBLANES = 16
V7X_VMEM_BYTES = 64 * 1024 * 1024
V_ROWS = HEAD_DIM + BF16_SUBLANES
SPLIT_TERMS = 3
GATE_LANES = 8
SCORE_PITCH_PAD = LANES
LOG2E = math.log2(math.e)
Q_SCALE = HEAD_DIM ** -0.5 * LOG2E
MOBA_Q_TILE = 2 * MOBA_BLOCK
ATTN_Q_TILE = MOBA_Q_TILE
DENSE_ROW_TILE = ATTN_Q_TILE
CUMSUM_BLK = ATTN_Q_TILE
PROJ_COL_CHUNK = 512
FFN_COL_CHUNK = 256
PLE_COL_CHUNK = 256
VMEM_LIMIT_BYTES = V7X_VMEM_BYTES // 8 * 7
NEG_INF = float("-inf")


def _params(n_axes):
    return pltpu.CompilerParams(
        dimension_semantics=("arbitrary",) * n_axes,
        vmem_limit_bytes=VMEM_LIMIT_BYTES)


def _split3(x):
    x1 = x.astype(BF16)
    r1 = x - x1.astype(F32)
    x2 = r1.astype(BF16)
    x3 = (r1 - x2.astype(F32)).astype(BF16)
    return x1, x2, x3


def _rmsnorm(x, g):
    ms = jnp.mean(x * x, axis=-1, keepdims=True)
    return x * lax.rsqrt(ms + RMS_EPS) * g


def _log_sigmoid(x):
    return jnp.minimum(x, 0.0) - jnp.log1p(jnp.exp(-jnp.abs(x)))


class _Slab(NamedTuple):
    stack: jax.Array
    layer: int

    @property
    def shape(self):
        return self.stack.shape[1:]


def _resident(x):
    if isinstance(x, _Slab):
        layer, zeros = x.layer, (0,) * len(x.shape)
        return pl.BlockSpec((None,) + x.shape, lambda i: (layer,) + zeros,
                            pipeline_mode=pl.Buffered(1))
    zeros = (0,) * x.ndim
    return pl.BlockSpec(x.shape, lambda i: zeros, pipeline_mode=pl.Buffered(1))


def _array(x):
    return x.stack if isinstance(x, _Slab) else x


class _Cast(NamedTuple):
    w: _Slab
    rows: int
    scaled: int = 0
    transpose: bool = False

    def plan(self, steps):
        cols = self.w.shape[1]
        min_rows = LANES if self.transpose else BF16_SUBLANES
        assert self.rows % min_rows == 0 and cols % LANES == 0
        chunks = math.gcd(steps, self.rows // min_rows)
        span, layer, chunk = steps // chunks, self.w.layer, self.rows // chunks
        in_spec = pl.BlockSpec((None, chunk, cols), lambda i: (layer, i // span, 0))
        if self.transpose:
            return (in_spec, pl.BlockSpec((cols, chunk), lambda i: (0, i // span)),
                    jax.ShapeDtypeStruct((cols, self.rows), BF16), span)
        return (in_spec, pl.BlockSpec((chunk, cols), lambda i: (i // span, 0)),
                jax.ShapeDtypeStruct((self.rows, cols), BF16), span)


def _cast_plans(casts, steps):
    plans = [c.plan(steps) for c in casts]
    return ([p[0] for p in plans], [p[1] for p in plans], [p[2] for p in plans],
            tuple((c.transpose, c.scaled, p[3]) for c, p in zip(casts, plans)))


def _cast_rows(jobs, in_refs, out_refs):
    for (transpose, scaled, span), in_ref, out_ref in zip(jobs, in_refs, out_refs):
        w = in_ref[...].T if transpose else in_ref[...]
        if scaled:
            first_row = 0 if transpose else pl.program_id(0) // span * w.shape[0]
            row = first_row + lax.broadcasted_iota(jnp.int32, (w.shape[0], 1), 0)
            w = w * jnp.where(row < scaled, Q_SCALE, 1.0)
        out_ref[...] = w.astype(BF16)


class _Mixer(NamedTuple):
    g: jax.Array
    w_qkv: jax.Array
    w_f: Optional[jax.Array] = None
    b_f: Optional[jax.Array] = None

    def operands(self):
        return tuple(_array(a) for a in self if a is not None)

    def in_specs(self):
        return [_resident(a) for a in self if a is not None]

    def out_specs(self, tm):
        d = self.w_qkv.shape[1]
        specs = [pl.BlockSpec((None, d, tm), lambda i: (i, 0, 0)),
                 pl.BlockSpec((tm, d), lambda i: (i, 0)),
                 pl.BlockSpec((d, tm), lambda i: (0, i))]
        if self.w_f is not None:
            specs.append(pl.BlockSpec((tm, LANES), lambda i: (i, 0)))
        return specs

    def out_shapes(self, n, tm):
        d = self.w_qkv.shape[1]
        shapes = [jax.ShapeDtypeStruct((n // tm, d, tm), BF16),
                  jax.ShapeDtypeStruct((n, d), BF16), jax.ShapeDtypeStruct((d, n), BF16)]
        if self.w_f is not None:
            shapes.append(jax.ShapeDtypeStruct((n, LANES), F32))
        return shapes


def _chunks(total, want):
    step = want if total % want == 0 else total
    return [slice(c * step, (c + 1) * step) for c in range(total // step)]


def _mixer_inputs(y, mixer_refs, out_refs):
    g_ref, wt_ref = mixer_refs[:2]
    d = wt_ref.shape[1]
    u = _rmsnorm(y, g_ref[...]).astype(BF16)

    def contract_last(a, b):
        return lax.dot_general(a, b, (((1,), (1,)), ((), ())), preferred_element_type=F32)

    if len(mixer_refs) > 2:
        wft_ref, bf_ref = mixer_refs[2:]
        f_logit = contract_last(u, wft_ref[...].astype(BF16)) + bf_ref[...]
        out_refs[3][...] = _log_sigmoid(f_logit)
    for first_row, out_ref in ((0, out_refs[0]), (2 * d, out_refs[2])):
        for rows in _chunks(d, PROJ_COL_CHUNK):
            w_rows = wt_ref[first_row + rows.start:first_row + rows.stop, :]
            out_ref[rows, :] = contract_last(w_rows, u).astype(BF16)
    for cols in _chunks(d, PROJ_COL_CHUNK):
        out_refs[1][:, cols] = contract_last(
            u, wt_ref[d + cols.start:d + cols.stop, :]).astype(BF16)


def _split_refs(refs, n_mixer, n_cast):
    n_out = len(refs) - n_cast
    return (refs[:n_mixer], refs[n_mixer:n_mixer + n_cast],
            refs[n_mixer + n_cast:n_out], refs[n_out:])


def _proj_kernel(x_ref, *refs, n_mixer, jobs):
    mixer_refs, cast_in, out_refs, cast_out = _split_refs(refs, n_mixer, len(jobs))
    _mixer_inputs(x_ref[...], mixer_refs, out_refs)
    _cast_rows(jobs, cast_in, cast_out)


def _project(h2, mixer, casts, *, tm):
    n, d = h2.shape
    n_mixer = len(mixer.operands())
    cast_in, cast_out, cast_shapes, jobs = _cast_plans(casts, n // tm)
    outs = pl.pallas_call(
        functools.partial(_proj_kernel, n_mixer=n_mixer, jobs=jobs),
        grid=(n // tm,),
        in_specs=[pl.BlockSpec((tm, d), lambda i: (i, 0))] + mixer.in_specs() + cast_in,
        out_specs=mixer.out_specs(tm) + cast_out,
        out_shape=mixer.out_shapes(n, tm) + cast_shapes,
        compiler_params=_params(1), name="proj")(
            h2, *mixer.operands(), *(c.w.stack for c in casts))
    return outs[:len(outs) - len(casts)], outs[len(outs) - len(casts):]


def _cumsum_kernel(lf_ref, aqt_ref, ak_ref, carry_ref):
    @pl.when(pl.program_id(1) == 0)
    def _():
        carry_ref[...] = jnp.zeros_like(carry_ref)

    t = lf_ref.shape[1]
    row = lax.broadcasted_iota(jnp.int32, (t, t), 0)
    col = lax.broadcasted_iota(jnp.int32, (t, t), 1)
    tril = jnp.where(col <= row, 1.0, 0.0).astype(BF16)
    x1, x2, x3 = _split3(lf_ref[0])
    cs = (jnp.dot(tril, x1, preferred_element_type=F32)
          + jnp.dot(tril, x2, preferred_element_type=F32)
          + jnp.dot(tril, x3, preferred_element_type=F32))
    cs = cs + carry_ref[0:1, :]
    carry_ref[...] = jnp.broadcast_to(cs[t - 1:t, :], carry_ref.shape)

    src = lax.broadcasted_iota(jnp.int32, (LANES, LANES), 0)
    dst = lax.broadcasted_iota(jnp.int32, (LANES, LANES), 1)
    lane = lax.broadcasted_iota(jnp.int32, (1, LANES), 1) & (GATE_LANES - 1)
    aq = jnp.where((lane >= SPLIT_TERMS) & (lane < 2 * SPLIT_TERMS), 1.0, 0.0)
    ak = jnp.where(lane < SPLIT_TERMS, 1.0, 0.0)
    for i, part in enumerate(_split3(cs * LOG2E)):
        to_q = jnp.where(dst == GATE_LANES * src + i, 1.0, 0.0).astype(BF16)
        to_k = jnp.where(dst == GATE_LANES * src + SPLIT_TERMS + i, 1.0, 0.0).astype(BF16)
        aq = aq + jnp.dot(part, to_q, preferred_element_type=F32)
        ak = ak - jnp.dot(part, to_k, preferred_element_type=F32)
    eye = jnp.where(src == dst, 1.0, 0.0).astype(BF16)
    aqt_ref[0] = lax.dot_general(eye, aq.astype(BF16), (((1,), (1,)), ((), ())),
                                 preferred_element_type=F32).astype(BF16)
    ak_ref[0] = ak.astype(BF16)


def _cumsum(lf3):
    b, s, _ = lf3.shape
    t = min(CUMSUM_BLK, s)
    nt = s // t
    spec = pl.BlockSpec((1, t, LANES), lambda i, j: (i, j, 0))
    return pl.pallas_call(
        _cumsum_kernel, grid=(b, nt),
        in_specs=[spec],
        out_specs=[pl.BlockSpec((1, LANES, t), lambda i, j: (i * nt + j, 0, 0)), spec],
        out_shape=[jax.ShapeDtypeStruct((b * nt, LANES, t), BF16),
                   jax.ShapeDtypeStruct(lf3.shape, BF16)],
        scratch_shapes=[pltpu.VMEM((F32_SUBLANES, LANES), F32)],
        compiler_params=_params(2), name="gate_cumsum")(lf3)


def _fill_value_rows(vt_ref, v_ref, blk):
    row = lax.broadcasted_iota(jnp.int32, (V_ROWS - HEAD_DIM, blk), 0)
    tail = jnp.where(row == 0, 1.0, 0.0).astype(BF16)
    for hh in range(2):
        for jb in range(v_ref.shape[1] // blk):
            head_rows = v_ref[hh * HEAD_DIM:(hh + 1) * HEAD_DIM, jb * blk:(jb + 1) * blk]
            vt_ref[hh, jb] = jnp.concatenate([head_rows, tail], axis=0)


def _flash_init(m_ref, acc_ref):
    m_ref[...] = jnp.full(m_ref.shape, NEG_INF, F32)
    acc_ref[...] = jnp.zeros(acc_ref.shape, F32)


def _pipelined_blocks(n, scores_into, update, final_scores, final_update, next_tile_scores):
    def step(j_next, j, slot):
        for hh in range(2):
            scores_into(j_next, 1 - slot, hh)
            update(j, slot, hh)

    def last_update_and_final(j, slot):
        for hh in range(2):
            final_scores(hh)
            update(j, slot, hh)
        next_tile_scores()
        final_update()

    def pair(jj, carry):
        j = 2 * jj
        step(j + 1, j, 0)
        step(j + 2, j + 1, 1)
        return carry
    lax.fori_loop(0, jnp.maximum(n - 1, 0) // 2, pair, 0)

    @pl.when(n % 2 == 1)
    def _():
        last_update_and_final(n - 1, 0)

    @pl.when((n % 2 == 0) & (n > 0))
    def _():
        step(n - 1, n - 2, 0)
        last_update_and_final(n - 1, 1)

    @pl.when(n == 0)
    def _():
        for hh in range(2):
            final_scores(hh)
        next_tile_scores()
        final_update()


def _store_scores(s_t, s_ref, mx_ref, idx):
    s_ref[idx + (slice(None), slice(0, s_t.shape[1]))] = s_t
    mx_ref[idx] = jnp.max(s_t, axis=0, keepdims=True)


def _flash_update(s_t, mx, v_rows, m_ref, acc_ref, hh, keep=None, const=None,
                  cols=slice(None)):
    m_old = m_ref[hh, :, cols]
    if const is not None:
        mx = mx + const
    if keep is not None:
        mx = jnp.where(keep, mx, NEG_INF)
    m_new = jnp.maximum(m_old, mx)
    m_safe = jnp.where(m_new == NEG_INF, 0.0, m_new)
    shift = m_safe if const is None else m_safe - const
    if keep is not None:
        shift = jnp.where(keep, shift, float("inf"))
    p = jnp.exp2(s_t - shift).astype(BF16)
    alpha = jnp.exp2(m_old - m_safe)
    m_ref[hh, :, cols] = m_new
    if not isinstance(v_rows, (list, tuple)):
        v_rows = [v_rows]
    keys = p.shape[0] // len(v_rows)
    acc = alpha * acc_ref[hh, :, cols]
    for i, v_i in enumerate(v_rows):
        acc = acc + jnp.dot(v_i, p[i * keys:(i + 1) * keys], preferred_element_type=F32)
    acc_ref[hh, :, cols] = acc


def _head_outputs(acc_ref):
    outs = []
    for hh in range(2):
        acc = acc_ref[hh]
        outs.append(acc[:HEAD_DIM] / acc[HEAD_DIM:HEAD_DIM + 1])
    return jnp.concatenate(outs, axis=0).astype(BF16)


def _for_each_query_tile(nq, tile):
    def body(qi, carry):
        tile(qi)
        return carry
    lax.fori_loop(0, nq, body, 0)


def _head_row_mask(hh):
    row = lax.broadcasted_iota(jnp.int32, (LANES, 1), 0)
    return (row < HEAD_DIM) if hh == 0 else (row >= HEAD_DIM)


def _fox_kernel(qt_ref, k_ref, v_ref, *refs):
    vt_ref = refs[3]
    nq, _, blk = qt_ref.shape
    _fill_value_rows(vt_ref, v_ref, blk)
    hp = pl.program_id(1)
    _for_each_query_tile(nq, lambda qi: _fox_tile(qi, hp, qt_ref, k_ref, *refs))


def _fox_tile(qi, hp, qt_ref, k_ref, aqt_ref, ak_ref, o_ref,
              vt_ref, s_ref, mx_ref, diag_ref, m_ref, acc_ref):
    nq, _, blk = qt_ref.shape
    gate_row = lax.broadcasted_iota(jnp.int32, (LANES, 1), 0)

    def query_operands(tile):
        qt, aqt = qt_ref[tile], aqt_ref[tile]
        out = []
        for hh in range(2):
            first = GATE_LANES * (2 * hp + hh)
            own_gate = (gate_row >= first) & (gate_row < first + GATE_LANES)
            q_rows = jnp.where(_head_row_mask(hh), qt, jnp.zeros_like(qt))
            g_rows = jnp.where(own_gate, aqt, jnp.zeros_like(aqt))
            out.append(jnp.concatenate([q_rows, g_rows], axis=0))
        return out

    def block_scores(w, j, hh):
        rows = pl.ds(pl.multiple_of(j * blk, blk), blk)
        keys = jnp.concatenate([k_ref[0, rows, :], ak_ref[0, rows, :]], axis=1)
        return jnp.dot(keys, w[hh], preferred_element_type=F32)

    w_q = query_operands(qi)

    def scores_into(j, slot, hh):
        _store_scores(block_scores(w_q, j, hh), s_ref, mx_ref, (slot, hh))

    def next_tile_scores():
        w_next = query_operands(jnp.minimum(qi + 1, nq - 1))
        for hh in range(2):
            _store_scores(block_scores(w_next, 0, hh), s_ref, mx_ref, (0, hh))

    def update(j, slot, hh):
        _flash_update(s_ref[slot, hh, :, :blk], mx_ref[slot, hh], vt_ref[hh, j],
                      m_ref, acc_ref, hh)

    def diagonal_scores(hh):
        diag_ref[hh] = block_scores(w_q, qi, hh)

    def diagonal_update():
        half = blk // 2
        lo, hi = slice(0, half), slice(half, blk)
        causal_lo = (lax.broadcasted_iota(jnp.int32, (half, half), 0)
                     <= lax.broadcasted_iota(jnp.int32, (half, half), 1))
        causal_hi = (lax.broadcasted_iota(jnp.int32, (blk, half), 0)
                     <= lax.broadcasted_iota(jnp.int32, (blk, half), 1) + half)
        for hh in range(2):
            v_rows = vt_ref[hh, qi]
            s_lo = jnp.where(causal_lo, diag_ref[hh, lo, lo], NEG_INF)
            _flash_update(s_lo, jnp.max(s_lo, axis=0, keepdims=True),
                          v_rows[:, lo], m_ref, acc_ref, hh, cols=lo)
            s_hi = jnp.where(causal_hi, diag_ref[hh, :, hi], NEG_INF)
            _flash_update(s_hi, jnp.max(s_hi, axis=0, keepdims=True),
                          v_rows, m_ref, acc_ref, hh, cols=hi)

    _flash_init(m_ref, acc_ref)
    _pipelined_blocks(qi, scores_into, update, diagonal_scores, diagonal_update,
                      next_tile_scores)
    o_ref[qi] = _head_outputs(acc_ref)


def _fox_attention(qt3, k3, vt, aqt3, ak3, n_heads):
    b, s, d = k3.shape
    assert n_heads * GATE_LANES <= LANES
    blk = qt3.shape[2]
    assert aqt3.shape[2] == blk and s % blk == 0
    npair = n_heads // 2
    nq = s // blk
    head_pair_tiles = pl.BlockSpec((nq, LANES, blk), lambda bi, hp: (bi, hp, 0))
    return pl.pallas_call(
        _fox_kernel, grid=(b, npair),
        in_specs=[
            head_pair_tiles,
            pl.BlockSpec((1, s, LANES), lambda bi, hp: (bi, 0, hp)),
            pl.BlockSpec((LANES, s), lambda bi, hp: (hp, bi)),
            pl.BlockSpec((nq, LANES, blk), lambda bi, hp: (bi, 0, 0)),
            pl.BlockSpec((1, s, LANES), lambda bi, hp: (bi, 0, 0)),
        ],
        out_specs=head_pair_tiles,
        out_shape=jax.ShapeDtypeStruct(qt3.shape, BF16),
        scratch_shapes=[
            pltpu.VMEM((2, s // blk, V_ROWS, blk), BF16),
            pltpu.VMEM((2, 2, blk, blk + SCORE_PITCH_PAD), F32),
            pltpu.VMEM((2, 2, 1, blk), F32),
            pltpu.VMEM((2, blk, blk), F32),
            pltpu.VMEM((2, 1, blk), F32),
            pltpu.VMEM((2, V_ROWS, blk), F32),
        ],
        compiler_params=_params(2), name="fox_attention")(qt3, k3, vt, aqt3, ak3)


def _t5_bucket_np(n):
    max_exact = REL_BUCKETS // 2
    nf = np.maximum(n, 1).astype(np.float64)
    large = max_exact + (np.log(nf / max_exact) / math.log(REL_MAX_DIST / max_exact)
                         * (REL_BUCKETS - max_exact)).astype(np.int32)
    return np.where(n < max_exact, n, np.minimum(large, REL_BUCKETS - 1)).astype(np.int32)


def _bucket_tiles(blk):
    key = np.arange(blk)[:, None]
    qry = np.arange(blk)[None, :]
    own = np.where(key <= qry, _t5_bucket_np(np.maximum(qry - key, 0)), -1)
    prev = _t5_bucket_np(blk + qry - key)
    return np.stack([own, prev]).astype(np.int32)


def _bias_kernel(tab_ref, bucket_ref, o_ref):
    h = pl.program_id(0)
    bucket = bucket_ref[...]
    acc = jnp.where(bucket < 0, NEG_INF, 0.0).astype(F32)
    for bkt in range(REL_BUCKETS):
        acc = jnp.where(bucket == bkt, tab_ref[bkt, h] * LOG2E, acc)
    o_ref[0] = acc


def _bias_tiles(rel_table, blk):
    n_heads = rel_table.shape[1]
    buckets = jnp.asarray(_bucket_tiles(blk))
    return pl.pallas_call(
        _bias_kernel, grid=(n_heads,),
        in_specs=[pl.BlockSpec(memory_space=pltpu.SMEM),
                  pl.BlockSpec((2, blk, blk), lambda h: (0, 0, 0))],
        out_specs=pl.BlockSpec((1, 2, blk, blk), lambda h: (h, 0, 0, 0)),
        out_shape=jax.ShapeDtypeStruct((n_heads, 2, blk, blk), F32),
        compiler_params=_params(1), name="t5_bias_tiles")(rel_table, buckets)


def _select_blocks(qt_ref, km_ref, sel_ref):
    nq, _, tq = qt_ref.shape
    blk = MOBA_BLOCK
    nblk = km_ref.shape[0]
    assert blk & (blk - 1) == 0
    qt_all = jnp.concatenate([qt_ref[t] for t in range(nq)], axis=1)
    lane = lax.broadcasted_iota(jnp.int32, (1, LANES), 1)
    km_parts = _split3(km_ref[...])
    terms = jnp.dot(
        jnp.concatenate([jnp.where((lane < HEAD_DIM) == (hh == 0), part, jnp.zeros_like(part))
                         for hh in range(2) for part in km_parts], axis=0),
        qt_all, preferred_element_type=F32)
    blk_id = lax.broadcasted_iota(jnp.int32, (nblk, nq * tq), 0)
    own = lax.shift_right_logical(lax.broadcasted_iota(jnp.int32, (nblk, nq * tq), 1),
                                  blk.bit_length() - 1)
    past = blk_id < own
    for hh in range(2):
        gate = None
        for i in range(SPLIT_TERMS):
            first = (SPLIT_TERMS * hh + i) * nblk
            term = terms[first:first + nblk]
            gate = term if gate is None else gate + term
        work = jnp.where(past, gate, NEG_INF)
        picked = jnp.zeros(gate.shape, F32)
        for _ in range(MOBA_TOP_K):
            best = jnp.max(work, axis=0, keepdims=True)
            first = jnp.min(jnp.where(work == best, blk_id, nblk), axis=0, keepdims=True)
            hit = blk_id == first
            picked = jnp.where(hit, 1.0, picked)
            work = jnp.where(hit, NEG_INF, work)
        keep_all = ((picked > 0.5) & past) | (blk_id == own) | (blk_id == (own | 1))
        keep_f = jnp.where(keep_all, 1.0, 0.0)
        for t in range(nq):
            sel_ref[hh, t] = keep_f[:, t * tq:(t + 1) * tq]


def _moba_kernel(tab_ref, qt_ref, k_ref, v_ref, bias_ref, o_ref,
                 vt_ref, km_ref, sel_ref, *scratch):
    blk = MOBA_BLOCK
    member = (lax.shift_right_logical(lax.broadcasted_iota(jnp.int32, (km_ref.shape[0],
                                                                      k_ref.shape[1]), 1),
                                      blk.bit_length() - 1)
              == lax.broadcasted_iota(jnp.int32, (km_ref.shape[0], k_ref.shape[1]), 0))
    km_ref[...] = jnp.dot(jnp.where(member, 1.0, 0.0).astype(BF16), k_ref[0],
                          preferred_element_type=F32) * (1.0 / blk)
    _fill_value_rows(vt_ref, v_ref, blk)
    _select_blocks(qt_ref, km_ref, sel_ref)
    hp = pl.program_id(1)
    _for_each_query_tile(
        qt_ref.shape[0],
        lambda qi: _moba_tile(qi, hp, tab_ref, qt_ref, k_ref, bias_ref, o_ref,
                              vt_ref, sel_ref, *scratch))


def _moba_tile(qi, hp, tab_ref, qt_ref, k_ref, bias_ref, o_ref,
               vt_ref, sel_ref, s_ref, mx_ref, near_ref, own1_ref, m_ref, acc_ref):
    blk = MOBA_BLOCK
    nq, _, tq = qt_ref.shape
    assert tq == 2 * blk and blk >= REL_MAX_DIST
    first_own = 2 * qi

    def head_queries(tile):
        qt = qt_ref[tile]
        return [jnp.where(_head_row_mask(hh), qt, jnp.zeros_like(qt)) for hh in range(2)]

    q_m_t = head_queries(qi)

    def scores(hh, j, queries=q_m_t):
        rows = pl.ds(pl.multiple_of(j * blk, blk), blk)
        return jnp.dot(k_ref[0, rows, :], queries[hh], preferred_element_type=F32)

    def next_tile_scores():
        q_next = head_queries(jnp.minimum(qi + 1, nq - 1))
        for hh in range(2):
            _store_scores(scores(hh, 0, q_next), s_ref, mx_ref, (0, hh))

    def keep(hh, j):
        return sel_ref[hh, qi, pl.ds(j, 1), :] > 0.5

    _flash_init(m_ref, acc_ref)
    far_bias = [tab_ref[REL_BUCKETS - 1, 2 * hp + hh] * LOG2E for hh in range(2)]

    j_prev = jnp.maximum(first_own - 1, 0)
    lo, hi = slice(0, blk), slice(blk, 2 * blk)
    near_mx = {}

    def near_scores(hh):
        own_t, prev_t = bias_ref[hh, 0], bias_ref[hh, 1]
        mask_prev = jnp.where(keep(hh, j_prev) & (qi >= 1), 0.0, NEG_INF)
        mask_own = jnp.where(keep(hh, first_own), 0.0, NEG_INF)
        far_t = jnp.full((blk, blk), far_bias[hh], F32)
        parts = [scores(hh, j_prev) + mask_prev + jnp.concatenate([prev_t, far_t], axis=1),
                 scores(hh, first_own) + mask_own + jnp.concatenate([own_t, prev_t], axis=1)]
        mx = None
        for i, part in enumerate(parts):
            near_ref[hh, i * blk:(i + 1) * blk] = part
            part_mx = jnp.max(part, axis=0, keepdims=True)
            mx = part_mx if mx is None else jnp.maximum(mx, part_mx)
        rows = pl.ds(pl.multiple_of((first_own + 1) * blk, blk), blk)
        last = jnp.dot(k_ref[0, rows, :], q_m_t[hh][:, hi], preferred_element_type=F32) + own_t
        own1_ref[hh] = last
        near_mx[hh] = (mx[:, lo], jnp.maximum(mx[:, hi], jnp.max(last, axis=0, keepdims=True)))

    def near_update():
        for hh in range(2):
            v_near = [vt_ref[hh, j_prev], vt_ref[hh, first_own], vt_ref[hh, first_own + 1]]
            _flash_update(near_ref[hh, :, lo], near_mx[hh][0], v_near[:2],
                          m_ref, acc_ref, hh, cols=lo)
            _flash_update(jnp.concatenate([near_ref[hh, :, hi], own1_ref[hh]], axis=0),
                          near_mx[hh][1], v_near, m_ref, acc_ref, hh, cols=hi)

    def scores_into(j, slot, hh):
        _store_scores(scores(hh, j), s_ref, mx_ref, (slot, hh))

    def update(j, slot, hh):
        _flash_update(s_ref[slot, hh, :, :tq], mx_ref[slot, hh], vt_ref[hh, j],
                      m_ref, acc_ref, hh, keep=keep(hh, j), const=far_bias[hh])

    _pipelined_blocks(jnp.maximum(first_own - 1, 0), scores_into, update,
                      near_scores, near_update, next_tile_scores)
    o_ref[qi] = _head_outputs(acc_ref)


def _moba_attention(qt3, k3, vt, rel_table, bias_t, n_heads):
    b, s, d = k3.shape
    blk = MOBA_BLOCK
    tq = qt3.shape[2]
    assert tq == MOBA_Q_TILE and s % tq == 0
    npair = n_heads // 2
    nblk = s // blk
    nq = s // tq
    head_pair_tiles = pl.BlockSpec((nq, LANES, tq), lambda bi, hp: (bi, hp, 0))
    return pl.pallas_call(
        _moba_kernel, grid=(b, npair),
        in_specs=[
            pl.BlockSpec(memory_space=pltpu.SMEM),
            head_pair_tiles,
            pl.BlockSpec((1, s, LANES), lambda bi, hp: (bi, 0, hp)),
            pl.BlockSpec((LANES, s), lambda bi, hp: (hp, bi)),
            pl.BlockSpec((2, 2, blk, blk), lambda bi, hp: (hp, 0, 0, 0)),
        ],
        out_specs=head_pair_tiles,
        out_shape=jax.ShapeDtypeStruct(qt3.shape, BF16),
        scratch_shapes=[
            pltpu.VMEM((2, nblk, V_ROWS, blk), BF16),
            pltpu.VMEM((nblk, LANES), F32),
            pltpu.VMEM((2, nq, nblk, tq), F32),
            pltpu.VMEM((2, 2, blk, tq + SCORE_PITCH_PAD), F32),
            pltpu.VMEM((2, 2, 1, tq), F32),
            pltpu.VMEM((2, 2 * blk, tq), F32),
            pltpu.VMEM((2, blk, blk), F32),
            pltpu.VMEM((2, 1, tq), F32),
            pltpu.VMEM((2, V_ROWS, tq), F32),
        ],
        compiler_params=_params(2), name="moba_attention")(
            rel_table, qt3, k3, vt, bias_t)


class _Ple(NamedTuple):
    g: jax.Array
    w_gate: jax.Array
    p: _Slab
    w_up: jax.Array

    def in_specs(self, tm):
        layer = self.p.layer
        return [_resident(self.g), _resident(self.w_gate),
                pl.BlockSpec((None, tm, self.p.shape[1]), lambda i: (layer, i, 0)),
                _resident(self.w_up)]

    def operands(self):
        return tuple(_array(a) for a in self)


def _ple_update(x, g_ref, wg_ref, p_ref, wu_ref, y_ref):
    u = _rmsnorm(x, g_ref[...]).astype(BF16)
    p = p_ref[...].astype(BF16)
    for cols in _chunks(x.shape[1], PLE_COL_CHUNK):
        gate = jax.nn.sigmoid(jnp.dot(u, wg_ref[:, cols], preferred_element_type=F32))
        up = jnp.dot(p, wu_ref[:, cols], preferred_element_type=F32)
        y_ref[:, cols] = x[:, cols] + gate * up
    return y_ref[...]


def _oproj_ffn_kernel(h_ref, ot_ref, wo_ref, g_ref, win_ref, wout_ref, *rest, tf):
    d_ff = wout_ref.shape[0]
    h1 = h_ref[...] + lax.dot_general(ot_ref[...], wo_ref[...], (((0,), (0,)), ((), ())),
                                      preferred_element_type=F32)
    u = _rmsnorm(h1, g_ref[...]).astype(BF16)
    acc = h1
    for c in range(d_ff // tf):
        gate = jnp.dot(u, win_ref[:, c * tf:(c + 1) * tf], preferred_element_type=F32)
        up = jnp.dot(u, win_ref[:, d_ff + c * tf:d_ff + (c + 1) * tf],
                     preferred_element_type=F32)
        act = (gate * jax.nn.sigmoid(gate) * up).astype(BF16)
        acc = acc + jnp.dot(act, wout_ref[c * tf:(c + 1) * tf, :],
                            preferred_element_type=F32)
    *ple_refs, out_ref = rest
    if ple_refs:
        *ple_refs, final_g_ref = ple_refs
        acc = _rmsnorm(_ple_update(acc, *ple_refs, out_ref), final_g_ref[...])
    out_ref[...] = acc


def _oproj_ffn(h2, o_t, w_o, g, w_in, w_out, *, tm, tf, last=None):
    n, d = h2.shape
    d_ff = w_out.shape[0]
    assert d_ff % tf == 0 and o_t.shape == (n // tm, d, tm)
    row = pl.BlockSpec((tm, d), lambda i: (i, 0))
    weights = (w_o, g, w_in, w_out)
    in_specs = ([row, pl.BlockSpec((None, d, tm), lambda i: (i, 0, 0))]
                + [_resident(w) for w in weights])
    operands = (h2, o_t, *map(_array, weights))
    if last is not None:
        ple, final_g = last
        in_specs += ple.in_specs(tm) + [_resident(final_g)]
        operands += ple.operands() + (final_g,)
    return pl.pallas_call(
        functools.partial(_oproj_ffn_kernel, tf=tf), grid=(n // tm,),
        in_specs=in_specs, out_specs=row,
        out_shape=jax.ShapeDtypeStruct((n, d), F32),
        compiler_params=_params(1), name="oproj_ffn")(*operands)


def _ple_next_kernel(x_ref, g_ref, wg_ref, p_ref, wu_ref, *refs, n_mixer, jobs):
    mixer_refs, cast_in, out_refs, cast_out = _split_refs(refs, n_mixer, len(jobs))
    y = _ple_update(x_ref[...], g_ref, wg_ref, p_ref, wu_ref, out_refs[0])
    _mixer_inputs(y, mixer_refs, out_refs[1:])
    _cast_rows(jobs, cast_in, cast_out)


def _ple_next(h2, ple, mixer, casts, *, tm):
    n, d = h2.shape
    row = pl.BlockSpec((tm, d), lambda i: (i, 0))
    n_mixer = len(mixer.operands())
    cast_in, cast_out, cast_shapes, jobs = _cast_plans(casts, n // tm)
    h, *outs = pl.pallas_call(
        functools.partial(_ple_next_kernel, n_mixer=n_mixer, jobs=jobs),
        grid=(n // tm,),
        in_specs=[row] + ple.in_specs(tm) + mixer.in_specs() + cast_in,
        out_specs=[row] + mixer.out_specs(tm) + cast_out,
        out_shape=[jax.ShapeDtypeStruct((n, d), F32)] + mixer.out_shapes(n, tm) + cast_shapes,
        compiler_params=_params(1), name="ple_proj")(
            h2, *ple.operands(), *mixer.operands(), *(c.w.stack for c in casts))
    return h, outs[:len(outs) - len(casts)], outs[len(outs) - len(casts):]


def _row_tile(n, want):
    t = min(want, n)
    assert n % t == 0
    return t


def _col_tile(n, want):
    t = min(want, n)
    while n % t:
        t -= LANES
    return t


def kernel(x, p, attn_norm_g, fox_w_in, fox_b_f, fox_w_o, moba_w_in, moba_w_o, rel_bias_table,
           ffn_norm_g, ffn_w_in, ffn_w_out, ple_norm_g, ple_w_gate, ple_w_up, final_norm_g):
    b, s, d = x.shape
    depth = p.shape[0]
    n_heads = rel_bias_table.shape[1]
    assert d == n_heads * HEAD_DIM and n_heads % 2 == 0 and n_heads <= LANES
    n = b * s
    tm = _row_tile(n, DENSE_ROW_TILE)
    tf = _col_tile(ffn_w_out.shape[1], FFN_COL_CHUNK)

    def row_vec(v):
        return v.reshape(1, -1).astype(F32)

    w_o_mix = (fox_w_o, moba_w_o)
    p3 = p.reshape(depth, n, -1)
    fox_wt = jnp.swapaxes(fox_w_in, 1, 2)

    def qkv_cast(i):
        if i % 2 == 0:
            return _Cast(_Slab(fox_wt, i // 2), 3 * d, scaled=d)
        return _Cast(_Slab(moba_w_in, i // 2), d, scaled=d, transpose=True)

    def layer_casts(i):
        jobs = [_Cast(_Slab(w, layer), w.shape[1]) for w, layer in (
            (w_o_mix[i % 2], i // 2), (ffn_w_in, i), (ffn_w_out, i), (ple_w_gate, i),
            (ple_w_up, i))]
        return jobs + ([qkv_cast(i + 1)] if i + 1 < depth else [])

    def mixer(i, w_qkv):
        parts = [row_vec(attn_norm_g[i]), w_qkv]
        if i % 2 == 0:
            pad_heads = LANES - n_heads
            parts += [jnp.pad(fox_wt[i // 2, 3 * d:], ((0, pad_heads), (0, 0))),
                      jnp.pad(row_vec(fox_b_f[i // 2]), ((0, 0), (0, pad_heads)))]
        return _Mixer(*parts)

    rel_table = rel_bias_table.astype(F32)
    bias_t = _bias_tiles(rel_table, MOBA_BLOCK)

    h = x.reshape(n, d).astype(F32)
    row = lax.broadcasted_iota(jnp.int32, (3 * d, 1), 0)
    w_qkv = (fox_wt[0, :3 * d] * jnp.where(row < d, Q_SCALE, 1.0)).astype(BF16)
    mixed, weights = _project(h, mixer(0, w_qkv), layer_casts(0), tm=tm)
    for i in range(depth):
        w_o, w_ffn_in, w_ffn_out, w_ple_gate, w_ple_up, *w_qkv_next = weights
        qt, k3, vt = mixed[0], mixed[1].reshape(b, s, d), mixed[2]
        if i % 2 == 0:
            aqt, ak = _cumsum(mixed[3].reshape(b, s, LANES))
            o_t = _fox_attention(qt, k3, vt, aqt, ak, n_heads)
        else:
            o_t = _moba_attention(qt, k3, vt, rel_table, bias_t, n_heads)
        ple = _Ple(row_vec(ple_norm_g[i]), w_ple_gate, _Slab(p3, i), w_ple_up)
        last = (ple, row_vec(final_norm_g)) if i + 1 == depth else None
        h = _oproj_ffn(h, o_t, w_o, row_vec(ffn_norm_g[i]), w_ffn_in, w_ffn_out,
                       tm=tm, tf=tf, last=last)
        if last is None:
            h, mixed, weights = _ple_next(h, ple, mixer(i + 1, *w_qkv_next),
                                          layer_casts(i + 1), tm=tm)
    return h.reshape(b, s, d).astype(x.dtype)
```
